```python
import jax, jax.numpy as jnp
from jax import lax
import numpy as np

D_MODEL = 1024
BATCH = 8
SEQ = 4096
DEPTH = 4

N_MIXERS = 3
SHORT_CONV_W = 3
POOL_WINDOWS = (2, 4, 8, 16)
N_POOL_GROUPS = len(POOL_WINDOWS)
POOL_GROUP_W = D_MODEL // N_POOL_GROUPS
CONF_CONV_W = 31
D_FF = ((8 * D_MODEL // 3 + 255) // 256) * 256
RMS_EPS = 1e-6
LN_EPS = 1e-5

kernel_name = "hybrid_interleaved_conv_pool_conformer"


def rmsnorm(x, g):
    xf = x.astype(jnp.float32)
    y = xf * lax.rsqrt(jnp.mean(xf * xf, axis=-1, keepdims=True) + RMS_EPS)
    return (y * g.astype(jnp.float32)).astype(x.dtype)


def layernorm(x, g, b):
    xf = x.astype(jnp.float32)
    mu = jnp.mean(xf, axis=-1, keepdims=True)
    xc = xf - mu
    var = jnp.mean(xc * xc, axis=-1, keepdims=True)
    y = xc * lax.rsqrt(var + LN_EPS) * g.astype(jnp.float32) + b.astype(jnp.float32)
    return y.astype(x.dtype)


def causal_depthwise_conv(u, w):
    k, c = w.shape
    return lax.conv_general_dilated(
        u, w[:, None, :].astype(u.dtype),
        window_strides=(1,), padding=((k - 1, 0),),
        dimension_numbers=("NWC", "WIO", "NWC"),
        feature_group_count=c)


def short_conv_mixer(u, w_in, conv_w, w_out):
    b, c, v = jnp.split(u @ w_in, 3, axis=-1)
    return (b * causal_depthwise_conv(c * v, conv_w)) @ w_out


def multiscale_pool_mixer(u, w_grp, scale):
    bn, s, d = u.shape
    ug = u.reshape(bn, s, N_POOL_GROUPS, POOL_GROUP_W).astype(jnp.float32)
    cs = jnp.cumsum(ug, axis=1)
    t = jnp.arange(s)
    pooled = []
    for g, w in enumerate(POOL_WINDOWS):
        c = cs[:, :, g, :]
        lag = jnp.pad(c, ((0, 0), (w, 0), (0, 0)))[:, :s]
        cnt = jnp.minimum(t + 1, w).astype(jnp.float32)[None, :, None]
        pooled.append((c - lag) / cnt)
    mixed = (jnp.stack(pooled, axis=2) - ug).astype(u.dtype)
    y = jnp.einsum("bsgc,gcd->bsgd", mixed, w_grp)
    return y.reshape(bn, s, d) * scale


def conformer_conv_module(u, w_pw1, b_pw1, w_dw, b_dw, ln_g, ln_b, w_pw2, b_pw2):
    a, gate = jnp.split(u @ w_pw1 + b_pw1, 2, axis=-1)
    h = a * jax.nn.sigmoid(gate)
    h = causal_depthwise_conv(h, w_dw) + b_dw
    h = jax.nn.silu(layernorm(h, ln_g, ln_b))
    return h @ w_pw2 + b_pw2


def swiglu_ffn(u, w_gu, w_down):
    g, up = jnp.split(u @ w_gu, 2, axis=-1)
    return (jax.nn.silu(g) * up) @ w_down


def _fwd_setup_inputs(seed: int = 0) -> dict:
    key = jax.random.key(seed)
    keys = iter(jax.random.split(key, 64))
    d, f = D_MODEL, D_FF

    def nrm(shape, scale):
        return jax.random.normal(next(keys), shape, jnp.float32) * scale

    def gain(n):
        return 1.0 + nrm((n,), 0.05)

    def short_conv_params(i):
        return {f"a{i}_w_in": nrm((d, 3 * d), d ** -0.5),
                f"a{i}_conv": nrm((SHORT_CONV_W, d), SHORT_CONV_W ** -0.5),
                f"a{i}_w_out": nrm((d, d), d ** -0.5)}

    def ffn_params(i):
        return {f"ln2_{i}": gain(d),
                f"ffn{i}_w_gu": nrm((d, 2 * f), d ** -0.5),
                f"ffn{i}_w_down": nrm((f, d), f ** -0.5)}

    p = {"x": nrm((BATCH, SEQ, d), 1.0)}
    p["ln1_0"] = gain(d)
    p.update(short_conv_params(0))
    p.update(ffn_params(0))
    p["ln1_1"] = gain(d)
    p["b1_w_grp"] = nrm((N_POOL_GROUPS, POOL_GROUP_W, POOL_GROUP_W), POOL_GROUP_W ** -0.5)
    p["b1_scale"] = 1.0 + nrm((d,), 0.1)
    p.update(ffn_params(1))
    p["ln1_2"] = gain(d)
    p["c2_w_pw1"] = nrm((d, 2 * d), d ** -0.5)
    p["c2_b_pw1"] = nrm((2 * d,), 0.02)
    p["c2_dw"] = nrm((CONF_CONV_W, d), CONF_CONV_W ** -0.5)
    p["c2_b_dw"] = nrm((d,), 0.02)
    p["c2_ln_g"] = gain(d)
    p["c2_ln_b"] = nrm((d,), 0.02)
    p["c2_w_pw2"] = nrm((d, d), d ** -0.5)
    p["c2_b_pw2"] = nrm((d,), 0.02)
    p.update(ffn_params(2))
    p["ln1_3"] = gain(d)
    p.update(short_conv_params(3))
    p.update(ffn_params(3))
    p["ln_f"] = gain(d)
    return p


def _fwd_reference(x,
              ln1_0, a0_w_in, a0_conv, a0_w_out, ln2_0, ffn0_w_gu, ffn0_w_down,
              ln1_1, b1_w_grp, b1_scale, ln2_1, ffn1_w_gu, ffn1_w_down,
              ln1_2, c2_w_pw1, c2_b_pw1, c2_dw, c2_b_dw, c2_ln_g, c2_ln_b, c2_w_pw2, c2_b_pw2,
              ln2_2, ffn2_w_gu, ffn2_w_down,
              ln1_3, a3_w_in, a3_conv, a3_w_out, ln2_3, ffn3_w_gu, ffn3_w_down,
              ln_f):
    mixer_fns = (short_conv_mixer, multiscale_pool_mixer, conformer_conv_module)
    mixer_params = (
        (a0_w_in, a0_conv, a0_w_out),
        (b1_w_grp, b1_scale),
        (c2_w_pw1, c2_b_pw1, c2_dw, c2_b_dw, c2_ln_g, c2_ln_b, c2_w_pw2, c2_b_pw2),
        (a3_w_in, a3_conv, a3_w_out),
    )
    pre_mix_norms = (ln1_0, ln1_1, ln1_2, ln1_3)
    pre_ffn_norms = (ln2_0, ln2_1, ln2_2, ln2_3)
    ffn_params = ((ffn0_w_gu, ffn0_w_down), (ffn1_w_gu, ffn1_w_down),
                  (ffn2_w_gu, ffn2_w_down), (ffn3_w_gu, ffn3_w_down))

    h = x
    for i in range(DEPTH):
        mixer = mixer_fns[i % N_MIXERS]
        h = h + mixer(rmsnorm(h, pre_mix_norms[i]), *mixer_params[i])
        h = h + swiglu_ffn(rmsnorm(h, pre_ffn_norms[i]), *ffn_params[i])
    return rmsnorm(h, ln_f)


import jax as _jax
import jax.numpy as _jnp

TWIN_FORMAT = 'train_step'
FWD_PARAMS = ['x', 'ln1_0', 'a0_w_in', 'a0_conv', 'a0_w_out', 'ln2_0', 'ffn0_w_gu', 'ffn0_w_down', 'ln1_1', 'b1_w_grp', 'b1_scale', 'ln2_1', 'ffn1_w_gu', 'ffn1_w_down', 'ln1_2', 'c2_w_pw1', 'c2_b_pw1', 'c2_dw', 'c2_b_dw', 'c2_ln_g', 'c2_ln_b', 'c2_w_pw2', 'c2_b_pw2', 'ln2_2', 'ffn2_w_gu', 'ffn2_w_down', 'ln1_3', 'a3_w_in', 'a3_conv', 'a3_w_out', 'ln2_3', 'ffn3_w_gu', 'ffn3_w_down', 'ln_f']
TWIN_WEIGHTS = ['ln1_0', 'a0_w_in', 'a0_conv', 'a0_w_out', 'ln2_0', 'ffn0_w_gu', 'ffn0_w_down', 'ln1_1', 'b1_w_grp', 'b1_scale', 'ln2_1', 'ffn1_w_gu', 'ffn1_w_down', 'ln1_2', 'c2_w_pw1', 'c2_b_pw1', 'c2_dw', 'c2_b_dw', 'c2_ln_g', 'c2_ln_b', 'c2_w_pw2', 'c2_b_pw2', 'ln2_2', 'ffn2_w_gu', 'ffn2_w_down', 'ln1_3', 'a3_w_in', 'a3_conv', 'a3_w_out', 'ln2_3', 'ffn3_w_gu', 'ffn3_w_down', 'ln_f']
TWIN_DIFF_INPUT = 'x'
TWIN_INPUTS = ['x', 'ln1_0', 'a0_w_in', 'a0_conv', 'a0_w_out', 'ln2_0', 'ffn0_w_gu', 'ffn0_w_down', 'ln1_1', 'b1_w_grp', 'b1_scale', 'ln2_1', 'ffn1_w_gu', 'ffn1_w_down', 'ln1_2', 'c2_w_pw1', 'c2_b_pw1', 'c2_dw', 'c2_b_dw', 'c2_ln_g', 'c2_ln_b', 'c2_w_pw2', 'c2_b_pw2', 'ln2_2', 'ffn2_w_gu', 'ffn2_w_down', 'ln1_3', 'a3_w_in', 'a3_conv', 'a3_w_out', 'ln2_3', 'ffn3_w_gu', 'ffn3_w_down', 'ln_f', 'loss_target', 'm_ln1_0', 'm_a0_w_in', 'm_a0_conv', 'm_a0_w_out', 'm_ln2_0', 'm_ffn0_w_gu', 'm_ffn0_w_down', 'm_ln1_1', 'm_b1_w_grp', 'm_b1_scale', 'm_ln2_1', 'm_ffn1_w_gu', 'm_ffn1_w_down', 'm_ln1_2', 'm_c2_w_pw1', 'm_c2_b_pw1', 'm_c2_dw', 'm_c2_b_dw', 'm_c2_ln_g', 'm_c2_ln_b', 'm_c2_w_pw2', 'm_c2_b_pw2', 'm_ln2_2', 'm_ffn2_w_gu', 'm_ffn2_w_down', 'm_ln1_3', 'm_a3_w_in', 'm_a3_conv', 'm_a3_w_out', 'm_ln2_3', 'm_ffn3_w_gu', 'm_ffn3_w_down', 'm_ln_f', 'v_ln1_0', 'v_a0_w_in', 'v_a0_conv', 'v_a0_w_out', 'v_ln2_0', 'v_ffn0_w_gu', 'v_ffn0_w_down', 'v_ln1_1', 'v_b1_w_grp', 'v_b1_scale', 'v_ln2_1', 'v_ffn1_w_gu', 'v_ffn1_w_down', 'v_ln1_2', 'v_c2_w_pw1', 'v_c2_b_pw1', 'v_c2_dw', 'v_c2_b_dw', 'v_c2_ln_g', 'v_c2_ln_b', 'v_c2_w_pw2', 'v_c2_b_pw2', 'v_ln2_2', 'v_ffn2_w_gu', 'v_ffn2_w_down', 'v_ln1_3', 'v_a3_w_in', 'v_a3_conv', 'v_a3_w_out', 'v_ln2_3', 'v_ffn3_w_gu', 'v_ffn3_w_down', 'v_ln_f']
TWIN_OUTPUTS = ['loss', 'grad_x', 'grad_ln1_0', 'grad_a0_w_in', 'grad_a0_conv', 'grad_a0_w_out', 'grad_ln2_0', 'grad_ffn0_w_gu', 'grad_ffn0_w_down', 'grad_ln1_1', 'grad_b1_w_grp', 'grad_b1_scale', 'grad_ln2_1', 'grad_ffn1_w_gu', 'grad_ffn1_w_down', 'grad_ln1_2', 'grad_c2_w_pw1', 'grad_c2_b_pw1', 'grad_c2_dw', 'grad_c2_b_dw', 'grad_c2_ln_g', 'grad_c2_ln_b', 'grad_c2_w_pw2', 'grad_c2_b_pw2', 'grad_ln2_2', 'grad_ffn2_w_gu', 'grad_ffn2_w_down', 'grad_ln1_3', 'grad_a3_w_in', 'grad_a3_conv', 'grad_a3_w_out', 'grad_ln2_3', 'grad_ffn3_w_gu', 'grad_ffn3_w_down', 'grad_ln_f', 'delta_ln1_0', 'delta_a0_w_in', 'delta_a0_conv', 'delta_a0_w_out', 'delta_ln2_0', 'delta_ffn0_w_gu', 'delta_ffn0_w_down', 'delta_ln1_1', 'delta_b1_w_grp', 'delta_b1_scale', 'delta_ln2_1', 'delta_ffn1_w_gu', 'delta_ffn1_w_down', 'delta_ln1_2', 'delta_c2_w_pw1', 'delta_c2_b_pw1', 'delta_c2_dw', 'delta_c2_b_dw', 'delta_c2_ln_g', 'delta_c2_ln_b', 'delta_c2_w_pw2', 'delta_c2_b_pw2', 'delta_ln2_2', 'delta_ffn2_w_gu', 'delta_ffn2_w_down', 'delta_ln1_3', 'delta_a3_w_in', 'delta_a3_conv', 'delta_a3_w_out', 'delta_ln2_3', 'delta_ffn3_w_gu', 'delta_ffn3_w_down', 'delta_ln_f', 'new_m_ln1_0', 'new_m_a0_w_in', 'new_m_a0_conv', 'new_m_a0_w_out', 'new_m_ln2_0', 'new_m_ffn0_w_gu', 'new_m_ffn0_w_down', 'new_m_ln1_1', 'new_m_b1_w_grp', 'new_m_b1_scale', 'new_m_ln2_1', 'new_m_ffn1_w_gu', 'new_m_ffn1_w_down', 'new_m_ln1_2', 'new_m_c2_w_pw1', 'new_m_c2_b_pw1', 'new_m_c2_dw', 'new_m_c2_b_dw', 'new_m_c2_ln_g', 'new_m_c2_ln_b', 'new_m_c2_w_pw2', 'new_m_c2_b_pw2', 'new_m_ln2_2', 'new_m_ffn2_w_gu', 'new_m_ffn2_w_down', 'new_m_ln1_3', 'new_m_a3_w_in', 'new_m_a3_conv', 'new_m_a3_w_out', 'new_m_ln2_3', 'new_m_ffn3_w_gu', 'new_m_ffn3_w_down', 'new_m_ln_f', 'new_v_ln1_0', 'new_v_a0_w_in', 'new_v_a0_conv', 'new_v_a0_w_out', 'new_v_ln2_0', 'new_v_ffn0_w_gu', 'new_v_ffn0_w_down', 'new_v_ln1_1', 'new_v_b1_w_grp', 'new_v_b1_scale', 'new_v_ln2_1', 'new_v_ffn1_w_gu', 'new_v_ffn1_w_down', 'new_v_ln1_2', 'new_v_c2_w_pw1', 'new_v_c2_b_pw1', 'new_v_c2_dw', 'new_v_c2_b_dw', 'new_v_c2_ln_g', 'new_v_c2_ln_b', 'new_v_c2_w_pw2', 'new_v_c2_b_pw2', 'new_v_ln2_2', 'new_v_ffn2_w_gu', 'new_v_ffn2_w_down', 'new_v_ln1_3', 'new_v_a3_w_in', 'new_v_a3_conv', 'new_v_a3_w_out', 'new_v_ln2_3', 'new_v_ffn3_w_gu', 'new_v_ffn3_w_down', 'new_v_ln_f']
TWIN_LEAF_KINDS = {'loss': 'loss', 'grad_x': 'grad_x', 'grad_ln1_0': 'grad_w', 'grad_a0_w_in': 'grad_w', 'grad_a0_conv': 'grad_w', 'grad_a0_w_out': 'grad_w', 'grad_ln2_0': 'grad_w', 'grad_ffn0_w_gu': 'grad_w', 'grad_ffn0_w_down': 'grad_w', 'grad_ln1_1': 'grad_w', 'grad_b1_w_grp': 'grad_w', 'grad_b1_scale': 'grad_w', 'grad_ln2_1': 'grad_w', 'grad_ffn1_w_gu': 'grad_w', 'grad_ffn1_w_down': 'grad_w', 'grad_ln1_2': 'grad_w', 'grad_c2_w_pw1': 'grad_w', 'grad_c2_b_pw1': 'grad_w', 'grad_c2_dw': 'grad_w', 'grad_c2_b_dw': 'grad_w', 'grad_c2_ln_g': 'grad_w', 'grad_c2_ln_b': 'grad_w', 'grad_c2_w_pw2': 'grad_w', 'grad_c2_b_pw2': 'grad_w', 'grad_ln2_2': 'grad_w', 'grad_ffn2_w_gu': 'grad_w', 'grad_ffn2_w_down': 'grad_w', 'grad_ln1_3': 'grad_w', 'grad_a3_w_in': 'grad_w', 'grad_a3_conv': 'grad_w', 'grad_a3_w_out': 'grad_w', 'grad_ln2_3': 'grad_w', 'grad_ffn3_w_gu': 'grad_w', 'grad_ffn3_w_down': 'grad_w', 'grad_ln_f': 'grad_w', 'delta_ln1_0': 'delta_w', 'delta_a0_w_in': 'delta_w', 'delta_a0_conv': 'delta_w', 'delta_a0_w_out': 'delta_w', 'delta_ln2_0': 'delta_w', 'delta_ffn0_w_gu': 'delta_w', 'delta_ffn0_w_down': 'delta_w', 'delta_ln1_1': 'delta_w', 'delta_b1_w_grp': 'delta_w', 'delta_b1_scale': 'delta_w', 'delta_ln2_1': 'delta_w', 'delta_ffn1_w_gu': 'delta_w', 'delta_ffn1_w_down': 'delta_w', 'delta_ln1_2': 'delta_w', 'delta_c2_w_pw1': 'delta_w', 'delta_c2_b_pw1': 'delta_w', 'delta_c2_dw': 'delta_w', 'delta_c2_b_dw': 'delta_w', 'delta_c2_ln_g': 'delta_w', 'delta_c2_ln_b': 'delta_w', 'delta_c2_w_pw2': 'delta_w', 'delta_c2_b_pw2': 'delta_w', 'delta_ln2_2': 'delta_w', 'delta_ffn2_w_gu': 'delta_w', 'delta_ffn2_w_down': 'delta_w', 'delta_ln1_3': 'delta_w', 'delta_a3_w_in': 'delta_w', 'delta_a3_conv': 'delta_w', 'delta_a3_w_out': 'delta_w', 'delta_ln2_3': 'delta_w', 'delta_ffn3_w_gu': 'delta_w', 'delta_ffn3_w_down': 'delta_w', 'delta_ln_f': 'delta_w', 'new_m_ln1_0': 'new_m', 'new_m_a0_w_in': 'new_m', 'new_m_a0_conv': 'new_m', 'new_m_a0_w_out': 'new_m', 'new_m_ln2_0': 'new_m', 'new_m_ffn0_w_gu': 'new_m', 'new_m_ffn0_w_down': 'new_m', 'new_m_ln1_1': 'new_m', 'new_m_b1_w_grp': 'new_m', 'new_m_b1_scale': 'new_m', 'new_m_ln2_1': 'new_m', 'new_m_ffn1_w_gu': 'new_m', 'new_m_ffn1_w_down': 'new_m', 'new_m_ln1_2': 'new_m', 'new_m_c2_w_pw1': 'new_m', 'new_m_c2_b_pw1': 'new_m', 'new_m_c2_dw': 'new_m', 'new_m_c2_b_dw': 'new_m', 'new_m_c2_ln_g': 'new_m', 'new_m_c2_ln_b': 'new_m', 'new_m_c2_w_pw2': 'new_m', 'new_m_c2_b_pw2': 'new_m', 'new_m_ln2_2': 'new_m', 'new_m_ffn2_w_gu': 'new_m', 'new_m_ffn2_w_down': 'new_m', 'new_m_ln1_3': 'new_m', 'new_m_a3_w_in': 'new_m', 'new_m_a3_conv': 'new_m', 'new_m_a3_w_out': 'new_m', 'new_m_ln2_3': 'new_m', 'new_m_ffn3_w_gu': 'new_m', 'new_m_ffn3_w_down': 'new_m', 'new_m_ln_f': 'new_m', 'new_v_ln1_0': 'new_v', 'new_v_a0_w_in': 'new_v', 'new_v_a0_conv': 'new_v', 'new_v_a0_w_out': 'new_v', 'new_v_ln2_0': 'new_v', 'new_v_ffn0_w_gu': 'new_v', 'new_v_ffn0_w_down': 'new_v', 'new_v_ln1_1': 'new_v', 'new_v_b1_w_grp': 'new_v', 'new_v_b1_scale': 'new_v', 'new_v_ln2_1': 'new_v', 'new_v_ffn1_w_gu': 'new_v', 'new_v_ffn1_w_down': 'new_v', 'new_v_ln1_2': 'new_v', 'new_v_c2_w_pw1': 'new_v', 'new_v_c2_b_pw1': 'new_v', 'new_v_c2_dw': 'new_v', 'new_v_c2_b_dw': 'new_v', 'new_v_c2_ln_g': 'new_v', 'new_v_c2_ln_b': 'new_v', 'new_v_c2_w_pw2': 'new_v', 'new_v_c2_b_pw2': 'new_v', 'new_v_ln2_2': 'new_v', 'new_v_ffn2_w_gu': 'new_v', 'new_v_ffn2_w_down': 'new_v', 'new_v_ln1_3': 'new_v', 'new_v_a3_w_in': 'new_v', 'new_v_a3_conv': 'new_v', 'new_v_a3_w_out': 'new_v', 'new_v_ln2_3': 'new_v', 'new_v_ffn3_w_gu': 'new_v', 'new_v_ffn3_w_down': 'new_v', 'new_v_ln_f': 'new_v'}


def _forward(args):
    return _fwd_reference(*[args[k] for k in FWD_PARAMS])


def _output_shape():
    def fwd():
        inp = _fwd_setup_inputs(0)
        return _fwd_reference(*[inp[k] for k in FWD_PARAMS])
    out = _jax.eval_shape(fwd)
    return out.shape, out.dtype

N_MICROBATCH = 1
ADAM_LR = 0.001
ADAM_B1 = 0.9
ADAM_B2 = 0.999
ADAM_EPS = 1e-08
ADAM_WD = 0.01
ADAM_STEP = 10
PER_EXAMPLE_BATCH_AXIS = {'x': 0, 'loss_target': 0}
SHARED_INPUTS = []
_WEIGHT_DTYPES = {'ln1_0': _jnp.float32, 'a0_w_in': _jnp.float32, 'a0_conv': _jnp.float32, 'a0_w_out': _jnp.float32, 'ln2_0': _jnp.float32, 'ffn0_w_gu': _jnp.float32, 'ffn0_w_down': _jnp.float32, 'ln1_1': _jnp.float32, 'b1_w_grp': _jnp.float32, 'b1_scale': _jnp.float32, 'ln2_1': _jnp.float32, 'ffn1_w_gu': _jnp.float32, 'ffn1_w_down': _jnp.float32, 'ln1_2': _jnp.float32, 'c2_w_pw1': _jnp.float32, 'c2_b_pw1': _jnp.float32, 'c2_dw': _jnp.float32, 'c2_b_dw': _jnp.float32, 'c2_ln_g': _jnp.float32, 'c2_ln_b': _jnp.float32, 'c2_w_pw2': _jnp.float32, 'c2_b_pw2': _jnp.float32, 'ln2_2': _jnp.float32, 'ffn2_w_gu': _jnp.float32, 'ffn2_w_down': _jnp.float32, 'ln1_3': _jnp.float32, 'a3_w_in': _jnp.float32, 'a3_conv': _jnp.float32, 'a3_w_out': _jnp.float32, 'ln2_3': _jnp.float32, 'ffn3_w_gu': _jnp.float32, 'ffn3_w_down': _jnp.float32, 'ln_f': _jnp.float32}
MOMENT_SCALE = {'ln1_0': 3.195674e-01, 'a0_w_in': 1.831579e-01, 'a0_conv': 1.849653e-01, 'a0_w_out': 1.838954e-01, 'ln2_0': 1.437257e-01, 'ffn0_w_gu': 5.827211e-02, 'ffn0_w_down': 9.506336e-02, 'ln1_1': 1.305592e-01, 'b1_w_grp': 1.263849e-01, 'b1_scale': 3.854357e-01, 'ln2_1': 1.079313e-01, 'ffn1_w_gu': 4.531695e-02, 'ffn1_w_down': 7.403233e-02, 'ln1_2': 8.004344e-02, 'c2_w_pw1': 5.463781e-02, 'c2_b_pw1': 7.365627e-02, 'c2_dw': 7.187236e-02, 'c2_b_dw': 1.563569e-01, 'c2_ln_g': 8.671286e-02, 'c2_ln_b': 8.214776e-02, 'c2_w_pw2': 7.244729e-02, 'c2_b_pw2': 1.518841e-01, 'ln2_2': 9.881677e-02, 'ffn2_w_gu': 3.883515e-02, 'ffn2_w_down': 6.363324e-02, 'ln1_3': 1.417888e-01, 'a3_w_in': 8.129740e-02, 'a3_conv': 8.463672e-02, 'a3_w_out': 8.296133e-02, 'ln2_3': 6.543103e-02, 'ffn3_w_gu': 2.780783e-02, 'ffn3_w_down': 4.566320e-02, 'ln_f': 3.205508e+01}


def _to_microbatches(a, axis):
    t = _jnp.moveaxis(a, axis, 0)
    t = t.reshape((N_MICROBATCH, t.shape[0] // N_MICROBATCH) + t.shape[1:])
    return _jnp.moveaxis(t, 1, axis + 1)


def setup_inputs(seed: int = 0) -> dict:
    inp = _fwd_setup_inputs(seed)
    key = _jax.random.fold_in(_jax.random.key(seed), 7919)
    shape, _ = _output_shape()
    out = dict(inp)
    out["loss_target"] = _jax.random.normal(_jax.random.fold_in(key, 0), shape, _jnp.float32)
    for i, name in enumerate(TWIN_WEIGHTS):
        w = inp[name].astype(_jnp.float32)
        if MOMENT_SCALE is None:
            s = _jnp.sqrt(_jnp.mean(_jnp.square(w)) + 1e-30)
        else:
            s = MOMENT_SCALE[name]
        km, kv = _jax.random.split(_jax.random.fold_in(key, i + 1))
        out[name] = w
        out["m_" + name] = s * _jax.random.normal(km, w.shape, _jnp.float32)
        out["v_" + name] = (s * s) * _jax.random.uniform(kv, w.shape, _jnp.float32, 0.5, 1.5)
    if N_MICROBATCH > 1:
        for name, axis in PER_EXAMPLE_BATCH_AXIS.items():
            out[name] = _to_microbatches(out[name], axis)
    return {'x': out['x'], 'ln1_0': out['ln1_0'], 'a0_w_in': out['a0_w_in'], 'a0_conv': out['a0_conv'], 'a0_w_out': out['a0_w_out'], 'ln2_0': out['ln2_0'], 'ffn0_w_gu': out['ffn0_w_gu'], 'ffn0_w_down': out['ffn0_w_down'], 'ln1_1': out['ln1_1'], 'b1_w_grp': out['b1_w_grp'], 'b1_scale': out['b1_scale'], 'ln2_1': out['ln2_1'], 'ffn1_w_gu': out['ffn1_w_gu'], 'ffn1_w_down': out['ffn1_w_down'], 'ln1_2': out['ln1_2'], 'c2_w_pw1': out['c2_w_pw1'], 'c2_b_pw1': out['c2_b_pw1'], 'c2_dw': out['c2_dw'], 'c2_b_dw': out['c2_b_dw'], 'c2_ln_g': out['c2_ln_g'], 'c2_ln_b': out['c2_ln_b'], 'c2_w_pw2': out['c2_w_pw2'], 'c2_b_pw2': out['c2_b_pw2'], 'ln2_2': out['ln2_2'], 'ffn2_w_gu': out['ffn2_w_gu'], 'ffn2_w_down': out['ffn2_w_down'], 'ln1_3': out['ln1_3'], 'a3_w_in': out['a3_w_in'], 'a3_conv': out['a3_conv'], 'a3_w_out': out['a3_w_out'], 'ln2_3': out['ln2_3'], 'ffn3_w_gu': out['ffn3_w_gu'], 'ffn3_w_down': out['ffn3_w_down'], 'ln_f': out['ln_f'], 'loss_target': out['loss_target'], 'm_ln1_0': out['m_ln1_0'], 'm_a0_w_in': out['m_a0_w_in'], 'm_a0_conv': out['m_a0_conv'], 'm_a0_w_out': out['m_a0_w_out'], 'm_ln2_0': out['m_ln2_0'], 'm_ffn0_w_gu': out['m_ffn0_w_gu'], 'm_ffn0_w_down': out['m_ffn0_w_down'], 'm_ln1_1': out['m_ln1_1'], 'm_b1_w_grp': out['m_b1_w_grp'], 'm_b1_scale': out['m_b1_scale'], 'm_ln2_1': out['m_ln2_1'], 'm_ffn1_w_gu': out['m_ffn1_w_gu'], 'm_ffn1_w_down': out['m_ffn1_w_down'], 'm_ln1_2': out['m_ln1_2'], 'm_c2_w_pw1': out['m_c2_w_pw1'], 'm_c2_b_pw1': out['m_c2_b_pw1'], 'm_c2_dw': out['m_c2_dw'], 'm_c2_b_dw': out['m_c2_b_dw'], 'm_c2_ln_g': out['m_c2_ln_g'], 'm_c2_ln_b': out['m_c2_ln_b'], 'm_c2_w_pw2': out['m_c2_w_pw2'], 'm_c2_b_pw2': out['m_c2_b_pw2'], 'm_ln2_2': out['m_ln2_2'], 'm_ffn2_w_gu': out['m_ffn2_w_gu'], 'm_ffn2_w_down': out['m_ffn2_w_down'], 'm_ln1_3': out['m_ln1_3'], 'm_a3_w_in': out['m_a3_w_in'], 'm_a3_conv': out['m_a3_conv'], 'm_a3_w_out': out['m_a3_w_out'], 'm_ln2_3': out['m_ln2_3'], 'm_ffn3_w_gu': out['m_ffn3_w_gu'], 'm_ffn3_w_down': out['m_ffn3_w_down'], 'm_ln_f': out['m_ln_f'], 'v_ln1_0': out['v_ln1_0'], 'v_a0_w_in': out['v_a0_w_in'], 'v_a0_conv': out['v_a0_conv'], 'v_a0_w_out': out['v_a0_w_out'], 'v_ln2_0': out['v_ln2_0'], 'v_ffn0_w_gu': out['v_ffn0_w_gu'], 'v_ffn0_w_down': out['v_ffn0_w_down'], 'v_ln1_1': out['v_ln1_1'], 'v_b1_w_grp': out['v_b1_w_grp'], 'v_b1_scale': out['v_b1_scale'], 'v_ln2_1': out['v_ln2_1'], 'v_ffn1_w_gu': out['v_ffn1_w_gu'], 'v_ffn1_w_down': out['v_ffn1_w_down'], 'v_ln1_2': out['v_ln1_2'], 'v_c2_w_pw1': out['v_c2_w_pw1'], 'v_c2_b_pw1': out['v_c2_b_pw1'], 'v_c2_dw': out['v_c2_dw'], 'v_c2_b_dw': out['v_c2_b_dw'], 'v_c2_ln_g': out['v_c2_ln_g'], 'v_c2_ln_b': out['v_c2_ln_b'], 'v_c2_w_pw2': out['v_c2_w_pw2'], 'v_c2_b_pw2': out['v_c2_b_pw2'], 'v_ln2_2': out['v_ln2_2'], 'v_ffn2_w_gu': out['v_ffn2_w_gu'], 'v_ffn2_w_down': out['v_ffn2_w_down'], 'v_ln1_3': out['v_ln1_3'], 'v_a3_w_in': out['v_a3_w_in'], 'v_a3_conv': out['v_a3_conv'], 'v_a3_w_out': out['v_a3_w_out'], 'v_ln2_3': out['v_ln2_3'], 'v_ffn3_w_gu': out['v_ffn3_w_gu'], 'v_ffn3_w_down': out['v_ffn3_w_down'], 'v_ln_f': out['v_ln_f']}


def _loss(weights, diff, rest, loss_target):
    with _jax.named_scope("forward"):
        args = {**rest, TWIN_DIFF_INPUT: diff, **{k: w.astype(_WEIGHT_DTYPES[k]) for k, w in weights.items()}}
        y = _forward(args)
    with _jax.named_scope("loss_head"):
        err = _jnp.square(y.astype(_jnp.float32) - loss_target)
        return 0.5 * _jnp.sum(_jnp.mean(err, axis=-1)) if err.ndim else 0.5 * err


def _adamw(w, g, m, v):
    m = ADAM_B1 * m + (1.0 - ADAM_B1) * g
    v = ADAM_B2 * v + (1.0 - ADAM_B2) * _jnp.square(g)
    m_hat = m / (1.0 - ADAM_B1 ** ADAM_STEP)
    v_hat = v / (1.0 - ADAM_B2 ** ADAM_STEP)
    delta = -ADAM_LR * (m_hat / (_jnp.sqrt(v_hat) + ADAM_EPS) + ADAM_WD * w)
    return delta, m, v


def reference(x, ln1_0, a0_w_in, a0_conv, a0_w_out, ln2_0, ffn0_w_gu, ffn0_w_down, ln1_1, b1_w_grp, b1_scale, ln2_1, ffn1_w_gu, ffn1_w_down, ln1_2, c2_w_pw1, c2_b_pw1, c2_dw, c2_b_dw, c2_ln_g, c2_ln_b, c2_w_pw2, c2_b_pw2, ln2_2, ffn2_w_gu, ffn2_w_down, ln1_3, a3_w_in, a3_conv, a3_w_out, ln2_3, ffn3_w_gu, ffn3_w_down, ln_f, loss_target, m_ln1_0, m_a0_w_in, m_a0_conv, m_a0_w_out, m_ln2_0, m_ffn0_w_gu, m_ffn0_w_down, m_ln1_1, m_b1_w_grp, m_b1_scale, m_ln2_1, m_ffn1_w_gu, m_ffn1_w_down, m_ln1_2, m_c2_w_pw1, m_c2_b_pw1, m_c2_dw, m_c2_b_dw, m_c2_ln_g, m_c2_ln_b, m_c2_w_pw2, m_c2_b_pw2, m_ln2_2, m_ffn2_w_gu, m_ffn2_w_down, m_ln1_3, m_a3_w_in, m_a3_conv, m_a3_w_out, m_ln2_3, m_ffn3_w_gu, m_ffn3_w_down, m_ln_f, v_ln1_0, v_a0_w_in, v_a0_conv, v_a0_w_out, v_ln2_0, v_ffn0_w_gu, v_ffn0_w_down, v_ln1_1, v_b1_w_grp, v_b1_scale, v_ln2_1, v_ffn1_w_gu, v_ffn1_w_down, v_ln1_2, v_c2_w_pw1, v_c2_b_pw1, v_c2_dw, v_c2_b_dw, v_c2_ln_g, v_c2_ln_b, v_c2_w_pw2, v_c2_b_pw2, v_ln2_2, v_ffn2_w_gu, v_ffn2_w_down, v_ln1_3, v_a3_w_in, v_a3_conv, v_a3_w_out, v_ln2_3, v_ffn3_w_gu, v_ffn3_w_down, v_ln_f):
    given = dict(x=x, ln1_0=ln1_0, a0_w_in=a0_w_in, a0_conv=a0_conv, a0_w_out=a0_w_out, ln2_0=ln2_0, ffn0_w_gu=ffn0_w_gu, ffn0_w_down=ffn0_w_down, ln1_1=ln1_1, b1_w_grp=b1_w_grp, b1_scale=b1_scale, ln2_1=ln2_1, ffn1_w_gu=ffn1_w_gu, ffn1_w_down=ffn1_w_down, ln1_2=ln1_2, c2_w_pw1=c2_w_pw1, c2_b_pw1=c2_b_pw1, c2_dw=c2_dw, c2_b_dw=c2_b_dw, c2_ln_g=c2_ln_g, c2_ln_b=c2_ln_b, c2_w_pw2=c2_w_pw2, c2_b_pw2=c2_b_pw2, ln2_2=ln2_2, ffn2_w_gu=ffn2_w_gu, ffn2_w_down=ffn2_w_down, ln1_3=ln1_3, a3_w_in=a3_w_in, a3_conv=a3_conv, a3_w_out=a3_w_out, ln2_3=ln2_3, ffn3_w_gu=ffn3_w_gu, ffn3_w_down=ffn3_w_down, ln_f=ln_f, loss_target=loss_target, m_ln1_0=m_ln1_0, m_a0_w_in=m_a0_w_in, m_a0_conv=m_a0_conv, m_a0_w_out=m_a0_w_out, m_ln2_0=m_ln2_0, m_ffn0_w_gu=m_ffn0_w_gu, m_ffn0_w_down=m_ffn0_w_down, m_ln1_1=m_ln1_1, m_b1_w_grp=m_b1_w_grp, m_b1_scale=m_b1_scale, m_ln2_1=m_ln2_1, m_ffn1_w_gu=m_ffn1_w_gu, m_ffn1_w_down=m_ffn1_w_down, m_ln1_2=m_ln1_2, m_c2_w_pw1=m_c2_w_pw1, m_c2_b_pw1=m_c2_b_pw1, m_c2_dw=m_c2_dw, m_c2_b_dw=m_c2_b_dw, m_c2_ln_g=m_c2_ln_g, m_c2_ln_b=m_c2_ln_b, m_c2_w_pw2=m_c2_w_pw2, m_c2_b_pw2=m_c2_b_pw2, m_ln2_2=m_ln2_2, m_ffn2_w_gu=m_ffn2_w_gu, m_ffn2_w_down=m_ffn2_w_down, m_ln1_3=m_ln1_3, m_a3_w_in=m_a3_w_in, m_a3_conv=m_a3_conv, m_a3_w_out=m_a3_w_out, m_ln2_3=m_ln2_3, m_ffn3_w_gu=m_ffn3_w_gu, m_ffn3_w_down=m_ffn3_w_down, m_ln_f=m_ln_f, v_ln1_0=v_ln1_0, v_a0_w_in=v_a0_w_in, v_a0_conv=v_a0_conv, v_a0_w_out=v_a0_w_out, v_ln2_0=v_ln2_0, v_ffn0_w_gu=v_ffn0_w_gu, v_ffn0_w_down=v_ffn0_w_down, v_ln1_1=v_ln1_1, v_b1_w_grp=v_b1_w_grp, v_b1_scale=v_b1_scale, v_ln2_1=v_ln2_1, v_ffn1_w_gu=v_ffn1_w_gu, v_ffn1_w_down=v_ffn1_w_down, v_ln1_2=v_ln1_2, v_c2_w_pw1=v_c2_w_pw1, v_c2_b_pw1=v_c2_b_pw1, v_c2_dw=v_c2_dw, v_c2_b_dw=v_c2_b_dw, v_c2_ln_g=v_c2_ln_g, v_c2_ln_b=v_c2_ln_b, v_c2_w_pw2=v_c2_w_pw2, v_c2_b_pw2=v_c2_b_pw2, v_ln2_2=v_ln2_2, v_ffn2_w_gu=v_ffn2_w_gu, v_ffn2_w_down=v_ffn2_w_down, v_ln1_3=v_ln1_3, v_a3_w_in=v_a3_w_in, v_a3_conv=v_a3_conv, v_a3_w_out=v_a3_w_out, v_ln2_3=v_ln2_3, v_ffn3_w_gu=v_ffn3_w_gu, v_ffn3_w_down=v_ffn3_w_down, v_ln_f=v_ln_f)
    weights = {n: given[n] for n in TWIN_WEIGHTS}
    shared = {n: given[n] for n in SHARED_INPUTS}
    per_example = {n: given[n] for n in ['x']}
    grad_fn = _jax.value_and_grad(_loss, argnums=(0, 1))

    def one_microbatch(ex, loss_target):
        ex = dict(ex)
        diff = ex.pop(TWIN_DIFF_INPUT)
        return grad_fn(weights, diff, {**shared, **ex}, loss_target)

    if N_MICROBATCH == 1:
        loss, (grad_w, grad_x) = one_microbatch(per_example, given["loss_target"])
    else:
        def body(carry, xs):
            loss_sum, grad_sum = carry
            l_k, (gw_k, gx_k) = one_microbatch(xs[0], xs[1])
            with _jax.named_scope("update"):
                return (loss_sum + l_k, _jax.tree.map(_jnp.add, grad_sum, gw_k)), gx_k

        init = (_jnp.zeros((), _jnp.float32), _jax.tree.map(_jnp.zeros_like, weights))
        (loss, grad_w), grad_x = _jax.lax.scan(body, init, (per_example, given["loss_target"]))
    with _jax.named_scope("update"):
        delta_w, new_m, new_v = {}, {}, {}
        for n in TWIN_WEIGHTS:
            delta_w[n], new_m[n], new_v[n] = _adamw(weights[n], grad_w[n], given["m_" + n], given["v_" + n])
    return (loss, grad_x, *[grad_w[n] for n in TWIN_WEIGHTS], *[delta_w[n] for n in TWIN_WEIGHTS],
            *[new_m[n] for n in TWIN_WEIGHTS], *[new_v[n] for n in TWIN_WEIGHTS])
```

```python
import functools

import jax
import jax.numpy as jnp
from jax import lax
from jax.experimental import pallas as pl
from jax.experimental.pallas import tpu as pltpu

F32 = jnp.float32
BF16 = jnp.bfloat16

RMS_EPS = 1e-6
LN_EPS = 1e-5
POOL_WINDOWS = (2, 4, 8, 16)
SHORT_CONV_W = 3
CONF_CONV_W = 31
N_CHIPS = 4
N_DEV = 8

ADAM_LR = 0.001
ADAM_B1 = 0.9
ADAM_B2 = 0.999
ADAM_EPS = 1e-08
ADAM_WD = 0.01
ADAM_STEP = 10

V7X_VMEM_BYTES = 64 * 1024 * 1024
VMEM_LIMIT = V7X_VMEM_BYTES - 8 * 1024 * 1024
LANES = 128
POOL_HALO = 16
SCONV_HALO = 16
CONF_HALO = 32


def _params(*sem):
    return pltpu.CompilerParams(dimension_semantics=sem, vmem_limit_bytes=VMEM_LIMIT)


def _tile(n, pref, mult=8):
    t = min(n, pref)
    while t > mult and (n % t or t % mult):
        t -= mult
    assert n % t == 0 and t % mult == 0, (n, pref, mult)
    return t


def _sigmoid(x):
    return jax.nn.sigmoid(x)


def _dot(a, b):
    return jnp.dot(a, b, preferred_element_type=F32)


def _dot_nt(a, b):
    return lax.dot_general(a, b, (((1,), (1,)), ((), ())), preferred_element_type=F32)


def _dot_tn(a, b):
    return lax.dot_general(a, b, (((0,), (0,)), ((), ())), preferred_element_type=F32)


def _colsum(x):
    return jnp.sum(x, axis=0, keepdims=True)


def _rms_stats(x):
    return lax.rsqrt(jnp.mean(x * x, axis=-1, keepdims=True) + RMS_EPS)


def _rms_bwd(du, x, gain):
    r = _rms_stats(x)
    xhat = x * r
    gdy = du * gain
    dx = r * (gdy - xhat * jnp.mean(gdy * xhat, axis=-1, keepdims=True))
    return dx, _colsum(du * xhat)


def _rms_fwd(h, gain, name):
    s, d = h.shape
    tm = _tile(s, 512)

    def body(h_ref, g_ref, u_ref):
        x = h_ref[...]
        u_ref[...] = (x * _rms_stats(x) * g_ref[...]).astype(u_ref.dtype)

    return pl.pallas_call(
        body, name=name, grid=(s // tm,),
        in_specs=[pl.BlockSpec((tm, d), lambda m: (m, 0)), pl.BlockSpec((1, d), lambda m: (0, 0))],
        out_specs=pl.BlockSpec((tm, d), lambda m: (m, 0)),
        out_shape=jax.ShapeDtypeStruct((s, d), BF16),
        compiler_params=_params("parallel"),
    )(h, gain)


def _mm_col(a, w, bias, name):
    s, k = a.shape
    nsh, _, ns = w.shape
    tm = _tile(s, 512)
    has_bias = bias is not None

    def body(a_ref, w_ref, *rest):
        o_ref = rest[-1]
        acc = _dot(a_ref[...], w_ref[...])
        if has_bias:
            acc = acc + rest[0][...]
        o_ref[...] = acc.astype(o_ref.dtype)

    in_specs = [pl.BlockSpec((tm, k), lambda j, m: (m, 0)), pl.BlockSpec((None, k, ns), lambda j, m: (j, 0, 0))]
    args = [a, w]
    if has_bias:
        in_specs.append(pl.BlockSpec((1, ns), lambda j, m: (0, j)))
        args.append(bias)
    return pl.pallas_call(
        body, name=name, grid=(nsh, s // tm), in_specs=in_specs,
        out_specs=pl.BlockSpec((tm, ns), lambda j, m: (m, j)),
        out_shape=jax.ShapeDtypeStruct((s, nsh * ns), BF16),
        compiler_params=_params("parallel", "parallel"),
    )(*args)


def _mm_row(a, w, res, bias, swiglu, name):
    s = a.shape[0]
    k, n = w.shape
    tm = _tile(s, 256 if swiglu else 512)
    has_bias = bias is not None

    def body(*refs):
        o_ref = refs[-1]
        if swiglu:
            g_ref, up_ref, w_ref, res_ref = refs[:4]
            g = g_ref[...].astype(F32)
            act = (g * _sigmoid(g) * up_ref[...].astype(F32)).astype(BF16)
            rest = refs[4:-1]
        else:
            a_ref, w_ref, res_ref = refs[:3]
            act = a_ref[...]
            rest = refs[3:-1]
        y = res_ref[...] + _dot(act, w_ref[...])
        if has_bias:
            y = y + rest[0][...]
        o_ref[...] = y

    if swiglu:
        in_specs = [pl.BlockSpec((tm, k), lambda m: (m, 0)), pl.BlockSpec((tm, k), lambda m: (m, 1))]
        args = [a, a]
    else:
        in_specs = [pl.BlockSpec((tm, k), lambda m: (m, 0))]
        args = [a]
    in_specs += [pl.BlockSpec((k, n), lambda m: (0, 0)), pl.BlockSpec((tm, n), lambda m: (m, 0))]
    args += [w, res]
    if has_bias:
        in_specs.append(pl.BlockSpec((1, n), lambda m: (0, 0)))
        args.append(bias)
    return pl.pallas_call(
        body, name=name, grid=(s // tm,), in_specs=in_specs,
        out_specs=pl.BlockSpec((tm, n), lambda m: (m, 0)),
        out_shape=jax.ShapeDtypeStruct((s, n), F32),
        compiler_params=_params("parallel"),
    )(*args)


def _mm_nt_row(dy, w, name):
    s, n = dy.shape
    k = w.shape[0]
    tm = _tile(s, 512)

    def body(dy_ref, w_ref, o_ref):
        o_ref[...] = _dot_nt(dy_ref[...].astype(BF16), w_ref[...])

    return pl.pallas_call(
        body, name=name, grid=(s // tm,),
        in_specs=[pl.BlockSpec((tm, n), lambda m: (m, 0)), pl.BlockSpec((k, n), lambda m: (0, 0))],
        out_specs=pl.BlockSpec((tm, k), lambda m: (m, 0)),
        out_shape=jax.ShapeDtypeStruct((s, k), F32),
        compiler_params=_params("parallel"),
    )(dy, w)


def _ffn_down_bwd(dh, w, gu, name):
    s, d = dh.shape
    f = w.shape[0]
    tm = _tile(s, 256)

    def body(dh_ref, w_ref, gu_ref, o_ref):
        da = _dot_nt(dh_ref[...].astype(BF16), w_ref[...])
        g = gu_ref[:, :f].astype(F32)
        up = gu_ref[:, f:].astype(F32)
        sg = _sigmoid(g)
        o_ref[:, :f] = (da * up * sg * (1.0 + g * (1.0 - sg))).astype(o_ref.dtype)
        o_ref[:, f:] = (da * g * sg).astype(o_ref.dtype)

    return pl.pallas_call(
        body, name=name, grid=(s // tm,),
        in_specs=[pl.BlockSpec((tm, d), lambda m: (m, 0)), pl.BlockSpec((f, d), lambda m: (0, 0)),
                  pl.BlockSpec((tm, 2 * f), lambda m: (m, 0))],
        out_specs=pl.BlockSpec((tm, 2 * f), lambda m: (m, 0)),
        out_shape=jax.ShapeDtypeStruct((s, 2 * f), BF16),
        compiler_params=_params("parallel"),
    )(dh, w, gu)


def _mm_nt_col_rms_bwd(dy, w, h, gain, dh, name):
    s = dy.shape[0]
    nsh, k, ns = w.shape
    tm = _tile(s, 512)
    nm = s // tm

    def body(dy_ref, w_ref, h_ref, g_ref, dh_ref, o_ref, dg_ref, acc_ref):
        m, j = pl.program_id(0), pl.program_id(1)
        part = _dot_nt(dy_ref[...], w_ref[...])

        @pl.when(j == 0)
        def _():
            acc_ref[...] = part

        @pl.when(j > 0)
        def _():
            acc_ref[...] += part

        @pl.when(j == nsh - 1)
        def _():
            dx, dg = _rms_bwd(acc_ref[...], h_ref[...], g_ref[...])
            o_ref[...] = dh_ref[...] + dx

            @pl.when(m == 0)
            def _():
                dg_ref[...] = dg

            @pl.when(m > 0)
            def _():
                dg_ref[...] += dg

    return pl.pallas_call(
        body, name=name, grid=(nm, nsh),
        in_specs=[pl.BlockSpec((tm, ns), lambda m, j: (m, j)), pl.BlockSpec((None, k, ns), lambda m, j: (j, 0, 0)),
                  pl.BlockSpec((tm, k), lambda m, j: (m, 0)), pl.BlockSpec((1, k), lambda m, j: (0, 0)),
                  pl.BlockSpec((tm, k), lambda m, j: (m, 0))],
        out_specs=[pl.BlockSpec((tm, k), lambda m, j: (m, 0)), pl.BlockSpec((1, k), lambda m, j: (0, 0))],
        out_shape=[jax.ShapeDtypeStruct((s, k), F32), jax.ShapeDtypeStruct((1, k), F32)],
        scratch_shapes=[pltpu.VMEM((tm, k), F32)],
        compiler_params=_params("arbitrary", "arbitrary"),
    )(dy, w, h, gain, dh)


def _mm_tn(a, dy, nsh, swiglu, name, tm_pref=1024):
    s = a.shape[0]
    k = a.shape[1] // 2 if swiglu else a.shape[1]
    ns = dy.shape[1] // nsh
    tm = _tile(s, tm_pref)
    tk = _tile(k, 1408, LANES)
    nk, nm = k // tk, s // tm

    def body(*refs):
        o_ref, acc_ref = refs[-2:]
        if swiglu:
            g = refs[0][...].astype(F32)
            act = (g * _sigmoid(g) * refs[1][...].astype(F32)).astype(BF16)
            dy_ref = refs[2]
        else:
            act = refs[0][...]
            dy_ref = refs[1]
        m = pl.program_id(2)
        part = _dot_tn(act, dy_ref[...].astype(BF16))

        @pl.when(m == 0)
        def _():
            acc_ref[...] = part

        @pl.when(m > 0)
        def _():
            acc_ref[...] += part

        @pl.when(m == nm - 1)
        def _():
            o_ref[...] = acc_ref[...].astype(o_ref.dtype)

    if swiglu:
        in_specs = [pl.BlockSpec((tm, tk), lambda j, kk, m: (m, kk)), pl.BlockSpec((tm, tk), lambda j, kk, m: (m, kk + nk))]
        args = [a, a]
    else:
        in_specs = [pl.BlockSpec((tm, tk), lambda j, kk, m: (m, kk))]
        args = [a]
    in_specs.append(pl.BlockSpec((tm, ns), lambda j, kk, m: (m, j)))
    args.append(dy)
    return pl.pallas_call(
        body, name=name, grid=(nsh, nk, nm), in_specs=in_specs,
        out_specs=pl.BlockSpec((None, tk, ns), lambda j, kk, m: (j, kk, 0)),
        out_shape=jax.ShapeDtypeStruct((nsh, k, ns), BF16),
        scratch_shapes=[pltpu.VMEM((tk, ns), F32)],
        compiler_params=_params("parallel", "parallel", "arbitrary"),
    )(*args)


def _main_spec(tm, w):
    return pl.BlockSpec((tm, w), lambda m: (m, 0))


def _before_spec(tm, hb, w):
    return pl.BlockSpec((hb, w), lambda m: (jnp.maximum(m * (tm // hb) - 1, 0), 0))


def _after_spec(tm, hb, w, s):
    return pl.BlockSpec((hb, w), lambda m: (jnp.minimum((m + 1) * (tm // hb), s // hb - 1), 0))


def _row_spec(w, rows=1):
    return pl.BlockSpec((rows, w), lambda m: (0, 0))


def _accumulate(ref, val, first):
    @pl.when(first)
    def _():
        ref[...] = val

    @pl.when(jnp.logical_not(first))
    def _():
        ref[...] += val


def _sconv_taps(zext_ref, cw_ref, tm, base):
    out = cw_ref[2:3, :] * zext_ref[pl.ds(base, tm), :]
    out = out + cw_ref[1:2, :] * zext_ref[pl.ds(base - 1, tm), :]
    return out + cw_ref[0:1, :] * zext_ref[pl.ds(base - 2, tm), :]


def _sconv_fill_z(zext_ref, main_ref, before_ref, d, m):
    hb = SCONV_HALO
    zb = before_ref[:, d:2 * d].astype(F32) * before_ref[:, 2 * d:].astype(F32)
    zext_ref[pl.ds(0, hb), :] = jnp.where(m > 0, zb, 0.0)
    zext_ref[pl.ds(hb, main_ref.shape[0]), :] = main_ref[:, d:2 * d].astype(F32) * main_ref[:, 2 * d:].astype(F32)


def _sconv_fwd(bcv, cw, name):
    s, d3 = bcv.shape
    d = d3 // 3
    tm = _tile(s, 512, SCONV_HALO)

    def body(main_ref, before_ref, cw_ref, p_ref, zext_ref):
        m = pl.program_id(0)
        _sconv_fill_z(zext_ref, main_ref, before_ref, d, m)
        zc = _sconv_taps(zext_ref, cw_ref, tm, SCONV_HALO)
        p_ref[...] = (main_ref[:, :d].astype(F32) * zc).astype(p_ref.dtype)

    return pl.pallas_call(
        body, name=name, grid=(s // tm,),
        in_specs=[_main_spec(tm, d3), _before_spec(tm, SCONV_HALO, d3), _row_spec(d, SHORT_CONV_W)],
        out_specs=_main_spec(tm, d),
        out_shape=jax.ShapeDtypeStruct((s, d), BF16),
        scratch_shapes=[pltpu.VMEM((tm + SCONV_HALO, d), F32)],
        compiler_params=_params("parallel"),
    )(bcv, bcv, cw)


def _sconv_bwd(dp, bcv, cw, name):
    s, d3 = bcv.shape
    d = d3 // 3
    tm = _tile(s, 512, SCONV_HALO)
    nm = s // tm
    ha = 8

    def body(dp_ref, dpa_ref, main_ref, before_ref, after_ref, cw_ref, o_ref, dcw_ref, zext_ref, dext_ref):
        m = pl.program_id(0)
        _sconv_fill_z(zext_ref, main_ref, before_ref, d, m)
        zc = _sconv_taps(zext_ref, cw_ref, tm, SCONV_HALO)
        dp_t = dp_ref[...]
        o_ref[:, :d] = (dp_t * zc).astype(o_ref.dtype)
        dzc = dp_t * main_ref[:, :d].astype(F32)
        dext_ref[pl.ds(0, tm), :] = dzc
        dza = dpa_ref[...] * after_ref[:, :d].astype(F32)[0:ha]
        dext_ref[pl.ds(tm, ha), :] = jnp.where(m < nm - 1, dza, 0.0)
        dz = cw_ref[2:3, :] * dzc
        dz = dz + cw_ref[1:2, :] * dext_ref[pl.ds(1, tm), :]
        dz = dz + cw_ref[0:1, :] * dext_ref[pl.ds(2, tm), :]
        o_ref[:, d:2 * d] = (dz * main_ref[:, 2 * d:].astype(F32)).astype(o_ref.dtype)
        o_ref[:, 2 * d:] = (dz * main_ref[:, d:2 * d].astype(F32)).astype(o_ref.dtype)

        @pl.when(m == 0)
        def _():
            dcw_ref[...] = jnp.zeros_like(dcw_ref)

        for kk in range(SHORT_CONV_W):
            zs = zext_ref[pl.ds(SCONV_HALO - 2 + kk, tm), :]
            dcw_ref[kk:kk + 1, :] += _colsum(dzc * zs)

    return pl.pallas_call(
        body, name=name, grid=(nm,),
        in_specs=[_main_spec(tm, d), _after_spec(tm, ha, d, s), _main_spec(tm, d3), _before_spec(tm, SCONV_HALO, d3),
                  _after_spec(tm, SCONV_HALO, d3, s), _row_spec(d, SHORT_CONV_W)],
        out_specs=[_main_spec(tm, d3), _row_spec(d, 8)],
        out_shape=[jax.ShapeDtypeStruct((s, d3), BF16), jax.ShapeDtypeStruct((8, d), F32)],
        scratch_shapes=[pltpu.VMEM((tm + SCONV_HALO, d), F32), pltpu.VMEM((tm + ha, d), F32)],
        compiler_params=_params("arbitrary"),
    )(dp, dp, bcv, bcv, bcv, cw)


def _pool_counts(t0, tm, w):
    t = t0 + lax.broadcasted_iota(jnp.int32, (tm, 1), 0)
    return jnp.minimum(t + 1, w).astype(F32)


def _pool_fwd(h, gain, wg, scale, name):
    s, d = h.shape
    ng, cg, _ = wg.shape
    tm = _tile(s, 512, POOL_HALO)

    def body(h_ref, hb_ref, g_ref, wg_ref, sc_ref, o_ref, mx_ref, uext_ref):
        m = pl.program_id(0)
        x = h_ref[...]
        gain_row = g_ref[...]
        xb = hb_ref[...]
        uext_ref[pl.ds(0, POOL_HALO), :] = jnp.where(m > 0, xb * _rms_stats(xb) * gain_row, 0.0)
        uext_ref[pl.ds(POOL_HALO, tm), :] = x * _rms_stats(x) * gain_row
        for gi, win in enumerate(POOL_WINDOWS):
            cols = pl.ds(gi * cg, cg)
            u_g = uext_ref[pl.ds(POOL_HALO, tm), cols]
            acc = u_g
            for i in range(1, win):
                acc = acc + uext_ref[pl.ds(POOL_HALO - i, tm), cols]
            mixed = (acc / _pool_counts(m * tm, tm, win) - u_g).astype(BF16)
            mx_ref[:, cols] = mixed
            o_ref[:, cols] = x[:, gi * cg:(gi + 1) * cg] + _dot(mixed, wg_ref[gi]) * sc_ref[:, cols]

    return pl.pallas_call(
        body, name=name, grid=(s // tm,),
        in_specs=[_main_spec(tm, d), _before_spec(tm, POOL_HALO, d), _row_spec(d),
                  pl.BlockSpec((ng, cg, cg), lambda m: (0, 0, 0)), _row_spec(d)],
        out_specs=[_main_spec(tm, d), _main_spec(tm, d)],
        out_shape=[jax.ShapeDtypeStruct((s, d), F32), jax.ShapeDtypeStruct((s, d), BF16)],
        scratch_shapes=[pltpu.VMEM((tm + POOL_HALO, d), F32)],
        compiler_params=_params("parallel"),
    )(h, h, gain, wg, scale)


def _pool_bwd_mm(dh, mixed, wg, scale, name):
    s, d = dh.shape
    ng, cg, _ = wg.shape
    tm = _tile(s, 512)

    def body(dh_ref, mx_ref, wg_ref, sc_ref, dmx_ref, dwg_ref, dsc_ref):
        first = pl.program_id(0) == 0
        for gi in range(ng):
            cols = pl.ds(gi * cg, cg)
            dh_g = dh_ref[:, cols]
            mixed = mx_ref[:, cols]
            w_g = wg_ref[gi]
            dy = (dh_g * sc_ref[:, cols]).astype(BF16)
            dmx_ref[:, cols] = _dot_nt(dy, w_g)
            _accumulate(dsc_ref.at[:, cols], _colsum(dh_g * _dot(mixed, w_g)), first)
            _accumulate(dwg_ref.at[gi], _dot_tn(mixed, dy), first)

    return pl.pallas_call(
        body, name=name, grid=(s // tm,),
        in_specs=[_main_spec(tm, d), _main_spec(tm, d), pl.BlockSpec((ng, cg, cg), lambda m: (0, 0, 0)), _row_spec(d)],
        out_specs=[_main_spec(tm, d), pl.BlockSpec((ng, cg, cg), lambda m: (0, 0, 0)), _row_spec(d)],
        out_shape=[jax.ShapeDtypeStruct((s, d), F32), jax.ShapeDtypeStruct((ng, cg, cg), F32),
                   jax.ShapeDtypeStruct((1, d), F32)],
        compiler_params=_params("arbitrary"),
    )(dh, mixed, wg, scale)


def _pool_bwd_rms(dmixed, h, gain, dh, name):
    s, d = h.shape
    cg = d // len(POOL_WINDOWS)
    tm = _tile(s, 512, POOL_HALO)
    nm = s // tm

    def body(dmx_ref, dmxa_ref, h_ref, g_ref, dh_ref, o_ref, dg_ref, eext_ref, du_ref):
        m = pl.program_id(0)
        for gi, win in enumerate(POOL_WINDOWS):
            cols = pl.ds(gi * cg, cg)
            dmx = dmx_ref[:, cols]
            eext_ref[pl.ds(0, tm), cols] = dmx / _pool_counts(m * tm, tm, win)
            ea = dmxa_ref[:, cols] / _pool_counts((m + 1) * tm, POOL_HALO, win)
            eext_ref[pl.ds(tm, POOL_HALO), cols] = jnp.where(m < nm - 1, ea, 0.0)
            acc = -dmx
            for i in range(win):
                acc = acc + eext_ref[pl.ds(i, tm), cols]
            du_ref[:, cols] = acc
        dx, dg = _rms_bwd(du_ref[...], h_ref[...], g_ref[...])
        o_ref[...] = dh_ref[...] + dx
        _accumulate(dg_ref, dg, m == 0)

    return pl.pallas_call(
        body, name=name, grid=(nm,),
        in_specs=[_main_spec(tm, d), _after_spec(tm, POOL_HALO, d, s), _main_spec(tm, d), _row_spec(d), _main_spec(tm, d)],
        out_specs=[_main_spec(tm, d), _row_spec(d)],
        out_shape=[jax.ShapeDtypeStruct((s, d), F32), jax.ShapeDtypeStruct((1, d), F32)],
        scratch_shapes=[pltpu.VMEM((tm + POOL_HALO, d), F32), pltpu.VMEM((tm, d), F32)],
        compiler_params=_params("arbitrary"),
    )(dmixed, dmixed, h, gain, dh)


def _conf_fill_h(hext_ref, main_ref, before_ref, d, m):
    hb = before_ref[:, :d].astype(F32) * _sigmoid(before_ref[:, d:].astype(F32))
    hext_ref[pl.ds(0, CONF_HALO), :] = jnp.where(m > 0, hb, 0.0)
    hext_ref[pl.ds(CONF_HALO, main_ref.shape[0]), :] = main_ref[:, :d].astype(F32) * _sigmoid(main_ref[:, d:].astype(F32))


def _layernorm_parts(hc, g, b):
    mu = jnp.mean(hc, axis=-1, keepdims=True)
    xc = hc - mu
    rs = lax.rsqrt(jnp.mean(xc * xc, axis=-1, keepdims=True) + LN_EPS)
    xhat = xc * rs
    return xhat, rs, xhat * g + b


def _conf_mid_fwd(ag, dw, b_dw, ln_g, ln_b, name):
    s, d2 = ag.shape
    d = d2 // 2
    tm = _tile(s, 256, CONF_HALO)
    base = CONF_HALO - (CONF_CONV_W - 1)

    def body(main_ref, before_ref, dw_ref, bdw_ref, g_ref, b_ref, s_ref, hc_ref, hext_ref):
        m = pl.program_id(0)
        _conf_fill_h(hext_ref, main_ref, before_ref, d, m)
        hc = bdw_ref[...] + dw_ref[0:1, :] * hext_ref[pl.ds(base, tm), :]
        for kk in range(1, CONF_CONV_W):
            hc = hc + dw_ref[kk:kk + 1, :] * hext_ref[pl.ds(base + kk, tm), :]
        hc_ref[...] = hc
        _, _, l = _layernorm_parts(hc, g_ref[...], b_ref[...])
        s_ref[...] = (l * _sigmoid(l)).astype(s_ref.dtype)

    return pl.pallas_call(
        body, name=name, grid=(s // tm,),
        in_specs=[_main_spec(tm, d2), _before_spec(tm, CONF_HALO, d2), _row_spec(d, CONF_CONV_W), _row_spec(d),
                  _row_spec(d), _row_spec(d)],
        out_specs=[_main_spec(tm, d), _main_spec(tm, d)],
        out_shape=[jax.ShapeDtypeStruct((s, d), BF16), jax.ShapeDtypeStruct((s, d), F32)],
        scratch_shapes=[pltpu.VMEM((tm + CONF_HALO, d), F32)],
        compiler_params=_params("parallel"),
    )(ag, ag, dw, b_dw, ln_g, ln_b)


def _conf_out_bwd(dh, w, hc, ln_g, ln_b, name):
    s, d = dh.shape
    tm = _tile(s, 256)

    def body(dh_ref, w_ref, hc_ref, g_ref, b_ref, o_ref, dg_ref, db_ref, dbo_ref):
        first = pl.program_id(0) == 0
        dh_t = dh_ref[...]
        ds = _dot_nt(dh_t.astype(BF16), w_ref[...])
        xhat, rs, l = _layernorm_parts(hc_ref[...], g_ref[...], b_ref[...])
        sg = _sigmoid(l)
        dl = ds * sg * (1.0 + l * (1.0 - sg))
        dxh = dl * g_ref[...]
        o_ref[...] = rs * (dxh - jnp.mean(dxh, axis=-1, keepdims=True)
                           - xhat * jnp.mean(dxh * xhat, axis=-1, keepdims=True))
        _accumulate(dg_ref, _colsum(dl * xhat), first)
        _accumulate(db_ref, _colsum(dl), first)
        _accumulate(dbo_ref, _colsum(dh_t), first)

    return pl.pallas_call(
        body, name=name, grid=(s // tm,),
        in_specs=[_main_spec(tm, d), pl.BlockSpec((d, d), lambda m: (0, 0)), _main_spec(tm, d), _row_spec(d), _row_spec(d)],
        out_specs=[_main_spec(tm, d), _row_spec(d), _row_spec(d), _row_spec(d)],
        out_shape=[jax.ShapeDtypeStruct((s, d), F32)] + [jax.ShapeDtypeStruct((1, d), F32)] * 3,
        compiler_params=_params("arbitrary"),
    )(dh, w, hc, ln_g, ln_b)


def _conf_mid_bwd(dhc, ag, dw, name):
    s, d2 = ag.shape
    d = d2 // 2
    tm = _tile(s, 256, CONF_HALO)
    nm = s // tm
    kw = CONF_CONV_W
    base = CONF_HALO - (kw - 1)

    def body(dhc_ref, dhca_ref, main_ref, before_ref, dw_ref, o_ref, ddw_ref, dbdw_ref, dbpw_ref, hext_ref, dext_ref):
        m = pl.program_id(0)
        first = m == 0
        _conf_fill_h(hext_ref, main_ref, before_ref, d, m)
        dhc_t = dhc_ref[...]
        dext_ref[pl.ds(0, tm), :] = dhc_t
        dext_ref[pl.ds(tm, CONF_HALO), :] = jnp.where(m < nm - 1, dhca_ref[...], 0.0)
        dhh = dw_ref[kw - 1:kw, :] * dhc_t
        for kk in range(kw - 1):
            dhh = dhh + dw_ref[kk:kk + 1, :] * dext_ref[pl.ds(kw - 1 - kk, tm), :]

        @pl.when(first)
        def _():
            ddw_ref[...] = jnp.zeros_like(ddw_ref)

        for kk in range(kw):
            ddw_ref[kk:kk + 1, :] += _colsum(dhc_t * hext_ref[pl.ds(base + kk, tm), :])
        a = main_ref[:, :d].astype(F32)
        sg = _sigmoid(main_ref[:, d:].astype(F32))
        da = dhh * sg
        dgate = dhh * a * sg * (1.0 - sg)
        o_ref[:, :d] = da.astype(o_ref.dtype)
        o_ref[:, d:] = dgate.astype(o_ref.dtype)
        _accumulate(dbdw_ref, _colsum(dhc_t), first)
        _accumulate(dbpw_ref.at[:, pl.ds(0, d)], _colsum(da), first)
        _accumulate(dbpw_ref.at[:, pl.ds(d, d)], _colsum(dgate), first)

    return pl.pallas_call(
        body, name=name, grid=(nm,),
        in_specs=[_main_spec(tm, d), _after_spec(tm, CONF_HALO, d, s), _main_spec(tm, d2), _before_spec(tm, CONF_HALO, d2),
                  _row_spec(d, kw)],
        out_specs=[_main_spec(tm, d2), _row_spec(d, 32), _row_spec(d), _row_spec(d2)],
        out_shape=[jax.ShapeDtypeStruct((s, d2), BF16), jax.ShapeDtypeStruct((32, d), F32),
                   jax.ShapeDtypeStruct((1, d), F32), jax.ShapeDtypeStruct((1, d2), F32)],
        scratch_shapes=[pltpu.VMEM((tm + CONF_HALO, d), F32), pltpu.VMEM((tm + CONF_HALO, d), F32)],
        compiler_params=_params("arbitrary"),
    )(dhc, dhc, ag, ag, dw)


def _loss_head(h, gain, target, name):
    s, d = h.shape
    tm = _tile(s, 512)

    def body(h_ref, g_ref, t_ref, loss_ref, dh_ref, dg_ref):
        first = pl.program_id(0) == 0
        x = h_ref[...]
        err = x * _rms_stats(x) * g_ref[...] - t_ref[...]
        part = 0.5 * jnp.sum(jnp.mean(err * err, axis=-1, keepdims=True), axis=0, keepdims=True)
        dx, dg = _rms_bwd(err * (1.0 / d), x, g_ref[...])
        dh_ref[...] = dx
        _accumulate(loss_ref, part, first)
        _accumulate(dg_ref, dg, first)

    return pl.pallas_call(
        body, name=name, grid=(s // tm,),
        in_specs=[_main_spec(tm, d), _row_spec(d), _main_spec(tm, d)],
        out_specs=[pl.BlockSpec((1, 1), lambda m: (0, 0)), _main_spec(tm, d), _row_spec(d)],
        out_shape=[jax.ShapeDtypeStruct((1, 1), F32), jax.ShapeDtypeStruct((s, d), F32), jax.ShapeDtypeStruct((1, d), F32)],
        compiler_params=_params("arbitrary"),
    )(h, gain, target)


def _ffn_fwd(h, gain, w_gu, w_down, i):
    u = _rms_fwd(h, gain, f"ffn{i}_rms")
    gu = _mm_col(u, w_gu, None, f"ffn{i}_up")
    h_new = _mm_row(gu, w_down, h, None, True, f"ffn{i}_down")
    return h_new, (h, u, gu)


def _ffn_bwd(dh, saved, gain, w_gu, w_down, i):
    h, u, gu = saved
    dgu = _ffn_down_bwd(dh, w_down, gu, f"ffn{i}_down_bwd")
    dw_down = _mm_tn(gu, dh, 1, True, f"ffn{i}_dw_down", tm_pref=512)
    dw_gu = _mm_tn(u, dgu, N_CHIPS, False, f"ffn{i}_dw_gu")
    dh_new, dgain = _mm_nt_col_rms_bwd(dgu, w_gu, h, gain, dh, f"ffn{i}_up_bwd")
    return dh_new, dgain, dw_gu, dw_down


def _device_step(x, target, wts):
    g = {}
    saved = {}
    h = x

    def short_conv_fwd(h, i):
        u = _rms_fwd(h, wts[f"ln1_{i}"], f"a{i}_rms")
        bcv = _mm_col(u, wts[f"a{i}_w_in"], None, f"a{i}_in")
        p = _sconv_fwd(bcv, wts[f"a{i}_conv"], f"a{i}_conv")
        return _mm_row(p, wts[f"a{i}_w_out"], h, None, False, f"a{i}_out"), (h, u, bcv, p)

    def short_conv_bwd(dh, sv, i):
        h, u, bcv, p = sv
        dp = _mm_nt_row(dh, wts[f"a{i}_w_out"], f"a{i}_out_bwd")
        g[f"a{i}_w_out"] = _mm_tn(p, dh, 1, False, f"a{i}_dw_out")
        dbcv, dcw = _sconv_bwd(dp, bcv, wts[f"a{i}_conv"], f"a{i}_conv_bwd")
        g[f"a{i}_conv"] = dcw[:SHORT_CONV_W]
        g[f"a{i}_w_in"] = _mm_tn(u, dbcv, N_CHIPS, False, f"a{i}_dw_in")
        dh, g[f"ln1_{i}"] = _mm_nt_col_rms_bwd(dbcv, wts[f"a{i}_w_in"], h, wts[f"ln1_{i}"], dh, f"a{i}_in_bwd")
        return dh

    h, saved["a0"] = short_conv_fwd(h, 0)
    h, saved["f0"] = _ffn_fwd(h, wts["ln2_0"], wts["ffn0_w_gu"], wts["ffn0_w_down"], 0)

    h_in = h
    h, mixed = _pool_fwd(h, wts["ln1_1"], wts["b1_w_grp"], wts["b1_scale"], "b1_fwd")
    saved["b1"] = (h_in, mixed)
    h, saved["f1"] = _ffn_fwd(h, wts["ln2_1"], wts["ffn1_w_gu"], wts["ffn1_w_down"], 1)

    h_in = h
    u = _rms_fwd(h, wts["ln1_2"], "c2_rms")
    ag = _mm_col(u, wts["c2_w_pw1"], wts["c2_b_pw1"], "c2_pw1")
    sw, hc = _conf_mid_fwd(ag, wts["c2_dw"], wts["c2_b_dw"], wts["c2_ln_g"], wts["c2_ln_b"], "c2_mid")
    h = _mm_row(sw, wts["c2_w_pw2"], h, wts["c2_b_pw2"], False, "c2_pw2")
    saved["c2"] = (h_in, u, ag, sw, hc)
    h, saved["f2"] = _ffn_fwd(h, wts["ln2_2"], wts["ffn2_w_gu"], wts["ffn2_w_down"], 2)

    h, saved["a3"] = short_conv_fwd(h, 3)
    h, saved["f3"] = _ffn_fwd(h, wts["ln2_3"], wts["ffn3_w_gu"], wts["ffn3_w_down"], 3)

    loss, dh, g["ln_f"] = _loss_head(h, wts["ln_f"], target, "loss_head")

    def ffn_bwd(dh, i):
        dh, g[f"ln2_{i}"], g[f"ffn{i}_w_gu"], g[f"ffn{i}_w_down"] = _ffn_bwd(
            dh, saved[f"f{i}"], wts[f"ln2_{i}"], wts[f"ffn{i}_w_gu"], wts[f"ffn{i}_w_down"], i)
        return dh

    dh = ffn_bwd(dh, 3)
    dh = short_conv_bwd(dh, saved["a3"], 3)

    dh = ffn_bwd(dh, 2)
    h_in, u, ag, sw, hc = saved["c2"]
    dhc, g["c2_ln_g"], g["c2_ln_b"], g["c2_b_pw2"] = _conf_out_bwd(
        dh, wts["c2_w_pw2"], hc, wts["c2_ln_g"], wts["c2_ln_b"], "c2_pw2_bwd")
    g["c2_w_pw2"] = _mm_tn(sw, dh, 1, False, "c2_dw_pw2")
    dag, ddw, g["c2_b_dw"], g["c2_b_pw1"] = _conf_mid_bwd(dhc, ag, wts["c2_dw"], "c2_mid_bwd")
    g["c2_dw"] = ddw[:CONF_CONV_W]
    g["c2_w_pw1"] = _mm_tn(u, dag, N_CHIPS, False, "c2_dw_pw1")
    dh, g["ln1_2"] = _mm_nt_col_rms_bwd(dag, wts["c2_w_pw1"], h_in, wts["ln1_2"], dh, "c2_pw1_bwd")

    dh = ffn_bwd(dh, 1)
    h_in, mixed = saved["b1"]
    dmixed, g["b1_w_grp"], g["b1_scale"] = _pool_bwd_mm(dh, mixed, wts["b1_w_grp"], wts["b1_scale"], "b1_bwd_mm")
    dh, g["ln1_1"] = _pool_bwd_rms(dmixed, h_in, wts["ln1_1"], dh, "b1_bwd_rms")

    dh = ffn_bwd(dh, 0)
    dh = short_conv_bwd(dh, saved["a0"], 0)
    return loss, dh, g


MESH = pl.DeviceIdType.MESH
ANY = pl.BlockSpec(memory_space=pl.ANY)


def _position():
    return lax.axis_index("x"), lax.axis_index("y"), lax.axis_index("c")


def _other_chips(x, y):
    return [(1 - x, y), (x, 1 - y), (1 - x, 1 - y)]


def _remote(src, dst, send_sem, recv_sem, to):
    return pltpu.make_async_remote_copy(src_ref=src, dst_ref=dst, send_sem=send_sem, recv_sem=recv_sem,
                                        device_id=to, device_id_type=MESH)


def _half_rows(ref_rows, c):
    hr = ref_rows // 2
    return pl.ds(pl.multiple_of(c * hr, 16), hr)


def _allgather8(v, name):
    m_per, n = v.shape

    def body(v_ref, out_ref, send_sems, recv_sems, local_sem):
        x, y, c = _position()
        me, sibling = (x, y, c), (x, y, 1 - c)
        chips = _other_chips(x, y)

        def rows(px, py, pc):
            return out_ref.at[pl.ds((4 * px + 2 * py + pc) * m_per, m_per), :]

        def copy(k, block, to, src=None):
            return _remote(rows(*block) if src is None else src, rows(*block), send_sems.at[k], recv_sems.at[k], to)

        mine = pltpu.make_async_copy(v_ref, rows(*me), local_sem)
        mine.start()
        first = [copy(0, me, sibling, src=v_ref)]
        first += [copy(1 + j, me, (*chip, c), src=v_ref) for j, chip in enumerate(chips)]
        for cp in first:
            cp.start()
        passed = [copy(4 + j, (*chip, c), sibling) for j, chip in enumerate(chips)]
        for j, chip in enumerate(chips):
            copy(1 + j, (*chip, c), me).wait_recv()
            passed[j].start()
        copy(0, sibling, me).wait_recv()
        for j, chip in enumerate(chips):
            copy(4 + j, (*chip, 1 - c), me).wait_recv()
        for cp in first + passed:
            cp.wait_send()
        mine.wait()

    return pl.pallas_call(
        body, name=name,
        out_shape=jax.ShapeDtypeStruct((N_DEV * m_per, n), v.dtype),
        in_specs=[pl.BlockSpec(memory_space=pltpu.VMEM)],
        out_specs=pl.BlockSpec(memory_space=pltpu.VMEM),
        scratch_shapes=[pltpu.SemaphoreType.DMA((7,)), pltpu.SemaphoreType.DMA((7,)), pltpu.SemaphoreType.DMA],
        compiler_params=pltpu.CompilerParams(vmem_limit_bytes=VMEM_LIMIT),
    )(v)


def _gather_weights(shards, name):
    n = len(shards)

    def body(*refs):
        src, out = refs[:n], refs[n:2 * n]
        ici_send, ici_recv, d2d_send, d2d_recv, local_sems = refs[2 * n:]
        x, y, c = _position()
        sibling = (x, y, 1 - c)
        chips = _other_chips(x, y)
        my_chip = 2 * x + y

        def ici(i, r, wait_only=False):
            px, py = chips[r]
            rows = _half_rows(src[i].shape[0], c)
            slot = (2 * px + py) if wait_only else my_chip
            return _remote(src[i].at[rows, :], out[i].at[slot, rows, :], ici_send.at[i, r], ici_recv.at[i, r], (px, py, c))

        def d2d(i, r, landed):
            px, py = chips[r]
            rows = _half_rows(src[i].shape[0], (1 - c) if landed else c)
            part = out[i].at[2 * px + py, rows, :]
            return _remote(part, part, d2d_send.at[i, r], d2d_recv.at[i, r], sibling)

        local = [pltpu.make_async_copy(src[i], out[i].at[my_chip], local_sems.at[i]) for i in range(n)]
        for cp in local:
            cp.start()
        for i in range(n):
            for r in range(3):
                ici(i, r).start()
        for i in range(n):
            for r in range(3):
                ici(i, r, wait_only=True).wait_recv()
                d2d(i, r, False).start()
        for i in range(n):
            for r in range(3):
                d2d(i, r, True).wait_recv()
        for i in range(n):
            for r in range(3):
                ici(i, r).wait_send()
                d2d(i, r, False).wait_send()
        for cp in local:
            cp.wait()

    return pl.pallas_call(
        body, name=name,
        out_shape=[jax.ShapeDtypeStruct((N_CHIPS,) + s.shape, s.dtype) for s in shards],
        in_specs=[ANY] * n, out_specs=[ANY] * n,
        scratch_shapes=[pltpu.SemaphoreType.DMA((n, 3))] * 4 + [pltpu.SemaphoreType.DMA((n,))],
    )(*shards)


def _send_sibling_halves(grads, name):
    n = len(grads)

    def body(*refs):
        src, out = refs[:n], refs[n:2 * n]
        send_sems, recv_sems = refs[2 * n:]
        x, y, c = _position()
        copies = [_remote(src[i].at[:, _half_rows(src[i].shape[1], 1 - c), :], out[i], send_sems.at[i], recv_sems.at[i],
                          (x, y, 1 - c)) for i in range(n)]
        for cp in copies:
            cp.start()
        for cp in copies:
            cp.wait()

    return pl.pallas_call(
        body, name=name,
        out_shape=[jax.ShapeDtypeStruct((g.shape[0], g.shape[1] // 2, g.shape[2]), g.dtype) for g in grads],
        in_specs=[ANY] * n, out_specs=[ANY] * n,
        scratch_shapes=[pltpu.SemaphoreType.DMA((n,))] * 2,
    )(*grads)


def _add_halves(grad, sib, c, name):
    nsh, r, cols = grad.shape
    hr = r // 2
    tr = _tile(hr, 512, 16)
    nt = hr // tr

    def body(c_ref, g_ref, s_ref, o_ref):
        o_ref[...] = (g_ref[...].astype(F32) + s_ref[...].astype(F32)).astype(o_ref.dtype)

    return pl.pallas_call(
        body, name=name,
        grid_spec=pltpu.PrefetchScalarGridSpec(
            num_scalar_prefetch=1, grid=(nsh, nt),
            in_specs=[pl.BlockSpec((None, tr, cols), lambda j, t, c_ref: (j, c_ref[0] * nt + t, 0)),
                      pl.BlockSpec((None, tr, cols), lambda j, t, c_ref: (j, t, 0))],
            out_specs=pl.BlockSpec((None, tr, cols), lambda j, t, c_ref: (j, t, 0))),
        out_shape=jax.ShapeDtypeStruct((nsh, hr, cols), BF16),
        compiler_params=_params("parallel", "parallel"),
    )(c, grad, sib)


def _exchange_chip_sums(parts, name):
    n = len(parts)

    def body(*refs):
        src, out = refs[:n], refs[n:2 * n]
        send_sems, recv_sems, local_sems = refs[2 * n:]
        x, y, c = _position()
        chips = _other_chips(x, y)
        my_chip = 2 * x + y

        def ici(i, r, wait_only=False):
            px, py = chips[r]
            peer_chip = 2 * px + py
            slot = peer_chip if wait_only else my_chip
            return _remote(src[i].at[peer_chip], out[i].at[slot], send_sems.at[i, r], recv_sems.at[i, r], (px, py, c))

        local = [pltpu.make_async_copy(src[i].at[my_chip], out[i].at[my_chip], local_sems.at[i]) for i in range(n)]
        for cp in local:
            cp.start()
        for i in range(n):
            for r in range(3):
                ici(i, r).start()
        for i in range(n):
            for r in range(3):
                ici(i, r, wait_only=True).wait_recv()
        for i in range(n):
            for r in range(3):
                ici(i, r).wait_send()
        for cp in local:
            cp.wait()

    return pl.pallas_call(
        body, name=name,
        out_shape=[jax.ShapeDtypeStruct(p.shape, p.dtype) for p in parts],
        in_specs=[ANY] * n, out_specs=[ANY] * n,
        scratch_shapes=[pltpu.SemaphoreType.DMA((n, 3))] * 2 + [pltpu.SemaphoreType.DMA((n,))],
    )(*parts)


def _sum_chips(parts, name):
    nsh, hr, cols = parts.shape
    tr = _tile(hr, 512, 16)

    def body(p_ref, o_ref):
        acc = p_ref[0].astype(F32)
        for k in range(1, nsh):
            acc = acc + p_ref[k].astype(F32)
        o_ref[...] = acc

    return pl.pallas_call(
        body, name=name, grid=(hr // tr,),
        in_specs=[pl.BlockSpec((nsh, tr, cols), lambda t: (0, t, 0))],
        out_specs=pl.BlockSpec((tr, cols), lambda t: (t, 0)),
        out_shape=jax.ShapeDtypeStruct((hr, cols), F32),
        compiler_params=_params("parallel"),
    )(parts)


def _exchange_halves(halves, name):
    n = len(halves)

    def body(*refs):
        src, out = refs[:n], refs[n:2 * n]
        send_sems, recv_sems, local_sems = refs[2 * n:]
        x, y, c = _position()
        local, remote = [], []
        for i in range(n):
            mine = out[i].at[_half_rows(out[i].shape[0], c), :]
            local.append(pltpu.make_async_copy(src[i], mine, local_sems.at[i]))
            remote.append(_remote(src[i], mine, send_sems.at[i], recv_sems.at[i], (x, y, 1 - c)))
        for cp in local + remote:
            cp.start()
        for i in range(n):
            theirs = out[i].at[_half_rows(out[i].shape[0], 1 - c), :]
            _remote(src[i], theirs, send_sems.at[i], recv_sems.at[i], (x, y, 1 - c)).wait_recv()
        for cp in remote:
            cp.wait_send()
        for cp in local:
            cp.wait()

    return pl.pallas_call(
        body, name=name,
        out_shape=[jax.ShapeDtypeStruct((2 * h.shape[0], h.shape[1]), h.dtype) for h in halves],
        in_specs=[ANY] * n, out_specs=[ANY] * n,
        scratch_shapes=[pltpu.SemaphoreType.DMA((n,))] * 3,
    )(*halves)


def _sum_devices(blocks, name):
    m8, n = blocks.shape
    m = m8 // N_DEV

    def body(b_ref, o_ref):
        acc = b_ref[pl.ds(0, m), :]
        for k in range(1, N_DEV):
            acc = acc + b_ref[pl.ds(k * m, m), :]
        o_ref[...] = acc

    return pl.pallas_call(
        body, name=name, out_shape=jax.ShapeDtypeStruct((m, n), F32),
        in_specs=[pl.BlockSpec(memory_space=pltpu.VMEM)], out_specs=pl.BlockSpec(memory_space=pltpu.VMEM),
        compiler_params=pltpu.CompilerParams(vmem_limit_bytes=VMEM_LIMIT),
    )(blocks)


def _adamw(w, g, m, v, name):
    r, cols = w.shape
    tr = _tile(r, 256) if r % 8 == 0 else r
    c1 = 1.0 / (1.0 - ADAM_B1 ** ADAM_STEP)
    c2 = 1.0 / (1.0 - ADAM_B2 ** ADAM_STEP)

    def body(w_ref, g_ref, m_ref, v_ref, go_ref, d_ref, mo_ref, vo_ref):
        grad = g_ref[...]
        new_m = ADAM_B1 * m_ref[...] + (1.0 - ADAM_B1) * grad
        new_v = ADAM_B2 * v_ref[...] + (1.0 - ADAM_B2) * (grad * grad)
        go_ref[...] = grad
        mo_ref[...] = new_m
        vo_ref[...] = new_v
        d_ref[...] = -ADAM_LR * ((new_m * c1) / (jnp.sqrt(new_v * c2) + ADAM_EPS) + ADAM_WD * w_ref[...])

    spec = pl.BlockSpec((tr, cols), lambda t: (t, 0))
    return pl.pallas_call(
        body, name=name, grid=(r // tr,), in_specs=[spec] * 4, out_specs=[spec] * 4,
        out_shape=[jax.ShapeDtypeStruct((r, cols), F32)] * 4,
        compiler_params=_params("parallel"),
    )(w, g, m, v)


WEIGHT_NAMES = (
    "ln1_0", "a0_w_in", "a0_conv", "a0_w_out", "ln2_0", "ffn0_w_gu", "ffn0_w_down",
    "ln1_1", "b1_w_grp", "b1_scale", "ln2_1", "ffn1_w_gu", "ffn1_w_down",
    "ln1_2", "c2_w_pw1", "c2_b_pw1", "c2_dw", "c2_b_dw", "c2_ln_g", "c2_ln_b", "c2_w_pw2", "c2_b_pw2",
    "ln2_2", "ffn2_w_gu", "ffn2_w_down",
    "ln1_3", "a3_w_in", "a3_conv", "a3_w_out", "ln2_3", "ffn3_w_gu", "ffn3_w_down", "ln_f")
BIG = ("a0_w_in", "a0_w_out", "ffn0_w_gu", "ffn0_w_down", "ffn1_w_gu", "ffn1_w_down", "c2_w_pw1", "c2_w_pw2",
       "ffn2_w_gu", "ffn2_w_down", "a3_w_in", "a3_w_out", "ffn3_w_gu", "ffn3_w_down")
SMALL_SHARDED = ("a0_conv", "a3_conv", "c2_dw", "b1_w_grp")
REPLICATED = tuple(n for n in WEIGHT_NAMES if n not in BIG and n not in SMALL_SHARDED)


def _pad_rows(a, mult=8):
    pad = -a.shape[0] % mult
    return a if pad == 0 else jnp.concatenate([a, jnp.zeros((pad, a.shape[1]), a.dtype)], axis=0)


def _pack_rows(parts, width):
    rows = [p.reshape(-1, width) for p in parts]
    return _pad_rows(jnp.concatenate(rows, axis=0)), [r.shape[0] for r in rows]


def _unpack_rows(packed, counts, shapes):
    out, at = [], 0
    for n, shp in zip(counts, shapes):
        out.append(packed[at:at + n].reshape(shp))
        at += n
    return out


def kernel(x, *rest):
    nw = len(WEIGHT_NAMES)
    w = dict(zip(WEIGHT_NAMES, rest[:nw]))
    target = rest[nw]
    mom = dict(zip(WEIGHT_NAMES, rest[nw + 1:2 * nw + 1]))
    vel = dict(zip(WEIGHT_NAMES, rest[2 * nw + 1:3 * nw + 1]))
    cx, cy, cc = _position()
    my_chip = 2 * cx + cy
    d = x.shape[-1]
    cq = d // N_CHIPS

    gathered = dict(zip(BIG, _gather_weights([w[n].astype(BF16) for n in BIG], "gather_weights")))
    small_blk, small_counts = _pack_rows([w[n] for n in SMALL_SHARDED], cq)
    small_all = _allgather8(small_blk, "gather_small").reshape(N_CHIPS, 2, small_blk.shape[0], cq)[:, 0]
    small_parts = _unpack_rows(jnp.transpose(small_all, (1, 0, 2)), small_counts,
                               [(w[n].reshape(-1, cq).shape[0], N_CHIPS, cq) for n in SMALL_SHARDED])
    wts = {n: w[n].reshape(1, -1) for n in REPLICATED}
    for n in BIG:
        gw = gathered[n]
        wts[n] = gw if n.endswith(("w_in", "w_gu", "w_pw1")) else gw.reshape(-1, gw.shape[-1])
    for n, part in zip(SMALL_SHARDED, small_parts):
        if n == "b1_w_grp":
            ng, rq, cg = w[n].shape
            full = jnp.transpose(part.reshape(ng, rq, N_CHIPS, cg), (0, 2, 1, 3)).reshape(ng, N_CHIPS * rq, cg)
            wts[n] = full.astype(BF16)
        else:
            wts[n] = part.reshape(part.shape[0], d)

    loss, dx, g = _device_step(x[0], target[0], wts)

    big_g = [g[n].reshape(N_CHIPS, -1, g[n].shape[-1]) for n in BIG]
    sib = _send_sibling_halves(big_g, "reduce_sibling")
    c_arr = cc.reshape(1).astype(jnp.int32)
    parts = [_add_halves(gg, ss, c_arr, f"reduce_add_{n}") for n, gg, ss in zip(BIG, big_g, sib)]
    landed = _exchange_chip_sums(parts, "reduce_chips")
    halves = [_sum_chips(p, f"reduce_sum_{n}") for n, p in zip(BIG, landed)]
    reduced = dict(zip(BIG, _exchange_halves(halves, "reduce_halves")))

    small_names = REPLICATED + SMALL_SHARDED
    flat = []
    for n in small_names:
        gn = g[n]
        if n == "b1_w_grp":
            gn = gn.reshape(-1, gn.shape[-1])
        flat.append(gn.astype(F32))
    sm_blk, sm_counts = _pack_rows(flat, cq)
    sm_sum = _sum_devices(_allgather8(sm_blk, "gather_small_grads"), "sum_small_grads")
    sm = dict(zip(small_names, _unpack_rows(sm_sum, sm_counts, [f.shape for f in flat])))

    out = {}
    for n in BIG:
        out[n] = _adamw(w[n], reduced[n], mom[n], vel[n], f"adamw_{n}")
    rep_w, rep_counts = _pack_rows([w[n] for n in REPLICATED], d)
    rep_g, _ = _pack_rows([sm[n] for n in REPLICATED], d)
    rep_m, _ = _pack_rows([mom[n] for n in REPLICATED], d)
    rep_v, _ = _pack_rows([vel[n] for n in REPLICATED], d)
    rep_out = _adamw(rep_w, rep_g, rep_m, rep_v, "adamw_replicated")
    rep_split = [_unpack_rows(o, rep_counts, [w[n].shape for n in REPLICATED]) for o in rep_out]
    for i, n in enumerate(REPLICATED):
        out[n] = tuple(rs[i] for rs in rep_split)
    shard_g = []
    for n in SMALL_SHARDED:
        full = sm[n]
        if n == "b1_w_grp":
            ng, rq, cg = w[n].shape
            full = full.reshape(ng, N_CHIPS, rq, cg)
            mine = lax.dynamic_index_in_dim(full, my_chip, axis=1, keepdims=False)
        else:
            full = full.reshape(full.shape[0], N_CHIPS, cq)
            mine = lax.dynamic_index_in_dim(full, my_chip, axis=1, keepdims=False)
        shard_g.append(mine)
    sh_w, sh_counts = _pack_rows([w[n] for n in SMALL_SHARDED], cq)
    sh_g, _ = _pack_rows(shard_g, cq)
    sh_m, _ = _pack_rows([mom[n] for n in SMALL_SHARDED], cq)
    sh_v, _ = _pack_rows([vel[n] for n in SMALL_SHARDED], cq)
    sh_out = _adamw(sh_w, sh_g, sh_m, sh_v, "adamw_small_sharded")
    sh_split = [_unpack_rows(o, sh_counts, [w[n].shape for n in SMALL_SHARDED]) for o in sh_out]
    for i, n in enumerate(SMALL_SHARDED):
        out[n] = tuple(rs[i] for rs in sh_split)

    total = lax.psum(loss[0, 0], ("x", "y", "c"))
    grads, deltas, new_m, new_v = ([out[n][k] for n in WEIGHT_NAMES] for k in range(4))
    return (total, dx.reshape(x.shape), *grads, *deltas, *new_m, *new_v)
```

```python
import functools

import jax
import jax.numpy as jnp
from jax import lax
from jax.experimental import pallas as pl
from jax.experimental.pallas import tpu as pltpu

F32 = jnp.float32
BF16 = jnp.bfloat16

RMS_EPS = 1e-6
LN_EPS = 1e-5
POOL_WINDOWS = (2, 4, 8, 16)
SHORT_CONV_W = 3
CONF_CONV_W = 31
N_CHIPS = 4
N_DEV = 8

ADAM_LR = 0.001
ADAM_B1 = 0.9
ADAM_B2 = 0.999
ADAM_EPS = 1e-08
ADAM_WD = 0.01
ADAM_STEP = 10

V7X_VMEM_BYTES = 64 * 1024 * 1024
VMEM_LIMIT = V7X_VMEM_BYTES - 8 * 1024 * 1024
LANES = 128
POOL_HALO = 16
SCONV_HALO = 16
CONF_HALO = 32


def _params(*sem):
    return pltpu.CompilerParams(dimension_semantics=sem, vmem_limit_bytes=VMEM_LIMIT)


def _tile(n, pref, mult=8):
    t = min(n, pref)
    while t > mult and (n % t or t % mult):
        t -= mult
    assert n % t == 0 and t % mult == 0, (n, pref, mult)
    return t


def _sigmoid(x):
    return jax.nn.sigmoid(x)


def _dot(a, b):
    return jnp.dot(a, b, preferred_element_type=F32)


def _dot_nt(a, b):
    return lax.dot_general(a, b, (((1,), (1,)), ((), ())), preferred_element_type=F32)


def _dot_tn(a, b):
    return lax.dot_general(a, b, (((0,), (0,)), ((), ())), preferred_element_type=F32)


def _colsum(x):
    return jnp.sum(x, axis=0, keepdims=True)


def _rms_stats(x):
    return lax.rsqrt(jnp.mean(x * x, axis=-1, keepdims=True) + RMS_EPS)


def _rms_bwd(du, x, gain):
    r = _rms_stats(x)
    xhat = x * r
    gdy = du * gain
    dx = r * (gdy - xhat * jnp.mean(gdy * xhat, axis=-1, keepdims=True))
    return dx, _colsum(du * xhat)


class _Task:
    def __init__(self, ins, out_shapes, aliases, sems, start, wait, done):
        self.ins, self.out_shapes, self.aliases, self.sems = list(ins), list(out_shapes), dict(aliases), list(sems)
        self.start, self.wait, self.done = start, wait, done


class _Schedule:
    def __init__(self):
        self.hosts, self.posts = {}, {}

    def host(self, kernel_name, *make_tasks):
        self.hosts.setdefault(kernel_name, []).extend(make_tasks)

    def post(self, kernel_name, *thunks):
        self.posts.setdefault(kernel_name, []).extend(thunks)

    def tasks_for(self, kernel_name):
        return [make() for make in self.hosts.pop(kernel_name, ())]

    def finished(self, kernel_name):
        for thunk in self.posts.pop(kernel_name, ()):
            thunk()


_ACTIVE_SCHEDULE = [None]


def _hosted(body, name, **kw):
    def run(*args):
        sched = _ACTIVE_SCHEDULE[0]
        tasks = sched.tasks_for(name) if sched is not None else []
        out = _call_with_tasks(body, name, tasks, kw, args) if tasks else pl.pallas_call(body, name=name, **kw)(*args)
        if sched is not None:
            sched.finished(name)
        return out

    return run


def _call_with_tasks(body, name, tasks, kw, args):
    grid = tuple(kw.get("grid", ()))
    single = not isinstance(kw["out_shape"], (list, tuple))
    out_shape = [kw["out_shape"]] if single else list(kw["out_shape"])
    out_specs = [kw["out_specs"]] if single else list(kw["out_specs"])
    scratch = list(kw.get("scratch_shapes", ()))
    n_in, n_out, n_scr = len(args), len(out_shape), len(scratch)
    t_in = [a for t in tasks for a in t.ins]
    t_out = [o for t in tasks for o in t.out_shapes]
    t_sem = [s for t in tasks for s in t.sems]
    aliases, at_in, at_out = {}, n_in, n_out
    for t in tasks:
        for i, o in t.aliases.items():
            aliases[at_in + i] = at_out + o
        at_in += len(t.ins)
        at_out += len(t.out_shapes)

    def wrapped(*refs):
        a = n_in
        b = a + len(t_in)
        c = b + n_out
        d = c + len(t_out)
        e = d + n_scr
        ins, tins, outs, touts, scr, tsems = refs[:a], refs[a:b], refs[b:c], refs[c:d], refs[d:e], refs[e:]
        views, i0, o0, s0 = [], 0, 0, 0
        for t in tasks:
            views.append((tins[i0:i0 + len(t.ins)], touts[o0:o0 + len(t.out_shapes)], tsems[s0:s0 + len(t.sems)]))
            i0, o0, s0 = i0 + len(t.ins), o0 + len(t.out_shapes), s0 + len(t.sems)

        def start_all():
            for t, v in zip(tasks, views):
                t.start(*v)

        def wait_all():
            for t, v in zip(tasks, views):
                t.wait(*v)

        if grid:
            first = functools.reduce(jnp.logical_and, [pl.program_id(i) == 0 for i in range(len(grid))])
            last = functools.reduce(jnp.logical_and, [pl.program_id(i) == grid[i] - 1 for i in range(len(grid))])
            pl.when(first)(start_all)
            body(*ins, *outs, *scr)
            pl.when(last)(wait_all)
        else:
            start_all()
            body(*ins, *outs, *scr)
            wait_all()

    res = pl.pallas_call(
        wrapped, name=name, grid=grid,
        in_specs=list(kw["in_specs"]) + [ANY] * len(t_in), out_specs=out_specs + [ANY] * len(t_out),
        out_shape=out_shape + t_out, scratch_shapes=scratch + t_sem, input_output_aliases=aliases,
        compiler_params=pltpu.CompilerParams(dimension_semantics=("arbitrary",) * len(grid), vmem_limit_bytes=VMEM_LIMIT),
    )(*args, *t_in)
    res = list(res)
    own, rest = res[:n_out], res[n_out:]
    for t in tasks:
        t.done(rest[:len(t.out_shapes)])
        rest = rest[len(t.out_shapes):]
    return own[0] if single else own


def _comm_only(tasks, name):
    _call_with_tasks(lambda: None, name, tasks, dict(grid=(), in_specs=[], out_specs=[], out_shape=[]), ())


def _rms_fwd(h, gain, name):
    s, d = h.shape
    tm = _tile(s, 512)

    def body(h_ref, g_ref, u_ref):
        x = h_ref[...]
        u_ref[...] = (x * _rms_stats(x) * g_ref[...]).astype(u_ref.dtype)

    return _hosted(
        body, name=name, grid=(s // tm,),
        in_specs=[pl.BlockSpec((tm, d), lambda m: (m, 0)), pl.BlockSpec((1, d), lambda m: (0, 0))],
        out_specs=pl.BlockSpec((tm, d), lambda m: (m, 0)),
        out_shape=jax.ShapeDtypeStruct((s, d), BF16),
        compiler_params=_params("parallel"),
    )(h, gain)


def _mm_col(a, w, bias, name):
    s, k = a.shape
    nsh, _, ns = w.shape
    tm = _tile(s, 512)
    has_bias = bias is not None

    def body(a_ref, w_ref, *rest):
        o_ref = rest[-1]
        acc = _dot(a_ref[...], w_ref[...])
        if has_bias:
            acc = acc + rest[0][...]
        o_ref[...] = acc.astype(o_ref.dtype)

    in_specs = [pl.BlockSpec((tm, k), lambda j, m: (m, 0)), pl.BlockSpec((None, k, ns), lambda j, m: (j, 0, 0))]
    args = [a, w]
    if has_bias:
        in_specs.append(pl.BlockSpec((1, ns), lambda j, m: (0, j)))
        args.append(bias)
    return _hosted(
        body, name=name, grid=(nsh, s // tm), in_specs=in_specs,
        out_specs=pl.BlockSpec((tm, ns), lambda j, m: (m, j)),
        out_shape=jax.ShapeDtypeStruct((s, nsh * ns), BF16),
        compiler_params=_params("parallel", "parallel"),
    )(*args)


def _mm_row(a, w, res, bias, swiglu, name):
    s = a.shape[0]
    k, n = w.shape
    tm = _tile(s, 256 if swiglu else 512)
    has_bias = bias is not None

    def body(*refs):
        o_ref = refs[-1]
        if swiglu:
            g_ref, up_ref, w_ref, res_ref = refs[:4]
            g = g_ref[...].astype(F32)
            act = (g * _sigmoid(g) * up_ref[...].astype(F32)).astype(BF16)
            rest = refs[4:-1]
        else:
            a_ref, w_ref, res_ref = refs[:3]
            act = a_ref[...]
            rest = refs[3:-1]
        y = res_ref[...] + _dot(act, w_ref[...])
        if has_bias:
            y = y + rest[0][...]
        o_ref[...] = y

    if swiglu:
        in_specs = [pl.BlockSpec((tm, k), lambda m: (m, 0)), pl.BlockSpec((tm, k), lambda m: (m, 1))]
        args = [a, a]
    else:
        in_specs = [pl.BlockSpec((tm, k), lambda m: (m, 0))]
        args = [a]
    in_specs += [pl.BlockSpec((k, n), lambda m: (0, 0)), pl.BlockSpec((tm, n), lambda m: (m, 0))]
    args += [w, res]
    if has_bias:
        in_specs.append(pl.BlockSpec((1, n), lambda m: (0, 0)))
        args.append(bias)
    return _hosted(
        body, name=name, grid=(s // tm,), in_specs=in_specs,
        out_specs=pl.BlockSpec((tm, n), lambda m: (m, 0)),
        out_shape=jax.ShapeDtypeStruct((s, n), F32),
        compiler_params=_params("parallel"),
    )(*args)


def _mm_nt_row(dy, w, name):
    s, n = dy.shape
    k = w.shape[0]
    tm = _tile(s, 512)

    def body(dy_ref, w_ref, o_ref):
        o_ref[...] = _dot_nt(dy_ref[...].astype(BF16), w_ref[...])

    return _hosted(
        body, name=name, grid=(s // tm,),
        in_specs=[pl.BlockSpec((tm, n), lambda m: (m, 0)), pl.BlockSpec((k, n), lambda m: (0, 0))],
        out_specs=pl.BlockSpec((tm, k), lambda m: (m, 0)),
        out_shape=jax.ShapeDtypeStruct((s, k), F32),
        compiler_params=_params("parallel"),
    )(dy, w)


def _ffn_down_bwd(dh, w, gu, name):
    s, d = dh.shape
    f = w.shape[0]
    tm = _tile(s, 256)

    def body(dh_ref, w_ref, gu_ref, o_ref):
        da = _dot_nt(dh_ref[...].astype(BF16), w_ref[...])
        g = gu_ref[:, :f].astype(F32)
        up = gu_ref[:, f:].astype(F32)
        sg = _sigmoid(g)
        o_ref[:, :f] = (da * up * sg * (1.0 + g * (1.0 - sg))).astype(o_ref.dtype)
        o_ref[:, f:] = (da * g * sg).astype(o_ref.dtype)

    return _hosted(
        body, name=name, grid=(s // tm,),
        in_specs=[pl.BlockSpec((tm, d), lambda m: (m, 0)), pl.BlockSpec((f, d), lambda m: (0, 0)),
                  pl.BlockSpec((tm, 2 * f), lambda m: (m, 0))],
        out_specs=pl.BlockSpec((tm, 2 * f), lambda m: (m, 0)),
        out_shape=jax.ShapeDtypeStruct((s, 2 * f), BF16),
        compiler_params=_params("parallel"),
    )(dh, w, gu)


def _mm_nt_col_rms_bwd(dy, w, h, gain, dh, name):
    s = dy.shape[0]
    nsh, k, ns = w.shape
    tm = _tile(s, 512)
    nm = s // tm

    def body(dy_ref, w_ref, h_ref, g_ref, dh_ref, o_ref, dg_ref, acc_ref):
        m, j = pl.program_id(0), pl.program_id(1)
        part = _dot_nt(dy_ref[...], w_ref[...])

        @pl.when(j == 0)
        def _():
            acc_ref[...] = part

        @pl.when(j > 0)
        def _():
            acc_ref[...] += part

        @pl.when(j == nsh - 1)
        def _():
            dx, dg = _rms_bwd(acc_ref[...], h_ref[...], g_ref[...])
            o_ref[...] = dh_ref[...] + dx

            @pl.when(m == 0)
            def _():
                dg_ref[...] = dg

            @pl.when(m > 0)
            def _():
                dg_ref[...] += dg

    return _hosted(
        body, name=name, grid=(nm, nsh),
        in_specs=[pl.BlockSpec((tm, ns), lambda m, j: (m, j)), pl.BlockSpec((None, k, ns), lambda m, j: (j, 0, 0)),
                  pl.BlockSpec((tm, k), lambda m, j: (m, 0)), pl.BlockSpec((1, k), lambda m, j: (0, 0)),
                  pl.BlockSpec((tm, k), lambda m, j: (m, 0))],
        out_specs=[pl.BlockSpec((tm, k), lambda m, j: (m, 0)), pl.BlockSpec((1, k), lambda m, j: (0, 0))],
        out_shape=[jax.ShapeDtypeStruct((s, k), F32), jax.ShapeDtypeStruct((1, k), F32)],
        scratch_shapes=[pltpu.VMEM((tm, k), F32)],
        compiler_params=_params("arbitrary", "arbitrary"),
    )(dy, w, h, gain, dh)


def _mm_tn(a, dy, nsh, swiglu, name, tm_pref=1024):
    s = a.shape[0]
    k = a.shape[1] // 2 if swiglu else a.shape[1]
    ns = dy.shape[1] // nsh
    tm = _tile(s, tm_pref)
    tk = _tile(k, 1408, LANES)
    nk, nm = k // tk, s // tm

    def body(*refs):
        o_ref, acc_ref = refs[-2:]
        if swiglu:
            g = refs[0][...].astype(F32)
            act = (g * _sigmoid(g) * refs[1][...].astype(F32)).astype(BF16)
            dy_ref = refs[2]
        else:
            act = refs[0][...]
            dy_ref = refs[1]
        m = pl.program_id(2)
        part = _dot_tn(act, dy_ref[...].astype(BF16))

        @pl.when(m == 0)
        def _():
            acc_ref[...] = part

        @pl.when(m > 0)
        def _():
            acc_ref[...] += part

        @pl.when(m == nm - 1)
        def _():
            o_ref[...] = acc_ref[...].astype(o_ref.dtype)

    if swiglu:
        in_specs = [pl.BlockSpec((tm, tk), lambda j, kk, m: (m, kk)), pl.BlockSpec((tm, tk), lambda j, kk, m: (m, kk + nk))]
        args = [a, a]
    else:
        in_specs = [pl.BlockSpec((tm, tk), lambda j, kk, m: (m, kk))]
        args = [a]
    in_specs.append(pl.BlockSpec((tm, ns), lambda j, kk, m: (m, j)))
    args.append(dy)
    return _hosted(
        body, name=name, grid=(nsh, nk, nm), in_specs=in_specs,
        out_specs=pl.BlockSpec((None, tk, ns), lambda j, kk, m: (j, kk, 0)),
        out_shape=jax.ShapeDtypeStruct((nsh, k, ns), BF16),
        scratch_shapes=[pltpu.VMEM((tk, ns), F32)],
        compiler_params=_params("parallel", "parallel", "arbitrary"),
    )(*args)


def _main_spec(tm, w):
    return pl.BlockSpec((tm, w), lambda m: (m, 0))


def _before_spec(tm, hb, w):
    return pl.BlockSpec((hb, w), lambda m: (jnp.maximum(m * (tm // hb) - 1, 0), 0))


def _after_spec(tm, hb, w, s):
    return pl.BlockSpec((hb, w), lambda m: (jnp.minimum((m + 1) * (tm // hb), s // hb - 1), 0))


def _row_spec(w, rows=1):
    return pl.BlockSpec((rows, w), lambda m: (0, 0))


def _accumulate(ref, val, first):
    @pl.when(first)
    def _():
        ref[...] = val

    @pl.when(jnp.logical_not(first))
    def _():
        ref[...] += val


def _sconv_taps(zext_ref, cw_ref, tm, base):
    out = cw_ref[2:3, :] * zext_ref[pl.ds(base, tm), :]
    out = out + cw_ref[1:2, :] * zext_ref[pl.ds(base - 1, tm), :]
    return out + cw_ref[0:1, :] * zext_ref[pl.ds(base - 2, tm), :]


def _sconv_fill_z(zext_ref, main_ref, before_ref, d, m):
    hb = SCONV_HALO
    zb = before_ref[:, d:2 * d].astype(F32) * before_ref[:, 2 * d:].astype(F32)
    zext_ref[pl.ds(0, hb), :] = jnp.where(m > 0, zb, 0.0)
    zext_ref[pl.ds(hb, main_ref.shape[0]), :] = main_ref[:, d:2 * d].astype(F32) * main_ref[:, 2 * d:].astype(F32)


def _sconv_fwd(bcv, cw, name):
    s, d3 = bcv.shape
    d = d3 // 3
    tm = _tile(s, 512, SCONV_HALO)

    def body(main_ref, before_ref, cw_ref, p_ref, zext_ref):
        m = pl.program_id(0)
        _sconv_fill_z(zext_ref, main_ref, before_ref, d, m)
        zc = _sconv_taps(zext_ref, cw_ref, tm, SCONV_HALO)
        p_ref[...] = (main_ref[:, :d].astype(F32) * zc).astype(p_ref.dtype)

    return _hosted(
        body, name=name, grid=(s // tm,),
        in_specs=[_main_spec(tm, d3), _before_spec(tm, SCONV_HALO, d3), _row_spec(d, SHORT_CONV_W)],
        out_specs=_main_spec(tm, d),
        out_shape=jax.ShapeDtypeStruct((s, d), BF16),
        scratch_shapes=[pltpu.VMEM((tm + SCONV_HALO, d), F32)],
        compiler_params=_params("parallel"),
    )(bcv, bcv, cw)


def _sconv_bwd(dp, bcv, cw, name):
    s, d3 = bcv.shape
    d = d3 // 3
    tm = _tile(s, 512, SCONV_HALO)
    nm = s // tm
    ha = 8

    def body(dp_ref, dpa_ref, main_ref, before_ref, after_ref, cw_ref, o_ref, dcw_ref, zext_ref, dext_ref):
        m = pl.program_id(0)
        _sconv_fill_z(zext_ref, main_ref, before_ref, d, m)
        zc = _sconv_taps(zext_ref, cw_ref, tm, SCONV_HALO)
        dp_t = dp_ref[...]
        o_ref[:, :d] = (dp_t * zc).astype(o_ref.dtype)
        dzc = dp_t * main_ref[:, :d].astype(F32)
        dext_ref[pl.ds(0, tm), :] = dzc
        dza = dpa_ref[...] * after_ref[:, :d].astype(F32)[0:ha]
        dext_ref[pl.ds(tm, ha), :] = jnp.where(m < nm - 1, dza, 0.0)
        dz = cw_ref[2:3, :] * dzc
        dz = dz + cw_ref[1:2, :] * dext_ref[pl.ds(1, tm), :]
        dz = dz + cw_ref[0:1, :] * dext_ref[pl.ds(2, tm), :]
        o_ref[:, d:2 * d] = (dz * main_ref[:, 2 * d:].astype(F32)).astype(o_ref.dtype)
        o_ref[:, 2 * d:] = (dz * main_ref[:, d:2 * d].astype(F32)).astype(o_ref.dtype)

        @pl.when(m == 0)
        def _():
            dcw_ref[...] = jnp.zeros_like(dcw_ref)

        for kk in range(SHORT_CONV_W):
            zs = zext_ref[pl.ds(SCONV_HALO - 2 + kk, tm), :]
            dcw_ref[kk:kk + 1, :] += _colsum(dzc * zs)

    return _hosted(
        body, name=name, grid=(nm,),
        in_specs=[_main_spec(tm, d), _after_spec(tm, ha, d, s), _main_spec(tm, d3), _before_spec(tm, SCONV_HALO, d3),
                  _after_spec(tm, SCONV_HALO, d3, s), _row_spec(d, SHORT_CONV_W)],
        out_specs=[_main_spec(tm, d3), _row_spec(d, 8)],
        out_shape=[jax.ShapeDtypeStruct((s, d3), BF16), jax.ShapeDtypeStruct((8, d), F32)],
        scratch_shapes=[pltpu.VMEM((tm + SCONV_HALO, d), F32), pltpu.VMEM((tm + ha, d), F32)],
        compiler_params=_params("arbitrary"),
    )(dp, dp, bcv, bcv, bcv, cw)


def _pool_counts(t0, tm, w):
    t = t0 + lax.broadcasted_iota(jnp.int32, (tm, 1), 0)
    return jnp.minimum(t + 1, w).astype(F32)


def _pool_fwd(h, gain, wg, scale, name):
    s, d = h.shape
    ng, cg, _ = wg.shape
    tm = _tile(s, 512, POOL_HALO)

    def body(h_ref, hb_ref, g_ref, wg_ref, sc_ref, o_ref, mx_ref, uext_ref):
        m = pl.program_id(0)
        x = h_ref[...]
        gain_row = g_ref[...]
        xb = hb_ref[...]
        uext_ref[pl.ds(0, POOL_HALO), :] = jnp.where(m > 0, xb * _rms_stats(xb) * gain_row, 0.0)
        uext_ref[pl.ds(POOL_HALO, tm), :] = x * _rms_stats(x) * gain_row
        for gi, win in enumerate(POOL_WINDOWS):
            cols = pl.ds(gi * cg, cg)
            u_g = uext_ref[pl.ds(POOL_HALO, tm), cols]
            acc = u_g
            for i in range(1, win):
                acc = acc + uext_ref[pl.ds(POOL_HALO - i, tm), cols]
            mixed = (acc / _pool_counts(m * tm, tm, win) - u_g).astype(BF16)
            mx_ref[:, cols] = mixed
            o_ref[:, cols] = x[:, gi * cg:(gi + 1) * cg] + _dot(mixed, wg_ref[gi]) * sc_ref[:, cols]

    return _hosted(
        body, name=name, grid=(s // tm,),
        in_specs=[_main_spec(tm, d), _before_spec(tm, POOL_HALO, d), _row_spec(d),
                  pl.BlockSpec((ng, cg, cg), lambda m: (0, 0, 0)), _row_spec(d)],
        out_specs=[_main_spec(tm, d), _main_spec(tm, d)],
        out_shape=[jax.ShapeDtypeStruct((s, d), F32), jax.ShapeDtypeStruct((s, d), BF16)],
        scratch_shapes=[pltpu.VMEM((tm + POOL_HALO, d), F32)],
        compiler_params=_params("parallel"),
    )(h, h, gain, wg, scale)


def _pool_bwd_mm(dh, mixed, wg, scale, name):
    s, d = dh.shape
    ng, cg, _ = wg.shape
    tm = _tile(s, 512)

    def body(dh_ref, mx_ref, wg_ref, sc_ref, dmx_ref, dwg_ref, dsc_ref):
        first = pl.program_id(0) == 0
        for gi in range(ng):
            cols = pl.ds(gi * cg, cg)
            dh_g = dh_ref[:, cols]
            mixed = mx_ref[:, cols]
            w_g = wg_ref[gi]
            dy = (dh_g * sc_ref[:, cols]).astype(BF16)
            dmx_ref[:, cols] = _dot_nt(dy, w_g)
            _accumulate(dsc_ref.at[:, cols], _colsum(dh_g * _dot(mixed, w_g)), first)
            _accumulate(dwg_ref.at[gi], _dot_tn(mixed, dy), first)

    return _hosted(
        body, name=name, grid=(s // tm,),
        in_specs=[_main_spec(tm, d), _main_spec(tm, d), pl.BlockSpec((ng, cg, cg), lambda m: (0, 0, 0)), _row_spec(d)],
        out_specs=[_main_spec(tm, d), pl.BlockSpec((ng, cg, cg), lambda m: (0, 0, 0)), _row_spec(d)],
        out_shape=[jax.ShapeDtypeStruct((s, d), F32), jax.ShapeDtypeStruct((ng, cg, cg), F32),
                   jax.ShapeDtypeStruct((1, d), F32)],
        compiler_params=_params("arbitrary"),
    )(dh, mixed, wg, scale)


def _pool_bwd_rms(dmixed, h, gain, dh, name):
    s, d = h.shape
    cg = d // len(POOL_WINDOWS)
    tm = _tile(s, 512, POOL_HALO)
    nm = s // tm

    def body(dmx_ref, dmxa_ref, h_ref, g_ref, dh_ref, o_ref, dg_ref, eext_ref, du_ref):
        m = pl.program_id(0)
        for gi, win in enumerate(POOL_WINDOWS):
            cols = pl.ds(gi * cg, cg)
            dmx = dmx_ref[:, cols]
            eext_ref[pl.ds(0, tm), cols] = dmx / _pool_counts(m * tm, tm, win)
            ea = dmxa_ref[:, cols] / _pool_counts((m + 1) * tm, POOL_HALO, win)
            eext_ref[pl.ds(tm, POOL_HALO), cols] = jnp.where(m < nm - 1, ea, 0.0)
            acc = -dmx
            for i in range(win):
                acc = acc + eext_ref[pl.ds(i, tm), cols]
            du_ref[:, cols] = acc
        dx, dg = _rms_bwd(du_ref[...], h_ref[...], g_ref[...])
        o_ref[...] = dh_ref[...] + dx
        _accumulate(dg_ref, dg, m == 0)

    return _hosted(
        body, name=name, grid=(nm,),
        in_specs=[_main_spec(tm, d), _after_spec(tm, POOL_HALO, d, s), _main_spec(tm, d), _row_spec(d), _main_spec(tm, d)],
        out_specs=[_main_spec(tm, d), _row_spec(d)],
        out_shape=[jax.ShapeDtypeStruct((s, d), F32), jax.ShapeDtypeStruct((1, d), F32)],
        scratch_shapes=[pltpu.VMEM((tm + POOL_HALO, d), F32), pltpu.VMEM((tm, d), F32)],
        compiler_params=_params("arbitrary"),
    )(dmixed, dmixed, h, gain, dh)


def _conf_fill_h(hext_ref, main_ref, before_ref, d, m):
    hb = before_ref[:, :d].astype(F32) * _sigmoid(before_ref[:, d:].astype(F32))
    hext_ref[pl.ds(0, CONF_HALO), :] = jnp.where(m > 0, hb, 0.0)
    hext_ref[pl.ds(CONF_HALO, main_ref.shape[0]), :] = main_ref[:, :d].astype(F32) * _sigmoid(main_ref[:, d:].astype(F32))


def _layernorm_parts(hc, g, b):
    mu = jnp.mean(hc, axis=-1, keepdims=True)
    xc = hc - mu
    rs = lax.rsqrt(jnp.mean(xc * xc, axis=-1, keepdims=True) + LN_EPS)
    xhat = xc * rs
    return xhat, rs, xhat * g + b


def _conf_mid_fwd(ag, dw, b_dw, ln_g, ln_b, name):
    s, d2 = ag.shape
    d = d2 // 2
    tm = _tile(s, 256, CONF_HALO)
    base = CONF_HALO - (CONF_CONV_W - 1)

    def body(main_ref, before_ref, dw_ref, bdw_ref, g_ref, b_ref, s_ref, hc_ref, hext_ref):
        m = pl.program_id(0)
        _conf_fill_h(hext_ref, main_ref, before_ref, d, m)
        hc = bdw_ref[...] + dw_ref[0:1, :] * hext_ref[pl.ds(base, tm), :]
        for kk in range(1, CONF_CONV_W):
            hc = hc + dw_ref[kk:kk + 1, :] * hext_ref[pl.ds(base + kk, tm), :]
        hc_ref[...] = hc
        _, _, l = _layernorm_parts(hc, g_ref[...], b_ref[...])
        s_ref[...] = (l * _sigmoid(l)).astype(s_ref.dtype)

    return _hosted(
        body, name=name, grid=(s // tm,),
        in_specs=[_main_spec(tm, d2), _before_spec(tm, CONF_HALO, d2), _row_spec(d, CONF_CONV_W), _row_spec(d),
                  _row_spec(d), _row_spec(d)],
        out_specs=[_main_spec(tm, d), _main_spec(tm, d)],
        out_shape=[jax.ShapeDtypeStruct((s, d), BF16), jax.ShapeDtypeStruct((s, d), F32)],
        scratch_shapes=[pltpu.VMEM((tm + CONF_HALO, d), F32)],
        compiler_params=_params("parallel"),
    )(ag, ag, dw, b_dw, ln_g, ln_b)


def _conf_out_bwd(dh, w, hc, ln_g, ln_b, name):
    s, d = dh.shape
    tm = _tile(s, 256)

    def body(dh_ref, w_ref, hc_ref, g_ref, b_ref, o_ref, dg_ref, db_ref, dbo_ref):
        first = pl.program_id(0) == 0
        dh_t = dh_ref[...]
        ds = _dot_nt(dh_t.astype(BF16), w_ref[...])
        xhat, rs, l = _layernorm_parts(hc_ref[...], g_ref[...], b_ref[...])
        sg = _sigmoid(l)
        dl = ds * sg * (1.0 + l * (1.0 - sg))
        dxh = dl * g_ref[...]
        o_ref[...] = rs * (dxh - jnp.mean(dxh, axis=-1, keepdims=True)
                           - xhat * jnp.mean(dxh * xhat, axis=-1, keepdims=True))
        _accumulate(dg_ref, _colsum(dl * xhat), first)
        _accumulate(db_ref, _colsum(dl), first)
        _accumulate(dbo_ref, _colsum(dh_t), first)

    return _hosted(
        body, name=name, grid=(s // tm,),
        in_specs=[_main_spec(tm, d), pl.BlockSpec((d, d), lambda m: (0, 0)), _main_spec(tm, d), _row_spec(d), _row_spec(d)],
        out_specs=[_main_spec(tm, d), _row_spec(d), _row_spec(d), _row_spec(d)],
        out_shape=[jax.ShapeDtypeStruct((s, d), F32)] + [jax.ShapeDtypeStruct((1, d), F32)] * 3,
        compiler_params=_params("arbitrary"),
    )(dh, w, hc, ln_g, ln_b)


def _conf_mid_bwd(dhc, ag, dw, name):
    s, d2 = ag.shape
    d = d2 // 2
    tm = _tile(s, 256, CONF_HALO)
    nm = s // tm
    kw = CONF_CONV_W
    base = CONF_HALO - (kw - 1)

    def body(dhc_ref, dhca_ref, main_ref, before_ref, dw_ref, o_ref, ddw_ref, dbdw_ref, dbpw_ref, hext_ref, dext_ref):
        m = pl.program_id(0)
        first = m == 0
        _conf_fill_h(hext_ref, main_ref, before_ref, d, m)
        dhc_t = dhc_ref[...]
        dext_ref[pl.ds(0, tm), :] = dhc_t
        dext_ref[pl.ds(tm, CONF_HALO), :] = jnp.where(m < nm - 1, dhca_ref[...], 0.0)
        dhh = dw_ref[kw - 1:kw, :] * dhc_t
        for kk in range(kw - 1):
            dhh = dhh + dw_ref[kk:kk + 1, :] * dext_ref[pl.ds(kw - 1 - kk, tm), :]

        @pl.when(first)
        def _():
            ddw_ref[...] = jnp.zeros_like(ddw_ref)

        for kk in range(kw):
            ddw_ref[kk:kk + 1, :] += _colsum(dhc_t * hext_ref[pl.ds(base + kk, tm), :])
        a = main_ref[:, :d].astype(F32)
        sg = _sigmoid(main_ref[:, d:].astype(F32))
        da = dhh * sg
        dgate = dhh * a * sg * (1.0 - sg)
        o_ref[:, :d] = da.astype(o_ref.dtype)
        o_ref[:, d:] = dgate.astype(o_ref.dtype)
        _accumulate(dbdw_ref, _colsum(dhc_t), first)
        _accumulate(dbpw_ref.at[:, pl.ds(0, d)], _colsum(da), first)
        _accumulate(dbpw_ref.at[:, pl.ds(d, d)], _colsum(dgate), first)

    return _hosted(
        body, name=name, grid=(nm,),
        in_specs=[_main_spec(tm, d), _after_spec(tm, CONF_HALO, d, s), _main_spec(tm, d2), _before_spec(tm, CONF_HALO, d2),
                  _row_spec(d, kw)],
        out_specs=[_main_spec(tm, d2), _row_spec(d, 32), _row_spec(d), _row_spec(d2)],
        out_shape=[jax.ShapeDtypeStruct((s, d2), BF16), jax.ShapeDtypeStruct((32, d), F32),
                   jax.ShapeDtypeStruct((1, d), F32), jax.ShapeDtypeStruct((1, d2), F32)],
        scratch_shapes=[pltpu.VMEM((tm + CONF_HALO, d), F32), pltpu.VMEM((tm + CONF_HALO, d), F32)],
        compiler_params=_params("arbitrary"),
    )(dhc, dhc, ag, ag, dw)


def _loss_head(h, gain, target, name):
    s, d = h.shape
    tm = _tile(s, 512)

    def body(h_ref, g_ref, t_ref, loss_ref, dh_ref, dg_ref):
        first = pl.program_id(0) == 0
        x = h_ref[...]
        err = x * _rms_stats(x) * g_ref[...] - t_ref[...]
        part = 0.5 * jnp.sum(jnp.mean(err * err, axis=-1, keepdims=True), axis=0, keepdims=True)
        dx, dg = _rms_bwd(err * (1.0 / d), x, g_ref[...])
        dh_ref[...] = dx
        _accumulate(loss_ref, part, first)
        _accumulate(dg_ref, dg, first)

    return _hosted(
        body, name=name, grid=(s // tm,),
        in_specs=[_main_spec(tm, d), _row_spec(d), _main_spec(tm, d)],
        out_specs=[pl.BlockSpec((1, 1), lambda m: (0, 0)), _main_spec(tm, d), _row_spec(d)],
        out_shape=[jax.ShapeDtypeStruct((1, 1), F32), jax.ShapeDtypeStruct((s, d), F32), jax.ShapeDtypeStruct((1, d), F32)],
        compiler_params=_params("arbitrary"),
    )(h, gain, target)


def _ffn_fwd(h, gain, w_gu, w_down, i):
    u = _rms_fwd(h, gain, f"ffn{i}_rms")
    gu = _mm_col(u, w_gu, None, f"ffn{i}_up")
    h_new = _mm_row(gu, w_down, h, None, True, f"ffn{i}_down")
    return h_new, (h, u, gu)


def _ffn_bwd(dh, saved, gain, w_gu, w_down, i):
    h, u, gu = saved
    dgu = _ffn_down_bwd(dh, w_down, gu, f"ffn{i}_down_bwd")
    dw_down = _mm_tn(gu, dh, 1, True, f"ffn{i}_dw_down", tm_pref=512)
    dw_gu = _mm_tn(u, dgu, N_CHIPS, False, f"ffn{i}_dw_gu")
    dh_new, dgain = _mm_nt_col_rms_bwd(dgu, w_gu, h, gain, dh, f"ffn{i}_up_bwd")
    return dh_new, dgain, dw_gu, dw_down


def _device_step(x, target, wts, g=None):
    g = {} if g is None else g
    saved = {}
    h = x

    def short_conv_fwd(h, i):
        u = _rms_fwd(h, wts[f"ln1_{i}"], f"a{i}_rms")
        bcv = _mm_col(u, wts[f"a{i}_w_in"], None, f"a{i}_in")
        p = _sconv_fwd(bcv, wts[f"a{i}_conv"], f"a{i}_conv")
        return _mm_row(p, wts[f"a{i}_w_out"], h, None, False, f"a{i}_out"), (h, u, bcv, p)

    def short_conv_bwd(dh, sv, i):
        h, u, bcv, p = sv
        dp = _mm_nt_row(dh, wts[f"a{i}_w_out"], f"a{i}_out_bwd")
        g[f"a{i}_w_out"] = _mm_tn(p, dh, 1, False, f"a{i}_dw_out")
        dbcv, dcw = _sconv_bwd(dp, bcv, wts[f"a{i}_conv"], f"a{i}_conv_bwd")
        g[f"a{i}_conv"] = dcw[:SHORT_CONV_W]
        g[f"a{i}_w_in"] = _mm_tn(u, dbcv, N_CHIPS, False, f"a{i}_dw_in")
        dh, g[f"ln1_{i}"] = _mm_nt_col_rms_bwd(dbcv, wts[f"a{i}_w_in"], h, wts[f"ln1_{i}"], dh, f"a{i}_in_bwd")
        return dh

    h, saved["a0"] = short_conv_fwd(h, 0)
    h, saved["f0"] = _ffn_fwd(h, wts["ln2_0"], wts["ffn0_w_gu"], wts["ffn0_w_down"], 0)

    h_in = h
    h, mixed = _pool_fwd(h, wts["ln1_1"], wts["b1_w_grp"], wts["b1_scale"], "b1_fwd")
    saved["b1"] = (h_in, mixed)
    h, saved["f1"] = _ffn_fwd(h, wts["ln2_1"], wts["ffn1_w_gu"], wts["ffn1_w_down"], 1)

    h_in = h
    u = _rms_fwd(h, wts["ln1_2"], "c2_rms")
    ag = _mm_col(u, wts["c2_w_pw1"], wts["c2_b_pw1"], "c2_pw1")
    sw, hc = _conf_mid_fwd(ag, wts["c2_dw"], wts["c2_b_dw"], wts["c2_ln_g"], wts["c2_ln_b"], "c2_mid")
    h = _mm_row(sw, wts["c2_w_pw2"], h, wts["c2_b_pw2"], False, "c2_pw2")
    saved["c2"] = (h_in, u, ag, sw, hc)
    h, saved["f2"] = _ffn_fwd(h, wts["ln2_2"], wts["ffn2_w_gu"], wts["ffn2_w_down"], 2)

    h, saved["a3"] = short_conv_fwd(h, 3)
    h, saved["f3"] = _ffn_fwd(h, wts["ln2_3"], wts["ffn3_w_gu"], wts["ffn3_w_down"], 3)

    loss, dh, g["ln_f"] = _loss_head(h, wts["ln_f"], target, "loss_head")

    def ffn_bwd(dh, i):
        dh, g[f"ln2_{i}"], g[f"ffn{i}_w_gu"], g[f"ffn{i}_w_down"] = _ffn_bwd(
            dh, saved[f"f{i}"], wts[f"ln2_{i}"], wts[f"ffn{i}_w_gu"], wts[f"ffn{i}_w_down"], i)
        return dh

    dh = ffn_bwd(dh, 3)
    dh = short_conv_bwd(dh, saved["a3"], 3)

    dh = ffn_bwd(dh, 2)
    h_in, u, ag, sw, hc = saved["c2"]
    dhc, g["c2_ln_g"], g["c2_ln_b"], g["c2_b_pw2"] = _conf_out_bwd(
        dh, wts["c2_w_pw2"], hc, wts["c2_ln_g"], wts["c2_ln_b"], "c2_pw2_bwd")
    g["c2_w_pw2"] = _mm_tn(sw, dh, 1, False, "c2_dw_pw2")
    dag, ddw, g["c2_b_dw"], g["c2_b_pw1"] = _conf_mid_bwd(dhc, ag, wts["c2_dw"], "c2_mid_bwd")
    g["c2_dw"] = ddw[:CONF_CONV_W]
    g["c2_w_pw1"] = _mm_tn(u, dag, N_CHIPS, False, "c2_dw_pw1")
    dh, g["ln1_2"] = _mm_nt_col_rms_bwd(dag, wts["c2_w_pw1"], h_in, wts["ln1_2"], dh, "c2_pw1_bwd")

    dh = ffn_bwd(dh, 1)
    h_in, mixed = saved["b1"]
    dmixed, g["b1_w_grp"], g["b1_scale"] = _pool_bwd_mm(dh, mixed, wts["b1_w_grp"], wts["b1_scale"], "b1_bwd_mm")
    dh, g["ln1_1"] = _pool_bwd_rms(dmixed, h_in, wts["ln1_1"], dh, "b1_bwd_rms")

    dh = ffn_bwd(dh, 0)
    dh = short_conv_bwd(dh, saved["a0"], 0)
    return loss, dh, g


MESH = pl.DeviceIdType.MESH
ANY = pl.BlockSpec(memory_space=pl.ANY)


def _position():
    return lax.axis_index("x"), lax.axis_index("y"), lax.axis_index("c")


def _other_chips(x, y):
    return [(1 - x, y), (x, 1 - y), (1 - x, 1 - y)]


def _remote(src, dst, send_sem, recv_sem, to):
    return pltpu.make_async_remote_copy(src_ref=src, dst_ref=dst, send_sem=send_sem, recv_sem=recv_sem,
                                        device_id=to, device_id_type=MESH)


def _half_rows(ref_rows, c):
    hr = ref_rows // 2
    return pl.ds(pl.multiple_of(c * hr, 16), hr)


def _allgather8(v, name):
    m_per, n = v.shape

    def body(v_ref, out_ref, send_sems, recv_sems, local_sem):
        x, y, c = _position()
        me, sibling = (x, y, c), (x, y, 1 - c)
        chips = _other_chips(x, y)

        def rows(px, py, pc):
            return out_ref.at[pl.ds((4 * px + 2 * py + pc) * m_per, m_per), :]

        def copy(k, block, to, src=None):
            return _remote(rows(*block) if src is None else src, rows(*block), send_sems.at[k], recv_sems.at[k], to)

        mine = pltpu.make_async_copy(v_ref, rows(*me), local_sem)
        mine.start()
        first = [copy(0, me, sibling, src=v_ref)]
        first += [copy(1 + j, me, (*chip, c), src=v_ref) for j, chip in enumerate(chips)]
        for cp in first:
            cp.start()
        passed = [copy(4 + j, (*chip, c), sibling) for j, chip in enumerate(chips)]
        for j, chip in enumerate(chips):
            copy(1 + j, (*chip, c), me).wait_recv()
            passed[j].start()
        copy(0, sibling, me).wait_recv()
        for j, chip in enumerate(chips):
            copy(4 + j, (*chip, 1 - c), me).wait_recv()
        for cp in first + passed:
            cp.wait_send()
        mine.wait()

    return _hosted(
        body, name=name,
        out_shape=jax.ShapeDtypeStruct((N_DEV * m_per, n), v.dtype),
        in_specs=[pl.BlockSpec(memory_space=pltpu.VMEM)],
        out_specs=pl.BlockSpec(memory_space=pltpu.VMEM),
        scratch_shapes=[pltpu.SemaphoreType.DMA((7,)), pltpu.SemaphoreType.DMA((7,)), pltpu.SemaphoreType.DMA],
        compiler_params=pltpu.CompilerParams(vmem_limit_bytes=VMEM_LIMIT),
    )(v)


def _cast_to_slot(w, idx, name):
    r, cols = w.shape
    tr = _tile(r, 256, 16)

    def body(idx_ref, w_ref, o_ref):
        o_ref[...] = w_ref[...].astype(o_ref.dtype)

    return _hosted(
        body, name=name,
        grid_spec=pltpu.PrefetchScalarGridSpec(
            num_scalar_prefetch=1, grid=(r // tr,),
            in_specs=[pl.BlockSpec((tr, cols), lambda t, idx_ref: (t, 0))],
            out_specs=pl.BlockSpec((None, tr, cols), lambda t, idx_ref: (idx_ref[0], t, 0))),
        out_shape=jax.ShapeDtypeStruct((N_CHIPS, r, cols), BF16),
        compiler_params=_params("parallel"),
    )(idx, w)


def _dma_sems(*shape):
    return [pltpu.SemaphoreType.DMA(shape), pltpu.SemaphoreType.DMA(shape)]


def _same_shapes(arrays):
    return [jax.ShapeDtypeStruct(a.shape, a.dtype) for a in arrays]


def _task_gather_ici(bufs, done):
    n = len(bufs)

    def copies(outs, sems, landing):
        x, y, c = _position()
        my_chip = 2 * x + y
        res = []
        for i in range(n):
            rows = _half_rows(bufs[i].shape[1], c)
            for r, (px, py) in enumerate(_other_chips(x, y)):
                slot = (2 * px + py) if landing else my_chip
                res.append(_remote(outs[i].at[my_chip, rows, :], outs[i].at[slot, rows, :], sems[0].at[i, r], sems[1].at[i, r],
                                   (px, py, c)))
        return res

    def start(ins, outs, sems):
        for cp in copies(outs, sems, False):
            cp.start()

    def wait(ins, outs, sems):
        for cp in copies(outs, sems, True):
            cp.wait_recv()
            cp.wait_send()

    return _Task(bufs, _same_shapes(bufs), {i: i for i in range(n)}, _dma_sems(n, 3), start, wait, done)


def _task_gather_d2d(bufs, done):
    n = len(bufs)

    def copies(outs, sems, landing):
        x, y, c = _position()
        res = []
        for i in range(n):
            rows = _half_rows(bufs[i].shape[1], (1 - c) if landing else c)
            for r, (px, py) in enumerate(_other_chips(x, y)):
                part = outs[i].at[2 * px + py, rows, :]
                res.append(_remote(part, part, sems[0].at[i, r], sems[1].at[i, r], (x, y, 1 - c)))
        return res

    def start(ins, outs, sems):
        for cp in copies(outs, sems, False):
            cp.start()

    def wait(ins, outs, sems):
        for cp in copies(outs, sems, True):
            cp.wait_recv()
        for cp in copies(outs, sems, False):
            cp.wait_send()

    return _Task(bufs, _same_shapes(bufs), {i: i for i in range(n)}, _dma_sems(n, 3), start, wait, done)


def _task_sibling_halves(grads, done):
    n = len(grads)

    def copies(ins, outs, sems):
        x, y, c = _position()
        return [_remote(ins[i].at[:, _half_rows(grads[i].shape[1], 1 - c), :], outs[i], sems[0].at[i], sems[1].at[i],
                        (x, y, 1 - c)) for i in range(n)]

    def start(ins, outs, sems):
        for cp in copies(ins, outs, sems):
            cp.start()

    def wait(ins, outs, sems):
        for cp in copies(ins, outs, sems):
            cp.wait()

    shapes = [jax.ShapeDtypeStruct((g.shape[0], g.shape[1] // 2, g.shape[2]), g.dtype) for g in grads]
    return _Task(grads, shapes, {}, _dma_sems(n), start, wait, done)


def _task_chip_sums(parts, done):
    n = len(parts)

    def copies(ins, outs, sems):
        x, y, c = _position()
        return [_remote(ins[i].at[2 * px + py], outs[i].at[r], sems[0].at[i, r], sems[1].at[i, r], (px, py, c))
                for i in range(n) for r, (px, py) in enumerate(_other_chips(x, y))]

    def start(ins, outs, sems):
        for cp in copies(ins, outs, sems):
            cp.start()

    def wait(ins, outs, sems):
        for cp in copies(ins, outs, sems):
            cp.wait()

    shapes = [jax.ShapeDtypeStruct((3,) + p.shape[1:], p.dtype) for p in parts]
    return _Task(parts, shapes, {}, _dma_sems(n, 3), start, wait, done)


def _task_reduced_halves(fulls, done):
    n = len(fulls)

    def copies(outs, sems, landing):
        x, y, c = _position()
        res = []
        for i in range(n):
            rows = _half_rows(fulls[i].shape[0], (1 - c) if landing else c)
            res.append(_remote(outs[i].at[rows, :], outs[i].at[rows, :], sems[0].at[i], sems[1].at[i], (x, y, 1 - c)))
        return res

    def start(ins, outs, sems):
        for cp in copies(outs, sems, False):
            cp.start()

    def wait(ins, outs, sems):
        for cp in copies(outs, sems, True):
            cp.wait_recv()
        for cp in copies(outs, sems, False):
            cp.wait_send()

    return _Task(fulls, _same_shapes(fulls), {i: i for i in range(n)}, _dma_sems(n), start, wait, done)


def _add_halves(grad, sib, c, name):
    nsh, r, cols = grad.shape
    hr = r // 2
    tr = _tile(hr, 512, 16)
    nt = hr // tr

    def body(c_ref, g_ref, s_ref, o_ref):
        o_ref[...] = (g_ref[...].astype(F32) + s_ref[...].astype(F32)).astype(o_ref.dtype)

    return _hosted(
        body, name=name,
        grid_spec=pltpu.PrefetchScalarGridSpec(
            num_scalar_prefetch=1, grid=(nsh, nt),
            in_specs=[pl.BlockSpec((None, tr, cols), lambda j, t, c_ref: (j, c_ref[1] * nt + t, 0)),
                      pl.BlockSpec((None, tr, cols), lambda j, t, c_ref: (j, t, 0))],
            out_specs=pl.BlockSpec((None, tr, cols), lambda j, t, c_ref: (j, t, 0))),
        out_shape=jax.ShapeDtypeStruct((nsh, hr, cols), BF16),
        compiler_params=_params("parallel", "parallel"),
    )(c, grad, sib)


def _sum_chips(own, landed, idx, name):
    nsh, hr, cols = own.shape
    tr = _tile(hr, 512, 16)
    nt = hr // tr

    def body(idx_ref, p_ref, l_ref, o_ref):
        acc = p_ref[...].astype(F32)
        for k in range(landed.shape[0]):
            acc = acc + l_ref[k].astype(F32)
        o_ref[...] = acc

    return _hosted(
        body, name=name,
        grid_spec=pltpu.PrefetchScalarGridSpec(
            num_scalar_prefetch=1, grid=(nt,),
            in_specs=[pl.BlockSpec((None, tr, cols), lambda t, idx_ref: (idx_ref[0], t, 0)),
                      pl.BlockSpec((landed.shape[0], tr, cols), lambda t, idx_ref: (0, t, 0))],
            out_specs=pl.BlockSpec((tr, cols), lambda t, idx_ref: (idx_ref[1] * nt + t, 0))),
        out_shape=jax.ShapeDtypeStruct((2 * hr, cols), F32),
        compiler_params=_params("parallel"),
    )(idx, own, landed)


def _sum_devices(blocks, name):
    m8, n = blocks.shape
    m = m8 // N_DEV

    def body(b_ref, o_ref):
        acc = b_ref[pl.ds(0, m), :]
        for k in range(1, N_DEV):
            acc = acc + b_ref[pl.ds(k * m, m), :]
        o_ref[...] = acc

    return _hosted(
        body, name=name, out_shape=jax.ShapeDtypeStruct((m, n), F32),
        in_specs=[pl.BlockSpec(memory_space=pltpu.VMEM)], out_specs=pl.BlockSpec(memory_space=pltpu.VMEM),
        compiler_params=pltpu.CompilerParams(vmem_limit_bytes=VMEM_LIMIT),
    )(blocks)


def _adamw(w, g, m, v, name):
    r, cols = w.shape
    tr = _tile(r, 256) if r % 8 == 0 else r
    c1 = 1.0 / (1.0 - ADAM_B1 ** ADAM_STEP)
    c2 = 1.0 / (1.0 - ADAM_B2 ** ADAM_STEP)

    def body(w_ref, g_ref, m_ref, v_ref, go_ref, d_ref, mo_ref, vo_ref):
        grad = g_ref[...]
        new_m = ADAM_B1 * m_ref[...] + (1.0 - ADAM_B1) * grad
        new_v = ADAM_B2 * v_ref[...] + (1.0 - ADAM_B2) * (grad * grad)
        go_ref[...] = grad
        mo_ref[...] = new_m
        vo_ref[...] = new_v
        d_ref[...] = -ADAM_LR * ((new_m * c1) / (jnp.sqrt(new_v * c2) + ADAM_EPS) + ADAM_WD * w_ref[...])

    spec = pl.BlockSpec((tr, cols), lambda t: (t, 0))
    return _hosted(
        body, name=name, grid=(r // tr,), in_specs=[spec] * 4, out_specs=[spec] * 4,
        out_shape=[jax.ShapeDtypeStruct((r, cols), F32)] * 4,
        compiler_params=_params("parallel"),
    )(w, g, m, v)


WEIGHT_NAMES = (
    "ln1_0", "a0_w_in", "a0_conv", "a0_w_out", "ln2_0", "ffn0_w_gu", "ffn0_w_down",
    "ln1_1", "b1_w_grp", "b1_scale", "ln2_1", "ffn1_w_gu", "ffn1_w_down",
    "ln1_2", "c2_w_pw1", "c2_b_pw1", "c2_dw", "c2_b_dw", "c2_ln_g", "c2_ln_b", "c2_w_pw2", "c2_b_pw2",
    "ln2_2", "ffn2_w_gu", "ffn2_w_down",
    "ln1_3", "a3_w_in", "a3_conv", "a3_w_out", "ln2_3", "ffn3_w_gu", "ffn3_w_down", "ln_f")
BIG = ("a0_w_in", "a0_w_out", "ffn0_w_gu", "ffn0_w_down", "ffn1_w_gu", "ffn1_w_down", "c2_w_pw1", "c2_w_pw2",
       "ffn2_w_gu", "ffn2_w_down", "a3_w_in", "a3_w_out", "ffn3_w_gu", "ffn3_w_down")
SMALL_SHARDED = ("a0_conv", "a3_conv", "c2_dw", "b1_w_grp")
REPLICATED = tuple(n for n in WEIGHT_NAMES if n not in BIG and n not in SMALL_SHARDED)


def _pad_rows(a, mult=8):
    pad = -a.shape[0] % mult
    return a if pad == 0 else jnp.concatenate([a, jnp.zeros((pad, a.shape[1]), a.dtype)], axis=0)


def _pack_rows(parts, width):
    rows = [p.reshape(-1, width) for p in parts]
    return _pad_rows(jnp.concatenate(rows, axis=0)), [r.shape[0] for r in rows]


def _unpack_rows(packed, counts, shapes):
    out, at = [], 0
    for n, shp in zip(counts, shapes):
        out.append(packed[at:at + n].reshape(shp))
        at += n
    return out


COLUMN_SHARDED = ("w_in", "w_gu", "w_pw1")


class _Weights(dict):
    def __init__(self, bufs):
        super().__init__()
        self.bufs = bufs

    def __missing__(self, name):
        buf = self.bufs[name]
        return buf if name.endswith(COLUMN_SHARDED) else buf.reshape(-1, buf.shape[-1])


class _Exchange:
    def __init__(self, w, mom, vel, idx):
        self.w, self.mom, self.vel, self.idx = w, mom, vel, idx
        self.bufs = {}
        self.weights = _Weights(self.bufs)
        self.grads = {}
        self.sib, self.part, self.landed, self.full, self.updates = {}, {}, {}, {}, {}

    def cast(self, n):
        self.bufs[n] = _cast_to_slot(self.w[n], self.idx, f"cast_{n}")

    @staticmethod
    def _store(table, names):
        def done(arrays):
            table.update(zip(names, arrays))
        return done

    def _grad(self, n):
        g = self.grads[n]
        return g.reshape(N_CHIPS, -1, g.shape[-1])

    def gather_ici(self, *names):
        return lambda: _task_gather_ici([self.bufs[n] for n in names], self._store(self.bufs, names))

    def gather_d2d(self, *names):
        return lambda: _task_gather_d2d([self.bufs[n] for n in names], self._store(self.bufs, names))

    def sibling_halves(self, *names):
        return lambda: _task_sibling_halves([self._grad(n) for n in names], self._store(self.sib, names))

    def add_halves(self, *names):
        def run():
            for n in names:
                self.part[n] = _add_halves(self._grad(n), self.sib.pop(n), self.idx, f"reduce_add_{n}")
        return run

    def chip_sums(self, *names):
        return lambda: _task_chip_sums([self.part[n] for n in names], self._store(self.landed, names))

    def sum_chips(self, *names):
        def run():
            for n in names:
                self.full[n] = _sum_chips(self.part.pop(n), self.landed.pop(n), self.idx, f"reduce_sum_{n}")
        return run

    def reduced_halves(self, *names):
        return lambda: _task_reduced_halves([self.full[n] for n in names], self._store(self.full, names))

    def adamw(self, *names):
        def run():
            for n in names:
                self.updates[n] = _adamw(self.w[n], self.full.pop(n), self.mom[n], self.vel[n], f"adamw_{n}")
        return run


def _plan(ex):
    s = _Schedule()

    def ffn(i):
        return f"ffn{i}_w_gu", f"ffn{i}_w_down"

    c2, a3 = ("c2_w_pw1", "c2_w_pw2"), ("a3_w_in", "a3_w_out")
    for i in (0, 3):
        gu, down = ffn(i)
        s.host(f"a{i}_in", ex.gather_ici(gu))
        s.host(f"a{i}_conv", ex.gather_ici(down), ex.gather_d2d(gu))
        s.host(f"a{i}_out", ex.gather_d2d(down))
    gu, down = ffn(1)
    s.host("ffn0_up", ex.gather_ici(gu))
    s.host("ffn0_down", ex.gather_ici(down), ex.gather_d2d(gu))
    s.host("b1_fwd", ex.gather_d2d(down))
    s.host("ffn1_up", ex.gather_ici(*c2))
    s.host("ffn1_down", ex.gather_d2d(*c2))
    s.host("c2_mid", ex.gather_ici(*ffn(2)))
    s.host("c2_pw2", ex.gather_d2d(*ffn(2)))
    s.host("ffn2_up", ex.gather_ici(*a3))
    s.host("ffn2_down", ex.gather_d2d(*a3))

    def reduce_on(names, first, ici_hosts, last):
        s.host(first, ex.sibling_halves(*names))
        s.post(first, ex.add_halves(*names))
        for host, hosted in ici_hosts:
            s.host(host, ex.chip_sums(*hosted))
        s.post(ici_hosts[-1][0], ex.sum_chips(*names))
        if last is not None:
            s.host(last, ex.reduced_halves(*names))
            s.post(last, ex.adamw(*names))

    gu, down = ffn(3)
    reduce_on((gu, down), "a3_out_bwd", [("a3_conv_bwd", (down,)), ("a3_in_bwd", (gu,))], "ffn2_down_bwd")
    reduce_on(a3, "ffn2_down_bwd", [("ffn2_dw_down", a3)], "ffn2_dw_gu")
    reduce_on(ffn(2), "c2_pw2_bwd", [("c2_mid_bwd", ffn(2))], "c2_pw1_bwd")
    reduce_on(c2, "ffn1_down_bwd", [("ffn1_dw_down", c2)], "ffn1_dw_gu")
    gu, down = ffn(1)
    reduce_on((gu, down), "b1_bwd_mm", [("b1_bwd_rms", (down,)), ("ffn0_down_bwd", (gu,))], "ffn0_dw_down")
    gu, down = ffn(0)
    reduce_on((gu, down), "a0_out_bwd", [("a0_conv_bwd", (down,)), ("a0_in_bwd", (gu,))], None)
    return s


def kernel(x, *rest):
    nw = len(WEIGHT_NAMES)
    w = dict(zip(WEIGHT_NAMES, rest[:nw]))
    target = rest[nw]
    mom = dict(zip(WEIGHT_NAMES, rest[nw + 1:2 * nw + 1]))
    vel = dict(zip(WEIGHT_NAMES, rest[2 * nw + 1:3 * nw + 1]))
    cx, cy, cc = _position()
    my_chip = 2 * cx + cy
    d = x.shape[-1]
    cq = d // N_CHIPS

    ex = _Exchange(w, mom, vel, jnp.stack([my_chip, cc]).astype(jnp.int32))
    sched = _plan(ex)
    for n in BIG:
        ex.cast(n)
    first_layer = ("a0_w_in", "a0_w_out")
    _comm_only([ex.gather_ici(*first_layer)()], "gather_a0_ici")
    _comm_only([ex.gather_d2d(*first_layer)()], "gather_a0_d2d")

    small_blk, small_counts = _pack_rows([w[n] for n in SMALL_SHARDED], cq)
    small_all = _allgather8(small_blk, "gather_small").reshape(N_CHIPS, 2, small_blk.shape[0], cq)[:, 0]
    small_parts = _unpack_rows(jnp.transpose(small_all, (1, 0, 2)), small_counts,
                               [(w[n].reshape(-1, cq).shape[0], N_CHIPS, cq) for n in SMALL_SHARDED])
    wts = ex.weights
    for n in REPLICATED:
        wts[n] = w[n].reshape(1, -1)
    for n, part in zip(SMALL_SHARDED, small_parts):
        if n == "b1_w_grp":
            ng, rq, cg = w[n].shape
            full = jnp.transpose(part.reshape(ng, rq, N_CHIPS, cg), (0, 2, 1, 3)).reshape(ng, N_CHIPS * rq, cg)
            wts[n] = full.astype(BF16)
        else:
            wts[n] = part.reshape(part.shape[0], d)

    _ACTIVE_SCHEDULE[0] = sched
    try:
        loss, dx, g = _device_step(x[0], target[0], wts, ex.grads)
    finally:
        _ACTIVE_SCHEDULE[0] = None
    assert not sched.hosts and not sched.posts, (sched.hosts, sched.posts)

    _comm_only([ex.reduced_halves("ffn0_w_gu", "ffn0_w_down")(), ex.sibling_halves(*first_layer)()], "reduce_tail_d2d")
    ex.adamw("ffn0_w_gu", "ffn0_w_down")()
    ex.add_halves(*first_layer)()
    _comm_only([ex.chip_sums(*first_layer)()], "reduce_tail_ici")
    ex.sum_chips(*first_layer)()
    _comm_only([ex.reduced_halves(*first_layer)()], "reduce_tail_halves")
    ex.adamw(*first_layer)()

    small_names = REPLICATED + SMALL_SHARDED
    flat = []
    for n in small_names:
        gn = g[n]
        if n == "b1_w_grp":
            gn = gn.reshape(-1, gn.shape[-1])
        flat.append(gn.astype(F32))
    sm_blk, sm_counts = _pack_rows(flat, cq)
    sm_sum = _sum_devices(_allgather8(sm_blk, "gather_small_grads"), "sum_small_grads")
    sm = dict(zip(small_names, _unpack_rows(sm_sum, sm_counts, [f.shape for f in flat])))

    out = ex.updates
    rep_w, rep_counts = _pack_rows([w[n] for n in REPLICATED], d)
    rep_g, _ = _pack_rows([sm[n] for n in REPLICATED], d)
    rep_m, _ = _pack_rows([mom[n] for n in REPLICATED], d)
    rep_v, _ = _pack_rows([vel[n] for n in REPLICATED], d)
    rep_out = _adamw(rep_w, rep_g, rep_m, rep_v, "adamw_replicated")
    rep_split = [_unpack_rows(o, rep_counts, [w[n].shape for n in REPLICATED]) for o in rep_out]
    for i, n in enumerate(REPLICATED):
        out[n] = tuple(rs[i] for rs in rep_split)
    shard_g = []
    for n in SMALL_SHARDED:
        full = sm[n]
        if n == "b1_w_grp":
            ng, rq, cg = w[n].shape
            full = full.reshape(ng, N_CHIPS, rq, cg)
            mine = lax.dynamic_index_in_dim(full, my_chip, axis=1, keepdims=False)
        else:
            full = full.reshape(full.shape[0], N_CHIPS, cq)
            mine = lax.dynamic_index_in_dim(full, my_chip, axis=1, keepdims=False)
        shard_g.append(mine)
    sh_w, sh_counts = _pack_rows([w[n] for n in SMALL_SHARDED], cq)
    sh_g, _ = _pack_rows(shard_g, cq)
    sh_m, _ = _pack_rows([mom[n] for n in SMALL_SHARDED], cq)
    sh_v, _ = _pack_rows([vel[n] for n in SMALL_SHARDED], cq)
    sh_out = _adamw(sh_w, sh_g, sh_m, sh_v, "adamw_small_sharded")
    sh_split = [_unpack_rows(o, sh_counts, [w[n].shape for n in SMALL_SHARDED]) for o in sh_out]
    for i, n in enumerate(SMALL_SHARDED):
        out[n] = tuple(rs[i] for rs in sh_split)

    total = lax.psum(loss[0, 0], ("x", "y", "c"))
    grads, deltas, new_m, new_v = ([out[n][k] for n in WEIGHT_NAMES] for k in range(4))
    return (total, dx.reshape(x.shape), *grads, *deltas, *new_m, *new_v)
```

```python
import functools

import jax
import jax.numpy as jnp
from jax import lax
from jax.experimental import pallas as pl
from jax.experimental.pallas import tpu as pltpu

F32 = jnp.float32
BF16 = jnp.bfloat16

RMS_EPS = 1e-6
LN_EPS = 1e-5
POOL_WINDOWS = (2, 4, 8, 16)
SHORT_CONV_W = 3
CONF_CONV_W = 31
N_CHIPS = 4
N_DEV = 8

ADAM_LR = 0.001
ADAM_B1 = 0.9
ADAM_B2 = 0.999
ADAM_EPS = 1e-08
ADAM_WD = 0.01
ADAM_STEP = 10

V7X_VMEM_BYTES = 64 * 1024 * 1024
VMEM_LIMIT = V7X_VMEM_BYTES - 8 * 1024 * 1024
LANES = 128
POOL_HALO = 16
SCONV_HALO = 16
CONF_HALO = 32


def _params(*sem):
    return pltpu.CompilerParams(dimension_semantics=sem, vmem_limit_bytes=VMEM_LIMIT)


def _tile(n, pref, mult=8):
    t = min(n, pref)
    while t > mult and (n % t or t % mult):
        t -= mult
    assert n % t == 0 and t % mult == 0, (n, pref, mult)
    return t


def _sigmoid(x):
    return jax.nn.sigmoid(x)


def _dot(a, b):
    return jnp.dot(a, b, preferred_element_type=F32)


def _dot_nt(a, b):
    return lax.dot_general(a, b, (((1,), (1,)), ((), ())), preferred_element_type=F32)


def _dot_tn(a, b):
    return lax.dot_general(a, b, (((0,), (0,)), ((), ())), preferred_element_type=F32)


def _colsum(x):
    return jnp.sum(x, axis=0, keepdims=True)


def _rms_stats(x):
    return lax.rsqrt(jnp.mean(x * x, axis=-1, keepdims=True) + RMS_EPS)


def _rms_bwd(du, x, gain):
    r = _rms_stats(x)
    xhat = x * r
    gdy = du * gain
    dx = r * (gdy - xhat * jnp.mean(gdy * xhat, axis=-1, keepdims=True))
    return dx, _colsum(du * xhat)


class _Task:
    def __init__(self, ins, out_shapes, aliases, sems, start, wait, done):
        self.ins, self.out_shapes, self.aliases, self.sems = list(ins), list(out_shapes), dict(aliases), list(sems)
        self.start, self.wait, self.done = start, wait, done


class _Schedule:
    def __init__(self):
        self.hosts, self.posts = {}, {}

    def host(self, kernel_name, *make_tasks):
        self.hosts.setdefault(kernel_name, []).extend(make_tasks)

    def post(self, kernel_name, *thunks):
        self.posts.setdefault(kernel_name, []).extend(thunks)

    def tasks_for(self, kernel_name):
        return [make() for make in self.hosts.pop(kernel_name, ())]

    def finished(self, kernel_name):
        for thunk in self.posts.pop(kernel_name, ()):
            thunk()


_ACTIVE_SCHEDULE = [None]


def _hosted(body, name, **kw):
    def run(*args):
        sched = _ACTIVE_SCHEDULE[0]
        tasks = sched.tasks_for(name) if sched is not None else []
        out = _call_with_tasks(body, name, tasks, kw, args) if tasks else pl.pallas_call(body, name=name, **kw)(*args)
        if sched is not None:
            sched.finished(name)
        return out

    return run


def _call_with_tasks(body, name, tasks, kw, args):
    grid = tuple(kw.get("grid", ()))
    single = not isinstance(kw["out_shape"], (list, tuple))
    out_shape = [kw["out_shape"]] if single else list(kw["out_shape"])
    out_specs = [kw["out_specs"]] if single else list(kw["out_specs"])
    scratch = list(kw.get("scratch_shapes", ()))
    n_in, n_out, n_scr = len(args), len(out_shape), len(scratch)
    t_in = [a for t in tasks for a in t.ins]
    t_out = [o for t in tasks for o in t.out_shapes]
    t_sem = [s for t in tasks for s in t.sems]
    aliases, at_in, at_out = {}, n_in, n_out
    for t in tasks:
        for i, o in t.aliases.items():
            aliases[at_in + i] = at_out + o
        at_in += len(t.ins)
        at_out += len(t.out_shapes)

    def wrapped(*refs):
        a = n_in
        b = a + len(t_in)
        c = b + n_out
        d = c + len(t_out)
        e = d + n_scr
        ins, tins, outs, touts, scr, tsems = refs[:a], refs[a:b], refs[b:c], refs[c:d], refs[d:e], refs[e:]
        views, i0, o0, s0 = [], 0, 0, 0
        for t in tasks:
            views.append((tins[i0:i0 + len(t.ins)], touts[o0:o0 + len(t.out_shapes)], tsems[s0:s0 + len(t.sems)]))
            i0, o0, s0 = i0 + len(t.ins), o0 + len(t.out_shapes), s0 + len(t.sems)

        def start_all():
            for t, v in zip(tasks, views):
                t.start(*v)

        def wait_all():
            for t, v in zip(tasks, views):
                t.wait(*v)

        if grid:
            first = functools.reduce(jnp.logical_and, [pl.program_id(i) == 0 for i in range(len(grid))])
            last = functools.reduce(jnp.logical_and, [pl.program_id(i) == grid[i] - 1 for i in range(len(grid))])
            pl.when(first)(start_all)
            body(*ins, *outs, *scr)
            pl.when(last)(wait_all)
        else:
            start_all()
            body(*ins, *outs, *scr)
            wait_all()

    res = pl.pallas_call(
        wrapped, name=name, grid=grid,
        in_specs=list(kw["in_specs"]) + [ANY] * len(t_in), out_specs=out_specs + [ANY] * len(t_out),
        out_shape=out_shape + t_out, scratch_shapes=scratch + t_sem, input_output_aliases=aliases,
        compiler_params=pltpu.CompilerParams(dimension_semantics=("arbitrary",) * len(grid), vmem_limit_bytes=VMEM_LIMIT),
    )(*args, *t_in)
    res = list(res)
    own, rest = res[:n_out], res[n_out:]
    for t in tasks:
        t.done(rest[:len(t.out_shapes)])
        rest = rest[len(t.out_shapes):]
    return own[0] if single else own


def _comm_only(tasks, name):
    _call_with_tasks(lambda: None, name, tasks, dict(grid=(), in_specs=[], out_specs=[], out_shape=[]), ())


def _rms_fwd(h, gain, name):
    s, d = h.shape
    tm = _tile(s, 512)

    def body(h_ref, g_ref, u_ref):
        x = h_ref[...]
        u_ref[...] = (x * _rms_stats(x) * g_ref[...]).astype(u_ref.dtype)

    return _hosted(
        body, name=name, grid=(s // tm,),
        in_specs=[pl.BlockSpec((tm, d), lambda m: (m, 0)), pl.BlockSpec((1, d), lambda m: (0, 0))],
        out_specs=pl.BlockSpec((tm, d), lambda m: (m, 0)),
        out_shape=jax.ShapeDtypeStruct((s, d), BF16),
        compiler_params=_params("parallel"),
    )(h, gain)


def _mm_col(a, w, bias, name):
    s, k = a.shape
    nsh, _, ns = w.shape
    tm = _tile(s, 512)
    has_bias = bias is not None

    def body(a_ref, w_ref, *rest):
        o_ref = rest[-1]
        acc = _dot(a_ref[...], w_ref[...])
        if has_bias:
            acc = acc + rest[0][...]
        o_ref[...] = acc.astype(o_ref.dtype)

    in_specs = [pl.BlockSpec((tm, k), lambda j, m: (m, 0)), pl.BlockSpec((None, k, ns), lambda j, m: (j, 0, 0))]
    args = [a, w]
    if has_bias:
        in_specs.append(pl.BlockSpec((1, ns), lambda j, m: (0, j)))
        args.append(bias)
    return _hosted(
        body, name=name, grid=(nsh, s // tm), in_specs=in_specs,
        out_specs=pl.BlockSpec((tm, ns), lambda j, m: (m, j)),
        out_shape=jax.ShapeDtypeStruct((s, nsh * ns), BF16),
        compiler_params=_params("parallel", "parallel"),
    )(*args)


def _mm_row(a, w, res, bias, name):
    s = a.shape[0]
    k, n = w.shape
    tm = _tile(s, 512)
    has_bias = bias is not None

    def body(a_ref, w_ref, res_ref, *rest):
        o_ref = rest[-1]
        y = res_ref[...] + _dot(a_ref[...], w_ref[...])
        if has_bias:
            y = y + rest[0][...]
        o_ref[...] = y

    in_specs = [pl.BlockSpec((tm, k), lambda m: (m, 0)), pl.BlockSpec((k, n), lambda m: (0, 0)),
                pl.BlockSpec((tm, n), lambda m: (m, 0))]
    args = [a, w, res]
    if has_bias:
        in_specs.append(pl.BlockSpec((1, n), lambda m: (0, 0)))
        args.append(bias)
    return _hosted(
        body, name=name, grid=(s // tm,), in_specs=in_specs,
        out_specs=pl.BlockSpec((tm, n), lambda m: (m, 0)),
        out_shape=jax.ShapeDtypeStruct((s, n), F32),
        compiler_params=_params("parallel"),
    )(*args)


def _mm_nt_row(dy, w, name):
    s, n = dy.shape
    k = w.shape[0]
    tm = _tile(s, 512)

    def body(dy_ref, w_ref, o_ref):
        o_ref[...] = _dot_nt(dy_ref[...].astype(BF16), w_ref[...])

    return _hosted(
        body, name=name, grid=(s // tm,),
        in_specs=[pl.BlockSpec((tm, n), lambda m: (m, 0)), pl.BlockSpec((k, n), lambda m: (0, 0))],
        out_specs=pl.BlockSpec((tm, k), lambda m: (m, 0)),
        out_shape=jax.ShapeDtypeStruct((s, k), F32),
        compiler_params=_params("parallel"),
    )(dy, w)


def _ffn_up(u, w, name):
    s, d = u.shape
    _, _, ns = w.shape
    tm = _tile(s, 512)

    def body(u_ref, wg_ref, wu_ref, act_ref, s1_ref, q1_ref):
        x = u_ref[...]
        g = _dot(x, wg_ref[...])
        up = _dot(x, wu_ref[...])
        sg = _sigmoid(g)
        s1 = g * sg
        act_ref[...] = (s1 * up).astype(act_ref.dtype)
        s1_ref[...] = s1.astype(s1_ref.dtype)
        q1_ref[...] = (up * sg * (1.0 + g * (1.0 - sg))).astype(q1_ref.dtype)

    out = pl.BlockSpec((tm, ns), lambda j, m: (m, j))
    return _hosted(
        body, name=name, grid=(2, s // tm),
        in_specs=[pl.BlockSpec((tm, d), lambda j, m: (m, 0)), pl.BlockSpec((None, d, ns), lambda j, m: (j, 0, 0)),
                  pl.BlockSpec((None, d, ns), lambda j, m: (j + 2, 0, 0))],
        out_specs=[out, out, out],
        out_shape=[jax.ShapeDtypeStruct((s, 2 * ns), BF16)] * 3,
        compiler_params=_params("parallel", "parallel"),
    )(u, w, w)


def _ffn_down_bwd(dh, w, s1, q1, name):
    s, d = dh.shape
    f = w.shape[0]
    tm = _tile(s, 256)

    def body(dh_ref, w_ref, s1_ref, q1_ref, o_ref):
        da = _dot_nt(dh_ref[...].astype(BF16), w_ref[...])
        o_ref[:, :f] = (da * q1_ref[...].astype(F32)).astype(o_ref.dtype)
        o_ref[:, f:] = (da * s1_ref[...].astype(F32)).astype(o_ref.dtype)

    return _hosted(
        body, name=name, grid=(s // tm,),
        in_specs=[pl.BlockSpec((tm, d), lambda m: (m, 0)), pl.BlockSpec((f, d), lambda m: (0, 0)),
                  pl.BlockSpec((tm, f), lambda m: (m, 0)), pl.BlockSpec((tm, f), lambda m: (m, 0))],
        out_specs=pl.BlockSpec((tm, 2 * f), lambda m: (m, 0)),
        out_shape=jax.ShapeDtypeStruct((s, 2 * f), BF16),
        compiler_params=_params("parallel"),
    )(dh, w, s1, q1)


def _mm_nt_col_rms_bwd(dy, w, h, gain, dh, name):
    s = dy.shape[0]
    nsh, k, ns = w.shape
    tm = _tile(s, 256)

    def body(dy_ref, w_ref, h_ref, g_ref, dh_ref, o_ref, dg_ref):
        du = _dot_nt(dy_ref[:, :ns], w_ref[0])
        for j in range(1, nsh):
            du = du + _dot_nt(dy_ref[:, j * ns:(j + 1) * ns], w_ref[j])
        dx, dg = _rms_bwd(du, h_ref[...], g_ref[...])
        o_ref[...] = dh_ref[...] + dx
        _accumulate(dg_ref, dg, pl.program_id(0) == 0)

    return _hosted(
        body, name=name, grid=(s // tm,),
        in_specs=[pl.BlockSpec((tm, nsh * ns), lambda m: (m, 0)), pl.BlockSpec((nsh, k, ns), lambda m: (0, 0, 0)),
                  pl.BlockSpec((tm, k), lambda m: (m, 0)), pl.BlockSpec((1, k), lambda m: (0, 0)),
                  pl.BlockSpec((tm, k), lambda m: (m, 0))],
        out_specs=[pl.BlockSpec((tm, k), lambda m: (m, 0)), pl.BlockSpec((1, k), lambda m: (0, 0))],
        out_shape=[jax.ShapeDtypeStruct((s, k), F32), jax.ShapeDtypeStruct((1, k), F32)],
        compiler_params=_params("arbitrary"),
    )(dy, w, h, gain, dh)


def _mm_tn(a, dy, nsh, name):
    s, k = a.shape
    ns = dy.shape[1] // nsh
    tm = _tile(s, 1024)
    tk = _tile(k, 1408, LANES)
    nk, nm = k // tk, s // tm

    def body(a_ref, dy_ref, o_ref, acc_ref):
        m = pl.program_id(2)
        part = _dot_tn(a_ref[...], dy_ref[...].astype(BF16))

        @pl.when(m == 0)
        def _():
            acc_ref[...] = part

        @pl.when(m > 0)
        def _():
            acc_ref[...] += part

        @pl.when(m == nm - 1)
        def _():
            o_ref[...] = acc_ref[...].astype(o_ref.dtype)

    return _hosted(
        body, name=name, grid=(nsh, nk, nm),
        in_specs=[pl.BlockSpec((tm, tk), lambda j, kk, m: (m, kk)), pl.BlockSpec((tm, ns), lambda j, kk, m: (m, j))],
        out_specs=pl.BlockSpec((None, tk, ns), lambda j, kk, m: (j, kk, 0)),
        out_shape=jax.ShapeDtypeStruct((nsh, k, ns), BF16),
        scratch_shapes=[pltpu.VMEM((tk, ns), F32)],
        compiler_params=_params("parallel", "parallel", "arbitrary"),
    )(a, dy)


def _main_spec(tm, w):
    return pl.BlockSpec((tm, w), lambda m: (m, 0))


def _before_spec(tm, hb, w):
    return pl.BlockSpec((hb, w), lambda m: (jnp.maximum(m * (tm // hb) - 1, 0), 0))


def _after_spec(tm, hb, w, s):
    return pl.BlockSpec((hb, w), lambda m: (jnp.minimum((m + 1) * (tm // hb), s // hb - 1), 0))


def _row_spec(w, rows=1):
    return pl.BlockSpec((rows, w), lambda m: (0, 0))


CHUNK_ROWS = 32


def _chunks(tm, d):
    for c0 in range(0, d, LANES):
        for r0 in range(0, tm, CHUNK_ROWS):
            yield pl.ds(c0, LANES), r0


def _fold_rows(x):
    acc = x[0:8]
    for i in range(8, CHUNK_ROWS, 8):
        acc = acc + x[i:i + 8]
    return acc


def _build_shifts(ext8_ref):
    n = ext8_ref.shape[1] - 8
    for r in range(1, 8):
        ext8_ref[r, pl.ds(0, n), :] = ext8_ref[0, pl.ds(r, n), :]


def _shifted(ext8_ref, start, cols):
    return ext8_ref[start % 8, pl.ds(start - start % 8, CHUNK_ROWS), cols]


def _sum_terms(terms, ways=4):
    accs = []
    for i, t in enumerate(terms):
        if i < ways:
            accs.append(t)
        else:
            accs[i % ways] = accs[i % ways] + t
    while len(accs) > 1:
        accs = [accs[i] + accs[i + 1] if i + 1 < len(accs) else accs[i] for i in range(0, len(accs), 2)]
    return accs[0]


def _accumulate(ref, val, first):
    @pl.when(first)
    def _():
        ref[...] = val

    @pl.when(jnp.logical_not(first))
    def _():
        ref[...] += val


def _sconv_taps(zext_ref, cw_ref, tm, base):
    out = cw_ref[2:3, :] * zext_ref[pl.ds(base, tm), :]
    out = out + cw_ref[1:2, :] * zext_ref[pl.ds(base - 1, tm), :]
    return out + cw_ref[0:1, :] * zext_ref[pl.ds(base - 2, tm), :]


def _sconv_fill_z(zext_ref, main_ref, before_ref, d, m):
    hb = SCONV_HALO
    zb = before_ref[:, d:2 * d].astype(F32) * before_ref[:, 2 * d:].astype(F32)
    zext_ref[pl.ds(0, hb), :] = jnp.where(m > 0, zb, 0.0)
    zext_ref[pl.ds(hb, main_ref.shape[0]), :] = main_ref[:, d:2 * d].astype(F32) * main_ref[:, 2 * d:].astype(F32)


def _sconv_fwd(bcv, cw, name):
    s, d3 = bcv.shape
    d = d3 // 3
    tm = _tile(s, 512, SCONV_HALO)

    def body(main_ref, before_ref, cw_ref, p_ref, zext_ref):
        m = pl.program_id(0)
        _sconv_fill_z(zext_ref, main_ref, before_ref, d, m)
        zc = _sconv_taps(zext_ref, cw_ref, tm, SCONV_HALO)
        p_ref[...] = (main_ref[:, :d].astype(F32) * zc).astype(p_ref.dtype)

    return _hosted(
        body, name=name, grid=(s // tm,),
        in_specs=[_main_spec(tm, d3), _before_spec(tm, SCONV_HALO, d3), _row_spec(d, SHORT_CONV_W)],
        out_specs=_main_spec(tm, d),
        out_shape=jax.ShapeDtypeStruct((s, d), BF16),
        scratch_shapes=[pltpu.VMEM((tm + SCONV_HALO, d), F32)],
        compiler_params=_params("parallel"),
    )(bcv, bcv, cw)


def _sconv_bwd(dp, bcv, cw, name):
    s, d3 = bcv.shape
    d = d3 // 3
    tm = _tile(s, 512, SCONV_HALO)
    nm = s // tm
    ha = 8

    def body(dp_ref, dpa_ref, main_ref, before_ref, after_ref, cw_ref, o_ref, dcw_ref, zext_ref, dext_ref):
        m = pl.program_id(0)
        _sconv_fill_z(zext_ref, main_ref, before_ref, d, m)
        zc = _sconv_taps(zext_ref, cw_ref, tm, SCONV_HALO)
        dp_t = dp_ref[...]
        o_ref[:, :d] = (dp_t * zc).astype(o_ref.dtype)
        dzc = dp_t * main_ref[:, :d].astype(F32)
        dext_ref[pl.ds(0, tm), :] = dzc
        dza = dpa_ref[...] * after_ref[:, :d].astype(F32)[0:ha]
        dext_ref[pl.ds(tm, ha), :] = jnp.where(m < nm - 1, dza, 0.0)
        dz = cw_ref[2:3, :] * dzc
        dz = dz + cw_ref[1:2, :] * dext_ref[pl.ds(1, tm), :]
        dz = dz + cw_ref[0:1, :] * dext_ref[pl.ds(2, tm), :]
        o_ref[:, d:2 * d] = (dz * main_ref[:, 2 * d:].astype(F32)).astype(o_ref.dtype)
        o_ref[:, 2 * d:] = (dz * main_ref[:, d:2 * d].astype(F32)).astype(o_ref.dtype)

        @pl.when(m == 0)
        def _():
            dcw_ref[...] = jnp.zeros_like(dcw_ref)

        for kk in range(SHORT_CONV_W):
            zs = zext_ref[pl.ds(SCONV_HALO - 2 + kk, tm), :]
            dcw_ref[kk:kk + 1, :] += _colsum(dzc * zs)

    return _hosted(
        body, name=name, grid=(nm,),
        in_specs=[_main_spec(tm, d), _after_spec(tm, ha, d, s), _main_spec(tm, d3), _before_spec(tm, SCONV_HALO, d3),
                  _after_spec(tm, SCONV_HALO, d3, s), _row_spec(d, SHORT_CONV_W)],
        out_specs=[_main_spec(tm, d3), _row_spec(d, 8)],
        out_shape=[jax.ShapeDtypeStruct((s, d3), BF16), jax.ShapeDtypeStruct((8, d), F32)],
        scratch_shapes=[pltpu.VMEM((tm + SCONV_HALO, d), F32), pltpu.VMEM((tm + ha, d), F32)],
        compiler_params=_params("arbitrary"),
    )(dp, dp, bcv, bcv, bcv, cw)


def _pool_counts(t0, tm, w):
    t = t0 + lax.broadcasted_iota(jnp.int32, (tm, 1), 0)
    return jnp.minimum(t + 1, w).astype(F32)


def _pool_fwd(h, gain, wg, scale, name):
    s, d = h.shape
    ng, cg, _ = wg.shape
    tm = _tile(s, 512, POOL_HALO)

    def body(h_ref, hb_ref, g_ref, wg_ref, sc_ref, o_ref, mx_ref, uext_ref):
        m = pl.program_id(0)
        x = h_ref[...]
        gain_row = g_ref[...]
        xb = hb_ref[...]
        uext_ref[pl.ds(0, POOL_HALO), :] = jnp.where(m > 0, xb * _rms_stats(xb) * gain_row, 0.0)
        uext_ref[pl.ds(POOL_HALO, tm), :] = x * _rms_stats(x) * gain_row
        for gi, win in enumerate(POOL_WINDOWS):
            cols = pl.ds(gi * cg, cg)
            u_g = uext_ref[pl.ds(POOL_HALO, tm), cols]
            acc = u_g
            for i in range(1, win):
                acc = acc + uext_ref[pl.ds(POOL_HALO - i, tm), cols]
            mixed = (acc / _pool_counts(m * tm, tm, win) - u_g).astype(BF16)
            mx_ref[:, cols] = mixed
            o_ref[:, cols] = x[:, gi * cg:(gi + 1) * cg] + _dot(mixed, wg_ref[gi]) * sc_ref[:, cols]

    return _hosted(
        body, name=name, grid=(s // tm,),
        in_specs=[_main_spec(tm, d), _before_spec(tm, POOL_HALO, d), _row_spec(d),
                  pl.BlockSpec((ng, cg, cg), lambda m: (0, 0, 0)), _row_spec(d)],
        out_specs=[_main_spec(tm, d), _main_spec(tm, d)],
        out_shape=[jax.ShapeDtypeStruct((s, d), F32), jax.ShapeDtypeStruct((s, d), BF16)],
        scratch_shapes=[pltpu.VMEM((tm + POOL_HALO, d), F32)],
        compiler_params=_params("parallel"),
    )(h, h, gain, wg, scale)


def _pool_bwd_mm(dh, mixed, wg, scale, name):
    s, d = dh.shape
    ng, cg, _ = wg.shape
    tm = _tile(s, 512)

    def body(dh_ref, mx_ref, wg_ref, sc_ref, dmx_ref, dwg_ref, dsc_ref):
        first = pl.program_id(0) == 0
        for gi in range(ng):
            cols = pl.ds(gi * cg, cg)
            dh_g = dh_ref[:, cols]
            mixed = mx_ref[:, cols]
            w_g = wg_ref[gi]
            dy = (dh_g * sc_ref[:, cols]).astype(BF16)
            dmx_ref[:, cols] = _dot_nt(dy, w_g)
            _accumulate(dsc_ref.at[:, cols], _colsum(dh_g * _dot(mixed, w_g)), first)
            _accumulate(dwg_ref.at[gi], _dot_tn(mixed, dy), first)

    return _hosted(
        body, name=name, grid=(s // tm,),
        in_specs=[_main_spec(tm, d), _main_spec(tm, d), pl.BlockSpec((ng, cg, cg), lambda m: (0, 0, 0)), _row_spec(d)],
        out_specs=[_main_spec(tm, d), pl.BlockSpec((ng, cg, cg), lambda m: (0, 0, 0)), _row_spec(d)],
        out_shape=[jax.ShapeDtypeStruct((s, d), F32), jax.ShapeDtypeStruct((ng, cg, cg), F32),
                   jax.ShapeDtypeStruct((1, d), F32)],
        compiler_params=_params("arbitrary"),
    )(dh, mixed, wg, scale)


def _pool_bwd_rms(dmixed, h, gain, dh, name):
    s, d = h.shape
    cg = d // len(POOL_WINDOWS)
    tm = _tile(s, 512, POOL_HALO)
    nm = s // tm

    def body(dmx_ref, dmxa_ref, h_ref, g_ref, dh_ref, o_ref, dg_ref, eext_ref, du_ref):
        m = pl.program_id(0)
        for gi, win in enumerate(POOL_WINDOWS):
            cols = pl.ds(gi * cg, cg)
            dmx = dmx_ref[:, cols]
            eext_ref[pl.ds(0, tm), cols] = dmx / _pool_counts(m * tm, tm, win)
            ea = dmxa_ref[:, cols] / _pool_counts((m + 1) * tm, POOL_HALO, win)
            eext_ref[pl.ds(tm, POOL_HALO), cols] = jnp.where(m < nm - 1, ea, 0.0)
            acc = -dmx
            for i in range(win):
                acc = acc + eext_ref[pl.ds(i, tm), cols]
            du_ref[:, cols] = acc
        dx, dg = _rms_bwd(du_ref[...], h_ref[...], g_ref[...])
        o_ref[...] = dh_ref[...] + dx
        _accumulate(dg_ref, dg, m == 0)

    return _hosted(
        body, name=name, grid=(nm,),
        in_specs=[_main_spec(tm, d), _after_spec(tm, POOL_HALO, d, s), _main_spec(tm, d), _row_spec(d), _main_spec(tm, d)],
        out_specs=[_main_spec(tm, d), _row_spec(d)],
        out_shape=[jax.ShapeDtypeStruct((s, d), F32), jax.ShapeDtypeStruct((1, d), F32)],
        scratch_shapes=[pltpu.VMEM((tm + POOL_HALO, d), F32), pltpu.VMEM((tm, d), F32)],
        compiler_params=_params("arbitrary"),
    )(dmixed, dmixed, h, gain, dh)


def _conf_fill_h(hext_ref, main_ref, before_ref, d, m):
    hb = before_ref[:, :d].astype(F32) * _sigmoid(before_ref[:, d:].astype(F32))
    hext_ref[pl.ds(0, CONF_HALO), :] = jnp.where(m > 0, hb, 0.0)
    hext_ref[pl.ds(CONF_HALO, main_ref.shape[0]), :] = main_ref[:, :d].astype(F32) * _sigmoid(main_ref[:, d:].astype(F32))


def _layernorm_parts(hc, g, b):
    mu = jnp.mean(hc, axis=-1, keepdims=True)
    xc = hc - mu
    rs = lax.rsqrt(jnp.mean(xc * xc, axis=-1, keepdims=True) + LN_EPS)
    xhat = xc * rs
    return xhat, rs, xhat * g + b


def _conf_mid_fwd(ag, dw, b_dw, ln_g, ln_b, name):
    s, d2 = ag.shape
    d = d2 // 2
    tm = _tile(s, 256, CONF_HALO)
    base = CONF_HALO - (CONF_CONV_W - 1)

    def body(main_ref, before_ref, dw_ref, bdw_ref, g_ref, b_ref, s_ref, hc_ref, hext_ref):
        m = pl.program_id(0)
        _conf_fill_h(hext_ref.at[0], main_ref, before_ref, d, m)
        _build_shifts(hext_ref)
        for cols, r0 in _chunks(tm, d):
            taps = (dw_ref[kk:kk + 1, cols] * _shifted(hext_ref, base + kk + r0, cols) for kk in range(CONF_CONV_W))
            hc_ref[pl.ds(r0, CHUNK_ROWS), cols] = bdw_ref[:, cols] + _sum_terms(taps)
        _, _, l = _layernorm_parts(hc_ref[...], g_ref[...], b_ref[...])
        s_ref[...] = (l * _sigmoid(l)).astype(s_ref.dtype)

    return _hosted(
        body, name=name, grid=(s // tm,),
        in_specs=[_main_spec(tm, d2), _before_spec(tm, CONF_HALO, d2), _row_spec(d, CONF_CONV_W), _row_spec(d),
                  _row_spec(d), _row_spec(d)],
        out_specs=[_main_spec(tm, d), _main_spec(tm, d)],
        out_shape=[jax.ShapeDtypeStruct((s, d), BF16), jax.ShapeDtypeStruct((s, d), F32)],
        scratch_shapes=[pltpu.VMEM((8, tm + CONF_HALO, d), F32)],
        compiler_params=_params("parallel"),
    )(ag, ag, dw, b_dw, ln_g, ln_b)


def _conf_out_bwd(dh, w, hc, ln_g, ln_b, name):
    s, d = dh.shape
    tm = _tile(s, 256)

    def body(dh_ref, w_ref, hc_ref, g_ref, b_ref, o_ref, dg_ref, db_ref, dbo_ref):
        first = pl.program_id(0) == 0
        dh_t = dh_ref[...]
        ds = _dot_nt(dh_t.astype(BF16), w_ref[...])
        xhat, rs, l = _layernorm_parts(hc_ref[...], g_ref[...], b_ref[...])
        sg = _sigmoid(l)
        dl = ds * sg * (1.0 + l * (1.0 - sg))
        dxh = dl * g_ref[...]
        o_ref[...] = rs * (dxh - jnp.mean(dxh, axis=-1, keepdims=True)
                           - xhat * jnp.mean(dxh * xhat, axis=-1, keepdims=True))
        _accumulate(dg_ref, _colsum(dl * xhat), first)
        _accumulate(db_ref, _colsum(dl), first)
        _accumulate(dbo_ref, _colsum(dh_t), first)

    return _hosted(
        body, name=name, grid=(s // tm,),
        in_specs=[_main_spec(tm, d), pl.BlockSpec((d, d), lambda m: (0, 0)), _main_spec(tm, d), _row_spec(d), _row_spec(d)],
        out_specs=[_main_spec(tm, d), _row_spec(d), _row_spec(d), _row_spec(d)],
        out_shape=[jax.ShapeDtypeStruct((s, d), F32)] + [jax.ShapeDtypeStruct((1, d), F32)] * 3,
        compiler_params=_params("arbitrary"),
    )(dh, w, hc, ln_g, ln_b)


def _conf_mid_bwd(dhc, ag, dw, name):
    s, d2 = ag.shape
    d = d2 // 2
    tm = _tile(s, 256, CONF_HALO)
    nm = s // tm
    kw = CONF_CONV_W
    base = CONF_HALO - (kw - 1)

    def body(dhc_ref, dhca_ref, main_ref, before_ref, dw_ref, o_ref, ddw_ref, dbdw_ref, dbpw_ref, hext_ref, dext_ref):
        m = pl.program_id(0)
        first = m == 0
        _conf_fill_h(hext_ref.at[0], main_ref, before_ref, d, m)
        _build_shifts(hext_ref)
        dext_ref[0, pl.ds(0, tm), :] = dhc_ref[...]
        dext_ref[0, pl.ds(tm, CONF_HALO), :] = jnp.where(m < nm - 1, dhca_ref[...], 0.0)
        _build_shifts(dext_ref)

        @pl.when(first)
        def _():
            ddw_ref[...] = jnp.zeros_like(ddw_ref)
            dbdw_ref[...] = jnp.zeros_like(dbdw_ref)
            dbpw_ref[...] = jnp.zeros_like(dbpw_ref)

        fold = _fold_rows
        for c0 in range(0, d, LANES):
            cols, gate_cols = pl.ds(c0, LANES), pl.ds(d + c0, LANES)
            sum_da = sum_dgate = sum_dhc = jnp.zeros((8, LANES), F32)
            for r0 in range(0, tm, CHUNK_ROWS):
                rows = pl.ds(r0, CHUNK_ROWS)
                dhc_c = dext_ref[0, rows, cols]
                dhh = _sum_terms(dw_ref[kk:kk + 1, cols] * _shifted(dext_ref, r0 + kw - 1 - kk, cols) for kk in range(kw))
                a = main_ref[rows, cols].astype(F32)
                sg = _sigmoid(main_ref[rows, gate_cols].astype(F32))
                da = dhh * sg
                dgate = dhh * a * sg * (1.0 - sg)
                o_ref[rows, cols] = da.astype(o_ref.dtype)
                o_ref[rows, gate_cols] = dgate.astype(o_ref.dtype)
                sum_da, sum_dgate, sum_dhc = sum_da + fold(da), sum_dgate + fold(dgate), sum_dhc + fold(dhc_c)
            dbdw_ref[:, cols] += _colsum(sum_dhc)
            dbpw_ref[:, cols] += _colsum(sum_da)
            dbpw_ref[:, gate_cols] += _colsum(sum_dgate)
            for kk in range(kw):
                acc = _sum_terms(fold(dext_ref[0, pl.ds(r0, CHUNK_ROWS), cols] * _shifted(hext_ref, base + kk + r0, cols))
                                 for r0 in range(0, tm, CHUNK_ROWS))
                ddw_ref[kk:kk + 1, cols] += _colsum(acc)

    return _hosted(
        body, name=name, grid=(nm,),
        in_specs=[_main_spec(tm, d), _after_spec(tm, CONF_HALO, d, s), _main_spec(tm, d2), _before_spec(tm, CONF_HALO, d2),
                  _row_spec(d, kw)],
        out_specs=[_main_spec(tm, d2), _row_spec(d, 32), _row_spec(d), _row_spec(d2)],
        out_shape=[jax.ShapeDtypeStruct((s, d2), BF16), jax.ShapeDtypeStruct((32, d), F32),
                   jax.ShapeDtypeStruct((1, d), F32), jax.ShapeDtypeStruct((1, d2), F32)],
        scratch_shapes=[pltpu.VMEM((8, tm + CONF_HALO, d), F32), pltpu.VMEM((8, tm + CONF_HALO, d), F32)],
        compiler_params=_params("arbitrary"),
    )(dhc, dhc, ag, ag, dw)


def _loss_head(h, gain, target, name):
    s, d = h.shape
    tm = _tile(s, 512)

    def body(h_ref, g_ref, t_ref, loss_ref, dh_ref, dg_ref):
        first = pl.program_id(0) == 0
        x = h_ref[...]
        err = x * _rms_stats(x) * g_ref[...] - t_ref[...]
        part = 0.5 * jnp.sum(jnp.mean(err * err, axis=-1, keepdims=True), axis=0, keepdims=True)
        dx, dg = _rms_bwd(err * (1.0 / d), x, g_ref[...])
        dh_ref[...] = dx
        _accumulate(loss_ref, part, first)
        _accumulate(dg_ref, dg, first)

    return _hosted(
        body, name=name, grid=(s // tm,),
        in_specs=[_main_spec(tm, d), _row_spec(d), _main_spec(tm, d)],
        out_specs=[pl.BlockSpec((1, 1), lambda m: (0, 0)), _main_spec(tm, d), _row_spec(d)],
        out_shape=[jax.ShapeDtypeStruct((1, 1), F32), jax.ShapeDtypeStruct((s, d), F32), jax.ShapeDtypeStruct((1, d), F32)],
        compiler_params=_params("arbitrary"),
    )(h, gain, target)


def _ffn_fwd(h, gain, w_gu, w_down, i):
    u = _rms_fwd(h, gain, f"ffn{i}_rms")
    act, s1, q1 = _ffn_up(u, w_gu, f"ffn{i}_up")
    h_new = _mm_row(act, w_down, h, None, f"ffn{i}_down")
    return h_new, (h, u, act, s1, q1)


def _ffn_bwd(dh, saved, gain, w_gu, w_down, i):
    h, u, act, s1, q1 = saved
    dgu = _ffn_down_bwd(dh, w_down, s1, q1, f"ffn{i}_down_bwd")
    dw_down = _mm_tn(act, dh, 1, f"ffn{i}_dw_down")
    dw_gu = _mm_tn(u, dgu, N_CHIPS, f"ffn{i}_dw_gu")
    dh_new, dgain = _mm_nt_col_rms_bwd(dgu, w_gu, h, gain, dh, f"ffn{i}_up_bwd")
    return dh_new, dgain, dw_gu, dw_down


def _device_step(x, target, wts, g=None):
    g = {} if g is None else g
    saved = {}
    h = x

    def short_conv_fwd(h, i):
        u = _rms_fwd(h, wts[f"ln1_{i}"], f"a{i}_rms")
        bcv = _mm_col(u, wts[f"a{i}_w_in"], None, f"a{i}_in")
        p = _sconv_fwd(bcv, wts[f"a{i}_conv"], f"a{i}_conv")
        return _mm_row(p, wts[f"a{i}_w_out"], h, None, f"a{i}_out"), (h, u, bcv, p)

    def short_conv_bwd(dh, sv, i):
        h, u, bcv, p = sv
        dp = _mm_nt_row(dh, wts[f"a{i}_w_out"], f"a{i}_out_bwd")
        g[f"a{i}_w_out"] = _mm_tn(p, dh, 1, f"a{i}_dw_out")
        dbcv, dcw = _sconv_bwd(dp, bcv, wts[f"a{i}_conv"], f"a{i}_conv_bwd")
        g[f"a{i}_conv"] = dcw[:SHORT_CONV_W]
        g[f"a{i}_w_in"] = _mm_tn(u, dbcv, N_CHIPS, f"a{i}_dw_in")
        dh, g[f"ln1_{i}"] = _mm_nt_col_rms_bwd(dbcv, wts[f"a{i}_w_in"], h, wts[f"ln1_{i}"], dh, f"a{i}_in_bwd")
        return dh

    h, saved["a0"] = short_conv_fwd(h, 0)
    h, saved["f0"] = _ffn_fwd(h, wts["ln2_0"], wts["ffn0_w_gu"], wts["ffn0_w_down"], 0)

    h_in = h
    h, mixed = _pool_fwd(h, wts["ln1_1"], wts["b1_w_grp"], wts["b1_scale"], "b1_fwd")
    saved["b1"] = (h_in, mixed)
    h, saved["f1"] = _ffn_fwd(h, wts["ln2_1"], wts["ffn1_w_gu"], wts["ffn1_w_down"], 1)

    h_in = h
    u = _rms_fwd(h, wts["ln1_2"], "c2_rms")
    ag = _mm_col(u, wts["c2_w_pw1"], wts["c2_b_pw1"], "c2_pw1")
    sw, hc = _conf_mid_fwd(ag, wts["c2_dw"], wts["c2_b_dw"], wts["c2_ln_g"], wts["c2_ln_b"], "c2_mid")
    h = _mm_row(sw, wts["c2_w_pw2"], h, wts["c2_b_pw2"], "c2_pw2")
    saved["c2"] = (h_in, u, ag, sw, hc)
    h, saved["f2"] = _ffn_fwd(h, wts["ln2_2"], wts["ffn2_w_gu"], wts["ffn2_w_down"], 2)

    h, saved["a3"] = short_conv_fwd(h, 3)
    h, saved["f3"] = _ffn_fwd(h, wts["ln2_3"], wts["ffn3_w_gu"], wts["ffn3_w_down"], 3)

    loss, dh, g["ln_f"] = _loss_head(h, wts["ln_f"], target, "loss_head")

    def ffn_bwd(dh, i):
        dh, g[f"ln2_{i}"], g[f"ffn{i}_w_gu"], g[f"ffn{i}_w_down"] = _ffn_bwd(
            dh, saved[f"f{i}"], wts[f"ln2_{i}"], wts[f"ffn{i}_w_gu"], wts[f"ffn{i}_w_down"], i)
        return dh

    dh = ffn_bwd(dh, 3)
    dh = short_conv_bwd(dh, saved["a3"], 3)

    dh = ffn_bwd(dh, 2)
    h_in, u, ag, sw, hc = saved["c2"]
    dhc, g["c2_ln_g"], g["c2_ln_b"], g["c2_b_pw2"] = _conf_out_bwd(
        dh, wts["c2_w_pw2"], hc, wts["c2_ln_g"], wts["c2_ln_b"], "c2_pw2_bwd")
    g["c2_w_pw2"] = _mm_tn(sw, dh, 1, "c2_dw_pw2")
    dag, ddw, g["c2_b_dw"], g["c2_b_pw1"] = _conf_mid_bwd(dhc, ag, wts["c2_dw"], "c2_mid_bwd")
    g["c2_dw"] = ddw[:CONF_CONV_W]
    g["c2_w_pw1"] = _mm_tn(u, dag, N_CHIPS, "c2_dw_pw1")
    dh, g["ln1_2"] = _mm_nt_col_rms_bwd(dag, wts["c2_w_pw1"], h_in, wts["ln1_2"], dh, "c2_pw1_bwd")

    dh = ffn_bwd(dh, 1)
    h_in, mixed = saved["b1"]
    dmixed, g["b1_w_grp"], g["b1_scale"] = _pool_bwd_mm(dh, mixed, wts["b1_w_grp"], wts["b1_scale"], "b1_bwd_mm")
    dh, g["ln1_1"] = _pool_bwd_rms(dmixed, h_in, wts["ln1_1"], dh, "b1_bwd_rms")

    dh = ffn_bwd(dh, 0)
    dh = short_conv_bwd(dh, saved["a0"], 0)
    return loss, dh, g


MESH = pl.DeviceIdType.MESH
ANY = pl.BlockSpec(memory_space=pl.ANY)


def _position():
    return lax.axis_index("x"), lax.axis_index("y"), lax.axis_index("c")


def _other_chips(x, y):
    return [(1 - x, y), (x, 1 - y), (1 - x, 1 - y)]


def _remote(src, dst, send_sem, recv_sem, to):
    return pltpu.make_async_remote_copy(src_ref=src, dst_ref=dst, send_sem=send_sem, recv_sem=recv_sem,
                                        device_id=to, device_id_type=MESH)


def _half_rows(ref_rows, c):
    hr = ref_rows // 2
    return pl.ds(pl.multiple_of(c * hr, 16), hr)


def _allgather8(v, name):
    m_per, n = v.shape

    def body(v_ref, out_ref, send_sems, recv_sems, local_sem):
        x, y, c = _position()
        me, sibling = (x, y, c), (x, y, 1 - c)
        chips = _other_chips(x, y)

        def rows(px, py, pc):
            return out_ref.at[pl.ds((4 * px + 2 * py + pc) * m_per, m_per), :]

        def copy(k, block, to, src=None):
            return _remote(rows(*block) if src is None else src, rows(*block), send_sems.at[k], recv_sems.at[k], to)

        mine = pltpu.make_async_copy(v_ref, rows(*me), local_sem)
        mine.start()
        first = [copy(0, me, sibling, src=v_ref)]
        first += [copy(1 + j, me, (*chip, c), src=v_ref) for j, chip in enumerate(chips)]
        for cp in first:
            cp.start()
        passed = [copy(4 + j, (*chip, c), sibling) for j, chip in enumerate(chips)]
        for j, chip in enumerate(chips):
            copy(1 + j, (*chip, c), me).wait_recv()
            passed[j].start()
        copy(0, sibling, me).wait_recv()
        for j, chip in enumerate(chips):
            copy(4 + j, (*chip, 1 - c), me).wait_recv()
        for cp in first + passed:
            cp.wait_send()
        mine.wait()

    return _hosted(
        body, name=name,
        out_shape=jax.ShapeDtypeStruct((N_DEV * m_per, n), v.dtype),
        in_specs=[pl.BlockSpec(memory_space=pltpu.VMEM)],
        out_specs=pl.BlockSpec(memory_space=pltpu.VMEM),
        scratch_shapes=[pltpu.SemaphoreType.DMA((7,)), pltpu.SemaphoreType.DMA((7,)), pltpu.SemaphoreType.DMA],
        compiler_params=pltpu.CompilerParams(vmem_limit_bytes=VMEM_LIMIT),
    )(v)


def _cast_to_slot(w, idx, name):
    r, cols = w.shape
    tr = _tile(r, 256, 16)

    def body(idx_ref, w_ref, o_ref):
        o_ref[...] = w_ref[...].astype(o_ref.dtype)

    return _hosted(
        body, name=name,
        grid_spec=pltpu.PrefetchScalarGridSpec(
            num_scalar_prefetch=1, grid=(r // tr,),
            in_specs=[pl.BlockSpec((tr, cols), lambda t, idx_ref: (t, 0))],
            out_specs=pl.BlockSpec((None, tr, cols), lambda t, idx_ref: (idx_ref[0], t, 0))),
        out_shape=jax.ShapeDtypeStruct((N_CHIPS, r, cols), BF16),
        compiler_params=_params("parallel"),
    )(idx, w)


def _dma_sems(*shape):
    return [pltpu.SemaphoreType.DMA(shape), pltpu.SemaphoreType.DMA(shape)]


def _same_shapes(arrays):
    return [jax.ShapeDtypeStruct(a.shape, a.dtype) for a in arrays]


def _task_gather_ici(bufs, done):
    n = len(bufs)

    def copies(outs, sems, landing):
        x, y, c = _position()
        my_chip = 2 * x + y
        res = []
        for i in range(n):
            rows = _half_rows(bufs[i].shape[1], c)
            for r, (px, py) in enumerate(_other_chips(x, y)):
                slot = (2 * px + py) if landing else my_chip
                res.append(_remote(outs[i].at[my_chip, rows, :], outs[i].at[slot, rows, :], sems[0].at[i, r], sems[1].at[i, r],
                                   (px, py, c)))
        return res

    def start(ins, outs, sems):
        for cp in copies(outs, sems, False):
            cp.start()

    def wait(ins, outs, sems):
        for cp in copies(outs, sems, True):
            cp.wait_recv()
            cp.wait_send()

    return _Task(bufs, _same_shapes(bufs), {i: i for i in range(n)}, _dma_sems(n, 3), start, wait, done)


def _task_gather_d2d(bufs, done):
    n = len(bufs)

    def copies(outs, sems, landing):
        x, y, c = _position()
        res = []
        for i in range(n):
            rows = _half_rows(bufs[i].shape[1], (1 - c) if landing else c)
            for r, (px, py) in enumerate(_other_chips(x, y)):
                part = outs[i].at[2 * px + py, rows, :]
                res.append(_remote(part, part, sems[0].at[i, r], sems[1].at[i, r], (x, y, 1 - c)))
        return res

    def start(ins, outs, sems):
        for cp in copies(outs, sems, False):
            cp.start()

    def wait(ins, outs, sems):
        for cp in copies(outs, sems, True):
            cp.wait_recv()
        for cp in copies(outs, sems, False):
            cp.wait_send()

    return _Task(bufs, _same_shapes(bufs), {i: i for i in range(n)}, _dma_sems(n, 3), start, wait, done)


def _task_sibling_halves(grads, done):
    n = len(grads)

    def copies(ins, outs, sems):
        x, y, c = _position()
        return [_remote(ins[i].at[:, _half_rows(grads[i].shape[1], 1 - c), :], outs[i], sems[0].at[i], sems[1].at[i],
                        (x, y, 1 - c)) for i in range(n)]

    def start(ins, outs, sems):
        for cp in copies(ins, outs, sems):
            cp.start()

    def wait(ins, outs, sems):
        for cp in copies(ins, outs, sems):
            cp.wait()

    shapes = [jax.ShapeDtypeStruct((g.shape[0], g.shape[1] // 2, g.shape[2]), g.dtype) for g in grads]
    return _Task(grads, shapes, {}, _dma_sems(n), start, wait, done)


def _task_chip_sums(parts, done):
    n = len(parts)

    def copies(ins, outs, sems):
        x, y, c = _position()
        return [_remote(ins[i].at[2 * px + py], outs[i].at[r], sems[0].at[i, r], sems[1].at[i, r], (px, py, c))
                for i in range(n) for r, (px, py) in enumerate(_other_chips(x, y))]

    def start(ins, outs, sems):
        for cp in copies(ins, outs, sems):
            cp.start()

    def wait(ins, outs, sems):
        for cp in copies(ins, outs, sems):
            cp.wait()

    shapes = [jax.ShapeDtypeStruct((3,) + p.shape[1:], p.dtype) for p in parts]
    return _Task(parts, shapes, {}, _dma_sems(n, 3), start, wait, done)


def _task_reduced_halves(fulls, done):
    n = len(fulls)

    def copies(outs, sems, landing):
        x, y, c = _position()
        res = []
        for i in range(n):
            rows = _half_rows(fulls[i].shape[0], (1 - c) if landing else c)
            res.append(_remote(outs[i].at[rows, :], outs[i].at[rows, :], sems[0].at[i], sems[1].at[i], (x, y, 1 - c)))
        return res

    def start(ins, outs, sems):
        for cp in copies(outs, sems, False):
            cp.start()

    def wait(ins, outs, sems):
        for cp in copies(outs, sems, True):
            cp.wait_recv()
        for cp in copies(outs, sems, False):
            cp.wait_send()

    return _Task(fulls, _same_shapes(fulls), {i: i for i in range(n)}, _dma_sems(n), start, wait, done)


def _add_halves(grad, sib, c, name):
    nsh, r, cols = grad.shape
    hr = r // 2
    tr = _tile(hr, 512, 16)
    nt = hr // tr

    def body(c_ref, g_ref, s_ref, o_ref):
        o_ref[...] = (g_ref[...].astype(F32) + s_ref[...].astype(F32)).astype(o_ref.dtype)

    return _hosted(
        body, name=name,
        grid_spec=pltpu.PrefetchScalarGridSpec(
            num_scalar_prefetch=1, grid=(nsh, nt),
            in_specs=[pl.BlockSpec((None, tr, cols), lambda j, t, c_ref: (j, c_ref[1] * nt + t, 0)),
                      pl.BlockSpec((None, tr, cols), lambda j, t, c_ref: (j, t, 0))],
            out_specs=pl.BlockSpec((None, tr, cols), lambda j, t, c_ref: (j, t, 0))),
        out_shape=jax.ShapeDtypeStruct((nsh, hr, cols), BF16),
        compiler_params=_params("parallel", "parallel"),
    )(c, grad, sib)


def _sum_chips(own, landed, idx, name):
    nsh, hr, cols = own.shape
    tr = _tile(hr, 512, 16)
    nt = hr // tr

    def body(idx_ref, p_ref, l_ref, o_ref):
        acc = p_ref[...].astype(F32)
        for k in range(landed.shape[0]):
            acc = acc + l_ref[k].astype(F32)
        o_ref[...] = acc

    return _hosted(
        body, name=name,
        grid_spec=pltpu.PrefetchScalarGridSpec(
            num_scalar_prefetch=1, grid=(nt,),
            in_specs=[pl.BlockSpec((None, tr, cols), lambda t, idx_ref: (idx_ref[0], t, 0)),
                      pl.BlockSpec((landed.shape[0], tr, cols), lambda t, idx_ref: (0, t, 0))],
            out_specs=pl.BlockSpec((tr, cols), lambda t, idx_ref: (idx_ref[1] * nt + t, 0))),
        out_shape=jax.ShapeDtypeStruct((2 * hr, cols), F32),
        compiler_params=_params("parallel"),
    )(idx, own, landed)


def _sum_devices(blocks, name):
    m8, n = blocks.shape
    m = m8 // N_DEV

    def body(b_ref, o_ref):
        acc = b_ref[pl.ds(0, m), :]
        for k in range(1, N_DEV):
            acc = acc + b_ref[pl.ds(k * m, m), :]
        o_ref[...] = acc

    return _hosted(
        body, name=name, out_shape=jax.ShapeDtypeStruct((m, n), F32),
        in_specs=[pl.BlockSpec(memory_space=pltpu.VMEM)], out_specs=pl.BlockSpec(memory_space=pltpu.VMEM),
        compiler_params=pltpu.CompilerParams(vmem_limit_bytes=VMEM_LIMIT),
    )(blocks)


def _adamw(w, g, m, v, name):
    r, cols = w.shape
    tr = _tile(r, 256) if r % 8 == 0 else r
    c1 = 1.0 / (1.0 - ADAM_B1 ** ADAM_STEP)
    c2 = 1.0 / (1.0 - ADAM_B2 ** ADAM_STEP)

    def body(w_ref, g_ref, m_ref, v_ref, go_ref, d_ref, mo_ref, vo_ref):
        grad = g_ref[...]
        new_m = ADAM_B1 * m_ref[...] + (1.0 - ADAM_B1) * grad
        new_v = ADAM_B2 * v_ref[...] + (1.0 - ADAM_B2) * (grad * grad)
        go_ref[...] = grad
        mo_ref[...] = new_m
        vo_ref[...] = new_v
        d_ref[...] = -ADAM_LR * ((new_m * c1) / (jnp.sqrt(new_v * c2) + ADAM_EPS) + ADAM_WD * w_ref[...])

    spec = pl.BlockSpec((tr, cols), lambda t: (t, 0))
    return _hosted(
        body, name=name, grid=(r // tr,), in_specs=[spec] * 4, out_specs=[spec] * 4,
        out_shape=[jax.ShapeDtypeStruct((r, cols), F32)] * 4,
        compiler_params=_params("parallel"),
    )(w, g, m, v)


WEIGHT_NAMES = (
    "ln1_0", "a0_w_in", "a0_conv", "a0_w_out", "ln2_0", "ffn0_w_gu", "ffn0_w_down",
    "ln1_1", "b1_w_grp", "b1_scale", "ln2_1", "ffn1_w_gu", "ffn1_w_down",
    "ln1_2", "c2_w_pw1", "c2_b_pw1", "c2_dw", "c2_b_dw", "c2_ln_g", "c2_ln_b", "c2_w_pw2", "c2_b_pw2",
    "ln2_2", "ffn2_w_gu", "ffn2_w_down",
    "ln1_3", "a3_w_in", "a3_conv", "a3_w_out", "ln2_3", "ffn3_w_gu", "ffn3_w_down", "ln_f")
BIG = ("a0_w_in", "a0_w_out", "ffn0_w_gu", "ffn0_w_down", "ffn1_w_gu", "ffn1_w_down", "c2_w_pw1", "c2_w_pw2",
       "ffn2_w_gu", "ffn2_w_down", "a3_w_in", "a3_w_out", "ffn3_w_gu", "ffn3_w_down")
SMALL_SHARDED = ("a0_conv", "a3_conv", "c2_dw", "b1_w_grp")
REPLICATED = tuple(n for n in WEIGHT_NAMES if n not in BIG and n not in SMALL_SHARDED)


def _pad_rows(a, mult=8):
    pad = -a.shape[0] % mult
    return a if pad == 0 else jnp.concatenate([a, jnp.zeros((pad, a.shape[1]), a.dtype)], axis=0)


def _pack_rows(parts, width):
    rows = [p.reshape(-1, width) for p in parts]
    return _pad_rows(jnp.concatenate(rows, axis=0)), [r.shape[0] for r in rows]


def _unpack_rows(packed, counts, shapes):
    out, at = [], 0
    for n, shp in zip(counts, shapes):
        out.append(packed[at:at + n].reshape(shp))
        at += n
    return out


COLUMN_SHARDED = ("w_in", "w_gu", "w_pw1")


class _Weights(dict):
    def __init__(self, bufs):
        super().__init__()
        self.bufs = bufs

    def __missing__(self, name):
        buf = self.bufs[name]
        return buf if name.endswith(COLUMN_SHARDED) else buf.reshape(-1, buf.shape[-1])


class _Exchange:
    def __init__(self, w, mom, vel, idx):
        self.w, self.mom, self.vel, self.idx = w, mom, vel, idx
        self.bufs = {}
        self.weights = _Weights(self.bufs)
        self.grads = {}
        self.sib, self.part, self.landed, self.full, self.updates = {}, {}, {}, {}, {}

    def cast(self, n):
        self.bufs[n] = _cast_to_slot(self.w[n], self.idx, f"cast_{n}")

    @staticmethod
    def _store(table, names):
        def done(arrays):
            table.update(zip(names, arrays))
        return done

    def _grad(self, n):
        g = self.grads[n]
        return g.reshape(N_CHIPS, -1, g.shape[-1])

    def gather_ici(self, *names):
        return lambda: _task_gather_ici([self.bufs[n] for n in names], self._store(self.bufs, names))

    def gather_d2d(self, *names):
        return lambda: _task_gather_d2d([self.bufs[n] for n in names], self._store(self.bufs, names))

    def sibling_halves(self, *names):
        return lambda: _task_sibling_halves([self._grad(n) for n in names], self._store(self.sib, names))

    def add_halves(self, *names):
        def run():
            for n in names:
                self.part[n] = _add_halves(self._grad(n), self.sib.pop(n), self.idx, f"reduce_add_{n}")
        return run

    def chip_sums(self, *names):
        return lambda: _task_chip_sums([self.part[n] for n in names], self._store(self.landed, names))

    def sum_chips(self, *names):
        def run():
            for n in names:
                self.full[n] = _sum_chips(self.part.pop(n), self.landed.pop(n), self.idx, f"reduce_sum_{n}")
        return run

    def reduced_halves(self, *names):
        return lambda: _task_reduced_halves([self.full[n] for n in names], self._store(self.full, names))

    def adamw(self, *names):
        def run():
            for n in names:
                self.updates[n] = _adamw(self.w[n], self.full.pop(n), self.mom[n], self.vel[n], f"adamw_{n}")
        return run


def _plan(ex):
    s = _Schedule()

    def ffn(i):
        return f"ffn{i}_w_gu", f"ffn{i}_w_down"

    c2, a3 = ("c2_w_pw1", "c2_w_pw2"), ("a3_w_in", "a3_w_out")
    for i in (0, 3):
        gu, down = ffn(i)
        s.host(f"a{i}_in", ex.gather_ici(gu))
        s.host(f"a{i}_conv", ex.gather_ici(down), ex.gather_d2d(gu))
        s.host(f"a{i}_out", ex.gather_d2d(down))
    gu, down = ffn(1)
    s.host("ffn0_up", ex.gather_ici(gu))
    s.host("ffn0_down", ex.gather_ici(down), ex.gather_d2d(gu))
    s.host("b1_fwd", ex.gather_d2d(down))
    s.host("ffn1_up", ex.gather_ici(*c2))
    s.host("ffn1_down", ex.gather_d2d(*c2))
    s.host("c2_mid", ex.gather_ici(*ffn(2)))
    s.host("c2_pw2", ex.gather_d2d(*ffn(2)))
    s.host("ffn2_up", ex.gather_ici(*a3))
    s.host("ffn2_down", ex.gather_d2d(*a3))

    def reduce_on(names, first, ici_hosts, last):
        s.host(first, ex.sibling_halves(*names))
        s.post(first, ex.add_halves(*names))
        for host, hosted in ici_hosts:
            s.host(host, ex.chip_sums(*hosted))
        s.post(ici_hosts[-1][0], ex.sum_chips(*names))
        if last is not None:
            s.host(last, ex.reduced_halves(*names))
            s.post(last, ex.adamw(*names))

    gu, down = ffn(3)
    reduce_on((gu, down), "a3_out_bwd", [("a3_conv_bwd", (down,)), ("a3_in_bwd", (gu,))], "ffn2_down_bwd")
    reduce_on(a3, "ffn2_down_bwd", [("ffn2_dw_down", a3)], "ffn2_dw_gu")
    reduce_on(ffn(2), "c2_pw2_bwd", [("c2_mid_bwd", ffn(2))], "c2_pw1_bwd")
    reduce_on(c2, "ffn1_down_bwd", [("ffn1_dw_down", c2)], "ffn1_dw_gu")
    gu, down = ffn(1)
    reduce_on((gu, down), "b1_bwd_mm", [("b1_bwd_rms", (down,)), ("ffn0_down_bwd", (gu,))], "ffn0_dw_down")
    gu, down = ffn(0)
    reduce_on((gu, down), "a0_out_bwd", [("a0_conv_bwd", (down,)), ("a0_in_bwd", (gu,))], None)
    return s


def kernel(x, *rest):
    nw = len(WEIGHT_NAMES)
    w = dict(zip(WEIGHT_NAMES, rest[:nw]))
    target = rest[nw]
    mom = dict(zip(WEIGHT_NAMES, rest[nw + 1:2 * nw + 1]))
    vel = dict(zip(WEIGHT_NAMES, rest[2 * nw + 1:3 * nw + 1]))
    cx, cy, cc = _position()
    my_chip = 2 * cx + cy
    d = x.shape[-1]
    cq = d // N_CHIPS

    ex = _Exchange(w, mom, vel, jnp.stack([my_chip, cc]).astype(jnp.int32))
    sched = _plan(ex)
    for n in BIG:
        ex.cast(n)
    first_layer = ("a0_w_in", "a0_w_out")
    _comm_only([ex.gather_ici(*first_layer)()], "gather_a0_ici")
    _comm_only([ex.gather_d2d(*first_layer)()], "gather_a0_d2d")

    small_blk, small_counts = _pack_rows([w[n] for n in SMALL_SHARDED], cq)
    small_all = _allgather8(small_blk, "gather_small").reshape(N_CHIPS, 2, small_blk.shape[0], cq)[:, 0]
    small_parts = _unpack_rows(jnp.transpose(small_all, (1, 0, 2)), small_counts,
                               [(w[n].reshape(-1, cq).shape[0], N_CHIPS, cq) for n in SMALL_SHARDED])
    wts = ex.weights
    for n in REPLICATED:
        wts[n] = w[n].reshape(1, -1)
    for n, part in zip(SMALL_SHARDED, small_parts):
        if n == "b1_w_grp":
            ng, rq, cg = w[n].shape
            full = jnp.transpose(part.reshape(ng, rq, N_CHIPS, cg), (0, 2, 1, 3)).reshape(ng, N_CHIPS * rq, cg)
            wts[n] = full.astype(BF16)
        else:
            wts[n] = part.reshape(part.shape[0], d)

    _ACTIVE_SCHEDULE[0] = sched
    try:
        loss, dx, g = _device_step(x[0], target[0], wts, ex.grads)
    finally:
        _ACTIVE_SCHEDULE[0] = None
    assert not sched.hosts and not sched.posts, (sched.hosts, sched.posts)

    _comm_only([ex.reduced_halves("ffn0_w_gu", "ffn0_w_down")(), ex.sibling_halves(*first_layer)()], "reduce_tail_d2d")
    ex.adamw("ffn0_w_gu", "ffn0_w_down")()
    ex.add_halves(*first_layer)()
    _comm_only([ex.chip_sums(*first_layer)()], "reduce_tail_ici")
    ex.sum_chips(*first_layer)()
    _comm_only([ex.reduced_halves(*first_layer)()], "reduce_tail_halves")
    ex.adamw(*first_layer)()

    small_names = REPLICATED + SMALL_SHARDED
    flat = []
    for n in small_names:
        gn = g[n]
        if n == "b1_w_grp":
            gn = gn.reshape(-1, gn.shape[-1])
        flat.append(gn.astype(F32))
    sm_blk, sm_counts = _pack_rows(flat, cq)
    sm_sum = _sum_devices(_allgather8(sm_blk, "gather_small_grads"), "sum_small_grads")
    sm = dict(zip(small_names, _unpack_rows(sm_sum, sm_counts, [f.shape for f in flat])))

    out = ex.updates
    rep_w, rep_counts = _pack_rows([w[n] for n in REPLICATED], d)
    rep_g, _ = _pack_rows([sm[n] for n in REPLICATED], d)
    rep_m, _ = _pack_rows([mom[n] for n in REPLICATED], d)
    rep_v, _ = _pack_rows([vel[n] for n in REPLICATED], d)
    rep_out = _adamw(rep_w, rep_g, rep_m, rep_v, "adamw_replicated")
    rep_split = [_unpack_rows(o, rep_counts, [w[n].shape for n in REPLICATED]) for o in rep_out]
    for i, n in enumerate(REPLICATED):
        out[n] = tuple(rs[i] for rs in rep_split)
    shard_g = []
    for n in SMALL_SHARDED:
        full = sm[n]
        if n == "b1_w_grp":
            ng, rq, cg = w[n].shape
            full = full.reshape(ng, N_CHIPS, rq, cg)
            mine = lax.dynamic_index_in_dim(full, my_chip, axis=1, keepdims=False)
        else:
            full = full.reshape(full.shape[0], N_CHIPS, cq)
            mine = lax.dynamic_index_in_dim(full, my_chip, axis=1, keepdims=False)
        shard_g.append(mine)
    sh_w, sh_counts = _pack_rows([w[n] for n in SMALL_SHARDED], cq)
    sh_g, _ = _pack_rows(shard_g, cq)
    sh_m, _ = _pack_rows([mom[n] for n in SMALL_SHARDED], cq)
    sh_v, _ = _pack_rows([vel[n] for n in SMALL_SHARDED], cq)
    sh_out = _adamw(sh_w, sh_g, sh_m, sh_v, "adamw_small_sharded")
    sh_split = [_unpack_rows(o, sh_counts, [w[n].shape for n in SMALL_SHARDED]) for o in sh_out]
    for i, n in enumerate(SMALL_SHARDED):
        out[n] = tuple(rs[i] for rs in sh_split)

    total = lax.psum(loss[0, 0], ("x", "y", "c"))
    grads, deltas, new_m, new_v = ([out[n][k] for n in WEIGHT_NAMES] for k in range(4))
    return (total, dx.reshape(x.shape), *grads, *deltas, *new_m, *new_v)
```

```python
import functools

import jax
import jax.numpy as jnp
from jax import lax
from jax.experimental import pallas as pl
from jax.experimental.pallas import tpu as pltpu

F32 = jnp.float32
BF16 = jnp.bfloat16

RMS_EPS = 1e-6
LN_EPS = 1e-5
POOL_WINDOWS = (2, 4, 8, 16)
SHORT_CONV_W = 3
CONF_CONV_W = 31
N_CHIPS = 4
N_DEV = 8

ADAM_LR = 0.001
ADAM_B1 = 0.9
ADAM_B2 = 0.999
ADAM_EPS = 1e-08
ADAM_WD = 0.01
ADAM_STEP = 10

V7X_VMEM_BYTES = 64 * 1024 * 1024
VMEM_LIMIT = V7X_VMEM_BYTES - 8 * 1024 * 1024
LANES = 128
POOL_HALO = 16
SCONV_HALO = 16
CONF_HALO = 32


def _params(*sem):
    return pltpu.CompilerParams(dimension_semantics=sem, vmem_limit_bytes=VMEM_LIMIT)


def _tile(n, pref, mult=8):
    t = min(n, pref)
    while t > mult and (n % t or t % mult):
        t -= mult
    assert n % t == 0 and t % mult == 0, (n, pref, mult)
    return t


def _sigmoid(x):
    return jax.nn.sigmoid(x)


def _dot(a, b):
    return jnp.dot(a, b, preferred_element_type=F32)


def _dot_nt(a, b):
    return lax.dot_general(a, b, (((1,), (1,)), ((), ())), preferred_element_type=F32)


def _dot_tn(a, b):
    return lax.dot_general(a, b, (((0,), (0,)), ((), ())), preferred_element_type=F32)


def _colsum(x):
    return jnp.sum(x, axis=0, keepdims=True)


def _rms_stats(x):
    return lax.rsqrt(jnp.mean(x * x, axis=-1, keepdims=True) + RMS_EPS)


def _rms_bwd(du, x, gain):
    r = _rms_stats(x)
    xhat = x * r
    gdy = du * gain
    dx = r * (gdy - xhat * jnp.mean(gdy * xhat, axis=-1, keepdims=True))
    return dx, _colsum(du * xhat)


class _Task:
    def __init__(self, ins, out_shapes, aliases, sems, start, wait, done):
        self.ins, self.out_shapes, self.aliases, self.sems = list(ins), list(out_shapes), dict(aliases), list(sems)
        self.start, self.wait, self.done = start, wait, done


class _Schedule:
    def __init__(self):
        self.hosts, self.posts = {}, {}

    def host(self, kernel_name, *make_tasks):
        self.hosts.setdefault(kernel_name, []).extend(make_tasks)

    def post(self, kernel_name, *thunks):
        self.posts.setdefault(kernel_name, []).extend(thunks)

    def tasks_for(self, kernel_name):
        return [make() for make in self.hosts.pop(kernel_name, ())]

    def finished(self, kernel_name):
        for thunk in self.posts.pop(kernel_name, ()):
            thunk()


_ACTIVE_SCHEDULE = [None]


def _hosted(body, name, **kw):
    def run(*args):
        sched = _ACTIVE_SCHEDULE[0]
        tasks = sched.tasks_for(name) if sched is not None else []
        out = _call_with_tasks(body, name, tasks, kw, args) if tasks else pl.pallas_call(body, name=name, **kw)(*args)
        if sched is not None:
            sched.finished(name)
        return out

    return run


def _call_with_tasks(body, name, tasks, kw, args):
    grid = tuple(kw.get("grid", ()))
    single = not isinstance(kw["out_shape"], (list, tuple))
    out_shape = [kw["out_shape"]] if single else list(kw["out_shape"])
    out_specs = [kw["out_specs"]] if single else list(kw["out_specs"])
    scratch = list(kw.get("scratch_shapes", ()))
    n_in, n_out, n_scr = len(args), len(out_shape), len(scratch)
    t_in = [a for t in tasks for a in t.ins]
    t_out = [o for t in tasks for o in t.out_shapes]
    t_sem = [s for t in tasks for s in t.sems]
    aliases, at_in, at_out = {}, n_in, n_out
    for t in tasks:
        for i, o in t.aliases.items():
            aliases[at_in + i] = at_out + o
        at_in += len(t.ins)
        at_out += len(t.out_shapes)

    def wrapped(*refs):
        a = n_in
        b = a + len(t_in)
        c = b + n_out
        d = c + len(t_out)
        e = d + n_scr
        ins, tins, outs, touts, scr, tsems = refs[:a], refs[a:b], refs[b:c], refs[c:d], refs[d:e], refs[e:]
        views, i0, o0, s0 = [], 0, 0, 0
        for t in tasks:
            views.append((tins[i0:i0 + len(t.ins)], touts[o0:o0 + len(t.out_shapes)], tsems[s0:s0 + len(t.sems)]))
            i0, o0, s0 = i0 + len(t.ins), o0 + len(t.out_shapes), s0 + len(t.sems)

        def start_all():
            for t, v in zip(tasks, views):
                t.start(*v)

        def wait_all():
            for t, v in zip(tasks, views):
                t.wait(*v)

        if grid:
            first = functools.reduce(jnp.logical_and, [pl.program_id(i) == 0 for i in range(len(grid))])
            last = functools.reduce(jnp.logical_and, [pl.program_id(i) == grid[i] - 1 for i in range(len(grid))])
            pl.when(first)(start_all)
            body(*ins, *outs, *scr)
            pl.when(last)(wait_all)
        else:
            start_all()
            body(*ins, *outs, *scr)
            wait_all()

    res = pl.pallas_call(
        wrapped, name=name, grid=grid,
        in_specs=list(kw["in_specs"]) + [ANY] * len(t_in), out_specs=out_specs + [ANY] * len(t_out),
        out_shape=out_shape + t_out, scratch_shapes=scratch + t_sem, input_output_aliases=aliases,
        compiler_params=pltpu.CompilerParams(dimension_semantics=("arbitrary",) * len(grid), vmem_limit_bytes=VMEM_LIMIT),
    )(*args, *t_in)
    res = list(res)
    own, rest = res[:n_out], res[n_out:]
    for t in tasks:
        t.done(rest[:len(t.out_shapes)])
        rest = rest[len(t.out_shapes):]
    return own[0] if single else own


def _comm_only(tasks, name):
    _call_with_tasks(lambda: None, name, tasks, dict(grid=(), in_specs=[], out_specs=[], out_shape=[]), ())


def _rms_fwd(h, gain, name):
    s, d = h.shape
    tm = _tile(s, 512)

    def body(h_ref, g_ref, u_ref):
        x = h_ref[...]
        u_ref[...] = (x * _rms_stats(x) * g_ref[...]).astype(u_ref.dtype)

    return _hosted(
        body, name=name, grid=(s // tm,),
        in_specs=[pl.BlockSpec((tm, d), lambda m: (m, 0)), pl.BlockSpec((1, d), lambda m: (0, 0))],
        out_specs=pl.BlockSpec((tm, d), lambda m: (m, 0)),
        out_shape=jax.ShapeDtypeStruct((s, d), BF16),
        compiler_params=_params("parallel"),
    )(h, gain)


def _mm_col(a, w, bias, name):
    s, k = a.shape
    nsh, _, ns = w.shape
    tm = _tile(s, 512)
    has_bias = bias is not None

    def body(a_ref, w_ref, *rest):
        o_ref = rest[-1]
        acc = _dot(a_ref[...], w_ref[...])
        if has_bias:
            acc = acc + rest[0][...]
        o_ref[...] = acc.astype(o_ref.dtype)

    in_specs = [pl.BlockSpec((tm, k), lambda j, m: (m, 0)), pl.BlockSpec((None, k, ns), lambda j, m: (j, 0, 0))]
    args = [a, w]
    if has_bias:
        in_specs.append(pl.BlockSpec((1, ns), lambda j, m: (0, j)))
        args.append(bias)
    return _hosted(
        body, name=name, grid=(nsh, s // tm), in_specs=in_specs,
        out_specs=pl.BlockSpec((tm, ns), lambda j, m: (m, j)),
        out_shape=jax.ShapeDtypeStruct((s, nsh * ns), BF16),
        compiler_params=_params("parallel", "parallel"),
    )(*args)


def _mm_row(a, w, res, bias, name):
    s = a.shape[0]
    k, n = w.shape
    tm = _tile(s, 512)
    has_bias = bias is not None

    def body(a_ref, w_ref, res_ref, *rest):
        o_ref = rest[-1]
        y = res_ref[...] + _dot(a_ref[...], w_ref[...])
        if has_bias:
            y = y + rest[0][...]
        o_ref[...] = y

    in_specs = [pl.BlockSpec((tm, k), lambda m: (m, 0)), pl.BlockSpec((k, n), lambda m: (0, 0)),
                pl.BlockSpec((tm, n), lambda m: (m, 0))]
    args = [a, w, res]
    if has_bias:
        in_specs.append(pl.BlockSpec((1, n), lambda m: (0, 0)))
        args.append(bias)
    return _hosted(
        body, name=name, grid=(s // tm,), in_specs=in_specs,
        out_specs=pl.BlockSpec((tm, n), lambda m: (m, 0)),
        out_shape=jax.ShapeDtypeStruct((s, n), F32),
        compiler_params=_params("parallel"),
    )(*args)


def _mm_nt_row(dy, w, name):
    s, n = dy.shape
    k = w.shape[0]
    tm = _tile(s, 512)

    def body(dy_ref, w_ref, o_ref):
        o_ref[...] = _dot_nt(dy_ref[...].astype(BF16), w_ref[...])

    return _hosted(
        body, name=name, grid=(s // tm,),
        in_specs=[pl.BlockSpec((tm, n), lambda m: (m, 0)), pl.BlockSpec((k, n), lambda m: (0, 0))],
        out_specs=pl.BlockSpec((tm, k), lambda m: (m, 0)),
        out_shape=jax.ShapeDtypeStruct((s, k), F32),
        compiler_params=_params("parallel"),
    )(dy, w)


def _ffn_up(u, w, name):
    s, d = u.shape
    _, _, ns = w.shape
    tm = _tile(s, 512)

    def body(u_ref, wg_ref, wu_ref, act_ref, s1_ref, q1_ref):
        x = u_ref[...]
        g = _dot(x, wg_ref[...])
        up = _dot(x, wu_ref[...])
        sg = _sigmoid(g)
        s1 = g * sg
        act_ref[...] = (s1 * up).astype(act_ref.dtype)
        s1_ref[...] = s1.astype(s1_ref.dtype)
        q1_ref[...] = (up * sg * (1.0 + g * (1.0 - sg))).astype(q1_ref.dtype)

    out = pl.BlockSpec((tm, ns), lambda j, m: (m, j))
    return _hosted(
        body, name=name, grid=(2, s // tm),
        in_specs=[pl.BlockSpec((tm, d), lambda j, m: (m, 0)), pl.BlockSpec((None, d, ns), lambda j, m: (j, 0, 0)),
                  pl.BlockSpec((None, d, ns), lambda j, m: (j + 2, 0, 0))],
        out_specs=[out, out, out],
        out_shape=[jax.ShapeDtypeStruct((s, 2 * ns), BF16)] * 3,
        compiler_params=_params("parallel", "parallel"),
    )(u, w, w)


def _ffn_down_bwd(dh, w, s1, q1, name):
    s, d = dh.shape
    f = w.shape[0]
    tm = _tile(s, 256)

    def body(dh_ref, w_ref, s1_ref, q1_ref, o_ref):
        da = _dot_nt(dh_ref[...].astype(BF16), w_ref[...])
        o_ref[:, :f] = (da * q1_ref[...].astype(F32)).astype(o_ref.dtype)
        o_ref[:, f:] = (da * s1_ref[...].astype(F32)).astype(o_ref.dtype)

    return _hosted(
        body, name=name, grid=(s // tm,),
        in_specs=[pl.BlockSpec((tm, d), lambda m: (m, 0)), pl.BlockSpec((f, d), lambda m: (0, 0)),
                  pl.BlockSpec((tm, f), lambda m: (m, 0)), pl.BlockSpec((tm, f), lambda m: (m, 0))],
        out_specs=pl.BlockSpec((tm, 2 * f), lambda m: (m, 0)),
        out_shape=jax.ShapeDtypeStruct((s, 2 * f), BF16),
        compiler_params=_params("parallel"),
    )(dh, w, s1, q1)


def _mm_nt_col_rms_bwd(dy, w, h, gain, dh, name):
    s = dy.shape[0]
    nsh, k, ns = w.shape
    tm = _tile(s, 256)

    def body(dy_ref, w_ref, h_ref, g_ref, dh_ref, o_ref, dg_ref):
        du = _dot_nt(dy_ref[:, :ns], w_ref[0])
        for j in range(1, nsh):
            du = du + _dot_nt(dy_ref[:, j * ns:(j + 1) * ns], w_ref[j])
        dx, dg = _rms_bwd(du, h_ref[...], g_ref[...])
        o_ref[...] = dh_ref[...] + dx
        _accumulate(dg_ref, dg, pl.program_id(0) == 0)

    return _hosted(
        body, name=name, grid=(s // tm,),
        in_specs=[pl.BlockSpec((tm, nsh * ns), lambda m: (m, 0)), pl.BlockSpec((nsh, k, ns), lambda m: (0, 0, 0)),
                  pl.BlockSpec((tm, k), lambda m: (m, 0)), pl.BlockSpec((1, k), lambda m: (0, 0)),
                  pl.BlockSpec((tm, k), lambda m: (m, 0))],
        out_specs=[pl.BlockSpec((tm, k), lambda m: (m, 0)), pl.BlockSpec((1, k), lambda m: (0, 0))],
        out_shape=[jax.ShapeDtypeStruct((s, k), F32), jax.ShapeDtypeStruct((1, k), F32)],
        compiler_params=_params("arbitrary"),
    )(dy, w, h, gain, dh)


def _mm_tn(a, dy, nsh, name):
    s, k = a.shape
    ns = dy.shape[1] // nsh
    tm = _tile(s, 1024)
    tk = _tile(k, 1408, LANES)
    nk, nm = k // tk, s // tm

    def body(a_ref, dy_ref, o_ref, acc_ref):
        m = pl.program_id(2)
        part = _dot_tn(a_ref[...], dy_ref[...].astype(BF16))

        @pl.when(m == 0)
        def _():
            acc_ref[...] = part

        @pl.when(m > 0)
        def _():
            acc_ref[...] += part

        @pl.when(m == nm - 1)
        def _():
            o_ref[...] = acc_ref[...].astype(o_ref.dtype)

    return _hosted(
        body, name=name, grid=(nsh, nk, nm),
        in_specs=[pl.BlockSpec((tm, tk), lambda j, kk, m: (m, kk)), pl.BlockSpec((tm, ns), lambda j, kk, m: (m, j))],
        out_specs=pl.BlockSpec((None, tk, ns), lambda j, kk, m: (j, kk, 0)),
        out_shape=jax.ShapeDtypeStruct((nsh, k, ns), BF16),
        scratch_shapes=[pltpu.VMEM((tk, ns), F32)],
        compiler_params=_params("parallel", "parallel", "arbitrary"),
    )(a, dy)


def _main_spec(tm, w):
    return pl.BlockSpec((tm, w), lambda m: (m, 0))


def _before_spec(tm, hb, w):
    return pl.BlockSpec((hb, w), lambda m: (jnp.maximum(m * (tm // hb) - 1, 0), 0))


def _after_spec(tm, hb, w, s):
    return pl.BlockSpec((hb, w), lambda m: (jnp.minimum((m + 1) * (tm // hb), s // hb - 1), 0))


def _row_spec(w, rows=1):
    return pl.BlockSpec((rows, w), lambda m: (0, 0))


CHUNK_ROWS = 32


def _chunks(tm, d):
    for c0 in range(0, d, LANES):
        for r0 in range(0, tm, CHUNK_ROWS):
            yield pl.ds(c0, LANES), r0


def _fold_rows(x):
    acc = x[0:8]
    for i in range(8, CHUNK_ROWS, 8):
        acc = acc + x[i:i + 8]
    return acc


def _build_shifts(ext8_ref):
    n = ext8_ref.shape[1] - 8
    for r in range(1, 8):
        ext8_ref[r, pl.ds(0, n), :] = ext8_ref[0, pl.ds(r, n), :]


def _shifted(ext8_ref, start, cols):
    return ext8_ref[start % 8, pl.ds(start - start % 8, CHUNK_ROWS), cols]


def _sum_terms(terms, ways=4):
    accs = []
    for i, t in enumerate(terms):
        if i < ways:
            accs.append(t)
        else:
            accs[i % ways] = accs[i % ways] + t
    while len(accs) > 1:
        accs = [accs[i] + accs[i + 1] if i + 1 < len(accs) else accs[i] for i in range(0, len(accs), 2)]
    return accs[0]


def _accumulate(ref, val, first):
    @pl.when(first)
    def _():
        ref[...] = val

    @pl.when(jnp.logical_not(first))
    def _():
        ref[...] += val


def _sconv_taps(zext_ref, cw_ref, tm, base):
    out = cw_ref[2:3, :] * zext_ref[pl.ds(base, tm), :]
    out = out + cw_ref[1:2, :] * zext_ref[pl.ds(base - 1, tm), :]
    return out + cw_ref[0:1, :] * zext_ref[pl.ds(base - 2, tm), :]


def _sconv_fill_z(zext_ref, main_ref, before_ref, d, m):
    hb = SCONV_HALO
    zb = before_ref[:, d:2 * d].astype(F32) * before_ref[:, 2 * d:].astype(F32)
    zext_ref[pl.ds(0, hb), :] = jnp.where(m > 0, zb, 0.0)
    zext_ref[pl.ds(hb, main_ref.shape[0]), :] = main_ref[:, d:2 * d].astype(F32) * main_ref[:, 2 * d:].astype(F32)


def _sconv_fwd(bcv, cw, name):
    s, d3 = bcv.shape
    d = d3 // 3
    tm = _tile(s, 512, SCONV_HALO)

    def body(main_ref, before_ref, cw_ref, p_ref, zext_ref):
        m = pl.program_id(0)
        _sconv_fill_z(zext_ref, main_ref, before_ref, d, m)
        zc = _sconv_taps(zext_ref, cw_ref, tm, SCONV_HALO)
        p_ref[...] = (main_ref[:, :d].astype(F32) * zc).astype(p_ref.dtype)

    return _hosted(
        body, name=name, grid=(s // tm,),
        in_specs=[_main_spec(tm, d3), _before_spec(tm, SCONV_HALO, d3), _row_spec(d, SHORT_CONV_W)],
        out_specs=_main_spec(tm, d),
        out_shape=jax.ShapeDtypeStruct((s, d), BF16),
        scratch_shapes=[pltpu.VMEM((tm + SCONV_HALO, d), F32)],
        compiler_params=_params("parallel"),
    )(bcv, bcv, cw)


def _sconv_bwd(dp, bcv, cw, name):
    s, d3 = bcv.shape
    d = d3 // 3
    tm = _tile(s, 512, SCONV_HALO)
    nm = s // tm
    ha = 8

    def body(dp_ref, dpa_ref, main_ref, before_ref, after_ref, cw_ref, o_ref, dcw_ref, zext_ref, dext_ref):
        m = pl.program_id(0)
        _sconv_fill_z(zext_ref, main_ref, before_ref, d, m)
        zc = _sconv_taps(zext_ref, cw_ref, tm, SCONV_HALO)
        dp_t = dp_ref[...]
        o_ref[:, :d] = (dp_t * zc).astype(o_ref.dtype)
        dzc = dp_t * main_ref[:, :d].astype(F32)
        dext_ref[pl.ds(0, tm), :] = dzc
        dza = dpa_ref[...] * after_ref[:, :d].astype(F32)[0:ha]
        dext_ref[pl.ds(tm, ha), :] = jnp.where(m < nm - 1, dza, 0.0)
        dz = cw_ref[2:3, :] * dzc
        dz = dz + cw_ref[1:2, :] * dext_ref[pl.ds(1, tm), :]
        dz = dz + cw_ref[0:1, :] * dext_ref[pl.ds(2, tm), :]
        o_ref[:, d:2 * d] = (dz * main_ref[:, 2 * d:].astype(F32)).astype(o_ref.dtype)
        o_ref[:, 2 * d:] = (dz * main_ref[:, d:2 * d].astype(F32)).astype(o_ref.dtype)

        @pl.when(m == 0)
        def _():
            dcw_ref[...] = jnp.zeros_like(dcw_ref)

        for kk in range(SHORT_CONV_W):
            zs = zext_ref[pl.ds(SCONV_HALO - 2 + kk, tm), :]
            dcw_ref[kk:kk + 1, :] += _colsum(dzc * zs)

    return _hosted(
        body, name=name, grid=(nm,),
        in_specs=[_main_spec(tm, d), _after_spec(tm, ha, d, s), _main_spec(tm, d3), _before_spec(tm, SCONV_HALO, d3),
                  _after_spec(tm, SCONV_HALO, d3, s), _row_spec(d, SHORT_CONV_W)],
        out_specs=[_main_spec(tm, d3), _row_spec(d, 8)],
        out_shape=[jax.ShapeDtypeStruct((s, d3), BF16), jax.ShapeDtypeStruct((8, d), F32)],
        scratch_shapes=[pltpu.VMEM((tm + SCONV_HALO, d), F32), pltpu.VMEM((tm + ha, d), F32)],
        compiler_params=_params("arbitrary"),
    )(dp, dp, bcv, bcv, bcv, cw)


def _pool_counts(t0, tm, w):
    t = t0 + lax.broadcasted_iota(jnp.int32, (tm, 1), 0)
    return jnp.minimum(t + 1, w).astype(F32)


def _pool_fwd(h, gain, wg, scale, name):
    s, d = h.shape
    ng, cg, _ = wg.shape
    tm = _tile(s, 512, POOL_HALO)

    def body(h_ref, hb_ref, g_ref, wg_ref, sc_ref, o_ref, mx_ref, uext_ref):
        m = pl.program_id(0)
        x = h_ref[...]
        gain_row = g_ref[...]
        xb = hb_ref[...]
        uext_ref[pl.ds(0, POOL_HALO), :] = jnp.where(m > 0, xb * _rms_stats(xb) * gain_row, 0.0)
        uext_ref[pl.ds(POOL_HALO, tm), :] = x * _rms_stats(x) * gain_row
        for gi, win in enumerate(POOL_WINDOWS):
            cols = pl.ds(gi * cg, cg)
            u_g = uext_ref[pl.ds(POOL_HALO, tm), cols]
            acc = u_g
            for i in range(1, win):
                acc = acc + uext_ref[pl.ds(POOL_HALO - i, tm), cols]
            mixed = (acc / _pool_counts(m * tm, tm, win) - u_g).astype(BF16)
            mx_ref[:, cols] = mixed
            o_ref[:, cols] = x[:, gi * cg:(gi + 1) * cg] + _dot(mixed, wg_ref[gi]) * sc_ref[:, cols]

    return _hosted(
        body, name=name, grid=(s // tm,),
        in_specs=[_main_spec(tm, d), _before_spec(tm, POOL_HALO, d), _row_spec(d),
                  pl.BlockSpec((ng, cg, cg), lambda m: (0, 0, 0)), _row_spec(d)],
        out_specs=[_main_spec(tm, d), _main_spec(tm, d)],
        out_shape=[jax.ShapeDtypeStruct((s, d), F32), jax.ShapeDtypeStruct((s, d), BF16)],
        scratch_shapes=[pltpu.VMEM((tm + POOL_HALO, d), F32)],
        compiler_params=_params("parallel"),
    )(h, h, gain, wg, scale)


def _pool_bwd_mm(dh, mixed, wg, scale, name):
    s, d = dh.shape
    ng, cg, _ = wg.shape
    tm = _tile(s, 512)

    def body(dh_ref, mx_ref, wg_ref, sc_ref, dmx_ref, dwg_ref, dsc_ref):
        first = pl.program_id(0) == 0
        for gi in range(ng):
            cols = pl.ds(gi * cg, cg)
            dh_g = dh_ref[:, cols]
            mixed = mx_ref[:, cols]
            w_g = wg_ref[gi]
            dy = (dh_g * sc_ref[:, cols]).astype(BF16)
            dmx_ref[:, cols] = _dot_nt(dy, w_g)
            _accumulate(dsc_ref.at[:, cols], _colsum(dh_g * _dot(mixed, w_g)), first)
            _accumulate(dwg_ref.at[gi], _dot_tn(mixed, dy), first)

    return _hosted(
        body, name=name, grid=(s // tm,),
        in_specs=[_main_spec(tm, d), _main_spec(tm, d), pl.BlockSpec((ng, cg, cg), lambda m: (0, 0, 0)), _row_spec(d)],
        out_specs=[_main_spec(tm, d), pl.BlockSpec((ng, cg, cg), lambda m: (0, 0, 0)), _row_spec(d)],
        out_shape=[jax.ShapeDtypeStruct((s, d), F32), jax.ShapeDtypeStruct((ng, cg, cg), F32),
                   jax.ShapeDtypeStruct((1, d), F32)],
        compiler_params=_params("arbitrary"),
    )(dh, mixed, wg, scale)


def _pool_bwd_rms(dmixed, h, gain, dh, name):
    s, d = h.shape
    cg = d // len(POOL_WINDOWS)
    tm = _tile(s, 512, POOL_HALO)
    nm = s // tm

    def body(dmx_ref, dmxa_ref, h_ref, g_ref, dh_ref, o_ref, dg_ref, eext_ref, du_ref):
        m = pl.program_id(0)
        for gi, win in enumerate(POOL_WINDOWS):
            cols = pl.ds(gi * cg, cg)
            dmx = dmx_ref[:, cols]
            eext_ref[pl.ds(0, tm), cols] = dmx / _pool_counts(m * tm, tm, win)
            ea = dmxa_ref[:, cols] / _pool_counts((m + 1) * tm, POOL_HALO, win)
            eext_ref[pl.ds(tm, POOL_HALO), cols] = jnp.where(m < nm - 1, ea, 0.0)
            acc = -dmx
            for i in range(win):
                acc = acc + eext_ref[pl.ds(i, tm), cols]
            du_ref[:, cols] = acc
        dx, dg = _rms_bwd(du_ref[...], h_ref[...], g_ref[...])
        o_ref[...] = dh_ref[...] + dx
        _accumulate(dg_ref, dg, m == 0)

    return _hosted(
        body, name=name, grid=(nm,),
        in_specs=[_main_spec(tm, d), _after_spec(tm, POOL_HALO, d, s), _main_spec(tm, d), _row_spec(d), _main_spec(tm, d)],
        out_specs=[_main_spec(tm, d), _row_spec(d)],
        out_shape=[jax.ShapeDtypeStruct((s, d), F32), jax.ShapeDtypeStruct((1, d), F32)],
        scratch_shapes=[pltpu.VMEM((tm + POOL_HALO, d), F32), pltpu.VMEM((tm, d), F32)],
        compiler_params=_params("arbitrary"),
    )(dmixed, dmixed, h, gain, dh)


def _conf_fill_h(hext_ref, main_ref, before_ref, d, m):
    hb = before_ref[:, :d].astype(F32) * _sigmoid(before_ref[:, d:].astype(F32))
    hext_ref[pl.ds(0, CONF_HALO), :] = jnp.where(m > 0, hb, 0.0)
    hext_ref[pl.ds(CONF_HALO, main_ref.shape[0]), :] = main_ref[:, :d].astype(F32) * _sigmoid(main_ref[:, d:].astype(F32))


def _layernorm_parts(hc, g, b):
    mu = jnp.mean(hc, axis=-1, keepdims=True)
    xc = hc - mu
    rs = lax.rsqrt(jnp.mean(xc * xc, axis=-1, keepdims=True) + LN_EPS)
    xhat = xc * rs
    return xhat, rs, xhat * g + b


def _conf_mid_fwd(ag, dw, b_dw, ln_g, ln_b, name):
    s, d2 = ag.shape
    d = d2 // 2
    tm = _tile(s, 256, CONF_HALO)
    base = CONF_HALO - (CONF_CONV_W - 1)

    def body(main_ref, before_ref, dw_ref, bdw_ref, g_ref, b_ref, s_ref, hc_ref, hext_ref):
        m = pl.program_id(0)
        _conf_fill_h(hext_ref.at[0], main_ref, before_ref, d, m)
        _build_shifts(hext_ref)
        for cols, r0 in _chunks(tm, d):
            taps = (dw_ref[kk:kk + 1, cols] * _shifted(hext_ref, base + kk + r0, cols) for kk in range(CONF_CONV_W))
            hc_ref[pl.ds(r0, CHUNK_ROWS), cols] = bdw_ref[:, cols] + _sum_terms(taps)
        _, _, l = _layernorm_parts(hc_ref[...], g_ref[...], b_ref[...])
        s_ref[...] = (l * _sigmoid(l)).astype(s_ref.dtype)

    return _hosted(
        body, name=name, grid=(s // tm,),
        in_specs=[_main_spec(tm, d2), _before_spec(tm, CONF_HALO, d2), _row_spec(d, CONF_CONV_W), _row_spec(d),
                  _row_spec(d), _row_spec(d)],
        out_specs=[_main_spec(tm, d), _main_spec(tm, d)],
        out_shape=[jax.ShapeDtypeStruct((s, d), BF16), jax.ShapeDtypeStruct((s, d), F32)],
        scratch_shapes=[pltpu.VMEM((8, tm + CONF_HALO, d), F32)],
        compiler_params=_params("parallel"),
    )(ag, ag, dw, b_dw, ln_g, ln_b)


def _conf_out_bwd(dh, w, hc, ln_g, ln_b, name):
    s, d = dh.shape
    tm = _tile(s, 256)

    def body(dh_ref, w_ref, hc_ref, g_ref, b_ref, o_ref, dg_ref, db_ref, dbo_ref):
        first = pl.program_id(0) == 0
        dh_t = dh_ref[...]
        ds = _dot_nt(dh_t.astype(BF16), w_ref[...])
        xhat, rs, l = _layernorm_parts(hc_ref[...], g_ref[...], b_ref[...])
        sg = _sigmoid(l)
        dl = ds * sg * (1.0 + l * (1.0 - sg))
        dxh = dl * g_ref[...]
        o_ref[...] = rs * (dxh - jnp.mean(dxh, axis=-1, keepdims=True)
                           - xhat * jnp.mean(dxh * xhat, axis=-1, keepdims=True))
        _accumulate(dg_ref, _colsum(dl * xhat), first)
        _accumulate(db_ref, _colsum(dl), first)
        _accumulate(dbo_ref, _colsum(dh_t), first)

    return _hosted(
        body, name=name, grid=(s // tm,),
        in_specs=[_main_spec(tm, d), pl.BlockSpec((d, d), lambda m: (0, 0)), _main_spec(tm, d), _row_spec(d), _row_spec(d)],
        out_specs=[_main_spec(tm, d), _row_spec(d), _row_spec(d), _row_spec(d)],
        out_shape=[jax.ShapeDtypeStruct((s, d), F32)] + [jax.ShapeDtypeStruct((1, d), F32)] * 3,
        compiler_params=_params("arbitrary"),
    )(dh, w, hc, ln_g, ln_b)


def _conf_mid_bwd(dhc, ag, dw, name):
    s, d2 = ag.shape
    d = d2 // 2
    tm = _tile(s, 256, CONF_HALO)
    nm = s // tm
    kw = CONF_CONV_W
    base = CONF_HALO - (kw - 1)

    def body(dhc_ref, dhca_ref, main_ref, before_ref, dw_ref, o_ref, ddw_ref, dbdw_ref, dbpw_ref, hext_ref, dext_ref):
        m = pl.program_id(0)
        first = m == 0
        _conf_fill_h(hext_ref.at[0], main_ref, before_ref, d, m)
        _build_shifts(hext_ref)
        dext_ref[0, pl.ds(0, tm), :] = dhc_ref[...]
        dext_ref[0, pl.ds(tm, CONF_HALO), :] = jnp.where(m < nm - 1, dhca_ref[...], 0.0)
        _build_shifts(dext_ref)

        @pl.when(first)
        def _():
            ddw_ref[...] = jnp.zeros_like(ddw_ref)
            dbdw_ref[...] = jnp.zeros_like(dbdw_ref)
            dbpw_ref[...] = jnp.zeros_like(dbpw_ref)

        fold = _fold_rows
        for c0 in range(0, d, LANES):
            cols, gate_cols = pl.ds(c0, LANES), pl.ds(d + c0, LANES)
            sum_da = sum_dgate = sum_dhc = jnp.zeros((8, LANES), F32)
            for r0 in range(0, tm, CHUNK_ROWS):
                rows = pl.ds(r0, CHUNK_ROWS)
                dhc_c = dext_ref[0, rows, cols]
                dhh = _sum_terms(dw_ref[kk:kk + 1, cols] * _shifted(dext_ref, r0 + kw - 1 - kk, cols) for kk in range(kw))
                a = main_ref[rows, cols].astype(F32)
                sg = _sigmoid(main_ref[rows, gate_cols].astype(F32))
                da = dhh * sg
                dgate = dhh * a * sg * (1.0 - sg)
                o_ref[rows, cols] = da.astype(o_ref.dtype)
                o_ref[rows, gate_cols] = dgate.astype(o_ref.dtype)
                sum_da, sum_dgate, sum_dhc = sum_da + fold(da), sum_dgate + fold(dgate), sum_dhc + fold(dhc_c)
            dbdw_ref[:, cols] += _colsum(sum_dhc)
            dbpw_ref[:, cols] += _colsum(sum_da)
            dbpw_ref[:, gate_cols] += _colsum(sum_dgate)
            for kk in range(kw):
                acc = _sum_terms(fold(dext_ref[0, pl.ds(r0, CHUNK_ROWS), cols] * _shifted(hext_ref, base + kk + r0, cols))
                                 for r0 in range(0, tm, CHUNK_ROWS))
                ddw_ref[kk:kk + 1, cols] += _colsum(acc)

    return _hosted(
        body, name=name, grid=(nm,),
        in_specs=[_main_spec(tm, d), _after_spec(tm, CONF_HALO, d, s), _main_spec(tm, d2), _before_spec(tm, CONF_HALO, d2),
                  _row_spec(d, kw)],
        out_specs=[_main_spec(tm, d2), _row_spec(d, 32), _row_spec(d), _row_spec(d2)],
        out_shape=[jax.ShapeDtypeStruct((s, d2), BF16), jax.ShapeDtypeStruct((32, d), F32),
                   jax.ShapeDtypeStruct((1, d), F32), jax.ShapeDtypeStruct((1, d2), F32)],
        scratch_shapes=[pltpu.VMEM((8, tm + CONF_HALO, d), F32), pltpu.VMEM((8, tm + CONF_HALO, d), F32)],
        compiler_params=_params("arbitrary"),
    )(dhc, dhc, ag, ag, dw)


def _loss_head(h, gain, target, name):
    s, d = h.shape
    tm = _tile(s, 512)

    def body(h_ref, g_ref, t_ref, loss_ref, dh_ref, dg_ref):
        first = pl.program_id(0) == 0
        x = h_ref[...]
        err = x * _rms_stats(x) * g_ref[...] - t_ref[...]
        part = 0.5 * jnp.sum(jnp.mean(err * err, axis=-1, keepdims=True), axis=0, keepdims=True)
        dx, dg = _rms_bwd(err * (1.0 / d), x, g_ref[...])
        dh_ref[...] = dx
        _accumulate(loss_ref, part, first)
        _accumulate(dg_ref, dg, first)

    return _hosted(
        body, name=name, grid=(s // tm,),
        in_specs=[_main_spec(tm, d), _row_spec(d), _main_spec(tm, d)],
        out_specs=[pl.BlockSpec((1, 1), lambda m: (0, 0)), _main_spec(tm, d), _row_spec(d)],
        out_shape=[jax.ShapeDtypeStruct((1, 1), F32), jax.ShapeDtypeStruct((s, d), F32), jax.ShapeDtypeStruct((1, d), F32)],
        compiler_params=_params("arbitrary"),
    )(h, gain, target)


def _ffn_fwd(h, wts, i):
    u = _rms_fwd(h, wts[f"ln2_{i}"], f"ffn{i}_rms")
    act, s1, q1 = _ffn_up(u, wts[f"ffn{i}_w_gu"], f"ffn{i}_up")
    h_new = _mm_row(act, wts[f"ffn{i}_w_down"], h, None, f"ffn{i}_down")
    return h_new, (h, u, act, s1, q1)


def _ffn_bwd(dh, saved, gain, w_gu, w_down, i):
    h, u, act, s1, q1 = saved
    dgu = _ffn_down_bwd(dh, w_down, s1, q1, f"ffn{i}_down_bwd")
    dw_down = _mm_tn(act, dh, 1, f"ffn{i}_dw_down")
    dw_gu = _mm_tn(u, dgu, N_CHIPS, f"ffn{i}_dw_gu")
    dh_new, dgain = _mm_nt_col_rms_bwd(dgu, w_gu, h, gain, dh, f"ffn{i}_up_bwd")
    return dh_new, dgain, dw_gu, dw_down


def _device_step(x, target, wts, g=None):
    g = {} if g is None else g
    saved = {}
    h = x

    def short_conv_fwd(h, i):
        u = _rms_fwd(h, wts[f"ln1_{i}"], f"a{i}_rms")
        bcv = _mm_col(u, wts[f"a{i}_w_in"], None, f"a{i}_in")
        p = _sconv_fwd(bcv, wts[f"a{i}_conv"], f"a{i}_conv")
        return _mm_row(p, wts[f"a{i}_w_out"], h, None, f"a{i}_out"), (h, u, bcv, p)

    def short_conv_bwd(dh, sv, i):
        h, u, bcv, p = sv
        dp = _mm_nt_row(dh, wts[f"a{i}_w_out"], f"a{i}_out_bwd")
        g[f"a{i}_w_out"] = _mm_tn(p, dh, 1, f"a{i}_dw_out")
        dbcv, dcw = _sconv_bwd(dp, bcv, wts[f"a{i}_conv"], f"a{i}_conv_bwd")
        g[f"a{i}_conv"] = dcw[:SHORT_CONV_W]
        g[f"a{i}_w_in"] = _mm_tn(u, dbcv, N_CHIPS, f"a{i}_dw_in")
        dh, g[f"ln1_{i}"] = _mm_nt_col_rms_bwd(dbcv, wts[f"a{i}_w_in"], h, wts[f"ln1_{i}"], dh, f"a{i}_in_bwd")
        return dh

    h, saved["a0"] = short_conv_fwd(h, 0)
    h, saved["f0"] = _ffn_fwd(h, wts, 0)

    h_in = h
    h, mixed = _pool_fwd(h, wts["ln1_1"], wts["b1_w_grp"], wts["b1_scale"], "b1_fwd")
    saved["b1"] = (h_in, mixed)
    h, saved["f1"] = _ffn_fwd(h, wts, 1)

    h_in = h
    u = _rms_fwd(h, wts["ln1_2"], "c2_rms")
    ag = _mm_col(u, wts["c2_w_pw1"], wts["c2_b_pw1"], "c2_pw1")
    sw, hc = _conf_mid_fwd(ag, wts["c2_dw"], wts["c2_b_dw"], wts["c2_ln_g"], wts["c2_ln_b"], "c2_mid")
    h = _mm_row(sw, wts["c2_w_pw2"], h, wts["c2_b_pw2"], "c2_pw2")
    saved["c2"] = (h_in, u, ag, sw, hc)
    h, saved["f2"] = _ffn_fwd(h, wts, 2)

    h, saved["a3"] = short_conv_fwd(h, 3)
    h, saved["f3"] = _ffn_fwd(h, wts, 3)

    loss, dh, g["ln_f"] = _loss_head(h, wts["ln_f"], target, "loss_head")

    def ffn_bwd(dh, i):
        dh, g[f"ln2_{i}"], g[f"ffn{i}_w_gu"], g[f"ffn{i}_w_down"] = _ffn_bwd(
            dh, saved[f"f{i}"], wts[f"ln2_{i}"], wts[f"ffn{i}_w_gu"], wts[f"ffn{i}_w_down"], i)
        return dh

    dh = ffn_bwd(dh, 3)
    dh = short_conv_bwd(dh, saved["a3"], 3)

    dh = ffn_bwd(dh, 2)
    h_in, u, ag, sw, hc = saved["c2"]
    dhc, g["c2_ln_g"], g["c2_ln_b"], g["c2_b_pw2"] = _conf_out_bwd(
        dh, wts["c2_w_pw2"], hc, wts["c2_ln_g"], wts["c2_ln_b"], "c2_pw2_bwd")
    g["c2_w_pw2"] = _mm_tn(sw, dh, 1, "c2_dw_pw2")
    dag, ddw, g["c2_b_dw"], g["c2_b_pw1"] = _conf_mid_bwd(dhc, ag, wts["c2_dw"], "c2_mid_bwd")
    g["c2_dw"] = ddw[:CONF_CONV_W]
    g["c2_w_pw1"] = _mm_tn(u, dag, N_CHIPS, "c2_dw_pw1")
    dh, g["ln1_2"] = _mm_nt_col_rms_bwd(dag, wts["c2_w_pw1"], h_in, wts["ln1_2"], dh, "c2_pw1_bwd")

    dh = ffn_bwd(dh, 1)
    h_in, mixed = saved["b1"]
    dmixed, g["b1_w_grp"], g["b1_scale"] = _pool_bwd_mm(dh, mixed, wts["b1_w_grp"], wts["b1_scale"], "b1_bwd_mm")
    dh, g["ln1_1"] = _pool_bwd_rms(dmixed, h_in, wts["ln1_1"], dh, "b1_bwd_rms")

    dh = ffn_bwd(dh, 0)
    dh = short_conv_bwd(dh, saved["a0"], 0)
    return loss, dh, g


MESH = pl.DeviceIdType.MESH
ANY = pl.BlockSpec(memory_space=pl.ANY)


def _position():
    return lax.axis_index("x"), lax.axis_index("y"), lax.axis_index("c")


def _other_chips(x, y):
    return [(1 - x, y), (x, 1 - y), (1 - x, 1 - y)]


def _remote(src, dst, send_sem, recv_sem, to):
    return pltpu.make_async_remote_copy(src_ref=src, dst_ref=dst, send_sem=send_sem, recv_sem=recv_sem,
                                        device_id=to, device_id_type=MESH)


def _half_rows(ref_rows, c):
    hr = ref_rows // 2
    return pl.ds(pl.multiple_of(c * hr, 16), hr)


def _allgather8(v, name):
    m_per, n = v.shape

    def body(v_ref, out_ref, send_sems, recv_sems, local_sem):
        x, y, c = _position()
        me, sibling = (x, y, c), (x, y, 1 - c)
        chips = _other_chips(x, y)

        def rows(px, py, pc):
            return out_ref.at[pl.ds((4 * px + 2 * py + pc) * m_per, m_per), :]

        def copy(k, block, to, src=None):
            return _remote(rows(*block) if src is None else src, rows(*block), send_sems.at[k], recv_sems.at[k], to)

        mine = pltpu.make_async_copy(v_ref, rows(*me), local_sem)
        mine.start()
        first = [copy(0, me, sibling, src=v_ref)]
        first += [copy(1 + j, me, (*chip, c), src=v_ref) for j, chip in enumerate(chips)]
        for cp in first:
            cp.start()
        passed = [copy(4 + j, (*chip, c), sibling) for j, chip in enumerate(chips)]
        for j, chip in enumerate(chips):
            copy(1 + j, (*chip, c), me).wait_recv()
            passed[j].start()
        copy(0, sibling, me).wait_recv()
        for j, chip in enumerate(chips):
            copy(4 + j, (*chip, 1 - c), me).wait_recv()
        for cp in first + passed:
            cp.wait_send()
        mine.wait()

    return _hosted(
        body, name=name,
        out_shape=jax.ShapeDtypeStruct((N_DEV * m_per, n), v.dtype),
        in_specs=[pl.BlockSpec(memory_space=pltpu.VMEM)],
        out_specs=pl.BlockSpec(memory_space=pltpu.VMEM),
        scratch_shapes=[pltpu.SemaphoreType.DMA((7,)), pltpu.SemaphoreType.DMA((7,)), pltpu.SemaphoreType.DMA],
        compiler_params=pltpu.CompilerParams(vmem_limit_bytes=VMEM_LIMIT),
    )(v)


def _cast_to_slot(ws, idx, name):
    r, cols = ws[0].shape
    assert all(w.shape == (r, cols) for w in ws)
    n = len(ws)
    tr = _tile(r, 256, 16)

    def body(idx_ref, *refs):
        for w_ref, o_ref in zip(refs[:n], refs[n:]):
            o_ref[...] = w_ref[...].astype(o_ref.dtype)

    return _hosted(
        body, name=name,
        grid_spec=pltpu.PrefetchScalarGridSpec(
            num_scalar_prefetch=1, grid=(r // tr,),
            in_specs=[pl.BlockSpec((tr, cols), lambda t, idx_ref: (t, 0))] * n,
            out_specs=[pl.BlockSpec((None, tr, cols), lambda t, idx_ref: (idx_ref[0], t, 0))] * n),
        out_shape=[jax.ShapeDtypeStruct((N_CHIPS, r, cols), BF16)] * n,
        compiler_params=_params("parallel"),
    )(idx, *ws)


def _dma_sems(*shape):
    return [pltpu.SemaphoreType.DMA(shape), pltpu.SemaphoreType.DMA(shape)]


def _same_shapes(arrays):
    return [jax.ShapeDtypeStruct(a.shape, a.dtype) for a in arrays]


def _part_rows(ref_rows, c, part):
    hr = ref_rows // 2
    i, n = part
    size = hr // n
    assert size * n == hr and size % 16 == 0, (ref_rows, part)
    return pl.ds(pl.multiple_of(c * hr + i * size, 16), size)


def _task_gather_ici(bufs, done, part=(0, 1)):
    n = len(bufs)

    def copies(outs, sems, landing):
        x, y, c = _position()
        my_chip = 2 * x + y
        res = []
        for i in range(n):
            rows = _part_rows(bufs[i].shape[1], c, part)
            for r, (px, py) in enumerate(_other_chips(x, y)):
                slot = (2 * px + py) if landing else my_chip
                res.append(_remote(outs[i].at[my_chip, rows, :], outs[i].at[slot, rows, :], sems[0].at[i, r], sems[1].at[i, r],
                                   (px, py, c)))
        return res

    def start(ins, outs, sems):
        for cp in copies(outs, sems, False):
            cp.start()

    def wait(ins, outs, sems):
        for cp in copies(outs, sems, True):
            cp.wait_recv()
            cp.wait_send()

    return _Task(bufs, _same_shapes(bufs), {i: i for i in range(n)}, _dma_sems(n, 3), start, wait, done)


def _task_gather_d2d(bufs, done):
    n = len(bufs)

    def copies(outs, sems, landing):
        x, y, c = _position()
        res = []
        for i in range(n):
            rows = _half_rows(bufs[i].shape[1], (1 - c) if landing else c)
            for r, (px, py) in enumerate(_other_chips(x, y)):
                part = outs[i].at[2 * px + py, rows, :]
                res.append(_remote(part, part, sems[0].at[i, r], sems[1].at[i, r], (x, y, 1 - c)))
        return res

    def start(ins, outs, sems):
        for cp in copies(outs, sems, False):
            cp.start()

    def wait(ins, outs, sems):
        for cp in copies(outs, sems, True):
            cp.wait_recv()
        for cp in copies(outs, sems, False):
            cp.wait_send()

    return _Task(bufs, _same_shapes(bufs), {i: i for i in range(n)}, _dma_sems(n, 3), start, wait, done)


def _task_sibling_halves(grads, done):
    n = len(grads)

    def copies(ins, outs, sems):
        x, y, c = _position()
        return [_remote(ins[i].at[:, _half_rows(grads[i].shape[1], 1 - c), :], outs[i], sems[0].at[i], sems[1].at[i],
                        (x, y, 1 - c)) for i in range(n)]

    def start(ins, outs, sems):
        for cp in copies(ins, outs, sems):
            cp.start()

    def wait(ins, outs, sems):
        for cp in copies(ins, outs, sems):
            cp.wait()

    shapes = [jax.ShapeDtypeStruct((g.shape[0], g.shape[1] // 2, g.shape[2]), g.dtype) for g in grads]
    return _Task(grads, shapes, {}, _dma_sems(n), start, wait, done)


def _task_chip_sums(parts, done, landed=None, part=(0, 1)):
    n = len(parts)
    i_part, n_parts = part
    sizes = [p.shape[1] // n_parts for p in parts]
    assert all(p.shape[1] == size * n_parts and size % 16 == 0 for p, size in zip(parts, sizes)), part
    rows = [pl.ds(i_part * size, size) for size in sizes]

    def copies(ins, outs, sems):
        x, y, c = _position()
        return [_remote(ins[i].at[2 * px + py, rows[i], :], outs[i].at[r, rows[i], :], sems[0].at[i, r], sems[1].at[i, r],
                        (px, py, c))
                for i in range(n) for r, (px, py) in enumerate(_other_chips(x, y))]

    def start(ins, outs, sems):
        for cp in copies(ins, outs, sems):
            cp.start()

    def wait(ins, outs, sems):
        for cp in copies(ins, outs, sems):
            cp.wait()

    shapes = [jax.ShapeDtypeStruct((3,) + p.shape[1:], p.dtype) for p in parts]
    if landed is None:
        return _Task(parts, shapes, {}, _dma_sems(n, 3), start, wait, done)
    return _Task(list(parts) + list(landed), shapes, {n + i: i for i in range(n)}, _dma_sems(n, 3), start, wait, done)


def _task_reduced_halves(fulls, done):
    n = len(fulls)

    def copies(outs, sems, landing):
        x, y, c = _position()
        res = []
        for i in range(n):
            rows = _half_rows(fulls[i].shape[0], (1 - c) if landing else c)
            res.append(_remote(outs[i].at[rows, :], outs[i].at[rows, :], sems[0].at[i], sems[1].at[i], (x, y, 1 - c)))
        return res

    def start(ins, outs, sems):
        for cp in copies(outs, sems, False):
            cp.start()

    def wait(ins, outs, sems):
        for cp in copies(outs, sems, True):
            cp.wait_recv()
        for cp in copies(outs, sems, False):
            cp.wait_send()

    return _Task(fulls, _same_shapes(fulls), {i: i for i in range(n)}, _dma_sems(n), start, wait, done)


def _add_halves(grad, sib, c, name):
    nsh, r, cols = grad.shape
    hr = r // 2
    tr = _tile(hr, 512, 16)
    nt = hr // tr

    def body(c_ref, g_ref, s_ref, o_ref):
        o_ref[...] = (g_ref[...].astype(F32) + s_ref[...].astype(F32)).astype(o_ref.dtype)

    return _hosted(
        body, name=name,
        grid_spec=pltpu.PrefetchScalarGridSpec(
            num_scalar_prefetch=1, grid=(nsh, nt),
            in_specs=[pl.BlockSpec((None, tr, cols), lambda j, t, c_ref: (j, c_ref[1] * nt + t, 0)),
                      pl.BlockSpec((None, tr, cols), lambda j, t, c_ref: (j, t, 0))],
            out_specs=pl.BlockSpec((None, tr, cols), lambda j, t, c_ref: (j, t, 0))),
        out_shape=jax.ShapeDtypeStruct((nsh, hr, cols), BF16),
        compiler_params=_params("parallel", "parallel"),
    )(c, grad, sib)


def _sum_chips(own, landed, idx, name):
    nsh, hr, cols = own.shape
    tr = _tile(hr, 512, 16)
    nt = hr // tr

    def body(idx_ref, p_ref, l_ref, o_ref):
        acc = p_ref[...].astype(F32)
        for k in range(landed.shape[0]):
            acc = acc + l_ref[k].astype(F32)
        o_ref[...] = acc

    return _hosted(
        body, name=name,
        grid_spec=pltpu.PrefetchScalarGridSpec(
            num_scalar_prefetch=1, grid=(nt,),
            in_specs=[pl.BlockSpec((None, tr, cols), lambda t, idx_ref: (idx_ref[0], t, 0)),
                      pl.BlockSpec((landed.shape[0], tr, cols), lambda t, idx_ref: (0, t, 0))],
            out_specs=pl.BlockSpec((tr, cols), lambda t, idx_ref: (idx_ref[1] * nt + t, 0))),
        out_shape=jax.ShapeDtypeStruct((2 * hr, cols), F32),
        compiler_params=_params("parallel"),
    )(idx, own, landed)


def _sum_devices(blocks, name):
    m8, n = blocks.shape
    m = m8 // N_DEV

    def body(b_ref, o_ref):
        acc = b_ref[pl.ds(0, m), :]
        for k in range(1, N_DEV):
            acc = acc + b_ref[pl.ds(k * m, m), :]
        o_ref[...] = acc

    return _hosted(
        body, name=name, out_shape=jax.ShapeDtypeStruct((m, n), F32),
        in_specs=[pl.BlockSpec(memory_space=pltpu.VMEM)], out_specs=pl.BlockSpec(memory_space=pltpu.VMEM),
        compiler_params=pltpu.CompilerParams(vmem_limit_bytes=VMEM_LIMIT),
    )(blocks)


def _adamw_update(w, grad, m, v):
    new_m = ADAM_B1 * m + (1.0 - ADAM_B1) * grad
    new_v = ADAM_B2 * v + (1.0 - ADAM_B2) * (grad * grad)
    m_hat = new_m * (1.0 / (1.0 - ADAM_B1 ** ADAM_STEP))
    v_hat = new_v * (1.0 / (1.0 - ADAM_B2 ** ADAM_STEP))
    return -ADAM_LR * (m_hat / (jnp.sqrt(v_hat) + ADAM_EPS) + ADAM_WD * w), new_m, new_v


def _adamw(w, g, m, v, name):
    r, cols = w.shape
    tr = _tile(r, 256)

    def body(w_ref, g_ref, m_ref, v_ref, go_ref, d_ref, mo_ref, vo_ref):
        grad = g_ref[...]
        go_ref[...] = grad
        d_ref[...], mo_ref[...], vo_ref[...] = _adamw_update(w_ref[...], grad, m_ref[...], v_ref[...])

    spec = pl.BlockSpec((tr, cols), lambda t: (t, 0))
    return _hosted(
        body, name=name, grid=(r // tr,), in_specs=[spec] * 4, out_specs=[spec] * 4,
        out_shape=[jax.ShapeDtypeStruct((r, cols), F32)] * 4,
        compiler_params=_params("parallel"),
    )(w, g, m, v)


def _adamw_small(grad_blocks, params, name):
    nb, npar = len(grad_blocks), len(params)

    def body(*refs):
        blocks, ins, outs = refs[:nb], refs[nb:nb + 3 * npar], refs[nb + 3 * npar:]
        for p, (w, _, _, blk, row0) in enumerate(params):
            grad = blocks[blk][pl.ds(row0, w.shape[0]), :]
            outs[4 * p][...] = grad
            outs[4 * p + 1][...], outs[4 * p + 2][...], outs[4 * p + 3][...] = _adamw_update(
                ins[3 * p][...], grad, ins[3 * p + 1][...], ins[3 * p + 2][...])

    args = list(grad_blocks) + [a for w, m, v, _, _ in params for a in (w, m, v)]
    vmem = pl.BlockSpec(memory_space=pltpu.VMEM)
    out = _hosted(
        body, name=name, in_specs=[vmem] * len(args), out_specs=[vmem] * (4 * npar),
        out_shape=[jax.ShapeDtypeStruct(w.shape, F32) for w, _, _, _, _ in params for _ in range(4)],
    )(*args)
    return [tuple(out[4 * p:4 * p + 4]) for p in range(npar)]


WEIGHT_NAMES = (
    "ln1_0", "a0_w_in", "a0_conv", "a0_w_out", "ln2_0", "ffn0_w_gu", "ffn0_w_down",
    "ln1_1", "b1_w_grp", "b1_scale", "ln2_1", "ffn1_w_gu", "ffn1_w_down",
    "ln1_2", "c2_w_pw1", "c2_b_pw1", "c2_dw", "c2_b_dw", "c2_ln_g", "c2_ln_b", "c2_w_pw2", "c2_b_pw2",
    "ln2_2", "ffn2_w_gu", "ffn2_w_down",
    "ln1_3", "a3_w_in", "a3_conv", "a3_w_out", "ln2_3", "ffn3_w_gu", "ffn3_w_down", "ln_f")
BIG = ("a0_w_in", "a0_w_out", "ffn0_w_gu", "ffn0_w_down", "ffn1_w_gu", "ffn1_w_down", "c2_w_pw1", "c2_w_pw2",
       "ffn2_w_gu", "ffn2_w_down", "a3_w_in", "a3_w_out", "ffn3_w_gu", "ffn3_w_down")
SMALL_SHARDED = ("a0_conv", "a3_conv", "c2_dw", "b1_w_grp")
REPLICATED = tuple(n for n in WEIGHT_NAMES if n not in BIG and n not in SMALL_SHARDED)


def _pad_rows(a, mult=8):
    pad = -a.shape[0] % mult
    return a if pad == 0 else jnp.concatenate([a, jnp.zeros((pad, a.shape[1]), a.dtype)], axis=0)


def _pack_rows(parts, width):
    rows = [p.reshape(-1, width) for p in parts]
    return _pad_rows(jnp.concatenate(rows, axis=0)), [r.shape[0] for r in rows]


def _unpack_rows(packed, counts, shapes):
    out, at = [], 0
    for n, shp in zip(counts, shapes):
        out.append(packed[at:at + n].reshape(shp))
        at += n
    return out


COLUMN_SHARDED = ("w_in", "w_gu", "w_pw1")


class _Weights(dict):
    def __init__(self, bufs):
        super().__init__()
        self.bufs = bufs

    def __missing__(self, name):
        buf = self.bufs[name]
        return buf if name.endswith(COLUMN_SHARDED) else buf.reshape(-1, buf.shape[-1])


class _Exchange:
    def __init__(self, w, mom, vel, idx):
        self.w, self.mom, self.vel, self.idx = w, mom, vel, idx
        self.bufs = {}
        self.weights = _Weights(self.bufs)
        self.grads = {}
        self.sib, self.part, self.landed, self.full, self.updates = {}, {}, {}, {}, {}

    def cast(self, names):
        by_shape = {}
        for n in names:
            by_shape.setdefault(self.w[n].shape, []).append(n)
        for group in by_shape.values():
            self.bufs.update(zip(group, _cast_to_slot([self.w[n] for n in group], self.idx, f"cast_{group[0]}")))

    @staticmethod
    def _store(table, names):
        def done(arrays):
            table.update(zip(names, arrays))
        return done

    def _grad(self, n):
        g = self.grads[n]
        return g.reshape(N_CHIPS, -1, g.shape[-1])

    def gather_ici(self, *names, part=(0, 1)):
        return lambda: _task_gather_ici([self.bufs[n] for n in names], self._store(self.bufs, names), part)

    def gather_d2d(self, *names):
        return lambda: _task_gather_d2d([self.bufs[n] for n in names], self._store(self.bufs, names))

    def sibling_halves(self, *names):
        return lambda: _task_sibling_halves([self._grad(n) for n in names], self._store(self.sib, names))

    def add_halves(self, *names):
        def run():
            for n in names:
                self.part[n] = _add_halves(self._grad(n), self.sib.pop(n), self.idx, f"reduce_add_{n}")
        return run

    def chip_sums(self, *names, part=(0, 1)):
        def make():
            landed = [self.landed[n] for n in names] if part[0] > 0 else None
            return _task_chip_sums([self.part[n] for n in names], self._store(self.landed, names), landed, part)
        return make

    def sum_chips(self, *names):
        def run():
            for n in names:
                self.full[n] = _sum_chips(self.part.pop(n), self.landed.pop(n), self.idx, f"reduce_sum_{n}")
        return run

    def reduced_halves(self, *names):
        return lambda: _task_reduced_halves([self.full[n] for n in names], self._store(self.full, names))

    def adamw(self, *names):
        def run():
            for n in names:
                self.updates[n] = _adamw(self.w[n], self.full.pop(n), self.mom[n], self.vel[n], f"adamw_{n}")
        return run


def _plan(ex):
    s = _Schedule()

    def ffn(i):
        return f"ffn{i}_w_gu", f"ffn{i}_w_down"

    c2, a3 = ("c2_w_pw1", "c2_w_pw2"), ("a3_w_in", "a3_w_out")
    first, second = (0, 2), (1, 2)
    gu, down = ffn(0)
    s.host("a0_rms", ex.gather_ici("a0_w_out"))
    s.host("a0_in", ex.gather_ici(gu, part=first), ex.gather_d2d("a0_w_out"))
    s.host("a0_conv", ex.gather_ici(gu, part=second))
    s.host("a0_out", ex.gather_ici(down), ex.gather_d2d(gu))
    s.host("ffn0_rms", ex.gather_d2d(down))
    gu, down = ffn(1)
    s.host("ffn0_up", ex.gather_ici(gu))
    s.host("ffn0_down", ex.gather_ici(down), ex.gather_d2d(gu))
    s.host("b1_fwd", ex.gather_d2d(down))
    s.host("ffn1_up", ex.gather_ici(*c2))
    s.host("ffn1_down", ex.gather_d2d(*c2))
    s.host("c2_mid", ex.gather_ici(*ffn(2)))
    s.host("c2_pw2", ex.gather_d2d(*ffn(2)))
    gu, down = ffn(3)
    s.host("ffn2_up", ex.gather_ici(*a3))
    s.host("ffn2_down", ex.gather_d2d(*a3), ex.gather_ici(down))
    s.host("a3_in", ex.gather_ici(gu, part=first))
    s.host("a3_conv", ex.gather_ici(gu, part=second))
    s.host("a3_out", ex.gather_d2d(gu, down))

    def reduce_on(names, first_host, ici_hosts, last_host):
        s.host(first_host, ex.sibling_halves(*names))
        s.post(first_host, ex.add_halves(*names))
        for host, hosted, part in ici_hosts:
            s.host(host, ex.chip_sums(*hosted, part=part))
        s.post(ici_hosts[-1][0], ex.sum_chips(*names))
        if last_host is not None:
            s.host(last_host, ex.reduced_halves(*names))
            s.post(last_host, ex.adamw(*names))

    whole = (0, 1)
    for layer, i in (("a3", 3), ("a0", 0)):
        gu, down = ffn(i)
        reduce_on((gu, down), f"{layer}_out_bwd",
                  [(f"{layer}_conv_bwd", (down,), whole), (f"{layer}_dw_in", (gu,), first), (f"{layer}_in_bwd", (gu,), second)],
                  "ffn2_down_bwd" if i == 3 else None)
    reduce_on(a3, "ffn2_down_bwd", [("ffn2_dw_down", a3[:1], whole), ("ffn2_dw_gu", a3[1:], whole)], "ffn2_up_bwd")
    reduce_on(ffn(2), "c2_pw2_bwd", [("c2_mid_bwd", ffn(2), whole)], "c2_pw1_bwd")
    reduce_on(c2, "ffn1_down_bwd", [("ffn1_dw_down", c2, whole)], "ffn1_dw_gu")
    gu, down = ffn(1)
    reduce_on((gu, down), "b1_bwd_mm",
              [("b1_bwd_rms", (down,), whole), ("ffn0_down_bwd", (gu,), first), ("ffn0_dw_down", (gu,), second)], "ffn0_dw_gu")
    return s


def kernel(x, *rest):
    nw = len(WEIGHT_NAMES)
    w = dict(zip(WEIGHT_NAMES, rest[:nw]))
    target = rest[nw]
    mom = dict(zip(WEIGHT_NAMES, rest[nw + 1:2 * nw + 1]))
    vel = dict(zip(WEIGHT_NAMES, rest[2 * nw + 1:3 * nw + 1]))
    cx, cy, cc = _position()
    my_chip = 2 * cx + cy
    d = x.shape[-1]
    cq = d // N_CHIPS

    ex = _Exchange(w, mom, vel, jnp.stack([my_chip, cc]).astype(jnp.int32))
    sched = _plan(ex)
    ex.cast(BIG)
    first_layer = ("a0_w_in", "a0_w_out")
    _comm_only([ex.gather_ici("a0_w_in")()], "gather_a0_ici")
    _comm_only([ex.gather_d2d("a0_w_in")()], "gather_a0_d2d")

    small_blk, small_counts = _pack_rows([w[n] for n in SMALL_SHARDED], cq)
    small_all = _allgather8(small_blk, "gather_small").reshape(N_CHIPS, 2, small_blk.shape[0], cq)[:, 0]
    small_parts = _unpack_rows(jnp.transpose(small_all, (1, 0, 2)), small_counts,
                               [(w[n].reshape(-1, cq).shape[0], N_CHIPS, cq) for n in SMALL_SHARDED])
    wts = ex.weights
    for n in REPLICATED:
        wts[n] = w[n].reshape(1, -1)
    for n, part in zip(SMALL_SHARDED, small_parts):
        if n == "b1_w_grp":
            ng, rq, cg = w[n].shape
            full = jnp.transpose(part.reshape(ng, rq, N_CHIPS, cg), (0, 2, 1, 3)).reshape(ng, N_CHIPS * rq, cg)
            wts[n] = full.astype(BF16)
        else:
            wts[n] = part.reshape(part.shape[0], d)

    _ACTIVE_SCHEDULE[0] = sched
    try:
        loss, dx, g = _device_step(x[0], target[0], wts, ex.grads)
    finally:
        _ACTIVE_SCHEDULE[0] = None
    assert not sched.hosts and not sched.posts, (sched.hosts, sched.posts)

    _comm_only([ex.reduced_halves("ffn0_w_gu", "ffn0_w_down")(), ex.sibling_halves(*first_layer)()], "reduce_tail_d2d")
    ex.adamw("ffn0_w_gu", "ffn0_w_down")()
    ex.add_halves(*first_layer)()
    _comm_only([ex.chip_sums(*first_layer)()], "reduce_tail_ici")
    ex.sum_chips(*first_layer)()
    _comm_only([ex.reduced_halves(*first_layer)()], "reduce_tail_halves")
    ex.adamw(*first_layer)()

    rep_rows = [g[n].reshape(-1, cq) for n in REPLICATED]
    by_chip = []
    for n in SMALL_SHARDED:
        gn = g[n]
        if n == "b1_w_grp":
            ng, rq, cg = w[n].shape
            by_chip.append(jnp.transpose(gn.reshape(ng, N_CHIPS, rq, cg), (1, 0, 2, 3)).reshape(N_CHIPS, ng * rq, cg))
        else:
            by_chip.append(jnp.transpose(gn.reshape(gn.shape[0], N_CHIPS, cq), (1, 0, 2)))
    shard_rows = jnp.concatenate(by_chip, axis=1)
    n_rep, n_shard = sum(r.shape[0] for r in rep_rows), shard_rows.shape[1]
    sm_blk = _pad_rows(jnp.concatenate(rep_rows + [shard_rows.reshape(N_CHIPS * n_shard, cq)], axis=0))
    sm_sum = _sum_devices(_allgather8(sm_blk, "gather_small_grads"), "sum_small_grads")
    mine = lax.dynamic_slice_in_dim(sm_sum, n_rep + my_chip * n_shard, n_shard, axis=0)

    out = ex.updates
    params, at = [], {0: 0, 1: 0}
    for block, names in ((0, REPLICATED), (1, SMALL_SHARDED)):
        for n in names:
            w2, m2, v2 = (a[n].reshape(-1, cq) for a in (w, mom, vel))
            params.append((w2, m2, v2, block, at[block]))
            at[block] += w2.shape[0]
    updated = _adamw_small([sm_sum, mine], params, "adamw_small")
    for n, res in zip(REPLICATED + SMALL_SHARDED, updated):
        out[n] = tuple(r.reshape(w[n].shape) for r in res)

    total = lax.psum(loss[0, 0], ("x", "y", "c"))
    grads, deltas, new_m, new_v = ([out[n][k] for n in WEIGHT_NAMES] for k in range(4))
    return (total, dx.reshape(x.shape), *grads, *deltas, *new_m, *new_v)
```

```python
import functools

import jax
import jax.numpy as jnp
from jax import lax
from jax.experimental import pallas as pl
from jax.experimental.pallas import tpu as pltpu

F32 = jnp.float32
BF16 = jnp.bfloat16

RMS_EPS = 1e-6
LN_EPS = 1e-5
POOL_WINDOWS = (2, 4, 8, 16)
SHORT_CONV_W = 3
CONF_CONV_W = 31
N_CHIPS = 4
N_DEV = 8

ADAM_LR = 0.001
ADAM_B1 = 0.9
ADAM_B2 = 0.999
ADAM_EPS = 1e-08
ADAM_WD = 0.01
ADAM_STEP = 10

V7X_VMEM_BYTES = 64 * 1024 * 1024
VMEM_LIMIT = V7X_VMEM_BYTES - 8 * 1024 * 1024
LANES = 128
POOL_HALO = 16
SCONV_HALO = 16
CONF_HALO = 32


def _params(*sem):
    return pltpu.CompilerParams(dimension_semantics=sem, vmem_limit_bytes=VMEM_LIMIT)


def _tile(n, pref, mult=8):
    t = min(n, pref)
    while t > mult and (n % t or t % mult):
        t -= mult
    assert n % t == 0 and t % mult == 0, (n, pref, mult)
    return t


def _sigmoid(x):
    return jax.nn.sigmoid(x)


def _dot(a, b):
    return jnp.dot(a, b, preferred_element_type=F32)


def _dot_nt(a, b):
    return lax.dot_general(a, b, (((1,), (1,)), ((), ())), preferred_element_type=F32)


def _dot_tn(a, b):
    return lax.dot_general(a, b, (((0,), (0,)), ((), ())), preferred_element_type=F32)


def _colsum(x):
    return jnp.sum(x, axis=0, keepdims=True)


def _rms_stats(x):
    return lax.rsqrt(jnp.mean(x * x, axis=-1, keepdims=True) + RMS_EPS)


def _rms_bwd(du, x, gain):
    r = _rms_stats(x)
    xhat = x * r
    gdy = du * gain
    dx = r * (gdy - xhat * jnp.mean(gdy * xhat, axis=-1, keepdims=True))
    return dx, _colsum(du * xhat)


class _Task:
    def __init__(self, ins, out_shapes, aliases, sems, start, wait, done):
        self.ins, self.out_shapes, self.aliases, self.sems = list(ins), list(out_shapes), dict(aliases), list(sems)
        self.start, self.wait, self.done = start, wait, done


class _Schedule:
    def __init__(self):
        self.hosts, self.posts = {}, {}

    def host(self, kernel_name, *make_tasks):
        self.hosts.setdefault(kernel_name, []).extend(make_tasks)

    def post(self, kernel_name, *thunks):
        self.posts.setdefault(kernel_name, []).extend(thunks)

    def tasks_for(self, kernel_name):
        return [make() for make in self.hosts.pop(kernel_name, ())]

    def finished(self, kernel_name):
        for thunk in self.posts.pop(kernel_name, ()):
            thunk()


_ACTIVE_SCHEDULE = [None]


def _hosted(body, name, **kw):
    def run(*args):
        sched = _ACTIVE_SCHEDULE[0]
        tasks = sched.tasks_for(name) if sched is not None else []
        out = _call_with_tasks(body, name, tasks, kw, args) if tasks else pl.pallas_call(body, name=name, **kw)(*args)
        if sched is not None:
            sched.finished(name)
        return out

    return run


def _call_with_tasks(body, name, tasks, kw, args):
    grid = tuple(kw.get("grid", ()))
    single = not isinstance(kw["out_shape"], (list, tuple))
    out_shape = [kw["out_shape"]] if single else list(kw["out_shape"])
    out_specs = [kw["out_specs"]] if single else list(kw["out_specs"])
    scratch = list(kw.get("scratch_shapes", ()))
    n_in, n_out, n_scr = len(args), len(out_shape), len(scratch)
    t_in = [a for t in tasks for a in t.ins]
    t_out = [o for t in tasks for o in t.out_shapes]
    t_sem = [s for t in tasks for s in t.sems]
    aliases, at_in, at_out = {}, n_in, n_out
    for t in tasks:
        for i, o in t.aliases.items():
            aliases[at_in + i] = at_out + o
        at_in += len(t.ins)
        at_out += len(t.out_shapes)

    def wrapped(*refs):
        a = n_in
        b = a + len(t_in)
        c = b + n_out
        d = c + len(t_out)
        e = d + n_scr
        ins, tins, outs, touts, scr, tsems = refs[:a], refs[a:b], refs[b:c], refs[c:d], refs[d:e], refs[e:]
        views, i0, o0, s0 = [], 0, 0, 0
        for t in tasks:
            views.append((tins[i0:i0 + len(t.ins)], touts[o0:o0 + len(t.out_shapes)], tsems[s0:s0 + len(t.sems)]))
            i0, o0, s0 = i0 + len(t.ins), o0 + len(t.out_shapes), s0 + len(t.sems)

        def start_all():
            for t, v in zip(tasks, views):
                t.start(*v)

        def wait_all():
            for t, v in zip(tasks, views):
                t.wait(*v)

        if grid:
            first = functools.reduce(jnp.logical_and, [pl.program_id(i) == 0 for i in range(len(grid))])
            last = functools.reduce(jnp.logical_and, [pl.program_id(i) == grid[i] - 1 for i in range(len(grid))])
            pl.when(first)(start_all)
            body(*ins, *outs, *scr)
            pl.when(last)(wait_all)
        else:
            start_all()
            body(*ins, *outs, *scr)
            wait_all()

    res = pl.pallas_call(
        wrapped, name=name, grid=grid,
        in_specs=list(kw["in_specs"]) + [ANY] * len(t_in), out_specs=out_specs + [ANY] * len(t_out),
        out_shape=out_shape + t_out, scratch_shapes=scratch + t_sem, input_output_aliases=aliases,
        compiler_params=pltpu.CompilerParams(dimension_semantics=("arbitrary",) * len(grid), vmem_limit_bytes=VMEM_LIMIT),
    )(*args, *t_in)
    res = list(res)
    own, rest = res[:n_out], res[n_out:]
    for t in tasks:
        t.done(rest[:len(t.out_shapes)])
        rest = rest[len(t.out_shapes):]
    return own[0] if single else own


def _comm_only(tasks, name):
    _call_with_tasks(lambda: None, name, tasks, dict(grid=(), in_specs=[], out_specs=[], out_shape=[]), ())


def _rms_fwd(h, gain, name):
    s, d = h.shape
    tm = _tile(s, 512)

    def body(h_ref, g_ref, u_ref):
        x = h_ref[...]
        u_ref[...] = (x * _rms_stats(x) * g_ref[...]).astype(u_ref.dtype)

    return _hosted(
        body, name=name, grid=(s // tm,),
        in_specs=[pl.BlockSpec((tm, d), lambda m: (m, 0)), pl.BlockSpec((1, d), lambda m: (0, 0))],
        out_specs=pl.BlockSpec((tm, d), lambda m: (m, 0)),
        out_shape=jax.ShapeDtypeStruct((s, d), BF16),
        compiler_params=_params("parallel"),
    )(h, gain)


def _mm_col(a, w, bias, name):
    s, k = a.shape
    nsh, _, ns = w.shape
    tm = _tile(s, 512)
    has_bias = bias is not None

    def body(a_ref, w_ref, *rest):
        o_ref = rest[-1]
        acc = _dot(a_ref[...], w_ref[...])
        if has_bias:
            acc = acc + rest[0][...]
        o_ref[...] = acc.astype(o_ref.dtype)

    in_specs = [pl.BlockSpec((tm, k), lambda j, m: (m, 0)), pl.BlockSpec((None, k, ns), lambda j, m: (j, 0, 0))]
    args = [a, w]
    if has_bias:
        in_specs.append(pl.BlockSpec((1, ns), lambda j, m: (0, j)))
        args.append(bias)
    return _hosted(
        body, name=name, grid=(nsh, s // tm), in_specs=in_specs,
        out_specs=pl.BlockSpec((tm, ns), lambda j, m: (m, j)),
        out_shape=jax.ShapeDtypeStruct((s, nsh * ns), BF16),
        compiler_params=_params("parallel", "parallel"),
    )(*args)


def _mm_row(a, w, res, bias, name):
    s = a.shape[0]
    k, n = w.shape
    tm = _tile(s, 512)
    has_bias = bias is not None

    def body(a_ref, w_ref, res_ref, *rest):
        o_ref = rest[-1]
        y = res_ref[...] + _dot(a_ref[...], w_ref[...])
        if has_bias:
            y = y + rest[0][...]
        o_ref[...] = y

    in_specs = [pl.BlockSpec((tm, k), lambda m: (m, 0)), pl.BlockSpec((k, n), lambda m: (0, 0)),
                pl.BlockSpec((tm, n), lambda m: (m, 0))]
    args = [a, w, res]
    if has_bias:
        in_specs.append(pl.BlockSpec((1, n), lambda m: (0, 0)))
        args.append(bias)
    return _hosted(
        body, name=name, grid=(s // tm,), in_specs=in_specs,
        out_specs=pl.BlockSpec((tm, n), lambda m: (m, 0)),
        out_shape=jax.ShapeDtypeStruct((s, n), F32),
        compiler_params=_params("parallel"),
    )(*args)


def _mm_nt_row(dy, w, name):
    s, n = dy.shape
    k = w.shape[0]
    tm = _tile(s, 512)

    def body(dy_ref, w_ref, o_ref):
        o_ref[...] = _dot_nt(dy_ref[...].astype(BF16), w_ref[...])

    return _hosted(
        body, name=name, grid=(s // tm,),
        in_specs=[pl.BlockSpec((tm, n), lambda m: (m, 0)), pl.BlockSpec((k, n), lambda m: (0, 0))],
        out_specs=pl.BlockSpec((tm, k), lambda m: (m, 0)),
        out_shape=jax.ShapeDtypeStruct((s, k), F32),
        compiler_params=_params("parallel"),
    )(dy, w)


def _ffn_up(u, w, name):
    s, d = u.shape
    _, _, ns = w.shape
    tm = _tile(s, 512)

    def body(u_ref, wg_ref, wu_ref, act_ref, s1_ref, q1_ref):
        x = u_ref[...]
        g = _dot(x, wg_ref[...])
        up = _dot(x, wu_ref[...])
        sg = _sigmoid(g)
        s1 = g * sg
        act_ref[...] = (s1 * up).astype(act_ref.dtype)
        s1_ref[...] = s1.astype(s1_ref.dtype)
        q1_ref[...] = (up * sg * (1.0 + g * (1.0 - sg))).astype(q1_ref.dtype)

    out = pl.BlockSpec((tm, ns), lambda j, m: (m, j))
    return _hosted(
        body, name=name, grid=(2, s // tm),
        in_specs=[pl.BlockSpec((tm, d), lambda j, m: (m, 0)), pl.BlockSpec((None, d, ns), lambda j, m: (j, 0, 0)),
                  pl.BlockSpec((None, d, ns), lambda j, m: (j + 2, 0, 0))],
        out_specs=[out, out, out],
        out_shape=[jax.ShapeDtypeStruct((s, 2 * ns), BF16)] * 3,
        compiler_params=_params("parallel", "parallel"),
    )(u, w, w)


def _ffn_down_bwd(dh, w, s1, q1, name):
    s, d = dh.shape
    f = w.shape[0]
    tm = _tile(s, 256)

    def body(dh_ref, w_ref, s1_ref, q1_ref, o_ref):
        da = _dot_nt(dh_ref[...].astype(BF16), w_ref[...])
        o_ref[:, :f] = (da * q1_ref[...].astype(F32)).astype(o_ref.dtype)
        o_ref[:, f:] = (da * s1_ref[...].astype(F32)).astype(o_ref.dtype)

    return _hosted(
        body, name=name, grid=(s // tm,),
        in_specs=[pl.BlockSpec((tm, d), lambda m: (m, 0)), pl.BlockSpec((f, d), lambda m: (0, 0)),
                  pl.BlockSpec((tm, f), lambda m: (m, 0)), pl.BlockSpec((tm, f), lambda m: (m, 0))],
        out_specs=pl.BlockSpec((tm, 2 * f), lambda m: (m, 0)),
        out_shape=jax.ShapeDtypeStruct((s, 2 * f), BF16),
        compiler_params=_params("parallel"),
    )(dh, w, s1, q1)


def _mm_nt_col_rms_bwd(dy, w, h, gain, dh, name):
    s = dy.shape[0]
    nsh, k, ns = w.shape
    tm = _tile(s, 256)

    def body(dy_ref, w_ref, h_ref, g_ref, dh_ref, o_ref, dg_ref):
        du = _dot_nt(dy_ref[:, :ns], w_ref[0])
        for j in range(1, nsh):
            du = du + _dot_nt(dy_ref[:, j * ns:(j + 1) * ns], w_ref[j])
        dx, dg = _rms_bwd(du, h_ref[...], g_ref[...])
        o_ref[...] = dh_ref[...] + dx
        _accumulate(dg_ref, dg, pl.program_id(0) == 0)

    return _hosted(
        body, name=name, grid=(s // tm,),
        in_specs=[pl.BlockSpec((tm, nsh * ns), lambda m: (m, 0)), pl.BlockSpec((nsh, k, ns), lambda m: (0, 0, 0)),
                  pl.BlockSpec((tm, k), lambda m: (m, 0)), pl.BlockSpec((1, k), lambda m: (0, 0)),
                  pl.BlockSpec((tm, k), lambda m: (m, 0))],
        out_specs=[pl.BlockSpec((tm, k), lambda m: (m, 0)), pl.BlockSpec((1, k), lambda m: (0, 0))],
        out_shape=[jax.ShapeDtypeStruct((s, k), F32), jax.ShapeDtypeStruct((1, k), F32)],
        compiler_params=_params("arbitrary"),
    )(dy, w, h, gain, dh)


def _mm_tn(a, dy, nsh, name):
    s, k = a.shape
    ns = dy.shape[1] // nsh
    tm = _tile(s, 1024)
    tk = _tile(k, 1408, LANES)
    nk, nm = k // tk, s // tm

    def body(a_ref, dy_ref, o_ref, acc_ref):
        m = pl.program_id(2)
        part = _dot_tn(a_ref[...], dy_ref[...].astype(BF16))

        @pl.when(m == 0)
        def _():
            acc_ref[...] = part

        @pl.when(m > 0)
        def _():
            acc_ref[...] += part

        @pl.when(m == nm - 1)
        def _():
            o_ref[...] = acc_ref[...].astype(o_ref.dtype)

    return _hosted(
        body, name=name, grid=(nsh, nk, nm),
        in_specs=[pl.BlockSpec((tm, tk), lambda j, kk, m: (m, kk)), pl.BlockSpec((tm, ns), lambda j, kk, m: (m, j))],
        out_specs=pl.BlockSpec((None, tk, ns), lambda j, kk, m: (j, kk, 0)),
        out_shape=jax.ShapeDtypeStruct((nsh, k, ns), BF16),
        scratch_shapes=[pltpu.VMEM((tk, ns), F32)],
        compiler_params=_params("parallel", "parallel", "arbitrary"),
    )(a, dy)


def _main_spec(tm, w):
    return pl.BlockSpec((tm, w), lambda m: (m, 0))


def _before_spec(tm, hb, w):
    return pl.BlockSpec((hb, w), lambda m: (jnp.maximum(m * (tm // hb) - 1, 0), 0))


def _after_spec(tm, hb, w, s):
    return pl.BlockSpec((hb, w), lambda m: (jnp.minimum((m + 1) * (tm // hb), s // hb - 1), 0))


def _row_spec(w, rows=1):
    return pl.BlockSpec((rows, w), lambda m: (0, 0))


CHUNK_LANES = 4 * LANES
CHUNK_ROWS = 32


def _build_shifts(ext8_ref):
    n = ext8_ref.shape[1] - 8
    for r in range(1, 8):
        ext8_ref[r, pl.ds(0, n), :] = ext8_ref[0, pl.ds(r, n), :]


def _shifted(ext8_ref, shift, r0, rows, cols):
    return ext8_ref[shift % 8, pl.ds(pl.multiple_of(shift - shift % 8 + r0, 8), rows), cols]


def _lane_chunk(i):
    return pl.ds(pl.multiple_of(i * CHUNK_LANES, CHUNK_LANES), CHUNK_LANES)


def _sum_terms(terms, ways=4):
    accs = []
    for i, t in enumerate(terms):
        if i < ways:
            accs.append(t)
        else:
            accs[i % ways] = accs[i % ways] + t
    while len(accs) > 1:
        accs = [accs[i] + accs[i + 1] if i + 1 < len(accs) else accs[i] for i in range(0, len(accs), 2)]
    return accs[0]


def _accumulate(ref, val, first):
    @pl.when(first)
    def _():
        ref[...] = val

    @pl.when(jnp.logical_not(first))
    def _():
        ref[...] += val


def _sconv_taps(zext_ref, cw_ref, tm, base):
    out = cw_ref[2:3, :] * zext_ref[pl.ds(base, tm), :]
    out = out + cw_ref[1:2, :] * zext_ref[pl.ds(base - 1, tm), :]
    return out + cw_ref[0:1, :] * zext_ref[pl.ds(base - 2, tm), :]


def _sconv_fill_z(zext_ref, main_ref, before_ref, d, m):
    hb = SCONV_HALO
    zb = before_ref[:, d:2 * d].astype(F32) * before_ref[:, 2 * d:].astype(F32)
    zext_ref[pl.ds(0, hb), :] = jnp.where(m > 0, zb, 0.0)
    zext_ref[pl.ds(hb, main_ref.shape[0]), :] = main_ref[:, d:2 * d].astype(F32) * main_ref[:, 2 * d:].astype(F32)


def _sconv_fwd(bcv, cw, name):
    s, d3 = bcv.shape
    d = d3 // 3
    tm = _tile(s, 512, SCONV_HALO)

    def body(main_ref, before_ref, cw_ref, p_ref, zext_ref):
        m = pl.program_id(0)
        _sconv_fill_z(zext_ref, main_ref, before_ref, d, m)
        zc = _sconv_taps(zext_ref, cw_ref, tm, SCONV_HALO)
        p_ref[...] = (main_ref[:, :d].astype(F32) * zc).astype(p_ref.dtype)

    return _hosted(
        body, name=name, grid=(s // tm,),
        in_specs=[_main_spec(tm, d3), _before_spec(tm, SCONV_HALO, d3), _row_spec(d, SHORT_CONV_W)],
        out_specs=_main_spec(tm, d),
        out_shape=jax.ShapeDtypeStruct((s, d), BF16),
        scratch_shapes=[pltpu.VMEM((tm + SCONV_HALO, d), F32)],
        compiler_params=_params("parallel"),
    )(bcv, bcv, cw)


def _sconv_bwd(dp, bcv, cw, name):
    s, d3 = bcv.shape
    d = d3 // 3
    tm = _tile(s, 512, SCONV_HALO)
    nm = s // tm
    ha = 8

    def body(dp_ref, dpa_ref, main_ref, before_ref, after_ref, cw_ref, o_ref, dcw_ref, zext_ref, dext_ref):
        m = pl.program_id(0)
        _sconv_fill_z(zext_ref, main_ref, before_ref, d, m)
        zc = _sconv_taps(zext_ref, cw_ref, tm, SCONV_HALO)
        dp_t = dp_ref[...]
        o_ref[:, :d] = (dp_t * zc).astype(o_ref.dtype)
        dzc = dp_t * main_ref[:, :d].astype(F32)
        dext_ref[pl.ds(0, tm), :] = dzc
        dza = dpa_ref[...] * after_ref[:, :d].astype(F32)[0:ha]
        dext_ref[pl.ds(tm, ha), :] = jnp.where(m < nm - 1, dza, 0.0)
        dz = cw_ref[2:3, :] * dzc
        dz = dz + cw_ref[1:2, :] * dext_ref[pl.ds(1, tm), :]
        dz = dz + cw_ref[0:1, :] * dext_ref[pl.ds(2, tm), :]
        o_ref[:, d:2 * d] = (dz * main_ref[:, 2 * d:].astype(F32)).astype(o_ref.dtype)
        o_ref[:, 2 * d:] = (dz * main_ref[:, d:2 * d].astype(F32)).astype(o_ref.dtype)

        @pl.when(m == 0)
        def _():
            dcw_ref[...] = jnp.zeros_like(dcw_ref)

        for kk in range(SHORT_CONV_W):
            zs = zext_ref[pl.ds(SCONV_HALO - 2 + kk, tm), :]
            dcw_ref[kk:kk + 1, :] += _colsum(dzc * zs)

    return _hosted(
        body, name=name, grid=(nm,),
        in_specs=[_main_spec(tm, d), _after_spec(tm, ha, d, s), _main_spec(tm, d3), _before_spec(tm, SCONV_HALO, d3),
                  _after_spec(tm, SCONV_HALO, d3, s), _row_spec(d, SHORT_CONV_W)],
        out_specs=[_main_spec(tm, d3), _row_spec(d, 8)],
        out_shape=[jax.ShapeDtypeStruct((s, d3), BF16), jax.ShapeDtypeStruct((8, d), F32)],
        scratch_shapes=[pltpu.VMEM((tm + SCONV_HALO, d), F32), pltpu.VMEM((tm + ha, d), F32)],
        compiler_params=_params("arbitrary"),
    )(dp, dp, bcv, bcv, bcv, cw)


def _pool_counts(t0, tm, w):
    t = t0 + lax.broadcasted_iota(jnp.int32, (tm, 1), 0)
    return jnp.minimum(t + 1, w).astype(F32)


def _pool_fwd(h, gain, wg, scale, name):
    s, d = h.shape
    ng, cg, _ = wg.shape
    tm = _tile(s, 512, POOL_HALO)

    def body(h_ref, hb_ref, g_ref, wg_ref, sc_ref, o_ref, mx_ref, uext_ref):
        m = pl.program_id(0)
        x = h_ref[...]
        gain_row = g_ref[...]
        xb = hb_ref[...]
        uext_ref[pl.ds(0, POOL_HALO), :] = jnp.where(m > 0, xb * _rms_stats(xb) * gain_row, 0.0)
        uext_ref[pl.ds(POOL_HALO, tm), :] = x * _rms_stats(x) * gain_row
        for gi, win in enumerate(POOL_WINDOWS):
            cols = pl.ds(gi * cg, cg)
            u_g = uext_ref[pl.ds(POOL_HALO, tm), cols]
            acc = u_g
            for i in range(1, win):
                acc = acc + uext_ref[pl.ds(POOL_HALO - i, tm), cols]
            mixed = (acc / _pool_counts(m * tm, tm, win) - u_g).astype(BF16)
            mx_ref[:, cols] = mixed
            o_ref[:, cols] = x[:, gi * cg:(gi + 1) * cg] + _dot(mixed, wg_ref[gi]) * sc_ref[:, cols]

    return _hosted(
        body, name=name, grid=(s // tm,),
        in_specs=[_main_spec(tm, d), _before_spec(tm, POOL_HALO, d), _row_spec(d),
                  pl.BlockSpec((ng, cg, cg), lambda m: (0, 0, 0)), _row_spec(d)],
        out_specs=[_main_spec(tm, d), _main_spec(tm, d)],
        out_shape=[jax.ShapeDtypeStruct((s, d), F32), jax.ShapeDtypeStruct((s, d), BF16)],
        scratch_shapes=[pltpu.VMEM((tm + POOL_HALO, d), F32)],
        compiler_params=_params("parallel"),
    )(h, h, gain, wg, scale)


def _pool_bwd_mm(dh, mixed, wg, scale, name):
    s, d = dh.shape
    ng, cg, _ = wg.shape
    tm = _tile(s, 512)

    def body(dh_ref, mx_ref, wg_ref, sc_ref, dmx_ref, dwg_ref, dsc_ref):
        first = pl.program_id(0) == 0
        for gi in range(ng):
            cols = pl.ds(gi * cg, cg)
            dh_g = dh_ref[:, cols]
            mixed = mx_ref[:, cols]
            w_g = wg_ref[gi]
            dy = (dh_g * sc_ref[:, cols]).astype(BF16)
            dmx_ref[:, cols] = _dot_nt(dy, w_g)
            _accumulate(dsc_ref.at[:, cols], _colsum(dh_g * _dot(mixed, w_g)), first)
            _accumulate(dwg_ref.at[gi], _dot_tn(mixed, dy), first)

    return _hosted(
        body, name=name, grid=(s // tm,),
        in_specs=[_main_spec(tm, d), _main_spec(tm, d), pl.BlockSpec((ng, cg, cg), lambda m: (0, 0, 0)), _row_spec(d)],
        out_specs=[_main_spec(tm, d), pl.BlockSpec((ng, cg, cg), lambda m: (0, 0, 0)), _row_spec(d)],
        out_shape=[jax.ShapeDtypeStruct((s, d), F32), jax.ShapeDtypeStruct((ng, cg, cg), F32),
                   jax.ShapeDtypeStruct((1, d), F32)],
        compiler_params=_params("arbitrary"),
    )(dh, mixed, wg, scale)


def _pool_bwd_rms(dmixed, h, gain, dh, name):
    s, d = h.shape
    cg = d // len(POOL_WINDOWS)
    tm = _tile(s, 512, POOL_HALO)
    nm = s // tm

    def body(dmx_ref, dmxa_ref, h_ref, g_ref, dh_ref, o_ref, dg_ref, eext_ref, du_ref):
        m = pl.program_id(0)
        for gi, win in enumerate(POOL_WINDOWS):
            cols = pl.ds(gi * cg, cg)
            dmx = dmx_ref[:, cols]
            eext_ref[pl.ds(0, tm), cols] = dmx / _pool_counts(m * tm, tm, win)
            ea = dmxa_ref[:, cols] / _pool_counts((m + 1) * tm, POOL_HALO, win)
            eext_ref[pl.ds(tm, POOL_HALO), cols] = jnp.where(m < nm - 1, ea, 0.0)
            acc = -dmx
            for i in range(win):
                acc = acc + eext_ref[pl.ds(i, tm), cols]
            du_ref[:, cols] = acc
        dx, dg = _rms_bwd(du_ref[...], h_ref[...], g_ref[...])
        o_ref[...] = dh_ref[...] + dx
        _accumulate(dg_ref, dg, m == 0)

    return _hosted(
        body, name=name, grid=(nm,),
        in_specs=[_main_spec(tm, d), _after_spec(tm, POOL_HALO, d, s), _main_spec(tm, d), _row_spec(d), _main_spec(tm, d)],
        out_specs=[_main_spec(tm, d), _row_spec(d)],
        out_shape=[jax.ShapeDtypeStruct((s, d), F32), jax.ShapeDtypeStruct((1, d), F32)],
        scratch_shapes=[pltpu.VMEM((tm + POOL_HALO, d), F32), pltpu.VMEM((tm, d), F32)],
        compiler_params=_params("arbitrary"),
    )(dmixed, dmixed, h, gain, dh)


def _conf_fill_h(hext_ref, main_ref, before_ref, d, m):
    hb = before_ref[:, :d].astype(F32) * _sigmoid(before_ref[:, d:].astype(F32))
    hext_ref[pl.ds(0, CONF_HALO), :] = jnp.where(m > 0, hb, 0.0)
    hext_ref[pl.ds(CONF_HALO, main_ref.shape[0]), :] = main_ref[:, :d].astype(F32) * _sigmoid(main_ref[:, d:].astype(F32))


def _layernorm_parts(hc, g, b):
    mu = jnp.mean(hc, axis=-1, keepdims=True)
    xc = hc - mu
    rs = lax.rsqrt(jnp.mean(xc * xc, axis=-1, keepdims=True) + LN_EPS)
    xhat = xc * rs
    return xhat, rs, xhat * g + b


def _conf_mid_fwd(ag, dw, b_dw, ln_g, ln_b, name):
    s, d2 = ag.shape
    d = d2 // 2
    tm = _tile(s, 256, CONF_HALO)
    base = CONF_HALO - (CONF_CONV_W - 1)

    def body(main_ref, before_ref, dw_ref, bdw_ref, g_ref, b_ref, s_ref, hc_ref, hext_ref):
        m = pl.program_id(0)
        _conf_fill_h(hext_ref.at[0], main_ref, before_ref, d, m)
        _build_shifts(hext_ref)
        row_chunks = tm // CHUNK_ROWS

        def conv_chunk(i, carry):
            cols = _lane_chunk(i // row_chunks)
            r0 = pl.multiple_of((i % row_chunks) * CHUNK_ROWS, CHUNK_ROWS)
            taps = (dw_ref[kk:kk + 1, cols] * _shifted(hext_ref, base + kk, r0, CHUNK_ROWS, cols) for kk in range(CONF_CONV_W))
            hc_ref[pl.ds(r0, CHUNK_ROWS), cols] = bdw_ref[:, cols] + _sum_terms(taps, ways=1)
            return carry

        lax.fori_loop(0, row_chunks * (d // CHUNK_LANES), conv_chunk, 0)
        _, _, l = _layernorm_parts(hc_ref[...], g_ref[...], b_ref[...])
        s_ref[...] = (l * _sigmoid(l)).astype(s_ref.dtype)

    return _hosted(
        body, name=name, grid=(s // tm,),
        in_specs=[_main_spec(tm, d2), _before_spec(tm, CONF_HALO, d2), _row_spec(d, CONF_CONV_W), _row_spec(d),
                  _row_spec(d), _row_spec(d)],
        out_specs=[_main_spec(tm, d), _main_spec(tm, d)],
        out_shape=[jax.ShapeDtypeStruct((s, d), BF16), jax.ShapeDtypeStruct((s, d), F32)],
        scratch_shapes=[pltpu.VMEM((8, tm + CONF_HALO, d), F32)],
        compiler_params=_params("parallel"),
    )(ag, ag, dw, b_dw, ln_g, ln_b)


def _conf_out_bwd(dh, w, hc, ln_g, ln_b, name):
    s, d = dh.shape
    tm = _tile(s, 256)

    def body(dh_ref, w_ref, hc_ref, g_ref, b_ref, o_ref, dg_ref, db_ref, dbo_ref):
        first = pl.program_id(0) == 0
        dh_t = dh_ref[...]
        ds = _dot_nt(dh_t.astype(BF16), w_ref[...])
        xhat, rs, l = _layernorm_parts(hc_ref[...], g_ref[...], b_ref[...])
        sg = _sigmoid(l)
        dl = ds * sg * (1.0 + l * (1.0 - sg))
        dxh = dl * g_ref[...]
        o_ref[...] = rs * (dxh - jnp.mean(dxh, axis=-1, keepdims=True)
                           - xhat * jnp.mean(dxh * xhat, axis=-1, keepdims=True))
        _accumulate(dg_ref, _colsum(dl * xhat), first)
        _accumulate(db_ref, _colsum(dl), first)
        _accumulate(dbo_ref, _colsum(dh_t), first)

    return _hosted(
        body, name=name, grid=(s // tm,),
        in_specs=[_main_spec(tm, d), pl.BlockSpec((d, d), lambda m: (0, 0)), _main_spec(tm, d), _row_spec(d), _row_spec(d)],
        out_specs=[_main_spec(tm, d), _row_spec(d), _row_spec(d), _row_spec(d)],
        out_shape=[jax.ShapeDtypeStruct((s, d), F32)] + [jax.ShapeDtypeStruct((1, d), F32)] * 3,
        compiler_params=_params("arbitrary"),
    )(dh, w, hc, ln_g, ln_b)


def _conf_mid_bwd(dhc, ag, dw, name):
    s, d2 = ag.shape
    d = d2 // 2
    tm = _tile(s, 256, CONF_HALO)
    nm = s // tm
    kw = CONF_CONV_W
    base = CONF_HALO - (kw - 1)

    def body(dhc_ref, dhca_ref, main_ref, before_ref, dw_ref, o_ref, ddw_ref, dbdw_ref, dbpw_ref, hext_ref, dext_ref):
        m = pl.program_id(0)
        first = m == 0
        _conf_fill_h(hext_ref.at[0], main_ref, before_ref, d, m)
        _build_shifts(hext_ref)
        dext_ref[0, pl.ds(0, tm), :] = dhc_ref[...]
        dext_ref[0, pl.ds(tm, CONF_HALO), :] = jnp.where(m < nm - 1, dhca_ref[...], 0.0)
        _build_shifts(dext_ref)

        @pl.when(first)
        def _():
            ddw_ref[...] = jnp.zeros_like(ddw_ref)
            dbdw_ref[...] = jnp.zeros_like(dbdw_ref)
            dbpw_ref[...] = jnp.zeros_like(dbpw_ref)

        zero = jnp.zeros((8, CHUNK_LANES), F32)
        tap_group = 8

        def fold(x):
            return functools.reduce(lambda p, q: p + q, [x[i:i + 8] for i in range(0, CHUNK_ROWS, 8)])

        def lane_chunk(ci, carry):
            cols = _lane_chunk(ci)
            gate_cols = pl.ds(pl.multiple_of(d + ci * CHUNK_LANES, CHUNK_LANES), CHUNK_LANES)

            def through_conv(ri, sums):
                r0 = pl.multiple_of(ri * CHUNK_ROWS, CHUNK_ROWS)
                rows = pl.ds(r0, CHUNK_ROWS)
                dhh = _sum_terms((dw_ref[kk:kk + 1, cols] * _shifted(dext_ref, kw - 1 - kk, r0, CHUNK_ROWS, cols)
                                  for kk in range(kw)), ways=1)
                a = main_ref[rows, cols].astype(F32)
                sg = _sigmoid(main_ref[rows, gate_cols].astype(F32))
                da = dhh * sg
                dgate = dhh * a * sg * (1.0 - sg)
                o_ref[rows, cols] = da.astype(o_ref.dtype)
                o_ref[rows, gate_cols] = dgate.astype(o_ref.dtype)
                return sums[0] + fold(da), sums[1] + fold(dgate), sums[2] + fold(dext_ref[0, rows, cols])

            sum_da, sum_dgate, sum_dhc = lax.fori_loop(0, tm // CHUNK_ROWS, through_conv, (zero, zero, zero))
            dbdw_ref[:, cols] += _colsum(sum_dhc)
            dbpw_ref[:, cols] += _colsum(sum_da)
            dbpw_ref[:, gate_cols] += _colsum(sum_dgate)

            for k0 in range(0, kw, tap_group):
                group = range(k0, min(k0 + tap_group, kw))

                def tap_gradients(ri, accs, group=group):
                    for sub in range(0, CHUNK_ROWS, 8):
                        r0 = pl.multiple_of(ri * CHUNK_ROWS + sub, 8)
                        dhc_c = dext_ref[0, pl.ds(r0, 8), cols]
                        accs = tuple(acc + dhc_c * _shifted(hext_ref, base + kk, r0, 8, cols) for kk, acc in zip(group, accs))
                    return accs

                accs = lax.fori_loop(0, tm // CHUNK_ROWS, tap_gradients, (zero,) * len(group))
                for kk, acc in zip(group, accs):
                    ddw_ref[kk:kk + 1, cols] += _colsum(acc)
            return carry

        lax.fori_loop(0, d // CHUNK_LANES, lane_chunk, 0)

    return _hosted(
        body, name=name, grid=(nm,),
        in_specs=[_main_spec(tm, d), _after_spec(tm, CONF_HALO, d, s), _main_spec(tm, d2), _before_spec(tm, CONF_HALO, d2),
                  _row_spec(d, kw)],
        out_specs=[_main_spec(tm, d2), _row_spec(d, 32), _row_spec(d), _row_spec(d2)],
        out_shape=[jax.ShapeDtypeStruct((s, d2), BF16), jax.ShapeDtypeStruct((32, d), F32),
                   jax.ShapeDtypeStruct((1, d), F32), jax.ShapeDtypeStruct((1, d2), F32)],
        scratch_shapes=[pltpu.VMEM((8, tm + CONF_HALO, d), F32), pltpu.VMEM((8, tm + CONF_HALO, d), F32)],
        compiler_params=_params("arbitrary"),
    )(dhc, dhc, ag, ag, dw)


def _loss_head(h, gain, target, name):
    s, d = h.shape
    tm = _tile(s, 512)

    def body(h_ref, g_ref, t_ref, loss_ref, dh_ref, dg_ref):
        first = pl.program_id(0) == 0
        x = h_ref[...]
        err = x * _rms_stats(x) * g_ref[...] - t_ref[...]
        part = 0.5 * jnp.sum(jnp.mean(err * err, axis=-1, keepdims=True), axis=0, keepdims=True)
        dx, dg = _rms_bwd(err * (1.0 / d), x, g_ref[...])
        dh_ref[...] = dx
        _accumulate(loss_ref, part, first)
        _accumulate(dg_ref, dg, first)

    return _hosted(
        body, name=name, grid=(s // tm,),
        in_specs=[_main_spec(tm, d), _row_spec(d), _main_spec(tm, d)],
        out_specs=[pl.BlockSpec((1, 1), lambda m: (0, 0)), _main_spec(tm, d), _row_spec(d)],
        out_shape=[jax.ShapeDtypeStruct((1, 1), F32), jax.ShapeDtypeStruct((s, d), F32), jax.ShapeDtypeStruct((1, d), F32)],
        compiler_params=_params("arbitrary"),
    )(h, gain, target)


def _ffn_fwd(h, wts, i):
    u = _rms_fwd(h, wts[f"ln2_{i}"], f"ffn{i}_rms")
    act, s1, q1 = _ffn_up(u, wts[f"ffn{i}_w_gu"], f"ffn{i}_up")
    h_new = _mm_row(act, wts[f"ffn{i}_w_down"], h, None, f"ffn{i}_down")
    return h_new, (h, u, act, s1, q1)


def _ffn_bwd(dh, saved, wts, i, g):
    h, u, act, s1, q1 = saved
    dgu = _ffn_down_bwd(dh, wts[f"ffn{i}_w_down"], s1, q1, f"ffn{i}_down_bwd")
    g[f"ffn{i}_w_down"] = _mm_tn(act, dh, 1, f"ffn{i}_dw_down")
    g[f"ffn{i}_w_gu"] = _mm_tn(u, dgu, N_CHIPS, f"ffn{i}_dw_gu")
    dh_new, g[f"ln2_{i}"] = _mm_nt_col_rms_bwd(dgu, wts[f"ffn{i}_w_gu"], h, wts[f"ln2_{i}"], dh, f"ffn{i}_up_bwd")
    return dh_new


def _device_step(x, target, wts, g=None):
    g = {} if g is None else g
    saved = {}
    h = x

    def short_conv_fwd(h, i):
        u = _rms_fwd(h, wts[f"ln1_{i}"], f"a{i}_rms")
        bcv = _mm_col(u, wts[f"a{i}_w_in"], None, f"a{i}_in")
        p = _sconv_fwd(bcv, wts[f"a{i}_conv"], f"a{i}_conv")
        return _mm_row(p, wts[f"a{i}_w_out"], h, None, f"a{i}_out"), (h, u, bcv, p)

    def short_conv_bwd(dh, sv, i):
        h, u, bcv, p = sv
        dp = _mm_nt_row(dh, wts[f"a{i}_w_out"], f"a{i}_out_bwd")
        dbcv, dcw = _sconv_bwd(dp, bcv, wts[f"a{i}_conv"], f"a{i}_conv_bwd")
        g[f"a{i}_conv"] = dcw[:SHORT_CONV_W]
        g[f"a{i}_w_in"] = _mm_tn(u, dbcv, N_CHIPS, f"a{i}_dw_in")
        g[f"a{i}_w_out"] = _mm_tn(p, dh, 1, f"a{i}_dw_out")
        dh, g[f"ln1_{i}"] = _mm_nt_col_rms_bwd(dbcv, wts[f"a{i}_w_in"], h, wts[f"ln1_{i}"], dh, f"a{i}_in_bwd")
        return dh

    h, saved["a0"] = short_conv_fwd(h, 0)
    h, saved["f0"] = _ffn_fwd(h, wts, 0)

    h_in = h
    h, mixed = _pool_fwd(h, wts["ln1_1"], wts["b1_w_grp"], wts["b1_scale"], "b1_fwd")
    saved["b1"] = (h_in, mixed)
    h, saved["f1"] = _ffn_fwd(h, wts, 1)

    h_in = h
    u = _rms_fwd(h, wts["ln1_2"], "c2_rms")
    ag = _mm_col(u, wts["c2_w_pw1"], wts["c2_b_pw1"], "c2_pw1")
    sw, hc = _conf_mid_fwd(ag, wts["c2_dw"], wts["c2_b_dw"], wts["c2_ln_g"], wts["c2_ln_b"], "c2_mid")
    h = _mm_row(sw, wts["c2_w_pw2"], h, wts["c2_b_pw2"], "c2_pw2")
    saved["c2"] = (h_in, u, ag, sw, hc)
    h, saved["f2"] = _ffn_fwd(h, wts, 2)

    h, saved["a3"] = short_conv_fwd(h, 3)
    h, saved["f3"] = _ffn_fwd(h, wts, 3)

    loss, dh, g["ln_f"] = _loss_head(h, wts["ln_f"], target, "loss_head")

    def ffn_bwd(dh, i):
        return _ffn_bwd(dh, saved[f"f{i}"], wts, i, g)

    dh = ffn_bwd(dh, 3)
    dh = short_conv_bwd(dh, saved["a3"], 3)

    dh = ffn_bwd(dh, 2)
    h_in, u, ag, sw, hc = saved["c2"]
    dhc, g["c2_ln_g"], g["c2_ln_b"], g["c2_b_pw2"] = _conf_out_bwd(
        dh, wts["c2_w_pw2"], hc, wts["c2_ln_g"], wts["c2_ln_b"], "c2_pw2_bwd")
    g["c2_w_pw2"] = _mm_tn(sw, dh, 1, "c2_dw_pw2")
    dag, ddw, g["c2_b_dw"], g["c2_b_pw1"] = _conf_mid_bwd(dhc, ag, wts["c2_dw"], "c2_mid_bwd")
    g["c2_dw"] = ddw[:CONF_CONV_W]
    g["c2_w_pw1"] = _mm_tn(u, dag, N_CHIPS, "c2_dw_pw1")
    dh, g["ln1_2"] = _mm_nt_col_rms_bwd(dag, wts["c2_w_pw1"], h_in, wts["ln1_2"], dh, "c2_pw1_bwd")

    dh = ffn_bwd(dh, 1)
    h_in, mixed = saved["b1"]
    dmixed, g["b1_w_grp"], g["b1_scale"] = _pool_bwd_mm(dh, mixed, wts["b1_w_grp"], wts["b1_scale"], "b1_bwd_mm")
    dh, g["ln1_1"] = _pool_bwd_rms(dmixed, h_in, wts["ln1_1"], dh, "b1_bwd_rms")

    dh = ffn_bwd(dh, 0)
    dh = short_conv_bwd(dh, saved["a0"], 0)
    return loss, dh, g


MESH = pl.DeviceIdType.MESH
ANY = pl.BlockSpec(memory_space=pl.ANY)


def _position():
    return lax.axis_index("x"), lax.axis_index("y"), lax.axis_index("c")


def _other_chips(x, y):
    return [(1 - x, y), (x, 1 - y), (1 - x, 1 - y)]


def _remote(src, dst, send_sem, recv_sem, to):
    return pltpu.make_async_remote_copy(src_ref=src, dst_ref=dst, send_sem=send_sem, recv_sem=recv_sem,
                                        device_id=to, device_id_type=MESH)


def _half_rows(ref_rows, c):
    hr = ref_rows // 2
    return pl.ds(pl.multiple_of(c * hr, 16), hr)


def _allgather8(v, name):
    m_per, n = v.shape

    def body(v_ref, out_ref, send_sems, recv_sems, local_sem):
        x, y, c = _position()
        me, sibling = (x, y, c), (x, y, 1 - c)
        chips = _other_chips(x, y)

        def rows(px, py, pc):
            return out_ref.at[pl.ds((4 * px + 2 * py + pc) * m_per, m_per), :]

        def copy(k, block, to, src=None):
            return _remote(rows(*block) if src is None else src, rows(*block), send_sems.at[k], recv_sems.at[k], to)

        mine = pltpu.make_async_copy(v_ref, rows(*me), local_sem)
        mine.start()
        first = [copy(0, me, sibling, src=v_ref)]
        first += [copy(1 + j, me, (*chip, c), src=v_ref) for j, chip in enumerate(chips)]
        for cp in first:
            cp.start()
        passed = [copy(4 + j, (*chip, c), sibling) for j, chip in enumerate(chips)]
        for j, chip in enumerate(chips):
            copy(1 + j, (*chip, c), me).wait_recv()
            passed[j].start()
        copy(0, sibling, me).wait_recv()
        for j, chip in enumerate(chips):
            copy(4 + j, (*chip, 1 - c), me).wait_recv()
        for cp in first + passed:
            cp.wait_send()
        mine.wait()

    return _hosted(
        body, name=name,
        out_shape=jax.ShapeDtypeStruct((N_DEV * m_per, n), v.dtype),
        in_specs=[pl.BlockSpec(memory_space=pltpu.VMEM)],
        out_specs=pl.BlockSpec(memory_space=pltpu.VMEM),
        scratch_shapes=[pltpu.SemaphoreType.DMA((7,)), pltpu.SemaphoreType.DMA((7,)), pltpu.SemaphoreType.DMA],
        compiler_params=pltpu.CompilerParams(vmem_limit_bytes=VMEM_LIMIT),
    )(v)


def _cast_to_slot(ws, idx, name):
    r, cols = ws[0].shape
    assert all(w.shape == (r, cols) for w in ws)
    n = len(ws)
    tr = _tile(r, 256, 16)

    def body(idx_ref, *refs):
        for w_ref, o_ref in zip(refs[:n], refs[n:]):
            o_ref[...] = w_ref[...].astype(o_ref.dtype)

    return _hosted(
        body, name=name,
        grid_spec=pltpu.PrefetchScalarGridSpec(
            num_scalar_prefetch=1, grid=(r // tr,),
            in_specs=[pl.BlockSpec((tr, cols), lambda t, idx_ref: (t, 0))] * n,
            out_specs=[pl.BlockSpec((None, tr, cols), lambda t, idx_ref: (idx_ref[0], t, 0))] * n),
        out_shape=[jax.ShapeDtypeStruct((N_CHIPS, r, cols), BF16)] * n,
        compiler_params=_params("parallel"),
    )(idx, *ws)


def _dma_sems(*shape):
    return [pltpu.SemaphoreType.DMA(shape), pltpu.SemaphoreType.DMA(shape)]


def _same_shapes(arrays):
    return [jax.ShapeDtypeStruct(a.shape, a.dtype) for a in arrays]


def _part_rows(ref_rows, c, part):
    hr = ref_rows // 2
    i, n = part
    size = hr // n
    assert size * n == hr and size % 16 == 0, (ref_rows, part)
    return pl.ds(pl.multiple_of(c * hr + i * size, 16), size)


def _task_gather_ici(bufs, done, part=(0, 1)):
    n = len(bufs)

    def copies(outs, sems, landing):
        x, y, c = _position()
        my_chip = 2 * x + y
        res = []
        for i in range(n):
            rows = _part_rows(bufs[i].shape[1], c, part)
            for r, (px, py) in enumerate(_other_chips(x, y)):
                slot = (2 * px + py) if landing else my_chip
                res.append(_remote(outs[i].at[my_chip, rows, :], outs[i].at[slot, rows, :], sems[0].at[i, r], sems[1].at[i, r],
                                   (px, py, c)))
        return res

    def start(ins, outs, sems):
        for cp in copies(outs, sems, False):
            cp.start()

    def wait(ins, outs, sems):
        for cp in copies(outs, sems, True):
            cp.wait_recv()
            cp.wait_send()

    return _Task(bufs, _same_shapes(bufs), {i: i for i in range(n)}, _dma_sems(n, 3), start, wait, done)


def _task_gather_d2d(bufs, done):
    n = len(bufs)

    def copies(outs, sems, landing):
        x, y, c = _position()
        res = []
        for i in range(n):
            rows = _half_rows(bufs[i].shape[1], (1 - c) if landing else c)
            for r, (px, py) in enumerate(_other_chips(x, y)):
                part = outs[i].at[2 * px + py, rows, :]
                res.append(_remote(part, part, sems[0].at[i, r], sems[1].at[i, r], (x, y, 1 - c)))
        return res

    def start(ins, outs, sems):
        for cp in copies(outs, sems, False):
            cp.start()

    def wait(ins, outs, sems):
        for cp in copies(outs, sems, True):
            cp.wait_recv()
        for cp in copies(outs, sems, False):
            cp.wait_send()

    return _Task(bufs, _same_shapes(bufs), {i: i for i in range(n)}, _dma_sems(n, 3), start, wait, done)


def _task_sibling_halves(grads, done):
    n = len(grads)

    def copies(ins, outs, sems):
        x, y, c = _position()
        return [_remote(ins[i].at[:, _half_rows(grads[i].shape[1], 1 - c), :], outs[i], sems[0].at[i], sems[1].at[i],
                        (x, y, 1 - c)) for i in range(n)]

    def start(ins, outs, sems):
        for cp in copies(ins, outs, sems):
            cp.start()

    def wait(ins, outs, sems):
        for cp in copies(ins, outs, sems):
            cp.wait()

    shapes = [jax.ShapeDtypeStruct((g.shape[0], g.shape[1] // 2, g.shape[2]), g.dtype) for g in grads]
    return _Task(grads, shapes, {}, _dma_sems(n), start, wait, done)


def _task_chip_sums(parts, done, landed=None, part=(0, 1)):
    n = len(parts)
    i_part, n_parts = part
    sizes = [p.shape[1] // n_parts for p in parts]
    assert all(p.shape[1] == size * n_parts and size % 16 == 0 for p, size in zip(parts, sizes)), part
    rows = [pl.ds(i_part * size, size) for size in sizes]

    def copies(ins, outs, sems):
        x, y, c = _position()
        return [_remote(ins[i].at[2 * px + py, rows[i], :], outs[i].at[r, rows[i], :], sems[0].at[i, r], sems[1].at[i, r],
                        (px, py, c))
                for i in range(n) for r, (px, py) in enumerate(_other_chips(x, y))]

    def start(ins, outs, sems):
        for cp in copies(ins, outs, sems):
            cp.start()

    def wait(ins, outs, sems):
        for cp in copies(ins, outs, sems):
            cp.wait()

    shapes = [jax.ShapeDtypeStruct((3,) + p.shape[1:], p.dtype) for p in parts]
    if landed is None:
        return _Task(parts, shapes, {}, _dma_sems(n, 3), start, wait, done)
    return _Task(list(parts) + list(landed), shapes, {n + i: i for i in range(n)}, _dma_sems(n, 3), start, wait, done)


def _task_reduced_halves(fulls, done):
    n = len(fulls)

    def copies(outs, sems, landing):
        x, y, c = _position()
        res = []
        for i in range(n):
            rows = _half_rows(fulls[i].shape[0], (1 - c) if landing else c)
            res.append(_remote(outs[i].at[rows, :], outs[i].at[rows, :], sems[0].at[i], sems[1].at[i], (x, y, 1 - c)))
        return res

    def start(ins, outs, sems):
        for cp in copies(outs, sems, False):
            cp.start()

    def wait(ins, outs, sems):
        for cp in copies(outs, sems, True):
            cp.wait_recv()
        for cp in copies(outs, sems, False):
            cp.wait_send()

    return _Task(fulls, _same_shapes(fulls), {i: i for i in range(n)}, _dma_sems(n), start, wait, done)


def _add_halves(grad, sib, c, name):
    nsh, r, cols = grad.shape
    hr = r // 2
    tr = _tile(hr, 512, 16)
    nt = hr // tr

    def body(c_ref, g_ref, s_ref, o_ref):
        o_ref[...] = (g_ref[...].astype(F32) + s_ref[...].astype(F32)).astype(o_ref.dtype)

    return _hosted(
        body, name=name,
        grid_spec=pltpu.PrefetchScalarGridSpec(
            num_scalar_prefetch=1, grid=(nsh, nt),
            in_specs=[pl.BlockSpec((None, tr, cols), lambda j, t, c_ref: (j, c_ref[1] * nt + t, 0)),
                      pl.BlockSpec((None, tr, cols), lambda j, t, c_ref: (j, t, 0))],
            out_specs=pl.BlockSpec((None, tr, cols), lambda j, t, c_ref: (j, t, 0))),
        out_shape=jax.ShapeDtypeStruct((nsh, hr, cols), BF16),
        compiler_params=_params("parallel", "parallel"),
    )(c, grad, sib)


def _sum_chips(own, landed, idx, name):
    nsh, hr, cols = own.shape
    tr = _tile(hr, 512, 16)
    nt = hr // tr

    def body(idx_ref, p_ref, l_ref, o_ref):
        acc = p_ref[...].astype(F32)
        for k in range(landed.shape[0]):
            acc = acc + l_ref[k].astype(F32)
        o_ref[...] = acc

    return _hosted(
        body, name=name,
        grid_spec=pltpu.PrefetchScalarGridSpec(
            num_scalar_prefetch=1, grid=(nt,),
            in_specs=[pl.BlockSpec((None, tr, cols), lambda t, idx_ref: (idx_ref[0], t, 0)),
                      pl.BlockSpec((landed.shape[0], tr, cols), lambda t, idx_ref: (0, t, 0))],
            out_specs=pl.BlockSpec((tr, cols), lambda t, idx_ref: (idx_ref[1] * nt + t, 0))),
        out_shape=jax.ShapeDtypeStruct((2 * hr, cols), F32),
        compiler_params=_params("parallel"),
    )(idx, own, landed)


def _sum_devices(blocks, name):
    m8, n = blocks.shape
    m = m8 // N_DEV

    def body(b_ref, o_ref):
        acc = b_ref[pl.ds(0, m), :]
        for k in range(1, N_DEV):
            acc = acc + b_ref[pl.ds(k * m, m), :]
        o_ref[...] = acc

    return _hosted(
        body, name=name, out_shape=jax.ShapeDtypeStruct((m, n), F32),
        in_specs=[pl.BlockSpec(memory_space=pltpu.VMEM)], out_specs=pl.BlockSpec(memory_space=pltpu.VMEM),
        compiler_params=pltpu.CompilerParams(vmem_limit_bytes=VMEM_LIMIT),
    )(blocks)


def _adamw_update(w, grad, m, v):
    new_m = ADAM_B1 * m + (1.0 - ADAM_B1) * grad
    new_v = ADAM_B2 * v + (1.0 - ADAM_B2) * (grad * grad)
    m_hat = new_m * (1.0 / (1.0 - ADAM_B1 ** ADAM_STEP))
    v_hat = new_v * (1.0 / (1.0 - ADAM_B2 ** ADAM_STEP))
    return -ADAM_LR * (m_hat / (jnp.sqrt(v_hat) + ADAM_EPS) + ADAM_WD * w), new_m, new_v


def _adamw(w, g, m, v, name):
    r, cols = w.shape
    tr = _tile(r, 256)

    def body(w_ref, g_ref, m_ref, v_ref, go_ref, d_ref, mo_ref, vo_ref):
        grad = g_ref[...]
        go_ref[...] = grad
        d_ref[...], mo_ref[...], vo_ref[...] = _adamw_update(w_ref[...], grad, m_ref[...], v_ref[...])

    spec = pl.BlockSpec((tr, cols), lambda t: (t, 0))
    return _hosted(
        body, name=name, grid=(r // tr,), in_specs=[spec] * 4, out_specs=[spec] * 4,
        out_shape=[jax.ShapeDtypeStruct((r, cols), F32)] * 4,
        compiler_params=_params("parallel"),
    )(w, g, m, v)


def _adamw_small(grad_blocks, params, name):
    nb, npar = len(grad_blocks), len(params)

    def body(*refs):
        blocks, ins, outs = refs[:nb], refs[nb:nb + 3 * npar], refs[nb + 3 * npar:]
        for p, (w, _, _, blk, row0) in enumerate(params):
            grad = blocks[blk][pl.ds(row0, w.shape[0]), :]
            outs[4 * p][...] = grad
            outs[4 * p + 1][...], outs[4 * p + 2][...], outs[4 * p + 3][...] = _adamw_update(
                ins[3 * p][...], grad, ins[3 * p + 1][...], ins[3 * p + 2][...])

    args = list(grad_blocks) + [a for w, m, v, _, _ in params for a in (w, m, v)]
    vmem = pl.BlockSpec(memory_space=pltpu.VMEM)
    out = _hosted(
        body, name=name, in_specs=[vmem] * len(args), out_specs=[vmem] * (4 * npar),
        out_shape=[jax.ShapeDtypeStruct(w.shape, F32) for w, _, _, _, _ in params for _ in range(4)],
    )(*args)
    return [tuple(out[4 * p:4 * p + 4]) for p in range(npar)]


WEIGHT_NAMES = (
    "ln1_0", "a0_w_in", "a0_conv", "a0_w_out", "ln2_0", "ffn0_w_gu", "ffn0_w_down",
    "ln1_1", "b1_w_grp", "b1_scale", "ln2_1", "ffn1_w_gu", "ffn1_w_down",
    "ln1_2", "c2_w_pw1", "c2_b_pw1", "c2_dw", "c2_b_dw", "c2_ln_g", "c2_ln_b", "c2_w_pw2", "c2_b_pw2",
    "ln2_2", "ffn2_w_gu", "ffn2_w_down",
    "ln1_3", "a3_w_in", "a3_conv", "a3_w_out", "ln2_3", "ffn3_w_gu", "ffn3_w_down", "ln_f")
BIG = ("a0_w_in", "a0_w_out", "ffn0_w_gu", "ffn0_w_down", "b1_w_grp", "ffn1_w_gu", "ffn1_w_down", "c2_w_pw1", "c2_w_pw2",
       "ffn2_w_gu", "ffn2_w_down", "a3_w_in", "a3_w_out", "ffn3_w_gu", "ffn3_w_down")
GROUPED = "b1_w_grp"
SMALL_SHARDED = ("a0_conv", "a3_conv", "c2_dw")
REPLICATED = tuple(n for n in WEIGHT_NAMES if n not in BIG and n not in SMALL_SHARDED)


def _pad_rows(a, mult=8):
    pad = -a.shape[0] % mult
    return a if pad == 0 else jnp.concatenate([a, jnp.zeros((pad, a.shape[1]), a.dtype)], axis=0)


def _pack_rows(parts, width):
    rows = [p.reshape(-1, width) for p in parts]
    return _pad_rows(jnp.concatenate(rows, axis=0)), [r.shape[0] for r in rows]


def _unpack_rows(packed, counts, shapes):
    out, at = [], 0
    for n, shp in zip(counts, shapes):
        out.append(packed[at:at + n].reshape(shp))
        at += n
    return out


COLUMN_SHARDED = ("w_in", "w_gu", "w_pw1")


class _Weights(dict):
    def __init__(self, bufs):
        super().__init__()
        self.bufs = bufs

    def __missing__(self, name):
        buf = self.bufs[name]
        if name == GROUPED:
            cg = buf.shape[-1]
            rq = cg // N_CHIPS
            return jnp.transpose(buf.reshape(N_CHIPS, -1, rq, cg), (1, 0, 2, 3)).reshape(-1, cg, cg)
        return buf if name.endswith(COLUMN_SHARDED) else buf.reshape(-1, buf.shape[-1])


class _Exchange:
    def __init__(self, w, mom, vel, idx):
        def shards(table):
            return {n: table[n].reshape(-1, table[n].shape[-1]) for n in BIG}

        self.w, self.mom, self.vel, self.idx = shards(w), shards(mom), shards(vel), idx
        self.bufs = {}
        self.weights = _Weights(self.bufs)
        self.grads = {}
        self.sib, self.part, self.landed, self.full, self.updates = {}, {}, {}, {}, {}

    def cast(self, names):
        by_shape = {}
        for n in names:
            by_shape.setdefault(self.w[n].shape, []).append(n)
        for group in by_shape.values():
            self.bufs.update(zip(group, _cast_to_slot([self.w[n] for n in group], self.idx, f"cast_{group[0]}")))

    @staticmethod
    def _store(table, names):
        def done(arrays):
            table.update(zip(names, arrays))
        return done

    def _grad(self, n):
        g = self.grads[n]
        if n == GROUPED:
            ng, cg, _ = g.shape
            g = jnp.transpose(g.reshape(ng, N_CHIPS, cg // N_CHIPS, cg), (1, 0, 2, 3)).astype(BF16)
        return g.reshape(N_CHIPS, -1, g.shape[-1])

    def gather_ici(self, *names, part=(0, 1)):
        return lambda: _task_gather_ici([self.bufs[n] for n in names], self._store(self.bufs, names), part)

    def gather_d2d(self, *names):
        return lambda: _task_gather_d2d([self.bufs[n] for n in names], self._store(self.bufs, names))

    def sibling_halves(self, *names):
        return lambda: _task_sibling_halves([self._grad(n) for n in names], self._store(self.sib, names))

    def add_halves(self, *names):
        def run():
            for n in names:
                self.part[n] = _add_halves(self._grad(n), self.sib.pop(n), self.idx, f"reduce_add_{n}")
        return run

    def chip_sums(self, *names, part=(0, 1)):
        def make():
            landed = [self.landed[n] for n in names] if part[0] > 0 else None
            return _task_chip_sums([self.part[n] for n in names], self._store(self.landed, names), landed, part)
        return make

    def sum_chips(self, *names):
        def run():
            for n in names:
                self.full[n] = _sum_chips(self.part.pop(n), self.landed.pop(n), self.idx, f"reduce_sum_{n}")
        return run

    def reduced_halves(self, *names):
        return lambda: _task_reduced_halves([self.full[n] for n in names], self._store(self.full, names))

    def adamw(self, *names):
        def run():
            for n in names:
                self.updates[n] = _adamw(self.w[n], self.full.pop(n), self.mom[n], self.vel[n], f"adamw_{n}")
        return run


def _plan(ex):
    s = _Schedule()

    def ffn(i):
        return f"ffn{i}_w_gu", f"ffn{i}_w_down"

    c2, a3 = ("c2_w_pw1", "c2_w_pw2"), ("a3_w_in", "a3_w_out")
    first, second = (0, 2), (1, 2)
    gu, down = ffn(0)
    s.host("a0_rms", ex.gather_ici("a0_w_out"))
    s.host("a0_in", ex.gather_ici(gu, part=first), ex.gather_d2d("a0_w_out"))
    s.host("a0_conv", ex.gather_ici(gu, part=second))
    s.host("a0_out", ex.gather_ici(down), ex.gather_d2d(gu))
    s.host("ffn0_rms", ex.gather_d2d(down))
    gu, down = ffn(1)
    s.host("ffn0_up", ex.gather_ici(gu, GROUPED))
    s.host("ffn0_down", ex.gather_ici(down), ex.gather_d2d(gu, GROUPED))
    s.host("b1_fwd", ex.gather_d2d(down))
    s.host("ffn1_up", ex.gather_ici(*c2))
    s.host("ffn1_down", ex.gather_d2d(*c2))
    s.host("c2_mid", ex.gather_ici(*ffn(2)))
    s.host("c2_pw2", ex.gather_d2d(*ffn(2)))
    gu, down = ffn(3)
    s.host("ffn2_up", ex.gather_ici(*a3))
    s.host("ffn2_down", ex.gather_d2d(*a3), ex.gather_ici(down))
    s.host("a3_in", ex.gather_ici(gu, part=first))
    s.host("a3_conv", ex.gather_ici(gu, part=second))
    s.host("a3_out", ex.gather_d2d(gu, down))

    def reduce_on(names, first_host, ici_hosts, last_host):
        s.host(first_host, ex.sibling_halves(*names))
        s.post(first_host, ex.add_halves(*names))
        for host, hosted, part in ici_hosts:
            s.host(host, ex.chip_sums(*hosted, part=part))
        s.post(ici_hosts[-1][0], ex.sum_chips(*names))
        if last_host is not None:
            s.host(last_host, ex.reduced_halves(*names))
            s.post(last_host, ex.adamw(*names))

    whole = (0, 1)
    gu, down = ffn(3)
    reduce_on((gu, down), "a3_out_bwd",
              [("a3_conv_bwd", (down,), whole), ("a3_dw_in", (gu,), first), ("a3_in_bwd", (gu,), second)], "ffn2_down_bwd")
    reduce_on(a3, "ffn2_down_bwd", [("ffn2_dw_down", a3[:1], whole), ("ffn2_dw_gu", a3[1:], whole)], "ffn2_up_bwd")
    gu, down = ffn(0)
    s.host("ffn0_dw_gu", ex.sibling_halves(down))
    s.post("ffn0_dw_gu", ex.add_halves(down))
    s.host("ffn0_up_bwd", ex.chip_sums(down))
    s.host("a0_out_bwd", ex.sibling_halves(gu, GROUPED))
    s.post("a0_out_bwd", ex.add_halves(gu, GROUPED))
    s.host("a0_conv_bwd", ex.chip_sums(gu, part=first), ex.chip_sums(GROUPED))
    s.host("a0_dw_in", ex.chip_sums(gu, part=second))
    s.post("a0_dw_in", ex.sum_chips(gu, down, GROUPED))
    s.host("a0_dw_out", ex.sibling_halves("a0_w_in"))
    s.post("a0_dw_out", ex.add_halves("a0_w_in"))
    s.host("a0_in_bwd", ex.chip_sums("a0_w_in"))
    s.post("a0_in_bwd", ex.sum_chips("a0_w_in"))
    reduce_on(ffn(2), "c2_pw2_bwd", [("c2_mid_bwd", ffn(2), whole)], "c2_pw1_bwd")
    reduce_on(c2, "ffn1_down_bwd", [("ffn1_dw_down", c2, whole)], "ffn1_dw_gu")
    gu, down = ffn(1)
    reduce_on((gu, down), "b1_bwd_mm",
              [("b1_bwd_rms", (down,), whole), ("ffn0_down_bwd", (gu,), first), ("ffn0_dw_down", (gu,), second)], "ffn0_dw_gu")
    return s


def kernel(x, *rest):
    nw = len(WEIGHT_NAMES)
    w = dict(zip(WEIGHT_NAMES, rest[:nw]))
    target = rest[nw]
    mom = dict(zip(WEIGHT_NAMES, rest[nw + 1:2 * nw + 1]))
    vel = dict(zip(WEIGHT_NAMES, rest[2 * nw + 1:3 * nw + 1]))
    cx, cy, cc = _position()
    my_chip = 2 * cx + cy
    d = x.shape[-1]
    cq = d // N_CHIPS

    ex = _Exchange(w, mom, vel, jnp.stack([my_chip, cc]).astype(jnp.int32))
    sched = _plan(ex)
    ex.cast(BIG)
    _comm_only([ex.gather_ici("a0_w_in")()], "gather_a0_ici")
    _comm_only([ex.gather_d2d("a0_w_in")()], "gather_a0_d2d")

    small_blk, small_counts = _pack_rows([w[n] for n in SMALL_SHARDED], cq)
    small_all = _allgather8(small_blk, "gather_small").reshape(N_CHIPS, 2, small_blk.shape[0], cq)[:, 0]
    small_parts = _unpack_rows(jnp.transpose(small_all, (1, 0, 2)), small_counts,
                               [(w[n].reshape(-1, cq).shape[0], N_CHIPS, cq) for n in SMALL_SHARDED])
    wts = ex.weights
    for n in REPLICATED:
        wts[n] = w[n].reshape(1, -1)
    for n, part in zip(SMALL_SHARDED, small_parts):
        wts[n] = part.reshape(part.shape[0], d)

    _ACTIVE_SCHEDULE[0] = sched
    try:
        loss, dx, g = _device_step(x[0], target[0], wts, ex.grads)
    finally:
        _ACTIVE_SCHEDULE[0] = None
    assert not sched.hosts and not sched.posts, (sched.hosts, sched.posts)

    summed, last = ("ffn0_w_gu", "ffn0_w_down", GROUPED, "a0_w_in"), "a0_w_out"
    _comm_only([ex.reduced_halves(*summed)(), ex.sibling_halves(last)()], "reduce_tail_d2d")
    ex.adamw(*summed)()
    ex.add_halves(last)()
    _comm_only([ex.chip_sums(last)()], "reduce_tail_ici")
    ex.sum_chips(last)()
    _comm_only([ex.reduced_halves(last)()], "reduce_tail_halves")
    ex.adamw(last)()

    rep_rows = [g[n].reshape(-1, cq) for n in REPLICATED]
    by_chip = [jnp.transpose(g[n].reshape(g[n].shape[0], N_CHIPS, cq), (1, 0, 2)) for n in SMALL_SHARDED]
    shard_rows = jnp.concatenate(by_chip, axis=1)
    n_rep, n_shard = sum(r.shape[0] for r in rep_rows), shard_rows.shape[1]
    sm_blk = _pad_rows(jnp.concatenate(rep_rows + [shard_rows.reshape(N_CHIPS * n_shard, cq)], axis=0))
    sm_sum = _sum_devices(_allgather8(sm_blk, "gather_small_grads"), "sum_small_grads")
    mine = lax.dynamic_slice_in_dim(sm_sum, n_rep + my_chip * n_shard, n_shard, axis=0)

    out = ex.updates
    params, at = [], {0: 0, 1: 0}
    for block, names in ((0, REPLICATED), (1, SMALL_SHARDED)):
        for n in names:
            w2, m2, v2 = (a[n].reshape(-1, cq) for a in (w, mom, vel))
            params.append((w2, m2, v2, block, at[block]))
            at[block] += w2.shape[0]
    updated = _adamw_small([sm_sum, mine], params, "adamw_small")
    for n, res in zip(REPLICATED + SMALL_SHARDED, updated):
        out[n] = tuple(r.reshape(w[n].shape) for r in res)

    total = lax.psum(loss[0, 0], ("x", "y", "c"))
    grads, deltas, new_m, new_v = ([out[n][k].reshape(w[n].shape) for n in WEIGHT_NAMES] for k in range(4))
    return (total, dx.reshape(x.shape), *grads, *deltas, *new_m, *new_v)
```

```python
import functools

import jax
import jax.numpy as jnp
from jax import lax
from jax.experimental import pallas as pl
from jax.experimental.pallas import tpu as pltpu

F32 = jnp.float32
BF16 = jnp.bfloat16

RMS_EPS = 1e-6
LN_EPS = 1e-5
POOL_WINDOWS = (2, 4, 8, 16)
SHORT_CONV_W = 3
CONF_CONV_W = 31
N_CHIPS = 4
N_DEV = 8

ADAM_LR = 0.001
ADAM_B1 = 0.9
ADAM_B2 = 0.999
ADAM_EPS = 1e-08
ADAM_WD = 0.01
ADAM_STEP = 10

V7X_VMEM_BYTES = 64 * 1024 * 1024
VMEM_LIMIT = V7X_VMEM_BYTES - 8 * 1024 * 1024
LANES = 128
POOL_HALO = 16
SCONV_HALO = 16
CONF_HALO = 32


def _params(*sem):
    return pltpu.CompilerParams(dimension_semantics=sem, vmem_limit_bytes=VMEM_LIMIT)


def _tile(n, pref, mult=8):
    t = min(n, pref)
    while t > mult and (n % t or t % mult):
        t -= mult
    assert n % t == 0 and t % mult == 0, (n, pref, mult)
    return t


def _sigmoid(x):
    return jax.nn.sigmoid(x)


def _dot(a, b):
    return jnp.dot(a, b, preferred_element_type=F32)


def _dot_nt(a, b):
    return lax.dot_general(a, b, (((1,), (1,)), ((), ())), preferred_element_type=F32)


def _dot_tn(a, b):
    return lax.dot_general(a, b, (((0,), (0,)), ((), ())), preferred_element_type=F32)


def _colsum(x):
    return jnp.sum(x, axis=0, keepdims=True)


def _rms_stats(x):
    return lax.rsqrt(jnp.mean(x * x, axis=-1, keepdims=True) + RMS_EPS)


def _rms_bwd(du, x, gain):
    r = _rms_stats(x)
    xhat = x * r
    gdy = du * gain
    dx = r * (gdy - xhat * jnp.mean(gdy * xhat, axis=-1, keepdims=True))
    return dx, _colsum(du * xhat)


class _Task:
    def __init__(self, ins, out_shapes, aliases, sems, start, wait, done):
        self.ins, self.out_shapes, self.aliases, self.sems = list(ins), list(out_shapes), dict(aliases), list(sems)
        self.start, self.wait, self.done = start, wait, done


class _Schedule:
    def __init__(self):
        self.hosts, self.posts = {}, {}

    def host(self, kernel_name, *make_tasks):
        self.hosts.setdefault(kernel_name, []).extend(make_tasks)

    def post(self, kernel_name, *thunks):
        self.posts.setdefault(kernel_name, []).extend(thunks)

    def tasks_for(self, kernel_name):
        return [make() for make in self.hosts.pop(kernel_name, ())]

    def finished(self, kernel_name):
        for thunk in self.posts.pop(kernel_name, ()):
            thunk()


_ACTIVE_SCHEDULE = [None]


def _hosted(body, name, **kw):
    def run(*args):
        sched = _ACTIVE_SCHEDULE[0]
        tasks = sched.tasks_for(name) if sched is not None else []
        out = _call_with_tasks(body, name, tasks, kw, args) if tasks else pl.pallas_call(body, name=name, **kw)(*args)
        if sched is not None:
            sched.finished(name)
        return out

    return run


def _call_with_tasks(body, name, tasks, kw, args):
    grid = tuple(kw.get("grid", ()))
    single = not isinstance(kw["out_shape"], (list, tuple))
    out_shape = [kw["out_shape"]] if single else list(kw["out_shape"])
    out_specs = [kw["out_specs"]] if single else list(kw["out_specs"])
    scratch = list(kw.get("scratch_shapes", ()))
    n_in, n_out, n_scr = len(args), len(out_shape), len(scratch)
    t_in = [a for t in tasks for a in t.ins]
    t_out = [o for t in tasks for o in t.out_shapes]
    t_sem = [s for t in tasks for s in t.sems]
    aliases, at_in, at_out = {}, n_in, n_out
    for t in tasks:
        for i, o in t.aliases.items():
            aliases[at_in + i] = at_out + o
        at_in += len(t.ins)
        at_out += len(t.out_shapes)

    def wrapped(*refs):
        a = n_in
        b = a + len(t_in)
        c = b + n_out
        d = c + len(t_out)
        e = d + n_scr
        ins, tins, outs, touts, scr, tsems = refs[:a], refs[a:b], refs[b:c], refs[c:d], refs[d:e], refs[e:]
        views, i0, o0, s0 = [], 0, 0, 0
        for t in tasks:
            views.append((tins[i0:i0 + len(t.ins)], touts[o0:o0 + len(t.out_shapes)], tsems[s0:s0 + len(t.sems)]))
            i0, o0, s0 = i0 + len(t.ins), o0 + len(t.out_shapes), s0 + len(t.sems)

        def start_all():
            for t, v in zip(tasks, views):
                t.start(*v)

        def wait_all():
            for t, v in zip(tasks, views):
                t.wait(*v)

        if grid:
            first = functools.reduce(jnp.logical_and, [pl.program_id(i) == 0 for i in range(len(grid))])
            last = functools.reduce(jnp.logical_and, [pl.program_id(i) == grid[i] - 1 for i in range(len(grid))])
            pl.when(first)(start_all)
            body(*ins, *outs, *scr)
            pl.when(last)(wait_all)
        else:
            start_all()
            body(*ins, *outs, *scr)
            wait_all()

    res = pl.pallas_call(
        wrapped, name=name, grid=grid,
        in_specs=list(kw["in_specs"]) + [ANY] * len(t_in), out_specs=out_specs + [ANY] * len(t_out),
        out_shape=out_shape + t_out, scratch_shapes=scratch + t_sem, input_output_aliases=aliases,
        compiler_params=pltpu.CompilerParams(dimension_semantics=("arbitrary",) * len(grid), vmem_limit_bytes=VMEM_LIMIT),
    )(*args, *t_in)
    res = list(res)
    own, rest = res[:n_out], res[n_out:]
    for t in tasks:
        t.done(rest[:len(t.out_shapes)])
        rest = rest[len(t.out_shapes):]
    return own[0] if single else own


def _comm_only(tasks, name):
    _call_with_tasks(lambda: None, name, tasks, dict(grid=(), in_specs=[], out_specs=[], out_shape=[]), ())


def _rms_fwd(h, gain, name):
    s, d = h.shape
    tm = _tile(s, 512)

    def body(h_ref, g_ref, u_ref):
        x = h_ref[...]
        u_ref[...] = (x * _rms_stats(x) * g_ref[...]).astype(u_ref.dtype)

    return _hosted(
        body, name=name, grid=(s // tm,),
        in_specs=[pl.BlockSpec((tm, d), lambda m: (m, 0)), pl.BlockSpec((1, d), lambda m: (0, 0))],
        out_specs=pl.BlockSpec((tm, d), lambda m: (m, 0)),
        out_shape=jax.ShapeDtypeStruct((s, d), BF16),
        compiler_params=_params("parallel"),
    )(h, gain)


def _mm_col(a, w, bias, name):
    s, k = a.shape
    nsh, _, ns = w.shape
    tm = _tile(s, 512)
    has_bias = bias is not None

    def body(a_ref, w_ref, *rest):
        o_ref = rest[-1]
        acc = _dot(a_ref[...], w_ref[...])
        if has_bias:
            acc = acc + rest[0][...]
        o_ref[...] = acc.astype(o_ref.dtype)

    in_specs = [pl.BlockSpec((tm, k), lambda j, m: (m, 0)), pl.BlockSpec((None, k, ns), lambda j, m: (j, 0, 0))]
    args = [a, w]
    if has_bias:
        in_specs.append(pl.BlockSpec((1, ns), lambda j, m: (0, j)))
        args.append(bias)
    return _hosted(
        body, name=name, grid=(nsh, s // tm), in_specs=in_specs,
        out_specs=pl.BlockSpec((tm, ns), lambda j, m: (m, j)),
        out_shape=jax.ShapeDtypeStruct((s, nsh * ns), BF16),
        compiler_params=_params("parallel", "parallel"),
    )(*args)


def _mm_row(a, w, res, bias, name):
    s = a.shape[0]
    k, n = w.shape
    tm = _tile(s, 512)
    has_bias = bias is not None

    def body(a_ref, w_ref, res_ref, *rest):
        o_ref = rest[-1]
        y = res_ref[...] + _dot(a_ref[...], w_ref[...])
        if has_bias:
            y = y + rest[0][...]
        o_ref[...] = y

    in_specs = [pl.BlockSpec((tm, k), lambda m: (m, 0)), pl.BlockSpec((k, n), lambda m: (0, 0)),
                pl.BlockSpec((tm, n), lambda m: (m, 0))]
    args = [a, w, res]
    if has_bias:
        in_specs.append(pl.BlockSpec((1, n), lambda m: (0, 0)))
        args.append(bias)
    return _hosted(
        body, name=name, grid=(s // tm,), in_specs=in_specs,
        out_specs=pl.BlockSpec((tm, n), lambda m: (m, 0)),
        out_shape=jax.ShapeDtypeStruct((s, n), F32),
        compiler_params=_params("parallel"),
    )(*args)


def _mm_nt_row(dy, w, name):
    s, n = dy.shape
    k = w.shape[0]
    tm = _tile(s, 512)

    def body(dy_ref, w_ref, o_ref):
        o_ref[...] = _dot_nt(dy_ref[...].astype(BF16), w_ref[...])

    return _hosted(
        body, name=name, grid=(s // tm,),
        in_specs=[pl.BlockSpec((tm, n), lambda m: (m, 0)), pl.BlockSpec((k, n), lambda m: (0, 0))],
        out_specs=pl.BlockSpec((tm, k), lambda m: (m, 0)),
        out_shape=jax.ShapeDtypeStruct((s, k), F32),
        compiler_params=_params("parallel"),
    )(dy, w)


def _ffn_up(u, w, name):
    s, d = u.shape
    _, _, ns = w.shape
    tm = _tile(s, 512)

    def body(u_ref, wg_ref, wu_ref, act_ref, s1_ref, q1_ref):
        x = u_ref[...]
        g = _dot(x, wg_ref[...])
        up = _dot(x, wu_ref[...])
        sg = _sigmoid(g)
        s1 = g * sg
        act_ref[...] = (s1 * up).astype(act_ref.dtype)
        s1_ref[...] = s1.astype(s1_ref.dtype)
        q1_ref[...] = (up * sg * (1.0 + g * (1.0 - sg))).astype(q1_ref.dtype)

    out = pl.BlockSpec((tm, ns), lambda j, m: (m, j))
    return _hosted(
        body, name=name, grid=(2, s // tm),
        in_specs=[pl.BlockSpec((tm, d), lambda j, m: (m, 0)), pl.BlockSpec((None, d, ns), lambda j, m: (j, 0, 0)),
                  pl.BlockSpec((None, d, ns), lambda j, m: (j + 2, 0, 0))],
        out_specs=[out, out, out],
        out_shape=[jax.ShapeDtypeStruct((s, 2 * ns), BF16)] * 3,
        compiler_params=_params("parallel", "parallel"),
    )(u, w, w)


def _ffn_down_bwd(dh, w, s1, q1, name):
    s, d = dh.shape
    f = w.shape[0]
    tm = _tile(s, 256)

    def body(dh_ref, w_ref, s1_ref, q1_ref, o_ref):
        da = _dot_nt(dh_ref[...].astype(BF16), w_ref[...])
        o_ref[:, :f] = (da * q1_ref[...].astype(F32)).astype(o_ref.dtype)
        o_ref[:, f:] = (da * s1_ref[...].astype(F32)).astype(o_ref.dtype)

    return _hosted(
        body, name=name, grid=(s // tm,),
        in_specs=[pl.BlockSpec((tm, d), lambda m: (m, 0)), pl.BlockSpec((f, d), lambda m: (0, 0)),
                  pl.BlockSpec((tm, f), lambda m: (m, 0)), pl.BlockSpec((tm, f), lambda m: (m, 0))],
        out_specs=pl.BlockSpec((tm, 2 * f), lambda m: (m, 0)),
        out_shape=jax.ShapeDtypeStruct((s, 2 * f), BF16),
        compiler_params=_params("parallel"),
    )(dh, w, s1, q1)


def _mm_nt_col_rms_bwd(dy, w, h, gain, dh, name):
    s = dy.shape[0]
    nsh, k, ns = w.shape
    tm = _tile(s, 256)

    def body(dy_ref, w_ref, h_ref, g_ref, dh_ref, o_ref, dg_ref):
        du = _dot_nt(dy_ref[:, :ns], w_ref[0])
        for j in range(1, nsh):
            du = du + _dot_nt(dy_ref[:, j * ns:(j + 1) * ns], w_ref[j])
        dx, dg = _rms_bwd(du, h_ref[...], g_ref[...])
        o_ref[...] = dh_ref[...] + dx
        _accumulate(dg_ref, dg, pl.program_id(0) == 0)

    return _hosted(
        body, name=name, grid=(s // tm,),
        in_specs=[pl.BlockSpec((tm, nsh * ns), lambda m: (m, 0)), pl.BlockSpec((nsh, k, ns), lambda m: (0, 0, 0)),
                  pl.BlockSpec((tm, k), lambda m: (m, 0)), pl.BlockSpec((1, k), lambda m: (0, 0)),
                  pl.BlockSpec((tm, k), lambda m: (m, 0))],
        out_specs=[pl.BlockSpec((tm, k), lambda m: (m, 0)), pl.BlockSpec((1, k), lambda m: (0, 0))],
        out_shape=[jax.ShapeDtypeStruct((s, k), F32), jax.ShapeDtypeStruct((1, k), F32)],
        compiler_params=_params("arbitrary"),
    )(dy, w, h, gain, dh)


def _mm_tn(a, dy, nsh, name):
    s, k = a.shape
    ns = dy.shape[1] // nsh
    tm = _tile(s, 1024)
    tk = _tile(k, 1408, LANES)
    nk, nm = k // tk, s // tm

    def body(a_ref, dy_ref, o_ref, acc_ref):
        m = pl.program_id(2)
        part = _dot_tn(a_ref[...], dy_ref[...].astype(BF16))

        @pl.when(m == 0)
        def _():
            acc_ref[...] = part

        @pl.when(m > 0)
        def _():
            acc_ref[...] += part

        @pl.when(m == nm - 1)
        def _():
            o_ref[...] = acc_ref[...].astype(o_ref.dtype)

    return _hosted(
        body, name=name, grid=(nsh, nk, nm),
        in_specs=[pl.BlockSpec((tm, tk), lambda j, kk, m: (m, kk)), pl.BlockSpec((tm, ns), lambda j, kk, m: (m, j))],
        out_specs=pl.BlockSpec((None, tk, ns), lambda j, kk, m: (j, kk, 0)),
        out_shape=jax.ShapeDtypeStruct((nsh, k, ns), BF16),
        scratch_shapes=[pltpu.VMEM((tk, ns), F32)],
        compiler_params=_params("parallel", "parallel", "arbitrary"),
    )(a, dy)


def _main_spec(tm, w):
    return pl.BlockSpec((tm, w), lambda m: (m, 0))


def _before_spec(tm, hb, w):
    return pl.BlockSpec((hb, w), lambda m: (jnp.maximum(m * (tm // hb) - 1, 0), 0))


def _after_spec(tm, hb, w, s):
    return pl.BlockSpec((hb, w), lambda m: (jnp.minimum((m + 1) * (tm // hb), s // hb - 1), 0))


def _row_spec(w, rows=1):
    return pl.BlockSpec((rows, w), lambda m: (0, 0))


CHUNK_LANES = 4 * LANES
CHUNK_ROWS = 32


def _build_shifts(ext8_ref):
    n = ext8_ref.shape[1] - 8
    for r in range(1, 8):
        ext8_ref[r, pl.ds(0, n), :] = ext8_ref[0, pl.ds(r, n), :]


def _shifted(ext8_ref, shift, r0, rows, cols):
    return ext8_ref[shift % 8, pl.ds(pl.multiple_of(shift - shift % 8 + r0, 8), rows), cols]


def _lane_chunk(i):
    return pl.ds(pl.multiple_of(i * CHUNK_LANES, CHUNK_LANES), CHUNK_LANES)


def _sum_terms(terms, ways=4):
    accs = []
    for i, t in enumerate(terms):
        if i < ways:
            accs.append(t)
        else:
            accs[i % ways] = accs[i % ways] + t
    while len(accs) > 1:
        accs = [accs[i] + accs[i + 1] if i + 1 < len(accs) else accs[i] for i in range(0, len(accs), 2)]
    return accs[0]


def _accumulate(ref, val, first):
    @pl.when(first)
    def _():
        ref[...] = val

    @pl.when(jnp.logical_not(first))
    def _():
        ref[...] += val


def _sconv_taps(zext_ref, cw_ref, tm, base):
    out = cw_ref[2:3, :] * zext_ref[pl.ds(base, tm), :]
    out = out + cw_ref[1:2, :] * zext_ref[pl.ds(base - 1, tm), :]
    return out + cw_ref[0:1, :] * zext_ref[pl.ds(base - 2, tm), :]


def _sconv_fill_z(zext_ref, main_ref, before_ref, d, m):
    hb = SCONV_HALO
    zb = before_ref[:, d:2 * d].astype(F32) * before_ref[:, 2 * d:].astype(F32)
    zext_ref[pl.ds(0, hb), :] = jnp.where(m > 0, zb, 0.0)
    zext_ref[pl.ds(hb, main_ref.shape[0]), :] = main_ref[:, d:2 * d].astype(F32) * main_ref[:, 2 * d:].astype(F32)


def _sconv_fwd(bcv, cw, name):
    s, d3 = bcv.shape
    d = d3 // 3
    tm = _tile(s, 512, SCONV_HALO)

    def body(main_ref, before_ref, cw_ref, p_ref, zext_ref):
        m = pl.program_id(0)
        _sconv_fill_z(zext_ref, main_ref, before_ref, d, m)
        zc = _sconv_taps(zext_ref, cw_ref, tm, SCONV_HALO)
        p_ref[...] = (main_ref[:, :d].astype(F32) * zc).astype(p_ref.dtype)

    return _hosted(
        body, name=name, grid=(s // tm,),
        in_specs=[_main_spec(tm, d3), _before_spec(tm, SCONV_HALO, d3), _row_spec(d, SHORT_CONV_W)],
        out_specs=_main_spec(tm, d),
        out_shape=jax.ShapeDtypeStruct((s, d), BF16),
        scratch_shapes=[pltpu.VMEM((tm + SCONV_HALO, d), F32)],
        compiler_params=_params("parallel"),
    )(bcv, bcv, cw)


def _sconv_bwd(dp, bcv, cw, name):
    s, d3 = bcv.shape
    d = d3 // 3
    tm = _tile(s, 512, SCONV_HALO)
    nm = s // tm
    ha = 8

    def body(dp_ref, dpa_ref, main_ref, before_ref, after_ref, cw_ref, o_ref, dcw_ref, zext_ref, dext_ref):
        m = pl.program_id(0)
        _sconv_fill_z(zext_ref, main_ref, before_ref, d, m)
        zc = _sconv_taps(zext_ref, cw_ref, tm, SCONV_HALO)
        dp_t = dp_ref[...]
        o_ref[:, :d] = (dp_t * zc).astype(o_ref.dtype)
        dzc = dp_t * main_ref[:, :d].astype(F32)
        dext_ref[pl.ds(0, tm), :] = dzc
        dza = dpa_ref[...] * after_ref[:, :d].astype(F32)[0:ha]
        dext_ref[pl.ds(tm, ha), :] = jnp.where(m < nm - 1, dza, 0.0)
        dz = cw_ref[2:3, :] * dzc
        dz = dz + cw_ref[1:2, :] * dext_ref[pl.ds(1, tm), :]
        dz = dz + cw_ref[0:1, :] * dext_ref[pl.ds(2, tm), :]
        o_ref[:, d:2 * d] = (dz * main_ref[:, 2 * d:].astype(F32)).astype(o_ref.dtype)
        o_ref[:, 2 * d:] = (dz * main_ref[:, d:2 * d].astype(F32)).astype(o_ref.dtype)

        @pl.when(m == 0)
        def _():
            dcw_ref[...] = jnp.zeros_like(dcw_ref)

        for kk in range(SHORT_CONV_W):
            zs = zext_ref[pl.ds(SCONV_HALO - 2 + kk, tm), :]
            dcw_ref[kk:kk + 1, :] += _colsum(dzc * zs)

    return _hosted(
        body, name=name, grid=(nm,),
        in_specs=[_main_spec(tm, d), _after_spec(tm, ha, d, s), _main_spec(tm, d3), _before_spec(tm, SCONV_HALO, d3),
                  _after_spec(tm, SCONV_HALO, d3, s), _row_spec(d, SHORT_CONV_W)],
        out_specs=[_main_spec(tm, d3), _row_spec(d, 8)],
        out_shape=[jax.ShapeDtypeStruct((s, d3), BF16), jax.ShapeDtypeStruct((8, d), F32)],
        scratch_shapes=[pltpu.VMEM((tm + SCONV_HALO, d), F32), pltpu.VMEM((tm + ha, d), F32)],
        compiler_params=_params("arbitrary"),
    )(dp, dp, bcv, bcv, bcv, cw)


def _pool_counts(t0, tm, w):
    t = t0 + lax.broadcasted_iota(jnp.int32, (tm, 1), 0)
    return jnp.minimum(t + 1, w).astype(F32)


def _pool_fwd(h, gain, wg, scale, name):
    s, d = h.shape
    ng, cg, _ = wg.shape
    tm = _tile(s, 512, POOL_HALO)

    def body(h_ref, hb_ref, g_ref, wg_ref, sc_ref, o_ref, mx_ref, uext_ref):
        m = pl.program_id(0)
        x = h_ref[...]
        gain_row = g_ref[...]
        xb = hb_ref[...]
        uext_ref[pl.ds(0, POOL_HALO), :] = jnp.where(m > 0, xb * _rms_stats(xb) * gain_row, 0.0)
        uext_ref[pl.ds(POOL_HALO, tm), :] = x * _rms_stats(x) * gain_row
        for gi, win in enumerate(POOL_WINDOWS):
            cols = pl.ds(gi * cg, cg)
            u_g = uext_ref[pl.ds(POOL_HALO, tm), cols]
            acc = u_g
            for i in range(1, win):
                acc = acc + uext_ref[pl.ds(POOL_HALO - i, tm), cols]
            mixed = (acc / _pool_counts(m * tm, tm, win) - u_g).astype(BF16)
            mx_ref[:, cols] = mixed
            o_ref[:, cols] = x[:, gi * cg:(gi + 1) * cg] + _dot(mixed, wg_ref[gi]) * sc_ref[:, cols]

    return _hosted(
        body, name=name, grid=(s // tm,),
        in_specs=[_main_spec(tm, d), _before_spec(tm, POOL_HALO, d), _row_spec(d),
                  pl.BlockSpec((ng, cg, cg), lambda m: (0, 0, 0)), _row_spec(d)],
        out_specs=[_main_spec(tm, d), _main_spec(tm, d)],
        out_shape=[jax.ShapeDtypeStruct((s, d), F32), jax.ShapeDtypeStruct((s, d), BF16)],
        scratch_shapes=[pltpu.VMEM((tm + POOL_HALO, d), F32)],
        compiler_params=_params("parallel"),
    )(h, h, gain, wg, scale)


def _pool_bwd_mm(dh, mixed, wg, scale, name):
    s, d = dh.shape
    ng, cg, _ = wg.shape
    tm = _tile(s, 512)

    def body(dh_ref, mx_ref, wg_ref, sc_ref, dmx_ref, dwg_ref, dsc_ref):
        first = pl.program_id(0) == 0
        for gi in range(ng):
            cols = pl.ds(gi * cg, cg)
            dh_g = dh_ref[:, cols]
            mixed = mx_ref[:, cols]
            w_g = wg_ref[gi]
            dy = (dh_g * sc_ref[:, cols]).astype(BF16)
            dmx_ref[:, cols] = _dot_nt(dy, w_g)
            _accumulate(dsc_ref.at[:, cols], _colsum(dh_g * _dot(mixed, w_g)), first)
            _accumulate(dwg_ref.at[gi], _dot_tn(mixed, dy), first)

    return _hosted(
        body, name=name, grid=(s // tm,),
        in_specs=[_main_spec(tm, d), _main_spec(tm, d), pl.BlockSpec((ng, cg, cg), lambda m: (0, 0, 0)), _row_spec(d)],
        out_specs=[_main_spec(tm, d), pl.BlockSpec((ng, cg, cg), lambda m: (0, 0, 0)), _row_spec(d)],
        out_shape=[jax.ShapeDtypeStruct((s, d), F32), jax.ShapeDtypeStruct((ng, cg, cg), F32),
                   jax.ShapeDtypeStruct((1, d), F32)],
        compiler_params=_params("arbitrary"),
    )(dh, mixed, wg, scale)


def _pool_bwd_rms(dmixed, h, gain, dh, name):
    s, d = h.shape
    cg = d // len(POOL_WINDOWS)
    tm = _tile(s, 512, POOL_HALO)
    nm = s // tm

    def body(dmx_ref, dmxa_ref, h_ref, g_ref, dh_ref, o_ref, dg_ref, eext_ref, du_ref):
        m = pl.program_id(0)
        for gi, win in enumerate(POOL_WINDOWS):
            cols = pl.ds(gi * cg, cg)
            dmx = dmx_ref[:, cols]
            eext_ref[pl.ds(0, tm), cols] = dmx / _pool_counts(m * tm, tm, win)
            ea = dmxa_ref[:, cols] / _pool_counts((m + 1) * tm, POOL_HALO, win)
            eext_ref[pl.ds(tm, POOL_HALO), cols] = jnp.where(m < nm - 1, ea, 0.0)
            acc = -dmx
            for i in range(win):
                acc = acc + eext_ref[pl.ds(i, tm), cols]
            du_ref[:, cols] = acc
        dx, dg = _rms_bwd(du_ref[...], h_ref[...], g_ref[...])
        o_ref[...] = dh_ref[...] + dx
        _accumulate(dg_ref, dg, m == 0)

    return _hosted(
        body, name=name, grid=(nm,),
        in_specs=[_main_spec(tm, d), _after_spec(tm, POOL_HALO, d, s), _main_spec(tm, d), _row_spec(d), _main_spec(tm, d)],
        out_specs=[_main_spec(tm, d), _row_spec(d)],
        out_shape=[jax.ShapeDtypeStruct((s, d), F32), jax.ShapeDtypeStruct((1, d), F32)],
        scratch_shapes=[pltpu.VMEM((tm + POOL_HALO, d), F32), pltpu.VMEM((tm, d), F32)],
        compiler_params=_params("arbitrary"),
    )(dmixed, dmixed, h, gain, dh)


def _conf_fill_h(hext_ref, main_ref, before_ref, d, m):
    hb = before_ref[:, :d].astype(F32) * _sigmoid(before_ref[:, d:].astype(F32))
    hext_ref[pl.ds(0, CONF_HALO), :] = jnp.where(m > 0, hb, 0.0)
    hext_ref[pl.ds(CONF_HALO, main_ref.shape[0]), :] = main_ref[:, :d].astype(F32) * _sigmoid(main_ref[:, d:].astype(F32))


def _layernorm_parts(hc, g, b):
    mu = jnp.mean(hc, axis=-1, keepdims=True)
    xc = hc - mu
    rs = lax.rsqrt(jnp.mean(xc * xc, axis=-1, keepdims=True) + LN_EPS)
    xhat = xc * rs
    return xhat, rs, xhat * g + b


def _conf_mid_fwd(ag, dw, b_dw, ln_g, ln_b, name):
    s, d2 = ag.shape
    d = d2 // 2
    tm = _tile(s, 256, CONF_HALO)
    base = CONF_HALO - (CONF_CONV_W - 1)

    def body(main_ref, before_ref, dw_ref, bdw_ref, g_ref, b_ref, s_ref, hc_ref, hext_ref):
        m = pl.program_id(0)
        _conf_fill_h(hext_ref.at[0], main_ref, before_ref, d, m)
        _build_shifts(hext_ref)
        row_chunks = tm // CHUNK_ROWS

        def conv_chunk(i, carry):
            cols = _lane_chunk(i // row_chunks)
            r0 = pl.multiple_of((i % row_chunks) * CHUNK_ROWS, CHUNK_ROWS)
            taps = (dw_ref[kk:kk + 1, cols] * _shifted(hext_ref, base + kk, r0, CHUNK_ROWS, cols) for kk in range(CONF_CONV_W))
            hc_ref[pl.ds(r0, CHUNK_ROWS), cols] = bdw_ref[:, cols] + _sum_terms(taps, ways=1)
            return carry

        lax.fori_loop(0, row_chunks * (d // CHUNK_LANES), conv_chunk, 0)
        _, _, l = _layernorm_parts(hc_ref[...], g_ref[...], b_ref[...])
        s_ref[...] = (l * _sigmoid(l)).astype(s_ref.dtype)

    return _hosted(
        body, name=name, grid=(s // tm,),
        in_specs=[_main_spec(tm, d2), _before_spec(tm, CONF_HALO, d2), _row_spec(d, CONF_CONV_W), _row_spec(d),
                  _row_spec(d), _row_spec(d)],
        out_specs=[_main_spec(tm, d), _main_spec(tm, d)],
        out_shape=[jax.ShapeDtypeStruct((s, d), BF16), jax.ShapeDtypeStruct((s, d), F32)],
        scratch_shapes=[pltpu.VMEM((8, tm + CONF_HALO, d), F32)],
        compiler_params=_params("parallel"),
    )(ag, ag, dw, b_dw, ln_g, ln_b)


def _conf_out_bwd(dh, w, hc, ln_g, ln_b, name):
    s, d = dh.shape
    tm = _tile(s, 256)

    def body(dh_ref, w_ref, hc_ref, g_ref, b_ref, o_ref, dg_ref, db_ref, dbo_ref):
        first = pl.program_id(0) == 0
        dh_t = dh_ref[...]
        ds = _dot_nt(dh_t.astype(BF16), w_ref[...])
        xhat, rs, l = _layernorm_parts(hc_ref[...], g_ref[...], b_ref[...])
        sg = _sigmoid(l)
        dl = ds * sg * (1.0 + l * (1.0 - sg))
        dxh = dl * g_ref[...]
        o_ref[...] = rs * (dxh - jnp.mean(dxh, axis=-1, keepdims=True)
                           - xhat * jnp.mean(dxh * xhat, axis=-1, keepdims=True))
        _accumulate(dg_ref, _colsum(dl * xhat), first)
        _accumulate(db_ref, _colsum(dl), first)
        _accumulate(dbo_ref, _colsum(dh_t), first)

    return _hosted(
        body, name=name, grid=(s // tm,),
        in_specs=[_main_spec(tm, d), pl.BlockSpec((d, d), lambda m: (0, 0)), _main_spec(tm, d), _row_spec(d), _row_spec(d)],
        out_specs=[_main_spec(tm, d), _row_spec(d), _row_spec(d), _row_spec(d)],
        out_shape=[jax.ShapeDtypeStruct((s, d), F32)] + [jax.ShapeDtypeStruct((1, d), F32)] * 3,
        compiler_params=_params("arbitrary"),
    )(dh, w, hc, ln_g, ln_b)


def _conf_mid_bwd(dhc, ag, dw, name):
    s, d2 = ag.shape
    d = d2 // 2
    tm = _tile(s, 256, CONF_HALO)
    nm = s // tm
    kw = CONF_CONV_W
    base = CONF_HALO - (kw - 1)

    def body(dhc_ref, dhca_ref, main_ref, before_ref, dw_ref, o_ref, ddw_ref, dbdw_ref, dbpw_ref, hext_ref, dext_ref):
        m = pl.program_id(0)
        first = m == 0
        _conf_fill_h(hext_ref.at[0], main_ref, before_ref, d, m)
        _build_shifts(hext_ref)
        dext_ref[0, pl.ds(0, tm), :] = dhc_ref[...]
        dext_ref[0, pl.ds(tm, CONF_HALO), :] = jnp.where(m < nm - 1, dhca_ref[...], 0.0)
        _build_shifts(dext_ref)

        @pl.when(first)
        def _():
            ddw_ref[...] = jnp.zeros_like(ddw_ref)
            dbdw_ref[...] = jnp.zeros_like(dbdw_ref)
            dbpw_ref[...] = jnp.zeros_like(dbpw_ref)

        zero = jnp.zeros((8, CHUNK_LANES), F32)
        tap_group = 8

        def fold(x):
            return functools.reduce(lambda p, q: p + q, [x[i:i + 8] for i in range(0, CHUNK_ROWS, 8)])

        def lane_chunk(ci, carry):
            cols = _lane_chunk(ci)
            gate_cols = pl.ds(pl.multiple_of(d + ci * CHUNK_LANES, CHUNK_LANES), CHUNK_LANES)

            def through_conv(ri, sums):
                r0 = pl.multiple_of(ri * CHUNK_ROWS, CHUNK_ROWS)
                rows = pl.ds(r0, CHUNK_ROWS)
                dhh = _sum_terms((dw_ref[kk:kk + 1, cols] * _shifted(dext_ref, kw - 1 - kk, r0, CHUNK_ROWS, cols)
                                  for kk in range(kw)), ways=1)
                a = main_ref[rows, cols].astype(F32)
                sg = _sigmoid(main_ref[rows, gate_cols].astype(F32))
                da = dhh * sg
                dgate = dhh * a * sg * (1.0 - sg)
                o_ref[rows, cols] = da.astype(o_ref.dtype)
                o_ref[rows, gate_cols] = dgate.astype(o_ref.dtype)
                return sums[0] + fold(da), sums[1] + fold(dgate), sums[2] + fold(dext_ref[0, rows, cols])

            sum_da, sum_dgate, sum_dhc = lax.fori_loop(0, tm // CHUNK_ROWS, through_conv, (zero, zero, zero))
            dbdw_ref[:, cols] += _colsum(sum_dhc)
            dbpw_ref[:, cols] += _colsum(sum_da)
            dbpw_ref[:, gate_cols] += _colsum(sum_dgate)

            for k0 in range(0, kw, tap_group):
                group = range(k0, min(k0 + tap_group, kw))

                def tap_gradients(ri, accs, group=group):
                    for sub in range(0, CHUNK_ROWS, 8):
                        r0 = pl.multiple_of(ri * CHUNK_ROWS + sub, 8)
                        dhc_c = dext_ref[0, pl.ds(r0, 8), cols]
                        accs = tuple(acc + dhc_c * _shifted(hext_ref, base + kk, r0, 8, cols) for kk, acc in zip(group, accs))
                    return accs

                accs = lax.fori_loop(0, tm // CHUNK_ROWS, tap_gradients, (zero,) * len(group))
                for kk, acc in zip(group, accs):
                    ddw_ref[kk:kk + 1, cols] += _colsum(acc)
            return carry

        lax.fori_loop(0, d // CHUNK_LANES, lane_chunk, 0)

    return _hosted(
        body, name=name, grid=(nm,),
        in_specs=[_main_spec(tm, d), _after_spec(tm, CONF_HALO, d, s), _main_spec(tm, d2), _before_spec(tm, CONF_HALO, d2),
                  _row_spec(d, kw)],
        out_specs=[_main_spec(tm, d2), _row_spec(d, 32), _row_spec(d), _row_spec(d2)],
        out_shape=[jax.ShapeDtypeStruct((s, d2), BF16), jax.ShapeDtypeStruct((32, d), F32),
                   jax.ShapeDtypeStruct((1, d), F32), jax.ShapeDtypeStruct((1, d2), F32)],
        scratch_shapes=[pltpu.VMEM((8, tm + CONF_HALO, d), F32), pltpu.VMEM((8, tm + CONF_HALO, d), F32)],
        compiler_params=_params("arbitrary"),
    )(dhc, dhc, ag, ag, dw)


def _loss_head(h, gain, target, name):
    s, d = h.shape
    tm = _tile(s, 512)

    def body(h_ref, g_ref, t_ref, loss_ref, dh_ref, dg_ref):
        first = pl.program_id(0) == 0
        x = h_ref[...]
        err = x * _rms_stats(x) * g_ref[...] - t_ref[...]
        part = 0.5 * jnp.sum(jnp.mean(err * err, axis=-1, keepdims=True), axis=0, keepdims=True)
        dx, dg = _rms_bwd(err * (1.0 / d), x, g_ref[...])
        dh_ref[...] = dx
        _accumulate(loss_ref, part, first)
        _accumulate(dg_ref, dg, first)

    return _hosted(
        body, name=name, grid=(s // tm,),
        in_specs=[_main_spec(tm, d), _row_spec(d), _main_spec(tm, d)],
        out_specs=[pl.BlockSpec((1, 1), lambda m: (0, 0)), _main_spec(tm, d), _row_spec(d)],
        out_shape=[jax.ShapeDtypeStruct((1, 1), F32), jax.ShapeDtypeStruct((s, d), F32), jax.ShapeDtypeStruct((1, d), F32)],
        compiler_params=_params("arbitrary"),
    )(h, gain, target)


def _ffn_fwd(h, wts, i):
    u = _rms_fwd(h, wts[f"ln2_{i}"], f"ffn{i}_rms")
    act, s1, q1 = _ffn_up(u, wts[f"ffn{i}_w_gu"], f"ffn{i}_up")
    h_new = _mm_row(act, wts[f"ffn{i}_w_down"], h, None, f"ffn{i}_down")
    return h_new, (h, u, act, s1, q1)


def _ffn_bwd(dh, saved, wts, i, g):
    h, u, act, s1, q1 = saved
    dgu = _ffn_down_bwd(dh, wts[f"ffn{i}_w_down"], s1, q1, f"ffn{i}_down_bwd")
    g[f"ffn{i}_w_down"] = _mm_tn(act, dh, 1, f"ffn{i}_dw_down")
    g[f"ffn{i}_w_gu"] = _mm_tn(u, dgu, N_CHIPS, f"ffn{i}_dw_gu")
    dh_new, g[f"ln2_{i}"] = _mm_nt_col_rms_bwd(dgu, wts[f"ffn{i}_w_gu"], h, wts[f"ln2_{i}"], dh, f"ffn{i}_up_bwd")
    return dh_new


def _device_step(x, target, wts, g=None):
    g = {} if g is None else g
    saved = {}
    h = x

    def short_conv_fwd(h, i):
        u = _rms_fwd(h, wts[f"ln1_{i}"], f"a{i}_rms")
        bcv = _mm_col(u, wts[f"a{i}_w_in"], None, f"a{i}_in")
        p = _sconv_fwd(bcv, wts[f"a{i}_conv"], f"a{i}_conv")
        return _mm_row(p, wts[f"a{i}_w_out"], h, None, f"a{i}_out"), (h, u, bcv, p)

    def short_conv_bwd(dh, sv, i):
        h, u, bcv, p = sv
        dp = _mm_nt_row(dh, wts[f"a{i}_w_out"], f"a{i}_out_bwd")
        dbcv, dcw = _sconv_bwd(dp, bcv, wts[f"a{i}_conv"], f"a{i}_conv_bwd")
        g[f"a{i}_conv"] = dcw[:SHORT_CONV_W]
        g[f"a{i}_w_in"] = _mm_tn(u, dbcv, N_CHIPS, f"a{i}_dw_in")
        g[f"a{i}_w_out"] = _mm_tn(p, dh, 1, f"a{i}_dw_out")
        dh, g[f"ln1_{i}"] = _mm_nt_col_rms_bwd(dbcv, wts[f"a{i}_w_in"], h, wts[f"ln1_{i}"], dh, f"a{i}_in_bwd")
        return dh

    h, saved["a0"] = short_conv_fwd(h, 0)
    h, saved["f0"] = _ffn_fwd(h, wts, 0)

    h_in = h
    h, mixed = _pool_fwd(h, wts["ln1_1"], wts["b1_w_grp"], wts["b1_scale"], "b1_fwd")
    saved["b1"] = (h_in, mixed)
    h, saved["f1"] = _ffn_fwd(h, wts, 1)

    h_in = h
    u = _rms_fwd(h, wts["ln1_2"], "c2_rms")
    ag = _mm_col(u, wts["c2_w_pw1"], wts["c2_b_pw1"], "c2_pw1")
    sw, hc = _conf_mid_fwd(ag, wts["c2_dw"], wts["c2_b_dw"], wts["c2_ln_g"], wts["c2_ln_b"], "c2_mid")
    h = _mm_row(sw, wts["c2_w_pw2"], h, wts["c2_b_pw2"], "c2_pw2")
    saved["c2"] = (h_in, u, ag, sw, hc)
    h, saved["f2"] = _ffn_fwd(h, wts, 2)

    h, saved["a3"] = short_conv_fwd(h, 3)
    h, saved["f3"] = _ffn_fwd(h, wts, 3)

    loss, dh, g["ln_f"] = _loss_head(h, wts["ln_f"], target, "loss_head")

    def ffn_bwd(dh, i):
        return _ffn_bwd(dh, saved[f"f{i}"], wts, i, g)

    dh = ffn_bwd(dh, 3)
    dh = short_conv_bwd(dh, saved["a3"], 3)

    dh = ffn_bwd(dh, 2)
    h_in, u, ag, sw, hc = saved["c2"]
    dhc, g["c2_ln_g"], g["c2_ln_b"], g["c2_b_pw2"] = _conf_out_bwd(
        dh, wts["c2_w_pw2"], hc, wts["c2_ln_g"], wts["c2_ln_b"], "c2_pw2_bwd")
    g["c2_w_pw2"] = _mm_tn(sw, dh, 1, "c2_dw_pw2")
    dag, ddw, g["c2_b_dw"], g["c2_b_pw1"] = _conf_mid_bwd(dhc, ag, wts["c2_dw"], "c2_mid_bwd")
    g["c2_dw"] = ddw[:CONF_CONV_W]
    g["c2_w_pw1"] = _mm_tn(u, dag, N_CHIPS, "c2_dw_pw1")
    dh, g["ln1_2"] = _mm_nt_col_rms_bwd(dag, wts["c2_w_pw1"], h_in, wts["ln1_2"], dh, "c2_pw1_bwd")

    dh = ffn_bwd(dh, 1)
    h_in, mixed = saved["b1"]
    dmixed, g["b1_w_grp"], g["b1_scale"] = _pool_bwd_mm(dh, mixed, wts["b1_w_grp"], wts["b1_scale"], "b1_bwd_mm")
    dh, g["ln1_1"] = _pool_bwd_rms(dmixed, h_in, wts["ln1_1"], dh, "b1_bwd_rms")

    dh = ffn_bwd(dh, 0)
    dh = short_conv_bwd(dh, saved["a0"], 0)
    return loss, dh, g


MESH = pl.DeviceIdType.MESH
ANY = pl.BlockSpec(memory_space=pl.ANY)


def _position():
    return lax.axis_index("x"), lax.axis_index("y"), lax.axis_index("c")


def _other_chips(x, y):
    return [(1 - x, y), (x, 1 - y), (1 - x, 1 - y)]


def _remote(src, dst, send_sem, recv_sem, to):
    return pltpu.make_async_remote_copy(src_ref=src, dst_ref=dst, send_sem=send_sem, recv_sem=recv_sem,
                                        device_id=to, device_id_type=MESH)


def _half_rows(ref_rows, c):
    hr = ref_rows // 2
    return pl.ds(pl.multiple_of(c * hr, 16), hr)


def _allgather8(v, name):
    m_per, n = v.shape

    def body(v_ref, out_ref, send_sems, recv_sems, local_sem):
        x, y, c = _position()
        me, sibling = (x, y, c), (x, y, 1 - c)
        chips = _other_chips(x, y)

        def rows(px, py, pc):
            return out_ref.at[pl.ds((4 * px + 2 * py + pc) * m_per, m_per), :]

        def copy(k, block, to, src=None):
            return _remote(rows(*block) if src is None else src, rows(*block), send_sems.at[k], recv_sems.at[k], to)

        mine = pltpu.make_async_copy(v_ref, rows(*me), local_sem)
        mine.start()
        first = [copy(0, me, sibling, src=v_ref)]
        first += [copy(1 + j, me, (*chip, c), src=v_ref) for j, chip in enumerate(chips)]
        for cp in first:
            cp.start()
        passed = [copy(4 + j, (*chip, c), sibling) for j, chip in enumerate(chips)]
        for j, chip in enumerate(chips):
            copy(1 + j, (*chip, c), me).wait_recv()
            passed[j].start()
        copy(0, sibling, me).wait_recv()
        for j, chip in enumerate(chips):
            copy(4 + j, (*chip, 1 - c), me).wait_recv()
        for cp in first + passed:
            cp.wait_send()
        mine.wait()

    return _hosted(
        body, name=name,
        out_shape=jax.ShapeDtypeStruct((N_DEV * m_per, n), v.dtype),
        in_specs=[pl.BlockSpec(memory_space=pltpu.VMEM)],
        out_specs=pl.BlockSpec(memory_space=pltpu.VMEM),
        scratch_shapes=[pltpu.SemaphoreType.DMA((7,)), pltpu.SemaphoreType.DMA((7,)), pltpu.SemaphoreType.DMA],
        compiler_params=pltpu.CompilerParams(vmem_limit_bytes=VMEM_LIMIT),
    )(v)


def _cast_to_slot(ws, idx, name):
    r, cols = ws[0].shape
    assert all(w.shape == (r, cols) for w in ws)
    n = len(ws)
    tr = _tile(r, 256, 16)

    def body(idx_ref, *refs):
        for w_ref, o_ref in zip(refs[:n], refs[n:]):
            o_ref[...] = w_ref[...].astype(o_ref.dtype)

    return _hosted(
        body, name=name,
        grid_spec=pltpu.PrefetchScalarGridSpec(
            num_scalar_prefetch=1, grid=(r // tr,),
            in_specs=[pl.BlockSpec((tr, cols), lambda t, idx_ref: (t, 0))] * n,
            out_specs=[pl.BlockSpec((None, tr, cols), lambda t, idx_ref: (idx_ref[0], t, 0))] * n),
        out_shape=[jax.ShapeDtypeStruct((N_CHIPS, r, cols), BF16)] * n,
        compiler_params=_params("parallel"),
    )(idx, *ws)


def _dma_sems(*shape):
    return [pltpu.SemaphoreType.DMA(shape), pltpu.SemaphoreType.DMA(shape)]


def _same_shapes(arrays):
    return [jax.ShapeDtypeStruct(a.shape, a.dtype) for a in arrays]


def _part_rows(ref_rows, c, part):
    hr = ref_rows // 2
    i, n = part
    size = hr // n
    assert size * n == hr and size % 16 == 0, (ref_rows, part)
    return pl.ds(pl.multiple_of(c * hr + i * size, 16), size)


def _task_gather_ici(bufs, done, part=(0, 1)):
    n = len(bufs)

    def copies(outs, sems, landing):
        x, y, c = _position()
        my_chip = 2 * x + y
        res = []
        for i in range(n):
            rows = _part_rows(bufs[i].shape[1], c, part)
            for r, (px, py) in enumerate(_other_chips(x, y)):
                slot = (2 * px + py) if landing else my_chip
                res.append(_remote(outs[i].at[my_chip, rows, :], outs[i].at[slot, rows, :], sems[0].at[i, r], sems[1].at[i, r],
                                   (px, py, c)))
        return res

    def start(ins, outs, sems):
        for cp in copies(outs, sems, False):
            cp.start()

    def wait(ins, outs, sems):
        for cp in copies(outs, sems, True):
            cp.wait_recv()
            cp.wait_send()

    return _Task(bufs, _same_shapes(bufs), {i: i for i in range(n)}, _dma_sems(n, 3), start, wait, done)


def _task_gather_d2d(bufs, done):
    n = len(bufs)

    def copies(outs, sems, landing):
        x, y, c = _position()
        res = []
        for i in range(n):
            rows = _half_rows(bufs[i].shape[1], (1 - c) if landing else c)
            for r, (px, py) in enumerate(_other_chips(x, y)):
                part = outs[i].at[2 * px + py, rows, :]
                res.append(_remote(part, part, sems[0].at[i, r], sems[1].at[i, r], (x, y, 1 - c)))
        return res

    def start(ins, outs, sems):
        for cp in copies(outs, sems, False):
            cp.start()

    def wait(ins, outs, sems):
        for cp in copies(outs, sems, True):
            cp.wait_recv()
        for cp in copies(outs, sems, False):
            cp.wait_send()

    return _Task(bufs, _same_shapes(bufs), {i: i for i in range(n)}, _dma_sems(n, 3), start, wait, done)


def _task_sibling_halves(grads, done):
    n = len(grads)

    def copies(ins, outs, sems):
        x, y, c = _position()
        return [_remote(ins[i].at[:, _half_rows(grads[i].shape[1], 1 - c), :], outs[i], sems[0].at[i], sems[1].at[i],
                        (x, y, 1 - c)) for i in range(n)]

    def start(ins, outs, sems):
        for cp in copies(ins, outs, sems):
            cp.start()

    def wait(ins, outs, sems):
        for cp in copies(ins, outs, sems):
            cp.wait()

    shapes = [jax.ShapeDtypeStruct((g.shape[0], g.shape[1] // 2, g.shape[2]), g.dtype) for g in grads]
    return _Task(grads, shapes, {}, _dma_sems(n), start, wait, done)


def _task_chip_sums(parts, done, landed=None, part=(0, 1)):
    n = len(parts)
    i_part, n_parts = part
    sizes = [p.shape[1] // n_parts for p in parts]
    assert all(p.shape[1] == size * n_parts and size % 16 == 0 for p, size in zip(parts, sizes)), part
    rows = [pl.ds(i_part * size, size) for size in sizes]

    def copies(ins, outs, sems):
        x, y, c = _position()
        return [_remote(ins[i].at[2 * px + py, rows[i], :], outs[i].at[r, rows[i], :], sems[0].at[i, r], sems[1].at[i, r],
                        (px, py, c))
                for i in range(n) for r, (px, py) in enumerate(_other_chips(x, y))]

    def start(ins, outs, sems):
        for cp in copies(ins, outs, sems):
            cp.start()

    def wait(ins, outs, sems):
        for cp in copies(ins, outs, sems):
            cp.wait()

    shapes = [jax.ShapeDtypeStruct((3,) + p.shape[1:], p.dtype) for p in parts]
    if landed is None:
        return _Task(parts, shapes, {}, _dma_sems(n, 3), start, wait, done)
    return _Task(list(parts) + list(landed), shapes, {n + i: i for i in range(n)}, _dma_sems(n, 3), start, wait, done)


def _task_sibling_parts(owns, landeds, done):
    n = len(owns)

    def copies(ins, outs, sems):
        x, y, c = _position()
        sibling = (x, y, 1 - c)
        res = []
        for i in range(n):
            res.append(_remote(ins[i].at[2 * x + y], outs[i].at[0], sems[0].at[i, 0], sems[1].at[i, 0], sibling))
            res.append(_remote(ins[n + i], outs[i].at[pl.ds(1, 3)], sems[0].at[i, 1], sems[1].at[i, 1], sibling))
        return res

    def start(ins, outs, sems):
        for cp in copies(ins, outs, sems):
            cp.start()

    def wait(ins, outs, sems):
        for cp in copies(ins, outs, sems):
            cp.wait()

    return _Task(list(owns) + list(landeds), _same_shapes(owns), {}, _dma_sems(n, 2), start, wait, done)


def _add_halves(grad, sib, c, name):
    nsh, r, cols = grad.shape
    hr = r // 2
    tr = _tile(hr, 512, 16)
    nt = hr // tr

    def body(c_ref, g_ref, s_ref, o_ref):
        o_ref[...] = (g_ref[...].astype(F32) + s_ref[...].astype(F32)).astype(o_ref.dtype)

    return _hosted(
        body, name=name,
        grid_spec=pltpu.PrefetchScalarGridSpec(
            num_scalar_prefetch=1, grid=(nsh, nt),
            in_specs=[pl.BlockSpec((None, tr, cols), lambda j, t, c_ref: (j, c_ref[1] * nt + t, 0)),
                      pl.BlockSpec((None, tr, cols), lambda j, t, c_ref: (j, t, 0))],
            out_specs=pl.BlockSpec((None, tr, cols), lambda j, t, c_ref: (j, t, 0))),
        out_shape=jax.ShapeDtypeStruct((nsh, hr, cols), BF16),
        compiler_params=_params("parallel", "parallel"),
    )(c, grad, sib)


def _adamw_reduced(w, own, landed, sib, m, v, idx, name):
    r, cols = w.shape
    hr = r // 2
    tr = _tile(hr, 256, 16)
    nt = hr // tr

    def body(idx_ref, w_ref, p_ref, l_ref, s_ref, m_ref, v_ref, go_ref, d_ref, mo_ref, vo_ref):
        mine = p_ref[...].astype(F32)
        for k in range(3):
            mine = mine + l_ref[k].astype(F32)
        theirs = s_ref[0].astype(F32)
        for k in range(1, 4):
            theirs = theirs + s_ref[k].astype(F32)
        grad = jnp.where(pl.program_id(0) // nt == idx_ref[1], mine, theirs)
        go_ref[...] = grad
        d_ref[...], mo_ref[...], vo_ref[...] = _adamw_update(w_ref[...], grad, m_ref[...], v_ref[...])

    def in_half(t, half):
        return jnp.clip(t - half * nt, 0, nt - 1)

    full = pl.BlockSpec((tr, cols), lambda t, idx_ref: (t, 0))
    return _hosted(
        body, name=name,
        grid_spec=pltpu.PrefetchScalarGridSpec(
            num_scalar_prefetch=1, grid=(2 * nt,),
            in_specs=[full,
                      pl.BlockSpec((None, tr, cols), lambda t, idx_ref: (idx_ref[0], in_half(t, idx_ref[1]), 0)),
                      pl.BlockSpec((3, tr, cols), lambda t, idx_ref: (0, in_half(t, idx_ref[1]), 0)),
                      pl.BlockSpec((4, tr, cols), lambda t, idx_ref: (0, in_half(t, 1 - idx_ref[1]), 0)),
                      full, full],
            out_specs=[full] * 4),
        out_shape=[jax.ShapeDtypeStruct((r, cols), F32)] * 4,
        compiler_params=_params("arbitrary"),
    )(idx, w, own, landed, sib, m, v)


def _sum_devices(blocks, name):
    m8, n = blocks.shape
    m = m8 // N_DEV

    def body(b_ref, o_ref):
        acc = b_ref[pl.ds(0, m), :]
        for k in range(1, N_DEV):
            acc = acc + b_ref[pl.ds(k * m, m), :]
        o_ref[...] = acc

    return _hosted(
        body, name=name, out_shape=jax.ShapeDtypeStruct((m, n), F32),
        in_specs=[pl.BlockSpec(memory_space=pltpu.VMEM)], out_specs=pl.BlockSpec(memory_space=pltpu.VMEM),
        compiler_params=pltpu.CompilerParams(vmem_limit_bytes=VMEM_LIMIT),
    )(blocks)


def _adamw_update(w, grad, m, v):
    new_m = ADAM_B1 * m + (1.0 - ADAM_B1) * grad
    new_v = ADAM_B2 * v + (1.0 - ADAM_B2) * (grad * grad)
    m_hat = new_m * (1.0 / (1.0 - ADAM_B1 ** ADAM_STEP))
    v_hat = new_v * (1.0 / (1.0 - ADAM_B2 ** ADAM_STEP))
    return -ADAM_LR * (m_hat / (jnp.sqrt(v_hat) + ADAM_EPS) + ADAM_WD * w), new_m, new_v


def _adamw(w, g, m, v, name):
    r, cols = w.shape
    tr = _tile(r, 256)

    def body(w_ref, g_ref, m_ref, v_ref, go_ref, d_ref, mo_ref, vo_ref):
        grad = g_ref[...]
        go_ref[...] = grad
        d_ref[...], mo_ref[...], vo_ref[...] = _adamw_update(w_ref[...], grad, m_ref[...], v_ref[...])

    spec = pl.BlockSpec((tr, cols), lambda t: (t, 0))
    return _hosted(
        body, name=name, grid=(r // tr,), in_specs=[spec] * 4, out_specs=[spec] * 4,
        out_shape=[jax.ShapeDtypeStruct((r, cols), F32)] * 4,
        compiler_params=_params("parallel"),
    )(w, g, m, v)


def _adamw_small(grad_blocks, params, name):
    nb, npar = len(grad_blocks), len(params)

    def body(*refs):
        blocks, ins, outs = refs[:nb], refs[nb:nb + 3 * npar], refs[nb + 3 * npar:]
        for p, (w, _, _, blk, row0) in enumerate(params):
            grad = blocks[blk][pl.ds(row0, w.shape[0]), :]
            outs[4 * p][...] = grad
            outs[4 * p + 1][...], outs[4 * p + 2][...], outs[4 * p + 3][...] = _adamw_update(
                ins[3 * p][...], grad, ins[3 * p + 1][...], ins[3 * p + 2][...])

    args = list(grad_blocks) + [a for w, m, v, _, _ in params for a in (w, m, v)]
    vmem = pl.BlockSpec(memory_space=pltpu.VMEM)
    out = _hosted(
        body, name=name, in_specs=[vmem] * len(args), out_specs=[vmem] * (4 * npar),
        out_shape=[jax.ShapeDtypeStruct(w.shape, F32) for w, _, _, _, _ in params for _ in range(4)],
    )(*args)
    return [tuple(out[4 * p:4 * p + 4]) for p in range(npar)]


WEIGHT_NAMES = (
    "ln1_0", "a0_w_in", "a0_conv", "a0_w_out", "ln2_0", "ffn0_w_gu", "ffn0_w_down",
    "ln1_1", "b1_w_grp", "b1_scale", "ln2_1", "ffn1_w_gu", "ffn1_w_down",
    "ln1_2", "c2_w_pw1", "c2_b_pw1", "c2_dw", "c2_b_dw", "c2_ln_g", "c2_ln_b", "c2_w_pw2", "c2_b_pw2",
    "ln2_2", "ffn2_w_gu", "ffn2_w_down",
    "ln1_3", "a3_w_in", "a3_conv", "a3_w_out", "ln2_3", "ffn3_w_gu", "ffn3_w_down", "ln_f")
BIG = ("a0_w_in", "a0_w_out", "ffn0_w_gu", "ffn0_w_down", "b1_w_grp", "ffn1_w_gu", "ffn1_w_down", "c2_w_pw1", "c2_w_pw2",
       "ffn2_w_gu", "ffn2_w_down", "a3_w_in", "a3_w_out", "ffn3_w_gu", "ffn3_w_down")
GROUPED = "b1_w_grp"
SMALL_SHARDED = ("a0_conv", "a3_conv", "c2_dw")
REPLICATED = tuple(n for n in WEIGHT_NAMES if n not in BIG and n not in SMALL_SHARDED)


def _pad_rows(a, mult=8):
    pad = -a.shape[0] % mult
    return a if pad == 0 else jnp.concatenate([a, jnp.zeros((pad, a.shape[1]), a.dtype)], axis=0)


def _pack_rows(parts, width):
    rows = [p.reshape(-1, width) for p in parts]
    return _pad_rows(jnp.concatenate(rows, axis=0)), [r.shape[0] for r in rows]


def _unpack_rows(packed, counts, shapes):
    out, at = [], 0
    for n, shp in zip(counts, shapes):
        out.append(packed[at:at + n].reshape(shp))
        at += n
    return out


COLUMN_SHARDED = ("w_in", "w_gu", "w_pw1")


class _Weights(dict):
    def __init__(self, bufs):
        super().__init__()
        self.bufs = bufs

    def __missing__(self, name):
        buf = self.bufs[name]
        if name == GROUPED:
            cg = buf.shape[-1]
            rq = cg // N_CHIPS
            return jnp.transpose(buf.reshape(N_CHIPS, -1, rq, cg), (1, 0, 2, 3)).reshape(-1, cg, cg)
        return buf if name.endswith(COLUMN_SHARDED) else buf.reshape(-1, buf.shape[-1])


class _Exchange:
    def __init__(self, w, mom, vel, idx):
        def shards(table):
            return {n: table[n].reshape(-1, table[n].shape[-1]) for n in BIG}

        self.w, self.mom, self.vel, self.idx = shards(w), shards(mom), shards(vel), idx
        self.bufs = {}
        self.weights = _Weights(self.bufs)
        self.grads = {}
        self.sib, self.part, self.landed, self.sib_parts, self.updates = {}, {}, {}, {}, {}

    def cast(self, names):
        by_shape = {}
        for n in names:
            by_shape.setdefault(self.w[n].shape, []).append(n)
        for group in by_shape.values():
            self.bufs.update(zip(group, _cast_to_slot([self.w[n] for n in group], self.idx, f"cast_{group[0]}")))

    @staticmethod
    def _store(table, names):
        def done(arrays):
            table.update(zip(names, arrays))
        return done

    def _grad(self, n):
        g = self.grads[n]
        if n == GROUPED:
            ng, cg, _ = g.shape
            g = jnp.transpose(g.reshape(ng, N_CHIPS, cg // N_CHIPS, cg), (1, 0, 2, 3)).astype(BF16)
        return g.reshape(N_CHIPS, -1, g.shape[-1])

    def gather_ici(self, *names, part=(0, 1)):
        return lambda: _task_gather_ici([self.bufs[n] for n in names], self._store(self.bufs, names), part)

    def gather_d2d(self, *names):
        return lambda: _task_gather_d2d([self.bufs[n] for n in names], self._store(self.bufs, names))

    def sibling_halves(self, *names):
        return lambda: _task_sibling_halves([self._grad(n) for n in names], self._store(self.sib, names))

    def add_halves(self, *names):
        def run():
            for n in names:
                self.part[n] = _add_halves(self._grad(n), self.sib.pop(n), self.idx, f"reduce_add_{n}")
        return run

    def chip_sums(self, *names, part=(0, 1)):
        def make():
            landed = [self.landed[n] for n in names] if part[0] > 0 else None
            return _task_chip_sums([self.part[n] for n in names], self._store(self.landed, names), landed, part)
        return make

    def sibling_parts(self, *names):
        return lambda: _task_sibling_parts([self.part[n] for n in names], [self.landed[n] for n in names],
                                           self._store(self.sib_parts, names))

    def adamw(self, *names):
        def run():
            for n in names:
                self.updates[n] = _adamw_reduced(self.w[n], self.part.pop(n), self.landed.pop(n), self.sib_parts.pop(n),
                                                 self.mom[n], self.vel[n], self.idx, f"adamw_{n}")
        return run


def _plan(ex):
    s = _Schedule()

    def ffn(i):
        return f"ffn{i}_w_gu", f"ffn{i}_w_down"

    c2, a3 = ("c2_w_pw1", "c2_w_pw2"), ("a3_w_in", "a3_w_out")
    first, second = (0, 2), (1, 2)
    gu, down = ffn(0)
    s.host("a0_rms", ex.gather_ici("a0_w_out"))
    s.host("a0_in", ex.gather_ici(gu, part=first), ex.gather_d2d("a0_w_out"))
    s.host("a0_conv", ex.gather_ici(gu, part=second))
    s.host("a0_out", ex.gather_ici(down), ex.gather_d2d(gu))
    s.host("ffn0_up", ex.gather_d2d(down))
    gu, down = ffn(1)
    s.host("ffn0_up", ex.gather_ici(gu, GROUPED))
    s.host("ffn0_down", ex.gather_ici(down), ex.gather_d2d(gu, GROUPED))
    s.host("ffn1_up", ex.gather_d2d(down), ex.gather_ici(*c2))
    s.host("ffn1_down", ex.gather_d2d(*c2))
    gu, down = ffn(2)
    s.host("c2_pw1", ex.gather_ici(down))
    s.host("c2_mid", ex.gather_ici(gu))
    s.host("c2_pw2", ex.gather_d2d(gu, down))
    gu, down = ffn(3)
    s.host("ffn2_up", ex.gather_ici(*a3))
    s.host("ffn2_down", ex.gather_d2d(*a3), ex.gather_ici(down))
    s.host("a3_in", ex.gather_ici(gu, part=first))
    s.host("a3_conv", ex.gather_ici(gu, part=second))
    s.host("a3_out", ex.gather_d2d(gu, down))

    def reduce_on(names, first_host, ici_hosts, last_host):
        s.host(first_host, ex.sibling_halves(*names))
        s.post(first_host, ex.add_halves(*names))
        for host, hosted, part in ici_hosts:
            s.host(host, ex.chip_sums(*hosted, part=part))
        s.host(last_host, ex.sibling_parts(*names))
        s.post(last_host, ex.adamw(*names))

    whole = (0, 1)
    gu, down = ffn(3)
    reduce_on((gu, down), "a3_out_bwd",
              [("a3_conv_bwd", (down,), whole), ("a3_dw_in", (gu,), first), ("a3_in_bwd", (gu,), second)], "ffn2_down_bwd")
    reduce_on(a3, "ffn2_down_bwd", [("ffn2_dw_down", a3[:1], whole), ("ffn2_dw_gu", a3[1:], whole)], "c2_pw2_bwd")
    gu, down = ffn(0)
    s.host("ffn0_dw_gu", ex.sibling_halves(down))
    s.post("ffn0_dw_gu", ex.add_halves(down))
    s.host("ffn0_up_bwd", ex.chip_sums(down))
    s.host("a0_out_bwd", ex.sibling_halves(gu, GROUPED))
    s.post("a0_out_bwd", ex.add_halves(gu, GROUPED))
    s.host("a0_conv_bwd", ex.chip_sums(gu, part=first), ex.chip_sums(GROUPED))
    s.host("a0_dw_in", ex.chip_sums(gu, part=second))
    s.host("a0_dw_out", ex.sibling_halves("a0_w_in"))
    s.post("a0_dw_out", ex.add_halves("a0_w_in"))
    s.host("a0_in_bwd", ex.chip_sums("a0_w_in"))
    reduce_on(ffn(2), "c2_pw2_bwd", [("c2_mid_bwd", ffn(2), whole)], "ffn1_down_bwd")
    reduce_on(c2, "ffn1_down_bwd", [("ffn1_dw_down", c2, whole)], "b1_bwd_mm")
    gu, down = ffn(1)
    reduce_on((gu, down), "b1_bwd_mm",
              [("b1_bwd_rms", (down,), whole), ("ffn0_down_bwd", (gu,), first), ("ffn0_dw_down", (gu,), second)], "ffn0_dw_gu")
    return s


def kernel(x, *rest):
    nw = len(WEIGHT_NAMES)
    w = dict(zip(WEIGHT_NAMES, rest[:nw]))
    target = rest[nw]
    mom = dict(zip(WEIGHT_NAMES, rest[nw + 1:2 * nw + 1]))
    vel = dict(zip(WEIGHT_NAMES, rest[2 * nw + 1:3 * nw + 1]))
    cx, cy, cc = _position()
    my_chip = 2 * cx + cy
    d = x.shape[-1]
    cq = d // N_CHIPS

    ex = _Exchange(w, mom, vel, jnp.stack([my_chip, cc]).astype(jnp.int32))
    sched = _plan(ex)
    ex.cast(BIG)
    _comm_only([ex.gather_ici("a0_w_in")()], "gather_a0_ici")
    _comm_only([ex.gather_d2d("a0_w_in")()], "gather_a0_d2d")

    small_blk, small_counts = _pack_rows([w[n] for n in SMALL_SHARDED], cq)
    small_all = _allgather8(small_blk, "gather_small").reshape(N_CHIPS, 2, small_blk.shape[0], cq)[:, 0]
    small_parts = _unpack_rows(jnp.transpose(small_all, (1, 0, 2)), small_counts,
                               [(w[n].reshape(-1, cq).shape[0], N_CHIPS, cq) for n in SMALL_SHARDED])
    wts = ex.weights
    for n in REPLICATED:
        wts[n] = w[n].reshape(1, -1)
    for n, part in zip(SMALL_SHARDED, small_parts):
        wts[n] = part.reshape(part.shape[0], d)

    _ACTIVE_SCHEDULE[0] = sched
    try:
        loss, dx, g = _device_step(x[0], target[0], wts, ex.grads)
    finally:
        _ACTIVE_SCHEDULE[0] = None
    assert not sched.hosts and not sched.posts, (sched.hosts, sched.posts)

    summed, last = ("ffn0_w_gu", "ffn0_w_down", GROUPED, "a0_w_in"), "a0_w_out"
    _comm_only([ex.sibling_parts(*summed)(), ex.sibling_halves(last)()], "reduce_tail_d2d")
    ex.adamw(*summed)()
    ex.add_halves(last)()
    _comm_only([ex.chip_sums(last)()], "reduce_tail_ici")
    _comm_only([ex.sibling_parts(last)()], "reduce_tail_parts")
    ex.adamw(last)()

    rep_rows = [g[n].reshape(-1, cq) for n in REPLICATED]
    by_chip = [jnp.transpose(g[n].reshape(g[n].shape[0], N_CHIPS, cq), (1, 0, 2)) for n in SMALL_SHARDED]
    shard_rows = jnp.concatenate(by_chip, axis=1)
    n_rep, n_shard = sum(r.shape[0] for r in rep_rows), shard_rows.shape[1]
    loss_row = jnp.broadcast_to(loss, (1, cq))
    sm_blk = _pad_rows(jnp.concatenate(rep_rows + [loss_row, shard_rows.reshape(N_CHIPS * n_shard, cq)], axis=0))
    sm_sum = _sum_devices(_allgather8(sm_blk, "gather_small_grads"), "sum_small_grads")
    mine = lax.dynamic_slice_in_dim(sm_sum, n_rep + 1 + my_chip * n_shard, n_shard, axis=0)

    out = ex.updates
    params, at = [], {0: 0, 1: 0}
    for block, names in ((0, REPLICATED), (1, SMALL_SHARDED)):
        for n in names:
            w2, m2, v2 = (a[n].reshape(-1, cq) for a in (w, mom, vel))
            params.append((w2, m2, v2, block, at[block]))
            at[block] += w2.shape[0]
    updated = _adamw_small([sm_sum, mine], params, "adamw_small")
    for n, res in zip(REPLICATED + SMALL_SHARDED, updated):
        out[n] = tuple(r.reshape(w[n].shape) for r in res)

    total = sm_sum[n_rep, 0]
    grads, deltas, new_m, new_v = ([out[n][k].reshape(w[n].shape) for n in WEIGHT_NAMES] for k in range(4))
    return (total, dx.reshape(x.shape), *grads, *deltas, *new_m, *new_v)
```

```python
import functools

import jax
import jax.numpy as jnp
from jax import lax
from jax.experimental import pallas as pl
from jax.experimental.pallas import tpu as pltpu

F32 = jnp.float32
BF16 = jnp.bfloat16

RMS_EPS = 1e-6
LN_EPS = 1e-5
POOL_WINDOWS = (2, 4, 8, 16)
SHORT_CONV_W = 3
CONF_CONV_W = 31
N_CHIPS = 4
N_DEV = 8

ADAM_LR = 0.001
ADAM_B1 = 0.9
ADAM_B2 = 0.999
ADAM_EPS = 1e-08
ADAM_WD = 0.01
ADAM_STEP = 10

V7X_VMEM_BYTES = 64 * 1024 * 1024
VMEM_LIMIT = V7X_VMEM_BYTES - 8 * 1024 * 1024
LANES = 128
POOL_HALO = 16
SCONV_HALO = 16
CONF_HALO = 32


def _params(*sem):
    return pltpu.CompilerParams(dimension_semantics=sem, vmem_limit_bytes=VMEM_LIMIT)


def _tile(n, pref, mult=8):
    t = min(n, pref)
    while t > mult and (n % t or t % mult):
        t -= mult
    assert n % t == 0 and t % mult == 0, (n, pref, mult)
    return t


def _sigmoid(x):
    return jax.nn.sigmoid(x)


def _dot(a, b):
    return jnp.dot(a, b, preferred_element_type=F32)


def _dot_nt(a, b):
    return lax.dot_general(a, b, (((1,), (1,)), ((), ())), preferred_element_type=F32)


def _dot_tn(a, b):
    return lax.dot_general(a, b, (((0,), (0,)), ((), ())), preferred_element_type=F32)


def _colsum(x):
    return jnp.sum(x, axis=0, keepdims=True)


def _rms_stats(x):
    return lax.rsqrt(jnp.mean(x * x, axis=-1, keepdims=True) + RMS_EPS)


def _rms_bwd(du, x, gain):
    r = _rms_stats(x)
    xhat = x * r
    gdy = du * gain
    dx = r * (gdy - xhat * jnp.mean(gdy * xhat, axis=-1, keepdims=True))
    return dx, _colsum(du * xhat)


class _Task:
    def __init__(self, ins, out_shapes, aliases, sems, start, wait, done):
        self.ins, self.out_shapes, self.aliases, self.sems = list(ins), list(out_shapes), dict(aliases), list(sems)
        self.start, self.wait, self.done = start, wait, done


class _Schedule:
    def __init__(self):
        self.hosts, self.posts = {}, {}

    def host(self, kernel_name, *make_tasks):
        self.hosts.setdefault(kernel_name, []).extend(make_tasks)

    def post(self, kernel_name, *thunks):
        self.posts.setdefault(kernel_name, []).extend(thunks)

    def tasks_for(self, kernel_name):
        return [make() for make in self.hosts.pop(kernel_name, ())]

    def finished(self, kernel_name):
        for thunk in self.posts.pop(kernel_name, ()):
            thunk()


_ACTIVE_SCHEDULE = [None]


def _hosted(body, name, **kw):
    def run(*args):
        sched = _ACTIVE_SCHEDULE[0]
        tasks = sched.tasks_for(name) if sched is not None else []
        out = _call_with_tasks(body, name, tasks, kw, args) if tasks else pl.pallas_call(body, name=name, **kw)(*args)
        if sched is not None:
            sched.finished(name)
        return out

    return run


def _call_with_tasks(body, name, tasks, kw, args):
    grid = tuple(kw.get("grid", ()))
    single = not isinstance(kw["out_shape"], (list, tuple))
    out_shape = [kw["out_shape"]] if single else list(kw["out_shape"])
    out_specs = [kw["out_specs"]] if single else list(kw["out_specs"])
    scratch = list(kw.get("scratch_shapes", ()))
    n_in, n_out, n_scr = len(args), len(out_shape), len(scratch)
    t_in = [a for t in tasks for a in t.ins]
    t_out = [o for t in tasks for o in t.out_shapes]
    t_sem = [s for t in tasks for s in t.sems]
    aliases, at_in, at_out = {}, n_in, n_out
    for t in tasks:
        for i, o in t.aliases.items():
            aliases[at_in + i] = at_out + o
        at_in += len(t.ins)
        at_out += len(t.out_shapes)

    def wrapped(*refs):
        a = n_in
        b = a + len(t_in)
        c = b + n_out
        d = c + len(t_out)
        e = d + n_scr
        ins, tins, outs, touts, scr, tsems = refs[:a], refs[a:b], refs[b:c], refs[c:d], refs[d:e], refs[e:]
        views, i0, o0, s0 = [], 0, 0, 0
        for t in tasks:
            views.append((tins[i0:i0 + len(t.ins)], touts[o0:o0 + len(t.out_shapes)], tsems[s0:s0 + len(t.sems)]))
            i0, o0, s0 = i0 + len(t.ins), o0 + len(t.out_shapes), s0 + len(t.sems)

        def start_all():
            for t, v in zip(tasks, views):
                t.start(*v)

        def wait_all():
            for t, v in zip(tasks, views):
                t.wait(*v)

        if grid:
            first = functools.reduce(jnp.logical_and, [pl.program_id(i) == 0 for i in range(len(grid))])
            last = functools.reduce(jnp.logical_and, [pl.program_id(i) == grid[i] - 1 for i in range(len(grid))])
            pl.when(first)(start_all)
            body(*ins, *outs, *scr)
            pl.when(last)(wait_all)
        else:
            start_all()
            body(*ins, *outs, *scr)
            wait_all()

    res = pl.pallas_call(
        wrapped, name=name, grid=grid,
        in_specs=list(kw["in_specs"]) + [ANY] * len(t_in), out_specs=out_specs + [ANY] * len(t_out),
        out_shape=out_shape + t_out, scratch_shapes=scratch + t_sem, input_output_aliases=aliases,
        compiler_params=pltpu.CompilerParams(dimension_semantics=("arbitrary",) * len(grid), vmem_limit_bytes=VMEM_LIMIT),
    )(*args, *t_in)
    res = list(res)
    own, rest = res[:n_out], res[n_out:]
    for t in tasks:
        t.done(rest[:len(t.out_shapes)])
        rest = rest[len(t.out_shapes):]
    return own[0] if single else own


def _comm_only(tasks, name):
    _call_with_tasks(lambda: None, name, tasks, dict(grid=(), in_specs=[], out_specs=[], out_shape=[]), ())


def _rms_fwd(h, gain, name):
    s, d = h.shape
    tm = _tile(s, 512)

    def body(h_ref, g_ref, u_ref):
        x = h_ref[...]
        u_ref[...] = (x * _rms_stats(x) * g_ref[...]).astype(u_ref.dtype)

    return _hosted(
        body, name=name, grid=(s // tm,),
        in_specs=[pl.BlockSpec((tm, d), lambda m: (m, 0)), pl.BlockSpec((1, d), lambda m: (0, 0))],
        out_specs=pl.BlockSpec((tm, d), lambda m: (m, 0)),
        out_shape=jax.ShapeDtypeStruct((s, d), BF16),
        compiler_params=_params("parallel"),
    )(h, gain)


def _mm_col(a, w, bias, name):
    s, k = a.shape
    nsh, _, ns = w.shape
    tm = _tile(s, 512)
    has_bias = bias is not None

    def body(a_ref, w_ref, *rest):
        o_ref = rest[-1]
        acc = _dot(a_ref[...], w_ref[...])
        if has_bias:
            acc = acc + rest[0][...]
        o_ref[...] = acc.astype(o_ref.dtype)

    in_specs = [pl.BlockSpec((tm, k), lambda j, m: (m, 0)), pl.BlockSpec((None, k, ns), lambda j, m: (j, 0, 0))]
    args = [a, w]
    if has_bias:
        in_specs.append(pl.BlockSpec((1, ns), lambda j, m: (0, j)))
        args.append(bias)
    return _hosted(
        body, name=name, grid=(nsh, s // tm), in_specs=in_specs,
        out_specs=pl.BlockSpec((tm, ns), lambda j, m: (m, j)),
        out_shape=jax.ShapeDtypeStruct((s, nsh * ns), BF16),
        compiler_params=_params("parallel", "parallel"),
    )(*args)


def _mm_row(a, w, res, bias, name):
    s = a.shape[0]
    k, n = w.shape
    tm = _tile(s, 512)
    has_bias = bias is not None

    def body(a_ref, w_ref, res_ref, *rest):
        o_ref = rest[-1]
        y = res_ref[...] + _dot(a_ref[...], w_ref[...])
        if has_bias:
            y = y + rest[0][...]
        o_ref[...] = y

    in_specs = [pl.BlockSpec((tm, k), lambda m: (m, 0)), pl.BlockSpec((k, n), lambda m: (0, 0)),
                pl.BlockSpec((tm, n), lambda m: (m, 0))]
    args = [a, w, res]
    if has_bias:
        in_specs.append(pl.BlockSpec((1, n), lambda m: (0, 0)))
        args.append(bias)
    return _hosted(
        body, name=name, grid=(s // tm,), in_specs=in_specs,
        out_specs=pl.BlockSpec((tm, n), lambda m: (m, 0)),
        out_shape=jax.ShapeDtypeStruct((s, n), F32),
        compiler_params=_params("parallel"),
    )(*args)


def _mm_nt_row(dy, w, name):
    s, n = dy.shape
    k = w.shape[0]
    tm = _tile(s, 512)

    def body(dy_ref, w_ref, o_ref):
        o_ref[...] = _dot_nt(dy_ref[...].astype(BF16), w_ref[...])

    return _hosted(
        body, name=name, grid=(s // tm,),
        in_specs=[pl.BlockSpec((tm, n), lambda m: (m, 0)), pl.BlockSpec((k, n), lambda m: (0, 0))],
        out_specs=pl.BlockSpec((tm, k), lambda m: (m, 0)),
        out_shape=jax.ShapeDtypeStruct((s, k), F32),
        compiler_params=_params("parallel"),
    )(dy, w)


def _ffn_up(u, w, name):
    s, d = u.shape
    _, _, ns = w.shape
    tm = _tile(s, 512)

    def body(u_ref, wg_ref, wu_ref, act_ref, s1_ref, q1_ref):
        x = u_ref[...]
        g = _dot(x, wg_ref[...])
        up = _dot(x, wu_ref[...])
        sg = _sigmoid(g)
        s1 = g * sg
        act_ref[...] = (s1 * up).astype(act_ref.dtype)
        s1_ref[...] = s1.astype(s1_ref.dtype)
        q1_ref[...] = (up * sg * (1.0 + g * (1.0 - sg))).astype(q1_ref.dtype)

    out = pl.BlockSpec((tm, ns), lambda j, m: (m, j))
    return _hosted(
        body, name=name, grid=(2, s // tm),
        in_specs=[pl.BlockSpec((tm, d), lambda j, m: (m, 0)), pl.BlockSpec((None, d, ns), lambda j, m: (j, 0, 0)),
                  pl.BlockSpec((None, d, ns), lambda j, m: (j + 2, 0, 0))],
        out_specs=[out, out, out],
        out_shape=[jax.ShapeDtypeStruct((s, 2 * ns), BF16)] * 3,
        compiler_params=_params("parallel", "parallel"),
    )(u, w, w)


def _ffn_down_bwd(dh, w, s1, q1, name):
    s, d = dh.shape
    f = w.shape[0]
    tm = _tile(s, 256)

    def body(dh_ref, w_ref, s1_ref, q1_ref, o_ref):
        da = _dot_nt(dh_ref[...].astype(BF16), w_ref[...])
        o_ref[:, :f] = (da * q1_ref[...].astype(F32)).astype(o_ref.dtype)
        o_ref[:, f:] = (da * s1_ref[...].astype(F32)).astype(o_ref.dtype)

    return _hosted(
        body, name=name, grid=(s // tm,),
        in_specs=[pl.BlockSpec((tm, d), lambda m: (m, 0)), pl.BlockSpec((f, d), lambda m: (0, 0)),
                  pl.BlockSpec((tm, f), lambda m: (m, 0)), pl.BlockSpec((tm, f), lambda m: (m, 0))],
        out_specs=pl.BlockSpec((tm, 2 * f), lambda m: (m, 0)),
        out_shape=jax.ShapeDtypeStruct((s, 2 * f), BF16),
        compiler_params=_params("parallel"),
    )(dh, w, s1, q1)


def _mm_nt_col_rms_bwd(dy, w, h, gain, dh, name):
    s = dy.shape[0]
    nsh, k, ns = w.shape
    tm = _tile(s, 256)

    def body(dy_ref, w_ref, h_ref, g_ref, dh_ref, o_ref, dg_ref):
        du = _dot_nt(dy_ref[:, :ns], w_ref[0])
        for j in range(1, nsh):
            du = du + _dot_nt(dy_ref[:, j * ns:(j + 1) * ns], w_ref[j])
        dx, dg = _rms_bwd(du, h_ref[...], g_ref[...])
        o_ref[...] = dh_ref[...] + dx
        _accumulate(dg_ref, dg, pl.program_id(0) == 0)

    return _hosted(
        body, name=name, grid=(s // tm,),
        in_specs=[pl.BlockSpec((tm, nsh * ns), lambda m: (m, 0)), pl.BlockSpec((nsh, k, ns), lambda m: (0, 0, 0)),
                  pl.BlockSpec((tm, k), lambda m: (m, 0)), pl.BlockSpec((1, k), lambda m: (0, 0)),
                  pl.BlockSpec((tm, k), lambda m: (m, 0))],
        out_specs=[pl.BlockSpec((tm, k), lambda m: (m, 0)), pl.BlockSpec((1, k), lambda m: (0, 0))],
        out_shape=[jax.ShapeDtypeStruct((s, k), F32), jax.ShapeDtypeStruct((1, k), F32)],
        compiler_params=_params("arbitrary"),
    )(dy, w, h, gain, dh)


def _mm_tn(a, dy, nsh, name):
    s, k = a.shape
    ns = dy.shape[1] // nsh
    tm = _tile(s, 1024)
    tk = _tile(k, 1408, LANES)
    nk, nm = k // tk, s // tm

    def body(a_ref, dy_ref, o_ref, acc_ref):
        m = pl.program_id(2)
        part = _dot_tn(a_ref[...], dy_ref[...].astype(BF16))

        @pl.when(m == 0)
        def _():
            acc_ref[...] = part

        @pl.when(m > 0)
        def _():
            acc_ref[...] += part

        @pl.when(m == nm - 1)
        def _():
            o_ref[...] = acc_ref[...].astype(o_ref.dtype)

    return _hosted(
        body, name=name, grid=(nsh, nk, nm),
        in_specs=[pl.BlockSpec((tm, tk), lambda j, kk, m: (m, kk)), pl.BlockSpec((tm, ns), lambda j, kk, m: (m, j))],
        out_specs=pl.BlockSpec((None, tk, ns), lambda j, kk, m: (j, kk, 0)),
        out_shape=jax.ShapeDtypeStruct((nsh, k, ns), BF16),
        scratch_shapes=[pltpu.VMEM((tk, ns), F32)],
        compiler_params=_params("parallel", "parallel", "arbitrary"),
    )(a, dy)


def _main_spec(tm, w):
    return pl.BlockSpec((tm, w), lambda m: (m, 0))


def _before_spec(tm, hb, w):
    return pl.BlockSpec((hb, w), lambda m: (jnp.maximum(m * (tm // hb) - 1, 0), 0))


def _after_spec(tm, hb, w, s):
    return pl.BlockSpec((hb, w), lambda m: (jnp.minimum((m + 1) * (tm // hb), s // hb - 1), 0))


def _row_spec(w, rows=1):
    return pl.BlockSpec((rows, w), lambda m: (0, 0))


CHUNK_LANES = 4 * LANES
CHUNK_ROWS = 32


def _build_shifts(ext8_ref):
    n = ext8_ref.shape[1] - 8
    for r in range(1, 8):
        ext8_ref[r, pl.ds(0, n), :] = ext8_ref[0, pl.ds(r, n), :]


def _shifted(ext8_ref, shift, r0, rows, cols):
    return ext8_ref[shift % 8, pl.ds(pl.multiple_of(shift - shift % 8 + r0, 8), rows), cols]


def _lane_chunk(i):
    return pl.ds(pl.multiple_of(i * CHUNK_LANES, CHUNK_LANES), CHUNK_LANES)


def _sum_terms(terms, ways=4):
    accs = []
    for i, t in enumerate(terms):
        if i < ways:
            accs.append(t)
        else:
            accs[i % ways] = accs[i % ways] + t
    while len(accs) > 1:
        accs = [accs[i] + accs[i + 1] if i + 1 < len(accs) else accs[i] for i in range(0, len(accs), 2)]
    return accs[0]


def _accumulate(ref, val, first):
    @pl.when(first)
    def _():
        ref[...] = val

    @pl.when(jnp.logical_not(first))
    def _():
        ref[...] += val


def _sconv_taps(zext_ref, cw_ref, tm, base):
    out = cw_ref[2:3, :] * zext_ref[pl.ds(base, tm), :]
    out = out + cw_ref[1:2, :] * zext_ref[pl.ds(base - 1, tm), :]
    return out + cw_ref[0:1, :] * zext_ref[pl.ds(base - 2, tm), :]


def _sconv_fill_z(zext_ref, main_ref, before_ref, d, m):
    hb = SCONV_HALO
    zb = before_ref[:, d:2 * d].astype(F32) * before_ref[:, 2 * d:].astype(F32)
    zext_ref[pl.ds(0, hb), :] = jnp.where(m > 0, zb, 0.0)
    zext_ref[pl.ds(hb, main_ref.shape[0]), :] = main_ref[:, d:2 * d].astype(F32) * main_ref[:, 2 * d:].astype(F32)


def _sconv_fwd(bcv, cw, name):
    s, d3 = bcv.shape
    d = d3 // 3
    tm = _tile(s, 512, SCONV_HALO)

    def body(main_ref, before_ref, cw_ref, p_ref, zext_ref):
        m = pl.program_id(0)
        _sconv_fill_z(zext_ref, main_ref, before_ref, d, m)
        zc = _sconv_taps(zext_ref, cw_ref, tm, SCONV_HALO)
        p_ref[...] = (main_ref[:, :d].astype(F32) * zc).astype(p_ref.dtype)

    return _hosted(
        body, name=name, grid=(s // tm,),
        in_specs=[_main_spec(tm, d3), _before_spec(tm, SCONV_HALO, d3), _row_spec(d, SHORT_CONV_W)],
        out_specs=_main_spec(tm, d),
        out_shape=jax.ShapeDtypeStruct((s, d), BF16),
        scratch_shapes=[pltpu.VMEM((tm + SCONV_HALO, d), F32)],
        compiler_params=_params("parallel"),
    )(bcv, bcv, cw)


def _sconv_bwd(dp, bcv, cw, name):
    s, d3 = bcv.shape
    d = d3 // 3
    tm = _tile(s, 512, SCONV_HALO)
    nm = s // tm
    ha = 8

    def body(dp_ref, dpa_ref, main_ref, before_ref, after_ref, cw_ref, o_ref, dcw_ref, zext_ref, dext_ref):
        m = pl.program_id(0)
        _sconv_fill_z(zext_ref, main_ref, before_ref, d, m)
        zc = _sconv_taps(zext_ref, cw_ref, tm, SCONV_HALO)
        dp_t = dp_ref[...]
        o_ref[:, :d] = (dp_t * zc).astype(o_ref.dtype)
        dzc = dp_t * main_ref[:, :d].astype(F32)
        dext_ref[pl.ds(0, tm), :] = dzc
        dza = dpa_ref[...] * after_ref[:, :d].astype(F32)[0:ha]
        dext_ref[pl.ds(tm, ha), :] = jnp.where(m < nm - 1, dza, 0.0)
        dz = cw_ref[2:3, :] * dzc
        dz = dz + cw_ref[1:2, :] * dext_ref[pl.ds(1, tm), :]
        dz = dz + cw_ref[0:1, :] * dext_ref[pl.ds(2, tm), :]
        o_ref[:, d:2 * d] = (dz * main_ref[:, 2 * d:].astype(F32)).astype(o_ref.dtype)
        o_ref[:, 2 * d:] = (dz * main_ref[:, d:2 * d].astype(F32)).astype(o_ref.dtype)

        @pl.when(m == 0)
        def _():
            dcw_ref[...] = jnp.zeros_like(dcw_ref)

        for kk in range(SHORT_CONV_W):
            zs = zext_ref[pl.ds(SCONV_HALO - 2 + kk, tm), :]
            dcw_ref[kk:kk + 1, :] += _colsum(dzc * zs)

    return _hosted(
        body, name=name, grid=(nm,),
        in_specs=[_main_spec(tm, d), _after_spec(tm, ha, d, s), _main_spec(tm, d3), _before_spec(tm, SCONV_HALO, d3),
                  _after_spec(tm, SCONV_HALO, d3, s), _row_spec(d, SHORT_CONV_W)],
        out_specs=[_main_spec(tm, d3), _row_spec(d, 8)],
        out_shape=[jax.ShapeDtypeStruct((s, d3), BF16), jax.ShapeDtypeStruct((8, d), F32)],
        scratch_shapes=[pltpu.VMEM((tm + SCONV_HALO, d), F32), pltpu.VMEM((tm + ha, d), F32)],
        compiler_params=_params("arbitrary"),
    )(dp, dp, bcv, bcv, bcv, cw)


def _pool_counts(t0, tm, w):
    t = t0 + lax.broadcasted_iota(jnp.int32, (tm, 1), 0)
    return jnp.minimum(t + 1, w).astype(F32)


def _pool_fwd(h, gain, wg, scale, name):
    s, d = h.shape
    ng, cg, _ = wg.shape
    tm = _tile(s, 512, POOL_HALO)

    def body(h_ref, hb_ref, g_ref, wg_ref, sc_ref, o_ref, mx_ref, uext_ref):
        m = pl.program_id(0)
        x = h_ref[...]
        gain_row = g_ref[...]
        xb = hb_ref[...]
        uext_ref[pl.ds(0, POOL_HALO), :] = jnp.where(m > 0, xb * _rms_stats(xb) * gain_row, 0.0)
        uext_ref[pl.ds(POOL_HALO, tm), :] = x * _rms_stats(x) * gain_row
        for gi, win in enumerate(POOL_WINDOWS):
            cols = pl.ds(gi * cg, cg)
            u_g = uext_ref[pl.ds(POOL_HALO, tm), cols]
            acc = u_g
            for i in range(1, win):
                acc = acc + uext_ref[pl.ds(POOL_HALO - i, tm), cols]
            mixed = (acc / _pool_counts(m * tm, tm, win) - u_g).astype(BF16)
            mx_ref[:, cols] = mixed
            o_ref[:, cols] = x[:, gi * cg:(gi + 1) * cg] + _dot(mixed, wg_ref[gi]) * sc_ref[:, cols]

    return _hosted(
        body, name=name, grid=(s // tm,),
        in_specs=[_main_spec(tm, d), _before_spec(tm, POOL_HALO, d), _row_spec(d),
                  pl.BlockSpec((ng, cg, cg), lambda m: (0, 0, 0)), _row_spec(d)],
        out_specs=[_main_spec(tm, d), _main_spec(tm, d)],
        out_shape=[jax.ShapeDtypeStruct((s, d), F32), jax.ShapeDtypeStruct((s, d), BF16)],
        scratch_shapes=[pltpu.VMEM((tm + POOL_HALO, d), F32)],
        compiler_params=_params("parallel"),
    )(h, h, gain, wg, scale)


def _pool_bwd_mm(dh, mixed, wg, scale, name):
    s, d = dh.shape
    ng, cg, _ = wg.shape
    tm = _tile(s, 512)

    def body(dh_ref, mx_ref, wg_ref, sc_ref, dmx_ref, dwg_ref, dsc_ref):
        first = pl.program_id(0) == 0
        for gi in range(ng):
            cols = pl.ds(gi * cg, cg)
            dh_g = dh_ref[:, cols]
            mixed = mx_ref[:, cols]
            w_g = wg_ref[gi]
            dy = (dh_g * sc_ref[:, cols]).astype(BF16)
            dmx_ref[:, cols] = _dot_nt(dy, w_g)
            _accumulate(dsc_ref.at[:, cols], _colsum(dh_g * _dot(mixed, w_g)), first)
            _accumulate(dwg_ref.at[gi], _dot_tn(mixed, dy), first)

    return _hosted(
        body, name=name, grid=(s // tm,),
        in_specs=[_main_spec(tm, d), _main_spec(tm, d), pl.BlockSpec((ng, cg, cg), lambda m: (0, 0, 0)), _row_spec(d)],
        out_specs=[_main_spec(tm, d), pl.BlockSpec((ng, cg, cg), lambda m: (0, 0, 0)), _row_spec(d)],
        out_shape=[jax.ShapeDtypeStruct((s, d), F32), jax.ShapeDtypeStruct((ng, cg, cg), F32),
                   jax.ShapeDtypeStruct((1, d), F32)],
        compiler_params=_params("arbitrary"),
    )(dh, mixed, wg, scale)


def _pool_bwd_rms(dmixed, h, gain, dh, name):
    s, d = h.shape
    cg = d // len(POOL_WINDOWS)
    tm = _tile(s, 512, POOL_HALO)
    nm = s // tm

    def body(dmx_ref, dmxa_ref, h_ref, g_ref, dh_ref, o_ref, dg_ref, eext_ref, du_ref):
        m = pl.program_id(0)
        for gi, win in enumerate(POOL_WINDOWS):
            cols = pl.ds(gi * cg, cg)
            dmx = dmx_ref[:, cols]
            eext_ref[pl.ds(0, tm), cols] = dmx / _pool_counts(m * tm, tm, win)
            ea = dmxa_ref[:, cols] / _pool_counts((m + 1) * tm, POOL_HALO, win)
            eext_ref[pl.ds(tm, POOL_HALO), cols] = jnp.where(m < nm - 1, ea, 0.0)
            acc = -dmx
            for i in range(win):
                acc = acc + eext_ref[pl.ds(i, tm), cols]
            du_ref[:, cols] = acc
        dx, dg = _rms_bwd(du_ref[...], h_ref[...], g_ref[...])
        o_ref[...] = dh_ref[...] + dx
        _accumulate(dg_ref, dg, m == 0)

    return _hosted(
        body, name=name, grid=(nm,),
        in_specs=[_main_spec(tm, d), _after_spec(tm, POOL_HALO, d, s), _main_spec(tm, d), _row_spec(d), _main_spec(tm, d)],
        out_specs=[_main_spec(tm, d), _row_spec(d)],
        out_shape=[jax.ShapeDtypeStruct((s, d), F32), jax.ShapeDtypeStruct((1, d), F32)],
        scratch_shapes=[pltpu.VMEM((tm + POOL_HALO, d), F32), pltpu.VMEM((tm, d), F32)],
        compiler_params=_params("arbitrary"),
    )(dmixed, dmixed, h, gain, dh)


def _conf_fill_h(hext_ref, main_ref, before_ref, d, m):
    hb = before_ref[:, :d].astype(F32) * _sigmoid(before_ref[:, d:].astype(F32))
    hext_ref[pl.ds(0, CONF_HALO), :] = jnp.where(m > 0, hb, 0.0)
    hext_ref[pl.ds(CONF_HALO, main_ref.shape[0]), :] = main_ref[:, :d].astype(F32) * _sigmoid(main_ref[:, d:].astype(F32))


def _layernorm_parts(hc, g, b):
    mu = jnp.mean(hc, axis=-1, keepdims=True)
    xc = hc - mu
    rs = lax.rsqrt(jnp.mean(xc * xc, axis=-1, keepdims=True) + LN_EPS)
    xhat = xc * rs
    return xhat, rs, xhat * g + b


def _conf_mid_fwd(ag, dw, b_dw, ln_g, ln_b, name):
    s, d2 = ag.shape
    d = d2 // 2
    tm = _tile(s, 256, CONF_HALO)
    base = CONF_HALO - (CONF_CONV_W - 1)

    def body(main_ref, before_ref, dw_ref, bdw_ref, g_ref, b_ref, s_ref, hc_ref, hext_ref):
        m = pl.program_id(0)
        _conf_fill_h(hext_ref.at[0], main_ref, before_ref, d, m)
        _build_shifts(hext_ref)
        row_chunks = tm // CHUNK_ROWS

        def conv_chunk(i, carry):
            cols = _lane_chunk(i // row_chunks)
            r0 = pl.multiple_of((i % row_chunks) * CHUNK_ROWS, CHUNK_ROWS)
            taps = (dw_ref[kk:kk + 1, cols] * _shifted(hext_ref, base + kk, r0, CHUNK_ROWS, cols) for kk in range(CONF_CONV_W))
            hc_ref[pl.ds(r0, CHUNK_ROWS), cols] = bdw_ref[:, cols] + _sum_terms(taps, ways=1)
            return carry

        lax.fori_loop(0, row_chunks * (d // CHUNK_LANES), conv_chunk, 0)
        _, _, l = _layernorm_parts(hc_ref[...], g_ref[...], b_ref[...])
        s_ref[...] = (l * _sigmoid(l)).astype(s_ref.dtype)

    return _hosted(
        body, name=name, grid=(s // tm,),
        in_specs=[_main_spec(tm, d2), _before_spec(tm, CONF_HALO, d2), _row_spec(d, CONF_CONV_W), _row_spec(d),
                  _row_spec(d), _row_spec(d)],
        out_specs=[_main_spec(tm, d), _main_spec(tm, d)],
        out_shape=[jax.ShapeDtypeStruct((s, d), BF16), jax.ShapeDtypeStruct((s, d), F32)],
        scratch_shapes=[pltpu.VMEM((8, tm + CONF_HALO, d), F32)],
        compiler_params=_params("parallel"),
    )(ag, ag, dw, b_dw, ln_g, ln_b)


def _conf_out_bwd(dh, w, hc, ln_g, ln_b, name):
    s, d = dh.shape
    tm = _tile(s, 256)

    def body(dh_ref, w_ref, hc_ref, g_ref, b_ref, o_ref, dg_ref, db_ref, dbo_ref):
        first = pl.program_id(0) == 0
        dh_t = dh_ref[...]
        ds = _dot_nt(dh_t.astype(BF16), w_ref[...])
        xhat, rs, l = _layernorm_parts(hc_ref[...], g_ref[...], b_ref[...])
        sg = _sigmoid(l)
        dl = ds * sg * (1.0 + l * (1.0 - sg))
        dxh = dl * g_ref[...]
        o_ref[...] = rs * (dxh - jnp.mean(dxh, axis=-1, keepdims=True)
                           - xhat * jnp.mean(dxh * xhat, axis=-1, keepdims=True))
        _accumulate(dg_ref, _colsum(dl * xhat), first)
        _accumulate(db_ref, _colsum(dl), first)
        _accumulate(dbo_ref, _colsum(dh_t), first)

    return _hosted(
        body, name=name, grid=(s // tm,),
        in_specs=[_main_spec(tm, d), pl.BlockSpec((d, d), lambda m: (0, 0)), _main_spec(tm, d), _row_spec(d), _row_spec(d)],
        out_specs=[_main_spec(tm, d), _row_spec(d), _row_spec(d), _row_spec(d)],
        out_shape=[jax.ShapeDtypeStruct((s, d), F32)] + [jax.ShapeDtypeStruct((1, d), F32)] * 3,
        compiler_params=_params("arbitrary"),
    )(dh, w, hc, ln_g, ln_b)


def _conf_mid_bwd(dhc, ag, dw, name):
    s, d2 = ag.shape
    d = d2 // 2
    tm = _tile(s, 256, CONF_HALO)
    nm = s // tm
    kw = CONF_CONV_W
    base = CONF_HALO - (kw - 1)

    def body(dhc_ref, dhca_ref, main_ref, before_ref, dw_ref, o_ref, ddw_ref, dbdw_ref, dbpw_ref, hext_ref, dext_ref):
        m = pl.program_id(0)
        first = m == 0
        _conf_fill_h(hext_ref.at[0], main_ref, before_ref, d, m)
        _build_shifts(hext_ref)
        dext_ref[0, pl.ds(0, tm), :] = dhc_ref[...]
        dext_ref[0, pl.ds(tm, CONF_HALO), :] = jnp.where(m < nm - 1, dhca_ref[...], 0.0)
        _build_shifts(dext_ref)

        @pl.when(first)
        def _():
            ddw_ref[...] = jnp.zeros_like(ddw_ref)
            dbdw_ref[...] = jnp.zeros_like(dbdw_ref)
            dbpw_ref[...] = jnp.zeros_like(dbpw_ref)

        zero = jnp.zeros((8, CHUNK_LANES), F32)
        tap_group = 8

        def fold(x):
            return functools.reduce(lambda p, q: p + q, [x[i:i + 8] for i in range(0, CHUNK_ROWS, 8)])

        def lane_chunk(ci, carry):
            cols = _lane_chunk(ci)
            gate_cols = pl.ds(pl.multiple_of(d + ci * CHUNK_LANES, CHUNK_LANES), CHUNK_LANES)

            def through_conv(ri, sums):
                r0 = pl.multiple_of(ri * CHUNK_ROWS, CHUNK_ROWS)
                rows = pl.ds(r0, CHUNK_ROWS)
                dhh = _sum_terms((dw_ref[kk:kk + 1, cols] * _shifted(dext_ref, kw - 1 - kk, r0, CHUNK_ROWS, cols)
                                  for kk in range(kw)), ways=1)
                a = main_ref[rows, cols].astype(F32)
                sg = _sigmoid(main_ref[rows, gate_cols].astype(F32))
                da = dhh * sg
                dgate = dhh * a * sg * (1.0 - sg)
                o_ref[rows, cols] = da.astype(o_ref.dtype)
                o_ref[rows, gate_cols] = dgate.astype(o_ref.dtype)
                return sums[0] + fold(da), sums[1] + fold(dgate), sums[2] + fold(dext_ref[0, rows, cols])

            sum_da, sum_dgate, sum_dhc = lax.fori_loop(0, tm // CHUNK_ROWS, through_conv, (zero, zero, zero))
            dbdw_ref[:, cols] += _colsum(sum_dhc)
            dbpw_ref[:, cols] += _colsum(sum_da)
            dbpw_ref[:, gate_cols] += _colsum(sum_dgate)

            for k0 in range(0, kw, tap_group):
                group = range(k0, min(k0 + tap_group, kw))

                def tap_gradients(ri, accs, group=group):
                    for sub in range(0, CHUNK_ROWS, 8):
                        r0 = pl.multiple_of(ri * CHUNK_ROWS + sub, 8)
                        dhc_c = dext_ref[0, pl.ds(r0, 8), cols]
                        accs = tuple(acc + dhc_c * _shifted(hext_ref, base + kk, r0, 8, cols) for kk, acc in zip(group, accs))
                    return accs

                accs = lax.fori_loop(0, tm // CHUNK_ROWS, tap_gradients, (zero,) * len(group))
                for kk, acc in zip(group, accs):
                    ddw_ref[kk:kk + 1, cols] += _colsum(acc)
            return carry

        lax.fori_loop(0, d // CHUNK_LANES, lane_chunk, 0)

    return _hosted(
        body, name=name, grid=(nm,),
        in_specs=[_main_spec(tm, d), _after_spec(tm, CONF_HALO, d, s), _main_spec(tm, d2), _before_spec(tm, CONF_HALO, d2),
                  _row_spec(d, kw)],
        out_specs=[_main_spec(tm, d2), _row_spec(d, 32), _row_spec(d), _row_spec(d2)],
        out_shape=[jax.ShapeDtypeStruct((s, d2), BF16), jax.ShapeDtypeStruct((32, d), F32),
                   jax.ShapeDtypeStruct((1, d), F32), jax.ShapeDtypeStruct((1, d2), F32)],
        scratch_shapes=[pltpu.VMEM((8, tm + CONF_HALO, d), F32), pltpu.VMEM((8, tm + CONF_HALO, d), F32)],
        compiler_params=_params("arbitrary"),
    )(dhc, dhc, ag, ag, dw)


def _loss_head(h, gain, target, name):
    s, d = h.shape
    tm = _tile(s, 512)

    def body(h_ref, g_ref, t_ref, loss_ref, dh_ref, dg_ref):
        first = pl.program_id(0) == 0
        x = h_ref[...]
        err = x * _rms_stats(x) * g_ref[...] - t_ref[...]
        part = 0.5 * jnp.sum(jnp.mean(err * err, axis=-1, keepdims=True), axis=0, keepdims=True)
        dx, dg = _rms_bwd(err * (1.0 / d), x, g_ref[...])
        dh_ref[...] = dx
        _accumulate(loss_ref, part, first)
        _accumulate(dg_ref, dg, first)

    return _hosted(
        body, name=name, grid=(s // tm,),
        in_specs=[_main_spec(tm, d), _row_spec(d), _main_spec(tm, d)],
        out_specs=[pl.BlockSpec((1, 1), lambda m: (0, 0)), _main_spec(tm, d), _row_spec(d)],
        out_shape=[jax.ShapeDtypeStruct((1, 1), F32), jax.ShapeDtypeStruct((s, d), F32), jax.ShapeDtypeStruct((1, d), F32)],
        compiler_params=_params("arbitrary"),
    )(h, gain, target)


def _ffn_fwd(h, wts, i):
    u = _rms_fwd(h, wts[f"ln2_{i}"], f"ffn{i}_rms")
    act, s1, q1 = _ffn_up(u, wts[f"ffn{i}_w_gu"], f"ffn{i}_up")
    h_new = _mm_row(act, wts[f"ffn{i}_w_down"], h, None, f"ffn{i}_down")
    return h_new, (h, u, act, s1, q1)


def _ffn_bwd(dh, saved, wts, i, g):
    h, u, act, s1, q1 = saved
    dgu = _ffn_down_bwd(dh, wts[f"ffn{i}_w_down"], s1, q1, f"ffn{i}_down_bwd")
    g[f"ffn{i}_w_down"] = _mm_tn(act, dh, 1, f"ffn{i}_dw_down")
    g[f"ffn{i}_w_gu"] = _mm_tn(u, dgu, N_CHIPS, f"ffn{i}_dw_gu")
    dh_new, g[f"ln2_{i}"] = _mm_nt_col_rms_bwd(dgu, wts[f"ffn{i}_w_gu"], h, wts[f"ln2_{i}"], dh, f"ffn{i}_up_bwd")
    return dh_new


def _device_step(x, target, wts, g=None):
    g = {} if g is None else g
    saved = {}
    h = x

    def short_conv_fwd(h, i):
        u = _rms_fwd(h, wts[f"ln1_{i}"], f"a{i}_rms")
        bcv = _mm_col(u, wts[f"a{i}_w_in"], None, f"a{i}_in")
        p = _sconv_fwd(bcv, wts[f"a{i}_conv"], f"a{i}_conv")
        return _mm_row(p, wts[f"a{i}_w_out"], h, None, f"a{i}_out"), (h, u, bcv, p)

    def short_conv_bwd(dh, sv, i):
        h, u, bcv, p = sv
        dp = _mm_nt_row(dh, wts[f"a{i}_w_out"], f"a{i}_out_bwd")
        dbcv, dcw = _sconv_bwd(dp, bcv, wts[f"a{i}_conv"], f"a{i}_conv_bwd")
        g[f"a{i}_conv"] = dcw[:SHORT_CONV_W]
        g[f"a{i}_w_in"] = _mm_tn(u, dbcv, N_CHIPS, f"a{i}_dw_in")
        g[f"a{i}_w_out"] = _mm_tn(p, dh, 1, f"a{i}_dw_out")
        dh, g[f"ln1_{i}"] = _mm_nt_col_rms_bwd(dbcv, wts[f"a{i}_w_in"], h, wts[f"ln1_{i}"], dh, f"a{i}_in_bwd")
        return dh

    h, saved["a0"] = short_conv_fwd(h, 0)
    h, saved["f0"] = _ffn_fwd(h, wts, 0)

    h_in = h
    h, mixed = _pool_fwd(h, wts["ln1_1"], wts["b1_w_grp"], wts["b1_scale"], "b1_fwd")
    saved["b1"] = (h_in, mixed)
    h, saved["f1"] = _ffn_fwd(h, wts, 1)

    h_in = h
    u = _rms_fwd(h, wts["ln1_2"], "c2_rms")
    ag = _mm_col(u, wts["c2_w_pw1"], wts["c2_b_pw1"], "c2_pw1")
    sw, hc = _conf_mid_fwd(ag, wts["c2_dw"], wts["c2_b_dw"], wts["c2_ln_g"], wts["c2_ln_b"], "c2_mid")
    h = _mm_row(sw, wts["c2_w_pw2"], h, wts["c2_b_pw2"], "c2_pw2")
    saved["c2"] = (h_in, u, ag, sw, hc)
    h, saved["f2"] = _ffn_fwd(h, wts, 2)

    h, saved["a3"] = short_conv_fwd(h, 3)
    h, saved["f3"] = _ffn_fwd(h, wts, 3)

    loss, dh, g["ln_f"] = _loss_head(h, wts["ln_f"], target, "loss_head")

    def ffn_bwd(dh, i):
        return _ffn_bwd(dh, saved[f"f{i}"], wts, i, g)

    dh = ffn_bwd(dh, 3)
    dh = short_conv_bwd(dh, saved["a3"], 3)

    dh = ffn_bwd(dh, 2)
    h_in, u, ag, sw, hc = saved["c2"]
    dhc, g["c2_ln_g"], g["c2_ln_b"], g["c2_b_pw2"] = _conf_out_bwd(
        dh, wts["c2_w_pw2"], hc, wts["c2_ln_g"], wts["c2_ln_b"], "c2_pw2_bwd")
    g["c2_w_pw2"] = _mm_tn(sw, dh, 1, "c2_dw_pw2")
    dag, ddw, g["c2_b_dw"], g["c2_b_pw1"] = _conf_mid_bwd(dhc, ag, wts["c2_dw"], "c2_mid_bwd")
    g["c2_dw"] = ddw[:CONF_CONV_W]
    g["c2_w_pw1"] = _mm_tn(u, dag, N_CHIPS, "c2_dw_pw1")
    dh, g["ln1_2"] = _mm_nt_col_rms_bwd(dag, wts["c2_w_pw1"], h_in, wts["ln1_2"], dh, "c2_pw1_bwd")

    dh = ffn_bwd(dh, 1)
    h_in, mixed = saved["b1"]
    dmixed, g["b1_w_grp"], g["b1_scale"] = _pool_bwd_mm(dh, mixed, wts["b1_w_grp"], wts["b1_scale"], "b1_bwd_mm")
    dh, g["ln1_1"] = _pool_bwd_rms(dmixed, h_in, wts["ln1_1"], dh, "b1_bwd_rms")

    dh = ffn_bwd(dh, 0)
    dh = short_conv_bwd(dh, saved["a0"], 0)
    return loss, dh, g


MESH = pl.DeviceIdType.MESH
ANY = pl.BlockSpec(memory_space=pl.ANY)


def _position():
    return lax.axis_index("x"), lax.axis_index("y"), lax.axis_index("c")


def _other_chips(x, y):
    return [(1 - x, y), (x, 1 - y), (1 - x, 1 - y)]


def _remote(src, dst, send_sem, recv_sem, to):
    return pltpu.make_async_remote_copy(src_ref=src, dst_ref=dst, send_sem=send_sem, recv_sem=recv_sem,
                                        device_id=to, device_id_type=MESH)


def _half_rows(ref_rows, c):
    hr = ref_rows // 2
    return pl.ds(pl.multiple_of(c * hr, 16), hr)


def _allgather8(v, name):
    m_per, n = v.shape

    def body(v_ref, out_ref, send_sems, recv_sems, local_sem):
        x, y, c = _position()
        me, sibling = (x, y, c), (x, y, 1 - c)
        chips = _other_chips(x, y)

        def rows(px, py, pc):
            return out_ref.at[pl.ds((4 * px + 2 * py + pc) * m_per, m_per), :]

        def copy(k, block, to, src=None):
            return _remote(rows(*block) if src is None else src, rows(*block), send_sems.at[k], recv_sems.at[k], to)

        mine = pltpu.make_async_copy(v_ref, rows(*me), local_sem)
        mine.start()
        first = [copy(0, me, sibling, src=v_ref)]
        first += [copy(1 + j, me, (*chip, c), src=v_ref) for j, chip in enumerate(chips)]
        for cp in first:
            cp.start()
        passed = [copy(4 + j, (*chip, c), sibling) for j, chip in enumerate(chips)]
        for j, chip in enumerate(chips):
            copy(1 + j, (*chip, c), me).wait_recv()
            passed[j].start()
        copy(0, sibling, me).wait_recv()
        for j, chip in enumerate(chips):
            copy(4 + j, (*chip, 1 - c), me).wait_recv()
        for cp in first + passed:
            cp.wait_send()
        mine.wait()

    return _hosted(
        body, name=name,
        out_shape=jax.ShapeDtypeStruct((N_DEV * m_per, n), v.dtype),
        in_specs=[pl.BlockSpec(memory_space=pltpu.VMEM)],
        out_specs=pl.BlockSpec(memory_space=pltpu.VMEM),
        scratch_shapes=[pltpu.SemaphoreType.DMA((7,)), pltpu.SemaphoreType.DMA((7,)), pltpu.SemaphoreType.DMA],
        compiler_params=pltpu.CompilerParams(vmem_limit_bytes=VMEM_LIMIT),
    )(v)


def _cast_to_slot(ws, idx, name):
    r, cols = ws[0].shape
    assert all(w.shape == (r, cols) for w in ws)
    n = len(ws)
    tr = _tile(r, 256, 16)

    def body(idx_ref, *refs):
        for w_ref, o_ref in zip(refs[:n], refs[n:]):
            o_ref[...] = w_ref[...].astype(o_ref.dtype)

    return _hosted(
        body, name=name,
        grid_spec=pltpu.PrefetchScalarGridSpec(
            num_scalar_prefetch=1, grid=(r // tr,),
            in_specs=[pl.BlockSpec((tr, cols), lambda t, idx_ref: (t, 0))] * n,
            out_specs=[pl.BlockSpec((None, tr, cols), lambda t, idx_ref: (idx_ref[0], t, 0))] * n),
        out_shape=[jax.ShapeDtypeStruct((N_CHIPS, r, cols), BF16)] * n,
        compiler_params=_params("parallel"),
    )(idx, *ws)


def _dma_sems(*shape):
    return [pltpu.SemaphoreType.DMA(shape), pltpu.SemaphoreType.DMA(shape)]


def _same_shapes(arrays):
    return [jax.ShapeDtypeStruct(a.shape, a.dtype) for a in arrays]


def _part_rows(ref_rows, c, part):
    hr = ref_rows // 2
    i, n = part
    size = hr // n
    assert size * n == hr and size % 16 == 0, (ref_rows, part)
    return pl.ds(pl.multiple_of(c * hr + i * size, 16), size)


def _task_gather_ici(bufs, done, part=(0, 1)):
    n = len(bufs)

    def copies(outs, sems, landing):
        x, y, c = _position()
        my_chip = 2 * x + y
        res = []
        for i in range(n):
            rows = _part_rows(bufs[i].shape[1], c, part)
            for r, (px, py) in enumerate(_other_chips(x, y)):
                slot = (2 * px + py) if landing else my_chip
                res.append(_remote(outs[i].at[my_chip, rows, :], outs[i].at[slot, rows, :], sems[0].at[i, r], sems[1].at[i, r],
                                   (px, py, c)))
        return res

    def start(ins, outs, sems):
        for cp in copies(outs, sems, False):
            cp.start()

    def wait(ins, outs, sems):
        for cp in copies(outs, sems, True):
            cp.wait_recv()
            cp.wait_send()

    return _Task(bufs, _same_shapes(bufs), {i: i for i in range(n)}, _dma_sems(n, 3), start, wait, done)


def _task_gather_d2d(bufs, done):
    n = len(bufs)

    def copies(outs, sems, landing):
        x, y, c = _position()
        res = []
        for i in range(n):
            rows = _half_rows(bufs[i].shape[1], (1 - c) if landing else c)
            for r, (px, py) in enumerate(_other_chips(x, y)):
                part = outs[i].at[2 * px + py, rows, :]
                res.append(_remote(part, part, sems[0].at[i, r], sems[1].at[i, r], (x, y, 1 - c)))
        return res

    def start(ins, outs, sems):
        for cp in copies(outs, sems, False):
            cp.start()

    def wait(ins, outs, sems):
        for cp in copies(outs, sems, True):
            cp.wait_recv()
        for cp in copies(outs, sems, False):
            cp.wait_send()

    return _Task(bufs, _same_shapes(bufs), {i: i for i in range(n)}, _dma_sems(n, 3), start, wait, done)


def _task_sibling_halves(grads, done):
    n = len(grads)

    def copies(ins, outs, sems):
        x, y, c = _position()
        return [_remote(ins[i].at[:, _half_rows(grads[i].shape[1], 1 - c), :], outs[i], sems[0].at[i], sems[1].at[i],
                        (x, y, 1 - c)) for i in range(n)]

    def start(ins, outs, sems):
        for cp in copies(ins, outs, sems):
            cp.start()

    def wait(ins, outs, sems):
        for cp in copies(ins, outs, sems):
            cp.wait()

    shapes = [jax.ShapeDtypeStruct((g.shape[0], g.shape[1] // 2, g.shape[2]), g.dtype) for g in grads]
    return _Task(grads, shapes, {}, _dma_sems(n), start, wait, done)


def _task_chip_sums(parts, done, landed=None, part=(0, 1)):
    n = len(parts)
    i_part, n_parts = part
    sizes = [p.shape[1] // n_parts for p in parts]
    assert all(p.shape[1] == size * n_parts and size % 16 == 0 for p, size in zip(parts, sizes)), part
    rows = [pl.ds(i_part * size, size) for size in sizes]

    def copies(ins, outs, sems):
        x, y, c = _position()
        return [_remote(ins[i].at[2 * px + py, rows[i], :], outs[i].at[r, rows[i], :], sems[0].at[i, r], sems[1].at[i, r],
                        (px, py, c))
                for i in range(n) for r, (px, py) in enumerate(_other_chips(x, y))]

    def start(ins, outs, sems):
        for cp in copies(ins, outs, sems):
            cp.start()

    def wait(ins, outs, sems):
        for cp in copies(ins, outs, sems):
            cp.wait()

    shapes = [jax.ShapeDtypeStruct((3,) + p.shape[1:], p.dtype) for p in parts]
    if landed is None:
        return _Task(parts, shapes, {}, _dma_sems(n, 3), start, wait, done)
    return _Task(list(parts) + list(landed), shapes, {n + i: i for i in range(n)}, _dma_sems(n, 3), start, wait, done)


def _task_sibling_parts(owns, landeds, done):
    n = len(owns)

    def copies(ins, outs, sems):
        x, y, c = _position()
        sibling = (x, y, 1 - c)
        res = []
        for i in range(n):
            res.append(_remote(ins[i].at[2 * x + y], outs[i].at[0], sems[0].at[i, 0], sems[1].at[i, 0], sibling))
            res.append(_remote(ins[n + i], outs[i].at[pl.ds(1, 3)], sems[0].at[i, 1], sems[1].at[i, 1], sibling))
        return res

    def start(ins, outs, sems):
        for cp in copies(ins, outs, sems):
            cp.start()

    def wait(ins, outs, sems):
        for cp in copies(ins, outs, sems):
            cp.wait()

    return _Task(list(owns) + list(landeds), _same_shapes(owns), {}, _dma_sems(n, 2), start, wait, done)


def _add_halves(grad, sib, c, name):
    nsh, r, cols = grad.shape
    hr = r // 2
    tr = _tile(hr, 512, 16)
    nt = hr // tr

    def body(c_ref, g_ref, s_ref, o_ref):
        o_ref[...] = (g_ref[...].astype(F32) + s_ref[...].astype(F32)).astype(o_ref.dtype)

    return _hosted(
        body, name=name,
        grid_spec=pltpu.PrefetchScalarGridSpec(
            num_scalar_prefetch=1, grid=(nsh, nt),
            in_specs=[pl.BlockSpec((None, tr, cols), lambda j, t, c_ref: (j, c_ref[1] * nt + t, 0)),
                      pl.BlockSpec((None, tr, cols), lambda j, t, c_ref: (j, t, 0))],
            out_specs=pl.BlockSpec((None, tr, cols), lambda j, t, c_ref: (j, t, 0))),
        out_shape=jax.ShapeDtypeStruct((nsh, hr, cols), BF16),
        compiler_params=_params("parallel", "parallel"),
    )(c, grad, sib)


def _adamw_reduced(w, own, landed, sib, m, v, idx, name):
    r, cols = w.shape
    hr = r // 2
    tr = _tile(hr, 256, 16)
    nt = hr // tr

    def body(idx_ref, w_ref, p_ref, l_ref, s_ref, m_ref, v_ref, go_ref, d_ref, mo_ref, vo_ref):
        mine = p_ref[...].astype(F32)
        for k in range(3):
            mine = mine + l_ref[k].astype(F32)
        theirs = s_ref[0].astype(F32)
        for k in range(1, 4):
            theirs = theirs + s_ref[k].astype(F32)
        grad = jnp.where(pl.program_id(0) // nt == idx_ref[1], mine, theirs)
        go_ref[...] = grad
        d_ref[...], mo_ref[...], vo_ref[...] = _adamw_update(w_ref[...], grad, m_ref[...], v_ref[...])

    def in_half(t, half):
        return jnp.clip(t - half * nt, 0, nt - 1)

    full = pl.BlockSpec((tr, cols), lambda t, idx_ref: (t, 0))
    return _hosted(
        body, name=name,
        grid_spec=pltpu.PrefetchScalarGridSpec(
            num_scalar_prefetch=1, grid=(2 * nt,),
            in_specs=[full,
                      pl.BlockSpec((None, tr, cols), lambda t, idx_ref: (idx_ref[0], in_half(t, idx_ref[1]), 0)),
                      pl.BlockSpec((3, tr, cols), lambda t, idx_ref: (0, in_half(t, idx_ref[1]), 0)),
                      pl.BlockSpec((4, tr, cols), lambda t, idx_ref: (0, in_half(t, 1 - idx_ref[1]), 0)),
                      full, full],
            out_specs=[full] * 4),
        out_shape=[jax.ShapeDtypeStruct((r, cols), F32)] * 4,
        compiler_params=_params("arbitrary"),
    )(idx, w, own, landed, sib, m, v)


def _sum_devices(blocks, name):
    m8, n = blocks.shape
    m = m8 // N_DEV

    def body(b_ref, o_ref):
        acc = b_ref[pl.ds(0, m), :]
        for k in range(1, N_DEV):
            acc = acc + b_ref[pl.ds(k * m, m), :]
        o_ref[...] = acc

    return _hosted(
        body, name=name, out_shape=jax.ShapeDtypeStruct((m, n), F32),
        in_specs=[pl.BlockSpec(memory_space=pltpu.VMEM)], out_specs=pl.BlockSpec(memory_space=pltpu.VMEM),
        compiler_params=pltpu.CompilerParams(vmem_limit_bytes=VMEM_LIMIT),
    )(blocks)


def _adamw_update(w, grad, m, v):
    new_m = ADAM_B1 * m + (1.0 - ADAM_B1) * grad
    new_v = ADAM_B2 * v + (1.0 - ADAM_B2) * (grad * grad)
    m_hat = new_m * (1.0 / (1.0 - ADAM_B1 ** ADAM_STEP))
    v_hat = new_v * (1.0 / (1.0 - ADAM_B2 ** ADAM_STEP))
    return -ADAM_LR * (m_hat / (jnp.sqrt(v_hat) + ADAM_EPS) + ADAM_WD * w), new_m, new_v


def _adamw(w, g, m, v, name):
    r, cols = w.shape
    tr = _tile(r, 256)

    def body(w_ref, g_ref, m_ref, v_ref, go_ref, d_ref, mo_ref, vo_ref):
        grad = g_ref[...]
        go_ref[...] = grad
        d_ref[...], mo_ref[...], vo_ref[...] = _adamw_update(w_ref[...], grad, m_ref[...], v_ref[...])

    spec = pl.BlockSpec((tr, cols), lambda t: (t, 0))
    return _hosted(
        body, name=name, grid=(r // tr,), in_specs=[spec] * 4, out_specs=[spec] * 4,
        out_shape=[jax.ShapeDtypeStruct((r, cols), F32)] * 4,
        compiler_params=_params("parallel"),
    )(w, g, m, v)


def _adamw_small(grad_blocks, params, name):
    nb, npar = len(grad_blocks), len(params)

    def body(*refs):
        blocks, ins, outs = refs[:nb], refs[nb:nb + 3 * npar], refs[nb + 3 * npar:]
        for p, (w, _, _, blk, row0) in enumerate(params):
            if w.ndim == 1:
                tiled = (w.shape[0] // LANES, LANES)
                grad = blocks[blk][pl.ds(row0, tiled[0]), pl.ds(0, LANES)]
                wmv = [ins[3 * p + k][...].reshape(tiled) for k in range(3)]
            else:
                grad = blocks[blk][pl.ds(row0, w.shape[0]), :]
                wmv = [ins[3 * p + k][...] for k in range(3)]
            for k, res in enumerate((grad,) + _adamw_update(wmv[0], grad, wmv[1], wmv[2])):
                outs[4 * p + k][...] = res.reshape(w.shape)

    args = list(grad_blocks) + [a for w, m, v, _, _ in params for a in (w, m, v)]
    vmem = pl.BlockSpec(memory_space=pltpu.VMEM)
    out = _hosted(
        body, name=name, in_specs=[vmem] * len(args), out_specs=[vmem] * (4 * npar),
        out_shape=[jax.ShapeDtypeStruct(w.shape, F32) for w, _, _, _, _ in params for _ in range(4)],
    )(*args)
    return [tuple(out[4 * p:4 * p + 4]) for p in range(npar)]


WEIGHT_NAMES = (
    "ln1_0", "a0_w_in", "a0_conv", "a0_w_out", "ln2_0", "ffn0_w_gu", "ffn0_w_down",
    "ln1_1", "b1_w_grp", "b1_scale", "ln2_1", "ffn1_w_gu", "ffn1_w_down",
    "ln1_2", "c2_w_pw1", "c2_b_pw1", "c2_dw", "c2_b_dw", "c2_ln_g", "c2_ln_b", "c2_w_pw2", "c2_b_pw2",
    "ln2_2", "ffn2_w_gu", "ffn2_w_down",
    "ln1_3", "a3_w_in", "a3_conv", "a3_w_out", "ln2_3", "ffn3_w_gu", "ffn3_w_down", "ln_f")
BIG = ("a0_w_in", "a0_w_out", "ffn0_w_gu", "ffn0_w_down", "b1_w_grp", "ffn1_w_gu", "ffn1_w_down", "c2_w_pw1", "c2_w_pw2",
       "ffn2_w_gu", "ffn2_w_down", "a3_w_in", "a3_w_out", "ffn3_w_gu", "ffn3_w_down")
GROUPED = "b1_w_grp"
SMALL_SHARDED = ("a0_conv", "a3_conv", "c2_dw")
REPLICATED = tuple(n for n in WEIGHT_NAMES if n not in BIG and n not in SMALL_SHARDED)


def _pad_rows(a, mult=8):
    pad = -a.shape[0] % mult
    return a if pad == 0 else jnp.concatenate([a, jnp.zeros((pad, a.shape[1]), a.dtype)], axis=0)


def _pack_rows(parts, width):
    rows = [p.reshape(-1, width) for p in parts]
    return _pad_rows(jnp.concatenate(rows, axis=0)), [r.shape[0] for r in rows]


def _unpack_rows(packed, counts, shapes):
    out, at = [], 0
    for n, shp in zip(counts, shapes):
        out.append(packed[at:at + n].reshape(shp))
        at += n
    return out


COLUMN_SHARDED = ("w_in", "w_gu", "w_pw1")


class _Weights(dict):
    def __init__(self, bufs):
        super().__init__()
        self.bufs = bufs

    def __missing__(self, name):
        buf = self.bufs[name]
        if name == GROUPED:
            cg = buf.shape[-1]
            rq = cg // N_CHIPS
            return jnp.transpose(buf.reshape(N_CHIPS, -1, rq, cg), (1, 0, 2, 3)).reshape(-1, cg, cg)
        return buf if name.endswith(COLUMN_SHARDED) else buf.reshape(-1, buf.shape[-1])


class _Exchange:
    def __init__(self, w, mom, vel, idx):
        def shards(table):
            return {n: table[n].reshape(-1, table[n].shape[-1]) for n in BIG}

        self.w, self.mom, self.vel, self.idx = shards(w), shards(mom), shards(vel), idx
        self.bufs = {}
        self.weights = _Weights(self.bufs)
        self.grads = {}
        self.sib, self.part, self.landed, self.sib_parts, self.updates = {}, {}, {}, {}, {}

    def cast(self, names):
        by_shape = {}
        for n in names:
            by_shape.setdefault(self.w[n].shape, []).append(n)
        for group in by_shape.values():
            self.bufs.update(zip(group, _cast_to_slot([self.w[n] for n in group], self.idx, f"cast_{group[0]}")))

    @staticmethod
    def _store(table, names):
        def done(arrays):
            table.update(zip(names, arrays))
        return done

    def _grad(self, n):
        g = self.grads[n]
        if n == GROUPED:
            ng, cg, _ = g.shape
            g = jnp.transpose(g.reshape(ng, N_CHIPS, cg // N_CHIPS, cg), (1, 0, 2, 3)).astype(BF16)
        return g.reshape(N_CHIPS, -1, g.shape[-1])

    def gather_ici(self, *names, part=(0, 1)):
        return lambda: _task_gather_ici([self.bufs[n] for n in names], self._store(self.bufs, names), part)

    def gather_d2d(self, *names):
        return lambda: _task_gather_d2d([self.bufs[n] for n in names], self._store(self.bufs, names))

    def sibling_halves(self, *names):
        return lambda: _task_sibling_halves([self._grad(n) for n in names], self._store(self.sib, names))

    def add_halves(self, *names):
        def run():
            for n in names:
                self.part[n] = _add_halves(self._grad(n), self.sib.pop(n), self.idx, f"reduce_add_{n}")
        return run

    def chip_sums(self, *names, part=(0, 1)):
        def make():
            landed = [self.landed[n] for n in names] if part[0] > 0 else None
            return _task_chip_sums([self.part[n] for n in names], self._store(self.landed, names), landed, part)
        return make

    def sibling_parts(self, *names):
        return lambda: _task_sibling_parts([self.part[n] for n in names], [self.landed[n] for n in names],
                                           self._store(self.sib_parts, names))

    def adamw(self, *names):
        def run():
            for n in names:
                self.updates[n] = _adamw_reduced(self.w[n], self.part.pop(n), self.landed.pop(n), self.sib_parts.pop(n),
                                                 self.mom[n], self.vel[n], self.idx, f"adamw_{n}")
        return run


def _plan(ex):
    s = _Schedule()

    def ffn(i):
        return f"ffn{i}_w_gu", f"ffn{i}_w_down"

    c2, a3 = ("c2_w_pw1", "c2_w_pw2"), ("a3_w_in", "a3_w_out")
    first, second = (0, 2), (1, 2)
    gu, down = ffn(0)
    s.host("a0_rms", ex.gather_ici("a0_w_out"))
    s.host("a0_in", ex.gather_ici(gu, part=first), ex.gather_d2d("a0_w_out"))
    s.host("a0_conv", ex.gather_ici(gu, part=second))
    s.host("a0_out", ex.gather_ici(down), ex.gather_d2d(gu))
    s.host("ffn0_up", ex.gather_d2d(down))
    gu, down = ffn(1)
    s.host("ffn0_up", ex.gather_ici(gu, GROUPED))
    s.host("ffn0_down", ex.gather_ici(down), ex.gather_d2d(gu, GROUPED))
    s.host("ffn1_up", ex.gather_d2d(down), ex.gather_ici(*c2))
    s.host("ffn1_down", ex.gather_d2d(*c2))
    gu, down = ffn(2)
    s.host("c2_pw1", ex.gather_ici(down))
    s.host("c2_mid", ex.gather_ici(gu))
    s.host("c2_pw2", ex.gather_d2d(gu, down))
    gu, down = ffn(3)
    s.host("ffn2_up", ex.gather_ici(*a3))
    s.host("ffn2_down", ex.gather_d2d(*a3), ex.gather_ici(down))
    s.host("a3_in", ex.gather_ici(gu, part=first))
    s.host("a3_conv", ex.gather_ici(gu, part=second))
    s.host("a3_out", ex.gather_d2d(gu, down))

    def reduce_on(names, first_host, ici_hosts, last_host):
        s.host(first_host, ex.sibling_halves(*names))
        s.post(first_host, ex.add_halves(*names))
        for host, hosted, part in ici_hosts:
            s.host(host, ex.chip_sums(*hosted, part=part))
        s.host(last_host, ex.sibling_parts(*names))
        s.post(last_host, ex.adamw(*names))

    whole = (0, 1)
    gu, down = ffn(3)
    reduce_on((gu, down), "a3_out_bwd",
              [("a3_conv_bwd", (down,), whole), ("a3_dw_in", (gu,), first), ("a3_in_bwd", (gu,), second)], "ffn2_down_bwd")
    reduce_on(a3, "ffn2_down_bwd", [("ffn2_dw_down", a3[:1], whole), ("ffn2_dw_gu", a3[1:], whole)], "c2_pw2_bwd")
    gu, down = ffn(0)
    s.host("ffn0_dw_gu", ex.sibling_halves(down))
    s.post("ffn0_dw_gu", ex.add_halves(down))
    s.host("ffn0_up_bwd", ex.chip_sums(down))
    s.host("a0_out_bwd", ex.sibling_halves(gu, GROUPED))
    s.post("a0_out_bwd", ex.add_halves(gu, GROUPED))
    s.host("a0_conv_bwd", ex.chip_sums(gu, part=first), ex.chip_sums(GROUPED))
    s.host("a0_dw_in", ex.chip_sums(gu, part=second))
    s.host("a0_dw_out", ex.sibling_halves("a0_w_in"))
    s.post("a0_dw_out", ex.add_halves("a0_w_in"))
    s.host("a0_in_bwd", ex.chip_sums("a0_w_in"))
    reduce_on(ffn(2), "c2_pw2_bwd", [("c2_mid_bwd", ffn(2), whole)], "ffn1_down_bwd")
    reduce_on(c2, "ffn1_down_bwd", [("ffn1_dw_down", c2, whole)], "b1_bwd_mm")
    gu, down = ffn(1)
    reduce_on((gu, down), "b1_bwd_mm",
              [("b1_bwd_rms", (down,), whole), ("ffn0_down_bwd", (gu,), first), ("ffn0_dw_down", (gu,), second)], "ffn0_dw_gu")
    return s


def kernel(x, *rest):
    nw = len(WEIGHT_NAMES)
    w = dict(zip(WEIGHT_NAMES, rest[:nw]))
    target = rest[nw]
    mom = dict(zip(WEIGHT_NAMES, rest[nw + 1:2 * nw + 1]))
    vel = dict(zip(WEIGHT_NAMES, rest[2 * nw + 1:3 * nw + 1]))
    cx, cy, cc = _position()
    my_chip = 2 * cx + cy
    d = x.shape[-1]
    cq = d // N_CHIPS

    ex = _Exchange(w, mom, vel, jnp.stack([my_chip, cc]).astype(jnp.int32))
    sched = _plan(ex)
    ex.cast(BIG)
    _comm_only([ex.gather_ici("a0_w_in")()], "gather_a0_ici")
    _comm_only([ex.gather_d2d("a0_w_in")()], "gather_a0_d2d")

    small_blk, small_counts = _pack_rows([w[n] for n in SMALL_SHARDED], cq)
    small_all = _allgather8(small_blk, "gather_small").reshape(N_CHIPS, 2, small_blk.shape[0], cq)[:, 0]
    small_parts = _unpack_rows(jnp.transpose(small_all, (1, 0, 2)), small_counts,
                               [(w[n].reshape(-1, cq).shape[0], N_CHIPS, cq) for n in SMALL_SHARDED])
    wts = ex.weights
    for n in REPLICATED:
        wts[n] = w[n].reshape(1, -1)
    for n, part in zip(SMALL_SHARDED, small_parts):
        wts[n] = part.reshape(part.shape[0], d)

    _ACTIVE_SCHEDULE[0] = sched
    try:
        loss, dx, g = _device_step(x[0], target[0], wts, ex.grads)
    finally:
        _ACTIVE_SCHEDULE[0] = None
    assert not sched.hosts and not sched.posts, (sched.hosts, sched.posts)

    summed, last = ("ffn0_w_gu", "ffn0_w_down", GROUPED, "a0_w_in"), "a0_w_out"
    _comm_only([ex.sibling_parts(*summed)(), ex.sibling_halves(last)()], "reduce_tail_d2d")
    ex.adamw(*summed)()
    ex.add_halves(last)()
    _comm_only([ex.chip_sums(last)()], "reduce_tail_ici")
    _comm_only([ex.sibling_parts(last)()], "reduce_tail_parts")
    ex.adamw(last)()

    rep_rows = [jnp.pad(g[n].reshape(-1, LANES), ((0, 0), (0, cq - LANES))) for n in REPLICATED]
    by_chip = [jnp.transpose(g[n].reshape(g[n].shape[0], N_CHIPS, cq), (1, 0, 2)) for n in SMALL_SHARDED]
    shard_rows = jnp.concatenate(by_chip, axis=1)
    n_rep, n_shard = sum(r.shape[0] for r in rep_rows), shard_rows.shape[1]
    loss_row = jnp.broadcast_to(loss, (1, cq))
    sm_blk = _pad_rows(jnp.concatenate(rep_rows + [loss_row, shard_rows.reshape(N_CHIPS * n_shard, cq)], axis=0))
    sm_sum = _sum_devices(_allgather8(sm_blk, "gather_small_grads"), "sum_small_grads")
    mine = lax.dynamic_slice_in_dim(sm_sum, n_rep + 1 + my_chip * n_shard, n_shard, axis=0)

    out = ex.updates
    params, at = [], {0: 0, 1: 0}
    for block, names in ((0, REPLICATED), (1, SMALL_SHARDED)):
        for n in names:
            params.append((w[n], mom[n], vel[n], block, at[block]))
            at[block] += w[n].size // LANES if w[n].ndim == 1 else w[n].shape[0]
    out.update(zip(REPLICATED + SMALL_SHARDED, _adamw_small([sm_sum, mine], params, "adamw_small")))

    total = sm_sum[n_rep, 0]
    grads, deltas, new_m, new_v = ([out[n][k].reshape(w[n].shape) for n in WEIGHT_NAMES] for k in range(4))
    return (total, dx.reshape(x.shape), *grads, *deltas, *new_m, *new_v)
```

```python
import functools

import jax
import jax.numpy as jnp
from jax import lax
from jax.experimental import pallas as pl
from jax.experimental.pallas import tpu as pltpu

F32 = jnp.float32
BF16 = jnp.bfloat16

RMS_EPS = 1e-6
LN_EPS = 1e-5
POOL_WINDOWS = (2, 4, 8, 16)
SHORT_CONV_W = 3
CONF_CONV_W = 31
N_CHIPS = 4
N_DEV = 8

ADAM_LR = 0.001
ADAM_B1 = 0.9
ADAM_B2 = 0.999
ADAM_EPS = 1e-08
ADAM_WD = 0.01
ADAM_STEP = 10

V7X_VMEM_BYTES = 64 * 1024 * 1024
VMEM_LIMIT = V7X_VMEM_BYTES - 8 * 1024 * 1024
LANES = 128
POOL_HALO = 16
SCONV_HALO = 16
CONF_HALO = 32


def _params(*sem):
    return pltpu.CompilerParams(dimension_semantics=sem, vmem_limit_bytes=VMEM_LIMIT)


def _tile(n, pref, mult=8):
    t = min(n, pref)
    while t > mult and (n % t or t % mult):
        t -= mult
    assert n % t == 0 and t % mult == 0, (n, pref, mult)
    return t


def _sigmoid(x):
    return jax.nn.sigmoid(x)


def _dot(a, b):
    return jnp.dot(a, b, preferred_element_type=F32)


def _dot_nt(a, b):
    return lax.dot_general(a, b, (((1,), (1,)), ((), ())), preferred_element_type=F32)


def _dot_tn(a, b):
    return lax.dot_general(a, b, (((0,), (0,)), ((), ())), preferred_element_type=F32)


def _colsum(x):
    return jnp.sum(x, axis=0, keepdims=True)


def _rms_stats(x):
    return lax.rsqrt(jnp.mean(x * x, axis=-1, keepdims=True) + RMS_EPS)


def _rms_bwd(du, x, gain):
    r = _rms_stats(x)
    xhat = x * r
    gdy = du * gain
    dx = r * (gdy - xhat * jnp.mean(gdy * xhat, axis=-1, keepdims=True))
    return dx, _colsum(du * xhat)


class _Task:
    def __init__(self, ins, out_shapes, aliases, sems, start, wait, done):
        self.ins, self.out_shapes, self.aliases, self.sems = list(ins), list(out_shapes), dict(aliases), list(sems)
        self.start, self.wait, self.done = start, wait, done


class _Schedule:
    def __init__(self):
        self.hosts, self.posts = {}, {}

    def host(self, kernel_name, *make_tasks):
        self.hosts.setdefault(kernel_name, []).extend(make_tasks)

    def post(self, kernel_name, *thunks):
        self.posts.setdefault(kernel_name, []).extend(thunks)

    def tasks_for(self, kernel_name):
        return [make() for make in self.hosts.pop(kernel_name, ())]

    def finished(self, kernel_name):
        for thunk in self.posts.pop(kernel_name, ()):
            thunk()


_ACTIVE_SCHEDULE = [None]


def _hosted(body, name, **kw):
    def run(*args):
        sched = _ACTIVE_SCHEDULE[0]
        tasks = sched.tasks_for(name) if sched is not None else []
        out = _call_with_tasks(body, name, tasks, kw, args) if tasks else pl.pallas_call(body, name=name, **kw)(*args)
        if sched is not None:
            sched.finished(name)
        return out

    return run


def _call_with_tasks(body, name, tasks, kw, args):
    spec = kw.get("grid_spec")
    n_pre = spec.num_scalar_prefetch if spec is not None else 0
    src = dict(grid=spec.grid, in_specs=spec.in_specs, out_specs=spec.out_specs) if spec is not None else kw
    pre, args = args[:n_pre], args[n_pre:]
    grid = tuple(src.get("grid", ()))
    single = not isinstance(kw["out_shape"], (list, tuple))
    out_shape = [kw["out_shape"]] if single else list(kw["out_shape"])
    out_specs = [src["out_specs"]] if single else list(src["out_specs"])
    scratch = list(kw.get("scratch_shapes", ()))
    n_in, n_out, n_scr = len(args), len(out_shape), len(scratch)
    t_in = [a for t in tasks for a in t.ins]
    t_out = [o for t in tasks for o in t.out_shapes]
    t_sem = [s for t in tasks for s in t.sems]
    aliases, at_in, at_out = {}, n_pre + n_in, n_out
    for t in tasks:
        for i, o in t.aliases.items():
            aliases[at_in + i] = at_out + o
        at_in += len(t.ins)
        at_out += len(t.out_shapes)

    def wrapped(*refs):
        pre_refs, refs = refs[:n_pre], refs[n_pre:]
        a = n_in
        b = a + len(t_in)
        c = b + n_out
        d = c + len(t_out)
        e = d + n_scr
        ins, tins, outs, touts, scr, tsems = refs[:a], refs[a:b], refs[b:c], refs[c:d], refs[d:e], refs[e:]
        views, i0, o0, s0 = [], 0, 0, 0
        for t in tasks:
            views.append((tins[i0:i0 + len(t.ins)], touts[o0:o0 + len(t.out_shapes)], tsems[s0:s0 + len(t.sems)]))
            i0, o0, s0 = i0 + len(t.ins), o0 + len(t.out_shapes), s0 + len(t.sems)

        def start_all():
            for t, v in zip(tasks, views):
                t.start(*v)

        def wait_all():
            for t, v in zip(tasks, views):
                t.wait(*v)

        if grid:
            first = functools.reduce(jnp.logical_and, [pl.program_id(i) == 0 for i in range(len(grid))])
            last = functools.reduce(jnp.logical_and, [pl.program_id(i) == grid[i] - 1 for i in range(len(grid))])
            pl.when(first)(start_all)
            body(*pre_refs, *ins, *outs, *scr)
            pl.when(last)(wait_all)
        else:
            start_all()
            body(*pre_refs, *ins, *outs, *scr)
            wait_all()

    in_specs = list(src["in_specs"]) + [ANY] * len(t_in)
    out_specs = out_specs + [ANY] * len(t_out)
    if spec is not None:
        layout = dict(grid_spec=pltpu.PrefetchScalarGridSpec(
            num_scalar_prefetch=n_pre, grid=grid, in_specs=in_specs, out_specs=out_specs, scratch_shapes=scratch + t_sem))
    else:
        layout = dict(grid=grid, in_specs=in_specs, out_specs=out_specs, scratch_shapes=scratch + t_sem)
    res = pl.pallas_call(
        wrapped, name=name, out_shape=out_shape + t_out, input_output_aliases=aliases,
        compiler_params=pltpu.CompilerParams(dimension_semantics=("arbitrary",) * len(grid), vmem_limit_bytes=VMEM_LIMIT),
        **layout,
    )(*pre, *args, *t_in)
    res = list(res)
    own, rest = res[:n_out], res[n_out:]
    for t in tasks:
        t.done(rest[:len(t.out_shapes)])
        rest = rest[len(t.out_shapes):]
    return own[0] if single else own


def _comm_only(tasks, name):
    _call_with_tasks(lambda: None, name, tasks, dict(grid=(), in_specs=[], out_specs=[], out_shape=[]), ())


def _rms_fwd(h, gain, name):
    s, d = h.shape
    tm = _tile(s, 512)

    def body(h_ref, g_ref, u_ref):
        x = h_ref[...]
        u_ref[...] = (x * _rms_stats(x) * g_ref[...]).astype(u_ref.dtype)

    return _hosted(
        body, name=name, grid=(s // tm,),
        in_specs=[pl.BlockSpec((tm, d), lambda m: (m, 0)), pl.BlockSpec((1, d), lambda m: (0, 0))],
        out_specs=pl.BlockSpec((tm, d), lambda m: (m, 0)),
        out_shape=jax.ShapeDtypeStruct((s, d), BF16),
        compiler_params=_params("parallel"),
    )(h, gain)


def _mm_col(a, w, bias, name):
    s, k = a.shape
    nsh, _, ns = w.shape
    tm = _tile(s, 512)
    has_bias = bias is not None

    def body(a_ref, w_ref, *rest):
        o_ref = rest[-1]
        x = a_ref[...]
        for j in range(nsh):
            cols = pl.ds(j * ns, ns)
            acc = _dot(x, w_ref[j])
            if has_bias:
                acc = acc + rest[0][:, cols]
            o_ref[:, cols] = acc.astype(o_ref.dtype)

    in_specs = [pl.BlockSpec((tm, k), lambda m: (m, 0)), pl.BlockSpec((nsh, k, ns), lambda m: (0, 0, 0))]
    args = [a, w]
    if has_bias:
        in_specs.append(pl.BlockSpec((1, nsh * ns), lambda m: (0, 0)))
        args.append(bias)
    return _hosted(
        body, name=name, grid=(s // tm,), in_specs=in_specs,
        out_specs=pl.BlockSpec((tm, nsh * ns), lambda m: (m, 0)),
        out_shape=jax.ShapeDtypeStruct((s, nsh * ns), BF16),
        compiler_params=_params("parallel"),
    )(*args)


def _mm_row(a, w, res, bias, name):
    s = a.shape[0]
    k, n = w.shape
    tm = _tile(s, 512)
    has_bias = bias is not None

    def body(a_ref, w_ref, res_ref, *rest):
        o_ref = rest[-1]
        y = res_ref[...] + _dot(a_ref[...], w_ref[...])
        if has_bias:
            y = y + rest[0][...]
        o_ref[...] = y

    in_specs = [pl.BlockSpec((tm, k), lambda m: (m, 0)), pl.BlockSpec((k, n), lambda m: (0, 0)),
                pl.BlockSpec((tm, n), lambda m: (m, 0))]
    args = [a, w, res]
    if has_bias:
        in_specs.append(pl.BlockSpec((1, n), lambda m: (0, 0)))
        args.append(bias)
    return _hosted(
        body, name=name, grid=(s // tm,), in_specs=in_specs,
        out_specs=pl.BlockSpec((tm, n), lambda m: (m, 0)),
        out_shape=jax.ShapeDtypeStruct((s, n), F32),
        compiler_params=_params("parallel"),
    )(*args)


def _mm_nt_row(dy, w, name):
    s, n = dy.shape
    k = w.shape[0]
    tm = _tile(s, 512)

    def body(dy_ref, w_ref, o_ref):
        o_ref[...] = _dot_nt(dy_ref[...].astype(BF16), w_ref[...])

    return _hosted(
        body, name=name, grid=(s // tm,),
        in_specs=[pl.BlockSpec((tm, n), lambda m: (m, 0)), pl.BlockSpec((k, n), lambda m: (0, 0))],
        out_specs=pl.BlockSpec((tm, k), lambda m: (m, 0)),
        out_shape=jax.ShapeDtypeStruct((s, k), F32),
        compiler_params=_params("parallel"),
    )(dy, w)


def _ffn_up(u, w, name):
    s, d = u.shape
    _, _, ns = w.shape
    tm = _tile(s, 512)

    def body(u_ref, wg_ref, wu_ref, act_ref, s1_ref, q1_ref):
        x = u_ref[...]
        g = _dot(x, wg_ref[...])
        up = _dot(x, wu_ref[...])
        sg = _sigmoid(g)
        s1 = g * sg
        act_ref[...] = (s1 * up).astype(act_ref.dtype)
        s1_ref[...] = s1.astype(s1_ref.dtype)
        q1_ref[...] = (up * sg * (1.0 + g * (1.0 - sg))).astype(q1_ref.dtype)

    out = pl.BlockSpec((tm, ns), lambda j, m: (m, j))
    return _hosted(
        body, name=name, grid=(2, s // tm),
        in_specs=[pl.BlockSpec((tm, d), lambda j, m: (m, 0)), pl.BlockSpec((None, d, ns), lambda j, m: (j, 0, 0)),
                  pl.BlockSpec((None, d, ns), lambda j, m: (j + 2, 0, 0))],
        out_specs=[out, out, out],
        out_shape=[jax.ShapeDtypeStruct((s, 2 * ns), BF16)] * 3,
        compiler_params=_params("parallel", "parallel"),
    )(u, w, w)


def _ffn_down_bwd(dh, w, s1, q1, name):
    s, d = dh.shape
    f = w.shape[0]
    tm = _tile(s, 256)

    def body(dh_ref, w_ref, s1_ref, q1_ref, o_ref):
        da = _dot_nt(dh_ref[...].astype(BF16), w_ref[...])
        o_ref[:, :f] = (da * q1_ref[...].astype(F32)).astype(o_ref.dtype)
        o_ref[:, f:] = (da * s1_ref[...].astype(F32)).astype(o_ref.dtype)

    return _hosted(
        body, name=name, grid=(s // tm,),
        in_specs=[pl.BlockSpec((tm, d), lambda m: (m, 0)), pl.BlockSpec((f, d), lambda m: (0, 0)),
                  pl.BlockSpec((tm, f), lambda m: (m, 0)), pl.BlockSpec((tm, f), lambda m: (m, 0))],
        out_specs=pl.BlockSpec((tm, 2 * f), lambda m: (m, 0)),
        out_shape=jax.ShapeDtypeStruct((s, 2 * f), BF16),
        compiler_params=_params("parallel"),
    )(dh, w, s1, q1)


def _mm_nt_col_rms_bwd(dy, w, h, gain, dh, name):
    s = dy.shape[0]
    nsh, k, ns = w.shape
    tm = _tile(s, 256)

    def body(dy_ref, w_ref, h_ref, g_ref, dh_ref, o_ref, dg_ref):
        du = _dot_nt(dy_ref[:, :ns], w_ref[0])
        for j in range(1, nsh):
            du = du + _dot_nt(dy_ref[:, j * ns:(j + 1) * ns], w_ref[j])
        dx, dg = _rms_bwd(du, h_ref[...], g_ref[...])
        o_ref[...] = dh_ref[...] + dx
        _accumulate(dg_ref, dg, pl.program_id(0) == 0)

    return _hosted(
        body, name=name, grid=(s // tm,),
        in_specs=[pl.BlockSpec((tm, nsh * ns), lambda m: (m, 0)), pl.BlockSpec((nsh, k, ns), lambda m: (0, 0, 0)),
                  pl.BlockSpec((tm, k), lambda m: (m, 0)), pl.BlockSpec((1, k), lambda m: (0, 0)),
                  pl.BlockSpec((tm, k), lambda m: (m, 0))],
        out_specs=[pl.BlockSpec((tm, k), lambda m: (m, 0)), pl.BlockSpec((1, k), lambda m: (0, 0))],
        out_shape=[jax.ShapeDtypeStruct((s, k), F32), jax.ShapeDtypeStruct((1, k), F32)],
        compiler_params=_params("arbitrary"),
    )(dy, w, h, gain, dh)


def _mm_tn(a, dy, nsh, name):
    s, k = a.shape
    ns = dy.shape[1] // nsh
    tm = _tile(s, 1024)
    tk = _tile(k, 1408, LANES)
    nk, nm = k // tk, s // tm

    def body(a_ref, dy_ref, o_ref, acc_ref):
        m = pl.program_id(2)
        part = _dot_tn(a_ref[...], dy_ref[...].astype(BF16))

        @pl.when(m == 0)
        def _():
            acc_ref[...] = part

        @pl.when(m > 0)
        def _():
            acc_ref[...] += part

        @pl.when(m == nm - 1)
        def _():
            o_ref[...] = acc_ref[...].astype(o_ref.dtype)

    return _hosted(
        body, name=name, grid=(nsh, nk, nm),
        in_specs=[pl.BlockSpec((tm, tk), lambda j, kk, m: (m, kk)), pl.BlockSpec((tm, ns), lambda j, kk, m: (m, j))],
        out_specs=pl.BlockSpec((None, tk, ns), lambda j, kk, m: (j, kk, 0)),
        out_shape=jax.ShapeDtypeStruct((nsh, k, ns), BF16),
        scratch_shapes=[pltpu.VMEM((tk, ns), F32)],
        compiler_params=_params("parallel", "parallel", "arbitrary"),
    )(a, dy)


def _main_spec(tm, w):
    return pl.BlockSpec((tm, w), lambda m: (m, 0))


def _before_spec(tm, hb, w):
    return pl.BlockSpec((hb, w), lambda m: (jnp.maximum(m * (tm // hb) - 1, 0), 0))


def _after_spec(tm, hb, w, s):
    return pl.BlockSpec((hb, w), lambda m: (jnp.minimum((m + 1) * (tm // hb), s // hb - 1), 0))


def _row_spec(w, rows=1):
    return pl.BlockSpec((rows, w), lambda m: (0, 0))


CHUNK_LANES = 4 * LANES
CHUNK_ROWS = 32


def _build_shifts(ext8_ref, residues=range(1, 8)):
    n = ext8_ref.shape[1] - 8
    for r in residues:
        ext8_ref[r, pl.ds(0, n), :] = ext8_ref[0, pl.ds(r, n), :]


def _fold_rows(x):
    return functools.reduce(lambda p, q: p + q, [x[i:i + 8] for i in range(0, x.shape[0], 8)])


def _shifted(ext8_ref, shift, r0, rows, cols):
    return ext8_ref[shift % 8, pl.ds(pl.multiple_of(shift - shift % 8 + r0, 8), rows), cols]


def _lane_chunk(i):
    return pl.ds(pl.multiple_of(i * CHUNK_LANES, CHUNK_LANES), CHUNK_LANES)


def _sum_terms(terms, ways=4):
    accs = []
    for i, t in enumerate(terms):
        if i < ways:
            accs.append(t)
        else:
            accs[i % ways] = accs[i % ways] + t
    while len(accs) > 1:
        accs = [accs[i] + accs[i + 1] if i + 1 < len(accs) else accs[i] for i in range(0, len(accs), 2)]
    return accs[0]


def _accumulate(ref, val, first):
    @pl.when(first)
    def _():
        ref[...] = val

    @pl.when(jnp.logical_not(first))
    def _():
        ref[...] += val


SCONV_Z_SHIFTS = tuple(SCONV_HALO - (SHORT_CONV_W - 1) + k for k in range(SHORT_CONV_W))


def _sconv_z_taps(zext_ref, r0, cols):
    return [_shifted(zext_ref, shift, r0, CHUNK_ROWS, cols) for shift in SCONV_Z_SHIFTS]


def _weighted(cw_ref, cols, terms):
    return _sum_terms((cw_ref[k:k + 1, cols] * t for k, t in enumerate(terms)), ways=len(terms))


def _sconv_fill_z(zext_ref, main_ref, before_ref, d, m):
    hb = SCONV_HALO
    zb = before_ref[:, d:2 * d].astype(F32) * before_ref[:, 2 * d:].astype(F32)
    zext_ref[pl.ds(0, hb), :] = jnp.where(m > 0, zb, 0.0)
    zext_ref[pl.ds(hb, main_ref.shape[0]), :] = main_ref[:, d:2 * d].astype(F32) * main_ref[:, 2 * d:].astype(F32)


def _sconv_fwd(bcv, cw, name):
    s, d3 = bcv.shape
    d = d3 // 3
    tm = _tile(s, 256, CHUNK_ROWS)
    row_chunks = tm // CHUNK_ROWS

    def body(main_ref, before_ref, cw_ref, p_ref, zext_ref):
        m = pl.program_id(0)
        _sconv_fill_z(zext_ref.at[0], main_ref, before_ref, d, m)
        _build_shifts(zext_ref, [shift % 8 for shift in SCONV_Z_SHIFTS if shift % 8])

        def chunk(i, carry):
            cols = _lane_chunk(i // row_chunks)
            r0 = pl.multiple_of((i % row_chunks) * CHUNK_ROWS, CHUNK_ROWS)
            rows = pl.ds(r0, CHUNK_ROWS)
            zc = _weighted(cw_ref, cols, _sconv_z_taps(zext_ref, r0, cols))
            p_ref[rows, cols] = (main_ref[rows, cols].astype(F32) * zc).astype(p_ref.dtype)
            return carry

        lax.fori_loop(0, row_chunks * (d // CHUNK_LANES), chunk, 0)

    return _hosted(
        body, name=name, grid=(s // tm,),
        in_specs=[_main_spec(tm, d3), _before_spec(tm, SCONV_HALO, d3), _row_spec(d, SHORT_CONV_W)],
        out_specs=_main_spec(tm, d),
        out_shape=jax.ShapeDtypeStruct((s, d), BF16),
        scratch_shapes=[pltpu.VMEM((8, tm + SCONV_HALO, d), F32)],
        compiler_params=_params("parallel"),
    )(bcv, bcv, cw)


def _sconv_bwd(dp, bcv, cw, name):
    s, d3 = bcv.shape
    d = d3 // 3
    tm = _tile(s, 256, CHUNK_ROWS)
    nm = s // tm
    ha = 8
    kw = SHORT_CONV_W

    def body(dp_ref, dpa_ref, main_ref, before_ref, after_ref, cw_ref, o_ref, dcw_ref, zext_ref, dext_ref):
        m = pl.program_id(0)
        _sconv_fill_z(zext_ref.at[0], main_ref, before_ref, d, m)
        _build_shifts(zext_ref, [shift % 8 for shift in SCONV_Z_SHIFTS if shift % 8])
        dext_ref[0, pl.ds(0, tm), :] = dp_ref[...] * main_ref[:, :d].astype(F32)
        dza = dpa_ref[...] * after_ref[:, :d].astype(F32)[0:ha]
        dext_ref[0, pl.ds(tm, ha), :] = jnp.where(m < nm - 1, dza, 0.0)
        _build_shifts(dext_ref, range(1, kw))

        @pl.when(m == 0)
        def _():
            dcw_ref[...] = jnp.zeros_like(dcw_ref)

        zero = jnp.zeros((8, CHUNK_LANES), F32)

        def lane_chunk(ci, carry):
            cols = _lane_chunk(ci)
            c_cols, v_cols = (pl.ds(pl.multiple_of(part * d + ci * CHUNK_LANES, CHUNK_LANES), CHUNK_LANES) for part in (1, 2))

            def row_chunk(ri, sums):
                r0 = pl.multiple_of(ri * CHUNK_ROWS, CHUNK_ROWS)
                rows = pl.ds(r0, CHUNK_ROWS)
                z = _sconv_z_taps(zext_ref, r0, cols)
                o_ref[rows, cols] = (dp_ref[rows, cols] * _weighted(cw_ref, cols, z)).astype(o_ref.dtype)
                dzc = [_shifted(dext_ref, kw - 1 - k, r0, CHUNK_ROWS, cols) for k in range(kw)]
                dz = _weighted(cw_ref, cols, dzc)
                o_ref[rows, c_cols] = (dz * main_ref[rows, v_cols].astype(F32)).astype(o_ref.dtype)
                o_ref[rows, v_cols] = (dz * main_ref[rows, c_cols].astype(F32)).astype(o_ref.dtype)
                return tuple(acc + _fold_rows(dzc[kw - 1] * z[k]) for k, acc in enumerate(sums))

            sums = lax.fori_loop(0, tm // CHUNK_ROWS, row_chunk, (zero,) * kw)
            for k in range(kw):
                dcw_ref[k:k + 1, cols] += _colsum(sums[k])
            return carry

        lax.fori_loop(0, d // CHUNK_LANES, lane_chunk, 0)

    return _hosted(
        body, name=name, grid=(nm,),
        in_specs=[_main_spec(tm, d), _after_spec(tm, ha, d, s), _main_spec(tm, d3), _before_spec(tm, SCONV_HALO, d3),
                  _after_spec(tm, SCONV_HALO, d3, s), _row_spec(d, SHORT_CONV_W)],
        out_specs=[_main_spec(tm, d3), _row_spec(d, 8)],
        out_shape=[jax.ShapeDtypeStruct((s, d3), BF16), jax.ShapeDtypeStruct((8, d), F32)],
        scratch_shapes=[pltpu.VMEM((8, tm + SCONV_HALO, d), F32), pltpu.VMEM((8, tm + ha, d), F32)],
        compiler_params=_params("arbitrary"),
    )(dp, dp, bcv, bcv, bcv, cw)


def _pool_counts(t0, tm, w):
    t = t0 + lax.broadcasted_iota(jnp.int32, (tm, 1), 0)
    return jnp.minimum(t + 1, w).astype(F32)


def _pool_fwd(h, gain, wg, scale, name):
    s, d = h.shape
    ng, cg, _ = wg.shape
    tm = _tile(s, 512, POOL_HALO)

    def body(h_ref, hb_ref, g_ref, wg_ref, sc_ref, o_ref, mx_ref, uext_ref):
        m = pl.program_id(0)
        x = h_ref[...]
        gain_row = g_ref[...]
        xb = hb_ref[...]
        uext_ref[pl.ds(0, POOL_HALO), :] = jnp.where(m > 0, xb * _rms_stats(xb) * gain_row, 0.0)
        uext_ref[pl.ds(POOL_HALO, tm), :] = x * _rms_stats(x) * gain_row
        for gi, win in enumerate(POOL_WINDOWS):
            cols = pl.ds(gi * cg, cg)
            u_g = uext_ref[pl.ds(POOL_HALO, tm), cols]
            acc = u_g
            for i in range(1, win):
                acc = acc + uext_ref[pl.ds(POOL_HALO - i, tm), cols]
            mixed = (acc / _pool_counts(m * tm, tm, win) - u_g).astype(BF16)
            mx_ref[:, cols] = mixed
            o_ref[:, cols] = x[:, gi * cg:(gi + 1) * cg] + _dot(mixed, wg_ref[gi]) * sc_ref[:, cols]

    return _hosted(
        body, name=name, grid=(s // tm,),
        in_specs=[_main_spec(tm, d), _before_spec(tm, POOL_HALO, d), _row_spec(d),
                  pl.BlockSpec((ng, cg, cg), lambda m: (0, 0, 0)), _row_spec(d)],
        out_specs=[_main_spec(tm, d), _main_spec(tm, d)],
        out_shape=[jax.ShapeDtypeStruct((s, d), F32), jax.ShapeDtypeStruct((s, d), BF16)],
        scratch_shapes=[pltpu.VMEM((tm + POOL_HALO, d), F32)],
        compiler_params=_params("parallel"),
    )(h, h, gain, wg, scale)


def _pool_bwd_mm(dh, mixed, wg, scale, name):
    s, d = dh.shape
    ng, cg, _ = wg.shape
    tm = _tile(s, 512)

    def body(dh_ref, mx_ref, wg_ref, sc_ref, dmx_ref, dwg_ref, dsc_ref):
        first = pl.program_id(0) == 0
        for gi in range(ng):
            cols = pl.ds(gi * cg, cg)
            dh_g = dh_ref[:, cols]
            mixed = mx_ref[:, cols]
            w_g = wg_ref[gi]
            dy = (dh_g * sc_ref[:, cols]).astype(BF16)
            dmx_ref[:, cols] = _dot_nt(dy, w_g)
            _accumulate(dsc_ref.at[:, cols], _colsum(dh_g * _dot(mixed, w_g)), first)
            _accumulate(dwg_ref.at[gi], _dot_tn(mixed, dy), first)

    return _hosted(
        body, name=name, grid=(s // tm,),
        in_specs=[_main_spec(tm, d), _main_spec(tm, d), pl.BlockSpec((ng, cg, cg), lambda m: (0, 0, 0)), _row_spec(d)],
        out_specs=[_main_spec(tm, d), pl.BlockSpec((ng, cg, cg), lambda m: (0, 0, 0)), _row_spec(d)],
        out_shape=[jax.ShapeDtypeStruct((s, d), F32), jax.ShapeDtypeStruct((ng, cg, cg), F32),
                   jax.ShapeDtypeStruct((1, d), F32)],
        compiler_params=_params("arbitrary"),
    )(dh, mixed, wg, scale)


def _pool_bwd_rms(dmixed, h, gain, dh, name):
    s, d = h.shape
    cg = d // len(POOL_WINDOWS)
    tm = _tile(s, 512, POOL_HALO)
    nm = s // tm

    def body(dmx_ref, dmxa_ref, h_ref, g_ref, dh_ref, o_ref, dg_ref, eext_ref, du_ref):
        m = pl.program_id(0)
        for gi, win in enumerate(POOL_WINDOWS):
            cols = pl.ds(gi * cg, cg)
            dmx = dmx_ref[:, cols]
            eext_ref[pl.ds(0, tm), cols] = dmx / _pool_counts(m * tm, tm, win)
            ea = dmxa_ref[:, cols] / _pool_counts((m + 1) * tm, POOL_HALO, win)
            eext_ref[pl.ds(tm, POOL_HALO), cols] = jnp.where(m < nm - 1, ea, 0.0)
            acc = -dmx
            for i in range(win):
                acc = acc + eext_ref[pl.ds(i, tm), cols]
            du_ref[:, cols] = acc
        dx, dg = _rms_bwd(du_ref[...], h_ref[...], g_ref[...])
        o_ref[...] = dh_ref[...] + dx
        _accumulate(dg_ref, dg, m == 0)

    return _hosted(
        body, name=name, grid=(nm,),
        in_specs=[_main_spec(tm, d), _after_spec(tm, POOL_HALO, d, s), _main_spec(tm, d), _row_spec(d), _main_spec(tm, d)],
        out_specs=[_main_spec(tm, d), _row_spec(d)],
        out_shape=[jax.ShapeDtypeStruct((s, d), F32), jax.ShapeDtypeStruct((1, d), F32)],
        scratch_shapes=[pltpu.VMEM((tm + POOL_HALO, d), F32), pltpu.VMEM((tm, d), F32)],
        compiler_params=_params("arbitrary"),
    )(dmixed, dmixed, h, gain, dh)


def _conf_fill_h(hext_ref, main_ref, before_ref, d, m):
    hb = before_ref[:, :d].astype(F32) * _sigmoid(before_ref[:, d:].astype(F32))
    hext_ref[pl.ds(0, CONF_HALO), :] = jnp.where(m > 0, hb, 0.0)
    hext_ref[pl.ds(CONF_HALO, main_ref.shape[0]), :] = main_ref[:, :d].astype(F32) * _sigmoid(main_ref[:, d:].astype(F32))


def _layernorm_parts(hc, g, b):
    mu = jnp.mean(hc, axis=-1, keepdims=True)
    xc = hc - mu
    rs = lax.rsqrt(jnp.mean(xc * xc, axis=-1, keepdims=True) + LN_EPS)
    xhat = xc * rs
    return xhat, rs, xhat * g + b


def _conf_mid_fwd(ag, dw, b_dw, ln_g, ln_b, name):
    s, d2 = ag.shape
    d = d2 // 2
    tm = _tile(s, 256, CONF_HALO)
    base = CONF_HALO - (CONF_CONV_W - 1)

    def body(main_ref, before_ref, dw_ref, bdw_ref, g_ref, b_ref, s_ref, hc_ref, hext_ref):
        m = pl.program_id(0)
        _conf_fill_h(hext_ref.at[0], main_ref, before_ref, d, m)
        _build_shifts(hext_ref)
        row_chunks = tm // CHUNK_ROWS

        def conv_chunk(i, carry):
            cols = _lane_chunk(i // row_chunks)
            r0 = pl.multiple_of((i % row_chunks) * CHUNK_ROWS, CHUNK_ROWS)
            taps = (dw_ref[kk:kk + 1, cols] * _shifted(hext_ref, base + kk, r0, CHUNK_ROWS, cols) for kk in range(CONF_CONV_W))
            hc_ref[pl.ds(r0, CHUNK_ROWS), cols] = bdw_ref[:, cols] + _sum_terms(taps, ways=1)
            return carry

        lax.fori_loop(0, row_chunks * (d // CHUNK_LANES), conv_chunk, 0)
        _, _, l = _layernorm_parts(hc_ref[...], g_ref[...], b_ref[...])
        s_ref[...] = (l * _sigmoid(l)).astype(s_ref.dtype)

    return _hosted(
        body, name=name, grid=(s // tm,),
        in_specs=[_main_spec(tm, d2), _before_spec(tm, CONF_HALO, d2), _row_spec(d, CONF_CONV_W), _row_spec(d),
                  _row_spec(d), _row_spec(d)],
        out_specs=[_main_spec(tm, d), _main_spec(tm, d)],
        out_shape=[jax.ShapeDtypeStruct((s, d), BF16), jax.ShapeDtypeStruct((s, d), F32)],
        scratch_shapes=[pltpu.VMEM((8, tm + CONF_HALO, d), F32)],
        compiler_params=_params("parallel"),
    )(ag, ag, dw, b_dw, ln_g, ln_b)


def _conf_out_bwd(dh, w, hc, ln_g, ln_b, name):
    s, d = dh.shape
    tm = _tile(s, 256)

    def body(dh_ref, w_ref, hc_ref, g_ref, b_ref, o_ref, dg_ref, db_ref, dbo_ref):
        first = pl.program_id(0) == 0
        dh_t = dh_ref[...]
        ds = _dot_nt(dh_t.astype(BF16), w_ref[...])
        xhat, rs, l = _layernorm_parts(hc_ref[...], g_ref[...], b_ref[...])
        sg = _sigmoid(l)
        dl = ds * sg * (1.0 + l * (1.0 - sg))
        dxh = dl * g_ref[...]
        o_ref[...] = rs * (dxh - jnp.mean(dxh, axis=-1, keepdims=True)
                           - xhat * jnp.mean(dxh * xhat, axis=-1, keepdims=True))
        _accumulate(dg_ref, _colsum(dl * xhat), first)
        _accumulate(db_ref, _colsum(dl), first)
        _accumulate(dbo_ref, _colsum(dh_t), first)

    return _hosted(
        body, name=name, grid=(s // tm,),
        in_specs=[_main_spec(tm, d), pl.BlockSpec((d, d), lambda m: (0, 0)), _main_spec(tm, d), _row_spec(d), _row_spec(d)],
        out_specs=[_main_spec(tm, d), _row_spec(d), _row_spec(d), _row_spec(d)],
        out_shape=[jax.ShapeDtypeStruct((s, d), F32)] + [jax.ShapeDtypeStruct((1, d), F32)] * 3,
        compiler_params=_params("arbitrary"),
    )(dh, w, hc, ln_g, ln_b)


def _conf_mid_bwd(dhc, ag, dw, name):
    s, d2 = ag.shape
    d = d2 // 2
    tm = _tile(s, 256, CONF_HALO)
    nm = s // tm
    kw = CONF_CONV_W
    base = CONF_HALO - (kw - 1)

    def body(dhc_ref, dhca_ref, main_ref, before_ref, dw_ref, o_ref, ddw_ref, dbdw_ref, dbpw_ref, hext_ref, dext_ref):
        m = pl.program_id(0)
        first = m == 0
        _conf_fill_h(hext_ref.at[0], main_ref, before_ref, d, m)
        _build_shifts(hext_ref)
        dext_ref[0, pl.ds(0, tm), :] = dhc_ref[...]
        dext_ref[0, pl.ds(tm, CONF_HALO), :] = jnp.where(m < nm - 1, dhca_ref[...], 0.0)
        _build_shifts(dext_ref)

        @pl.when(first)
        def _():
            ddw_ref[...] = jnp.zeros_like(ddw_ref)
            dbdw_ref[...] = jnp.zeros_like(dbdw_ref)
            dbpw_ref[...] = jnp.zeros_like(dbpw_ref)

        zero = jnp.zeros((8, CHUNK_LANES), F32)
        tap_group = 8

        def fold(x):
            return functools.reduce(lambda p, q: p + q, [x[i:i + 8] for i in range(0, CHUNK_ROWS, 8)])

        def lane_chunk(ci, carry):
            cols = _lane_chunk(ci)
            gate_cols = pl.ds(pl.multiple_of(d + ci * CHUNK_LANES, CHUNK_LANES), CHUNK_LANES)

            def through_conv(ri, sums):
                r0 = pl.multiple_of(ri * CHUNK_ROWS, CHUNK_ROWS)
                rows = pl.ds(r0, CHUNK_ROWS)
                dhh = _sum_terms((dw_ref[kk:kk + 1, cols] * _shifted(dext_ref, kw - 1 - kk, r0, CHUNK_ROWS, cols)
                                  for kk in range(kw)), ways=1)
                a = main_ref[rows, cols].astype(F32)
                sg = _sigmoid(main_ref[rows, gate_cols].astype(F32))
                da = dhh * sg
                dgate = dhh * a * sg * (1.0 - sg)
                o_ref[rows, cols] = da.astype(o_ref.dtype)
                o_ref[rows, gate_cols] = dgate.astype(o_ref.dtype)
                return sums[0] + fold(da), sums[1] + fold(dgate), sums[2] + fold(dext_ref[0, rows, cols])

            sum_da, sum_dgate, sum_dhc = lax.fori_loop(0, tm // CHUNK_ROWS, through_conv, (zero, zero, zero))
            dbdw_ref[:, cols] += _colsum(sum_dhc)
            dbpw_ref[:, cols] += _colsum(sum_da)
            dbpw_ref[:, gate_cols] += _colsum(sum_dgate)

            for k0 in range(0, kw, tap_group):
                group = range(k0, min(k0 + tap_group, kw))

                def tap_gradients(ri, accs, group=group):
                    for sub in range(0, CHUNK_ROWS, 8):
                        r0 = pl.multiple_of(ri * CHUNK_ROWS + sub, 8)
                        dhc_c = dext_ref[0, pl.ds(r0, 8), cols]
                        accs = tuple(acc + dhc_c * _shifted(hext_ref, base + kk, r0, 8, cols) for kk, acc in zip(group, accs))
                    return accs

                accs = lax.fori_loop(0, tm // CHUNK_ROWS, tap_gradients, (zero,) * len(group))
                for kk, acc in zip(group, accs):
                    ddw_ref[kk:kk + 1, cols] += _colsum(acc)
            return carry

        lax.fori_loop(0, d // CHUNK_LANES, lane_chunk, 0)

    return _hosted(
        body, name=name, grid=(nm,),
        in_specs=[_main_spec(tm, d), _after_spec(tm, CONF_HALO, d, s), _main_spec(tm, d2), _before_spec(tm, CONF_HALO, d2),
                  _row_spec(d, kw)],
        out_specs=[_main_spec(tm, d2), _row_spec(d, 32), _row_spec(d), _row_spec(d2)],
        out_shape=[jax.ShapeDtypeStruct((s, d2), BF16), jax.ShapeDtypeStruct((32, d), F32),
                   jax.ShapeDtypeStruct((1, d), F32), jax.ShapeDtypeStruct((1, d2), F32)],
        scratch_shapes=[pltpu.VMEM((8, tm + CONF_HALO, d), F32), pltpu.VMEM((8, tm + CONF_HALO, d), F32)],
        compiler_params=_params("arbitrary"),
    )(dhc, dhc, ag, ag, dw)


def _loss_head(h, gain, target, name):
    s, d = h.shape
    tm = _tile(s, 512)

    def body(h_ref, g_ref, t_ref, loss_ref, dh_ref, dg_ref):
        first = pl.program_id(0) == 0
        x = h_ref[...]
        err = x * _rms_stats(x) * g_ref[...] - t_ref[...]
        part = 0.5 * jnp.sum(jnp.mean(err * err, axis=-1, keepdims=True), axis=0, keepdims=True)
        dx, dg = _rms_bwd(err * (1.0 / d), x, g_ref[...])
        dh_ref[...] = dx
        _accumulate(loss_ref, part, first)
        _accumulate(dg_ref, dg, first)

    return _hosted(
        body, name=name, grid=(s // tm,),
        in_specs=[_main_spec(tm, d), _row_spec(d), _main_spec(tm, d)],
        out_specs=[pl.BlockSpec((1, 1), lambda m: (0, 0)), _main_spec(tm, d), _row_spec(d)],
        out_shape=[jax.ShapeDtypeStruct((1, 1), F32), jax.ShapeDtypeStruct((s, d), F32), jax.ShapeDtypeStruct((1, d), F32)],
        compiler_params=_params("arbitrary"),
    )(h, gain, target)


def _ffn_fwd(h, wts, i):
    u = _rms_fwd(h, wts[f"ln2_{i}"], f"ffn{i}_rms")
    act, s1, q1 = _ffn_up(u, wts[f"ffn{i}_w_gu"], f"ffn{i}_up")
    h_new = _mm_row(act, wts[f"ffn{i}_w_down"], h, None, f"ffn{i}_down")
    return h_new, (h, u, act, s1, q1)


def _ffn_bwd(dh, saved, wts, i, g):
    h, u, act, s1, q1 = saved
    dgu = _ffn_down_bwd(dh, wts[f"ffn{i}_w_down"], s1, q1, f"ffn{i}_down_bwd")
    g[f"ffn{i}_w_down"] = _mm_tn(act, dh, 1, f"ffn{i}_dw_down")
    g[f"ffn{i}_w_gu"] = _mm_tn(u, dgu, N_CHIPS, f"ffn{i}_dw_gu")
    dh_new, g[f"ln2_{i}"] = _mm_nt_col_rms_bwd(dgu, wts[f"ffn{i}_w_gu"], h, wts[f"ln2_{i}"], dh, f"ffn{i}_up_bwd")
    return dh_new


def _device_step(x, target, wts, g=None):
    g = {} if g is None else g
    saved = {}
    h = x

    def short_conv_fwd(h, i):
        u = _rms_fwd(h, wts[f"ln1_{i}"], f"a{i}_rms")
        bcv = _mm_col(u, wts[f"a{i}_w_in"], None, f"a{i}_in")
        p = _sconv_fwd(bcv, wts[f"a{i}_conv"], f"a{i}_conv")
        return _mm_row(p, wts[f"a{i}_w_out"], h, None, f"a{i}_out"), (h, u, bcv, p)

    def short_conv_bwd(dh, sv, i):
        h, u, bcv, p = sv
        dp = _mm_nt_row(dh, wts[f"a{i}_w_out"], f"a{i}_out_bwd")
        dbcv, dcw = _sconv_bwd(dp, bcv, wts[f"a{i}_conv"], f"a{i}_conv_bwd")
        g[f"a{i}_conv"] = dcw[:SHORT_CONV_W]
        g[f"a{i}_w_in"] = _mm_tn(u, dbcv, N_CHIPS, f"a{i}_dw_in")
        g[f"a{i}_w_out"] = _mm_tn(p, dh, 1, f"a{i}_dw_out")
        dh, g[f"ln1_{i}"] = _mm_nt_col_rms_bwd(dbcv, wts[f"a{i}_w_in"], h, wts[f"ln1_{i}"], dh, f"a{i}_in_bwd")
        return dh

    h, saved["a0"] = short_conv_fwd(h, 0)
    h, saved["f0"] = _ffn_fwd(h, wts, 0)

    h_in = h
    h, mixed = _pool_fwd(h, wts["ln1_1"], wts["b1_w_grp"], wts["b1_scale"], "b1_fwd")
    saved["b1"] = (h_in, mixed)
    h, saved["f1"] = _ffn_fwd(h, wts, 1)

    h_in = h
    u = _rms_fwd(h, wts["ln1_2"], "c2_rms")
    ag = _mm_col(u, wts["c2_w_pw1"], wts["c2_b_pw1"], "c2_pw1")
    sw, hc = _conf_mid_fwd(ag, wts["c2_dw"], wts["c2_b_dw"], wts["c2_ln_g"], wts["c2_ln_b"], "c2_mid")
    h = _mm_row(sw, wts["c2_w_pw2"], h, wts["c2_b_pw2"], "c2_pw2")
    saved["c2"] = (h_in, u, ag, sw, hc)
    h, saved["f2"] = _ffn_fwd(h, wts, 2)

    h, saved["a3"] = short_conv_fwd(h, 3)
    h, saved["f3"] = _ffn_fwd(h, wts, 3)

    loss, dh, g["ln_f"] = _loss_head(h, wts["ln_f"], target, "loss_head")

    def ffn_bwd(dh, i):
        return _ffn_bwd(dh, saved[f"f{i}"], wts, i, g)

    dh = ffn_bwd(dh, 3)
    dh = short_conv_bwd(dh, saved["a3"], 3)

    dh = ffn_bwd(dh, 2)
    h_in, u, ag, sw, hc = saved["c2"]
    dhc, g["c2_ln_g"], g["c2_ln_b"], g["c2_b_pw2"] = _conf_out_bwd(
        dh, wts["c2_w_pw2"], hc, wts["c2_ln_g"], wts["c2_ln_b"], "c2_pw2_bwd")
    g["c2_w_pw2"] = _mm_tn(sw, dh, 1, "c2_dw_pw2")
    dag, ddw, g["c2_b_dw"], g["c2_b_pw1"] = _conf_mid_bwd(dhc, ag, wts["c2_dw"], "c2_mid_bwd")
    g["c2_dw"] = ddw[:CONF_CONV_W]
    g["c2_w_pw1"] = _mm_tn(u, dag, N_CHIPS, "c2_dw_pw1")
    dh, g["ln1_2"] = _mm_nt_col_rms_bwd(dag, wts["c2_w_pw1"], h_in, wts["ln1_2"], dh, "c2_pw1_bwd")

    dh = ffn_bwd(dh, 1)
    h_in, mixed = saved["b1"]
    dmixed, g["b1_w_grp"], g["b1_scale"] = _pool_bwd_mm(dh, mixed, wts["b1_w_grp"], wts["b1_scale"], "b1_bwd_mm")
    dh, g["ln1_1"] = _pool_bwd_rms(dmixed, h_in, wts["ln1_1"], dh, "b1_bwd_rms")

    dh = ffn_bwd(dh, 0)
    dh = short_conv_bwd(dh, saved["a0"], 0)
    return loss, dh, g


MESH = pl.DeviceIdType.MESH
ANY = pl.BlockSpec(memory_space=pl.ANY)


def _position():
    return lax.axis_index("x"), lax.axis_index("y"), lax.axis_index("c")


def _other_chips(x, y):
    return [(1 - x, y), (x, 1 - y), (1 - x, 1 - y)]


def _remote(src, dst, send_sem, recv_sem, to):
    return pltpu.make_async_remote_copy(src_ref=src, dst_ref=dst, send_sem=send_sem, recv_sem=recv_sem,
                                        device_id=to, device_id_type=MESH)


def _half_rows(ref_rows, c):
    hr = ref_rows // 2
    return pl.ds(pl.multiple_of(c * hr, 16), hr)


def _allgather8(v, name):
    m_per, n = v.shape

    def body(v_ref, out_ref, send_sems, recv_sems, local_sem):
        x, y, c = _position()
        me, sibling = (x, y, c), (x, y, 1 - c)
        chips = _other_chips(x, y)

        def rows(px, py, pc):
            return out_ref.at[pl.ds((4 * px + 2 * py + pc) * m_per, m_per), :]

        def copy(k, block, to, src=None):
            return _remote(rows(*block) if src is None else src, rows(*block), send_sems.at[k], recv_sems.at[k], to)

        mine = pltpu.make_async_copy(v_ref, rows(*me), local_sem)
        mine.start()
        first = [copy(0, me, sibling, src=v_ref)]
        first += [copy(1 + j, me, (*chip, c), src=v_ref) for j, chip in enumerate(chips)]
        for cp in first:
            cp.start()
        passed = [copy(4 + j, (*chip, c), sibling) for j, chip in enumerate(chips)]
        for j, chip in enumerate(chips):
            copy(1 + j, (*chip, c), me).wait_recv()
            passed[j].start()
        copy(0, sibling, me).wait_recv()
        for j, chip in enumerate(chips):
            copy(4 + j, (*chip, 1 - c), me).wait_recv()
        for cp in first + passed:
            cp.wait_send()
        mine.wait()

    return _hosted(
        body, name=name,
        out_shape=jax.ShapeDtypeStruct((N_DEV * m_per, n), v.dtype),
        in_specs=[pl.BlockSpec(memory_space=pltpu.VMEM)],
        out_specs=pl.BlockSpec(memory_space=pltpu.VMEM),
        scratch_shapes=[pltpu.SemaphoreType.DMA((7,)), pltpu.SemaphoreType.DMA((7,)), pltpu.SemaphoreType.DMA],
        compiler_params=pltpu.CompilerParams(vmem_limit_bytes=VMEM_LIMIT),
    )(v)


def _cast_to_slot(ws, idx, name):
    r, cols = ws[0].shape
    assert all(w.shape == (r, cols) for w in ws)
    n = len(ws)
    tr = _tile(r, 256, 16)

    def body(idx_ref, *refs):
        for w_ref, o_ref in zip(refs[:n], refs[n:]):
            o_ref[...] = w_ref[...].astype(o_ref.dtype)

    return _hosted(
        body, name=name,
        grid_spec=pltpu.PrefetchScalarGridSpec(
            num_scalar_prefetch=1, grid=(r // tr,),
            in_specs=[pl.BlockSpec((tr, cols), lambda t, idx_ref: (t, 0))] * n,
            out_specs=[pl.BlockSpec((None, tr, cols), lambda t, idx_ref: (idx_ref[0], t, 0))] * n),
        out_shape=[jax.ShapeDtypeStruct((N_CHIPS, r, cols), BF16)] * n,
        compiler_params=_params("parallel"),
    )(idx, *ws)


def _dma_sems(*shape):
    return [pltpu.SemaphoreType.DMA(shape), pltpu.SemaphoreType.DMA(shape)]


def _same_shapes(arrays):
    return [jax.ShapeDtypeStruct(a.shape, a.dtype) for a in arrays]


def _part_rows(ref_rows, c, part):
    hr = ref_rows // 2
    i, n = part
    size = hr // n
    assert size * n == hr and size % 16 == 0, (ref_rows, part)
    return pl.ds(pl.multiple_of(c * hr + i * size, 16), size)


def _task_gather_ici(bufs, done, part=(0, 1)):
    n = len(bufs)

    def copies(outs, sems, landing):
        x, y, c = _position()
        my_chip = 2 * x + y
        res = []
        for i in range(n):
            rows = _part_rows(bufs[i].shape[1], c, part)
            for r, (px, py) in enumerate(_other_chips(x, y)):
                slot = (2 * px + py) if landing else my_chip
                res.append(_remote(outs[i].at[my_chip, rows, :], outs[i].at[slot, rows, :], sems[0].at[i, r], sems[1].at[i, r],
                                   (px, py, c)))
        return res

    def start(ins, outs, sems):
        for cp in copies(outs, sems, False):
            cp.start()

    def wait(ins, outs, sems):
        for cp in copies(outs, sems, True):
            cp.wait_recv()
            cp.wait_send()

    return _Task(bufs, _same_shapes(bufs), {i: i for i in range(n)}, _dma_sems(n, 3), start, wait, done)


def _task_gather_d2d(bufs, done):
    n = len(bufs)

    def copies(outs, sems, landing):
        x, y, c = _position()
        res = []
        for i in range(n):
            rows = _half_rows(bufs[i].shape[1], (1 - c) if landing else c)
            for r, (px, py) in enumerate(_other_chips(x, y)):
                part = outs[i].at[2 * px + py, rows, :]
                res.append(_remote(part, part, sems[0].at[i, r], sems[1].at[i, r], (x, y, 1 - c)))
        return res

    def start(ins, outs, sems):
        for cp in copies(outs, sems, False):
            cp.start()

    def wait(ins, outs, sems):
        for cp in copies(outs, sems, True):
            cp.wait_recv()
        for cp in copies(outs, sems, False):
            cp.wait_send()

    return _Task(bufs, _same_shapes(bufs), {i: i for i in range(n)}, _dma_sems(n, 3), start, wait, done)


def _task_sibling_halves(grads, done):
    n = len(grads)

    def copies(ins, outs, sems):
        x, y, c = _position()
        return [_remote(ins[i].at[:, _half_rows(grads[i].shape[1], 1 - c), :], outs[i], sems[0].at[i], sems[1].at[i],
                        (x, y, 1 - c)) for i in range(n)]

    def start(ins, outs, sems):
        for cp in copies(ins, outs, sems):
            cp.start()

    def wait(ins, outs, sems):
        for cp in copies(ins, outs, sems):
            cp.wait()

    shapes = [jax.ShapeDtypeStruct((g.shape[0], g.shape[1] // 2, g.shape[2]), g.dtype) for g in grads]
    return _Task(grads, shapes, {}, _dma_sems(n), start, wait, done)


def _task_chip_sums(parts, done, landed=None, part=(0, 1)):
    n = len(parts)
    i_part, n_parts = part
    sizes = [p.shape[1] // n_parts for p in parts]
    assert all(p.shape[1] == size * n_parts and size % 16 == 0 for p, size in zip(parts, sizes)), part
    rows = [pl.ds(i_part * size, size) for size in sizes]

    def copies(ins, outs, sems):
        x, y, c = _position()
        return [_remote(ins[i].at[2 * px + py, rows[i], :], outs[i].at[r, rows[i], :], sems[0].at[i, r], sems[1].at[i, r],
                        (px, py, c))
                for i in range(n) for r, (px, py) in enumerate(_other_chips(x, y))]

    def start(ins, outs, sems):
        for cp in copies(ins, outs, sems):
            cp.start()

    def wait(ins, outs, sems):
        for cp in copies(ins, outs, sems):
            cp.wait()

    shapes = [jax.ShapeDtypeStruct((3,) + p.shape[1:], p.dtype) for p in parts]
    if landed is None:
        return _Task(parts, shapes, {}, _dma_sems(n, 3), start, wait, done)
    return _Task(list(parts) + list(landed), shapes, {n + i: i for i in range(n)}, _dma_sems(n, 3), start, wait, done)


def _task_sibling_parts(owns, landeds, done):
    n = len(owns)

    def copies(ins, outs, sems):
        x, y, c = _position()
        sibling = (x, y, 1 - c)
        res = []
        for i in range(n):
            res.append(_remote(ins[i].at[2 * x + y], outs[i].at[0], sems[0].at[i, 0], sems[1].at[i, 0], sibling))
            res.append(_remote(ins[n + i], outs[i].at[pl.ds(1, 3)], sems[0].at[i, 1], sems[1].at[i, 1], sibling))
        return res

    def start(ins, outs, sems):
        for cp in copies(ins, outs, sems):
            cp.start()

    def wait(ins, outs, sems):
        for cp in copies(ins, outs, sems):
            cp.wait()

    return _Task(list(owns) + list(landeds), _same_shapes(owns), {}, _dma_sems(n, 2), start, wait, done)


def _add_halves(grad, sib, c, name):
    nsh, r, cols = grad.shape
    hr = r // 2
    tr = _tile(hr, 512, 16)
    nt = hr // tr

    def body(c_ref, g_ref, s_ref, o_ref):
        o_ref[...] = (g_ref[...].astype(F32) + s_ref[...].astype(F32)).astype(o_ref.dtype)

    return _hosted(
        body, name=name,
        grid_spec=pltpu.PrefetchScalarGridSpec(
            num_scalar_prefetch=1, grid=(nsh, nt),
            in_specs=[pl.BlockSpec((None, tr, cols), lambda j, t, c_ref: (j, c_ref[1] * nt + t, 0)),
                      pl.BlockSpec((None, tr, cols), lambda j, t, c_ref: (j, t, 0))],
            out_specs=pl.BlockSpec((None, tr, cols), lambda j, t, c_ref: (j, t, 0))),
        out_shape=jax.ShapeDtypeStruct((nsh, hr, cols), BF16),
        compiler_params=_params("parallel", "parallel"),
    )(c, grad, sib)


def _adamw_reduced(w, own, landed, sib, m, v, idx, name):
    r, cols = w.shape
    hr = r // 2
    tr = _tile(hr, 256, 16)
    nt = hr // tr

    def body(idx_ref, w_ref, p_ref, l_ref, s_ref, m_ref, v_ref, go_ref, d_ref, mo_ref, vo_ref):
        mine = p_ref[...].astype(F32)
        for k in range(3):
            mine = mine + l_ref[k].astype(F32)
        theirs = s_ref[0].astype(F32)
        for k in range(1, 4):
            theirs = theirs + s_ref[k].astype(F32)
        grad = jnp.where(pl.program_id(0) // nt == idx_ref[1], mine, theirs)
        go_ref[...] = grad
        d_ref[...], mo_ref[...], vo_ref[...] = _adamw_update(w_ref[...], grad, m_ref[...], v_ref[...])

    def in_half(t, half):
        return jnp.clip(t - half * nt, 0, nt - 1)

    full = pl.BlockSpec((tr, cols), lambda t, idx_ref: (t, 0))
    return _hosted(
        body, name=name,
        grid_spec=pltpu.PrefetchScalarGridSpec(
            num_scalar_prefetch=1, grid=(2 * nt,),
            in_specs=[full,
                      pl.BlockSpec((None, tr, cols), lambda t, idx_ref: (idx_ref[0], in_half(t, idx_ref[1]), 0)),
                      pl.BlockSpec((3, tr, cols), lambda t, idx_ref: (0, in_half(t, idx_ref[1]), 0)),
                      pl.BlockSpec((4, tr, cols), lambda t, idx_ref: (0, in_half(t, 1 - idx_ref[1]), 0)),
                      full, full],
            out_specs=[full] * 4),
        out_shape=[jax.ShapeDtypeStruct((r, cols), F32)] * 4,
        compiler_params=_params("arbitrary"),
    )(idx, w, own, landed, sib, m, v)


def _sum_devices(blocks, name):
    m8, n = blocks.shape
    m = m8 // N_DEV

    def body(b_ref, o_ref):
        acc = b_ref[pl.ds(0, m), :]
        for k in range(1, N_DEV):
            acc = acc + b_ref[pl.ds(k * m, m), :]
        o_ref[...] = acc

    return _hosted(
        body, name=name, out_shape=jax.ShapeDtypeStruct((m, n), F32),
        in_specs=[pl.BlockSpec(memory_space=pltpu.VMEM)], out_specs=pl.BlockSpec(memory_space=pltpu.VMEM),
        compiler_params=pltpu.CompilerParams(vmem_limit_bytes=VMEM_LIMIT),
    )(blocks)


def _adamw_update(w, grad, m, v):
    new_m = ADAM_B1 * m + (1.0 - ADAM_B1) * grad
    new_v = ADAM_B2 * v + (1.0 - ADAM_B2) * (grad * grad)
    m_hat = new_m * (1.0 / (1.0 - ADAM_B1 ** ADAM_STEP))
    v_hat = new_v * (1.0 / (1.0 - ADAM_B2 ** ADAM_STEP))
    return -ADAM_LR * (m_hat / (jnp.sqrt(v_hat) + ADAM_EPS) + ADAM_WD * w), new_m, new_v


def _adamw(w, g, m, v, name):
    r, cols = w.shape
    tr = _tile(r, 256)

    def body(w_ref, g_ref, m_ref, v_ref, go_ref, d_ref, mo_ref, vo_ref):
        grad = g_ref[...]
        go_ref[...] = grad
        d_ref[...], mo_ref[...], vo_ref[...] = _adamw_update(w_ref[...], grad, m_ref[...], v_ref[...])

    spec = pl.BlockSpec((tr, cols), lambda t: (t, 0))
    return _hosted(
        body, name=name, grid=(r // tr,), in_specs=[spec] * 4, out_specs=[spec] * 4,
        out_shape=[jax.ShapeDtypeStruct((r, cols), F32)] * 4,
        compiler_params=_params("parallel"),
    )(w, g, m, v)


def _adamw_small(grad_blocks, params, name):
    nb, npar = len(grad_blocks), len(params)

    def body(*refs):
        blocks, ins, outs = refs[:nb], refs[nb:nb + 3 * npar], refs[nb + 3 * npar:]
        for p, (w, _, _, blk, row0) in enumerate(params):
            if w.ndim == 1:
                tiled = (w.shape[0] // LANES, LANES)
                grad = blocks[blk][pl.ds(row0, tiled[0]), pl.ds(0, LANES)]
                wmv = [ins[3 * p + k][...].reshape(tiled) for k in range(3)]
            else:
                grad = blocks[blk][pl.ds(row0, w.shape[0]), :]
                wmv = [ins[3 * p + k][...] for k in range(3)]
            for k, res in enumerate((grad,) + _adamw_update(wmv[0], grad, wmv[1], wmv[2])):
                outs[4 * p + k][...] = res.reshape(w.shape)

    args = list(grad_blocks) + [a for w, m, v, _, _ in params for a in (w, m, v)]
    vmem = pl.BlockSpec(memory_space=pltpu.VMEM)
    out = _hosted(
        body, name=name, in_specs=[vmem] * len(args), out_specs=[vmem] * (4 * npar),
        out_shape=[jax.ShapeDtypeStruct(w.shape, F32) for w, _, _, _, _ in params for _ in range(4)],
    )(*args)
    return [tuple(out[4 * p:4 * p + 4]) for p in range(npar)]


WEIGHT_NAMES = (
    "ln1_0", "a0_w_in", "a0_conv", "a0_w_out", "ln2_0", "ffn0_w_gu", "ffn0_w_down",
    "ln1_1", "b1_w_grp", "b1_scale", "ln2_1", "ffn1_w_gu", "ffn1_w_down",
    "ln1_2", "c2_w_pw1", "c2_b_pw1", "c2_dw", "c2_b_dw", "c2_ln_g", "c2_ln_b", "c2_w_pw2", "c2_b_pw2",
    "ln2_2", "ffn2_w_gu", "ffn2_w_down",
    "ln1_3", "a3_w_in", "a3_conv", "a3_w_out", "ln2_3", "ffn3_w_gu", "ffn3_w_down", "ln_f")
BIG = ("a0_w_in", "a0_w_out", "ffn0_w_gu", "ffn0_w_down", "b1_w_grp", "ffn1_w_gu", "ffn1_w_down", "c2_w_pw1", "c2_w_pw2",
       "ffn2_w_gu", "ffn2_w_down", "a3_w_in", "a3_w_out", "ffn3_w_gu", "ffn3_w_down")
GROUPED = "b1_w_grp"
SMALL_SHARDED = ("a0_conv", "a3_conv", "c2_dw")
REPLICATED = tuple(n for n in WEIGHT_NAMES if n not in BIG and n not in SMALL_SHARDED)


def _pad_rows(a, mult=8):
    pad = -a.shape[0] % mult
    return a if pad == 0 else jnp.concatenate([a, jnp.zeros((pad, a.shape[1]), a.dtype)], axis=0)


def _pack_rows(parts, width):
    rows = [p.reshape(-1, width) for p in parts]
    return _pad_rows(jnp.concatenate(rows, axis=0)), [r.shape[0] for r in rows]


def _unpack_rows(packed, counts, shapes):
    out, at = [], 0
    for n, shp in zip(counts, shapes):
        out.append(packed[at:at + n].reshape(shp))
        at += n
    return out


COLUMN_SHARDED = ("w_in", "w_gu", "w_pw1")


class _Weights(dict):
    def __init__(self, bufs):
        super().__init__()
        self.bufs = bufs

    def __missing__(self, name):
        buf = self.bufs[name]
        if name == GROUPED:
            cg = buf.shape[-1]
            rq = cg // N_CHIPS
            return jnp.transpose(buf.reshape(N_CHIPS, -1, rq, cg), (1, 0, 2, 3)).reshape(-1, cg, cg)
        return buf if name.endswith(COLUMN_SHARDED) else buf.reshape(-1, buf.shape[-1])


class _Exchange:
    def __init__(self, w, mom, vel, idx):
        def shards(table):
            return {n: table[n].reshape(-1, table[n].shape[-1]) for n in BIG}

        self.w, self.mom, self.vel, self.idx = shards(w), shards(mom), shards(vel), idx
        self.bufs = {}
        self.weights = _Weights(self.bufs)
        self.grads = {}
        self.sib, self.part, self.landed, self.sib_parts, self.updates = {}, {}, {}, {}, {}

    def cast(self, names):
        by_shape = {}
        for n in names:
            by_shape.setdefault(self.w[n].shape, []).append(n)
        for group in by_shape.values():
            self.bufs.update(zip(group, _cast_to_slot([self.w[n] for n in group], self.idx, f"cast_{group[0]}")))

    @staticmethod
    def _store(table, names):
        def done(arrays):
            table.update(zip(names, arrays))
        return done

    def _grad(self, n):
        g = self.grads[n]
        if n == GROUPED:
            ng, cg, _ = g.shape
            g = jnp.transpose(g.reshape(ng, N_CHIPS, cg // N_CHIPS, cg), (1, 0, 2, 3)).astype(BF16)
        return g.reshape(N_CHIPS, -1, g.shape[-1])

    def gather_ici(self, *names, part=(0, 1)):
        return lambda: _task_gather_ici([self.bufs[n] for n in names], self._store(self.bufs, names), part)

    def gather_d2d(self, *names):
        return lambda: _task_gather_d2d([self.bufs[n] for n in names], self._store(self.bufs, names))

    def sibling_halves(self, *names):
        return lambda: _task_sibling_halves([self._grad(n) for n in names], self._store(self.sib, names))

    def add_halves(self, *names):
        def run():
            for n in names:
                self.part[n] = _add_halves(self._grad(n), self.sib.pop(n), self.idx, f"reduce_add_{n}")
        return run

    def chip_sums(self, *names, part=(0, 1)):
        def make():
            landed = [self.landed[n] for n in names] if part[0] > 0 else None
            return _task_chip_sums([self.part[n] for n in names], self._store(self.landed, names), landed, part)
        return make

    def sibling_parts(self, *names):
        return lambda: _task_sibling_parts([self.part[n] for n in names], [self.landed[n] for n in names],
                                           self._store(self.sib_parts, names))

    def adamw(self, *names):
        def run():
            for n in names:
                self.updates[n] = _adamw_reduced(self.w[n], self.part.pop(n), self.landed.pop(n), self.sib_parts.pop(n),
                                                 self.mom[n], self.vel[n], self.idx, f"adamw_{n}")
        return run


def _plan(ex):
    s = _Schedule()

    def ffn(i):
        return f"ffn{i}_w_gu", f"ffn{i}_w_down"

    c2, a3 = ("c2_w_pw1", "c2_w_pw2"), ("a3_w_in", "a3_w_out")
    first, second = (0, 2), (1, 2)
    s.host("cast_ffn0_w_gu", ex.gather_ici("a0_w_in", part=first))
    s.host("cast_ffn0_w_down", ex.gather_ici("a0_w_in", part=second))
    s.host("cast_c2_w_pw1", ex.gather_d2d("a0_w_in"))
    gu, down = ffn(0)
    s.host("a0_rms", ex.gather_ici("a0_w_out"))
    s.host("a0_in", ex.gather_ici(gu, part=first), ex.gather_d2d("a0_w_out"))
    s.host("a0_conv", ex.gather_ici(gu, part=second))
    s.host("a0_out", ex.gather_ici(down), ex.gather_d2d(gu))
    s.host("ffn0_up", ex.gather_d2d(down))
    gu, down = ffn(1)
    s.host("ffn0_up", ex.gather_ici(gu, GROUPED))
    s.host("ffn0_down", ex.gather_ici(down), ex.gather_d2d(gu, GROUPED))
    s.host("ffn1_up", ex.gather_d2d(down), ex.gather_ici(*c2))
    s.host("ffn1_down", ex.gather_d2d(*c2))
    gu, down = ffn(2)
    s.host("c2_pw1", ex.gather_ici(down))
    s.host("c2_mid", ex.gather_ici(gu))
    s.host("c2_pw2", ex.gather_d2d(gu, down))
    gu, down = ffn(3)
    s.host("ffn2_up", ex.gather_ici(*a3))
    s.host("ffn2_down", ex.gather_d2d(*a3), ex.gather_ici(down))
    s.host("a3_in", ex.gather_ici(gu, part=first))
    s.host("a3_conv", ex.gather_ici(gu, part=second))
    s.host("a3_out", ex.gather_d2d(gu, down))

    def reduce_on(names, first_host, ici_hosts, last_host):
        s.host(first_host, ex.sibling_halves(*names))
        s.post(first_host, ex.add_halves(*names))
        for host, hosted, part in ici_hosts:
            s.host(host, ex.chip_sums(*hosted, part=part))
        s.host(last_host, ex.sibling_parts(*names))
        s.post(last_host, ex.adamw(*names))

    whole = (0, 1)
    gu, down = ffn(3)
    reduce_on((gu, down), "a3_out_bwd",
              [("a3_conv_bwd", (down,), whole), ("a3_dw_in", (gu,), first), ("a3_in_bwd", (gu,), second)], "ffn2_down_bwd")
    reduce_on(a3, "ffn2_down_bwd", [("ffn2_dw_down", a3[:1], whole), ("ffn2_dw_gu", a3[1:], whole)], "c2_pw2_bwd")
    gu, down = ffn(0)
    s.host("ffn0_dw_gu", ex.sibling_halves(down))
    s.post("ffn0_dw_gu", ex.add_halves(down))
    s.host("ffn0_up_bwd", ex.chip_sums(down))
    s.host("a0_out_bwd", ex.sibling_halves(gu, GROUPED))
    s.post("a0_out_bwd", ex.add_halves(gu, GROUPED))
    s.host("a0_conv_bwd", ex.chip_sums(gu, part=first), ex.chip_sums(GROUPED))
    s.host("a0_dw_in", ex.chip_sums(gu, part=second))
    s.host("a0_dw_out", ex.sibling_halves("a0_w_in"))
    s.post("a0_dw_out", ex.add_halves("a0_w_in"))
    s.host("a0_in_bwd", ex.chip_sums("a0_w_in"))
    s.host(f"adamw_{gu}", ex.chip_sums("a0_w_out"))
    s.host(f"adamw_{down}", ex.sibling_parts("a0_w_out"))
    reduce_on(ffn(2), "c2_pw2_bwd", [("c2_mid_bwd", ffn(2), whole)], "ffn1_down_bwd")
    reduce_on(c2, "ffn1_down_bwd", [("ffn1_dw_down", c2, whole)], "b1_bwd_mm")
    gu, down = ffn(1)
    reduce_on((gu, down), "b1_bwd_mm",
              [("b1_bwd_rms", (down,), whole), ("ffn0_down_bwd", (gu,), first), ("ffn0_dw_down", (gu,), second)], "ffn0_dw_gu")
    return s


def kernel(x, *rest):
    nw = len(WEIGHT_NAMES)
    w = dict(zip(WEIGHT_NAMES, rest[:nw]))
    target = rest[nw]
    mom = dict(zip(WEIGHT_NAMES, rest[nw + 1:2 * nw + 1]))
    vel = dict(zip(WEIGHT_NAMES, rest[2 * nw + 1:3 * nw + 1]))
    cx, cy, cc = _position()
    my_chip = 2 * cx + cy
    ex = _Exchange(w, mom, vel, jnp.stack([my_chip, cc]).astype(jnp.int32))
    _ACTIVE_SCHEDULE[0] = _plan(ex)
    try:
        return _scheduled_step(x, target, w, mom, vel, ex, my_chip)
    finally:
        _ACTIVE_SCHEDULE[0] = None


def _scheduled_step(x, target, w, mom, vel, ex, my_chip):
    d = x.shape[-1]
    cq = d // N_CHIPS
    ex.cast(BIG)

    small_blk, small_counts = _pack_rows([w[n] for n in SMALL_SHARDED], cq)
    small_all = _allgather8(small_blk, "gather_small").reshape(N_CHIPS, 2, small_blk.shape[0], cq)[:, 0]
    small_parts = _unpack_rows(jnp.transpose(small_all, (1, 0, 2)), small_counts,
                               [(w[n].reshape(-1, cq).shape[0], N_CHIPS, cq) for n in SMALL_SHARDED])
    wts = ex.weights
    for n in REPLICATED:
        wts[n] = w[n].reshape(1, -1)
    for n, part in zip(SMALL_SHARDED, small_parts):
        wts[n] = part.reshape(part.shape[0], d)

    loss, dx, g = _device_step(x[0], target[0], wts, ex.grads)

    summed, last = ("ffn0_w_gu", "ffn0_w_down", GROUPED, "a0_w_in"), "a0_w_out"
    _comm_only([ex.sibling_parts(*summed)(), ex.sibling_halves(last)()], "reduce_tail_d2d")
    ex.add_halves(last)()
    ex.adamw(*summed)()
    ex.adamw(last)()
    sched = _ACTIVE_SCHEDULE[0]
    assert not sched.hosts and not sched.posts, (sched.hosts, sched.posts)

    rep_rows = [jnp.pad(g[n].reshape(-1, LANES), ((0, 0), (0, cq - LANES))) for n in REPLICATED]
    by_chip = [jnp.transpose(g[n].reshape(g[n].shape[0], N_CHIPS, cq), (1, 0, 2)) for n in SMALL_SHARDED]
    shard_rows = jnp.concatenate(by_chip, axis=1)
    n_rep, n_shard = sum(r.shape[0] for r in rep_rows), shard_rows.shape[1]
    loss_row = jnp.broadcast_to(loss, (1, cq))
    sm_blk = _pad_rows(jnp.concatenate(rep_rows + [loss_row, shard_rows.reshape(N_CHIPS * n_shard, cq)], axis=0))
    sm_sum = _sum_devices(_allgather8(sm_blk, "gather_small_grads"), "sum_small_grads")
    mine = lax.dynamic_slice_in_dim(sm_sum, n_rep + 1 + my_chip * n_shard, n_shard, axis=0)

    out = ex.updates
    params, at = [], {0: 0, 1: 0}
    for block, names in ((0, REPLICATED), (1, SMALL_SHARDED)):
        for n in names:
            params.append((w[n], mom[n], vel[n], block, at[block]))
            at[block] += w[n].size // LANES if w[n].ndim == 1 else w[n].shape[0]
    out.update(zip(REPLICATED + SMALL_SHARDED, _adamw_small([sm_sum, mine], params, "adamw_small")))

    total = sm_sum[n_rep, 0]
    grads, deltas, new_m, new_v = ([out[n][k].reshape(w[n].shape) for n in WEIGHT_NAMES] for k in range(4))
    return (total, dx.reshape(x.shape), *grads, *deltas, *new_m, *new_v)
```

```python
import functools

import jax
import jax.numpy as jnp
from jax import lax
from jax.experimental import pallas as pl
from jax.experimental.pallas import tpu as pltpu

F32 = jnp.float32
BF16 = jnp.bfloat16

RMS_EPS = 1e-6
LN_EPS = 1e-5
POOL_WINDOWS = (2, 4, 8, 16)
SHORT_CONV_W = 3
CONF_CONV_W = 31
N_CHIPS = 4
N_DEV = 8

ADAM_LR = 0.001
ADAM_B1 = 0.9
ADAM_B2 = 0.999
ADAM_EPS = 1e-08
ADAM_WD = 0.01
ADAM_STEP = 10

V7X_VMEM_BYTES = 64 * 1024 * 1024
VMEM_LIMIT = V7X_VMEM_BYTES - 8 * 1024 * 1024
LANES = 128
POOL_HALO = 16
SCONV_HALO = 16
CONF_HALO = 32


def _params(*sem):
    return pltpu.CompilerParams(dimension_semantics=sem, vmem_limit_bytes=VMEM_LIMIT)


def _tile(n, pref, mult=8):
    t = min(n, pref)
    while t > mult and (n % t or t % mult):
        t -= mult
    assert n % t == 0 and t % mult == 0, (n, pref, mult)
    return t


def _sigmoid(x):
    return jax.nn.sigmoid(x)


def _dot(a, b):
    return jnp.dot(a, b, preferred_element_type=F32)


def _dot_nt(a, b):
    return lax.dot_general(a, b, (((1,), (1,)), ((), ())), preferred_element_type=F32)


def _dot_tn(a, b):
    return lax.dot_general(a, b, (((0,), (0,)), ((), ())), preferred_element_type=F32)


def _colsum(x):
    return jnp.sum(x, axis=0, keepdims=True)


def _rms_stats(x):
    return lax.rsqrt(jnp.mean(x * x, axis=-1, keepdims=True) + RMS_EPS)


def _rms_bwd(du, x, gain):
    r = _rms_stats(x)
    xhat = x * r
    gdy = du * gain
    dx = r * (gdy - xhat * jnp.mean(gdy * xhat, axis=-1, keepdims=True))
    return dx, _colsum(du * xhat)


class _Task:
    def __init__(self, ins, out_shapes, aliases, sems, start, wait, done):
        self.ins, self.out_shapes, self.aliases, self.sems = list(ins), list(out_shapes), dict(aliases), list(sems)
        self.start, self.wait, self.done = start, wait, done


class _Schedule:
    def __init__(self):
        self.hosts, self.posts = {}, {}

    def host(self, kernel_name, *make_tasks):
        self.hosts.setdefault(kernel_name, []).extend(make_tasks)

    def post(self, kernel_name, *thunks):
        self.posts.setdefault(kernel_name, []).extend(thunks)

    def tasks_for(self, kernel_name):
        return [make() for make in self.hosts.pop(kernel_name, ())]

    def finished(self, kernel_name):
        for thunk in self.posts.pop(kernel_name, ()):
            thunk()


_ACTIVE_SCHEDULE = [None]


def _hosted(body, name, **kw):
    def run(*args):
        sched = _ACTIVE_SCHEDULE[0]
        tasks = sched.tasks_for(name) if sched is not None else []
        out = _call_with_tasks(body, name, tasks, kw, args) if tasks else pl.pallas_call(body, name=name, **kw)(*args)
        if sched is not None:
            sched.finished(name)
        return out

    return run


def _call_with_tasks(body, name, tasks, kw, args):
    spec = kw.get("grid_spec")
    n_pre = spec.num_scalar_prefetch if spec is not None else 0
    src = dict(grid=spec.grid, in_specs=spec.in_specs, out_specs=spec.out_specs) if spec is not None else kw
    pre, args = args[:n_pre], args[n_pre:]
    grid = tuple(src.get("grid", ()))
    single = not isinstance(kw["out_shape"], (list, tuple))
    out_shape = [kw["out_shape"]] if single else list(kw["out_shape"])
    out_specs = [src["out_specs"]] if single else list(src["out_specs"])
    scratch = list(kw.get("scratch_shapes", ()))
    n_in, n_out, n_scr = len(args), len(out_shape), len(scratch)
    t_in = [a for t in tasks for a in t.ins]
    t_out = [o for t in tasks for o in t.out_shapes]
    t_sem = [s for t in tasks for s in t.sems]
    aliases, at_in, at_out = {}, n_pre + n_in, n_out
    for t in tasks:
        for i, o in t.aliases.items():
            aliases[at_in + i] = at_out + o
        at_in += len(t.ins)
        at_out += len(t.out_shapes)

    def wrapped(*refs):
        pre_refs, refs = refs[:n_pre], refs[n_pre:]
        a = n_in
        b = a + len(t_in)
        c = b + n_out
        d = c + len(t_out)
        e = d + n_scr
        ins, tins, outs, touts, scr, tsems = refs[:a], refs[a:b], refs[b:c], refs[c:d], refs[d:e], refs[e:]
        views, i0, o0, s0 = [], 0, 0, 0
        for t in tasks:
            views.append((tins[i0:i0 + len(t.ins)], touts[o0:o0 + len(t.out_shapes)], tsems[s0:s0 + len(t.sems)]))
            i0, o0, s0 = i0 + len(t.ins), o0 + len(t.out_shapes), s0 + len(t.sems)

        def start_all():
            for t, v in zip(tasks, views):
                t.start(*v)

        def wait_all():
            for t, v in zip(tasks, views):
                t.wait(*v)

        if grid:
            first = functools.reduce(jnp.logical_and, [pl.program_id(i) == 0 for i in range(len(grid))])
            last = functools.reduce(jnp.logical_and, [pl.program_id(i) == grid[i] - 1 for i in range(len(grid))])
            pl.when(first)(start_all)
            body(*pre_refs, *ins, *outs, *scr)
            pl.when(last)(wait_all)
        else:
            start_all()
            body(*pre_refs, *ins, *outs, *scr)
            wait_all()

    in_specs = list(src["in_specs"]) + [ANY] * len(t_in)
    out_specs = out_specs + [ANY] * len(t_out)
    if spec is not None:
        layout = dict(grid_spec=pltpu.PrefetchScalarGridSpec(
            num_scalar_prefetch=n_pre, grid=grid, in_specs=in_specs, out_specs=out_specs, scratch_shapes=scratch + t_sem))
    else:
        layout = dict(grid=grid, in_specs=in_specs, out_specs=out_specs, scratch_shapes=scratch + t_sem)
    res = pl.pallas_call(
        wrapped, name=name, out_shape=out_shape + t_out, input_output_aliases=aliases,
        compiler_params=pltpu.CompilerParams(dimension_semantics=("arbitrary",) * len(grid), vmem_limit_bytes=VMEM_LIMIT),
        **layout,
    )(*pre, *args, *t_in)
    res = list(res)
    own, rest = res[:n_out], res[n_out:]
    for t in tasks:
        t.done(rest[:len(t.out_shapes)])
        rest = rest[len(t.out_shapes):]
    return own[0] if single else own


def _comm_only(tasks, name):
    _call_with_tasks(lambda: None, name, tasks, dict(grid=(), in_specs=[], out_specs=[], out_shape=[]), ())


def _rms_fwd(h, gain, name):
    s, d = h.shape
    tm = _tile(s, 512)

    def body(h_ref, g_ref, u_ref):
        x = h_ref[...]
        u_ref[...] = (x * _rms_stats(x) * g_ref[...]).astype(u_ref.dtype)

    return _hosted(
        body, name=name, grid=(s // tm,),
        in_specs=[pl.BlockSpec((tm, d), lambda m: (m, 0)), pl.BlockSpec((1, d), lambda m: (0, 0))],
        out_specs=pl.BlockSpec((tm, d), lambda m: (m, 0)),
        out_shape=jax.ShapeDtypeStruct((s, d), BF16),
        compiler_params=_params("parallel"),
    )(h, gain)


def _mm_col(a, w, bias, name):
    s, k = a.shape
    nsh, _, ns = w.shape
    tm = _tile(s, 512)
    has_bias = bias is not None

    def body(a_ref, w_ref, *rest):
        o_ref = rest[-1]
        x = a_ref[...]
        for j in range(nsh):
            cols = pl.ds(j * ns, ns)
            acc = _dot(x, w_ref[j])
            if has_bias:
                acc = acc + rest[0][:, cols]
            o_ref[:, cols] = acc.astype(o_ref.dtype)

    in_specs = [pl.BlockSpec((tm, k), lambda m: (m, 0)), pl.BlockSpec((nsh, k, ns), lambda m: (0, 0, 0))]
    args = [a, w]
    if has_bias:
        in_specs.append(pl.BlockSpec((1, nsh * ns), lambda m: (0, 0)))
        args.append(bias)
    return _hosted(
        body, name=name, grid=(s // tm,), in_specs=in_specs,
        out_specs=pl.BlockSpec((tm, nsh * ns), lambda m: (m, 0)),
        out_shape=jax.ShapeDtypeStruct((s, nsh * ns), BF16),
        compiler_params=_params("parallel"),
    )(*args)


def _mm_row(a, w, res, bias, name):
    s = a.shape[0]
    k, n = w.shape
    tm = _tile(s, 512)
    has_bias = bias is not None

    def body(a_ref, w_ref, res_ref, *rest):
        o_ref = rest[-1]
        y = res_ref[...] + _dot(a_ref[...], w_ref[...])
        if has_bias:
            y = y + rest[0][...]
        o_ref[...] = y

    in_specs = [pl.BlockSpec((tm, k), lambda m: (m, 0)), pl.BlockSpec((k, n), lambda m: (0, 0)),
                pl.BlockSpec((tm, n), lambda m: (m, 0))]
    args = [a, w, res]
    if has_bias:
        in_specs.append(pl.BlockSpec((1, n), lambda m: (0, 0)))
        args.append(bias)
    return _hosted(
        body, name=name, grid=(s // tm,), in_specs=in_specs,
        out_specs=pl.BlockSpec((tm, n), lambda m: (m, 0)),
        out_shape=jax.ShapeDtypeStruct((s, n), F32),
        compiler_params=_params("parallel"),
    )(*args)


def _mm_nt_row(dy, w, name):
    s, n = dy.shape
    k = w.shape[0]
    tm = _tile(s, 512)

    def body(dy_ref, w_ref, o_ref):
        o_ref[...] = _dot_nt(dy_ref[...].astype(BF16), w_ref[...])

    return _hosted(
        body, name=name, grid=(s // tm,),
        in_specs=[pl.BlockSpec((tm, n), lambda m: (m, 0)), pl.BlockSpec((k, n), lambda m: (0, 0))],
        out_specs=pl.BlockSpec((tm, k), lambda m: (m, 0)),
        out_shape=jax.ShapeDtypeStruct((s, k), F32),
        compiler_params=_params("parallel"),
    )(dy, w)


def _ffn_up(u, w, name):
    s, d = u.shape
    _, _, ns = w.shape
    tm = _tile(s, 512)

    def body(u_ref, wg_ref, wu_ref, act_ref, s1_ref, q1_ref):
        x = u_ref[...]
        g = _dot(x, wg_ref[...])
        up = _dot(x, wu_ref[...])
        sg = _sigmoid(g)
        s1 = g * sg
        act_ref[...] = (s1 * up).astype(act_ref.dtype)
        s1_ref[...] = s1.astype(s1_ref.dtype)
        q1_ref[...] = (up * sg * (1.0 + g * (1.0 - sg))).astype(q1_ref.dtype)

    out = pl.BlockSpec((tm, ns), lambda j, m: (m, j))
    return _hosted(
        body, name=name, grid=(2, s // tm),
        in_specs=[pl.BlockSpec((tm, d), lambda j, m: (m, 0)), pl.BlockSpec((None, d, ns), lambda j, m: (j, 0, 0)),
                  pl.BlockSpec((None, d, ns), lambda j, m: (j + 2, 0, 0))],
        out_specs=[out, out, out],
        out_shape=[jax.ShapeDtypeStruct((s, 2 * ns), BF16)] * 3,
        compiler_params=_params("parallel", "parallel"),
    )(u, w, w)


def _ffn_down_bwd(dh, w, s1, q1, name):
    s, d = dh.shape
    f = w.shape[0]
    tm = _tile(s, 256)

    def body(dh_ref, w_ref, s1_ref, q1_ref, o_ref):
        da = _dot_nt(dh_ref[...].astype(BF16), w_ref[...])
        o_ref[:, :f] = (da * q1_ref[...].astype(F32)).astype(o_ref.dtype)
        o_ref[:, f:] = (da * s1_ref[...].astype(F32)).astype(o_ref.dtype)

    return _hosted(
        body, name=name, grid=(s // tm,),
        in_specs=[pl.BlockSpec((tm, d), lambda m: (m, 0)), pl.BlockSpec((f, d), lambda m: (0, 0)),
                  pl.BlockSpec((tm, f), lambda m: (m, 0)), pl.BlockSpec((tm, f), lambda m: (m, 0))],
        out_specs=pl.BlockSpec((tm, 2 * f), lambda m: (m, 0)),
        out_shape=jax.ShapeDtypeStruct((s, 2 * f), BF16),
        compiler_params=_params("parallel"),
    )(dh, w, s1, q1)


def _mm_nt_col_rms_bwd(dy, w, h, gain, dh, name):
    s = dy.shape[0]
    nsh, k, ns = w.shape
    tm = _tile(s, 256)

    def body(dy_ref, w_ref, h_ref, g_ref, dh_ref, o_ref, dg_ref):
        du = _dot_nt(dy_ref[:, :ns], w_ref[0])
        for j in range(1, nsh):
            du = du + _dot_nt(dy_ref[:, j * ns:(j + 1) * ns], w_ref[j])
        dx, dg = _rms_bwd(du, h_ref[...], g_ref[...])
        o_ref[...] = dh_ref[...] + dx
        _accumulate(dg_ref, dg, pl.program_id(0) == 0)

    return _hosted(
        body, name=name, grid=(s // tm,),
        in_specs=[pl.BlockSpec((tm, nsh * ns), lambda m: (m, 0)), pl.BlockSpec((nsh, k, ns), lambda m: (0, 0, 0)),
                  pl.BlockSpec((tm, k), lambda m: (m, 0)), pl.BlockSpec((1, k), lambda m: (0, 0)),
                  pl.BlockSpec((tm, k), lambda m: (m, 0))],
        out_specs=[pl.BlockSpec((tm, k), lambda m: (m, 0)), pl.BlockSpec((1, k), lambda m: (0, 0))],
        out_shape=[jax.ShapeDtypeStruct((s, k), F32), jax.ShapeDtypeStruct((1, k), F32)],
        compiler_params=_params("arbitrary"),
    )(dy, w, h, gain, dh)


def _mm_tn(a, dy, nsh, name):
    s, k = a.shape
    ns = dy.shape[1] // nsh
    tm = _tile(s, 1024)
    tk = _tile(k, 1408, LANES)
    nk, nm = k // tk, s // tm

    def body(a_ref, dy_ref, o_ref, acc_ref):
        m = pl.program_id(2)
        part = _dot_tn(a_ref[...], dy_ref[...].astype(BF16))

        @pl.when(m == 0)
        def _():
            acc_ref[...] = part

        @pl.when(m > 0)
        def _():
            acc_ref[...] += part

        @pl.when(m == nm - 1)
        def _():
            o_ref[...] = acc_ref[...].astype(o_ref.dtype)

    return _hosted(
        body, name=name, grid=(nsh, nk, nm),
        in_specs=[pl.BlockSpec((tm, tk), lambda j, kk, m: (m, kk)), pl.BlockSpec((tm, ns), lambda j, kk, m: (m, j))],
        out_specs=pl.BlockSpec((None, tk, ns), lambda j, kk, m: (j, kk, 0)),
        out_shape=jax.ShapeDtypeStruct((nsh, k, ns), BF16),
        scratch_shapes=[pltpu.VMEM((tk, ns), F32)],
        compiler_params=_params("parallel", "parallel", "arbitrary"),
    )(a, dy)


def _main_spec(tm, w):
    return pl.BlockSpec((tm, w), lambda m: (m, 0))


def _before_spec(tm, hb, w):
    return pl.BlockSpec((hb, w), lambda m: (jnp.maximum(m * (tm // hb) - 1, 0), 0))


def _after_spec(tm, hb, w, s):
    return pl.BlockSpec((hb, w), lambda m: (jnp.minimum((m + 1) * (tm // hb), s // hb - 1), 0))


def _row_spec(w, rows=1):
    return pl.BlockSpec((rows, w), lambda m: (0, 0))


CHUNK_LANES = 4 * LANES
CHUNK_ROWS = 32


def _build_shifts(ext8_ref, residues=range(1, 8)):
    n = ext8_ref.shape[1] - 8
    for r in residues:
        ext8_ref[r, pl.ds(0, n), :] = ext8_ref[0, pl.ds(r, n), :]


def _fold_rows(x):
    return functools.reduce(lambda p, q: p + q, [x[i:i + 8] for i in range(0, x.shape[0], 8)])


def _shifted(ext8_ref, shift, r0, rows, cols):
    return ext8_ref[shift % 8, pl.ds(pl.multiple_of(shift - shift % 8 + r0, 8), rows), cols]


def _lane_chunk(i):
    return pl.ds(pl.multiple_of(i * CHUNK_LANES, CHUNK_LANES), CHUNK_LANES)


def _sum_terms(terms, ways=4):
    accs = []
    for i, t in enumerate(terms):
        if i < ways:
            accs.append(t)
        else:
            accs[i % ways] = accs[i % ways] + t
    while len(accs) > 1:
        accs = [accs[i] + accs[i + 1] if i + 1 < len(accs) else accs[i] for i in range(0, len(accs), 2)]
    return accs[0]


def _accumulate(ref, val, first):
    @pl.when(first)
    def _():
        ref[...] = val

    @pl.when(jnp.logical_not(first))
    def _():
        ref[...] += val


SCONV_Z_SHIFTS = tuple(SCONV_HALO - (SHORT_CONV_W - 1) + k for k in range(SHORT_CONV_W))


def _sconv_z_taps(zext_ref, r0, cols):
    return [_shifted(zext_ref, shift, r0, CHUNK_ROWS, cols) for shift in SCONV_Z_SHIFTS]


def _weighted(cw_ref, cols, terms):
    return _sum_terms((cw_ref[k:k + 1, cols] * t for k, t in enumerate(terms)), ways=len(terms))


def _sconv_fill_z(zext_ref, main_ref, before_ref, d, m):
    hb = SCONV_HALO
    zb = before_ref[:, d:2 * d].astype(F32) * before_ref[:, 2 * d:].astype(F32)
    zext_ref[pl.ds(0, hb), :] = jnp.where(m > 0, zb, 0.0)
    zext_ref[pl.ds(hb, main_ref.shape[0]), :] = main_ref[:, d:2 * d].astype(F32) * main_ref[:, 2 * d:].astype(F32)


def _sconv_fwd(bcv, cw, name):
    s, d3 = bcv.shape
    d = d3 // 3
    tm = _tile(s, 256, CHUNK_ROWS)
    row_chunks = tm // CHUNK_ROWS

    def body(main_ref, before_ref, cw_ref, p_ref, zext_ref):
        m = pl.program_id(0)
        _sconv_fill_z(zext_ref.at[0], main_ref, before_ref, d, m)
        _build_shifts(zext_ref, [shift % 8 for shift in SCONV_Z_SHIFTS if shift % 8])

        def chunk(i, carry):
            cols = _lane_chunk(i // row_chunks)
            r0 = pl.multiple_of((i % row_chunks) * CHUNK_ROWS, CHUNK_ROWS)
            rows = pl.ds(r0, CHUNK_ROWS)
            zc = _weighted(cw_ref, cols, _sconv_z_taps(zext_ref, r0, cols))
            p_ref[rows, cols] = (main_ref[rows, cols].astype(F32) * zc).astype(p_ref.dtype)
            return carry

        lax.fori_loop(0, row_chunks * (d // CHUNK_LANES), chunk, 0)

    return _hosted(
        body, name=name, grid=(s // tm,),
        in_specs=[_main_spec(tm, d3), _before_spec(tm, SCONV_HALO, d3), _row_spec(d, SHORT_CONV_W)],
        out_specs=_main_spec(tm, d),
        out_shape=jax.ShapeDtypeStruct((s, d), BF16),
        scratch_shapes=[pltpu.VMEM((8, tm + SCONV_HALO, d), F32)],
        compiler_params=_params("parallel"),
    )(bcv, bcv, cw)


def _sconv_bwd(dp, bcv, cw, name):
    s, d3 = bcv.shape
    d = d3 // 3
    tm = _tile(s, 256, CHUNK_ROWS)
    nm = s // tm
    ha = 8
    kw = SHORT_CONV_W

    def body(dp_ref, dpa_ref, main_ref, before_ref, after_ref, cw_ref, o_ref, dcw_ref, zext_ref, dext_ref):
        m = pl.program_id(0)
        _sconv_fill_z(zext_ref.at[0], main_ref, before_ref, d, m)
        _build_shifts(zext_ref, [shift % 8 for shift in SCONV_Z_SHIFTS if shift % 8])
        dext_ref[0, pl.ds(0, tm), :] = dp_ref[...] * main_ref[:, :d].astype(F32)
        dza = dpa_ref[...] * after_ref[:, :d].astype(F32)[0:ha]
        dext_ref[0, pl.ds(tm, ha), :] = jnp.where(m < nm - 1, dza, 0.0)
        _build_shifts(dext_ref, range(1, kw))

        @pl.when(m == 0)
        def _():
            dcw_ref[...] = jnp.zeros_like(dcw_ref)

        zero = jnp.zeros((8, CHUNK_LANES), F32)

        def lane_chunk(ci, carry):
            cols = _lane_chunk(ci)
            c_cols, v_cols = (pl.ds(pl.multiple_of(part * d + ci * CHUNK_LANES, CHUNK_LANES), CHUNK_LANES) for part in (1, 2))

            def row_chunk(ri, sums):
                r0 = pl.multiple_of(ri * CHUNK_ROWS, CHUNK_ROWS)
                rows = pl.ds(r0, CHUNK_ROWS)
                z = _sconv_z_taps(zext_ref, r0, cols)
                o_ref[rows, cols] = (dp_ref[rows, cols] * _weighted(cw_ref, cols, z)).astype(o_ref.dtype)
                dzc = [_shifted(dext_ref, kw - 1 - k, r0, CHUNK_ROWS, cols) for k in range(kw)]
                dz = _weighted(cw_ref, cols, dzc)
                o_ref[rows, c_cols] = (dz * main_ref[rows, v_cols].astype(F32)).astype(o_ref.dtype)
                o_ref[rows, v_cols] = (dz * main_ref[rows, c_cols].astype(F32)).astype(o_ref.dtype)
                return tuple(acc + _fold_rows(dzc[kw - 1] * z[k]) for k, acc in enumerate(sums))

            sums = lax.fori_loop(0, tm // CHUNK_ROWS, row_chunk, (zero,) * kw)
            for k in range(kw):
                dcw_ref[k:k + 1, cols] += _colsum(sums[k])
            return carry

        lax.fori_loop(0, d // CHUNK_LANES, lane_chunk, 0)

    return _hosted(
        body, name=name, grid=(nm,),
        in_specs=[_main_spec(tm, d), _after_spec(tm, ha, d, s), _main_spec(tm, d3), _before_spec(tm, SCONV_HALO, d3),
                  _after_spec(tm, SCONV_HALO, d3, s), _row_spec(d, SHORT_CONV_W)],
        out_specs=[_main_spec(tm, d3), _row_spec(d, 8)],
        out_shape=[jax.ShapeDtypeStruct((s, d3), BF16), jax.ShapeDtypeStruct((8, d), F32)],
        scratch_shapes=[pltpu.VMEM((8, tm + SCONV_HALO, d), F32), pltpu.VMEM((8, tm + ha, d), F32)],
        compiler_params=_params("arbitrary"),
    )(dp, dp, bcv, bcv, bcv, cw)


def _pool_counts(t0, tm, w):
    t = t0 + lax.broadcasted_iota(jnp.int32, (tm, 1), 0)
    return jnp.minimum(t + 1, w).astype(F32)


def _pool_fwd(h, gain, wg, scale, name):
    s, d = h.shape
    ng, cg, _ = wg.shape
    tm = _tile(s, 512, POOL_HALO)

    def body(h_ref, hb_ref, g_ref, wg_ref, sc_ref, o_ref, mx_ref, uext_ref):
        m = pl.program_id(0)
        x = h_ref[...]
        gain_row = g_ref[...]
        xb = hb_ref[...]
        uext_ref[pl.ds(0, POOL_HALO), :] = jnp.where(m > 0, xb * _rms_stats(xb) * gain_row, 0.0)
        uext_ref[pl.ds(POOL_HALO, tm), :] = x * _rms_stats(x) * gain_row
        for gi, win in enumerate(POOL_WINDOWS):
            cols = pl.ds(gi * cg, cg)
            u_g = uext_ref[pl.ds(POOL_HALO, tm), cols]
            acc = u_g
            for i in range(1, win):
                acc = acc + uext_ref[pl.ds(POOL_HALO - i, tm), cols]
            mixed = (acc / _pool_counts(m * tm, tm, win) - u_g).astype(BF16)
            mx_ref[:, cols] = mixed
            o_ref[:, cols] = x[:, gi * cg:(gi + 1) * cg] + _dot(mixed, wg_ref[gi]) * sc_ref[:, cols]

    return _hosted(
        body, name=name, grid=(s // tm,),
        in_specs=[_main_spec(tm, d), _before_spec(tm, POOL_HALO, d), _row_spec(d),
                  pl.BlockSpec((ng, cg, cg), lambda m: (0, 0, 0)), _row_spec(d)],
        out_specs=[_main_spec(tm, d), _main_spec(tm, d)],
        out_shape=[jax.ShapeDtypeStruct((s, d), F32), jax.ShapeDtypeStruct((s, d), BF16)],
        scratch_shapes=[pltpu.VMEM((tm + POOL_HALO, d), F32)],
        compiler_params=_params("parallel"),
    )(h, h, gain, wg, scale)


def _pool_bwd_mm(dh, mixed, wg, scale, name):
    s, d = dh.shape
    ng, cg, _ = wg.shape
    tm = _tile(s, 512)

    def body(dh_ref, mx_ref, wg_ref, sc_ref, dmx_ref, dwg_ref, dsc_ref):
        first = pl.program_id(0) == 0
        for gi in range(ng):
            cols = pl.ds(gi * cg, cg)
            dh_g = dh_ref[:, cols]
            mixed = mx_ref[:, cols]
            w_g = wg_ref[gi]
            dy = (dh_g * sc_ref[:, cols]).astype(BF16)
            dmx_ref[:, cols] = _dot_nt(dy, w_g)
            _accumulate(dsc_ref.at[:, cols], _colsum(dh_g * _dot(mixed, w_g)), first)
            _accumulate(dwg_ref.at[gi], _dot_tn(mixed, dy), first)

    return _hosted(
        body, name=name, grid=(s // tm,),
        in_specs=[_main_spec(tm, d), _main_spec(tm, d), pl.BlockSpec((ng, cg, cg), lambda m: (0, 0, 0)), _row_spec(d)],
        out_specs=[_main_spec(tm, d), pl.BlockSpec((ng, cg, cg), lambda m: (0, 0, 0)), _row_spec(d)],
        out_shape=[jax.ShapeDtypeStruct((s, d), F32), jax.ShapeDtypeStruct((ng, cg, cg), F32),
                   jax.ShapeDtypeStruct((1, d), F32)],
        compiler_params=_params("arbitrary"),
    )(dh, mixed, wg, scale)


def _pool_bwd_rms(dmixed, h, gain, dh, name):
    s, d = h.shape
    cg = d // len(POOL_WINDOWS)
    tm = _tile(s, 512, POOL_HALO)
    nm = s // tm

    def body(dmx_ref, dmxa_ref, h_ref, g_ref, dh_ref, o_ref, dg_ref, eext_ref, du_ref):
        m = pl.program_id(0)
        for gi, win in enumerate(POOL_WINDOWS):
            cols = pl.ds(gi * cg, cg)
            dmx = dmx_ref[:, cols]
            eext_ref[pl.ds(0, tm), cols] = dmx / _pool_counts(m * tm, tm, win)
            ea = dmxa_ref[:, cols] / _pool_counts((m + 1) * tm, POOL_HALO, win)
            eext_ref[pl.ds(tm, POOL_HALO), cols] = jnp.where(m < nm - 1, ea, 0.0)
            acc = -dmx
            for i in range(win):
                acc = acc + eext_ref[pl.ds(i, tm), cols]
            du_ref[:, cols] = acc
        dx, dg = _rms_bwd(du_ref[...], h_ref[...], g_ref[...])
        o_ref[...] = dh_ref[...] + dx
        _accumulate(dg_ref, dg, m == 0)

    return _hosted(
        body, name=name, grid=(nm,),
        in_specs=[_main_spec(tm, d), _after_spec(tm, POOL_HALO, d, s), _main_spec(tm, d), _row_spec(d), _main_spec(tm, d)],
        out_specs=[_main_spec(tm, d), _row_spec(d)],
        out_shape=[jax.ShapeDtypeStruct((s, d), F32), jax.ShapeDtypeStruct((1, d), F32)],
        scratch_shapes=[pltpu.VMEM((tm + POOL_HALO, d), F32), pltpu.VMEM((tm, d), F32)],
        compiler_params=_params("arbitrary"),
    )(dmixed, dmixed, h, gain, dh)


def _conf_fill_h(hext_ref, main_ref, before_ref, d, m):
    hb = before_ref[:, :d].astype(F32) * _sigmoid(before_ref[:, d:].astype(F32))
    hext_ref[pl.ds(0, CONF_HALO), :] = jnp.where(m > 0, hb, 0.0)
    hext_ref[pl.ds(CONF_HALO, main_ref.shape[0]), :] = main_ref[:, :d].astype(F32) * _sigmoid(main_ref[:, d:].astype(F32))


def _layernorm_parts(hc, g, b):
    mu = jnp.mean(hc, axis=-1, keepdims=True)
    xc = hc - mu
    rs = lax.rsqrt(jnp.mean(xc * xc, axis=-1, keepdims=True) + LN_EPS)
    xhat = xc * rs
    return xhat, rs, xhat * g + b


def _conf_mid_fwd(ag, dw, b_dw, ln_g, ln_b, name):
    s, d2 = ag.shape
    d = d2 // 2
    tm = _tile(s, 256, CONF_HALO)
    base = CONF_HALO - (CONF_CONV_W - 1)

    def body(main_ref, before_ref, dw_ref, bdw_ref, g_ref, b_ref, s_ref, hc_ref, hext_ref):
        m = pl.program_id(0)
        _conf_fill_h(hext_ref.at[0], main_ref, before_ref, d, m)
        _build_shifts(hext_ref)
        row_chunks = tm // CHUNK_ROWS

        def conv_chunk(i, carry):
            cols = _lane_chunk(i // row_chunks)
            r0 = pl.multiple_of((i % row_chunks) * CHUNK_ROWS, CHUNK_ROWS)
            taps = (dw_ref[kk:kk + 1, cols] * _shifted(hext_ref, base + kk, r0, CHUNK_ROWS, cols) for kk in range(CONF_CONV_W))
            hc_ref[pl.ds(r0, CHUNK_ROWS), cols] = bdw_ref[:, cols] + _sum_terms(taps, ways=1)
            return carry

        lax.fori_loop(0, row_chunks * (d // CHUNK_LANES), conv_chunk, 0)
        _, _, l = _layernorm_parts(hc_ref[...], g_ref[...], b_ref[...])
        s_ref[...] = (l * _sigmoid(l)).astype(s_ref.dtype)

    return _hosted(
        body, name=name, grid=(s // tm,),
        in_specs=[_main_spec(tm, d2), _before_spec(tm, CONF_HALO, d2), _row_spec(d, CONF_CONV_W), _row_spec(d),
                  _row_spec(d), _row_spec(d)],
        out_specs=[_main_spec(tm, d), _main_spec(tm, d)],
        out_shape=[jax.ShapeDtypeStruct((s, d), BF16), jax.ShapeDtypeStruct((s, d), F32)],
        scratch_shapes=[pltpu.VMEM((8, tm + CONF_HALO, d), F32)],
        compiler_params=_params("parallel"),
    )(ag, ag, dw, b_dw, ln_g, ln_b)


def _conf_out_bwd(dh, w, hc, ln_g, ln_b, name):
    s, d = dh.shape
    tm = _tile(s, 256)

    def body(dh_ref, w_ref, hc_ref, g_ref, b_ref, o_ref, dg_ref, db_ref, dbo_ref):
        first = pl.program_id(0) == 0
        dh_t = dh_ref[...]
        ds = _dot_nt(dh_t.astype(BF16), w_ref[...])
        xhat, rs, l = _layernorm_parts(hc_ref[...], g_ref[...], b_ref[...])
        sg = _sigmoid(l)
        dl = ds * sg * (1.0 + l * (1.0 - sg))
        dxh = dl * g_ref[...]
        o_ref[...] = rs * (dxh - jnp.mean(dxh, axis=-1, keepdims=True)
                           - xhat * jnp.mean(dxh * xhat, axis=-1, keepdims=True))
        _accumulate(dg_ref, _colsum(dl * xhat), first)
        _accumulate(db_ref, _colsum(dl), first)
        _accumulate(dbo_ref, _colsum(dh_t), first)

    return _hosted(
        body, name=name, grid=(s // tm,),
        in_specs=[_main_spec(tm, d), pl.BlockSpec((d, d), lambda m: (0, 0)), _main_spec(tm, d), _row_spec(d), _row_spec(d)],
        out_specs=[_main_spec(tm, d), _row_spec(d), _row_spec(d), _row_spec(d)],
        out_shape=[jax.ShapeDtypeStruct((s, d), F32)] + [jax.ShapeDtypeStruct((1, d), F32)] * 3,
        compiler_params=_params("arbitrary"),
    )(dh, w, hc, ln_g, ln_b)


def _conf_mid_bwd(dhc, ag, dw, name):
    s, d2 = ag.shape
    d = d2 // 2
    tm = _tile(s, 256, CONF_HALO)
    nm = s // tm
    kw = CONF_CONV_W
    base = CONF_HALO - (kw - 1)

    def body(dhc_ref, dhca_ref, main_ref, before_ref, dw_ref, o_ref, ddw_ref, dbdw_ref, dbpw_ref, hext_ref, dext_ref):
        m = pl.program_id(0)
        first = m == 0
        _conf_fill_h(hext_ref.at[0], main_ref, before_ref, d, m)
        _build_shifts(hext_ref)
        dext_ref[0, pl.ds(0, tm), :] = dhc_ref[...]
        dext_ref[0, pl.ds(tm, CONF_HALO), :] = jnp.where(m < nm - 1, dhca_ref[...], 0.0)
        _build_shifts(dext_ref)

        @pl.when(first)
        def _():
            ddw_ref[...] = jnp.zeros_like(ddw_ref)
            dbdw_ref[...] = jnp.zeros_like(dbdw_ref)
            dbpw_ref[...] = jnp.zeros_like(dbpw_ref)

        zero = jnp.zeros((8, CHUNK_LANES), F32)
        tap_group = 8

        def fold(x):
            return functools.reduce(lambda p, q: p + q, [x[i:i + 8] for i in range(0, CHUNK_ROWS, 8)])

        def lane_chunk(ci, carry):
            cols = _lane_chunk(ci)
            gate_cols = pl.ds(pl.multiple_of(d + ci * CHUNK_LANES, CHUNK_LANES), CHUNK_LANES)

            def through_conv(ri, sums):
                r0 = pl.multiple_of(ri * CHUNK_ROWS, CHUNK_ROWS)
                rows = pl.ds(r0, CHUNK_ROWS)
                dhh = _sum_terms((dw_ref[kk:kk + 1, cols] * _shifted(dext_ref, kw - 1 - kk, r0, CHUNK_ROWS, cols)
                                  for kk in range(kw)), ways=1)
                a = main_ref[rows, cols].astype(F32)
                sg = _sigmoid(main_ref[rows, gate_cols].astype(F32))
                da = dhh * sg
                dgate = dhh * a * sg * (1.0 - sg)
                o_ref[rows, cols] = da.astype(o_ref.dtype)
                o_ref[rows, gate_cols] = dgate.astype(o_ref.dtype)
                return sums[0] + fold(da), sums[1] + fold(dgate), sums[2] + fold(dext_ref[0, rows, cols])

            sum_da, sum_dgate, sum_dhc = lax.fori_loop(0, tm // CHUNK_ROWS, through_conv, (zero, zero, zero))
            dbdw_ref[:, cols] += _colsum(sum_dhc)
            dbpw_ref[:, cols] += _colsum(sum_da)
            dbpw_ref[:, gate_cols] += _colsum(sum_dgate)

            for k0 in range(0, kw, tap_group):
                group = range(k0, min(k0 + tap_group, kw))

                def tap_gradients(ri, accs, group=group):
                    for sub in range(0, CHUNK_ROWS, 8):
                        r0 = pl.multiple_of(ri * CHUNK_ROWS + sub, 8)
                        dhc_c = dext_ref[0, pl.ds(r0, 8), cols]
                        accs = tuple(acc + dhc_c * _shifted(hext_ref, base + kk, r0, 8, cols) for kk, acc in zip(group, accs))
                    return accs

                accs = lax.fori_loop(0, tm // CHUNK_ROWS, tap_gradients, (zero,) * len(group))
                for kk, acc in zip(group, accs):
                    ddw_ref[kk:kk + 1, cols] += _colsum(acc)
            return carry

        lax.fori_loop(0, d // CHUNK_LANES, lane_chunk, 0)

    return _hosted(
        body, name=name, grid=(nm,),
        in_specs=[_main_spec(tm, d), _after_spec(tm, CONF_HALO, d, s), _main_spec(tm, d2), _before_spec(tm, CONF_HALO, d2),
                  _row_spec(d, kw)],
        out_specs=[_main_spec(tm, d2), _row_spec(d, 32), _row_spec(d), _row_spec(d2)],
        out_shape=[jax.ShapeDtypeStruct((s, d2), BF16), jax.ShapeDtypeStruct((32, d), F32),
                   jax.ShapeDtypeStruct((1, d), F32), jax.ShapeDtypeStruct((1, d2), F32)],
        scratch_shapes=[pltpu.VMEM((8, tm + CONF_HALO, d), F32), pltpu.VMEM((8, tm + CONF_HALO, d), F32)],
        compiler_params=_params("arbitrary"),
    )(dhc, dhc, ag, ag, dw)


def _loss_head(h, gain, target, name):
    s, d = h.shape
    tm = _tile(s, 512)

    def body(h_ref, g_ref, t_ref, loss_ref, dh_ref, dg_ref):
        first = pl.program_id(0) == 0
        x = h_ref[...]
        err = x * _rms_stats(x) * g_ref[...] - t_ref[...]
        part = 0.5 * jnp.sum(jnp.mean(err * err, axis=-1, keepdims=True), axis=0, keepdims=True)
        dx, dg = _rms_bwd(err * (1.0 / d), x, g_ref[...])
        dh_ref[...] = dx
        _accumulate(loss_ref, part, first)
        _accumulate(dg_ref, dg, first)

    return _hosted(
        body, name=name, grid=(s // tm,),
        in_specs=[_main_spec(tm, d), _row_spec(d), _main_spec(tm, d)],
        out_specs=[pl.BlockSpec((1, 1), lambda m: (0, 0)), _main_spec(tm, d), _row_spec(d)],
        out_shape=[jax.ShapeDtypeStruct((1, 1), F32), jax.ShapeDtypeStruct((s, d), F32), jax.ShapeDtypeStruct((1, d), F32)],
        compiler_params=_params("arbitrary"),
    )(h, gain, target)


def _ffn_fwd(h, wts, i):
    u = _rms_fwd(h, wts[f"ln2_{i}"], f"ffn{i}_rms")
    act, s1, q1 = _ffn_up(u, wts[f"ffn{i}_w_gu"], f"ffn{i}_up")
    h_new = _mm_row(act, wts[f"ffn{i}_w_down"], h, None, f"ffn{i}_down")
    return h_new, (h, u, act, s1, q1)


def _ffn_bwd(dh, saved, wts, i, g):
    h, u, act, s1, q1 = saved
    dgu = _ffn_down_bwd(dh, wts[f"ffn{i}_w_down"], s1, q1, f"ffn{i}_down_bwd")
    g[f"ffn{i}_w_down"] = _mm_tn(act, dh, 1, f"ffn{i}_dw_down")
    g[f"ffn{i}_w_gu"] = _mm_tn(u, dgu, N_CHIPS, f"ffn{i}_dw_gu")
    dh_new, g[f"ln2_{i}"] = _mm_nt_col_rms_bwd(dgu, wts[f"ffn{i}_w_gu"], h, wts[f"ln2_{i}"], dh, f"ffn{i}_up_bwd")
    return dh_new


def _device_step(x, target, wts, g=None):
    g = {} if g is None else g
    saved = {}
    h = x

    def short_conv_fwd(h, i):
        u = _rms_fwd(h, wts[f"ln1_{i}"], f"a{i}_rms")
        bcv = _mm_col(u, wts[f"a{i}_w_in"], None, f"a{i}_in")
        p = _sconv_fwd(bcv, wts[f"a{i}_conv"], f"a{i}_conv")
        return _mm_row(p, wts[f"a{i}_w_out"], h, None, f"a{i}_out"), (h, u, bcv, p)

    def short_conv_bwd(dh, sv, i):
        h, u, bcv, p = sv
        dp = _mm_nt_row(dh, wts[f"a{i}_w_out"], f"a{i}_out_bwd")
        dbcv, dcw = _sconv_bwd(dp, bcv, wts[f"a{i}_conv"], f"a{i}_conv_bwd")
        g[f"a{i}_conv"] = dcw[:SHORT_CONV_W]
        g[f"a{i}_w_in"] = _mm_tn(u, dbcv, N_CHIPS, f"a{i}_dw_in")
        g[f"a{i}_w_out"] = _mm_tn(p, dh, 1, f"a{i}_dw_out")
        dh, g[f"ln1_{i}"] = _mm_nt_col_rms_bwd(dbcv, wts[f"a{i}_w_in"], h, wts[f"ln1_{i}"], dh, f"a{i}_in_bwd")
        return dh

    h, saved["a0"] = short_conv_fwd(h, 0)
    h, saved["f0"] = _ffn_fwd(h, wts, 0)

    h_in = h
    h, mixed = _pool_fwd(h, wts["ln1_1"], wts["b1_w_grp"], wts["b1_scale"], "b1_fwd")
    saved["b1"] = (h_in, mixed)
    h, saved["f1"] = _ffn_fwd(h, wts, 1)

    h_in = h
    u = _rms_fwd(h, wts["ln1_2"], "c2_rms")
    ag = _mm_col(u, wts["c2_w_pw1"], wts["c2_b_pw1"], "c2_pw1")
    sw, hc = _conf_mid_fwd(ag, wts["c2_dw"], wts["c2_b_dw"], wts["c2_ln_g"], wts["c2_ln_b"], "c2_mid")
    h = _mm_row(sw, wts["c2_w_pw2"], h, wts["c2_b_pw2"], "c2_pw2")
    saved["c2"] = (h_in, u, ag, sw, hc)
    h, saved["f2"] = _ffn_fwd(h, wts, 2)

    h, saved["a3"] = short_conv_fwd(h, 3)
    h, saved["f3"] = _ffn_fwd(h, wts, 3)

    loss, dh, g["ln_f"] = _loss_head(h, wts["ln_f"], target, "loss_head")

    def ffn_bwd(dh, i):
        return _ffn_bwd(dh, saved[f"f{i}"], wts, i, g)

    dh = ffn_bwd(dh, 3)
    dh = short_conv_bwd(dh, saved["a3"], 3)

    dh = ffn_bwd(dh, 2)
    h_in, u, ag, sw, hc = saved["c2"]
    dhc, g["c2_ln_g"], g["c2_ln_b"], g["c2_b_pw2"] = _conf_out_bwd(
        dh, wts["c2_w_pw2"], hc, wts["c2_ln_g"], wts["c2_ln_b"], "c2_pw2_bwd")
    g["c2_w_pw2"] = _mm_tn(sw, dh, 1, "c2_dw_pw2")
    dag, ddw, g["c2_b_dw"], g["c2_b_pw1"] = _conf_mid_bwd(dhc, ag, wts["c2_dw"], "c2_mid_bwd")
    g["c2_dw"] = ddw[:CONF_CONV_W]
    g["c2_w_pw1"] = _mm_tn(u, dag, N_CHIPS, "c2_dw_pw1")
    dh, g["ln1_2"] = _mm_nt_col_rms_bwd(dag, wts["c2_w_pw1"], h_in, wts["ln1_2"], dh, "c2_pw1_bwd")

    dh = ffn_bwd(dh, 1)
    h_in, mixed = saved["b1"]
    dmixed, g["b1_w_grp"], g["b1_scale"] = _pool_bwd_mm(dh, mixed, wts["b1_w_grp"], wts["b1_scale"], "b1_bwd_mm")
    dh, g["ln1_1"] = _pool_bwd_rms(dmixed, h_in, wts["ln1_1"], dh, "b1_bwd_rms")

    dh = ffn_bwd(dh, 0)
    dh = short_conv_bwd(dh, saved["a0"], 0)
    return loss, dh, g


MESH = pl.DeviceIdType.MESH
ANY = pl.BlockSpec(memory_space=pl.ANY)


def _position():
    return lax.axis_index("x"), lax.axis_index("y"), lax.axis_index("c")


def _other_chips(x, y):
    return [(1 - x, y), (x, 1 - y), (1 - x, 1 - y)]


def _remote(src, dst, send_sem, recv_sem, to):
    return pltpu.make_async_remote_copy(src_ref=src, dst_ref=dst, send_sem=send_sem, recv_sem=recv_sem,
                                        device_id=to, device_id_type=MESH)


def _half_rows(ref_rows, c):
    hr = ref_rows // 2
    return pl.ds(pl.multiple_of(c * hr, 16), hr)


def _allgather8(v, name):
    m_per, n = v.shape

    def body(v_ref, out_ref, send_sems, recv_sems, local_sem):
        x, y, c = _position()
        me, sibling = (x, y, c), (x, y, 1 - c)
        chips = _other_chips(x, y)

        def rows(px, py, pc):
            return out_ref.at[pl.ds((4 * px + 2 * py + pc) * m_per, m_per), :]

        def copy(k, block, to, src=None):
            return _remote(rows(*block) if src is None else src, rows(*block), send_sems.at[k], recv_sems.at[k], to)

        mine = pltpu.make_async_copy(v_ref, rows(*me), local_sem)
        mine.start()
        first = [copy(0, me, sibling, src=v_ref)]
        first += [copy(1 + j, me, (*chip, c), src=v_ref) for j, chip in enumerate(chips)]
        for cp in first:
            cp.start()
        passed = [copy(4 + j, (*chip, c), sibling) for j, chip in enumerate(chips)]
        for j, chip in enumerate(chips):
            copy(1 + j, (*chip, c), me).wait_recv()
            passed[j].start()
        copy(0, sibling, me).wait_recv()
        for j, chip in enumerate(chips):
            copy(4 + j, (*chip, 1 - c), me).wait_recv()
        for cp in first + passed:
            cp.wait_send()
        mine.wait()

    return _hosted(
        body, name=name,
        out_shape=jax.ShapeDtypeStruct((N_DEV * m_per, n), v.dtype),
        in_specs=[pl.BlockSpec(memory_space=pltpu.VMEM)],
        out_specs=pl.BlockSpec(memory_space=pltpu.VMEM),
        scratch_shapes=[pltpu.SemaphoreType.DMA((7,)), pltpu.SemaphoreType.DMA((7,)), pltpu.SemaphoreType.DMA],
        compiler_params=pltpu.CompilerParams(vmem_limit_bytes=VMEM_LIMIT),
    )(v)


def _cast_to_slot(ws, idx, name):
    r, cols = ws[0].shape
    assert all(w.shape == (r, cols) for w in ws)
    n = len(ws)
    tr = _tile(r, 256, 16)

    def body(idx_ref, *refs):
        for w_ref, o_ref in zip(refs[:n], refs[n:]):
            o_ref[...] = w_ref[...].astype(o_ref.dtype)

    return _hosted(
        body, name=name,
        grid_spec=pltpu.PrefetchScalarGridSpec(
            num_scalar_prefetch=1, grid=(r // tr,),
            in_specs=[pl.BlockSpec((tr, cols), lambda t, idx_ref: (t, 0))] * n,
            out_specs=[pl.BlockSpec((None, tr, cols), lambda t, idx_ref: (idx_ref[0], t, 0))] * n),
        out_shape=[jax.ShapeDtypeStruct((N_CHIPS, r, cols), BF16)] * n,
        compiler_params=_params("parallel"),
    )(idx, *ws)


def _dma_sems(*shape):
    return [pltpu.SemaphoreType.DMA(shape), pltpu.SemaphoreType.DMA(shape)]


def _same_shapes(arrays):
    return [jax.ShapeDtypeStruct(a.shape, a.dtype) for a in arrays]


def _part_rows(ref_rows, c, part):
    hr = ref_rows // 2
    i, n = part
    size = hr // n
    assert size * n == hr and size % 16 == 0, (ref_rows, part)
    return pl.ds(pl.multiple_of(c * hr + i * size, 16), size)


def _task_gather_ici(bufs, done, part=(0, 1)):
    n = len(bufs)

    def copies(outs, sems, landing):
        x, y, c = _position()
        my_chip = 2 * x + y
        res = []
        for i in range(n):
            rows = _part_rows(bufs[i].shape[1], c, part)
            for r, (px, py) in enumerate(_other_chips(x, y)):
                slot = (2 * px + py) if landing else my_chip
                res.append(_remote(outs[i].at[my_chip, rows, :], outs[i].at[slot, rows, :], sems[0].at[i, r], sems[1].at[i, r],
                                   (px, py, c)))
        return res

    def start(ins, outs, sems):
        for cp in copies(outs, sems, False):
            cp.start()

    def wait(ins, outs, sems):
        for cp in copies(outs, sems, True):
            cp.wait_recv()
            cp.wait_send()

    return _Task(bufs, _same_shapes(bufs), {i: i for i in range(n)}, _dma_sems(n, 3), start, wait, done)


def _task_gather_d2d(bufs, done):
    n = len(bufs)

    def copies(outs, sems, landing):
        x, y, c = _position()
        res = []
        for i in range(n):
            rows = _half_rows(bufs[i].shape[1], (1 - c) if landing else c)
            for r, (px, py) in enumerate(_other_chips(x, y)):
                part = outs[i].at[2 * px + py, rows, :]
                res.append(_remote(part, part, sems[0].at[i, r], sems[1].at[i, r], (x, y, 1 - c)))
        return res

    def start(ins, outs, sems):
        for cp in copies(outs, sems, False):
            cp.start()

    def wait(ins, outs, sems):
        for cp in copies(outs, sems, True):
            cp.wait_recv()
        for cp in copies(outs, sems, False):
            cp.wait_send()

    return _Task(bufs, _same_shapes(bufs), {i: i for i in range(n)}, _dma_sems(n, 3), start, wait, done)


def _task_sibling_halves(grads, done):
    n = len(grads)

    def copies(ins, outs, sems):
        x, y, c = _position()
        return [_remote(ins[i].at[:, _half_rows(grads[i].shape[1], 1 - c), :], outs[i], sems[0].at[i], sems[1].at[i],
                        (x, y, 1 - c)) for i in range(n)]

    def start(ins, outs, sems):
        for cp in copies(ins, outs, sems):
            cp.start()

    def wait(ins, outs, sems):
        for cp in copies(ins, outs, sems):
            cp.wait()

    shapes = [jax.ShapeDtypeStruct((g.shape[0], g.shape[1] // 2, g.shape[2]), g.dtype) for g in grads]
    return _Task(grads, shapes, {}, _dma_sems(n), start, wait, done)


def _task_chip_sums(parts, done, landed=None, part=(0, 1)):
    n = len(parts)
    i_part, n_parts = part
    sizes = [p.shape[1] // n_parts for p in parts]
    assert all(p.shape[1] == size * n_parts and size % 16 == 0 for p, size in zip(parts, sizes)), part
    rows = [pl.ds(i_part * size, size) for size in sizes]

    def copies(ins, outs, sems):
        x, y, c = _position()
        return [_remote(ins[i].at[2 * px + py, rows[i], :], outs[i].at[r, rows[i], :], sems[0].at[i, r], sems[1].at[i, r],
                        (px, py, c))
                for i in range(n) for r, (px, py) in enumerate(_other_chips(x, y))]

    def start(ins, outs, sems):
        for cp in copies(ins, outs, sems):
            cp.start()

    def wait(ins, outs, sems):
        for cp in copies(ins, outs, sems):
            cp.wait()

    shapes = [jax.ShapeDtypeStruct((3,) + p.shape[1:], p.dtype) for p in parts]
    if landed is None:
        return _Task(parts, shapes, {}, _dma_sems(n, 3), start, wait, done)
    return _Task(list(parts) + list(landed), shapes, {n + i: i for i in range(n)}, _dma_sems(n, 3), start, wait, done)


def _task_sibling_parts(owns, landeds, done):
    n = len(owns)

    def copies(ins, outs, sems):
        x, y, c = _position()
        sibling = (x, y, 1 - c)
        res = []
        for i in range(n):
            res.append(_remote(ins[i].at[2 * x + y], outs[i].at[0], sems[0].at[i, 0], sems[1].at[i, 0], sibling))
            res.append(_remote(ins[n + i], outs[i].at[pl.ds(1, 3)], sems[0].at[i, 1], sems[1].at[i, 1], sibling))
        return res

    def start(ins, outs, sems):
        for cp in copies(ins, outs, sems):
            cp.start()

    def wait(ins, outs, sems):
        for cp in copies(ins, outs, sems):
            cp.wait()

    return _Task(list(owns) + list(landeds), _same_shapes(owns), {}, _dma_sems(n, 2), start, wait, done)


def _add_halves(grad, sib, c, name):
    nsh, r, cols = grad.shape
    hr = r // 2
    tr = _tile(hr, 512, 16)
    nt = hr // tr

    def body(c_ref, g_ref, s_ref, o_ref):
        o_ref[...] = (g_ref[...].astype(F32) + s_ref[...].astype(F32)).astype(o_ref.dtype)

    return _hosted(
        body, name=name,
        grid_spec=pltpu.PrefetchScalarGridSpec(
            num_scalar_prefetch=1, grid=(nsh, nt),
            in_specs=[pl.BlockSpec((None, tr, cols), lambda j, t, c_ref: (j, c_ref[1] * nt + t, 0)),
                      pl.BlockSpec((None, tr, cols), lambda j, t, c_ref: (j, t, 0))],
            out_specs=pl.BlockSpec((None, tr, cols), lambda j, t, c_ref: (j, t, 0))),
        out_shape=jax.ShapeDtypeStruct((nsh, hr, cols), BF16),
        compiler_params=_params("parallel", "parallel"),
    )(c, grad, sib)


def _adamw_reduced(w, own, landed, sib, m, v, idx, name):
    r, cols = w.shape
    hr = r // 2
    tr = _tile(hr, 256, 16)
    nt = hr // tr

    def body(idx_ref, w_ref, p_ref, l_ref, s_ref, m_ref, v_ref, go_ref, d_ref, mo_ref, vo_ref):
        mine = p_ref[...].astype(F32)
        for k in range(3):
            mine = mine + l_ref[k].astype(F32)
        theirs = s_ref[0].astype(F32)
        for k in range(1, 4):
            theirs = theirs + s_ref[k].astype(F32)
        grad = jnp.where(pl.program_id(0) // nt == idx_ref[1], mine, theirs)
        go_ref[...] = grad
        d_ref[...], mo_ref[...], vo_ref[...] = _adamw_update(w_ref[...], grad, m_ref[...], v_ref[...])

    def in_half(t, half):
        return jnp.clip(t - half * nt, 0, nt - 1)

    full = pl.BlockSpec((tr, cols), lambda t, idx_ref: (t, 0))
    return _hosted(
        body, name=name,
        grid_spec=pltpu.PrefetchScalarGridSpec(
            num_scalar_prefetch=1, grid=(2 * nt,),
            in_specs=[full,
                      pl.BlockSpec((None, tr, cols), lambda t, idx_ref: (idx_ref[0], in_half(t, idx_ref[1]), 0)),
                      pl.BlockSpec((3, tr, cols), lambda t, idx_ref: (0, in_half(t, idx_ref[1]), 0)),
                      pl.BlockSpec((4, tr, cols), lambda t, idx_ref: (0, in_half(t, 1 - idx_ref[1]), 0)),
                      full, full],
            out_specs=[full] * 4),
        out_shape=[jax.ShapeDtypeStruct((r, cols), F32)] * 4,
        compiler_params=_params("arbitrary"),
    )(idx, w, own, landed, sib, m, v)


def _sum_devices(blocks, name):
    m8, n = blocks.shape
    m = m8 // N_DEV

    def body(b_ref, o_ref):
        acc = b_ref[pl.ds(0, m), :]
        for k in range(1, N_DEV):
            acc = acc + b_ref[pl.ds(k * m, m), :]
        o_ref[...] = acc

    return _hosted(
        body, name=name, out_shape=jax.ShapeDtypeStruct((m, n), F32),
        in_specs=[pl.BlockSpec(memory_space=pltpu.VMEM)], out_specs=pl.BlockSpec(memory_space=pltpu.VMEM),
        compiler_params=pltpu.CompilerParams(vmem_limit_bytes=VMEM_LIMIT),
    )(blocks)


def _adamw_update(w, grad, m, v):
    new_m = ADAM_B1 * m + (1.0 - ADAM_B1) * grad
    new_v = ADAM_B2 * v + (1.0 - ADAM_B2) * (grad * grad)
    m_hat = new_m * (1.0 / (1.0 - ADAM_B1 ** ADAM_STEP))
    v_hat = new_v * (1.0 / (1.0 - ADAM_B2 ** ADAM_STEP))
    return -ADAM_LR * (m_hat / (jnp.sqrt(v_hat) + ADAM_EPS) + ADAM_WD * w), new_m, new_v


def _adamw(w, g, m, v, name):
    r, cols = w.shape
    tr = _tile(r, 256)

    def body(w_ref, g_ref, m_ref, v_ref, go_ref, d_ref, mo_ref, vo_ref):
        grad = g_ref[...]
        go_ref[...] = grad
        d_ref[...], mo_ref[...], vo_ref[...] = _adamw_update(w_ref[...], grad, m_ref[...], v_ref[...])

    spec = pl.BlockSpec((tr, cols), lambda t: (t, 0))
    return _hosted(
        body, name=name, grid=(r // tr,), in_specs=[spec] * 4, out_specs=[spec] * 4,
        out_shape=[jax.ShapeDtypeStruct((r, cols), F32)] * 4,
        compiler_params=_params("parallel"),
    )(w, g, m, v)


def _adamw_small(grad_blocks, params, name):
    nb, npar = len(grad_blocks), len(params)

    def body(*refs):
        blocks, ins, outs = refs[:nb], refs[nb:nb + 3 * npar], refs[nb + 3 * npar:]
        for p, (w, _, _, blk, row0) in enumerate(params):
            if w.ndim == 1:
                tiled = (w.shape[0] // LANES, LANES)
                grad = blocks[blk][pl.ds(row0, tiled[0]), pl.ds(0, LANES)]
                wmv = [ins[3 * p + k][...].reshape(tiled) for k in range(3)]
            else:
                grad = blocks[blk][pl.ds(row0, w.shape[0]), :]
                wmv = [ins[3 * p + k][...] for k in range(3)]
            for k, res in enumerate((grad,) + _adamw_update(wmv[0], grad, wmv[1], wmv[2])):
                outs[4 * p + k][...] = res.reshape(w.shape)

    args = list(grad_blocks) + [a for w, m, v, _, _ in params for a in (w, m, v)]
    vmem = pl.BlockSpec(memory_space=pltpu.VMEM)
    out = _hosted(
        body, name=name, in_specs=[vmem] * len(args), out_specs=[vmem] * (4 * npar),
        out_shape=[jax.ShapeDtypeStruct(w.shape, F32) for w, _, _, _, _ in params for _ in range(4)],
    )(*args)
    return [tuple(out[4 * p:4 * p + 4]) for p in range(npar)]


WEIGHT_NAMES = (
    "ln1_0", "a0_w_in", "a0_conv", "a0_w_out", "ln2_0", "ffn0_w_gu", "ffn0_w_down",
    "ln1_1", "b1_w_grp", "b1_scale", "ln2_1", "ffn1_w_gu", "ffn1_w_down",
    "ln1_2", "c2_w_pw1", "c2_b_pw1", "c2_dw", "c2_b_dw", "c2_ln_g", "c2_ln_b", "c2_w_pw2", "c2_b_pw2",
    "ln2_2", "ffn2_w_gu", "ffn2_w_down",
    "ln1_3", "a3_w_in", "a3_conv", "a3_w_out", "ln2_3", "ffn3_w_gu", "ffn3_w_down", "ln_f")
BIG = ("a0_w_in", "a0_w_out", "ffn0_w_gu", "ffn0_w_down", "b1_w_grp", "ffn1_w_gu", "ffn1_w_down", "c2_w_pw1", "c2_w_pw2",
       "ffn2_w_gu", "ffn2_w_down", "a3_w_in", "a3_w_out", "ffn3_w_gu", "ffn3_w_down")
GROUPED = "b1_w_grp"
SMALL_SHARDED = ("a0_conv", "a3_conv", "c2_dw")
REPLICATED = tuple(n for n in WEIGHT_NAMES if n not in BIG and n not in SMALL_SHARDED)


def _pad_rows(a, mult=8):
    pad = -a.shape[0] % mult
    return a if pad == 0 else jnp.concatenate([a, jnp.zeros((pad, a.shape[1]), a.dtype)], axis=0)


def _pack_rows(parts, width):
    rows = [p.reshape(-1, width) for p in parts]
    return _pad_rows(jnp.concatenate(rows, axis=0)), [r.shape[0] for r in rows]


def _unpack_rows(packed, counts, shapes):
    out, at = [], 0
    for n, shp in zip(counts, shapes):
        out.append(packed[at:at + n].reshape(shp))
        at += n
    return out


COLUMN_SHARDED = ("w_in", "w_gu", "w_pw1")


class _Weights(dict):
    def __init__(self, bufs):
        super().__init__()
        self.bufs = bufs

    def __missing__(self, name):
        buf = self.bufs[name]
        if name == GROUPED:
            cg = buf.shape[-1]
            rq = cg // N_CHIPS
            return jnp.transpose(buf.reshape(N_CHIPS, -1, rq, cg), (1, 0, 2, 3)).reshape(-1, cg, cg)
        return buf if name.endswith(COLUMN_SHARDED) else buf.reshape(-1, buf.shape[-1])


class _Exchange:
    def __init__(self, w, mom, vel, idx):
        def shards(table):
            return {n: table[n].reshape(-1, table[n].shape[-1]) for n in BIG}

        self.w, self.mom, self.vel, self.idx = shards(w), shards(mom), shards(vel), idx
        self.bufs = {}
        self.weights = _Weights(self.bufs)
        self.grads = {}
        self.sib, self.part, self.landed, self.sib_parts, self.updates = {}, {}, {}, {}, {}

    def cast(self, names):
        by_shape = {}
        for n in names:
            by_shape.setdefault(self.w[n].shape, []).append(n)
        for group in by_shape.values():
            self.bufs.update(zip(group, _cast_to_slot([self.w[n] for n in group], self.idx, f"cast_{group[0]}")))

    @staticmethod
    def _store(table, names):
        def done(arrays):
            table.update(zip(names, arrays))
        return done

    def _grad(self, n):
        g = self.grads[n]
        if n == GROUPED:
            ng, cg, _ = g.shape
            g = jnp.transpose(g.reshape(ng, N_CHIPS, cg // N_CHIPS, cg), (1, 0, 2, 3)).astype(BF16)
        return g.reshape(N_CHIPS, -1, g.shape[-1])

    def gather_ici(self, *names, part=(0, 1)):
        return lambda: _task_gather_ici([self.bufs[n] for n in names], self._store(self.bufs, names), part)

    def gather_d2d(self, *names):
        return lambda: _task_gather_d2d([self.bufs[n] for n in names], self._store(self.bufs, names))

    def sibling_halves(self, *names):
        return lambda: _task_sibling_halves([self._grad(n) for n in names], self._store(self.sib, names))

    def add_halves(self, *names):
        def run():
            for n in names:
                self.part[n] = _add_halves(self._grad(n), self.sib.pop(n), self.idx, f"reduce_add_{n}")
        return run

    def chip_sums(self, *names, part=(0, 1)):
        def make():
            landed = [self.landed[n] for n in names] if part[0] > 0 else None
            return _task_chip_sums([self.part[n] for n in names], self._store(self.landed, names), landed, part)
        return make

    def sibling_parts(self, *names):
        return lambda: _task_sibling_parts([self.part[n] for n in names], [self.landed[n] for n in names],
                                           self._store(self.sib_parts, names))

    def adamw(self, *names):
        def run():
            for n in names:
                self.updates[n] = _adamw_reduced(self.w[n], self.part.pop(n), self.landed.pop(n), self.sib_parts.pop(n),
                                                 self.mom[n], self.vel[n], self.idx, f"adamw_{n}")
        return run


def _plan(ex):
    s = _Schedule()

    def ffn(i):
        return f"ffn{i}_w_gu", f"ffn{i}_w_down"

    c2, a3 = ("c2_w_pw1", "c2_w_pw2"), ("a3_w_in", "a3_w_out")
    first, second = (0, 2), (1, 2)
    s.host("cast_ffn0_w_gu", ex.gather_ici("a0_w_in", part=first))
    s.host("cast_ffn0_w_down", ex.gather_ici("a0_w_in", part=second))
    s.host("cast_c2_w_pw1", ex.gather_d2d("a0_w_in"))
    gu, down = ffn(0)
    s.host("a0_rms", ex.gather_ici("a0_w_out"))
    s.host("a0_in", ex.gather_ici(gu, part=first), ex.gather_d2d("a0_w_out"))
    s.host("a0_conv", ex.gather_ici(gu, part=second))
    s.host("a0_out", ex.gather_ici(down), ex.gather_d2d(gu))
    s.host("ffn0_up", ex.gather_d2d(down))
    gu, down = ffn(1)
    s.host("ffn0_up", ex.gather_ici(gu, GROUPED))
    s.host("ffn0_down", ex.gather_ici(down), ex.gather_d2d(gu, GROUPED))
    s.host("ffn1_up", ex.gather_d2d(down), ex.gather_ici(*c2))
    gu, down = ffn(2)
    s.host("ffn1_up", ex.gather_ici(gu, part=first))
    s.host("ffn1_down", ex.gather_d2d(*c2), ex.gather_ici(down))
    s.host("c2_mid", ex.gather_ici(gu, part=second))
    s.host("c2_pw2", ex.gather_d2d(gu, down))
    s.host("ffn2_up", ex.gather_ici(*a3))
    gu, down = ffn(3)
    s.host("ffn2_up", ex.gather_ici(gu, part=first))
    s.host("ffn2_down", ex.gather_d2d(*a3), ex.gather_ici(down))
    s.host("a3_in", ex.gather_ici(gu, part=second))
    s.host("a3_out", ex.gather_d2d(gu, down))

    def reduce_on(names, first_host, ici_hosts, last_host):
        s.host(first_host, ex.sibling_halves(*names))
        s.post(first_host, ex.add_halves(*names))
        for host, hosted, part in ici_hosts:
            s.host(host, ex.chip_sums(*hosted, part=part))
        s.host(last_host, ex.sibling_parts(*names))
        s.post(last_host, ex.adamw(*names))

    whole = (0, 1)
    gu, down = ffn(3)
    reduce_on((gu, down), "a3_out_bwd",
              [("a3_conv_bwd", (down,), whole), ("a3_dw_in", (gu,), first), ("a3_in_bwd", (gu,), second)], "ffn2_down_bwd")
    reduce_on(a3, "ffn2_down_bwd", [("ffn2_dw_gu", a3, whole)], "c2_pw2_bwd")
    gu, down = ffn(0)
    s.host("ffn0_dw_gu", ex.sibling_halves(down))
    s.post("ffn0_dw_gu", ex.add_halves(down))
    s.host("ffn0_up_bwd", ex.chip_sums(down))
    s.host("a0_out_bwd", ex.sibling_halves(gu, GROUPED))
    s.post("a0_out_bwd", ex.add_halves(gu, GROUPED))
    s.host("a0_conv_bwd", ex.chip_sums(gu, part=first), ex.chip_sums(GROUPED))
    s.host("a0_dw_in", ex.chip_sums(gu, part=second))
    s.host("a0_dw_out", ex.sibling_halves("a0_w_in"))
    s.post("a0_dw_out", ex.add_halves("a0_w_in"))
    s.host("a0_in_bwd", ex.chip_sums("a0_w_in"))
    s.host(f"adamw_{gu}", ex.chip_sums("a0_w_out"))
    s.host(f"adamw_{down}", ex.sibling_parts("a0_w_out"))
    reduce_on(ffn(2), "c2_pw2_bwd", [("c2_mid_bwd", ffn(2), whole)], "ffn1_down_bwd")
    reduce_on(c2, "ffn1_down_bwd", [("ffn1_dw_down", c2, whole)], "b1_bwd_mm")
    gu, down = ffn(1)
    reduce_on((gu, down), "b1_bwd_mm", [("ffn0_down_bwd", (down,), whole), ("ffn0_dw_gu", (gu,), whole)], "ffn0_up_bwd")
    return s


def kernel(x, *rest):
    nw = len(WEIGHT_NAMES)
    w = dict(zip(WEIGHT_NAMES, rest[:nw]))
    target = rest[nw]
    mom = dict(zip(WEIGHT_NAMES, rest[nw + 1:2 * nw + 1]))
    vel = dict(zip(WEIGHT_NAMES, rest[2 * nw + 1:3 * nw + 1]))
    cx, cy, cc = _position()
    my_chip = 2 * cx + cy
    ex = _Exchange(w, mom, vel, jnp.stack([my_chip, cc]).astype(jnp.int32))
    _ACTIVE_SCHEDULE[0] = _plan(ex)
    try:
        return _scheduled_step(x, target, w, mom, vel, ex, my_chip)
    finally:
        _ACTIVE_SCHEDULE[0] = None


def _scheduled_step(x, target, w, mom, vel, ex, my_chip):
    d = x.shape[-1]
    cq = d // N_CHIPS
    ex.cast(BIG)

    small_blk, small_counts = _pack_rows([w[n] for n in SMALL_SHARDED], cq)
    small_all = _allgather8(small_blk, "gather_small").reshape(N_CHIPS, 2, small_blk.shape[0], cq)[:, 0]
    small_parts = _unpack_rows(jnp.transpose(small_all, (1, 0, 2)), small_counts,
                               [(w[n].reshape(-1, cq).shape[0], N_CHIPS, cq) for n in SMALL_SHARDED])
    wts = ex.weights
    for n in REPLICATED:
        wts[n] = w[n].reshape(1, -1)
    for n, part in zip(SMALL_SHARDED, small_parts):
        wts[n] = part.reshape(part.shape[0], d)

    loss, dx, g = _device_step(x[0], target[0], wts, ex.grads)

    summed, last = ("ffn0_w_gu", "ffn0_w_down", GROUPED, "a0_w_in"), "a0_w_out"
    _comm_only([ex.sibling_parts(*summed)(), ex.sibling_halves(last)()], "reduce_tail_d2d")
    ex.add_halves(last)()
    ex.adamw(*summed)()
    ex.adamw(last)()
    sched = _ACTIVE_SCHEDULE[0]
    assert not sched.hosts and not sched.posts, (sched.hosts, sched.posts)

    rep_rows = [jnp.pad(g[n].reshape(-1, LANES), ((0, 0), (0, cq - LANES))) for n in REPLICATED]
    by_chip = [jnp.transpose(g[n].reshape(g[n].shape[0], N_CHIPS, cq), (1, 0, 2)) for n in SMALL_SHARDED]
    shard_rows = jnp.concatenate(by_chip, axis=1)
    n_rep, n_shard = sum(r.shape[0] for r in rep_rows), shard_rows.shape[1]
    loss_row = jnp.broadcast_to(loss, (1, cq))
    sm_blk = _pad_rows(jnp.concatenate(rep_rows + [loss_row, shard_rows.reshape(N_CHIPS * n_shard, cq)], axis=0))
    sm_sum = _sum_devices(_allgather8(sm_blk, "gather_small_grads"), "sum_small_grads")
    mine = lax.dynamic_slice_in_dim(sm_sum, n_rep + 1 + my_chip * n_shard, n_shard, axis=0)

    out = ex.updates
    params, at = [], {0: 0, 1: 0}
    for block, names in ((0, REPLICATED), (1, SMALL_SHARDED)):
        for n in names:
            params.append((w[n], mom[n], vel[n], block, at[block]))
            at[block] += w[n].size // LANES if w[n].ndim == 1 else w[n].shape[0]
    out.update(zip(REPLICATED + SMALL_SHARDED, _adamw_small([sm_sum, mine], params, "adamw_small")))

    total = sm_sum[n_rep, 0]
    grads, deltas, new_m, new_v = ([out[n][k].reshape(w[n].shape) for n in WEIGHT_NAMES] for k in range(4))
    return (total, dx.reshape(x.shape), *grads, *deltas, *new_m, *new_v)
```

```python
import functools

import jax
import jax.numpy as jnp
from jax import lax
from jax.experimental import pallas as pl
from jax.experimental.pallas import tpu as pltpu

F32 = jnp.float32
BF16 = jnp.bfloat16

RMS_EPS = 1e-6
LN_EPS = 1e-5
POOL_WINDOWS = (2, 4, 8, 16)
SHORT_CONV_W = 3
CONF_CONV_W = 31
N_CHIPS = 4
N_DEV = 8

ADAM_LR = 0.001
ADAM_B1 = 0.9
ADAM_B2 = 0.999
ADAM_EPS = 1e-08
ADAM_WD = 0.01
ADAM_STEP = 10

V7X_VMEM_BYTES = 64 * 1024 * 1024
VMEM_LIMIT = V7X_VMEM_BYTES - 8 * 1024 * 1024
LANES = 128
POOL_HALO = 16
SCONV_HALO = 16
CONF_HALO = 32


def _params(*sem):
    return pltpu.CompilerParams(dimension_semantics=sem, vmem_limit_bytes=VMEM_LIMIT)


def _tile(n, pref, mult=8):
    t = min(n, pref)
    while t > mult and (n % t or t % mult):
        t -= mult
    assert n % t == 0 and t % mult == 0, (n, pref, mult)
    return t


def _sigmoid(x):
    return jax.nn.sigmoid(x)


def _dot(a, b):
    return jnp.dot(a, b, preferred_element_type=F32)


def _dot_nt(a, b):
    return lax.dot_general(a, b, (((1,), (1,)), ((), ())), preferred_element_type=F32)


def _dot_tn(a, b):
    return lax.dot_general(a, b, (((0,), (0,)), ((), ())), preferred_element_type=F32)


def _colsum(x):
    return jnp.sum(x, axis=0, keepdims=True)


def _rms_stats(x):
    return lax.rsqrt(jnp.mean(x * x, axis=-1, keepdims=True) + RMS_EPS)


def _rms_bwd(du, x, gain):
    r = _rms_stats(x)
    xhat = x * r
    gdy = du * gain
    dx = r * (gdy - xhat * jnp.mean(gdy * xhat, axis=-1, keepdims=True))
    return dx, _colsum(du * xhat)


class _Task:
    def __init__(self, ins, out_shapes, aliases, sems, start, wait, done):
        self.ins, self.out_shapes, self.aliases, self.sems = list(ins), list(out_shapes), dict(aliases), list(sems)
        self.start, self.wait, self.done = start, wait, done


class _Schedule:
    def __init__(self):
        self.hosts, self.posts = {}, {}

    def host(self, kernel_name, *make_tasks):
        self.hosts.setdefault(kernel_name, []).extend(make_tasks)

    def post(self, kernel_name, *thunks):
        self.posts.setdefault(kernel_name, []).extend(thunks)

    def tasks_for(self, kernel_name):
        return [make() for make in self.hosts.pop(kernel_name, ())]

    def finished(self, kernel_name):
        for thunk in self.posts.pop(kernel_name, ()):
            thunk()


_ACTIVE_SCHEDULE = [None]


def _hosted(body, name, **kw):
    def run(*args):
        sched = _ACTIVE_SCHEDULE[0]
        tasks = sched.tasks_for(name) if sched is not None else []
        out = _call_with_tasks(body, name, tasks, kw, args) if tasks else pl.pallas_call(body, name=name, **kw)(*args)
        if sched is not None:
            sched.finished(name)
        return out

    return run


def _call_with_tasks(body, name, tasks, kw, args):
    spec = kw.get("grid_spec")
    n_pre = spec.num_scalar_prefetch if spec is not None else 0
    src = dict(grid=spec.grid, in_specs=spec.in_specs, out_specs=spec.out_specs) if spec is not None else kw
    pre, args = args[:n_pre], args[n_pre:]
    grid = tuple(src.get("grid", ()))
    single = not isinstance(kw["out_shape"], (list, tuple))
    out_shape = [kw["out_shape"]] if single else list(kw["out_shape"])
    out_specs = [src["out_specs"]] if single else list(src["out_specs"])
    scratch = list(kw.get("scratch_shapes", ()))
    n_in, n_out, n_scr = len(args), len(out_shape), len(scratch)
    t_in = [a for t in tasks for a in t.ins]
    t_out = [o for t in tasks for o in t.out_shapes]
    t_sem = [s for t in tasks for s in t.sems]
    aliases, at_in, at_out = {}, n_pre + n_in, n_out
    for t in tasks:
        for i, o in t.aliases.items():
            aliases[at_in + i] = at_out + o
        at_in += len(t.ins)
        at_out += len(t.out_shapes)

    def wrapped(*refs):
        pre_refs, refs = refs[:n_pre], refs[n_pre:]
        a = n_in
        b = a + len(t_in)
        c = b + n_out
        d = c + len(t_out)
        e = d + n_scr
        ins, tins, outs, touts, scr, tsems = refs[:a], refs[a:b], refs[b:c], refs[c:d], refs[d:e], refs[e:]
        views, i0, o0, s0 = [], 0, 0, 0
        for t in tasks:
            views.append((tins[i0:i0 + len(t.ins)], touts[o0:o0 + len(t.out_shapes)], tsems[s0:s0 + len(t.sems)]))
            i0, o0, s0 = i0 + len(t.ins), o0 + len(t.out_shapes), s0 + len(t.sems)

        def start_all():
            for t, v in zip(tasks, views):
                t.start(*v)

        def wait_all():
            for t, v in zip(tasks, views):
                t.wait(*v)

        if grid:
            first = functools.reduce(jnp.logical_and, [pl.program_id(i) == 0 for i in range(len(grid))])
            last = functools.reduce(jnp.logical_and, [pl.program_id(i) == grid[i] - 1 for i in range(len(grid))])
            pl.when(first)(start_all)
            body(*pre_refs, *ins, *outs, *scr)
            pl.when(last)(wait_all)
        else:
            start_all()
            body(*pre_refs, *ins, *outs, *scr)
            wait_all()

    in_specs = list(src["in_specs"]) + [ANY] * len(t_in)
    out_specs = out_specs + [ANY] * len(t_out)
    if spec is not None:
        layout = dict(grid_spec=pltpu.PrefetchScalarGridSpec(
            num_scalar_prefetch=n_pre, grid=grid, in_specs=in_specs, out_specs=out_specs, scratch_shapes=scratch + t_sem))
    else:
        layout = dict(grid=grid, in_specs=in_specs, out_specs=out_specs, scratch_shapes=scratch + t_sem)
    res = pl.pallas_call(
        wrapped, name=name, out_shape=out_shape + t_out, input_output_aliases=aliases,
        compiler_params=pltpu.CompilerParams(dimension_semantics=("arbitrary",) * len(grid), vmem_limit_bytes=VMEM_LIMIT),
        **layout,
    )(*pre, *args, *t_in)
    res = list(res)
    own, rest = res[:n_out], res[n_out:]
    for t in tasks:
        t.done(rest[:len(t.out_shapes)])
        rest = rest[len(t.out_shapes):]
    return own[0] if single else own


def _comm_only(tasks, name):
    _call_with_tasks(lambda: None, name, tasks, dict(grid=(), in_specs=[], out_specs=[], out_shape=[]), ())


def _mm_col(h, gain, w, bias, name):
    s, k = h.shape
    nsh, _, ns = w.shape
    tm = _tile(s, 512)
    has_bias = bias is not None

    def body(h_ref, gain_ref, w_ref, *rest):
        u_ref, o_ref = rest[-2:]
        x = h_ref[...]
        x = (x * _rms_stats(x) * gain_ref[...]).astype(BF16)
        u_ref[...] = x
        for j in range(nsh):
            cols = pl.ds(j * ns, ns)
            acc = _dot(x, w_ref[j])
            if has_bias:
                acc = acc + rest[0][:, cols]
            o_ref[:, cols] = acc.astype(o_ref.dtype)

    tokens = pl.BlockSpec((tm, k), lambda m: (m, 0))
    in_specs = [tokens, pl.BlockSpec((1, k), lambda m: (0, 0)), pl.BlockSpec((nsh, k, ns), lambda m: (0, 0, 0))]
    args = [h, gain, w]
    if has_bias:
        in_specs.append(pl.BlockSpec((1, nsh * ns), lambda m: (0, 0)))
        args.append(bias)
    return _hosted(
        body, name=name, grid=(s // tm,), in_specs=in_specs,
        out_specs=[tokens, pl.BlockSpec((tm, nsh * ns), lambda m: (m, 0))],
        out_shape=[jax.ShapeDtypeStruct((s, k), BF16), jax.ShapeDtypeStruct((s, nsh * ns), BF16)],
        compiler_params=_params("parallel"),
    )(*args)


def _mm_row(a, w, res, bias, name):
    s = a.shape[0]
    k, n = w.shape
    tm = _tile(s, 512)
    has_bias = bias is not None

    def body(a_ref, w_ref, res_ref, *rest):
        o_ref = rest[-1]
        y = res_ref[...] + _dot(a_ref[...], w_ref[...])
        if has_bias:
            y = y + rest[0][...]
        o_ref[...] = y

    in_specs = [pl.BlockSpec((tm, k), lambda m: (m, 0)), pl.BlockSpec((k, n), lambda m: (0, 0)),
                pl.BlockSpec((tm, n), lambda m: (m, 0))]
    args = [a, w, res]
    if has_bias:
        in_specs.append(pl.BlockSpec((1, n), lambda m: (0, 0)))
        args.append(bias)
    return _hosted(
        body, name=name, grid=(s // tm,), in_specs=in_specs,
        out_specs=pl.BlockSpec((tm, n), lambda m: (m, 0)),
        out_shape=jax.ShapeDtypeStruct((s, n), F32),
        compiler_params=_params("parallel"),
    )(*args)


def _mm_nt_row(dy, w, name):
    s, n = dy.shape
    k = w.shape[0]
    tm = _tile(s, 512)

    def body(dy_ref, w_ref, o_ref):
        o_ref[...] = _dot_nt(dy_ref[...].astype(BF16), w_ref[...])

    return _hosted(
        body, name=name, grid=(s // tm,),
        in_specs=[pl.BlockSpec((tm, n), lambda m: (m, 0)), pl.BlockSpec((k, n), lambda m: (0, 0))],
        out_specs=pl.BlockSpec((tm, k), lambda m: (m, 0)),
        out_shape=jax.ShapeDtypeStruct((s, k), F32),
        compiler_params=_params("parallel"),
    )(dy, w)


def _ffn_up(h, gain, w, name):
    s, d = h.shape
    _, _, ns = w.shape
    tm = _tile(s, 512)

    def body(h_ref, gain_ref, wg_ref, wu_ref, u_ref, act_ref, s1_ref, q1_ref):
        x = h_ref[...]
        x = (x * _rms_stats(x) * gain_ref[...]).astype(BF16)

        @pl.when(pl.program_id(0) == 0)
        def _():
            u_ref[...] = x

        g = _dot(x, wg_ref[...])
        up = _dot(x, wu_ref[...])
        sg = _sigmoid(g)
        s1 = g * sg
        act_ref[...] = (s1 * up).astype(act_ref.dtype)
        s1_ref[...] = s1.astype(s1_ref.dtype)
        q1_ref[...] = (up * sg * (1.0 + g * (1.0 - sg))).astype(q1_ref.dtype)

    out = pl.BlockSpec((tm, ns), lambda j, m: (m, j))
    tokens = pl.BlockSpec((tm, d), lambda j, m: (m, 0))
    nm = s // tm
    u_once = pl.BlockSpec((tm, d), lambda j, m: (jnp.where(j == 0, m, nm - 1), 0))
    return _hosted(
        body, name=name, grid=(2, nm),
        in_specs=[tokens, pl.BlockSpec((1, d), lambda j, m: (0, 0)), pl.BlockSpec((None, d, ns), lambda j, m: (j, 0, 0)),
                  pl.BlockSpec((None, d, ns), lambda j, m: (j + 2, 0, 0))],
        out_specs=[u_once, out, out, out],
        out_shape=[jax.ShapeDtypeStruct((s, d), BF16)] + [jax.ShapeDtypeStruct((s, 2 * ns), BF16)] * 3,
        compiler_params=_params("arbitrary", "arbitrary"),
    )(h, gain, w, w)


def _ffn_down_bwd(dh, w, s1, q1, name):
    s, d = dh.shape
    f = w.shape[0]
    tm = _tile(s, 256)

    def body(dh_ref, w_ref, s1_ref, q1_ref, o_ref):
        da = _dot_nt(dh_ref[...].astype(BF16), w_ref[...])
        o_ref[:, :f] = (da * q1_ref[...].astype(F32)).astype(o_ref.dtype)
        o_ref[:, f:] = (da * s1_ref[...].astype(F32)).astype(o_ref.dtype)

    return _hosted(
        body, name=name, grid=(s // tm,),
        in_specs=[pl.BlockSpec((tm, d), lambda m: (m, 0)), pl.BlockSpec((f, d), lambda m: (0, 0)),
                  pl.BlockSpec((tm, f), lambda m: (m, 0)), pl.BlockSpec((tm, f), lambda m: (m, 0))],
        out_specs=pl.BlockSpec((tm, 2 * f), lambda m: (m, 0)),
        out_shape=jax.ShapeDtypeStruct((s, 2 * f), BF16),
        compiler_params=_params("parallel"),
    )(dh, w, s1, q1)


def _mm_nt_col_rms_bwd(dy, w, h, gain, dh, name):
    s = dy.shape[0]
    nsh, k, ns = w.shape
    tm = _tile(s, 256)

    def body(dy_ref, w_ref, h_ref, g_ref, dh_ref, o_ref, dg_ref):
        du = _dot_nt(dy_ref[:, :ns], w_ref[0])
        for j in range(1, nsh):
            du = du + _dot_nt(dy_ref[:, j * ns:(j + 1) * ns], w_ref[j])
        dx, dg = _rms_bwd(du, h_ref[...], g_ref[...])
        o_ref[...] = dh_ref[...] + dx
        _accumulate(dg_ref, dg, pl.program_id(0) == 0)

    return _hosted(
        body, name=name, grid=(s // tm,),
        in_specs=[pl.BlockSpec((tm, nsh * ns), lambda m: (m, 0)), pl.BlockSpec((nsh, k, ns), lambda m: (0, 0, 0)),
                  pl.BlockSpec((tm, k), lambda m: (m, 0)), pl.BlockSpec((1, k), lambda m: (0, 0)),
                  pl.BlockSpec((tm, k), lambda m: (m, 0))],
        out_specs=[pl.BlockSpec((tm, k), lambda m: (m, 0)), pl.BlockSpec((1, k), lambda m: (0, 0))],
        out_shape=[jax.ShapeDtypeStruct((s, k), F32), jax.ShapeDtypeStruct((1, k), F32)],
        compiler_params=_params("arbitrary"),
    )(dy, w, h, gain, dh)


def _mm_tn(a, dy, nsh, name):
    s, k = a.shape
    ns = dy.shape[1] // nsh
    tm = _tile(s, 1024)
    tk = _tile(k, 1408, LANES)
    nk, nm = k // tk, s // tm

    def body(a_ref, dy_ref, o_ref, acc_ref):
        m = pl.program_id(2)
        part = _dot_tn(a_ref[...], dy_ref[...].astype(BF16))

        @pl.when(m == 0)
        def _():
            acc_ref[...] = part

        @pl.when(m > 0)
        def _():
            acc_ref[...] += part

        @pl.when(m == nm - 1)
        def _():
            o_ref[...] = acc_ref[...].astype(o_ref.dtype)

    return _hosted(
        body, name=name, grid=(nsh, nk, nm),
        in_specs=[pl.BlockSpec((tm, tk), lambda j, kk, m: (m, kk)), pl.BlockSpec((tm, ns), lambda j, kk, m: (m, j))],
        out_specs=pl.BlockSpec((None, tk, ns), lambda j, kk, m: (j, kk, 0)),
        out_shape=jax.ShapeDtypeStruct((nsh, k, ns), BF16),
        scratch_shapes=[pltpu.VMEM((tk, ns), F32)],
        compiler_params=_params("parallel", "parallel", "arbitrary"),
    )(a, dy)


def _main_spec(tm, w):
    return pl.BlockSpec((tm, w), lambda m: (m, 0))


def _before_spec(tm, hb, w):
    return pl.BlockSpec((hb, w), lambda m: (jnp.maximum(m * (tm // hb) - 1, 0), 0))


def _after_spec(tm, hb, w, s):
    return pl.BlockSpec((hb, w), lambda m: (jnp.minimum((m + 1) * (tm // hb), s // hb - 1), 0))


def _row_spec(w, rows=1):
    return pl.BlockSpec((rows, w), lambda m: (0, 0))


CHUNK_LANES = 4 * LANES
CHUNK_ROWS = 32


def _build_shifts(ext8_ref, residues=range(1, 8)):
    n = ext8_ref.shape[1] - 8
    for r in residues:
        ext8_ref[r, pl.ds(0, n), :] = ext8_ref[0, pl.ds(r, n), :]


def _fold_rows(x):
    return functools.reduce(lambda p, q: p + q, [x[i:i + 8] for i in range(0, x.shape[0], 8)])


def _shifted(ext8_ref, shift, r0, rows, cols):
    return ext8_ref[shift % 8, pl.ds(pl.multiple_of(shift - shift % 8 + r0, 8), rows), cols]


def _lane_chunk(i):
    return pl.ds(pl.multiple_of(i * CHUNK_LANES, CHUNK_LANES), CHUNK_LANES)


def _sum_terms(terms, ways=4):
    accs = []
    for i, t in enumerate(terms):
        if i < ways:
            accs.append(t)
        else:
            accs[i % ways] = accs[i % ways] + t
    while len(accs) > 1:
        accs = [accs[i] + accs[i + 1] if i + 1 < len(accs) else accs[i] for i in range(0, len(accs), 2)]
    return accs[0]


def _accumulate(ref, val, first):
    @pl.when(first)
    def _():
        ref[...] = val

    @pl.when(jnp.logical_not(first))
    def _():
        ref[...] += val


SCONV_Z_SHIFTS = tuple(SCONV_HALO - (SHORT_CONV_W - 1) + k for k in range(SHORT_CONV_W))


def _sconv_z_taps(zext_ref, r0, cols):
    return [_shifted(zext_ref, shift, r0, CHUNK_ROWS, cols) for shift in SCONV_Z_SHIFTS]


def _weighted(cw_ref, cols, terms):
    return _sum_terms((cw_ref[k:k + 1, cols] * t for k, t in enumerate(terms)), ways=len(terms))


def _sconv_fill_z(zext_ref, main_ref, before_ref, d, m):
    hb = SCONV_HALO
    zb = before_ref[:, d:2 * d].astype(F32) * before_ref[:, 2 * d:].astype(F32)
    zext_ref[pl.ds(0, hb), :] = jnp.where(m > 0, zb, 0.0)
    zext_ref[pl.ds(hb, main_ref.shape[0]), :] = main_ref[:, d:2 * d].astype(F32) * main_ref[:, 2 * d:].astype(F32)


def _sconv_fwd(bcv, cw, name):
    s, d3 = bcv.shape
    d = d3 // 3
    tm = _tile(s, 256, CHUNK_ROWS)
    row_chunks = tm // CHUNK_ROWS

    def body(main_ref, before_ref, cw_ref, p_ref, zext_ref):
        m = pl.program_id(0)
        _sconv_fill_z(zext_ref.at[0], main_ref, before_ref, d, m)
        _build_shifts(zext_ref, [shift % 8 for shift in SCONV_Z_SHIFTS if shift % 8])

        def chunk(i, carry):
            cols = _lane_chunk(i // row_chunks)
            r0 = pl.multiple_of((i % row_chunks) * CHUNK_ROWS, CHUNK_ROWS)
            rows = pl.ds(r0, CHUNK_ROWS)
            zc = _weighted(cw_ref, cols, _sconv_z_taps(zext_ref, r0, cols))
            p_ref[rows, cols] = (main_ref[rows, cols].astype(F32) * zc).astype(p_ref.dtype)
            return carry

        lax.fori_loop(0, row_chunks * (d // CHUNK_LANES), chunk, 0)

    return _hosted(
        body, name=name, grid=(s // tm,),
        in_specs=[_main_spec(tm, d3), _before_spec(tm, SCONV_HALO, d3), _row_spec(d, SHORT_CONV_W)],
        out_specs=_main_spec(tm, d),
        out_shape=jax.ShapeDtypeStruct((s, d), BF16),
        scratch_shapes=[pltpu.VMEM((8, tm + SCONV_HALO, d), F32)],
        compiler_params=_params("parallel"),
    )(bcv, bcv, cw)


def _sconv_bwd(dp, bcv, cw, name):
    s, d3 = bcv.shape
    d = d3 // 3
    tm = _tile(s, 256, CHUNK_ROWS)
    nm = s // tm
    ha = 8
    kw = SHORT_CONV_W

    def body(dp_ref, dpa_ref, main_ref, before_ref, after_ref, cw_ref, o_ref, dcw_ref, zext_ref, dext_ref):
        m = pl.program_id(0)
        _sconv_fill_z(zext_ref.at[0], main_ref, before_ref, d, m)
        _build_shifts(zext_ref, [shift % 8 for shift in SCONV_Z_SHIFTS if shift % 8])
        dext_ref[0, pl.ds(0, tm), :] = dp_ref[...] * main_ref[:, :d].astype(F32)
        dza = dpa_ref[...] * after_ref[:, :d].astype(F32)[0:ha]
        dext_ref[0, pl.ds(tm, ha), :] = jnp.where(m < nm - 1, dza, 0.0)
        _build_shifts(dext_ref, range(1, kw))

        @pl.when(m == 0)
        def _():
            dcw_ref[...] = jnp.zeros_like(dcw_ref)

        zero = jnp.zeros((8, CHUNK_LANES), F32)

        def lane_chunk(ci, carry):
            cols = _lane_chunk(ci)
            c_cols, v_cols = (pl.ds(pl.multiple_of(part * d + ci * CHUNK_LANES, CHUNK_LANES), CHUNK_LANES) for part in (1, 2))

            def row_chunk(ri, sums):
                r0 = pl.multiple_of(ri * CHUNK_ROWS, CHUNK_ROWS)
                rows = pl.ds(r0, CHUNK_ROWS)
                z = _sconv_z_taps(zext_ref, r0, cols)
                o_ref[rows, cols] = (dp_ref[rows, cols] * _weighted(cw_ref, cols, z)).astype(o_ref.dtype)
                dzc = [_shifted(dext_ref, kw - 1 - k, r0, CHUNK_ROWS, cols) for k in range(kw)]
                dz = _weighted(cw_ref, cols, dzc)
                o_ref[rows, c_cols] = (dz * main_ref[rows, v_cols].astype(F32)).astype(o_ref.dtype)
                o_ref[rows, v_cols] = (dz * main_ref[rows, c_cols].astype(F32)).astype(o_ref.dtype)
                return tuple(acc + _fold_rows(dzc[kw - 1] * z[k]) for k, acc in enumerate(sums))

            sums = lax.fori_loop(0, tm // CHUNK_ROWS, row_chunk, (zero,) * kw)
            for k in range(kw):
                dcw_ref[k:k + 1, cols] += _colsum(sums[k])
            return carry

        lax.fori_loop(0, d // CHUNK_LANES, lane_chunk, 0)

    return _hosted(
        body, name=name, grid=(nm,),
        in_specs=[_main_spec(tm, d), _after_spec(tm, ha, d, s), _main_spec(tm, d3), _before_spec(tm, SCONV_HALO, d3),
                  _after_spec(tm, SCONV_HALO, d3, s), _row_spec(d, SHORT_CONV_W)],
        out_specs=[_main_spec(tm, d3), _row_spec(d, 8)],
        out_shape=[jax.ShapeDtypeStruct((s, d3), BF16), jax.ShapeDtypeStruct((8, d), F32)],
        scratch_shapes=[pltpu.VMEM((8, tm + SCONV_HALO, d), F32), pltpu.VMEM((8, tm + ha, d), F32)],
        compiler_params=_params("arbitrary"),
    )(dp, dp, bcv, bcv, bcv, cw)


def _pool_counts(t0, tm, w):
    t = t0 + lax.broadcasted_iota(jnp.int32, (tm, 1), 0)
    return jnp.minimum(t + 1, w).astype(F32)


def _pool_fwd(h, gain, wg, scale, name):
    s, d = h.shape
    ng, cg, _ = wg.shape
    tm = _tile(s, 512, POOL_HALO)

    def body(h_ref, hb_ref, g_ref, wg_ref, sc_ref, o_ref, mx_ref, uext_ref):
        m = pl.program_id(0)
        x = h_ref[...]
        gain_row = g_ref[...]
        xb = hb_ref[...]
        uext_ref[pl.ds(0, POOL_HALO), :] = jnp.where(m > 0, xb * _rms_stats(xb) * gain_row, 0.0)
        uext_ref[pl.ds(POOL_HALO, tm), :] = x * _rms_stats(x) * gain_row
        for gi, win in enumerate(POOL_WINDOWS):
            cols = pl.ds(gi * cg, cg)
            u_g = uext_ref[pl.ds(POOL_HALO, tm), cols]
            acc = u_g
            for i in range(1, win):
                acc = acc + uext_ref[pl.ds(POOL_HALO - i, tm), cols]
            mixed = (acc / _pool_counts(m * tm, tm, win) - u_g).astype(BF16)
            mx_ref[:, cols] = mixed
            o_ref[:, cols] = x[:, gi * cg:(gi + 1) * cg] + _dot(mixed, wg_ref[gi]) * sc_ref[:, cols]

    return _hosted(
        body, name=name, grid=(s // tm,),
        in_specs=[_main_spec(tm, d), _before_spec(tm, POOL_HALO, d), _row_spec(d),
                  pl.BlockSpec((ng, cg, cg), lambda m: (0, 0, 0)), _row_spec(d)],
        out_specs=[_main_spec(tm, d), _main_spec(tm, d)],
        out_shape=[jax.ShapeDtypeStruct((s, d), F32), jax.ShapeDtypeStruct((s, d), BF16)],
        scratch_shapes=[pltpu.VMEM((tm + POOL_HALO, d), F32)],
        compiler_params=_params("parallel"),
    )(h, h, gain, wg, scale)


def _pool_bwd_mm(dh, mixed, wg, scale, name):
    s, d = dh.shape
    ng, cg, _ = wg.shape
    tm = _tile(s, 512)

    def body(dh_ref, mx_ref, wg_ref, sc_ref, dmx_ref, dwg_ref, dsc_ref):
        first = pl.program_id(0) == 0
        for gi in range(ng):
            cols = pl.ds(gi * cg, cg)
            dh_g = dh_ref[:, cols]
            mixed = mx_ref[:, cols]
            w_g = wg_ref[gi]
            dy = (dh_g * sc_ref[:, cols]).astype(BF16)
            dmx_ref[:, cols] = _dot_nt(dy, w_g)
            _accumulate(dsc_ref.at[:, cols], _colsum(dh_g * _dot(mixed, w_g)), first)
            _accumulate(dwg_ref.at[gi], _dot_tn(mixed, dy), first)

    return _hosted(
        body, name=name, grid=(s // tm,),
        in_specs=[_main_spec(tm, d), _main_spec(tm, d), pl.BlockSpec((ng, cg, cg), lambda m: (0, 0, 0)), _row_spec(d)],
        out_specs=[_main_spec(tm, d), pl.BlockSpec((ng, cg, cg), lambda m: (0, 0, 0)), _row_spec(d)],
        out_shape=[jax.ShapeDtypeStruct((s, d), F32), jax.ShapeDtypeStruct((ng, cg, cg), F32),
                   jax.ShapeDtypeStruct((1, d), F32)],
        compiler_params=_params("arbitrary"),
    )(dh, mixed, wg, scale)


def _pool_bwd_rms(dmixed, h, gain, dh, name):
    s, d = h.shape
    cg = d // len(POOL_WINDOWS)
    tm = _tile(s, 512, POOL_HALO)
    nm = s // tm

    def body(dmx_ref, dmxa_ref, h_ref, g_ref, dh_ref, o_ref, dg_ref, eext_ref, du_ref):
        m = pl.program_id(0)
        for gi, win in enumerate(POOL_WINDOWS):
            cols = pl.ds(gi * cg, cg)
            dmx = dmx_ref[:, cols]
            eext_ref[pl.ds(0, tm), cols] = dmx / _pool_counts(m * tm, tm, win)
            ea = dmxa_ref[:, cols] / _pool_counts((m + 1) * tm, POOL_HALO, win)
            eext_ref[pl.ds(tm, POOL_HALO), cols] = jnp.where(m < nm - 1, ea, 0.0)
            acc = -dmx
            for i in range(win):
                acc = acc + eext_ref[pl.ds(i, tm), cols]
            du_ref[:, cols] = acc
        dx, dg = _rms_bwd(du_ref[...], h_ref[...], g_ref[...])
        o_ref[...] = dh_ref[...] + dx
        _accumulate(dg_ref, dg, m == 0)

    return _hosted(
        body, name=name, grid=(nm,),
        in_specs=[_main_spec(tm, d), _after_spec(tm, POOL_HALO, d, s), _main_spec(tm, d), _row_spec(d), _main_spec(tm, d)],
        out_specs=[_main_spec(tm, d), _row_spec(d)],
        out_shape=[jax.ShapeDtypeStruct((s, d), F32), jax.ShapeDtypeStruct((1, d), F32)],
        scratch_shapes=[pltpu.VMEM((tm + POOL_HALO, d), F32), pltpu.VMEM((tm, d), F32)],
        compiler_params=_params("arbitrary"),
    )(dmixed, dmixed, h, gain, dh)


def _conf_fill_h(hext_ref, main_ref, before_ref, d, m):
    hb = before_ref[:, :d].astype(F32) * _sigmoid(before_ref[:, d:].astype(F32))
    hext_ref[pl.ds(0, CONF_HALO), :] = jnp.where(m > 0, hb, 0.0)
    hext_ref[pl.ds(CONF_HALO, main_ref.shape[0]), :] = main_ref[:, :d].astype(F32) * _sigmoid(main_ref[:, d:].astype(F32))


def _layernorm_parts(hc, g, b):
    mu = jnp.mean(hc, axis=-1, keepdims=True)
    xc = hc - mu
    rs = lax.rsqrt(jnp.mean(xc * xc, axis=-1, keepdims=True) + LN_EPS)
    xhat = xc * rs
    return xhat, rs, xhat * g + b


def _conf_mid_fwd(ag, dw, b_dw, ln_g, ln_b, name):
    s, d2 = ag.shape
    d = d2 // 2
    tm = _tile(s, 256, CONF_HALO)
    base = CONF_HALO - (CONF_CONV_W - 1)

    def body(main_ref, before_ref, dw_ref, bdw_ref, g_ref, b_ref, s_ref, hc_ref, hext_ref):
        m = pl.program_id(0)
        _conf_fill_h(hext_ref.at[0], main_ref, before_ref, d, m)
        _build_shifts(hext_ref)
        row_chunks = tm // CHUNK_ROWS

        def conv_chunk(i, carry):
            cols = _lane_chunk(i // row_chunks)
            r0 = pl.multiple_of((i % row_chunks) * CHUNK_ROWS, CHUNK_ROWS)
            taps = (dw_ref[kk:kk + 1, cols] * _shifted(hext_ref, base + kk, r0, CHUNK_ROWS, cols) for kk in range(CONF_CONV_W))
            hc_ref[pl.ds(r0, CHUNK_ROWS), cols] = bdw_ref[:, cols] + _sum_terms(taps, ways=1)
            return carry

        lax.fori_loop(0, row_chunks * (d // CHUNK_LANES), conv_chunk, 0)
        _, _, l = _layernorm_parts(hc_ref[...], g_ref[...], b_ref[...])
        s_ref[...] = (l * _sigmoid(l)).astype(s_ref.dtype)

    return _hosted(
        body, name=name, grid=(s // tm,),
        in_specs=[_main_spec(tm, d2), _before_spec(tm, CONF_HALO, d2), _row_spec(d, CONF_CONV_W), _row_spec(d),
                  _row_spec(d), _row_spec(d)],
        out_specs=[_main_spec(tm, d), _main_spec(tm, d)],
        out_shape=[jax.ShapeDtypeStruct((s, d), BF16), jax.ShapeDtypeStruct((s, d), F32)],
        scratch_shapes=[pltpu.VMEM((8, tm + CONF_HALO, d), F32)],
        compiler_params=_params("parallel"),
    )(ag, ag, dw, b_dw, ln_g, ln_b)


def _conf_out_bwd(dh, w, hc, ln_g, ln_b, name):
    s, d = dh.shape
    tm = _tile(s, 256)

    def body(dh_ref, w_ref, hc_ref, g_ref, b_ref, o_ref, dg_ref, db_ref, dbo_ref):
        first = pl.program_id(0) == 0
        dh_t = dh_ref[...]
        ds = _dot_nt(dh_t.astype(BF16), w_ref[...])
        xhat, rs, l = _layernorm_parts(hc_ref[...], g_ref[...], b_ref[...])
        sg = _sigmoid(l)
        dl = ds * sg * (1.0 + l * (1.0 - sg))
        dxh = dl * g_ref[...]
        o_ref[...] = rs * (dxh - jnp.mean(dxh, axis=-1, keepdims=True)
                           - xhat * jnp.mean(dxh * xhat, axis=-1, keepdims=True))
        _accumulate(dg_ref, _colsum(dl * xhat), first)
        _accumulate(db_ref, _colsum(dl), first)
        _accumulate(dbo_ref, _colsum(dh_t), first)

    return _hosted(
        body, name=name, grid=(s // tm,),
        in_specs=[_main_spec(tm, d), pl.BlockSpec((d, d), lambda m: (0, 0)), _main_spec(tm, d), _row_spec(d), _row_spec(d)],
        out_specs=[_main_spec(tm, d), _row_spec(d), _row_spec(d), _row_spec(d)],
        out_shape=[jax.ShapeDtypeStruct((s, d), F32)] + [jax.ShapeDtypeStruct((1, d), F32)] * 3,
        compiler_params=_params("arbitrary"),
    )(dh, w, hc, ln_g, ln_b)


def _conf_mid_bwd(dhc, ag, dw, name):
    s, d2 = ag.shape
    d = d2 // 2
    tm = _tile(s, 256, CONF_HALO)
    nm = s // tm
    kw = CONF_CONV_W
    base = CONF_HALO - (kw - 1)

    def body(dhc_ref, dhca_ref, main_ref, before_ref, dw_ref, o_ref, ddw_ref, dbdw_ref, dbpw_ref, hext_ref, dext_ref):
        m = pl.program_id(0)
        first = m == 0
        _conf_fill_h(hext_ref.at[0], main_ref, before_ref, d, m)
        _build_shifts(hext_ref)
        dext_ref[0, pl.ds(0, tm), :] = dhc_ref[...]
        dext_ref[0, pl.ds(tm, CONF_HALO), :] = jnp.where(m < nm - 1, dhca_ref[...], 0.0)
        _build_shifts(dext_ref)

        @pl.when(first)
        def _():
            ddw_ref[...] = jnp.zeros_like(ddw_ref)
            dbdw_ref[...] = jnp.zeros_like(dbdw_ref)
            dbpw_ref[...] = jnp.zeros_like(dbpw_ref)

        zero = jnp.zeros((8, CHUNK_LANES), F32)
        tap_group = 8

        def fold(x):
            return functools.reduce(lambda p, q: p + q, [x[i:i + 8] for i in range(0, CHUNK_ROWS, 8)])

        def lane_chunk(ci, carry):
            cols = _lane_chunk(ci)
            gate_cols = pl.ds(pl.multiple_of(d + ci * CHUNK_LANES, CHUNK_LANES), CHUNK_LANES)

            def through_conv(ri, sums):
                r0 = pl.multiple_of(ri * CHUNK_ROWS, CHUNK_ROWS)
                rows = pl.ds(r0, CHUNK_ROWS)
                dhh = _sum_terms((dw_ref[kk:kk + 1, cols] * _shifted(dext_ref, kw - 1 - kk, r0, CHUNK_ROWS, cols)
                                  for kk in range(kw)), ways=1)
                a = main_ref[rows, cols].astype(F32)
                sg = _sigmoid(main_ref[rows, gate_cols].astype(F32))
                da = dhh * sg
                dgate = dhh * a * sg * (1.0 - sg)
                o_ref[rows, cols] = da.astype(o_ref.dtype)
                o_ref[rows, gate_cols] = dgate.astype(o_ref.dtype)
                return sums[0] + fold(da), sums[1] + fold(dgate), sums[2] + fold(dext_ref[0, rows, cols])

            sum_da, sum_dgate, sum_dhc = lax.fori_loop(0, tm // CHUNK_ROWS, through_conv, (zero, zero, zero))
            dbdw_ref[:, cols] += _colsum(sum_dhc)
            dbpw_ref[:, cols] += _colsum(sum_da)
            dbpw_ref[:, gate_cols] += _colsum(sum_dgate)

            for k0 in range(0, kw, tap_group):
                group = range(k0, min(k0 + tap_group, kw))

                def tap_gradients(ri, accs, group=group):
                    for sub in range(0, CHUNK_ROWS, 8):
                        r0 = pl.multiple_of(ri * CHUNK_ROWS + sub, 8)
                        dhc_c = dext_ref[0, pl.ds(r0, 8), cols]
                        accs = tuple(acc + dhc_c * _shifted(hext_ref, base + kk, r0, 8, cols) for kk, acc in zip(group, accs))
                    return accs

                accs = lax.fori_loop(0, tm // CHUNK_ROWS, tap_gradients, (zero,) * len(group))
                for kk, acc in zip(group, accs):
                    ddw_ref[kk:kk + 1, cols] += _colsum(acc)
            return carry

        lax.fori_loop(0, d // CHUNK_LANES, lane_chunk, 0)

    return _hosted(
        body, name=name, grid=(nm,),
        in_specs=[_main_spec(tm, d), _after_spec(tm, CONF_HALO, d, s), _main_spec(tm, d2), _before_spec(tm, CONF_HALO, d2),
                  _row_spec(d, kw)],
        out_specs=[_main_spec(tm, d2), _row_spec(d, 32), _row_spec(d), _row_spec(d2)],
        out_shape=[jax.ShapeDtypeStruct((s, d2), BF16), jax.ShapeDtypeStruct((32, d), F32),
                   jax.ShapeDtypeStruct((1, d), F32), jax.ShapeDtypeStruct((1, d2), F32)],
        scratch_shapes=[pltpu.VMEM((8, tm + CONF_HALO, d), F32), pltpu.VMEM((8, tm + CONF_HALO, d), F32)],
        compiler_params=_params("arbitrary"),
    )(dhc, dhc, ag, ag, dw)


def _loss_head(h, gain, target, name):
    s, d = h.shape
    tm = _tile(s, 512)

    def body(h_ref, g_ref, t_ref, loss_ref, dh_ref, dg_ref):
        first = pl.program_id(0) == 0
        x = h_ref[...]
        err = x * _rms_stats(x) * g_ref[...] - t_ref[...]
        part = 0.5 * jnp.sum(jnp.mean(err * err, axis=-1, keepdims=True), axis=0, keepdims=True)
        dx, dg = _rms_bwd(err * (1.0 / d), x, g_ref[...])
        dh_ref[...] = dx
        _accumulate(loss_ref, part, first)
        _accumulate(dg_ref, dg, first)

    return _hosted(
        body, name=name, grid=(s // tm,),
        in_specs=[_main_spec(tm, d), _row_spec(d), _main_spec(tm, d)],
        out_specs=[pl.BlockSpec((1, 1), lambda m: (0, 0)), _main_spec(tm, d), _row_spec(d)],
        out_shape=[jax.ShapeDtypeStruct((1, 1), F32), jax.ShapeDtypeStruct((s, d), F32), jax.ShapeDtypeStruct((1, d), F32)],
        compiler_params=_params("arbitrary"),
    )(h, gain, target)


def _ffn_fwd(h, wts, i):
    u, act, s1, q1 = _ffn_up(h, wts[f"ln2_{i}"], wts[f"ffn{i}_w_gu"], f"ffn{i}_up")
    h_new = _mm_row(act, wts[f"ffn{i}_w_down"], h, None, f"ffn{i}_down")
    return h_new, (h, u, act, s1, q1)


def _ffn_bwd(dh, saved, wts, i, g):
    h, u, act, s1, q1 = saved
    dgu = _ffn_down_bwd(dh, wts[f"ffn{i}_w_down"], s1, q1, f"ffn{i}_down_bwd")
    g[f"ffn{i}_w_down"] = _mm_tn(act, dh, 1, f"ffn{i}_dw_down")
    g[f"ffn{i}_w_gu"] = _mm_tn(u, dgu, N_CHIPS, f"ffn{i}_dw_gu")
    dh_new, g[f"ln2_{i}"] = _mm_nt_col_rms_bwd(dgu, wts[f"ffn{i}_w_gu"], h, wts[f"ln2_{i}"], dh, f"ffn{i}_up_bwd")
    return dh_new


def _device_step(x, target, wts, g=None):
    g = {} if g is None else g
    saved = {}
    h = x

    def short_conv_fwd(h, i):
        u, bcv = _mm_col(h, wts[f"ln1_{i}"], wts[f"a{i}_w_in"], None, f"a{i}_in")
        p = _sconv_fwd(bcv, wts[f"a{i}_conv"], f"a{i}_conv")
        return _mm_row(p, wts[f"a{i}_w_out"], h, None, f"a{i}_out"), (h, u, bcv, p)

    def short_conv_bwd(dh, sv, i):
        h, u, bcv, p = sv
        dp = _mm_nt_row(dh, wts[f"a{i}_w_out"], f"a{i}_out_bwd")
        dbcv, dcw = _sconv_bwd(dp, bcv, wts[f"a{i}_conv"], f"a{i}_conv_bwd")
        g[f"a{i}_conv"] = dcw[:SHORT_CONV_W]
        g[f"a{i}_w_in"] = _mm_tn(u, dbcv, N_CHIPS, f"a{i}_dw_in")
        g[f"a{i}_w_out"] = _mm_tn(p, dh, 1, f"a{i}_dw_out")
        dh, g[f"ln1_{i}"] = _mm_nt_col_rms_bwd(dbcv, wts[f"a{i}_w_in"], h, wts[f"ln1_{i}"], dh, f"a{i}_in_bwd")
        return dh

    h, saved["a0"] = short_conv_fwd(h, 0)
    h, saved["f0"] = _ffn_fwd(h, wts, 0)

    h_in = h
    h, mixed = _pool_fwd(h, wts["ln1_1"], wts["b1_w_grp"], wts["b1_scale"], "b1_fwd")
    saved["b1"] = (h_in, mixed)
    h, saved["f1"] = _ffn_fwd(h, wts, 1)

    h_in = h
    u, ag = _mm_col(h, wts["ln1_2"], wts["c2_w_pw1"], wts["c2_b_pw1"], "c2_pw1")
    sw, hc = _conf_mid_fwd(ag, wts["c2_dw"], wts["c2_b_dw"], wts["c2_ln_g"], wts["c2_ln_b"], "c2_mid")
    h = _mm_row(sw, wts["c2_w_pw2"], h, wts["c2_b_pw2"], "c2_pw2")
    saved["c2"] = (h_in, u, ag, sw, hc)
    h, saved["f2"] = _ffn_fwd(h, wts, 2)

    h, saved["a3"] = short_conv_fwd(h, 3)
    h, saved["f3"] = _ffn_fwd(h, wts, 3)

    loss, dh, g["ln_f"] = _loss_head(h, wts["ln_f"], target, "loss_head")

    def ffn_bwd(dh, i):
        return _ffn_bwd(dh, saved[f"f{i}"], wts, i, g)

    dh = ffn_bwd(dh, 3)
    dh = short_conv_bwd(dh, saved["a3"], 3)

    dh = ffn_bwd(dh, 2)
    h_in, u, ag, sw, hc = saved["c2"]
    dhc, g["c2_ln_g"], g["c2_ln_b"], g["c2_b_pw2"] = _conf_out_bwd(
        dh, wts["c2_w_pw2"], hc, wts["c2_ln_g"], wts["c2_ln_b"], "c2_pw2_bwd")
    g["c2_w_pw2"] = _mm_tn(sw, dh, 1, "c2_dw_pw2")
    dag, ddw, g["c2_b_dw"], g["c2_b_pw1"] = _conf_mid_bwd(dhc, ag, wts["c2_dw"], "c2_mid_bwd")
    g["c2_dw"] = ddw[:CONF_CONV_W]
    g["c2_w_pw1"] = _mm_tn(u, dag, N_CHIPS, "c2_dw_pw1")
    dh, g["ln1_2"] = _mm_nt_col_rms_bwd(dag, wts["c2_w_pw1"], h_in, wts["ln1_2"], dh, "c2_pw1_bwd")

    dh = ffn_bwd(dh, 1)
    h_in, mixed = saved["b1"]
    dmixed, g["b1_w_grp"], g["b1_scale"] = _pool_bwd_mm(dh, mixed, wts["b1_w_grp"], wts["b1_scale"], "b1_bwd_mm")
    dh, g["ln1_1"] = _pool_bwd_rms(dmixed, h_in, wts["ln1_1"], dh, "b1_bwd_rms")

    dh = ffn_bwd(dh, 0)
    dh = short_conv_bwd(dh, saved["a0"], 0)
    return loss, dh, g


MESH = pl.DeviceIdType.MESH
ANY = pl.BlockSpec(memory_space=pl.ANY)


def _position():
    return lax.axis_index("x"), lax.axis_index("y"), lax.axis_index("c")


def _other_chips(x, y):
    return [(1 - x, y), (x, 1 - y), (1 - x, 1 - y)]


def _remote(src, dst, send_sem, recv_sem, to):
    return pltpu.make_async_remote_copy(src_ref=src, dst_ref=dst, send_sem=send_sem, recv_sem=recv_sem,
                                        device_id=to, device_id_type=MESH)


def _half_rows(ref_rows, c):
    hr = ref_rows // 2
    return pl.ds(pl.multiple_of(c * hr, 16), hr)


def _allgather8(v, name):
    m_per, n = v.shape

    def body(v_ref, out_ref, send_sems, recv_sems, local_sem):
        x, y, c = _position()
        me, sibling = (x, y, c), (x, y, 1 - c)
        chips = _other_chips(x, y)

        def rows(px, py, pc):
            return out_ref.at[pl.ds((4 * px + 2 * py + pc) * m_per, m_per), :]

        def copy(k, block, to, src=None):
            return _remote(rows(*block) if src is None else src, rows(*block), send_sems.at[k], recv_sems.at[k], to)

        mine = pltpu.make_async_copy(v_ref, rows(*me), local_sem)
        mine.start()
        first = [copy(0, me, sibling, src=v_ref)]
        first += [copy(1 + j, me, (*chip, c), src=v_ref) for j, chip in enumerate(chips)]
        for cp in first:
            cp.start()
        passed = [copy(4 + j, (*chip, c), sibling) for j, chip in enumerate(chips)]
        for j, chip in enumerate(chips):
            copy(1 + j, (*chip, c), me).wait_recv()
            passed[j].start()
        copy(0, sibling, me).wait_recv()
        for j, chip in enumerate(chips):
            copy(4 + j, (*chip, 1 - c), me).wait_recv()
        for cp in first + passed:
            cp.wait_send()
        mine.wait()

    return _hosted(
        body, name=name,
        out_shape=jax.ShapeDtypeStruct((N_DEV * m_per, n), v.dtype),
        in_specs=[pl.BlockSpec(memory_space=pltpu.VMEM)],
        out_specs=pl.BlockSpec(memory_space=pltpu.VMEM),
        scratch_shapes=[pltpu.SemaphoreType.DMA((7,)), pltpu.SemaphoreType.DMA((7,)), pltpu.SemaphoreType.DMA],
        compiler_params=pltpu.CompilerParams(vmem_limit_bytes=VMEM_LIMIT),
    )(v)


def _cast_to_slot(ws, idx, name):
    r, cols = ws[0].shape
    assert all(w.shape == (r, cols) for w in ws)
    n = len(ws)
    tr = _tile(r, 256, 16)

    def body(idx_ref, *refs):
        for w_ref, o_ref in zip(refs[:n], refs[n:]):
            o_ref[...] = w_ref[...].astype(o_ref.dtype)

    return _hosted(
        body, name=name,
        grid_spec=pltpu.PrefetchScalarGridSpec(
            num_scalar_prefetch=1, grid=(r // tr,),
            in_specs=[pl.BlockSpec((tr, cols), lambda t, idx_ref: (t, 0))] * n,
            out_specs=[pl.BlockSpec((None, tr, cols), lambda t, idx_ref: (idx_ref[0], t, 0))] * n),
        out_shape=[jax.ShapeDtypeStruct((N_CHIPS, r, cols), BF16)] * n,
        compiler_params=_params("parallel"),
    )(idx, *ws)


def _dma_sems(*shape):
    return [pltpu.SemaphoreType.DMA(shape), pltpu.SemaphoreType.DMA(shape)]


def _same_shapes(arrays):
    return [jax.ShapeDtypeStruct(a.shape, a.dtype) for a in arrays]


def _part_rows(ref_rows, c, part):
    hr = ref_rows // 2
    i, n = part
    size = hr // n
    assert size * n == hr and size % 16 == 0, (ref_rows, part)
    return pl.ds(pl.multiple_of(c * hr + i * size, 16), size)


def _task_gather_ici(bufs, done, part=(0, 1)):
    n = len(bufs)

    def copies(outs, sems, landing):
        x, y, c = _position()
        my_chip = 2 * x + y
        res = []
        for i in range(n):
            rows = _part_rows(bufs[i].shape[1], c, part)
            for r, (px, py) in enumerate(_other_chips(x, y)):
                slot = (2 * px + py) if landing else my_chip
                res.append(_remote(outs[i].at[my_chip, rows, :], outs[i].at[slot, rows, :], sems[0].at[i, r], sems[1].at[i, r],
                                   (px, py, c)))
        return res

    def start(ins, outs, sems):
        for cp in copies(outs, sems, False):
            cp.start()

    def wait(ins, outs, sems):
        for cp in copies(outs, sems, True):
            cp.wait_recv()
            cp.wait_send()

    return _Task(bufs, _same_shapes(bufs), {i: i for i in range(n)}, _dma_sems(n, 3), start, wait, done)


def _task_gather_d2d(bufs, done):
    n = len(bufs)

    def copies(outs, sems, landing):
        x, y, c = _position()
        res = []
        for i in range(n):
            rows = _half_rows(bufs[i].shape[1], (1 - c) if landing else c)
            for r, (px, py) in enumerate(_other_chips(x, y)):
                part = outs[i].at[2 * px + py, rows, :]
                res.append(_remote(part, part, sems[0].at[i, r], sems[1].at[i, r], (x, y, 1 - c)))
        return res

    def start(ins, outs, sems):
        for cp in copies(outs, sems, False):
            cp.start()

    def wait(ins, outs, sems):
        for cp in copies(outs, sems, True):
            cp.wait_recv()
        for cp in copies(outs, sems, False):
            cp.wait_send()

    return _Task(bufs, _same_shapes(bufs), {i: i for i in range(n)}, _dma_sems(n, 3), start, wait, done)


def _task_sibling_halves(grads, done):
    n = len(grads)

    def copies(ins, outs, sems):
        x, y, c = _position()
        return [_remote(ins[i].at[:, _half_rows(grads[i].shape[1], 1 - c), :], outs[i], sems[0].at[i], sems[1].at[i],
                        (x, y, 1 - c)) for i in range(n)]

    def start(ins, outs, sems):
        for cp in copies(ins, outs, sems):
            cp.start()

    def wait(ins, outs, sems):
        for cp in copies(ins, outs, sems):
            cp.wait()

    shapes = [jax.ShapeDtypeStruct((g.shape[0], g.shape[1] // 2, g.shape[2]), g.dtype) for g in grads]
    return _Task(grads, shapes, {}, _dma_sems(n), start, wait, done)


def _task_chip_sums(parts, done, landed=None, part=(0, 1)):
    n = len(parts)
    i_part, n_parts = part
    sizes = [p.shape[1] // n_parts for p in parts]
    assert all(p.shape[1] == size * n_parts and size % 16 == 0 for p, size in zip(parts, sizes)), part
    rows = [pl.ds(i_part * size, size) for size in sizes]

    def copies(ins, outs, sems):
        x, y, c = _position()
        return [_remote(ins[i].at[2 * px + py, rows[i], :], outs[i].at[r, rows[i], :], sems[0].at[i, r], sems[1].at[i, r],
                        (px, py, c))
                for i in range(n) for r, (px, py) in enumerate(_other_chips(x, y))]

    def start(ins, outs, sems):
        for cp in copies(ins, outs, sems):
            cp.start()

    def wait(ins, outs, sems):
        for cp in copies(ins, outs, sems):
            cp.wait()

    shapes = [jax.ShapeDtypeStruct((3,) + p.shape[1:], p.dtype) for p in parts]
    if landed is None:
        return _Task(parts, shapes, {}, _dma_sems(n, 3), start, wait, done)
    return _Task(list(parts) + list(landed), shapes, {n + i: i for i in range(n)}, _dma_sems(n, 3), start, wait, done)


def _task_sibling_parts(owns, landeds, done):
    n = len(owns)

    def copies(ins, outs, sems):
        x, y, c = _position()
        sibling = (x, y, 1 - c)
        res = []
        for i in range(n):
            res.append(_remote(ins[i].at[2 * x + y], outs[i].at[0], sems[0].at[i, 0], sems[1].at[i, 0], sibling))
            res.append(_remote(ins[n + i], outs[i].at[pl.ds(1, 3)], sems[0].at[i, 1], sems[1].at[i, 1], sibling))
        return res

    def start(ins, outs, sems):
        for cp in copies(ins, outs, sems):
            cp.start()

    def wait(ins, outs, sems):
        for cp in copies(ins, outs, sems):
            cp.wait()

    return _Task(list(owns) + list(landeds), _same_shapes(owns), {}, _dma_sems(n, 2), start, wait, done)


def _add_halves(grad, sib, c, name):
    nsh, r, cols = grad.shape
    hr = r // 2
    tr = _tile(hr, 512, 16)
    nt = hr // tr

    def body(c_ref, g_ref, s_ref, o_ref):
        o_ref[...] = (g_ref[...].astype(F32) + s_ref[...].astype(F32)).astype(o_ref.dtype)

    return _hosted(
        body, name=name,
        grid_spec=pltpu.PrefetchScalarGridSpec(
            num_scalar_prefetch=1, grid=(nsh, nt),
            in_specs=[pl.BlockSpec((None, tr, cols), lambda j, t, c_ref: (j, c_ref[1] * nt + t, 0)),
                      pl.BlockSpec((None, tr, cols), lambda j, t, c_ref: (j, t, 0))],
            out_specs=pl.BlockSpec((None, tr, cols), lambda j, t, c_ref: (j, t, 0))),
        out_shape=jax.ShapeDtypeStruct((nsh, hr, cols), BF16),
        compiler_params=_params("parallel", "parallel"),
    )(c, grad, sib)


def _adamw_reduced(w, own, landed, sib, m, v, idx, name):
    r, cols = w.shape
    hr = r // 2
    tr = _tile(hr, 256, 16)
    nt = hr // tr

    def body(idx_ref, w_ref, p_ref, l_ref, s_ref, m_ref, v_ref, go_ref, d_ref, mo_ref, vo_ref):
        mine = p_ref[...].astype(F32)
        for k in range(3):
            mine = mine + l_ref[k].astype(F32)
        theirs = s_ref[0].astype(F32)
        for k in range(1, 4):
            theirs = theirs + s_ref[k].astype(F32)
        grad = jnp.where(pl.program_id(0) // nt == idx_ref[1], mine, theirs)
        go_ref[...] = grad
        d_ref[...], mo_ref[...], vo_ref[...] = _adamw_update(w_ref[...], grad, m_ref[...], v_ref[...])

    def in_half(t, half):
        return jnp.clip(t - half * nt, 0, nt - 1)

    full = pl.BlockSpec((tr, cols), lambda t, idx_ref: (t, 0))
    return _hosted(
        body, name=name,
        grid_spec=pltpu.PrefetchScalarGridSpec(
            num_scalar_prefetch=1, grid=(2 * nt,),
            in_specs=[full,
                      pl.BlockSpec((None, tr, cols), lambda t, idx_ref: (idx_ref[0], in_half(t, idx_ref[1]), 0)),
                      pl.BlockSpec((3, tr, cols), lambda t, idx_ref: (0, in_half(t, idx_ref[1]), 0)),
                      pl.BlockSpec((4, tr, cols), lambda t, idx_ref: (0, in_half(t, 1 - idx_ref[1]), 0)),
                      full, full],
            out_specs=[full] * 4),
        out_shape=[jax.ShapeDtypeStruct((r, cols), F32)] * 4,
        compiler_params=_params("arbitrary"),
    )(idx, w, own, landed, sib, m, v)


def _sum_devices(blocks, name):
    m8, n = blocks.shape
    m = m8 // N_DEV

    def body(b_ref, o_ref):
        acc = b_ref[pl.ds(0, m), :]
        for k in range(1, N_DEV):
            acc = acc + b_ref[pl.ds(k * m, m), :]
        o_ref[...] = acc

    return _hosted(
        body, name=name, out_shape=jax.ShapeDtypeStruct((m, n), F32),
        in_specs=[pl.BlockSpec(memory_space=pltpu.VMEM)], out_specs=pl.BlockSpec(memory_space=pltpu.VMEM),
        compiler_params=pltpu.CompilerParams(vmem_limit_bytes=VMEM_LIMIT),
    )(blocks)


def _adamw_update(w, grad, m, v):
    new_m = ADAM_B1 * m + (1.0 - ADAM_B1) * grad
    new_v = ADAM_B2 * v + (1.0 - ADAM_B2) * (grad * grad)
    m_hat = new_m * (1.0 / (1.0 - ADAM_B1 ** ADAM_STEP))
    v_hat = new_v * (1.0 / (1.0 - ADAM_B2 ** ADAM_STEP))
    return -ADAM_LR * (m_hat / (jnp.sqrt(v_hat) + ADAM_EPS) + ADAM_WD * w), new_m, new_v


def _adamw(w, g, m, v, name):
    r, cols = w.shape
    tr = _tile(r, 256)

    def body(w_ref, g_ref, m_ref, v_ref, go_ref, d_ref, mo_ref, vo_ref):
        grad = g_ref[...]
        go_ref[...] = grad
        d_ref[...], mo_ref[...], vo_ref[...] = _adamw_update(w_ref[...], grad, m_ref[...], v_ref[...])

    spec = pl.BlockSpec((tr, cols), lambda t: (t, 0))
    return _hosted(
        body, name=name, grid=(r // tr,), in_specs=[spec] * 4, out_specs=[spec] * 4,
        out_shape=[jax.ShapeDtypeStruct((r, cols), F32)] * 4,
        compiler_params=_params("parallel"),
    )(w, g, m, v)


def _adamw_small(grad_blocks, params, name):
    nb, npar = len(grad_blocks), len(params)

    def body(*refs):
        blocks, ins, outs = refs[:nb], refs[nb:nb + 3 * npar], refs[nb + 3 * npar:]
        for p, (w, _, _, blk, row0) in enumerate(params):
            if w.ndim == 1:
                tiled = (w.shape[0] // LANES, LANES)
                grad = blocks[blk][pl.ds(row0, tiled[0]), pl.ds(0, LANES)]
                wmv = [ins[3 * p + k][...].reshape(tiled) for k in range(3)]
            else:
                grad = blocks[blk][pl.ds(row0, w.shape[0]), :]
                wmv = [ins[3 * p + k][...] for k in range(3)]
            for k, res in enumerate((grad,) + _adamw_update(wmv[0], grad, wmv[1], wmv[2])):
                outs[4 * p + k][...] = res.reshape(w.shape)

    args = list(grad_blocks) + [a for w, m, v, _, _ in params for a in (w, m, v)]
    vmem = pl.BlockSpec(memory_space=pltpu.VMEM)
    out = _hosted(
        body, name=name, in_specs=[vmem] * len(args), out_specs=[vmem] * (4 * npar),
        out_shape=[jax.ShapeDtypeStruct(w.shape, F32) for w, _, _, _, _ in params for _ in range(4)],
    )(*args)
    return [tuple(out[4 * p:4 * p + 4]) for p in range(npar)]


WEIGHT_NAMES = (
    "ln1_0", "a0_w_in", "a0_conv", "a0_w_out", "ln2_0", "ffn0_w_gu", "ffn0_w_down",
    "ln1_1", "b1_w_grp", "b1_scale", "ln2_1", "ffn1_w_gu", "ffn1_w_down",
    "ln1_2", "c2_w_pw1", "c2_b_pw1", "c2_dw", "c2_b_dw", "c2_ln_g", "c2_ln_b", "c2_w_pw2", "c2_b_pw2",
    "ln2_2", "ffn2_w_gu", "ffn2_w_down",
    "ln1_3", "a3_w_in", "a3_conv", "a3_w_out", "ln2_3", "ffn3_w_gu", "ffn3_w_down", "ln_f")
BIG = ("a0_w_in", "a0_w_out", "ffn0_w_gu", "ffn0_w_down", "b1_w_grp", "ffn1_w_gu", "ffn1_w_down", "c2_w_pw1", "c2_w_pw2",
       "ffn2_w_gu", "ffn2_w_down", "a3_w_in", "a3_w_out", "ffn3_w_gu", "ffn3_w_down")
GROUPED = "b1_w_grp"
SMALL_SHARDED = ("a0_conv", "a3_conv", "c2_dw")
REPLICATED = tuple(n for n in WEIGHT_NAMES if n not in BIG and n not in SMALL_SHARDED)


def _pad_rows(a, mult=8):
    pad = -a.shape[0] % mult
    return a if pad == 0 else jnp.concatenate([a, jnp.zeros((pad, a.shape[1]), a.dtype)], axis=0)


def _pack_rows(parts, width):
    rows = [p.reshape(-1, width) for p in parts]
    return _pad_rows(jnp.concatenate(rows, axis=0)), [r.shape[0] for r in rows]


def _unpack_rows(packed, counts, shapes):
    out, at = [], 0
    for n, shp in zip(counts, shapes):
        out.append(packed[at:at + n].reshape(shp))
        at += n
    return out


COLUMN_SHARDED = ("w_in", "w_gu", "w_pw1")


class _Weights(dict):
    def __init__(self, bufs):
        super().__init__()
        self.bufs = bufs

    def __missing__(self, name):
        buf = self.bufs[name]
        if name == GROUPED:
            cg = buf.shape[-1]
            rq = cg // N_CHIPS
            return jnp.transpose(buf.reshape(N_CHIPS, -1, rq, cg), (1, 0, 2, 3)).reshape(-1, cg, cg)
        return buf if name.endswith(COLUMN_SHARDED) else buf.reshape(-1, buf.shape[-1])


class _Exchange:
    def __init__(self, w, mom, vel, idx):
        def shards(table):
            return {n: table[n].reshape(-1, table[n].shape[-1]) for n in BIG}

        self.w, self.mom, self.vel, self.idx = shards(w), shards(mom), shards(vel), idx
        self.bufs = {}
        self.weights = _Weights(self.bufs)
        self.grads = {}
        self.sib, self.part, self.landed, self.sib_parts, self.updates = {}, {}, {}, {}, {}

    def cast(self, names):
        by_shape = {}
        for n in names:
            by_shape.setdefault(self.w[n].shape, []).append(n)
        for group in by_shape.values():
            self.bufs.update(zip(group, _cast_to_slot([self.w[n] for n in group], self.idx, f"cast_{group[0]}")))

    @staticmethod
    def _store(table, names):
        def done(arrays):
            table.update(zip(names, arrays))
        return done

    def _grad(self, n):
        g = self.grads[n]
        if n == GROUPED:
            ng, cg, _ = g.shape
            g = jnp.transpose(g.reshape(ng, N_CHIPS, cg // N_CHIPS, cg), (1, 0, 2, 3)).astype(BF16)
        return g.reshape(N_CHIPS, -1, g.shape[-1])

    def gather_ici(self, *names, part=(0, 1)):
        return lambda: _task_gather_ici([self.bufs[n] for n in names], self._store(self.bufs, names), part)

    def gather_d2d(self, *names):
        return lambda: _task_gather_d2d([self.bufs[n] for n in names], self._store(self.bufs, names))

    def sibling_halves(self, *names):
        return lambda: _task_sibling_halves([self._grad(n) for n in names], self._store(self.sib, names))

    def add_halves(self, *names):
        def run():
            for n in names:
                self.part[n] = _add_halves(self._grad(n), self.sib.pop(n), self.idx, f"reduce_add_{n}")
        return run

    def chip_sums(self, *names, part=(0, 1)):
        def make():
            landed = [self.landed[n] for n in names] if part[0] > 0 else None
            return _task_chip_sums([self.part[n] for n in names], self._store(self.landed, names), landed, part)
        return make

    def sibling_parts(self, *names):
        return lambda: _task_sibling_parts([self.part[n] for n in names], [self.landed[n] for n in names],
                                           self._store(self.sib_parts, names))

    def adamw(self, *names):
        def run():
            for n in names:
                self.updates[n] = _adamw_reduced(self.w[n], self.part.pop(n), self.landed.pop(n), self.sib_parts.pop(n),
                                                 self.mom[n], self.vel[n], self.idx, f"adamw_{n}")
        return run


def _plan(ex):
    s = _Schedule()

    def ffn(i):
        return f"ffn{i}_w_gu", f"ffn{i}_w_down"

    c2, a3 = ("c2_w_pw1", "c2_w_pw2"), ("a3_w_in", "a3_w_out")
    first, second = (0, 2), (1, 2)
    s.host("cast_ffn0_w_gu", ex.gather_ici("a0_w_in", part=first))
    s.host("cast_ffn0_w_down", ex.gather_ici("a0_w_in", part=second))
    s.host("cast_c2_w_pw1", ex.gather_d2d("a0_w_in"))
    gu, down = ffn(0)
    s.host("gather_small", ex.gather_ici("a0_w_out"))
    s.host("a0_in", ex.gather_ici(gu, part=first), ex.gather_d2d("a0_w_out"))
    s.host("a0_conv", ex.gather_ici(gu, part=second))
    s.host("a0_out", ex.gather_ici(down), ex.gather_d2d(gu))
    s.host("ffn0_up", ex.gather_d2d(down))
    gu, down = ffn(1)
    s.host("ffn0_up", ex.gather_ici(gu, GROUPED))
    s.host("ffn0_down", ex.gather_ici(down), ex.gather_d2d(gu, GROUPED))
    s.host("ffn1_up", ex.gather_d2d(down), ex.gather_ici(*c2))
    gu, down = ffn(2)
    s.host("ffn1_up", ex.gather_ici(gu, part=first))
    s.host("ffn1_down", ex.gather_d2d(*c2), ex.gather_ici(down))
    s.host("c2_mid", ex.gather_ici(gu, part=second))
    s.host("c2_pw2", ex.gather_d2d(gu, down))
    s.host("ffn2_up", ex.gather_ici(*a3))
    gu, down = ffn(3)
    s.host("ffn2_up", ex.gather_ici(gu, part=first))
    s.host("ffn2_down", ex.gather_d2d(*a3), ex.gather_ici(down))
    s.host("a3_in", ex.gather_ici(gu, part=second))
    s.host("a3_out", ex.gather_d2d(gu, down))

    def reduce_on(names, first_host, ici_hosts, last_host):
        s.host(first_host, ex.sibling_halves(*names))
        s.post(first_host, ex.add_halves(*names))
        for host, hosted, part in ici_hosts:
            s.host(host, ex.chip_sums(*hosted, part=part))
        s.host(last_host, ex.sibling_parts(*names))
        s.post(last_host, ex.adamw(*names))

    whole = (0, 1)
    gu, down = ffn(3)
    reduce_on((gu, down), "a3_out_bwd",
              [("a3_conv_bwd", (down,), whole), ("a3_dw_in", (gu,), first), ("a3_in_bwd", (gu,), second)], "ffn2_down_bwd")
    reduce_on(a3, "ffn2_down_bwd", [("ffn2_dw_gu", a3, whole)], "c2_pw2_bwd")
    gu, down = ffn(0)
    s.host("ffn0_dw_gu", ex.sibling_halves(down))
    s.post("ffn0_dw_gu", ex.add_halves(down))
    s.host("ffn0_up_bwd", ex.chip_sums(down))
    s.host("a0_out_bwd", ex.sibling_halves(gu, GROUPED))
    s.post("a0_out_bwd", ex.add_halves(gu, GROUPED))
    s.host("a0_conv_bwd", ex.chip_sums(gu, part=first), ex.chip_sums(GROUPED))
    s.host("a0_dw_in", ex.chip_sums(gu, part=second))
    s.host("a0_dw_out", ex.sibling_halves("a0_w_in"))
    s.post("a0_dw_out", ex.add_halves("a0_w_in"))
    s.host("a0_in_bwd", ex.chip_sums("a0_w_in"))
    s.host(f"adamw_{gu}", ex.chip_sums("a0_w_out"))
    s.host(f"adamw_{down}", ex.sibling_parts("a0_w_out"))
    reduce_on(ffn(2), "c2_pw2_bwd", [("c2_mid_bwd", ffn(2), whole)], "ffn1_down_bwd")
    reduce_on(c2, "ffn1_down_bwd", [("ffn1_dw_down", c2, whole)], "b1_bwd_mm")
    gu, down = ffn(1)
    reduce_on((gu, down), "b1_bwd_mm", [("ffn0_down_bwd", (down,), whole), ("ffn0_dw_gu", (gu,), whole)], "ffn0_up_bwd")
    return s


def kernel(x, *rest):
    nw = len(WEIGHT_NAMES)
    w = dict(zip(WEIGHT_NAMES, rest[:nw]))
    target = rest[nw]
    mom = dict(zip(WEIGHT_NAMES, rest[nw + 1:2 * nw + 1]))
    vel = dict(zip(WEIGHT_NAMES, rest[2 * nw + 1:3 * nw + 1]))
    cx, cy, cc = _position()
    my_chip = 2 * cx + cy
    ex = _Exchange(w, mom, vel, jnp.stack([my_chip, cc]).astype(jnp.int32))
    _ACTIVE_SCHEDULE[0] = _plan(ex)
    try:
        return _scheduled_step(x, target, w, mom, vel, ex, my_chip)
    finally:
        _ACTIVE_SCHEDULE[0] = None


def _scheduled_step(x, target, w, mom, vel, ex, my_chip):
    d = x.shape[-1]
    cq = d // N_CHIPS
    ex.cast(BIG)

    small_blk, small_counts = _pack_rows([w[n] for n in SMALL_SHARDED], cq)
    small_all = _allgather8(small_blk, "gather_small").reshape(N_CHIPS, 2, small_blk.shape[0], cq)[:, 0]
    small_parts = _unpack_rows(jnp.transpose(small_all, (1, 0, 2)), small_counts,
                               [(w[n].reshape(-1, cq).shape[0], N_CHIPS, cq) for n in SMALL_SHARDED])
    wts = ex.weights
    for n in REPLICATED:
        wts[n] = w[n].reshape(1, -1)
    for n, part in zip(SMALL_SHARDED, small_parts):
        wts[n] = part.reshape(part.shape[0], d)

    loss, dx, g = _device_step(x[0], target[0], wts, ex.grads)

    summed, last = ("ffn0_w_gu", "ffn0_w_down", GROUPED, "a0_w_in"), "a0_w_out"
    _comm_only([ex.sibling_parts(*summed)(), ex.sibling_halves(last)()], "reduce_tail_d2d")
    ex.add_halves(last)()
    ex.adamw(*summed)()
    ex.adamw(last)()
    sched = _ACTIVE_SCHEDULE[0]
    assert not sched.hosts and not sched.posts, (sched.hosts, sched.posts)

    rep_rows = [jnp.pad(g[n].reshape(-1, LANES), ((0, 0), (0, cq - LANES))) for n in REPLICATED]
    by_chip = [jnp.transpose(g[n].reshape(g[n].shape[0], N_CHIPS, cq), (1, 0, 2)) for n in SMALL_SHARDED]
    shard_rows = jnp.concatenate(by_chip, axis=1)
    n_rep, n_shard = sum(r.shape[0] for r in rep_rows), shard_rows.shape[1]
    loss_row = jnp.broadcast_to(loss, (1, cq))
    sm_blk = _pad_rows(jnp.concatenate(rep_rows + [loss_row, shard_rows.reshape(N_CHIPS * n_shard, cq)], axis=0))
    sm_sum = _sum_devices(_allgather8(sm_blk, "gather_small_grads"), "sum_small_grads")
    mine = lax.dynamic_slice_in_dim(sm_sum, n_rep + 1 + my_chip * n_shard, n_shard, axis=0)

    out = ex.updates
    params, at = [], {0: 0, 1: 0}
    for block, names in ((0, REPLICATED), (1, SMALL_SHARDED)):
        for n in names:
            params.append((w[n], mom[n], vel[n], block, at[block]))
            at[block] += w[n].size // LANES if w[n].ndim == 1 else w[n].shape[0]
    out.update(zip(REPLICATED + SMALL_SHARDED, _adamw_small([sm_sum, mine], params, "adamw_small")))

    total = sm_sum[n_rep, 0]
    grads, deltas, new_m, new_v = ([out[n][k].reshape(w[n].shape) for n in WEIGHT_NAMES] for k in range(4))
    return (total, dx.reshape(x.shape), *grads, *deltas, *new_m, *new_v)
```

```python
import functools

import jax
import jax.numpy as jnp
from jax import lax
from jax.experimental import pallas as pl
from jax.experimental.pallas import tpu as pltpu

F32 = jnp.float32
BF16 = jnp.bfloat16

RMS_EPS = 1e-6
LN_EPS = 1e-5
POOL_WINDOWS = (2, 4, 8, 16)
SHORT_CONV_W = 3
CONF_CONV_W = 31
N_CHIPS = 4
N_DEV = 8

ADAM_LR = 0.001
ADAM_B1 = 0.9
ADAM_B2 = 0.999
ADAM_EPS = 1e-08
ADAM_WD = 0.01
ADAM_STEP = 10

V7X_VMEM_BYTES = 64 * 1024 * 1024
VMEM_LIMIT = V7X_VMEM_BYTES - 8 * 1024 * 1024
LANES = 128
POOL_HALO = 16
SCONV_HALO = 16
CONF_HALO = 32


def _params(*sem):
    return pltpu.CompilerParams(dimension_semantics=sem, vmem_limit_bytes=VMEM_LIMIT)


def _tile(n, pref, mult=8):
    t = min(n, pref)
    while t > mult and (n % t or t % mult):
        t -= mult
    assert n % t == 0 and t % mult == 0, (n, pref, mult)
    return t


def _sigmoid(x):
    return jax.nn.sigmoid(x)


def _dot(a, b):
    return jnp.dot(a, b, preferred_element_type=F32)


def _dot_nt(a, b):
    return lax.dot_general(a, b, (((1,), (1,)), ((), ())), preferred_element_type=F32)


def _dot_tn(a, b):
    return lax.dot_general(a, b, (((0,), (0,)), ((), ())), preferred_element_type=F32)


def _colsum(x):
    return jnp.sum(x, axis=0, keepdims=True)


def _rms_stats(x):
    return lax.rsqrt(jnp.mean(x * x, axis=-1, keepdims=True) + RMS_EPS)


def _rms_bwd(du, x, gain):
    r = _rms_stats(x)
    xhat = x * r
    gdy = du * gain
    dx = r * (gdy - xhat * jnp.mean(gdy * xhat, axis=-1, keepdims=True))
    return dx, _colsum(du * xhat)


class _Task:
    def __init__(self, ins, out_shapes, aliases, sems, start, wait, done):
        self.ins, self.out_shapes, self.aliases, self.sems = list(ins), list(out_shapes), dict(aliases), list(sems)
        self.start, self.wait, self.done = start, wait, done


class _Schedule:
    def __init__(self):
        self.hosts, self.posts = {}, {}

    def host(self, kernel_name, *make_tasks):
        self.hosts.setdefault(kernel_name, []).extend(make_tasks)

    def post(self, kernel_name, *thunks):
        self.posts.setdefault(kernel_name, []).extend(thunks)

    def tasks_for(self, kernel_name):
        return [make() for make in self.hosts.pop(kernel_name, ())]

    def finished(self, kernel_name):
        for thunk in self.posts.pop(kernel_name, ()):
            thunk()


_ACTIVE_SCHEDULE = [None]


def _hosted(body, name, **kw):
    def run(*args):
        sched = _ACTIVE_SCHEDULE[0]
        tasks = sched.tasks_for(name) if sched is not None else []
        out = _call_with_tasks(body, name, tasks, kw, args) if tasks else pl.pallas_call(body, name=name, **kw)(*args)
        if sched is not None:
            sched.finished(name)
        return out

    return run


def _call_with_tasks(body, name, tasks, kw, args):
    spec = kw.get("grid_spec")
    n_pre = spec.num_scalar_prefetch if spec is not None else 0
    src = dict(grid=spec.grid, in_specs=spec.in_specs, out_specs=spec.out_specs) if spec is not None else kw
    pre, args = args[:n_pre], args[n_pre:]
    grid = tuple(src.get("grid", ()))
    single = not isinstance(kw["out_shape"], (list, tuple))
    out_shape = [kw["out_shape"]] if single else list(kw["out_shape"])
    out_specs = [src["out_specs"]] if single else list(src["out_specs"])
    scratch = list(kw.get("scratch_shapes", ()))
    n_in, n_out, n_scr = len(args), len(out_shape), len(scratch)
    t_in = [a for t in tasks for a in t.ins]
    t_out = [o for t in tasks for o in t.out_shapes]
    t_sem = [s for t in tasks for s in t.sems]
    aliases, at_in, at_out = {}, n_pre + n_in, n_out
    for t in tasks:
        for i, o in t.aliases.items():
            aliases[at_in + i] = at_out + o
        at_in += len(t.ins)
        at_out += len(t.out_shapes)

    def wrapped(*refs):
        pre_refs, refs = refs[:n_pre], refs[n_pre:]
        a = n_in
        b = a + len(t_in)
        c = b + n_out
        d = c + len(t_out)
        e = d + n_scr
        ins, tins, outs, touts, scr, tsems = refs[:a], refs[a:b], refs[b:c], refs[c:d], refs[d:e], refs[e:]
        views, i0, o0, s0 = [], 0, 0, 0
        for t in tasks:
            views.append((tins[i0:i0 + len(t.ins)], touts[o0:o0 + len(t.out_shapes)], tsems[s0:s0 + len(t.sems)]))
            i0, o0, s0 = i0 + len(t.ins), o0 + len(t.out_shapes), s0 + len(t.sems)

        def start_all():
            for t, v in zip(tasks, views):
                t.start(*v)

        def wait_all():
            for t, v in zip(tasks, views):
                t.wait(*v)

        if grid:
            first = functools.reduce(jnp.logical_and, [pl.program_id(i) == 0 for i in range(len(grid))])
            last = functools.reduce(jnp.logical_and, [pl.program_id(i) == grid[i] - 1 for i in range(len(grid))])
            pl.when(first)(start_all)
            body(*pre_refs, *ins, *outs, *scr)
            pl.when(last)(wait_all)
        else:
            start_all()
            body(*pre_refs, *ins, *outs, *scr)
            wait_all()

    in_specs = list(src["in_specs"]) + [ANY] * len(t_in)
    out_specs = out_specs + [ANY] * len(t_out)
    if spec is not None:
        layout = dict(grid_spec=pltpu.PrefetchScalarGridSpec(
            num_scalar_prefetch=n_pre, grid=grid, in_specs=in_specs, out_specs=out_specs, scratch_shapes=scratch + t_sem))
    else:
        layout = dict(grid=grid, in_specs=in_specs, out_specs=out_specs, scratch_shapes=scratch + t_sem)
    res = pl.pallas_call(
        wrapped, name=name, out_shape=out_shape + t_out, input_output_aliases=aliases,
        compiler_params=pltpu.CompilerParams(dimension_semantics=("arbitrary",) * len(grid), vmem_limit_bytes=VMEM_LIMIT),
        **layout,
    )(*pre, *args, *t_in)
    res = list(res)
    own, rest = res[:n_out], res[n_out:]
    for t in tasks:
        t.done(rest[:len(t.out_shapes)])
        rest = rest[len(t.out_shapes):]
    return own[0] if single else own


def _comm_only(tasks, name):
    _call_with_tasks(lambda: None, name, tasks, dict(grid=(), in_specs=[], out_specs=[], out_shape=[]), ())


def _mm_col(h, gain, w, bias, name):
    s, k = h.shape
    nsh, _, ns = w.shape
    tm = _tile(s, 512)
    has_bias = bias is not None

    def body(h_ref, gain_ref, w_ref, *rest):
        u_ref, o_ref = rest[-2:]
        x = h_ref[...]
        x = (x * _rms_stats(x) * gain_ref[...]).astype(BF16)
        u_ref[...] = x
        for j in range(nsh):
            cols = pl.ds(j * ns, ns)
            acc = _dot(x, w_ref[j])
            if has_bias:
                acc = acc + rest[0][:, cols]
            o_ref[:, cols] = acc.astype(o_ref.dtype)

    tokens = pl.BlockSpec((tm, k), lambda m: (m, 0))
    in_specs = [tokens, pl.BlockSpec((1, k), lambda m: (0, 0)), pl.BlockSpec((nsh, k, ns), lambda m: (0, 0, 0))]
    args = [h, gain, w]
    if has_bias:
        in_specs.append(pl.BlockSpec((1, nsh * ns), lambda m: (0, 0)))
        args.append(bias)
    return _hosted(
        body, name=name, grid=(s // tm,), in_specs=in_specs,
        out_specs=[tokens, pl.BlockSpec((tm, nsh * ns), lambda m: (m, 0))],
        out_shape=[jax.ShapeDtypeStruct((s, k), BF16), jax.ShapeDtypeStruct((s, nsh * ns), BF16)],
        compiler_params=_params("parallel"),
    )(*args)


def _mm_row(a, w, res, bias, name):
    s = a.shape[0]
    k, n = w.shape
    tm = _tile(s, 512)
    has_bias = bias is not None

    def body(a_ref, w_ref, res_ref, *rest):
        o_ref = rest[-1]
        y = res_ref[...] + _dot(a_ref[...], w_ref[...])
        if has_bias:
            y = y + rest[0][...]
        o_ref[...] = y

    in_specs = [pl.BlockSpec((tm, k), lambda m: (m, 0)), pl.BlockSpec((k, n), lambda m: (0, 0)),
                pl.BlockSpec((tm, n), lambda m: (m, 0))]
    args = [a, w, res]
    if has_bias:
        in_specs.append(pl.BlockSpec((1, n), lambda m: (0, 0)))
        args.append(bias)
    return _hosted(
        body, name=name, grid=(s // tm,), in_specs=in_specs,
        out_specs=pl.BlockSpec((tm, n), lambda m: (m, 0)),
        out_shape=jax.ShapeDtypeStruct((s, n), F32),
        compiler_params=_params("parallel"),
    )(*args)


def _mm_nt_row(dy, w, name):
    s, n = dy.shape
    k = w.shape[0]
    tm = _tile(s, 512)

    def body(dy_ref, w_ref, o_ref):
        o_ref[...] = _dot_nt(dy_ref[...].astype(BF16), w_ref[...])

    return _hosted(
        body, name=name, grid=(s // tm,),
        in_specs=[pl.BlockSpec((tm, n), lambda m: (m, 0)), pl.BlockSpec((k, n), lambda m: (0, 0))],
        out_specs=pl.BlockSpec((tm, k), lambda m: (m, 0)),
        out_shape=jax.ShapeDtypeStruct((s, k), F32),
        compiler_params=_params("parallel"),
    )(dy, w)


def _ffn_up(h, gain, w, name):
    s, d = h.shape
    _, _, ns = w.shape
    tm = _tile(s, 512)

    def body(h_ref, gain_ref, wg_ref, wu_ref, u_ref, act_ref, s1_ref, q1_ref):
        x = h_ref[...]
        x = (x * _rms_stats(x) * gain_ref[...]).astype(BF16)

        @pl.when(pl.program_id(0) == 0)
        def _():
            u_ref[...] = x

        g = _dot(x, wg_ref[...])
        up = _dot(x, wu_ref[...])
        sg = _sigmoid(g)
        s1 = g * sg
        act_ref[...] = (s1 * up).astype(act_ref.dtype)
        s1_ref[...] = s1.astype(s1_ref.dtype)
        q1_ref[...] = (up * sg * (1.0 + g * (1.0 - sg))).astype(q1_ref.dtype)

    out = pl.BlockSpec((tm, ns), lambda j, m: (m, j))
    tokens = pl.BlockSpec((tm, d), lambda j, m: (m, 0))
    nm = s // tm
    u_once = pl.BlockSpec((tm, d), lambda j, m: (jnp.where(j == 0, m, nm - 1), 0))
    return _hosted(
        body, name=name, grid=(2, nm),
        in_specs=[tokens, pl.BlockSpec((1, d), lambda j, m: (0, 0)), pl.BlockSpec((None, d, ns), lambda j, m: (j, 0, 0)),
                  pl.BlockSpec((None, d, ns), lambda j, m: (j + 2, 0, 0))],
        out_specs=[u_once, out, out, out],
        out_shape=[jax.ShapeDtypeStruct((s, d), BF16)] + [jax.ShapeDtypeStruct((s, 2 * ns), BF16)] * 3,
        compiler_params=_params("arbitrary", "arbitrary"),
    )(h, gain, w, w)


def _ffn_down_bwd(dh, w, s1, q1, name):
    s, d = dh.shape
    f = w.shape[0]
    tm = _tile(s, 256)

    def body(dh_ref, w_ref, s1_ref, q1_ref, o_ref):
        da = _dot_nt(dh_ref[...].astype(BF16), w_ref[...])
        o_ref[:, :f] = (da * q1_ref[...].astype(F32)).astype(o_ref.dtype)
        o_ref[:, f:] = (da * s1_ref[...].astype(F32)).astype(o_ref.dtype)

    return _hosted(
        body, name=name, grid=(s // tm,),
        in_specs=[pl.BlockSpec((tm, d), lambda m: (m, 0)), pl.BlockSpec((f, d), lambda m: (0, 0)),
                  pl.BlockSpec((tm, f), lambda m: (m, 0)), pl.BlockSpec((tm, f), lambda m: (m, 0))],
        out_specs=pl.BlockSpec((tm, 2 * f), lambda m: (m, 0)),
        out_shape=jax.ShapeDtypeStruct((s, 2 * f), BF16),
        compiler_params=_params("parallel"),
    )(dh, w, s1, q1)


def _mm_nt_col_rms_bwd(dy, w, h, gain, dh, name):
    s = dy.shape[0]
    nsh, k, ns = w.shape
    tm = _tile(s, 256)

    def body(dy_ref, w_ref, h_ref, g_ref, dh_ref, o_ref, dg_ref):
        du = _dot_nt(dy_ref[:, :ns], w_ref[0])
        for j in range(1, nsh):
            du = du + _dot_nt(dy_ref[:, j * ns:(j + 1) * ns], w_ref[j])
        dx, dg = _rms_bwd(du, h_ref[...], g_ref[...])
        o_ref[...] = dh_ref[...] + dx
        _accumulate(dg_ref, dg, pl.program_id(0) == 0)

    return _hosted(
        body, name=name, grid=(s // tm,),
        in_specs=[pl.BlockSpec((tm, nsh * ns), lambda m: (m, 0)), pl.BlockSpec((nsh, k, ns), lambda m: (0, 0, 0)),
                  pl.BlockSpec((tm, k), lambda m: (m, 0)), pl.BlockSpec((1, k), lambda m: (0, 0)),
                  pl.BlockSpec((tm, k), lambda m: (m, 0))],
        out_specs=[pl.BlockSpec((tm, k), lambda m: (m, 0)), pl.BlockSpec((1, k), lambda m: (0, 0))],
        out_shape=[jax.ShapeDtypeStruct((s, k), F32), jax.ShapeDtypeStruct((1, k), F32)],
        compiler_params=_params("arbitrary"),
    )(dy, w, h, gain, dh)


def _mm_tn(a, dy, nsh, name):
    s, k = a.shape
    ns = dy.shape[1] // nsh
    tm = _tile(s, 1024)
    tk = _tile(k, 1408, LANES)
    nk, nm = k // tk, s // tm

    def body(a_ref, dy_ref, o_ref, acc_ref):
        m = pl.program_id(2)
        part = _dot_tn(a_ref[...], dy_ref[...].astype(BF16))

        @pl.when(m == 0)
        def _():
            acc_ref[...] = part

        @pl.when(m > 0)
        def _():
            acc_ref[...] += part

        @pl.when(m == nm - 1)
        def _():
            o_ref[...] = acc_ref[...].astype(o_ref.dtype)

    return _hosted(
        body, name=name, grid=(nsh, nk, nm),
        in_specs=[pl.BlockSpec((tm, tk), lambda j, kk, m: (m, kk)), pl.BlockSpec((tm, ns), lambda j, kk, m: (m, j))],
        out_specs=pl.BlockSpec((None, tk, ns), lambda j, kk, m: (j, kk, 0)),
        out_shape=jax.ShapeDtypeStruct((nsh, k, ns), BF16),
        scratch_shapes=[pltpu.VMEM((tk, ns), F32)],
        compiler_params=_params("parallel", "parallel", "arbitrary"),
    )(a, dy)


def _main_spec(tm, w):
    return pl.BlockSpec((tm, w), lambda m: (m, 0))


def _before_spec(tm, hb, w):
    return pl.BlockSpec((hb, w), lambda m: (jnp.maximum(m * (tm // hb) - 1, 0), 0))


def _after_spec(tm, hb, w, s):
    return pl.BlockSpec((hb, w), lambda m: (jnp.minimum((m + 1) * (tm // hb), s // hb - 1), 0))


def _row_spec(w, rows=1):
    return pl.BlockSpec((rows, w), lambda m: (0, 0))


CHUNK_LANES = 4 * LANES
CHUNK_ROWS = 32


def _build_shifts(ext8_ref, residues=range(1, 8)):
    n = ext8_ref.shape[1] - 8
    for r in residues:
        ext8_ref[r, pl.ds(0, n), :] = ext8_ref[0, pl.ds(r, n), :]


def _fold_rows(x):
    return functools.reduce(lambda p, q: p + q, [x[i:i + 8] for i in range(0, x.shape[0], 8)])


def _shifted(ext8_ref, shift, r0, rows, cols):
    return ext8_ref[shift % 8, pl.ds(pl.multiple_of(shift - shift % 8 + r0, 8), rows), cols]


def _lane_chunk(i):
    return pl.ds(pl.multiple_of(i * CHUNK_LANES, CHUNK_LANES), CHUNK_LANES)


def _sum_terms(terms, ways=4):
    accs = []
    for i, t in enumerate(terms):
        if i < ways:
            accs.append(t)
        else:
            accs[i % ways] = accs[i % ways] + t
    while len(accs) > 1:
        accs = [accs[i] + accs[i + 1] if i + 1 < len(accs) else accs[i] for i in range(0, len(accs), 2)]
    return accs[0]


def _accumulate(ref, val, first):
    @pl.when(first)
    def _():
        ref[...] = val

    @pl.when(jnp.logical_not(first))
    def _():
        ref[...] += val


SCONV_Z_SHIFTS = tuple(SCONV_HALO - (SHORT_CONV_W - 1) + k for k in range(SHORT_CONV_W))


def _sconv_z_taps(zext_ref, r0, cols):
    return [_shifted(zext_ref, shift, r0, CHUNK_ROWS, cols) for shift in SCONV_Z_SHIFTS]


def _weighted(cw_ref, cols, terms):
    return _sum_terms((cw_ref[k:k + 1, cols] * t for k, t in enumerate(terms)), ways=len(terms))


def _sconv_fill_z(zext_ref, main_ref, before_ref, d, m):
    hb = SCONV_HALO
    zb = before_ref[:, d:2 * d].astype(F32) * before_ref[:, 2 * d:].astype(F32)
    zext_ref[pl.ds(0, hb), :] = jnp.where(m > 0, zb, 0.0)
    zext_ref[pl.ds(hb, main_ref.shape[0]), :] = main_ref[:, d:2 * d].astype(F32) * main_ref[:, 2 * d:].astype(F32)


def _sconv_fwd(bcv, cw, name):
    s, d3 = bcv.shape
    d = d3 // 3
    tm = _tile(s, 256, CHUNK_ROWS)
    row_chunks = tm // CHUNK_ROWS

    def body(main_ref, before_ref, cw_ref, p_ref, zext_ref):
        m = pl.program_id(0)
        _sconv_fill_z(zext_ref.at[0], main_ref, before_ref, d, m)
        _build_shifts(zext_ref, [shift % 8 for shift in SCONV_Z_SHIFTS if shift % 8])

        def chunk(i, carry):
            cols = _lane_chunk(i // row_chunks)
            r0 = pl.multiple_of((i % row_chunks) * CHUNK_ROWS, CHUNK_ROWS)
            rows = pl.ds(r0, CHUNK_ROWS)
            zc = _weighted(cw_ref, cols, _sconv_z_taps(zext_ref, r0, cols))
            p_ref[rows, cols] = (main_ref[rows, cols].astype(F32) * zc).astype(p_ref.dtype)
            return carry

        lax.fori_loop(0, row_chunks * (d // CHUNK_LANES), chunk, 0)

    return _hosted(
        body, name=name, grid=(s // tm,),
        in_specs=[_main_spec(tm, d3), _before_spec(tm, SCONV_HALO, d3), _row_spec(d, SHORT_CONV_W)],
        out_specs=_main_spec(tm, d),
        out_shape=jax.ShapeDtypeStruct((s, d), BF16),
        scratch_shapes=[pltpu.VMEM((8, tm + SCONV_HALO, d), F32)],
        compiler_params=_params("parallel"),
    )(bcv, bcv, cw)


def _sconv_bwd(dp, bcv, cw, name):
    s, d3 = bcv.shape
    d = d3 // 3
    tm = _tile(s, 256, CHUNK_ROWS)
    nm = s // tm
    ha = 8
    kw = SHORT_CONV_W

    def body(dp_ref, dpa_ref, main_ref, before_ref, after_ref, cw_ref, o_ref, dcw_ref, zext_ref, dext_ref):
        m = pl.program_id(0)
        _sconv_fill_z(zext_ref.at[0], main_ref, before_ref, d, m)
        _build_shifts(zext_ref, [shift % 8 for shift in SCONV_Z_SHIFTS if shift % 8])
        dext_ref[0, pl.ds(0, tm), :] = dp_ref[...] * main_ref[:, :d].astype(F32)
        dza = dpa_ref[...] * after_ref[:, :d].astype(F32)[0:ha]
        dext_ref[0, pl.ds(tm, ha), :] = jnp.where(m < nm - 1, dza, 0.0)
        _build_shifts(dext_ref, range(1, kw))

        @pl.when(m == 0)
        def _():
            dcw_ref[...] = jnp.zeros_like(dcw_ref)

        zero = jnp.zeros((8, CHUNK_LANES), F32)

        def lane_chunk(ci, carry):
            cols = _lane_chunk(ci)
            c_cols, v_cols = (pl.ds(pl.multiple_of(part * d + ci * CHUNK_LANES, CHUNK_LANES), CHUNK_LANES) for part in (1, 2))

            def row_chunk(ri, sums):
                r0 = pl.multiple_of(ri * CHUNK_ROWS, CHUNK_ROWS)
                rows = pl.ds(r0, CHUNK_ROWS)
                z = _sconv_z_taps(zext_ref, r0, cols)
                o_ref[rows, cols] = (dp_ref[rows, cols] * _weighted(cw_ref, cols, z)).astype(o_ref.dtype)
                dzc = [_shifted(dext_ref, kw - 1 - k, r0, CHUNK_ROWS, cols) for k in range(kw)]
                dz = _weighted(cw_ref, cols, dzc)
                o_ref[rows, c_cols] = (dz * main_ref[rows, v_cols].astype(F32)).astype(o_ref.dtype)
                o_ref[rows, v_cols] = (dz * main_ref[rows, c_cols].astype(F32)).astype(o_ref.dtype)
                return tuple(acc + _fold_rows(dzc[kw - 1] * z[k]) for k, acc in enumerate(sums))

            sums = lax.fori_loop(0, tm // CHUNK_ROWS, row_chunk, (zero,) * kw)
            for k in range(kw):
                dcw_ref[k:k + 1, cols] += _colsum(sums[k])
            return carry

        lax.fori_loop(0, d // CHUNK_LANES, lane_chunk, 0)

    return _hosted(
        body, name=name, grid=(nm,),
        in_specs=[_main_spec(tm, d), _after_spec(tm, ha, d, s), _main_spec(tm, d3), _before_spec(tm, SCONV_HALO, d3),
                  _after_spec(tm, SCONV_HALO, d3, s), _row_spec(d, SHORT_CONV_W)],
        out_specs=[_main_spec(tm, d3), _row_spec(d, 8)],
        out_shape=[jax.ShapeDtypeStruct((s, d3), BF16), jax.ShapeDtypeStruct((8, d), F32)],
        scratch_shapes=[pltpu.VMEM((8, tm + SCONV_HALO, d), F32), pltpu.VMEM((8, tm + ha, d), F32)],
        compiler_params=_params("arbitrary"),
    )(dp, dp, bcv, bcv, bcv, cw)


def _pool_counts(t0, tm, w):
    t = t0 + lax.broadcasted_iota(jnp.int32, (tm, 1), 0)
    return jnp.minimum(t + 1, w).astype(F32)


def _pool_fwd(h, gain, wg, scale, name):
    s, d = h.shape
    ng, cg, _ = wg.shape
    tm = _tile(s, 512, POOL_HALO)

    def body(h_ref, hb_ref, g_ref, wg_ref, sc_ref, o_ref, mx_ref, uext_ref):
        m = pl.program_id(0)
        x = h_ref[...]
        gain_row = g_ref[...]
        xb = hb_ref[...]
        uext_ref[pl.ds(0, POOL_HALO), :] = jnp.where(m > 0, xb * _rms_stats(xb) * gain_row, 0.0)
        uext_ref[pl.ds(POOL_HALO, tm), :] = x * _rms_stats(x) * gain_row
        for gi, win in enumerate(POOL_WINDOWS):
            cols = pl.ds(gi * cg, cg)
            u_g = uext_ref[pl.ds(POOL_HALO, tm), cols]
            acc = u_g
            for i in range(1, win):
                acc = acc + uext_ref[pl.ds(POOL_HALO - i, tm), cols]
            mixed = (acc / _pool_counts(m * tm, tm, win) - u_g).astype(BF16)
            mx_ref[:, cols] = mixed
            o_ref[:, cols] = x[:, gi * cg:(gi + 1) * cg] + _dot(mixed, wg_ref[gi]) * sc_ref[:, cols]

    return _hosted(
        body, name=name, grid=(s // tm,),
        in_specs=[_main_spec(tm, d), _before_spec(tm, POOL_HALO, d), _row_spec(d),
                  pl.BlockSpec((ng, cg, cg), lambda m: (0, 0, 0)), _row_spec(d)],
        out_specs=[_main_spec(tm, d), _main_spec(tm, d)],
        out_shape=[jax.ShapeDtypeStruct((s, d), F32), jax.ShapeDtypeStruct((s, d), BF16)],
        scratch_shapes=[pltpu.VMEM((tm + POOL_HALO, d), F32)],
        compiler_params=_params("parallel"),
    )(h, h, gain, wg, scale)


def _pool_bwd_mm(dh, mixed, wg, scale, name):
    s, d = dh.shape
    ng, cg, _ = wg.shape
    tm = _tile(s, 512)

    def body(dh_ref, mx_ref, wg_ref, sc_ref, dmx_ref, dwg_ref, dsc_ref):
        first = pl.program_id(0) == 0
        for gi in range(ng):
            cols = pl.ds(gi * cg, cg)
            dh_g = dh_ref[:, cols]
            mixed = mx_ref[:, cols]
            w_g = wg_ref[gi]
            dy = (dh_g * sc_ref[:, cols]).astype(BF16)
            dmx_ref[:, cols] = _dot_nt(dy, w_g)
            _accumulate(dsc_ref.at[:, cols], _colsum(dh_g * _dot(mixed, w_g)), first)
            _accumulate(dwg_ref.at[gi], _dot_tn(mixed, dy), first)

    return _hosted(
        body, name=name, grid=(s // tm,),
        in_specs=[_main_spec(tm, d), _main_spec(tm, d), pl.BlockSpec((ng, cg, cg), lambda m: (0, 0, 0)), _row_spec(d)],
        out_specs=[_main_spec(tm, d), pl.BlockSpec((ng, cg, cg), lambda m: (0, 0, 0)), _row_spec(d)],
        out_shape=[jax.ShapeDtypeStruct((s, d), F32), jax.ShapeDtypeStruct((ng, cg, cg), F32),
                   jax.ShapeDtypeStruct((1, d), F32)],
        compiler_params=_params("arbitrary"),
    )(dh, mixed, wg, scale)


def _pool_bwd_rms(dmixed, h, gain, dh, name):
    s, d = h.shape
    cg = d // len(POOL_WINDOWS)
    tm = _tile(s, 512, POOL_HALO)
    nm = s // tm

    def body(dmx_ref, dmxa_ref, h_ref, g_ref, dh_ref, o_ref, dg_ref, eext_ref, du_ref):
        m = pl.program_id(0)
        for gi, win in enumerate(POOL_WINDOWS):
            cols = pl.ds(gi * cg, cg)
            dmx = dmx_ref[:, cols]
            eext_ref[pl.ds(0, tm), cols] = dmx / _pool_counts(m * tm, tm, win)
            ea = dmxa_ref[:, cols] / _pool_counts((m + 1) * tm, POOL_HALO, win)
            eext_ref[pl.ds(tm, POOL_HALO), cols] = jnp.where(m < nm - 1, ea, 0.0)
            acc = -dmx
            for i in range(win):
                acc = acc + eext_ref[pl.ds(i, tm), cols]
            du_ref[:, cols] = acc
        dx, dg = _rms_bwd(du_ref[...], h_ref[...], g_ref[...])
        o_ref[...] = dh_ref[...] + dx
        _accumulate(dg_ref, dg, m == 0)

    return _hosted(
        body, name=name, grid=(nm,),
        in_specs=[_main_spec(tm, d), _after_spec(tm, POOL_HALO, d, s), _main_spec(tm, d), _row_spec(d), _main_spec(tm, d)],
        out_specs=[_main_spec(tm, d), _row_spec(d)],
        out_shape=[jax.ShapeDtypeStruct((s, d), F32), jax.ShapeDtypeStruct((1, d), F32)],
        scratch_shapes=[pltpu.VMEM((tm + POOL_HALO, d), F32), pltpu.VMEM((tm, d), F32)],
        compiler_params=_params("arbitrary"),
    )(dmixed, dmixed, h, gain, dh)


def _conf_fill_h(hext_ref, main_ref, before_ref, d, m):
    hb = before_ref[:, :d].astype(F32) * _sigmoid(before_ref[:, d:].astype(F32))
    hext_ref[pl.ds(0, CONF_HALO), :] = jnp.where(m > 0, hb, 0.0)
    hext_ref[pl.ds(CONF_HALO, main_ref.shape[0]), :] = main_ref[:, :d].astype(F32) * _sigmoid(main_ref[:, d:].astype(F32))


def _layernorm_parts(hc, g, b):
    mu = jnp.mean(hc, axis=-1, keepdims=True)
    xc = hc - mu
    rs = lax.rsqrt(jnp.mean(xc * xc, axis=-1, keepdims=True) + LN_EPS)
    xhat = xc * rs
    return xhat, rs, xhat * g + b


def _conf_mid_fwd(ag, dw, b_dw, ln_g, ln_b, name):
    s, d2 = ag.shape
    d = d2 // 2
    tm = _tile(s, 256, CONF_HALO)
    base = CONF_HALO - (CONF_CONV_W - 1)

    def body(main_ref, before_ref, dw_ref, bdw_ref, g_ref, b_ref, s_ref, hc_ref, hext_ref):
        m = pl.program_id(0)
        _conf_fill_h(hext_ref.at[0], main_ref, before_ref, d, m)
        _build_shifts(hext_ref)
        row_chunks = tm // CHUNK_ROWS

        def conv_chunk(i, carry):
            cols = _lane_chunk(i // row_chunks)
            r0 = pl.multiple_of((i % row_chunks) * CHUNK_ROWS, CHUNK_ROWS)
            taps = (dw_ref[kk:kk + 1, cols] * _shifted(hext_ref, base + kk, r0, CHUNK_ROWS, cols) for kk in range(CONF_CONV_W))
            hc_ref[pl.ds(r0, CHUNK_ROWS), cols] = bdw_ref[:, cols] + _sum_terms(taps, ways=1)
            return carry

        lax.fori_loop(0, row_chunks * (d // CHUNK_LANES), conv_chunk, 0)
        _, _, l = _layernorm_parts(hc_ref[...], g_ref[...], b_ref[...])
        s_ref[...] = (l * _sigmoid(l)).astype(s_ref.dtype)

    return _hosted(
        body, name=name, grid=(s // tm,),
        in_specs=[_main_spec(tm, d2), _before_spec(tm, CONF_HALO, d2), _row_spec(d, CONF_CONV_W), _row_spec(d),
                  _row_spec(d), _row_spec(d)],
        out_specs=[_main_spec(tm, d), _main_spec(tm, d)],
        out_shape=[jax.ShapeDtypeStruct((s, d), BF16), jax.ShapeDtypeStruct((s, d), F32)],
        scratch_shapes=[pltpu.VMEM((8, tm + CONF_HALO, d), F32)],
        compiler_params=_params("parallel"),
    )(ag, ag, dw, b_dw, ln_g, ln_b)


def _conf_out_bwd(dh, w, hc, ln_g, ln_b, name):
    s, d = dh.shape
    tm = _tile(s, 256)

    def body(dh_ref, w_ref, hc_ref, g_ref, b_ref, o_ref, dg_ref, db_ref, dbo_ref):
        first = pl.program_id(0) == 0
        dh_t = dh_ref[...]
        ds = _dot_nt(dh_t.astype(BF16), w_ref[...])
        xhat, rs, l = _layernorm_parts(hc_ref[...], g_ref[...], b_ref[...])
        sg = _sigmoid(l)
        dl = ds * sg * (1.0 + l * (1.0 - sg))
        dxh = dl * g_ref[...]
        o_ref[...] = rs * (dxh - jnp.mean(dxh, axis=-1, keepdims=True)
                           - xhat * jnp.mean(dxh * xhat, axis=-1, keepdims=True))
        _accumulate(dg_ref, _colsum(dl * xhat), first)
        _accumulate(db_ref, _colsum(dl), first)
        _accumulate(dbo_ref, _colsum(dh_t), first)

    return _hosted(
        body, name=name, grid=(s // tm,),
        in_specs=[_main_spec(tm, d), pl.BlockSpec((d, d), lambda m: (0, 0)), _main_spec(tm, d), _row_spec(d), _row_spec(d)],
        out_specs=[_main_spec(tm, d), _row_spec(d), _row_spec(d), _row_spec(d)],
        out_shape=[jax.ShapeDtypeStruct((s, d), F32)] + [jax.ShapeDtypeStruct((1, d), F32)] * 3,
        compiler_params=_params("arbitrary"),
    )(dh, w, hc, ln_g, ln_b)


def _conf_mid_bwd(dhc, ag, dw, name):
    s, d2 = ag.shape
    d = d2 // 2
    tm = _tile(s, 256, CONF_HALO)
    nm = s // tm
    kw = CONF_CONV_W
    base = CONF_HALO - (kw - 1)

    def body(dhc_ref, dhca_ref, main_ref, before_ref, dw_ref, o_ref, ddw_ref, dbdw_ref, dbpw_ref, hext_ref, dext_ref):
        m = pl.program_id(0)
        first = m == 0
        _conf_fill_h(hext_ref.at[0], main_ref, before_ref, d, m)
        _build_shifts(hext_ref)
        dext_ref[0, pl.ds(0, tm), :] = dhc_ref[...]
        dext_ref[0, pl.ds(tm, CONF_HALO), :] = jnp.where(m < nm - 1, dhca_ref[...], 0.0)
        _build_shifts(dext_ref)

        @pl.when(first)
        def _():
            ddw_ref[...] = jnp.zeros_like(ddw_ref)
            dbdw_ref[...] = jnp.zeros_like(dbdw_ref)
            dbpw_ref[...] = jnp.zeros_like(dbpw_ref)

        zero = jnp.zeros((8, CHUNK_LANES), F32)
        tap_group = 8

        def fold(x):
            return functools.reduce(lambda p, q: p + q, [x[i:i + 8] for i in range(0, CHUNK_ROWS, 8)])

        def lane_chunk(ci, carry):
            cols = _lane_chunk(ci)
            gate_cols = pl.ds(pl.multiple_of(d + ci * CHUNK_LANES, CHUNK_LANES), CHUNK_LANES)

            def through_conv(ri, sums):
                r0 = pl.multiple_of(ri * CHUNK_ROWS, CHUNK_ROWS)
                rows = pl.ds(r0, CHUNK_ROWS)
                dhh = _sum_terms((dw_ref[kk:kk + 1, cols] * _shifted(dext_ref, kw - 1 - kk, r0, CHUNK_ROWS, cols)
                                  for kk in range(kw)), ways=1)
                a = main_ref[rows, cols].astype(F32)
                sg = _sigmoid(main_ref[rows, gate_cols].astype(F32))
                da = dhh * sg
                dgate = dhh * a * sg * (1.0 - sg)
                o_ref[rows, cols] = da.astype(o_ref.dtype)
                o_ref[rows, gate_cols] = dgate.astype(o_ref.dtype)
                return sums[0] + fold(da), sums[1] + fold(dgate), sums[2] + fold(dext_ref[0, rows, cols])

            sum_da, sum_dgate, sum_dhc = lax.fori_loop(0, tm // CHUNK_ROWS, through_conv, (zero, zero, zero))
            dbdw_ref[:, cols] += _colsum(sum_dhc)
            dbpw_ref[:, cols] += _colsum(sum_da)
            dbpw_ref[:, gate_cols] += _colsum(sum_dgate)

            for k0 in range(0, kw, tap_group):
                group = range(k0, min(k0 + tap_group, kw))

                def tap_gradients(ri, accs, group=group):
                    for sub in range(0, CHUNK_ROWS, 8):
                        r0 = pl.multiple_of(ri * CHUNK_ROWS + sub, 8)
                        dhc_c = dext_ref[0, pl.ds(r0, 8), cols]
                        accs = tuple(acc + dhc_c * _shifted(hext_ref, base + kk, r0, 8, cols) for kk, acc in zip(group, accs))
                    return accs

                accs = lax.fori_loop(0, tm // CHUNK_ROWS, tap_gradients, (zero,) * len(group))
                for kk, acc in zip(group, accs):
                    ddw_ref[kk:kk + 1, cols] += _colsum(acc)
            return carry

        lax.fori_loop(0, d // CHUNK_LANES, lane_chunk, 0)

    return _hosted(
        body, name=name, grid=(nm,),
        in_specs=[_main_spec(tm, d), _after_spec(tm, CONF_HALO, d, s), _main_spec(tm, d2), _before_spec(tm, CONF_HALO, d2),
                  _row_spec(d, kw)],
        out_specs=[_main_spec(tm, d2), _row_spec(d, 32), _row_spec(d), _row_spec(d2)],
        out_shape=[jax.ShapeDtypeStruct((s, d2), BF16), jax.ShapeDtypeStruct((32, d), F32),
                   jax.ShapeDtypeStruct((1, d), F32), jax.ShapeDtypeStruct((1, d2), F32)],
        scratch_shapes=[pltpu.VMEM((8, tm + CONF_HALO, d), F32), pltpu.VMEM((8, tm + CONF_HALO, d), F32)],
        compiler_params=_params("arbitrary"),
    )(dhc, dhc, ag, ag, dw)


def _loss_head(h, gain, target, name):
    s, d = h.shape
    tm = _tile(s, 512)

    def body(h_ref, g_ref, t_ref, loss_ref, dh_ref, dg_ref):
        first = pl.program_id(0) == 0
        x = h_ref[...]
        err = x * _rms_stats(x) * g_ref[...] - t_ref[...]
        part = 0.5 * jnp.sum(jnp.mean(err * err, axis=-1, keepdims=True), axis=0, keepdims=True)
        dx, dg = _rms_bwd(err * (1.0 / d), x, g_ref[...])
        dh_ref[...] = dx
        _accumulate(loss_ref, part, first)
        _accumulate(dg_ref, dg, first)

    return _hosted(
        body, name=name, grid=(s // tm,),
        in_specs=[_main_spec(tm, d), _row_spec(d), _main_spec(tm, d)],
        out_specs=[pl.BlockSpec((1, 1), lambda m: (0, 0)), _main_spec(tm, d), _row_spec(d)],
        out_shape=[jax.ShapeDtypeStruct((1, 1), F32), jax.ShapeDtypeStruct((s, d), F32), jax.ShapeDtypeStruct((1, d), F32)],
        compiler_params=_params("arbitrary"),
    )(h, gain, target)


def _ffn_fwd(h, wts, i):
    u, act, s1, q1 = _ffn_up(h, wts[f"ln2_{i}"], wts[f"ffn{i}_w_gu"], f"ffn{i}_up")
    h_new = _mm_row(act, wts[f"ffn{i}_w_down"], h, None, f"ffn{i}_down")
    return h_new, (h, u, act, s1, q1)


def _ffn_bwd(dh, saved, wts, i, g):
    h, u, act, s1, q1 = saved
    dgu = _ffn_down_bwd(dh, wts[f"ffn{i}_w_down"], s1, q1, f"ffn{i}_down_bwd")
    g[f"ffn{i}_w_down"] = _mm_tn(act, dh, 1, f"ffn{i}_dw_down")
    g[f"ffn{i}_w_gu"] = _mm_tn(u, dgu, N_CHIPS, f"ffn{i}_dw_gu")
    dh_new, g[f"ln2_{i}"] = _mm_nt_col_rms_bwd(dgu, wts[f"ffn{i}_w_gu"], h, wts[f"ln2_{i}"], dh, f"ffn{i}_up_bwd")
    return dh_new


def _device_step(x, target, wts, g=None):
    g = {} if g is None else g
    saved = {}
    h = x

    def short_conv_fwd(h, i):
        u, bcv = _mm_col(h, wts[f"ln1_{i}"], wts[f"a{i}_w_in"], None, f"a{i}_in")
        p = _sconv_fwd(bcv, wts[f"a{i}_conv"], f"a{i}_conv")
        return _mm_row(p, wts[f"a{i}_w_out"], h, None, f"a{i}_out"), (h, u, bcv, p)

    def short_conv_bwd(dh, sv, i):
        h, u, bcv, p = sv
        dp = _mm_nt_row(dh, wts[f"a{i}_w_out"], f"a{i}_out_bwd")
        dbcv, dcw = _sconv_bwd(dp, bcv, wts[f"a{i}_conv"], f"a{i}_conv_bwd")
        g[f"a{i}_conv"] = dcw[:SHORT_CONV_W]
        g[f"a{i}_w_in"] = _mm_tn(u, dbcv, N_CHIPS, f"a{i}_dw_in")
        g[f"a{i}_w_out"] = _mm_tn(p, dh, 1, f"a{i}_dw_out")
        dh, g[f"ln1_{i}"] = _mm_nt_col_rms_bwd(dbcv, wts[f"a{i}_w_in"], h, wts[f"ln1_{i}"], dh, f"a{i}_in_bwd")
        return dh

    h, saved["a0"] = short_conv_fwd(h, 0)
    h, saved["f0"] = _ffn_fwd(h, wts, 0)

    h_in = h
    h, mixed = _pool_fwd(h, wts["ln1_1"], wts["b1_w_grp"], wts["b1_scale"], "b1_fwd")
    saved["b1"] = (h_in, mixed)
    h, saved["f1"] = _ffn_fwd(h, wts, 1)

    h_in = h
    u, ag = _mm_col(h, wts["ln1_2"], wts["c2_w_pw1"], wts["c2_b_pw1"], "c2_pw1")
    sw, hc = _conf_mid_fwd(ag, wts["c2_dw"], wts["c2_b_dw"], wts["c2_ln_g"], wts["c2_ln_b"], "c2_mid")
    h = _mm_row(sw, wts["c2_w_pw2"], h, wts["c2_b_pw2"], "c2_pw2")
    saved["c2"] = (h_in, u, ag, sw, hc)
    h, saved["f2"] = _ffn_fwd(h, wts, 2)

    h, saved["a3"] = short_conv_fwd(h, 3)
    h, saved["f3"] = _ffn_fwd(h, wts, 3)

    loss, dh, g["ln_f"] = _loss_head(h, wts["ln_f"], target, "loss_head")

    def ffn_bwd(dh, i):
        return _ffn_bwd(dh, saved[f"f{i}"], wts, i, g)

    dh = ffn_bwd(dh, 3)
    dh = short_conv_bwd(dh, saved["a3"], 3)

    dh = ffn_bwd(dh, 2)
    h_in, u, ag, sw, hc = saved["c2"]
    dhc, g["c2_ln_g"], g["c2_ln_b"], g["c2_b_pw2"] = _conf_out_bwd(
        dh, wts["c2_w_pw2"], hc, wts["c2_ln_g"], wts["c2_ln_b"], "c2_pw2_bwd")
    g["c2_w_pw2"] = _mm_tn(sw, dh, 1, "c2_dw_pw2")
    dag, ddw, g["c2_b_dw"], g["c2_b_pw1"] = _conf_mid_bwd(dhc, ag, wts["c2_dw"], "c2_mid_bwd")
    g["c2_dw"] = ddw[:CONF_CONV_W]
    g["c2_w_pw1"] = _mm_tn(u, dag, N_CHIPS, "c2_dw_pw1")
    dh, g["ln1_2"] = _mm_nt_col_rms_bwd(dag, wts["c2_w_pw1"], h_in, wts["ln1_2"], dh, "c2_pw1_bwd")

    dh = ffn_bwd(dh, 1)
    h_in, mixed = saved["b1"]
    dmixed, g["b1_w_grp"], g["b1_scale"] = _pool_bwd_mm(dh, mixed, wts["b1_w_grp"], wts["b1_scale"], "b1_bwd_mm")
    dh, g["ln1_1"] = _pool_bwd_rms(dmixed, h_in, wts["ln1_1"], dh, "b1_bwd_rms")

    dh = ffn_bwd(dh, 0)
    dh = short_conv_bwd(dh, saved["a0"], 0)
    return loss, dh, g


MESH = pl.DeviceIdType.MESH
ANY = pl.BlockSpec(memory_space=pl.ANY)


def _position():
    return lax.axis_index("x"), lax.axis_index("y"), lax.axis_index("c")


def _other_chips(x, y):
    return [(1 - x, y), (x, 1 - y), (1 - x, 1 - y)]


def _remote(src, dst, send_sem, recv_sem, to):
    return pltpu.make_async_remote_copy(src_ref=src, dst_ref=dst, send_sem=send_sem, recv_sem=recv_sem,
                                        device_id=to, device_id_type=MESH)


def _half_rows(ref_rows, c):
    hr = ref_rows // 2
    return pl.ds(pl.multiple_of(c * hr, 16), hr)


def _allgather8(v, name):
    m_per, n = v.shape

    def body(v_ref, out_ref, send_sems, recv_sems, local_sem):
        x, y, c = _position()
        me, sibling = (x, y, c), (x, y, 1 - c)
        chips = _other_chips(x, y)

        def rows(px, py, pc):
            return out_ref.at[pl.ds((4 * px + 2 * py + pc) * m_per, m_per), :]

        def copy(k, block, to, src=None):
            return _remote(rows(*block) if src is None else src, rows(*block), send_sems.at[k], recv_sems.at[k], to)

        mine = pltpu.make_async_copy(v_ref, rows(*me), local_sem)
        mine.start()
        first = [copy(0, me, sibling, src=v_ref)]
        first += [copy(1 + j, me, (*chip, c), src=v_ref) for j, chip in enumerate(chips)]
        for cp in first:
            cp.start()
        passed = [copy(4 + j, (*chip, c), sibling) for j, chip in enumerate(chips)]
        for j, chip in enumerate(chips):
            copy(1 + j, (*chip, c), me).wait_recv()
            passed[j].start()
        copy(0, sibling, me).wait_recv()
        for j, chip in enumerate(chips):
            copy(4 + j, (*chip, 1 - c), me).wait_recv()
        for cp in first + passed:
            cp.wait_send()
        mine.wait()

    return _hosted(
        body, name=name,
        out_shape=jax.ShapeDtypeStruct((N_DEV * m_per, n), v.dtype),
        in_specs=[pl.BlockSpec(memory_space=pltpu.VMEM)],
        out_specs=pl.BlockSpec(memory_space=pltpu.VMEM),
        scratch_shapes=[pltpu.SemaphoreType.DMA((7,)), pltpu.SemaphoreType.DMA((7,)), pltpu.SemaphoreType.DMA],
        compiler_params=pltpu.CompilerParams(vmem_limit_bytes=VMEM_LIMIT),
    )(v)


def _cast_to_slot(ws, idx, name):
    r, cols = ws[0].shape
    assert all(w.shape == (r, cols) for w in ws)
    n = len(ws)
    tr = _tile(r, 256, 16)

    def body(idx_ref, *refs):
        for w_ref, o_ref in zip(refs[:n], refs[n:]):
            o_ref[...] = w_ref[...].astype(o_ref.dtype)

    return _hosted(
        body, name=name,
        grid_spec=pltpu.PrefetchScalarGridSpec(
            num_scalar_prefetch=1, grid=(r // tr,),
            in_specs=[pl.BlockSpec((tr, cols), lambda t, idx_ref: (t, 0))] * n,
            out_specs=[pl.BlockSpec((None, tr, cols), lambda t, idx_ref: (idx_ref[0], t, 0))] * n),
        out_shape=[jax.ShapeDtypeStruct((N_CHIPS, r, cols), BF16)] * n,
        compiler_params=_params("parallel"),
    )(idx, *ws)


def _dma_sems(*shape):
    return [pltpu.SemaphoreType.DMA(shape), pltpu.SemaphoreType.DMA(shape)]


def _same_shapes(arrays):
    return [jax.ShapeDtypeStruct(a.shape, a.dtype) for a in arrays]


def _part_rows(ref_rows, c, part):
    hr = ref_rows // 2
    i, n = part
    size = hr // n
    assert size * n == hr and size % 16 == 0, (ref_rows, part)
    return pl.ds(pl.multiple_of(c * hr + i * size, 16), size)


def _task_gather_ici(bufs, done, part=(0, 1)):
    n = len(bufs)

    def copies(outs, sems, landing):
        x, y, c = _position()
        my_chip = 2 * x + y
        res = []
        for i in range(n):
            rows = _part_rows(bufs[i].shape[1], c, part)
            for r, (px, py) in enumerate(_other_chips(x, y)):
                slot = (2 * px + py) if landing else my_chip
                res.append(_remote(outs[i].at[my_chip, rows, :], outs[i].at[slot, rows, :], sems[0].at[i, r], sems[1].at[i, r],
                                   (px, py, c)))
        return res

    def start(ins, outs, sems):
        for cp in copies(outs, sems, False):
            cp.start()

    def wait(ins, outs, sems):
        for cp in copies(outs, sems, True):
            cp.wait_recv()
            cp.wait_send()

    return _Task(bufs, _same_shapes(bufs), {i: i for i in range(n)}, _dma_sems(n, 3), start, wait, done)


def _task_gather_d2d(bufs, done):
    n = len(bufs)

    def copies(outs, sems, landing):
        x, y, c = _position()
        res = []
        for i in range(n):
            rows = _half_rows(bufs[i].shape[1], (1 - c) if landing else c)
            for r, (px, py) in enumerate(_other_chips(x, y)):
                part = outs[i].at[2 * px + py, rows, :]
                res.append(_remote(part, part, sems[0].at[i, r], sems[1].at[i, r], (x, y, 1 - c)))
        return res

    def start(ins, outs, sems):
        for cp in copies(outs, sems, False):
            cp.start()

    def wait(ins, outs, sems):
        for cp in copies(outs, sems, True):
            cp.wait_recv()
        for cp in copies(outs, sems, False):
            cp.wait_send()

    return _Task(bufs, _same_shapes(bufs), {i: i for i in range(n)}, _dma_sems(n, 3), start, wait, done)


def _task_sibling_halves(grads, done):
    n = len(grads)

    def copies(ins, outs, sems):
        x, y, c = _position()
        return [_remote(ins[i].at[:, _half_rows(grads[i].shape[1], 1 - c), :], outs[i], sems[0].at[i], sems[1].at[i],
                        (x, y, 1 - c)) for i in range(n)]

    def start(ins, outs, sems):
        for cp in copies(ins, outs, sems):
            cp.start()

    def wait(ins, outs, sems):
        for cp in copies(ins, outs, sems):
            cp.wait()

    shapes = [jax.ShapeDtypeStruct((g.shape[0], g.shape[1] // 2, g.shape[2]), g.dtype) for g in grads]
    return _Task(grads, shapes, {}, _dma_sems(n), start, wait, done)


def _task_chip_sums(parts, done, landed=None, part=(0, 1)):
    n = len(parts)
    i_part, n_parts = part
    sizes = [p.shape[1] // n_parts for p in parts]
    assert all(p.shape[1] == size * n_parts and size % 16 == 0 for p, size in zip(parts, sizes)), part
    rows = [pl.ds(i_part * size, size) for size in sizes]

    def copies(ins, outs, sems):
        x, y, c = _position()
        return [_remote(ins[i].at[2 * px + py, rows[i], :], outs[i].at[r, rows[i], :], sems[0].at[i, r], sems[1].at[i, r],
                        (px, py, c))
                for i in range(n) for r, (px, py) in enumerate(_other_chips(x, y))]

    def start(ins, outs, sems):
        for cp in copies(ins, outs, sems):
            cp.start()

    def wait(ins, outs, sems):
        for cp in copies(ins, outs, sems):
            cp.wait()

    shapes = [jax.ShapeDtypeStruct((3,) + p.shape[1:], p.dtype) for p in parts]
    if landed is None:
        return _Task(parts, shapes, {}, _dma_sems(n, 3), start, wait, done)
    return _Task(list(parts) + list(landed), shapes, {n + i: i for i in range(n)}, _dma_sems(n, 3), start, wait, done)


def _task_sibling_parts(owns, landeds, done):
    n = len(owns)

    def copies(ins, outs, sems):
        x, y, c = _position()
        sibling = (x, y, 1 - c)
        res = []
        for i in range(n):
            res.append(_remote(ins[i].at[2 * x + y], outs[i].at[0], sems[0].at[i, 0], sems[1].at[i, 0], sibling))
            res.append(_remote(ins[n + i], outs[i].at[pl.ds(1, 3)], sems[0].at[i, 1], sems[1].at[i, 1], sibling))
        return res

    def start(ins, outs, sems):
        for cp in copies(ins, outs, sems):
            cp.start()

    def wait(ins, outs, sems):
        for cp in copies(ins, outs, sems):
            cp.wait()

    return _Task(list(owns) + list(landeds), _same_shapes(owns), {}, _dma_sems(n, 2), start, wait, done)


def _add_halves(grads, sibs, idx, name):
    n = len(grads)
    nsh = grads[0].shape[0]

    def body(idx_ref, *refs):
        for g_ref, s_ref, o_ref in zip(refs[:n], refs[n:2 * n], refs[2 * n:]):
            o_ref[...] = (g_ref[...].astype(F32) + s_ref[...].astype(F32)).astype(o_ref.dtype)

    half_of = [pl.BlockSpec((None, s.shape[1], s.shape[2]), lambda j, idx_ref: (j, idx_ref[1], 0)) for s in sibs]
    whole = [pl.BlockSpec((None, s.shape[1], s.shape[2]), lambda j, idx_ref: (j, 0, 0)) for s in sibs]
    return _hosted(
        body, name=name,
        grid_spec=pltpu.PrefetchScalarGridSpec(num_scalar_prefetch=1, grid=(nsh,), in_specs=half_of + whole, out_specs=whole),
        out_shape=_same_shapes(sibs),
        compiler_params=_params("parallel"),
    )(idx, *grads, *sibs)


def _adamw_reduced(w, own, landed, sib, m, v, idx, name):
    r, cols = w.shape
    hr = r // 2
    tr = _tile(hr, 256, 16)
    nt = hr // tr

    def body(idx_ref, w_ref, p_ref, l_ref, s_ref, m_ref, v_ref, go_ref, d_ref, mo_ref, vo_ref):
        mine = p_ref[...].astype(F32)
        for k in range(3):
            mine = mine + l_ref[k].astype(F32)
        theirs = s_ref[0].astype(F32)
        for k in range(1, 4):
            theirs = theirs + s_ref[k].astype(F32)
        grad = jnp.where(pl.program_id(0) // nt == idx_ref[1], mine, theirs)
        go_ref[...] = grad
        d_ref[...], mo_ref[...], vo_ref[...] = _adamw_update(w_ref[...], grad, m_ref[...], v_ref[...])

    def in_half(t, half):
        return jnp.clip(t - half * nt, 0, nt - 1)

    full = pl.BlockSpec((tr, cols), lambda t, idx_ref: (t, 0))
    return _hosted(
        body, name=name,
        grid_spec=pltpu.PrefetchScalarGridSpec(
            num_scalar_prefetch=1, grid=(2 * nt,),
            in_specs=[full,
                      pl.BlockSpec((None, tr, cols), lambda t, idx_ref: (idx_ref[0], in_half(t, idx_ref[1]), 0)),
                      pl.BlockSpec((3, tr, cols), lambda t, idx_ref: (0, in_half(t, idx_ref[1]), 0)),
                      pl.BlockSpec((4, tr, cols), lambda t, idx_ref: (0, in_half(t, 1 - idx_ref[1]), 0)),
                      full, full],
            out_specs=[full] * 4),
        out_shape=[jax.ShapeDtypeStruct((r, cols), F32)] * 4,
        compiler_params=_params("arbitrary"),
    )(idx, w, own, landed, sib, m, v)


def _sum_devices(blocks, name):
    m8, n = blocks.shape
    m = m8 // N_DEV

    def body(b_ref, o_ref):
        acc = b_ref[pl.ds(0, m), :]
        for k in range(1, N_DEV):
            acc = acc + b_ref[pl.ds(k * m, m), :]
        o_ref[...] = acc

    return _hosted(
        body, name=name, out_shape=jax.ShapeDtypeStruct((m, n), F32),
        in_specs=[pl.BlockSpec(memory_space=pltpu.VMEM)], out_specs=pl.BlockSpec(memory_space=pltpu.VMEM),
        compiler_params=pltpu.CompilerParams(vmem_limit_bytes=VMEM_LIMIT),
    )(blocks)


def _adamw_update(w, grad, m, v):
    new_m = ADAM_B1 * m + (1.0 - ADAM_B1) * grad
    new_v = ADAM_B2 * v + (1.0 - ADAM_B2) * (grad * grad)
    m_hat = new_m * (1.0 / (1.0 - ADAM_B1 ** ADAM_STEP))
    v_hat = new_v * (1.0 / (1.0 - ADAM_B2 ** ADAM_STEP))
    return -ADAM_LR * (m_hat / (jnp.sqrt(v_hat) + ADAM_EPS) + ADAM_WD * w), new_m, new_v


def _adamw(w, g, m, v, name):
    r, cols = w.shape
    tr = _tile(r, 256)

    def body(w_ref, g_ref, m_ref, v_ref, go_ref, d_ref, mo_ref, vo_ref):
        grad = g_ref[...]
        go_ref[...] = grad
        d_ref[...], mo_ref[...], vo_ref[...] = _adamw_update(w_ref[...], grad, m_ref[...], v_ref[...])

    spec = pl.BlockSpec((tr, cols), lambda t: (t, 0))
    return _hosted(
        body, name=name, grid=(r // tr,), in_specs=[spec] * 4, out_specs=[spec] * 4,
        out_shape=[jax.ShapeDtypeStruct((r, cols), F32)] * 4,
        compiler_params=_params("parallel"),
    )(w, g, m, v)


def _adamw_small(grad_blocks, params, name):
    nb, npar = len(grad_blocks), len(params)

    def body(*refs):
        blocks, ins, outs = refs[:nb], refs[nb:nb + 3 * npar], refs[nb + 3 * npar:]
        for p, (w, _, _, blk, row0) in enumerate(params):
            if w.ndim == 1:
                tiled = (w.shape[0] // LANES, LANES)
                grad = blocks[blk][pl.ds(row0, tiled[0]), pl.ds(0, LANES)]
                wmv = [ins[3 * p + k][...].reshape(tiled) for k in range(3)]
            else:
                grad = blocks[blk][pl.ds(row0, w.shape[0]), :]
                wmv = [ins[3 * p + k][...] for k in range(3)]
            for k, res in enumerate((grad,) + _adamw_update(wmv[0], grad, wmv[1], wmv[2])):
                outs[4 * p + k][...] = res.reshape(w.shape)

    args = list(grad_blocks) + [a for w, m, v, _, _ in params for a in (w, m, v)]
    vmem = pl.BlockSpec(memory_space=pltpu.VMEM)
    out = _hosted(
        body, name=name, in_specs=[vmem] * len(args), out_specs=[vmem] * (4 * npar),
        out_shape=[jax.ShapeDtypeStruct(w.shape, F32) for w, _, _, _, _ in params for _ in range(4)],
    )(*args)
    return [tuple(out[4 * p:4 * p + 4]) for p in range(npar)]


WEIGHT_NAMES = (
    "ln1_0", "a0_w_in", "a0_conv", "a0_w_out", "ln2_0", "ffn0_w_gu", "ffn0_w_down",
    "ln1_1", "b1_w_grp", "b1_scale", "ln2_1", "ffn1_w_gu", "ffn1_w_down",
    "ln1_2", "c2_w_pw1", "c2_b_pw1", "c2_dw", "c2_b_dw", "c2_ln_g", "c2_ln_b", "c2_w_pw2", "c2_b_pw2",
    "ln2_2", "ffn2_w_gu", "ffn2_w_down",
    "ln1_3", "a3_w_in", "a3_conv", "a3_w_out", "ln2_3", "ffn3_w_gu", "ffn3_w_down", "ln_f")
BIG = ("a0_w_in", "a0_w_out", "ffn0_w_gu", "ffn0_w_down", "b1_w_grp", "ffn1_w_gu", "ffn1_w_down", "c2_w_pw1", "c2_w_pw2",
       "ffn2_w_gu", "ffn2_w_down", "a3_w_in", "a3_w_out", "ffn3_w_gu", "ffn3_w_down")
GROUPED = "b1_w_grp"
SMALL_SHARDED = ("a0_conv", "a3_conv", "c2_dw")
REPLICATED = tuple(n for n in WEIGHT_NAMES if n not in BIG and n not in SMALL_SHARDED)


def _pad_rows(a, mult=8):
    pad = -a.shape[0] % mult
    return a if pad == 0 else jnp.concatenate([a, jnp.zeros((pad, a.shape[1]), a.dtype)], axis=0)


def _pack_rows(parts, width):
    rows = [p.reshape(-1, width) for p in parts]
    return _pad_rows(jnp.concatenate(rows, axis=0)), [r.shape[0] for r in rows]


def _unpack_rows(packed, counts, shapes):
    out, at = [], 0
    for n, shp in zip(counts, shapes):
        out.append(packed[at:at + n].reshape(shp))
        at += n
    return out


COLUMN_SHARDED = ("w_in", "w_gu", "w_pw1")


class _Weights(dict):
    def __init__(self, bufs):
        super().__init__()
        self.bufs = bufs

    def __missing__(self, name):
        buf = self.bufs[name]
        if name == GROUPED:
            cg = buf.shape[-1]
            rq = cg // N_CHIPS
            return jnp.transpose(buf.reshape(N_CHIPS, -1, rq, cg), (1, 0, 2, 3)).reshape(-1, cg, cg)
        return buf if name.endswith(COLUMN_SHARDED) else buf.reshape(-1, buf.shape[-1])


class _Exchange:
    def __init__(self, w, mom, vel, idx):
        def shards(table):
            return {n: table[n].reshape(-1, table[n].shape[-1]) for n in BIG}

        self.w, self.mom, self.vel, self.idx = shards(w), shards(mom), shards(vel), idx
        self.bufs = {}
        self.weights = _Weights(self.bufs)
        self.grads = {}
        self.sib, self.part, self.landed, self.sib_parts, self.updates = {}, {}, {}, {}, {}

    def cast(self, names):
        by_shape = {}
        for n in names:
            by_shape.setdefault(self.w[n].shape, []).append(n)
        for group in by_shape.values():
            self.bufs.update(zip(group, _cast_to_slot([self.w[n] for n in group], self.idx, f"cast_{group[0]}")))

    @staticmethod
    def _store(table, names):
        def done(arrays):
            table.update(zip(names, arrays))
        return done

    def _grad(self, n):
        g = self.grads[n]
        if n == GROUPED:
            ng, cg, _ = g.shape
            g = jnp.transpose(g.reshape(ng, N_CHIPS, cg // N_CHIPS, cg), (1, 0, 2, 3)).astype(BF16)
        return g.reshape(N_CHIPS, -1, g.shape[-1])

    def gather_ici(self, *names, part=(0, 1)):
        return lambda: _task_gather_ici([self.bufs[n] for n in names], self._store(self.bufs, names), part)

    def gather_d2d(self, *names):
        return lambda: _task_gather_d2d([self.bufs[n] for n in names], self._store(self.bufs, names))

    def sibling_halves(self, *names):
        return lambda: _task_sibling_halves([self._grad(n) for n in names], self._store(self.sib, names))

    def add_halves(self, *names):
        def run():
            parts = _add_halves([self._grad(n) for n in names], [self.sib.pop(n) for n in names], self.idx,
                                f"reduce_add_{names[0]}")
            self.part.update(zip(names, parts))
        return run

    def chip_sums(self, *names, part=(0, 1)):
        def make():
            landed = [self.landed[n] for n in names] if part[0] > 0 else None
            return _task_chip_sums([self.part[n] for n in names], self._store(self.landed, names), landed, part)
        return make

    def sibling_parts(self, *names):
        return lambda: _task_sibling_parts([self.part[n] for n in names], [self.landed[n] for n in names],
                                           self._store(self.sib_parts, names))

    def adamw(self, *names):
        def run():
            for n in names:
                self.updates[n] = _adamw_reduced(self.w[n], self.part.pop(n), self.landed.pop(n), self.sib_parts.pop(n),
                                                 self.mom[n], self.vel[n], self.idx, f"adamw_{n}")
        return run


def _plan(ex):
    s = _Schedule()

    def ffn(i):
        return f"ffn{i}_w_gu", f"ffn{i}_w_down"

    c2, a3 = ("c2_w_pw1", "c2_w_pw2"), ("a3_w_in", "a3_w_out")
    first, second = (0, 2), (1, 2)
    s.host("cast_ffn0_w_gu", ex.gather_ici("a0_w_in", part=first))
    s.host("cast_ffn0_w_down", ex.gather_ici("a0_w_in", part=second))
    s.host("cast_c2_w_pw1", ex.gather_d2d("a0_w_in"))
    gu, down = ffn(0)
    s.host("gather_small", ex.gather_ici("a0_w_out"))
    s.host("a0_in", ex.gather_ici(gu, part=first), ex.gather_d2d("a0_w_out"))
    s.host("a0_conv", ex.gather_ici(gu, part=second))
    s.host("a0_out", ex.gather_ici(down), ex.gather_d2d(gu))
    s.host("ffn0_up", ex.gather_d2d(down))
    gu, down = ffn(1)
    s.host("ffn0_up", ex.gather_ici(gu, GROUPED))
    s.host("ffn0_down", ex.gather_ici(down), ex.gather_d2d(gu, GROUPED))
    s.host("ffn1_up", ex.gather_d2d(down), ex.gather_ici(*c2))
    gu, down = ffn(2)
    s.host("ffn1_up", ex.gather_ici(gu, part=first))
    s.host("ffn1_down", ex.gather_d2d(*c2), ex.gather_ici(down))
    s.host("c2_mid", ex.gather_ici(gu, part=second))
    s.host("c2_pw2", ex.gather_d2d(gu, down), ex.gather_ici(a3[1]))
    s.host("ffn2_up", ex.gather_ici(a3[0]))
    gu, down = ffn(3)
    s.host("ffn2_up", ex.gather_ici(gu, part=first))
    s.host("ffn2_down", ex.gather_d2d(*a3), ex.gather_ici(down))
    s.host("a3_in", ex.gather_ici(gu, part=second))
    s.host("a3_out", ex.gather_d2d(gu, down))

    def reduce_on(names, first_host, ici_hosts, last_host):
        s.host(first_host, ex.sibling_halves(*names))
        s.post(first_host, ex.add_halves(*names))
        for host, hosted, part in ici_hosts:
            s.host(host, ex.chip_sums(*hosted, part=part))
        s.host(last_host, ex.sibling_parts(*names))
        s.post(last_host, ex.adamw(*names))

    whole = (0, 1)
    gu, down = ffn(3)
    reduce_on((gu, down), "a3_out_bwd",
              [("a3_conv_bwd", (down,), whole), ("a3_dw_in", (gu,), first), ("a3_in_bwd", (gu,), second)], "ffn2_down_bwd")
    reduce_on(a3, "ffn2_down_bwd", [("ffn2_dw_gu", a3, whole)], "c2_pw2_bwd")
    gu, down = ffn(0)
    s.host("ffn0_dw_gu", ex.sibling_halves(down))
    s.post("ffn0_dw_gu", ex.add_halves(down))
    s.host("ffn0_up_bwd", ex.chip_sums(down))
    s.host("a0_out_bwd", ex.sibling_halves(gu, GROUPED))
    s.post("a0_out_bwd", ex.add_halves(gu, GROUPED))
    s.host("a0_conv_bwd", ex.chip_sums(gu, part=first), ex.chip_sums(GROUPED))
    s.host("a0_dw_in", ex.chip_sums(gu, part=second))
    s.host("a0_dw_out", ex.sibling_halves("a0_w_in"))
    s.post("a0_dw_out", ex.add_halves("a0_w_in"))
    s.host("a0_in_bwd", ex.chip_sums("a0_w_in"), ex.sibling_parts(gu, down, GROUPED))
    s.host(f"adamw_{gu}", ex.chip_sums("a0_w_out"))
    s.host(f"adamw_{down}", ex.sibling_parts("a0_w_out"))
    reduce_on(ffn(2), "c2_pw2_bwd", [("c2_mid_bwd", ffn(2), whole)], "ffn1_down_bwd")
    reduce_on(c2, "ffn1_down_bwd", [("ffn1_dw_down", c2, whole)], "b1_bwd_mm")
    gu, down = ffn(1)
    reduce_on((gu, down), "b1_bwd_mm", [("ffn0_down_bwd", (down,), whole), ("ffn0_dw_gu", (gu,), whole)], "ffn0_up_bwd")
    return s


def kernel(x, *rest):
    nw = len(WEIGHT_NAMES)
    w = dict(zip(WEIGHT_NAMES, rest[:nw]))
    target = rest[nw]
    mom = dict(zip(WEIGHT_NAMES, rest[nw + 1:2 * nw + 1]))
    vel = dict(zip(WEIGHT_NAMES, rest[2 * nw + 1:3 * nw + 1]))
    cx, cy, cc = _position()
    my_chip = 2 * cx + cy
    ex = _Exchange(w, mom, vel, jnp.stack([my_chip, cc]).astype(jnp.int32))
    _ACTIVE_SCHEDULE[0] = _plan(ex)
    try:
        return _scheduled_step(x, target, w, mom, vel, ex, my_chip)
    finally:
        _ACTIVE_SCHEDULE[0] = None


def _scheduled_step(x, target, w, mom, vel, ex, my_chip):
    d = x.shape[-1]
    cq = d // N_CHIPS
    ex.cast(BIG)

    small_blk, small_counts = _pack_rows([w[n] for n in SMALL_SHARDED], cq)
    small_all = _allgather8(small_blk, "gather_small").reshape(N_CHIPS, 2, small_blk.shape[0], cq)[:, 0]
    small_parts = _unpack_rows(jnp.transpose(small_all, (1, 0, 2)), small_counts,
                               [(w[n].reshape(-1, cq).shape[0], N_CHIPS, cq) for n in SMALL_SHARDED])
    wts = ex.weights
    for n in REPLICATED:
        wts[n] = w[n].reshape(1, -1)
    for n, part in zip(SMALL_SHARDED, small_parts):
        wts[n] = part.reshape(part.shape[0], d)

    loss, dx, g = _device_step(x[0], target[0], wts, ex.grads)

    summed, last = ("ffn0_w_gu", "ffn0_w_down", GROUPED, "a0_w_in"), "a0_w_out"
    _comm_only([ex.sibling_parts("a0_w_in")(), ex.sibling_halves(last)()], "reduce_tail_d2d")
    ex.add_halves(last)()
    ex.adamw(*summed)()
    ex.adamw(last)()
    sched = _ACTIVE_SCHEDULE[0]
    assert not sched.hosts and not sched.posts, (sched.hosts, sched.posts)

    rep_rows = [jnp.pad(g[n].reshape(-1, LANES), ((0, 0), (0, cq - LANES))) for n in REPLICATED]
    by_chip = [jnp.transpose(g[n].reshape(g[n].shape[0], N_CHIPS, cq), (1, 0, 2)) for n in SMALL_SHARDED]
    shard_rows = jnp.concatenate(by_chip, axis=1)
    n_rep, n_shard = sum(r.shape[0] for r in rep_rows), shard_rows.shape[1]
    loss_row = jnp.broadcast_to(loss, (1, cq))
    sm_blk = _pad_rows(jnp.concatenate(rep_rows + [loss_row, shard_rows.reshape(N_CHIPS * n_shard, cq)], axis=0))
    sm_sum = _sum_devices(_allgather8(sm_blk, "gather_small_grads"), "sum_small_grads")
    mine = lax.dynamic_slice_in_dim(sm_sum, n_rep + 1 + my_chip * n_shard, n_shard, axis=0)

    out = ex.updates
    params, at = [], {0: 0, 1: 0}
    for block, names in ((0, REPLICATED), (1, SMALL_SHARDED)):
        for n in names:
            params.append((w[n], mom[n], vel[n], block, at[block]))
            at[block] += w[n].size // LANES if w[n].ndim == 1 else w[n].shape[0]
    out.update(zip(REPLICATED + SMALL_SHARDED, _adamw_small([sm_sum, mine], params, "adamw_small")))

    total = sm_sum[n_rep, 0]
    grads, deltas, new_m, new_v = ([out[n][k].reshape(w[n].shape) for n in WEIGHT_NAMES] for k in range(4))
    return (total, dx.reshape(x.shape), *grads, *deltas, *new_m, *new_v)
```

```python
import functools

import jax
import jax.numpy as jnp
from jax import lax
from jax.experimental import pallas as pl
from jax.experimental.pallas import tpu as pltpu

F32 = jnp.float32
BF16 = jnp.bfloat16

RMS_EPS = 1e-6
LN_EPS = 1e-5
POOL_WINDOWS = (2, 4, 8, 16)
SHORT_CONV_W = 3
CONF_CONV_W = 31
N_CHIPS = 4
N_DEV = 8

ADAM_LR = 0.001
ADAM_B1 = 0.9
ADAM_B2 = 0.999
ADAM_EPS = 1e-08
ADAM_WD = 0.01
ADAM_STEP = 10

V7X_VMEM_BYTES = 64 * 1024 * 1024
VMEM_LIMIT = V7X_VMEM_BYTES - 8 * 1024 * 1024
LANES = 128
POOL_HALO = 16
SCONV_HALO = 16
CONF_HALO = 32


def _params(*sem):
    return pltpu.CompilerParams(dimension_semantics=sem, vmem_limit_bytes=VMEM_LIMIT)


def _tile(n, pref, mult=8):
    t = min(n, pref)
    while t > mult and (n % t or t % mult):
        t -= mult
    assert n % t == 0 and t % mult == 0, (n, pref, mult)
    return t


def _sigmoid(x):
    return jax.nn.sigmoid(x)


def _dot(a, b):
    return jnp.dot(a, b, preferred_element_type=F32)


def _dot_nt(a, b):
    return lax.dot_general(a, b, (((1,), (1,)), ((), ())), preferred_element_type=F32)


def _dot_tn(a, b):
    return lax.dot_general(a, b, (((0,), (0,)), ((), ())), preferred_element_type=F32)


def _colsum(x):
    return jnp.sum(x, axis=0, keepdims=True)


def _rms_stats(x):
    return lax.rsqrt(jnp.mean(x * x, axis=-1, keepdims=True) + RMS_EPS)


def _rms_bwd(du, x, gain):
    r = _rms_stats(x)
    xhat = x * r
    gdy = du * gain
    dx = r * (gdy - xhat * jnp.mean(gdy * xhat, axis=-1, keepdims=True))
    return dx, _colsum(du * xhat)


class _Task:
    def __init__(self, ins, out_shapes, aliases, sems, start, wait, done):
        self.ins, self.out_shapes, self.aliases, self.sems = list(ins), list(out_shapes), dict(aliases), list(sems)
        self.start, self.wait, self.done = start, wait, done


class _Schedule:
    def __init__(self):
        self.hosts, self.posts = {}, {}

    def host(self, kernel_name, *make_tasks):
        self.hosts.setdefault(kernel_name, []).extend(make_tasks)

    def post(self, kernel_name, *thunks):
        self.posts.setdefault(kernel_name, []).extend(thunks)

    def tasks_for(self, kernel_name):
        return [make() for make in self.hosts.pop(kernel_name, ())]

    def finished(self, kernel_name):
        for thunk in self.posts.pop(kernel_name, ()):
            thunk()


_ACTIVE_SCHEDULE = [None]


def _hosted(body, name, **kw):
    def run(*args):
        sched = _ACTIVE_SCHEDULE[0]
        tasks = sched.tasks_for(name) if sched is not None else []
        out = _call_with_tasks(body, name, tasks, kw, args) if tasks else pl.pallas_call(body, name=name, **kw)(*args)
        if sched is not None:
            sched.finished(name)
        return out

    return run


def _call_with_tasks(body, name, tasks, kw, args):
    spec = kw.get("grid_spec")
    n_pre = spec.num_scalar_prefetch if spec is not None else 0
    src = dict(grid=spec.grid, in_specs=spec.in_specs, out_specs=spec.out_specs) if spec is not None else kw
    pre, args = args[:n_pre], args[n_pre:]
    grid = tuple(src.get("grid", ()))
    single = not isinstance(kw["out_shape"], (list, tuple))
    out_shape = [kw["out_shape"]] if single else list(kw["out_shape"])
    out_specs = [src["out_specs"]] if single else list(src["out_specs"])
    scratch = list(kw.get("scratch_shapes", ()))
    n_in, n_out, n_scr = len(args), len(out_shape), len(scratch)
    t_in = [a for t in tasks for a in t.ins]
    t_out = [o for t in tasks for o in t.out_shapes]
    t_sem = [s for t in tasks for s in t.sems]
    aliases, at_in, at_out = {}, n_pre + n_in, n_out
    for t in tasks:
        for i, o in t.aliases.items():
            aliases[at_in + i] = at_out + o
        at_in += len(t.ins)
        at_out += len(t.out_shapes)

    def wrapped(*refs):
        pre_refs, refs = refs[:n_pre], refs[n_pre:]
        a = n_in
        b = a + len(t_in)
        c = b + n_out
        d = c + len(t_out)
        e = d + n_scr
        ins, tins, outs, touts, scr, tsems = refs[:a], refs[a:b], refs[b:c], refs[c:d], refs[d:e], refs[e:]
        views, i0, o0, s0 = [], 0, 0, 0
        for t in tasks:
            views.append((tins[i0:i0 + len(t.ins)], touts[o0:o0 + len(t.out_shapes)], tsems[s0:s0 + len(t.sems)]))
            i0, o0, s0 = i0 + len(t.ins), o0 + len(t.out_shapes), s0 + len(t.sems)

        def start_all():
            for t, v in zip(tasks, views):
                t.start(*v)

        def wait_all():
            for t, v in zip(tasks, views):
                t.wait(*v)

        if grid:
            first = functools.reduce(jnp.logical_and, [pl.program_id(i) == 0 for i in range(len(grid))])
            last = functools.reduce(jnp.logical_and, [pl.program_id(i) == grid[i] - 1 for i in range(len(grid))])
            pl.when(first)(start_all)
            body(*pre_refs, *ins, *outs, *scr)
            pl.when(last)(wait_all)
        else:
            start_all()
            body(*pre_refs, *ins, *outs, *scr)
            wait_all()

    in_specs = list(src["in_specs"]) + [ANY] * len(t_in)
    out_specs = out_specs + [ANY] * len(t_out)
    if spec is not None:
        layout = dict(grid_spec=pltpu.PrefetchScalarGridSpec(
            num_scalar_prefetch=n_pre, grid=grid, in_specs=in_specs, out_specs=out_specs, scratch_shapes=scratch + t_sem))
    else:
        layout = dict(grid=grid, in_specs=in_specs, out_specs=out_specs, scratch_shapes=scratch + t_sem)
    res = pl.pallas_call(
        wrapped, name=name, out_shape=out_shape + t_out, input_output_aliases=aliases,
        compiler_params=pltpu.CompilerParams(dimension_semantics=("arbitrary",) * len(grid), vmem_limit_bytes=VMEM_LIMIT),
        **layout,
    )(*pre, *args, *t_in)
    res = list(res)
    own, rest = res[:n_out], res[n_out:]
    for t in tasks:
        t.done(rest[:len(t.out_shapes)])
        rest = rest[len(t.out_shapes):]
    return own[0] if single else own


def _comm_only(tasks, name):
    _call_with_tasks(lambda: None, name, tasks, dict(grid=(), in_specs=[], out_specs=[], out_shape=[]), ())


def _mm_col(h, gain, w, bias, name):
    s, k = h.shape
    nsh, _, ns = w.shape
    tm = _tile(s, 512)
    has_bias = bias is not None

    def body(h_ref, gain_ref, w_ref, *rest):
        u_ref, o_ref = rest[-2:]
        x = h_ref[...]
        x = (x * _rms_stats(x) * gain_ref[...]).astype(BF16)
        u_ref[...] = x
        for j in range(nsh):
            cols = pl.ds(j * ns, ns)
            acc = _dot(x, w_ref[j])
            if has_bias:
                acc = acc + rest[0][:, cols]
            o_ref[:, cols] = acc.astype(o_ref.dtype)

    tokens = pl.BlockSpec((tm, k), lambda m: (m, 0))
    in_specs = [tokens, pl.BlockSpec((1, k), lambda m: (0, 0)), pl.BlockSpec((nsh, k, ns), lambda m: (0, 0, 0))]
    args = [h, gain, w]
    if has_bias:
        in_specs.append(pl.BlockSpec((1, nsh * ns), lambda m: (0, 0)))
        args.append(bias)
    return _hosted(
        body, name=name, grid=(s // tm,), in_specs=in_specs,
        out_specs=[tokens, pl.BlockSpec((tm, nsh * ns), lambda m: (m, 0))],
        out_shape=[jax.ShapeDtypeStruct((s, k), BF16), jax.ShapeDtypeStruct((s, nsh * ns), BF16)],
        compiler_params=_params("parallel"),
    )(*args)


def _mm_row(a, w, res, bias, name):
    s = a.shape[0]
    k, n = w.shape
    tm = _tile(s, 512)
    has_bias = bias is not None

    def body(a_ref, w_ref, res_ref, *rest):
        o_ref = rest[-1]
        y = res_ref[...] + _dot(a_ref[...], w_ref[...])
        if has_bias:
            y = y + rest[0][...]
        o_ref[...] = y

    in_specs = [pl.BlockSpec((tm, k), lambda m: (m, 0)), pl.BlockSpec((k, n), lambda m: (0, 0)),
                pl.BlockSpec((tm, n), lambda m: (m, 0))]
    args = [a, w, res]
    if has_bias:
        in_specs.append(pl.BlockSpec((1, n), lambda m: (0, 0)))
        args.append(bias)
    return _hosted(
        body, name=name, grid=(s // tm,), in_specs=in_specs,
        out_specs=pl.BlockSpec((tm, n), lambda m: (m, 0)),
        out_shape=jax.ShapeDtypeStruct((s, n), F32),
        compiler_params=_params("parallel"),
    )(*args)


def _mm_nt_row(dy, w, name):
    s, n = dy.shape
    k = w.shape[0]
    tm = _tile(s, 512)

    def body(dy_ref, w_ref, o_ref):
        o_ref[...] = _dot_nt(dy_ref[...].astype(BF16), w_ref[...])

    return _hosted(
        body, name=name, grid=(s // tm,),
        in_specs=[pl.BlockSpec((tm, n), lambda m: (m, 0)), pl.BlockSpec((k, n), lambda m: (0, 0))],
        out_specs=pl.BlockSpec((tm, k), lambda m: (m, 0)),
        out_shape=jax.ShapeDtypeStruct((s, k), F32),
        compiler_params=_params("parallel"),
    )(dy, w)


def _ffn_up(h, gain, w, name):
    s, d = h.shape
    _, _, ns = w.shape
    tm = _tile(s, 512)

    def body(h_ref, gain_ref, wg_ref, wu_ref, u_ref, act_ref, s1_ref, q1_ref):
        x = h_ref[...]
        x = (x * _rms_stats(x) * gain_ref[...]).astype(BF16)

        @pl.when(pl.program_id(0) == 0)
        def _():
            u_ref[...] = x

        g = _dot(x, wg_ref[...])
        up = _dot(x, wu_ref[...])
        sg = _sigmoid(g)
        s1 = g * sg
        act_ref[...] = (s1 * up).astype(act_ref.dtype)
        s1_ref[...] = s1.astype(s1_ref.dtype)
        q1_ref[...] = (up * sg * (1.0 + g * (1.0 - sg))).astype(q1_ref.dtype)

    out = pl.BlockSpec((tm, ns), lambda j, m: (m, j))
    tokens = pl.BlockSpec((tm, d), lambda j, m: (m, 0))
    nm = s // tm
    u_once = pl.BlockSpec((tm, d), lambda j, m: (jnp.where(j == 0, m, nm - 1), 0))
    return _hosted(
        body, name=name, grid=(2, nm),
        in_specs=[tokens, pl.BlockSpec((1, d), lambda j, m: (0, 0)), pl.BlockSpec((None, d, ns), lambda j, m: (j, 0, 0)),
                  pl.BlockSpec((None, d, ns), lambda j, m: (j + 2, 0, 0))],
        out_specs=[u_once, out, out, out],
        out_shape=[jax.ShapeDtypeStruct((s, d), BF16)] + [jax.ShapeDtypeStruct((s, 2 * ns), BF16)] * 3,
        compiler_params=_params("arbitrary", "arbitrary"),
    )(h, gain, w, w)


def _ffn_down_bwd(dh, w, s1, q1, name):
    s, d = dh.shape
    f = w.shape[0]
    tm = _tile(s, 256)

    def body(dh_ref, w_ref, s1_ref, q1_ref, o_ref):
        da = _dot_nt(dh_ref[...].astype(BF16), w_ref[...])
        o_ref[:, :f] = (da * q1_ref[...].astype(F32)).astype(o_ref.dtype)
        o_ref[:, f:] = (da * s1_ref[...].astype(F32)).astype(o_ref.dtype)

    return _hosted(
        body, name=name, grid=(s // tm,),
        in_specs=[pl.BlockSpec((tm, d), lambda m: (m, 0)), pl.BlockSpec((f, d), lambda m: (0, 0)),
                  pl.BlockSpec((tm, f), lambda m: (m, 0)), pl.BlockSpec((tm, f), lambda m: (m, 0))],
        out_specs=pl.BlockSpec((tm, 2 * f), lambda m: (m, 0)),
        out_shape=jax.ShapeDtypeStruct((s, 2 * f), BF16),
        compiler_params=_params("parallel"),
    )(dh, w, s1, q1)


def _mm_nt_col_rms_bwd(dy, w, h, gain, dh, name):
    s = dy.shape[0]
    nsh, k, ns = w.shape
    tm = _tile(s, 256)

    def body(dy_ref, w_ref, h_ref, g_ref, dh_ref, o_ref, dg_ref):
        du = _dot_nt(dy_ref[:, :ns], w_ref[0])
        for j in range(1, nsh):
            du = du + _dot_nt(dy_ref[:, j * ns:(j + 1) * ns], w_ref[j])
        dx, dg = _rms_bwd(du, h_ref[...], g_ref[...])
        o_ref[...] = dh_ref[...] + dx
        _accumulate(dg_ref, dg, pl.program_id(0) == 0)

    return _hosted(
        body, name=name, grid=(s // tm,),
        in_specs=[pl.BlockSpec((tm, nsh * ns), lambda m: (m, 0)), pl.BlockSpec((nsh, k, ns), lambda m: (0, 0, 0)),
                  pl.BlockSpec((tm, k), lambda m: (m, 0)), pl.BlockSpec((1, k), lambda m: (0, 0)),
                  pl.BlockSpec((tm, k), lambda m: (m, 0))],
        out_specs=[pl.BlockSpec((tm, k), lambda m: (m, 0)), pl.BlockSpec((1, k), lambda m: (0, 0))],
        out_shape=[jax.ShapeDtypeStruct((s, k), F32), jax.ShapeDtypeStruct((1, k), F32)],
        compiler_params=_params("arbitrary"),
    )(dy, w, h, gain, dh)


def _mm_tn(a, dy, nsh, name):
    s, k = a.shape
    ns = dy.shape[1] // nsh
    tm = _tile(s, 2048 if dy.dtype == BF16 else 1024)
    tk = _tile(k, 1408, LANES)
    nk, nm = k // tk, s // tm

    def body(a_ref, dy_ref, o_ref, acc_ref):
        m = pl.program_id(2)
        part = _dot_tn(a_ref[...], dy_ref[...].astype(BF16))

        @pl.when(m == 0)
        def _():
            acc_ref[...] = part

        @pl.when(m > 0)
        def _():
            acc_ref[...] += part

        @pl.when(m == nm - 1)
        def _():
            o_ref[...] = acc_ref[...].astype(o_ref.dtype)

    return _hosted(
        body, name=name, grid=(nsh, nk, nm),
        in_specs=[pl.BlockSpec((tm, tk), lambda j, kk, m: (m, kk)), pl.BlockSpec((tm, ns), lambda j, kk, m: (m, j))],
        out_specs=pl.BlockSpec((None, tk, ns), lambda j, kk, m: (j, kk, 0)),
        out_shape=jax.ShapeDtypeStruct((nsh, k, ns), BF16),
        scratch_shapes=[pltpu.VMEM((tk, ns), F32)],
        compiler_params=_params("parallel", "parallel", "arbitrary"),
    )(a, dy)


def _main_spec(tm, w):
    return pl.BlockSpec((tm, w), lambda m: (m, 0))


def _before_spec(tm, hb, w):
    return pl.BlockSpec((hb, w), lambda m: (jnp.maximum(m * (tm // hb) - 1, 0), 0))


def _after_spec(tm, hb, w, s):
    return pl.BlockSpec((hb, w), lambda m: (jnp.minimum((m + 1) * (tm // hb), s // hb - 1), 0))


def _row_spec(w, rows=1):
    return pl.BlockSpec((rows, w), lambda m: (0, 0))


CHUNK_LANES = 4 * LANES
CHUNK_ROWS = 32


def _build_shifts(ext8_ref, residues=range(1, 8)):
    n = ext8_ref.shape[1] - 8
    for r in residues:
        ext8_ref[r, pl.ds(0, n), :] = ext8_ref[0, pl.ds(r, n), :]


def _fold_rows(x):
    return functools.reduce(lambda p, q: p + q, [x[i:i + 8] for i in range(0, x.shape[0], 8)])


def _shifted(ext8_ref, shift, r0, rows, cols):
    return ext8_ref[shift % 8, pl.ds(pl.multiple_of(shift - shift % 8 + r0, 8), rows), cols]


def _lane_chunk(i):
    return pl.ds(pl.multiple_of(i * CHUNK_LANES, CHUNK_LANES), CHUNK_LANES)


def _sum_terms(terms, ways=4):
    accs = []
    for i, t in enumerate(terms):
        if i < ways:
            accs.append(t)
        else:
            accs[i % ways] = accs[i % ways] + t
    while len(accs) > 1:
        accs = [accs[i] + accs[i + 1] if i + 1 < len(accs) else accs[i] for i in range(0, len(accs), 2)]
    return accs[0]


def _accumulate(ref, val, first):
    @pl.when(first)
    def _():
        ref[...] = val

    @pl.when(jnp.logical_not(first))
    def _():
        ref[...] += val


SCONV_Z_SHIFTS = tuple(SCONV_HALO - (SHORT_CONV_W - 1) + k for k in range(SHORT_CONV_W))


def _sconv_z_taps(zext_ref, r0, cols):
    return [_shifted(zext_ref, shift, r0, CHUNK_ROWS, cols) for shift in SCONV_Z_SHIFTS]


def _weighted(cw_ref, cols, terms):
    return _sum_terms((cw_ref[k:k + 1, cols] * t for k, t in enumerate(terms)), ways=len(terms))


def _sconv_fill_z(zext_ref, main_ref, before_ref, d, m):
    hb = SCONV_HALO
    zb = before_ref[:, d:2 * d].astype(F32) * before_ref[:, 2 * d:].astype(F32)
    zext_ref[pl.ds(0, hb), :] = jnp.where(m > 0, zb, 0.0)
    zext_ref[pl.ds(hb, main_ref.shape[0]), :] = main_ref[:, d:2 * d].astype(F32) * main_ref[:, 2 * d:].astype(F32)


def _sconv_fwd(bcv, cw, name):
    s, d3 = bcv.shape
    d = d3 // 3
    tm = _tile(s, 256, CHUNK_ROWS)
    row_chunks = tm // CHUNK_ROWS

    def body(main_ref, before_ref, cw_ref, p_ref, zext_ref):
        m = pl.program_id(0)
        _sconv_fill_z(zext_ref.at[0], main_ref, before_ref, d, m)
        _build_shifts(zext_ref, [shift % 8 for shift in SCONV_Z_SHIFTS if shift % 8])

        def chunk(i, carry):
            cols = _lane_chunk(i // row_chunks)
            r0 = pl.multiple_of((i % row_chunks) * CHUNK_ROWS, CHUNK_ROWS)
            rows = pl.ds(r0, CHUNK_ROWS)
            zc = _weighted(cw_ref, cols, _sconv_z_taps(zext_ref, r0, cols))
            p_ref[rows, cols] = (main_ref[rows, cols].astype(F32) * zc).astype(p_ref.dtype)
            return carry

        lax.fori_loop(0, row_chunks * (d // CHUNK_LANES), chunk, 0)

    return _hosted(
        body, name=name, grid=(s // tm,),
        in_specs=[_main_spec(tm, d3), _before_spec(tm, SCONV_HALO, d3), _row_spec(d, SHORT_CONV_W)],
        out_specs=_main_spec(tm, d),
        out_shape=jax.ShapeDtypeStruct((s, d), BF16),
        scratch_shapes=[pltpu.VMEM((8, tm + SCONV_HALO, d), F32)],
        compiler_params=_params("parallel"),
    )(bcv, bcv, cw)


def _sconv_bwd(dp, bcv, cw, name):
    s, d3 = bcv.shape
    d = d3 // 3
    tm = _tile(s, 256, CHUNK_ROWS)
    nm = s // tm
    ha = 8
    kw = SHORT_CONV_W

    def body(dp_ref, dpa_ref, main_ref, before_ref, after_ref, cw_ref, o_ref, dcw_ref, zext_ref, dext_ref):
        m = pl.program_id(0)
        _sconv_fill_z(zext_ref.at[0], main_ref, before_ref, d, m)
        _build_shifts(zext_ref, [shift % 8 for shift in SCONV_Z_SHIFTS if shift % 8])
        dext_ref[0, pl.ds(0, tm), :] = dp_ref[...] * main_ref[:, :d].astype(F32)
        dza = dpa_ref[...] * after_ref[:, :d].astype(F32)[0:ha]
        dext_ref[0, pl.ds(tm, ha), :] = jnp.where(m < nm - 1, dza, 0.0)
        _build_shifts(dext_ref, range(1, kw))

        @pl.when(m == 0)
        def _():
            dcw_ref[...] = jnp.zeros_like(dcw_ref)

        zero = jnp.zeros((8, CHUNK_LANES), F32)

        def lane_chunk(ci, carry):
            cols = _lane_chunk(ci)
            c_cols, v_cols = (pl.ds(pl.multiple_of(part * d + ci * CHUNK_LANES, CHUNK_LANES), CHUNK_LANES) for part in (1, 2))

            def row_chunk(ri, sums):
                r0 = pl.multiple_of(ri * CHUNK_ROWS, CHUNK_ROWS)
                rows = pl.ds(r0, CHUNK_ROWS)
                z = _sconv_z_taps(zext_ref, r0, cols)
                o_ref[rows, cols] = (dp_ref[rows, cols] * _weighted(cw_ref, cols, z)).astype(o_ref.dtype)
                dzc = [_shifted(dext_ref, kw - 1 - k, r0, CHUNK_ROWS, cols) for k in range(kw)]
                dz = _weighted(cw_ref, cols, dzc)
                o_ref[rows, c_cols] = (dz * main_ref[rows, v_cols].astype(F32)).astype(o_ref.dtype)
                o_ref[rows, v_cols] = (dz * main_ref[rows, c_cols].astype(F32)).astype(o_ref.dtype)
                return tuple(acc + _fold_rows(dzc[kw - 1] * z[k]) for k, acc in enumerate(sums))

            sums = lax.fori_loop(0, tm // CHUNK_ROWS, row_chunk, (zero,) * kw)
            for k in range(kw):
                dcw_ref[k:k + 1, cols] += _colsum(sums[k])
            return carry

        lax.fori_loop(0, d // CHUNK_LANES, lane_chunk, 0)

    return _hosted(
        body, name=name, grid=(nm,),
        in_specs=[_main_spec(tm, d), _after_spec(tm, ha, d, s), _main_spec(tm, d3), _before_spec(tm, SCONV_HALO, d3),
                  _after_spec(tm, SCONV_HALO, d3, s), _row_spec(d, SHORT_CONV_W)],
        out_specs=[_main_spec(tm, d3), _row_spec(d, 8)],
        out_shape=[jax.ShapeDtypeStruct((s, d3), BF16), jax.ShapeDtypeStruct((8, d), F32)],
        scratch_shapes=[pltpu.VMEM((8, tm + SCONV_HALO, d), F32), pltpu.VMEM((8, tm + ha, d), F32)],
        compiler_params=_params("arbitrary"),
    )(dp, dp, bcv, bcv, bcv, cw)


def _pool_counts(t0, tm, w):
    t = t0 + lax.broadcasted_iota(jnp.int32, (tm, 1), 0)
    return jnp.minimum(t + 1, w).astype(F32)


def _pool_fwd(h, gain, wg, scale, name):
    s, d = h.shape
    ng, cg, _ = wg.shape
    tm = _tile(s, 512, POOL_HALO)

    def body(h_ref, hb_ref, g_ref, wg_ref, sc_ref, o_ref, mx_ref, uext_ref):
        m = pl.program_id(0)
        x = h_ref[...]
        gain_row = g_ref[...]
        xb = hb_ref[...]
        uext_ref[pl.ds(0, POOL_HALO), :] = jnp.where(m > 0, xb * _rms_stats(xb) * gain_row, 0.0)
        uext_ref[pl.ds(POOL_HALO, tm), :] = x * _rms_stats(x) * gain_row
        for gi, win in enumerate(POOL_WINDOWS):
            cols = pl.ds(gi * cg, cg)
            u_g = uext_ref[pl.ds(POOL_HALO, tm), cols]
            acc = u_g
            for i in range(1, win):
                acc = acc + uext_ref[pl.ds(POOL_HALO - i, tm), cols]
            mixed = (acc / _pool_counts(m * tm, tm, win) - u_g).astype(BF16)
            mx_ref[:, cols] = mixed
            o_ref[:, cols] = x[:, gi * cg:(gi + 1) * cg] + _dot(mixed, wg_ref[gi]) * sc_ref[:, cols]

    return _hosted(
        body, name=name, grid=(s // tm,),
        in_specs=[_main_spec(tm, d), _before_spec(tm, POOL_HALO, d), _row_spec(d),
                  pl.BlockSpec((ng, cg, cg), lambda m: (0, 0, 0)), _row_spec(d)],
        out_specs=[_main_spec(tm, d), _main_spec(tm, d)],
        out_shape=[jax.ShapeDtypeStruct((s, d), F32), jax.ShapeDtypeStruct((s, d), BF16)],
        scratch_shapes=[pltpu.VMEM((tm + POOL_HALO, d), F32)],
        compiler_params=_params("parallel"),
    )(h, h, gain, wg, scale)


def _pool_bwd_mm(dh, mixed, wg, scale, name):
    s, d = dh.shape
    ng, cg, _ = wg.shape
    tm = _tile(s, 512)

    def body(dh_ref, mx_ref, wg_ref, sc_ref, dmx_ref, dwg_ref, dsc_ref):
        first = pl.program_id(0) == 0
        for gi in range(ng):
            cols = pl.ds(gi * cg, cg)
            dh_g = dh_ref[:, cols]
            mixed = mx_ref[:, cols]
            w_g = wg_ref[gi]
            dy = (dh_g * sc_ref[:, cols]).astype(BF16)
            dmx_ref[:, cols] = _dot_nt(dy, w_g)
            _accumulate(dsc_ref.at[:, cols], _colsum(dh_g * _dot(mixed, w_g)), first)
            _accumulate(dwg_ref.at[gi], _dot_tn(mixed, dy), first)

    return _hosted(
        body, name=name, grid=(s // tm,),
        in_specs=[_main_spec(tm, d), _main_spec(tm, d), pl.BlockSpec((ng, cg, cg), lambda m: (0, 0, 0)), _row_spec(d)],
        out_specs=[_main_spec(tm, d), pl.BlockSpec((ng, cg, cg), lambda m: (0, 0, 0)), _row_spec(d)],
        out_shape=[jax.ShapeDtypeStruct((s, d), F32), jax.ShapeDtypeStruct((ng, cg, cg), F32),
                   jax.ShapeDtypeStruct((1, d), F32)],
        compiler_params=_params("arbitrary"),
    )(dh, mixed, wg, scale)


def _pool_bwd_rms(dmixed, h, gain, dh, name):
    s, d = h.shape
    cg = d // len(POOL_WINDOWS)
    tm = _tile(s, 512, POOL_HALO)
    nm = s // tm

    def body(dmx_ref, dmxa_ref, h_ref, g_ref, dh_ref, o_ref, dg_ref, eext_ref, du_ref):
        m = pl.program_id(0)
        for gi, win in enumerate(POOL_WINDOWS):
            cols = pl.ds(gi * cg, cg)
            dmx = dmx_ref[:, cols]
            eext_ref[pl.ds(0, tm), cols] = dmx / _pool_counts(m * tm, tm, win)
            ea = dmxa_ref[:, cols] / _pool_counts((m + 1) * tm, POOL_HALO, win)
            eext_ref[pl.ds(tm, POOL_HALO), cols] = jnp.where(m < nm - 1, ea, 0.0)
            acc = -dmx
            for i in range(win):
                acc = acc + eext_ref[pl.ds(i, tm), cols]
            du_ref[:, cols] = acc
        dx, dg = _rms_bwd(du_ref[...], h_ref[...], g_ref[...])
        o_ref[...] = dh_ref[...] + dx
        _accumulate(dg_ref, dg, m == 0)

    return _hosted(
        body, name=name, grid=(nm,),
        in_specs=[_main_spec(tm, d), _after_spec(tm, POOL_HALO, d, s), _main_spec(tm, d), _row_spec(d), _main_spec(tm, d)],
        out_specs=[_main_spec(tm, d), _row_spec(d)],
        out_shape=[jax.ShapeDtypeStruct((s, d), F32), jax.ShapeDtypeStruct((1, d), F32)],
        scratch_shapes=[pltpu.VMEM((tm + POOL_HALO, d), F32), pltpu.VMEM((tm, d), F32)],
        compiler_params=_params("arbitrary"),
    )(dmixed, dmixed, h, gain, dh)


def _conf_fill_h(hext_ref, main_ref, before_ref, d, m):
    hb = before_ref[:, :d].astype(F32) * _sigmoid(before_ref[:, d:].astype(F32))
    hext_ref[pl.ds(0, CONF_HALO), :] = jnp.where(m > 0, hb, 0.0)
    hext_ref[pl.ds(CONF_HALO, main_ref.shape[0]), :] = main_ref[:, :d].astype(F32) * _sigmoid(main_ref[:, d:].astype(F32))


def _layernorm_parts(hc, g, b):
    mu = jnp.mean(hc, axis=-1, keepdims=True)
    xc = hc - mu
    rs = lax.rsqrt(jnp.mean(xc * xc, axis=-1, keepdims=True) + LN_EPS)
    xhat = xc * rs
    return xhat, rs, xhat * g + b


def _conf_mid_fwd(ag, dw, b_dw, ln_g, ln_b, name):
    s, d2 = ag.shape
    d = d2 // 2
    tm = _tile(s, 256, CONF_HALO)
    base = CONF_HALO - (CONF_CONV_W - 1)

    def body(main_ref, before_ref, dw_ref, bdw_ref, g_ref, b_ref, s_ref, hc_ref, hext_ref):
        m = pl.program_id(0)
        _conf_fill_h(hext_ref.at[0], main_ref, before_ref, d, m)
        _build_shifts(hext_ref)
        row_chunks = tm // CHUNK_ROWS

        def conv_chunk(i, carry):
            cols = _lane_chunk(i // row_chunks)
            r0 = pl.multiple_of((i % row_chunks) * CHUNK_ROWS, CHUNK_ROWS)
            taps = (dw_ref[kk:kk + 1, cols] * _shifted(hext_ref, base + kk, r0, CHUNK_ROWS, cols) for kk in range(CONF_CONV_W))
            hc_ref[pl.ds(r0, CHUNK_ROWS), cols] = bdw_ref[:, cols] + _sum_terms(taps, ways=1)
            return carry

        lax.fori_loop(0, row_chunks * (d // CHUNK_LANES), conv_chunk, 0)
        _, _, l = _layernorm_parts(hc_ref[...], g_ref[...], b_ref[...])
        s_ref[...] = (l * _sigmoid(l)).astype(s_ref.dtype)

    return _hosted(
        body, name=name, grid=(s // tm,),
        in_specs=[_main_spec(tm, d2), _before_spec(tm, CONF_HALO, d2), _row_spec(d, CONF_CONV_W), _row_spec(d),
                  _row_spec(d), _row_spec(d)],
        out_specs=[_main_spec(tm, d), _main_spec(tm, d)],
        out_shape=[jax.ShapeDtypeStruct((s, d), BF16), jax.ShapeDtypeStruct((s, d), F32)],
        scratch_shapes=[pltpu.VMEM((8, tm + CONF_HALO, d), F32)],
        compiler_params=_params("parallel"),
    )(ag, ag, dw, b_dw, ln_g, ln_b)


def _conf_out_bwd(dh, w, hc, ln_g, ln_b, name):
    s, d = dh.shape
    tm = _tile(s, 256)

    def body(dh_ref, w_ref, hc_ref, g_ref, b_ref, o_ref, dg_ref, db_ref, dbo_ref):
        first = pl.program_id(0) == 0
        dh_t = dh_ref[...]
        ds = _dot_nt(dh_t.astype(BF16), w_ref[...])
        xhat, rs, l = _layernorm_parts(hc_ref[...], g_ref[...], b_ref[...])
        sg = _sigmoid(l)
        dl = ds * sg * (1.0 + l * (1.0 - sg))
        dxh = dl * g_ref[...]
        o_ref[...] = rs * (dxh - jnp.mean(dxh, axis=-1, keepdims=True)
                           - xhat * jnp.mean(dxh * xhat, axis=-1, keepdims=True))
        _accumulate(dg_ref, _colsum(dl * xhat), first)
        _accumulate(db_ref, _colsum(dl), first)
        _accumulate(dbo_ref, _colsum(dh_t), first)

    return _hosted(
        body, name=name, grid=(s // tm,),
        in_specs=[_main_spec(tm, d), pl.BlockSpec((d, d), lambda m: (0, 0)), _main_spec(tm, d), _row_spec(d), _row_spec(d)],
        out_specs=[_main_spec(tm, d), _row_spec(d), _row_spec(d), _row_spec(d)],
        out_shape=[jax.ShapeDtypeStruct((s, d), F32)] + [jax.ShapeDtypeStruct((1, d), F32)] * 3,
        compiler_params=_params("arbitrary"),
    )(dh, w, hc, ln_g, ln_b)


def _conf_mid_bwd(dhc, ag, dw, name):
    s, d2 = ag.shape
    d = d2 // 2
    tm = _tile(s, 256, CONF_HALO)
    nm = s // tm
    kw = CONF_CONV_W
    base = CONF_HALO - (kw - 1)

    def body(dhc_ref, dhca_ref, main_ref, before_ref, dw_ref, o_ref, ddw_ref, dbdw_ref, dbpw_ref, hext_ref, dext_ref):
        m = pl.program_id(0)
        first = m == 0
        _conf_fill_h(hext_ref.at[0], main_ref, before_ref, d, m)
        _build_shifts(hext_ref)
        dext_ref[0, pl.ds(0, tm), :] = dhc_ref[...]
        dext_ref[0, pl.ds(tm, CONF_HALO), :] = jnp.where(m < nm - 1, dhca_ref[...], 0.0)
        _build_shifts(dext_ref)

        @pl.when(first)
        def _():
            ddw_ref[...] = jnp.zeros_like(ddw_ref)
            dbdw_ref[...] = jnp.zeros_like(dbdw_ref)
            dbpw_ref[...] = jnp.zeros_like(dbpw_ref)

        zero = jnp.zeros((8, CHUNK_LANES), F32)
        tap_group = 8

        def fold(x):
            return functools.reduce(lambda p, q: p + q, [x[i:i + 8] for i in range(0, CHUNK_ROWS, 8)])

        def lane_chunk(ci, carry):
            cols = _lane_chunk(ci)
            gate_cols = pl.ds(pl.multiple_of(d + ci * CHUNK_LANES, CHUNK_LANES), CHUNK_LANES)

            def through_conv(ri, sums):
                r0 = pl.multiple_of(ri * CHUNK_ROWS, CHUNK_ROWS)
                rows = pl.ds(r0, CHUNK_ROWS)
                dhh = _sum_terms((dw_ref[kk:kk + 1, cols] * _shifted(dext_ref, kw - 1 - kk, r0, CHUNK_ROWS, cols)
                                  for kk in range(kw)), ways=1)
                a = main_ref[rows, cols].astype(F32)
                sg = _sigmoid(main_ref[rows, gate_cols].astype(F32))
                da = dhh * sg
                dgate = dhh * a * sg * (1.0 - sg)
                o_ref[rows, cols] = da.astype(o_ref.dtype)
                o_ref[rows, gate_cols] = dgate.astype(o_ref.dtype)
                return sums[0] + fold(da), sums[1] + fold(dgate), sums[2] + fold(dext_ref[0, rows, cols])

            sum_da, sum_dgate, sum_dhc = lax.fori_loop(0, tm // CHUNK_ROWS, through_conv, (zero, zero, zero))
            dbdw_ref[:, cols] += _colsum(sum_dhc)
            dbpw_ref[:, cols] += _colsum(sum_da)
            dbpw_ref[:, gate_cols] += _colsum(sum_dgate)

            for k0 in range(0, kw, tap_group):
                group = range(k0, min(k0 + tap_group, kw))

                def tap_gradients(ri, accs, group=group):
                    for sub in range(0, CHUNK_ROWS, 8):
                        r0 = pl.multiple_of(ri * CHUNK_ROWS + sub, 8)
                        dhc_c = dext_ref[0, pl.ds(r0, 8), cols]
                        accs = tuple(acc + dhc_c * _shifted(hext_ref, base + kk, r0, 8, cols) for kk, acc in zip(group, accs))
                    return accs

                accs = lax.fori_loop(0, tm // CHUNK_ROWS, tap_gradients, (zero,) * len(group))
                for kk, acc in zip(group, accs):
                    ddw_ref[kk:kk + 1, cols] += _colsum(acc)
            return carry

        lax.fori_loop(0, d // CHUNK_LANES, lane_chunk, 0)

    return _hosted(
        body, name=name, grid=(nm,),
        in_specs=[_main_spec(tm, d), _after_spec(tm, CONF_HALO, d, s), _main_spec(tm, d2), _before_spec(tm, CONF_HALO, d2),
                  _row_spec(d, kw)],
        out_specs=[_main_spec(tm, d2), _row_spec(d, 32), _row_spec(d), _row_spec(d2)],
        out_shape=[jax.ShapeDtypeStruct((s, d2), BF16), jax.ShapeDtypeStruct((32, d), F32),
                   jax.ShapeDtypeStruct((1, d), F32), jax.ShapeDtypeStruct((1, d2), F32)],
        scratch_shapes=[pltpu.VMEM((8, tm + CONF_HALO, d), F32), pltpu.VMEM((8, tm + CONF_HALO, d), F32)],
        compiler_params=_params("arbitrary"),
    )(dhc, dhc, ag, ag, dw)


def _loss_head(h, gain, target, name):
    s, d = h.shape
    tm = _tile(s, 512)

    def body(h_ref, g_ref, t_ref, loss_ref, dh_ref, dg_ref):
        first = pl.program_id(0) == 0
        x = h_ref[...]
        err = x * _rms_stats(x) * g_ref[...] - t_ref[...]
        part = 0.5 * jnp.sum(jnp.mean(err * err, axis=-1, keepdims=True), axis=0, keepdims=True)
        dx, dg = _rms_bwd(err * (1.0 / d), x, g_ref[...])
        dh_ref[...] = dx
        _accumulate(loss_ref, part, first)
        _accumulate(dg_ref, dg, first)

    return _hosted(
        body, name=name, grid=(s // tm,),
        in_specs=[_main_spec(tm, d), _row_spec(d), _main_spec(tm, d)],
        out_specs=[pl.BlockSpec((1, 1), lambda m: (0, 0)), _main_spec(tm, d), _row_spec(d)],
        out_shape=[jax.ShapeDtypeStruct((1, 1), F32), jax.ShapeDtypeStruct((s, d), F32), jax.ShapeDtypeStruct((1, d), F32)],
        compiler_params=_params("arbitrary"),
    )(h, gain, target)


def _ffn_fwd(h, wts, i):
    u, act, s1, q1 = _ffn_up(h, wts[f"ln2_{i}"], wts[f"ffn{i}_w_gu"], f"ffn{i}_up")
    h_new = _mm_row(act, wts[f"ffn{i}_w_down"], h, None, f"ffn{i}_down")
    return h_new, (h, u, act, s1, q1)


def _ffn_bwd(dh, saved, wts, i, g):
    h, u, act, s1, q1 = saved
    dgu = _ffn_down_bwd(dh, wts[f"ffn{i}_w_down"], s1, q1, f"ffn{i}_down_bwd")
    g[f"ffn{i}_w_down"] = _mm_tn(act, dh, 1, f"ffn{i}_dw_down")
    g[f"ffn{i}_w_gu"] = _mm_tn(u, dgu, N_CHIPS, f"ffn{i}_dw_gu")
    dh_new, g[f"ln2_{i}"] = _mm_nt_col_rms_bwd(dgu, wts[f"ffn{i}_w_gu"], h, wts[f"ln2_{i}"], dh, f"ffn{i}_up_bwd")
    return dh_new


def _device_step(x, target, wts, g=None):
    g = {} if g is None else g
    saved = {}
    h = x

    def short_conv_fwd(h, i):
        u, bcv = _mm_col(h, wts[f"ln1_{i}"], wts[f"a{i}_w_in"], None, f"a{i}_in")
        p = _sconv_fwd(bcv, wts[f"a{i}_conv"], f"a{i}_conv")
        return _mm_row(p, wts[f"a{i}_w_out"], h, None, f"a{i}_out"), (h, u, bcv, p)

    def short_conv_bwd(dh, sv, i):
        h, u, bcv, p = sv
        dp = _mm_nt_row(dh, wts[f"a{i}_w_out"], f"a{i}_out_bwd")
        dbcv, dcw = _sconv_bwd(dp, bcv, wts[f"a{i}_conv"], f"a{i}_conv_bwd")
        g[f"a{i}_conv"] = dcw[:SHORT_CONV_W]
        g[f"a{i}_w_in"] = _mm_tn(u, dbcv, N_CHIPS, f"a{i}_dw_in")
        g[f"a{i}_w_out"] = _mm_tn(p, dh, 1, f"a{i}_dw_out")
        dh, g[f"ln1_{i}"] = _mm_nt_col_rms_bwd(dbcv, wts[f"a{i}_w_in"], h, wts[f"ln1_{i}"], dh, f"a{i}_in_bwd")
        return dh

    h, saved["a0"] = short_conv_fwd(h, 0)
    h, saved["f0"] = _ffn_fwd(h, wts, 0)

    h_in = h
    h, mixed = _pool_fwd(h, wts["ln1_1"], wts["b1_w_grp"], wts["b1_scale"], "b1_fwd")
    saved["b1"] = (h_in, mixed)
    h, saved["f1"] = _ffn_fwd(h, wts, 1)

    h_in = h
    u, ag = _mm_col(h, wts["ln1_2"], wts["c2_w_pw1"], wts["c2_b_pw1"], "c2_pw1")
    sw, hc = _conf_mid_fwd(ag, wts["c2_dw"], wts["c2_b_dw"], wts["c2_ln_g"], wts["c2_ln_b"], "c2_mid")
    h = _mm_row(sw, wts["c2_w_pw2"], h, wts["c2_b_pw2"], "c2_pw2")
    saved["c2"] = (h_in, u, ag, sw, hc)
    h, saved["f2"] = _ffn_fwd(h, wts, 2)

    h, saved["a3"] = short_conv_fwd(h, 3)
    h, saved["f3"] = _ffn_fwd(h, wts, 3)

    loss, dh, g["ln_f"] = _loss_head(h, wts["ln_f"], target, "loss_head")

    def ffn_bwd(dh, i):
        return _ffn_bwd(dh, saved[f"f{i}"], wts, i, g)

    dh = ffn_bwd(dh, 3)
    dh = short_conv_bwd(dh, saved["a3"], 3)

    dh = ffn_bwd(dh, 2)
    h_in, u, ag, sw, hc = saved["c2"]
    dhc, g["c2_ln_g"], g["c2_ln_b"], g["c2_b_pw2"] = _conf_out_bwd(
        dh, wts["c2_w_pw2"], hc, wts["c2_ln_g"], wts["c2_ln_b"], "c2_pw2_bwd")
    g["c2_w_pw2"] = _mm_tn(sw, dh, 1, "c2_dw_pw2")
    dag, ddw, g["c2_b_dw"], g["c2_b_pw1"] = _conf_mid_bwd(dhc, ag, wts["c2_dw"], "c2_mid_bwd")
    g["c2_dw"] = ddw[:CONF_CONV_W]
    g["c2_w_pw1"] = _mm_tn(u, dag, N_CHIPS, "c2_dw_pw1")
    dh, g["ln1_2"] = _mm_nt_col_rms_bwd(dag, wts["c2_w_pw1"], h_in, wts["ln1_2"], dh, "c2_pw1_bwd")

    dh = ffn_bwd(dh, 1)
    h_in, mixed = saved["b1"]
    dmixed, g["b1_w_grp"], g["b1_scale"] = _pool_bwd_mm(dh, mixed, wts["b1_w_grp"], wts["b1_scale"], "b1_bwd_mm")
    dh, g["ln1_1"] = _pool_bwd_rms(dmixed, h_in, wts["ln1_1"], dh, "b1_bwd_rms")

    dh = ffn_bwd(dh, 0)
    dh = short_conv_bwd(dh, saved["a0"], 0)
    return loss, dh, g


MESH = pl.DeviceIdType.MESH
ANY = pl.BlockSpec(memory_space=pl.ANY)


def _position():
    return lax.axis_index("x"), lax.axis_index("y"), lax.axis_index("c")


def _other_chips(x, y):
    return [(1 - x, y), (x, 1 - y), (1 - x, 1 - y)]


def _remote(src, dst, send_sem, recv_sem, to):
    return pltpu.make_async_remote_copy(src_ref=src, dst_ref=dst, send_sem=send_sem, recv_sem=recv_sem,
                                        device_id=to, device_id_type=MESH)


def _half_rows(ref_rows, c):
    hr = ref_rows // 2
    return pl.ds(pl.multiple_of(c * hr, 16), hr)


def _allgather8(v, name):
    m_per, n = v.shape

    def body(v_ref, out_ref, send_sems, recv_sems, local_sem):
        x, y, c = _position()
        me, sibling = (x, y, c), (x, y, 1 - c)
        chips = _other_chips(x, y)

        def rows(px, py, pc):
            return out_ref.at[pl.ds((4 * px + 2 * py + pc) * m_per, m_per), :]

        def copy(k, block, to, src=None):
            return _remote(rows(*block) if src is None else src, rows(*block), send_sems.at[k], recv_sems.at[k], to)

        mine = pltpu.make_async_copy(v_ref, rows(*me), local_sem)
        mine.start()
        first = [copy(0, me, sibling, src=v_ref)]
        first += [copy(1 + j, me, (*chip, c), src=v_ref) for j, chip in enumerate(chips)]
        for cp in first:
            cp.start()
        passed = [copy(4 + j, (*chip, c), sibling) for j, chip in enumerate(chips)]
        for j, chip in enumerate(chips):
            copy(1 + j, (*chip, c), me).wait_recv()
            passed[j].start()
        copy(0, sibling, me).wait_recv()
        for j, chip in enumerate(chips):
            copy(4 + j, (*chip, 1 - c), me).wait_recv()
        for cp in first + passed:
            cp.wait_send()
        mine.wait()

    return _hosted(
        body, name=name,
        out_shape=jax.ShapeDtypeStruct((N_DEV * m_per, n), v.dtype),
        in_specs=[pl.BlockSpec(memory_space=pltpu.VMEM)],
        out_specs=pl.BlockSpec(memory_space=pltpu.VMEM),
        scratch_shapes=[pltpu.SemaphoreType.DMA((7,)), pltpu.SemaphoreType.DMA((7,)), pltpu.SemaphoreType.DMA],
        compiler_params=pltpu.CompilerParams(vmem_limit_bytes=VMEM_LIMIT),
    )(v)


def _cast_to_slot(ws, idx, name):
    r, cols = ws[0].shape
    assert all(w.shape == (r, cols) for w in ws)
    n = len(ws)
    tr = _tile(r, 256, 16)

    def body(idx_ref, *refs):
        for w_ref, o_ref in zip(refs[:n], refs[n:]):
            o_ref[...] = w_ref[...].astype(o_ref.dtype)

    return _hosted(
        body, name=name,
        grid_spec=pltpu.PrefetchScalarGridSpec(
            num_scalar_prefetch=1, grid=(r // tr,),
            in_specs=[pl.BlockSpec((tr, cols), lambda t, idx_ref: (t, 0))] * n,
            out_specs=[pl.BlockSpec((None, tr, cols), lambda t, idx_ref: (idx_ref[0], t, 0))] * n),
        out_shape=[jax.ShapeDtypeStruct((N_CHIPS, r, cols), BF16)] * n,
        compiler_params=_params("parallel"),
    )(idx, *ws)


def _dma_sems(*shape):
    return [pltpu.SemaphoreType.DMA(shape), pltpu.SemaphoreType.DMA(shape)]


def _same_shapes(arrays):
    return [jax.ShapeDtypeStruct(a.shape, a.dtype) for a in arrays]


def _part_rows(ref_rows, c, part):
    hr = ref_rows // 2
    i, n = part
    size = hr // n
    assert size * n == hr and size % 16 == 0, (ref_rows, part)
    return pl.ds(pl.multiple_of(c * hr + i * size, 16), size)


def _task_gather_ici(bufs, done, part=(0, 1)):
    n = len(bufs)

    def copies(outs, sems, landing):
        x, y, c = _position()
        my_chip = 2 * x + y
        res = []
        for i in range(n):
            rows = _part_rows(bufs[i].shape[1], c, part)
            for r, (px, py) in enumerate(_other_chips(x, y)):
                slot = (2 * px + py) if landing else my_chip
                res.append(_remote(outs[i].at[my_chip, rows, :], outs[i].at[slot, rows, :], sems[0].at[i, r], sems[1].at[i, r],
                                   (px, py, c)))
        return res

    def start(ins, outs, sems):
        for cp in copies(outs, sems, False):
            cp.start()

    def wait(ins, outs, sems):
        for cp in copies(outs, sems, True):
            cp.wait_recv()
            cp.wait_send()

    return _Task(bufs, _same_shapes(bufs), {i: i for i in range(n)}, _dma_sems(n, 3), start, wait, done)


def _task_gather_d2d(bufs, done):
    n = len(bufs)

    def copies(outs, sems, landing):
        x, y, c = _position()
        res = []
        for i in range(n):
            rows = _half_rows(bufs[i].shape[1], (1 - c) if landing else c)
            for r, (px, py) in enumerate(_other_chips(x, y)):
                part = outs[i].at[2 * px + py, rows, :]
                res.append(_remote(part, part, sems[0].at[i, r], sems[1].at[i, r], (x, y, 1 - c)))
        return res

    def start(ins, outs, sems):
        for cp in copies(outs, sems, False):
            cp.start()

    def wait(ins, outs, sems):
        for cp in copies(outs, sems, True):
            cp.wait_recv()
        for cp in copies(outs, sems, False):
            cp.wait_send()

    return _Task(bufs, _same_shapes(bufs), {i: i for i in range(n)}, _dma_sems(n, 3), start, wait, done)


def _task_sibling_halves(grads, done):
    n = len(grads)

    def copies(ins, outs, sems):
        x, y, c = _position()
        return [_remote(ins[i].at[:, _half_rows(grads[i].shape[1], 1 - c), :], outs[i], sems[0].at[i], sems[1].at[i],
                        (x, y, 1 - c)) for i in range(n)]

    def start(ins, outs, sems):
        for cp in copies(ins, outs, sems):
            cp.start()

    def wait(ins, outs, sems):
        for cp in copies(ins, outs, sems):
            cp.wait()

    shapes = [jax.ShapeDtypeStruct((g.shape[0], g.shape[1] // 2, g.shape[2]), g.dtype) for g in grads]
    return _Task(grads, shapes, {}, _dma_sems(n), start, wait, done)


def _task_chip_sums(parts, done, landed=None, part=(0, 1)):
    n = len(parts)
    i_part, n_parts = part
    sizes = [p.shape[1] // n_parts for p in parts]
    assert all(p.shape[1] == size * n_parts and size % 16 == 0 for p, size in zip(parts, sizes)), part
    rows = [pl.ds(i_part * size, size) for size in sizes]

    def copies(ins, outs, sems):
        x, y, c = _position()
        return [_remote(ins[i].at[2 * px + py, rows[i], :], outs[i].at[r, rows[i], :], sems[0].at[i, r], sems[1].at[i, r],
                        (px, py, c))
                for i in range(n) for r, (px, py) in enumerate(_other_chips(x, y))]

    def start(ins, outs, sems):
        for cp in copies(ins, outs, sems):
            cp.start()

    def wait(ins, outs, sems):
        for cp in copies(ins, outs, sems):
            cp.wait()

    shapes = [jax.ShapeDtypeStruct((3,) + p.shape[1:], p.dtype) for p in parts]
    if landed is None:
        return _Task(parts, shapes, {}, _dma_sems(n, 3), start, wait, done)
    return _Task(list(parts) + list(landed), shapes, {n + i: i for i in range(n)}, _dma_sems(n, 3), start, wait, done)


def _task_sibling_parts(owns, landeds, done):
    n = len(owns)

    def copies(ins, outs, sems):
        x, y, c = _position()
        sibling = (x, y, 1 - c)
        res = []
        for i in range(n):
            res.append(_remote(ins[i].at[2 * x + y], outs[i].at[0], sems[0].at[i, 0], sems[1].at[i, 0], sibling))
            res.append(_remote(ins[n + i], outs[i].at[pl.ds(1, 3)], sems[0].at[i, 1], sems[1].at[i, 1], sibling))
        return res

    def start(ins, outs, sems):
        for cp in copies(ins, outs, sems):
            cp.start()

    def wait(ins, outs, sems):
        for cp in copies(ins, outs, sems):
            cp.wait()

    return _Task(list(owns) + list(landeds), _same_shapes(owns), {}, _dma_sems(n, 2), start, wait, done)


def _add_halves(grads, sibs, idx, name):
    n = len(grads)
    nsh = grads[0].shape[0]

    def body(idx_ref, *refs):
        for g_ref, s_ref, o_ref in zip(refs[:n], refs[n:2 * n], refs[2 * n:]):
            o_ref[...] = (g_ref[...].astype(F32) + s_ref[...].astype(F32)).astype(o_ref.dtype)

    half_of = [pl.BlockSpec((None, s.shape[1], s.shape[2]), lambda j, idx_ref: (j, idx_ref[1], 0)) for s in sibs]
    whole = [pl.BlockSpec((None, s.shape[1], s.shape[2]), lambda j, idx_ref: (j, 0, 0)) for s in sibs]
    return _hosted(
        body, name=name,
        grid_spec=pltpu.PrefetchScalarGridSpec(num_scalar_prefetch=1, grid=(nsh,), in_specs=half_of + whole, out_specs=whole),
        out_shape=_same_shapes(sibs),
        compiler_params=_params("parallel"),
    )(idx, *grads, *sibs)


def _adamw_reduced(w, own, landed, sib, m, v, idx, name):
    r, cols = w.shape
    hr = r // 2
    tr = _tile(hr, 256, 16)
    nt = hr // tr

    def body(idx_ref, w_ref, p_ref, l_ref, s_ref, m_ref, v_ref, go_ref, d_ref, mo_ref, vo_ref):
        mine = p_ref[...].astype(F32)
        for k in range(3):
            mine = mine + l_ref[k].astype(F32)
        theirs = s_ref[0].astype(F32)
        for k in range(1, 4):
            theirs = theirs + s_ref[k].astype(F32)
        grad = jnp.where(pl.program_id(0) // nt == idx_ref[1], mine, theirs)
        go_ref[...] = grad
        d_ref[...], mo_ref[...], vo_ref[...] = _adamw_update(w_ref[...], grad, m_ref[...], v_ref[...])

    def in_half(t, half):
        return jnp.clip(t - half * nt, 0, nt - 1)

    full = pl.BlockSpec((tr, cols), lambda t, idx_ref: (t, 0))
    return _hosted(
        body, name=name,
        grid_spec=pltpu.PrefetchScalarGridSpec(
            num_scalar_prefetch=1, grid=(2 * nt,),
            in_specs=[full,
                      pl.BlockSpec((None, tr, cols), lambda t, idx_ref: (idx_ref[0], in_half(t, idx_ref[1]), 0)),
                      pl.BlockSpec((3, tr, cols), lambda t, idx_ref: (0, in_half(t, idx_ref[1]), 0)),
                      pl.BlockSpec((4, tr, cols), lambda t, idx_ref: (0, in_half(t, 1 - idx_ref[1]), 0)),
                      full, full],
            out_specs=[full] * 4),
        out_shape=[jax.ShapeDtypeStruct((r, cols), F32)] * 4,
        compiler_params=_params("arbitrary"),
    )(idx, w, own, landed, sib, m, v)


def _sum_devices(blocks, name):
    m8, n = blocks.shape
    m = m8 // N_DEV

    def body(b_ref, o_ref):
        acc = b_ref[pl.ds(0, m), :]
        for k in range(1, N_DEV):
            acc = acc + b_ref[pl.ds(k * m, m), :]
        o_ref[...] = acc

    return _hosted(
        body, name=name, out_shape=jax.ShapeDtypeStruct((m, n), F32),
        in_specs=[pl.BlockSpec(memory_space=pltpu.VMEM)], out_specs=pl.BlockSpec(memory_space=pltpu.VMEM),
        compiler_params=pltpu.CompilerParams(vmem_limit_bytes=VMEM_LIMIT),
    )(blocks)


def _adamw_update(w, grad, m, v):
    new_m = ADAM_B1 * m + (1.0 - ADAM_B1) * grad
    new_v = ADAM_B2 * v + (1.0 - ADAM_B2) * (grad * grad)
    m_hat = new_m * (1.0 / (1.0 - ADAM_B1 ** ADAM_STEP))
    v_hat = new_v * (1.0 / (1.0 - ADAM_B2 ** ADAM_STEP))
    return -ADAM_LR * (m_hat / (jnp.sqrt(v_hat) + ADAM_EPS) + ADAM_WD * w), new_m, new_v


def _adamw(w, g, m, v, name):
    r, cols = w.shape
    tr = _tile(r, 256)

    def body(w_ref, g_ref, m_ref, v_ref, go_ref, d_ref, mo_ref, vo_ref):
        grad = g_ref[...]
        go_ref[...] = grad
        d_ref[...], mo_ref[...], vo_ref[...] = _adamw_update(w_ref[...], grad, m_ref[...], v_ref[...])

    spec = pl.BlockSpec((tr, cols), lambda t: (t, 0))
    return _hosted(
        body, name=name, grid=(r // tr,), in_specs=[spec] * 4, out_specs=[spec] * 4,
        out_shape=[jax.ShapeDtypeStruct((r, cols), F32)] * 4,
        compiler_params=_params("parallel"),
    )(w, g, m, v)


def _adamw_small(grad_blocks, params, name):
    nb, npar = len(grad_blocks), len(params)

    def body(*refs):
        blocks, ins, outs = refs[:nb], refs[nb:nb + 3 * npar], refs[nb + 3 * npar:]
        for p, (w, _, _, blk, row0) in enumerate(params):
            if w.ndim == 1:
                tiled = (w.shape[0] // LANES, LANES)
                grad = blocks[blk][pl.ds(row0, tiled[0]), pl.ds(0, LANES)]
                wmv = [ins[3 * p + k][...].reshape(tiled) for k in range(3)]
            else:
                grad = blocks[blk][pl.ds(row0, w.shape[0]), :]
                wmv = [ins[3 * p + k][...] for k in range(3)]
            for k, res in enumerate((grad,) + _adamw_update(wmv[0], grad, wmv[1], wmv[2])):
                outs[4 * p + k][...] = res.reshape(w.shape)

    args = list(grad_blocks) + [a for w, m, v, _, _ in params for a in (w, m, v)]
    vmem = pl.BlockSpec(memory_space=pltpu.VMEM)
    out = _hosted(
        body, name=name, in_specs=[vmem] * len(args), out_specs=[vmem] * (4 * npar),
        out_shape=[jax.ShapeDtypeStruct(w.shape, F32) for w, _, _, _, _ in params for _ in range(4)],
    )(*args)
    return [tuple(out[4 * p:4 * p + 4]) for p in range(npar)]


WEIGHT_NAMES = (
    "ln1_0", "a0_w_in", "a0_conv", "a0_w_out", "ln2_0", "ffn0_w_gu", "ffn0_w_down",
    "ln1_1", "b1_w_grp", "b1_scale", "ln2_1", "ffn1_w_gu", "ffn1_w_down",
    "ln1_2", "c2_w_pw1", "c2_b_pw1", "c2_dw", "c2_b_dw", "c2_ln_g", "c2_ln_b", "c2_w_pw2", "c2_b_pw2",
    "ln2_2", "ffn2_w_gu", "ffn2_w_down",
    "ln1_3", "a3_w_in", "a3_conv", "a3_w_out", "ln2_3", "ffn3_w_gu", "ffn3_w_down", "ln_f")
BIG = ("a0_w_in", "a0_w_out", "ffn0_w_gu", "ffn0_w_down", "b1_w_grp", "ffn1_w_gu", "ffn1_w_down", "c2_w_pw1", "c2_w_pw2",
       "ffn2_w_gu", "ffn2_w_down", "a3_w_in", "a3_w_out", "ffn3_w_gu", "ffn3_w_down")
GROUPED = "b1_w_grp"
SMALL_SHARDED = ("a0_conv", "a3_conv", "c2_dw")
REPLICATED = tuple(n for n in WEIGHT_NAMES if n not in BIG and n not in SMALL_SHARDED)


def _pad_rows(a, mult=8):
    pad = -a.shape[0] % mult
    return a if pad == 0 else jnp.concatenate([a, jnp.zeros((pad, a.shape[1]), a.dtype)], axis=0)


def _pack_rows(parts, width):
    rows = [p.reshape(-1, width) for p in parts]
    return _pad_rows(jnp.concatenate(rows, axis=0)), [r.shape[0] for r in rows]


def _unpack_rows(packed, counts, shapes):
    out, at = [], 0
    for n, shp in zip(counts, shapes):
        out.append(packed[at:at + n].reshape(shp))
        at += n
    return out


COLUMN_SHARDED = ("w_in", "w_gu", "w_pw1")


class _Weights(dict):
    def __init__(self, bufs):
        super().__init__()
        self.bufs = bufs

    def __missing__(self, name):
        buf = self.bufs[name]
        if name == GROUPED:
            cg = buf.shape[-1]
            rq = cg // N_CHIPS
            return jnp.transpose(buf.reshape(N_CHIPS, -1, rq, cg), (1, 0, 2, 3)).reshape(-1, cg, cg)
        return buf if name.endswith(COLUMN_SHARDED) else buf.reshape(-1, buf.shape[-1])


class _Exchange:
    def __init__(self, w, mom, vel, idx):
        def shards(table):
            return {n: table[n].reshape(-1, table[n].shape[-1]) for n in BIG}

        self.w, self.mom, self.vel, self.idx = shards(w), shards(mom), shards(vel), idx
        self.bufs = {}
        self.weights = _Weights(self.bufs)
        self.grads = {}
        self.sib, self.part, self.landed, self.sib_parts, self.updates = {}, {}, {}, {}, {}

    def cast(self, names):
        by_shape = {}
        for n in names:
            by_shape.setdefault(self.w[n].shape, []).append(n)
        for group in by_shape.values():
            self.bufs.update(zip(group, _cast_to_slot([self.w[n] for n in group], self.idx, f"cast_{group[0]}")))

    @staticmethod
    def _store(table, names):
        def done(arrays):
            table.update(zip(names, arrays))
        return done

    def _grad(self, n):
        g = self.grads[n]
        if n == GROUPED:
            ng, cg, _ = g.shape
            g = jnp.transpose(g.reshape(ng, N_CHIPS, cg // N_CHIPS, cg), (1, 0, 2, 3)).astype(BF16)
        return g.reshape(N_CHIPS, -1, g.shape[-1])

    def gather_ici(self, *names, part=(0, 1)):
        return lambda: _task_gather_ici([self.bufs[n] for n in names], self._store(self.bufs, names), part)

    def gather_d2d(self, *names):
        return lambda: _task_gather_d2d([self.bufs[n] for n in names], self._store(self.bufs, names))

    def sibling_halves(self, *names):
        return lambda: _task_sibling_halves([self._grad(n) for n in names], self._store(self.sib, names))

    def add_halves(self, *names):
        def run():
            parts = _add_halves([self._grad(n) for n in names], [self.sib.pop(n) for n in names], self.idx,
                                f"reduce_add_{names[0]}")
            self.part.update(zip(names, parts))
        return run

    def chip_sums(self, *names, part=(0, 1)):
        def make():
            landed = [self.landed[n] for n in names] if part[0] > 0 else None
            return _task_chip_sums([self.part[n] for n in names], self._store(self.landed, names), landed, part)
        return make

    def sibling_parts(self, *names):
        return lambda: _task_sibling_parts([self.part[n] for n in names], [self.landed[n] for n in names],
                                           self._store(self.sib_parts, names))

    def adamw(self, *names):
        def run():
            for n in names:
                self.updates[n] = _adamw_reduced(self.w[n], self.part.pop(n), self.landed.pop(n), self.sib_parts.pop(n),
                                                 self.mom[n], self.vel[n], self.idx, f"adamw_{n}")
        return run


def _plan(ex):
    s = _Schedule()

    def ffn(i):
        return f"ffn{i}_w_gu", f"ffn{i}_w_down"

    c2, a3 = ("c2_w_pw1", "c2_w_pw2"), ("a3_w_in", "a3_w_out")
    first, second = (0, 2), (1, 2)
    s.host("cast_ffn0_w_gu", ex.gather_ici("a0_w_in", part=first))
    s.host("cast_ffn0_w_down", ex.gather_ici("a0_w_in", part=second))
    s.host("cast_c2_w_pw1", ex.gather_d2d("a0_w_in"))
    gu, down = ffn(0)
    s.host("gather_small", ex.gather_ici("a0_w_out"))
    s.host("a0_in", ex.gather_ici(gu, part=first), ex.gather_d2d("a0_w_out"))
    s.host("a0_conv", ex.gather_ici(gu, part=second))
    s.host("a0_out", ex.gather_ici(down), ex.gather_d2d(gu))
    s.host("ffn0_up", ex.gather_d2d(down))
    gu, down = ffn(1)
    s.host("ffn0_up", ex.gather_ici(gu, GROUPED))
    s.host("ffn0_down", ex.gather_ici(down), ex.gather_d2d(gu, GROUPED))
    s.host("ffn1_up", ex.gather_d2d(down), ex.gather_ici(*c2))
    gu, down = ffn(2)
    s.host("ffn1_up", ex.gather_ici(gu, part=first))
    s.host("ffn1_down", ex.gather_d2d(*c2), ex.gather_ici(down))
    s.host("c2_mid", ex.gather_ici(gu, part=second))
    s.host("c2_pw2", ex.gather_d2d(gu, down), ex.gather_ici(a3[1]))
    s.host("ffn2_up", ex.gather_ici(a3[0]))
    gu, down = ffn(3)
    s.host("ffn2_up", ex.gather_ici(gu, part=first))
    s.host("ffn2_down", ex.gather_d2d(*a3), ex.gather_ici(down))
    s.host("a3_in", ex.gather_ici(gu, part=second))
    s.host("a3_out", ex.gather_d2d(gu, down))

    def reduce_on(names, first_host, ici_hosts, last_host):
        s.host(first_host, ex.sibling_halves(*names))
        s.post(first_host, ex.add_halves(*names))
        for host, hosted, part in ici_hosts:
            s.host(host, ex.chip_sums(*hosted, part=part))
        s.host(last_host, ex.sibling_parts(*names))
        s.post(last_host, ex.adamw(*names))

    whole = (0, 1)
    gu, down = ffn(3)
    reduce_on((gu, down), "a3_out_bwd",
              [("a3_conv_bwd", (down,), whole), ("a3_dw_in", (gu,), first), ("a3_in_bwd", (gu,), second)], "ffn2_down_bwd")
    reduce_on(a3, "ffn2_down_bwd", [("ffn2_dw_gu", a3, whole)], "c2_pw2_bwd")
    gu, down = ffn(0)
    s.host("ffn0_dw_gu", ex.sibling_halves(down))
    s.post("ffn0_dw_gu", ex.add_halves(down))
    s.host("ffn0_up_bwd", ex.chip_sums(down))
    s.host("a0_out_bwd", ex.sibling_halves(gu, GROUPED))
    s.post("a0_out_bwd", ex.add_halves(gu, GROUPED))
    s.host("a0_conv_bwd", ex.chip_sums(gu, part=first), ex.chip_sums(GROUPED))
    s.host("a0_dw_in", ex.chip_sums(gu, part=second))
    s.host("a0_dw_out", ex.sibling_halves("a0_w_in"))
    s.post("a0_dw_out", ex.add_halves("a0_w_in"))
    s.host("a0_in_bwd", ex.chip_sums("a0_w_in"), ex.sibling_parts(gu, down, GROUPED))
    s.host(f"adamw_{gu}", ex.chip_sums("a0_w_out"))
    s.host(f"adamw_{down}", ex.sibling_parts("a0_w_out"))
    reduce_on(ffn(2), "c2_pw2_bwd", [("c2_mid_bwd", ffn(2), whole)], "ffn1_down_bwd")
    reduce_on(c2, "ffn1_down_bwd", [("ffn1_dw_down", c2, whole)], "b1_bwd_mm")
    gu, down = ffn(1)
    reduce_on((gu, down), "b1_bwd_mm", [("ffn0_down_bwd", (down,), whole), ("ffn0_dw_gu", (gu,), whole)], "ffn0_up_bwd")
    return s


def kernel(x, *rest):
    nw = len(WEIGHT_NAMES)
    w = dict(zip(WEIGHT_NAMES, rest[:nw]))
    target = rest[nw]
    mom = dict(zip(WEIGHT_NAMES, rest[nw + 1:2 * nw + 1]))
    vel = dict(zip(WEIGHT_NAMES, rest[2 * nw + 1:3 * nw + 1]))
    cx, cy, cc = _position()
    my_chip = 2 * cx + cy
    ex = _Exchange(w, mom, vel, jnp.stack([my_chip, cc]).astype(jnp.int32))
    _ACTIVE_SCHEDULE[0] = _plan(ex)
    try:
        return _scheduled_step(x, target, w, mom, vel, ex, my_chip)
    finally:
        _ACTIVE_SCHEDULE[0] = None


def _scheduled_step(x, target, w, mom, vel, ex, my_chip):
    d = x.shape[-1]
    cq = d // N_CHIPS
    ex.cast(BIG)

    small_blk, small_counts = _pack_rows([w[n] for n in SMALL_SHARDED], cq)
    small_all = _allgather8(small_blk, "gather_small").reshape(N_CHIPS, 2, small_blk.shape[0], cq)[:, 0]
    small_parts = _unpack_rows(jnp.transpose(small_all, (1, 0, 2)), small_counts,
                               [(w[n].reshape(-1, cq).shape[0], N_CHIPS, cq) for n in SMALL_SHARDED])
    wts = ex.weights
    for n in REPLICATED:
        wts[n] = w[n].reshape(1, -1)
    for n, part in zip(SMALL_SHARDED, small_parts):
        wts[n] = part.reshape(part.shape[0], d)

    loss, dx, g = _device_step(x[0], target[0], wts, ex.grads)

    summed, last = ("ffn0_w_gu", "ffn0_w_down", GROUPED, "a0_w_in"), "a0_w_out"
    _comm_only([ex.sibling_parts("a0_w_in")(), ex.sibling_halves(last)()], "reduce_tail_d2d")
    ex.add_halves(last)()
    ex.adamw(*summed)()
    ex.adamw(last)()
    sched = _ACTIVE_SCHEDULE[0]
    assert not sched.hosts and not sched.posts, (sched.hosts, sched.posts)

    rep_rows = [jnp.pad(g[n].reshape(-1, LANES), ((0, 0), (0, cq - LANES))) for n in REPLICATED]
    by_chip = [jnp.transpose(g[n].reshape(g[n].shape[0], N_CHIPS, cq), (1, 0, 2)) for n in SMALL_SHARDED]
    shard_rows = jnp.concatenate(by_chip, axis=1)
    n_rep, n_shard = sum(r.shape[0] for r in rep_rows), shard_rows.shape[1]
    loss_row = jnp.broadcast_to(loss, (1, cq))
    sm_blk = _pad_rows(jnp.concatenate(rep_rows + [loss_row, shard_rows.reshape(N_CHIPS * n_shard, cq)], axis=0))
    sm_sum = _sum_devices(_allgather8(sm_blk, "gather_small_grads"), "sum_small_grads")
    mine = lax.dynamic_slice_in_dim(sm_sum, n_rep + 1 + my_chip * n_shard, n_shard, axis=0)

    out = ex.updates
    params, at = [], {0: 0, 1: 0}
    for block, names in ((0, REPLICATED), (1, SMALL_SHARDED)):
        for n in names:
            params.append((w[n], mom[n], vel[n], block, at[block]))
            at[block] += w[n].size // LANES if w[n].ndim == 1 else w[n].shape[0]
    out.update(zip(REPLICATED + SMALL_SHARDED, _adamw_small([sm_sum, mine], params, "adamw_small")))

    total = sm_sum[n_rep, 0]
    grads, deltas, new_m, new_v = ([out[n][k].reshape(w[n].shape) for n in WEIGHT_NAMES] for k in range(4))
    return (total, dx.reshape(x.shape), *grads, *deltas, *new_m, *new_v)
```

```python
import functools

import jax
import jax.numpy as jnp
from jax import lax
from jax.experimental import pallas as pl
from jax.experimental.pallas import tpu as pltpu

F32 = jnp.float32
BF16 = jnp.bfloat16

RMS_EPS = 1e-6
LN_EPS = 1e-5
POOL_WINDOWS = (2, 4, 8, 16)
SHORT_CONV_W = 3
CONF_CONV_W = 31
N_CHIPS = 4
N_DEV = 8

ADAM_LR = 0.001
ADAM_B1 = 0.9
ADAM_B2 = 0.999
ADAM_EPS = 1e-08
ADAM_WD = 0.01
ADAM_STEP = 10

V7X_VMEM_BYTES = 64 * 1024 * 1024
VMEM_LIMIT = V7X_VMEM_BYTES - 8 * 1024 * 1024
LANES = 128
POOL_HALO = 16
SCONV_HALO = 16
CONF_HALO = 32


def _params(*sem):
    return pltpu.CompilerParams(dimension_semantics=sem, vmem_limit_bytes=VMEM_LIMIT)


def _tile(n, pref, mult=8):
    t = min(n, pref)
    while t > mult and (n % t or t % mult):
        t -= mult
    assert n % t == 0 and t % mult == 0, (n, pref, mult)
    return t


def _sigmoid(x):
    return jax.nn.sigmoid(x)


def _dot(a, b):
    return jnp.dot(a, b, preferred_element_type=F32)


def _dot_nt(a, b):
    return lax.dot_general(a, b, (((1,), (1,)), ((), ())), preferred_element_type=F32)


def _dot_tn(a, b):
    return lax.dot_general(a, b, (((0,), (0,)), ((), ())), preferred_element_type=F32)


def _colsum(x):
    return jnp.sum(x, axis=0, keepdims=True)


def _rms_stats(x):
    return lax.rsqrt(jnp.mean(x * x, axis=-1, keepdims=True) + RMS_EPS)


def _rms_bwd(du, x, gain):
    r = _rms_stats(x)
    xhat = x * r
    gdy = du * gain
    dx = r * (gdy - xhat * jnp.mean(gdy * xhat, axis=-1, keepdims=True))
    return dx, _colsum(du * xhat)


class _Task:
    def __init__(self, ins, out_shapes, aliases, sems, start, wait, done):
        self.ins, self.out_shapes, self.aliases, self.sems = list(ins), list(out_shapes), dict(aliases), list(sems)
        self.start, self.wait, self.done = start, wait, done


class _Schedule:
    def __init__(self):
        self.hosts, self.posts = {}, {}

    def host(self, kernel_name, *make_tasks):
        self.hosts.setdefault(kernel_name, []).extend(make_tasks)

    def post(self, kernel_name, *thunks):
        self.posts.setdefault(kernel_name, []).extend(thunks)

    def tasks_for(self, kernel_name):
        return [make() for make in self.hosts.pop(kernel_name, ())]

    def finished(self, kernel_name):
        for thunk in self.posts.pop(kernel_name, ()):
            thunk()


_ACTIVE_SCHEDULE = [None]


def _hosted(body, name, **kw):
    def run(*args):
        sched = _ACTIVE_SCHEDULE[0]
        tasks = sched.tasks_for(name) if sched is not None else []
        out = _call_with_tasks(body, name, tasks, kw, args) if tasks else pl.pallas_call(body, name=name, **kw)(*args)
        if sched is not None:
            sched.finished(name)
        return out

    return run


def _call_with_tasks(body, name, tasks, kw, args):
    spec = kw.get("grid_spec")
    n_pre = spec.num_scalar_prefetch if spec is not None else 0
    src = dict(grid=spec.grid, in_specs=spec.in_specs, out_specs=spec.out_specs) if spec is not None else kw
    pre, args = args[:n_pre], args[n_pre:]
    grid = tuple(src.get("grid", ()))
    single = not isinstance(kw["out_shape"], (list, tuple))
    out_shape = [kw["out_shape"]] if single else list(kw["out_shape"])
    out_specs = [src["out_specs"]] if single else list(src["out_specs"])
    scratch = list(kw.get("scratch_shapes", ()))
    n_in, n_out, n_scr = len(args), len(out_shape), len(scratch)
    t_in = [a for t in tasks for a in t.ins]
    t_out = [o for t in tasks for o in t.out_shapes]
    t_sem = [s for t in tasks for s in t.sems]
    aliases, at_in, at_out = {}, n_pre + n_in, n_out
    for t in tasks:
        for i, o in t.aliases.items():
            aliases[at_in + i] = at_out + o
        at_in += len(t.ins)
        at_out += len(t.out_shapes)

    def wrapped(*refs):
        pre_refs, refs = refs[:n_pre], refs[n_pre:]
        a = n_in
        b = a + len(t_in)
        c = b + n_out
        d = c + len(t_out)
        e = d + n_scr
        ins, tins, outs, touts, scr, tsems = refs[:a], refs[a:b], refs[b:c], refs[c:d], refs[d:e], refs[e:]
        views, i0, o0, s0 = [], 0, 0, 0
        for t in tasks:
            views.append((tins[i0:i0 + len(t.ins)], touts[o0:o0 + len(t.out_shapes)], tsems[s0:s0 + len(t.sems)]))
            i0, o0, s0 = i0 + len(t.ins), o0 + len(t.out_shapes), s0 + len(t.sems)

        def start_all():
            for t, v in zip(tasks, views):
                t.start(*v)

        def wait_all():
            for t, v in zip(tasks, views):
                t.wait(*v)

        if grid:
            first = functools.reduce(jnp.logical_and, [pl.program_id(i) == 0 for i in range(len(grid))])
            last = functools.reduce(jnp.logical_and, [pl.program_id(i) == grid[i] - 1 for i in range(len(grid))])
            pl.when(first)(start_all)
            body(*pre_refs, *ins, *outs, *scr)
            pl.when(last)(wait_all)
        else:
            start_all()
            body(*pre_refs, *ins, *outs, *scr)
            wait_all()

    in_specs = list(src["in_specs"]) + [ANY] * len(t_in)
    out_specs = out_specs + [ANY] * len(t_out)
    if spec is not None:
        layout = dict(grid_spec=pltpu.PrefetchScalarGridSpec(
            num_scalar_prefetch=n_pre, grid=grid, in_specs=in_specs, out_specs=out_specs, scratch_shapes=scratch + t_sem))
    else:
        layout = dict(grid=grid, in_specs=in_specs, out_specs=out_specs, scratch_shapes=scratch + t_sem)
    res = pl.pallas_call(
        wrapped, name=name, out_shape=out_shape + t_out, input_output_aliases=aliases,
        compiler_params=pltpu.CompilerParams(dimension_semantics=("arbitrary",) * len(grid), vmem_limit_bytes=VMEM_LIMIT),
        **layout,
    )(*pre, *args, *t_in)
    res = list(res)
    own, rest = res[:n_out], res[n_out:]
    for t in tasks:
        t.done(rest[:len(t.out_shapes)])
        rest = rest[len(t.out_shapes):]
    return own[0] if single else own


def _comm_only(tasks, name):
    _call_with_tasks(lambda: None, name, tasks, dict(grid=(), in_specs=[], out_specs=[], out_shape=[]), ())


def _mm_col(h, gain, w, bias, name):
    s, k = h.shape
    nsh, _, ns = w.shape
    tm = _tile(s, 512)
    has_bias = bias is not None

    def body(h_ref, gain_ref, w_ref, *rest):
        u_ref, o_ref = rest[-2:]
        x = h_ref[...]
        x = (x * _rms_stats(x) * gain_ref[...]).astype(BF16)
        u_ref[...] = x
        for j in range(nsh):
            cols = pl.ds(j * ns, ns)
            acc = _dot(x, w_ref[j])
            if has_bias:
                acc = acc + rest[0][:, cols]
            o_ref[:, cols] = acc.astype(o_ref.dtype)

    tokens = pl.BlockSpec((tm, k), lambda m: (m, 0))
    in_specs = [tokens, pl.BlockSpec((1, k), lambda m: (0, 0)), pl.BlockSpec((nsh, k, ns), lambda m: (0, 0, 0))]
    args = [h, gain, w]
    if has_bias:
        in_specs.append(pl.BlockSpec((1, nsh * ns), lambda m: (0, 0)))
        args.append(bias)
    return _hosted(
        body, name=name, grid=(s // tm,), in_specs=in_specs,
        out_specs=[tokens, pl.BlockSpec((tm, nsh * ns), lambda m: (m, 0))],
        out_shape=[jax.ShapeDtypeStruct((s, k), BF16), jax.ShapeDtypeStruct((s, nsh * ns), BF16)],
        compiler_params=_params("parallel"),
    )(*args)


def _mm_row(a, w, res, bias, name):
    s = a.shape[0]
    k, n = w.shape
    tm = _tile(s, 1024)
    has_bias = bias is not None

    def body(a_ref, w_ref, res_ref, *rest):
        o_ref = rest[-1]
        y = res_ref[...] + _dot(a_ref[...], w_ref[...])
        if has_bias:
            y = y + rest[0][...]
        o_ref[...] = y

    in_specs = [pl.BlockSpec((tm, k), lambda m: (m, 0)), pl.BlockSpec((k, n), lambda m: (0, 0)),
                pl.BlockSpec((tm, n), lambda m: (m, 0))]
    args = [a, w, res]
    if has_bias:
        in_specs.append(pl.BlockSpec((1, n), lambda m: (0, 0)))
        args.append(bias)
    return _hosted(
        body, name=name, grid=(s // tm,), in_specs=in_specs,
        out_specs=pl.BlockSpec((tm, n), lambda m: (m, 0)),
        out_shape=jax.ShapeDtypeStruct((s, n), F32),
        compiler_params=_params("parallel"),
    )(*args)


def _mm_nt_row(dy, w, name):
    s, n = dy.shape
    k = w.shape[0]
    tm = _tile(s, 512)

    def body(dy_ref, w_ref, o_ref):
        o_ref[...] = _dot_nt(dy_ref[...].astype(BF16), w_ref[...])

    return _hosted(
        body, name=name, grid=(s // tm,),
        in_specs=[pl.BlockSpec((tm, n), lambda m: (m, 0)), pl.BlockSpec((k, n), lambda m: (0, 0))],
        out_specs=pl.BlockSpec((tm, k), lambda m: (m, 0)),
        out_shape=jax.ShapeDtypeStruct((s, k), F32),
        compiler_params=_params("parallel"),
    )(dy, w)


def _ffn_up(h, gain, w, name):
    s, d = h.shape
    _, _, ns = w.shape
    tm = _tile(s, 512)

    def body(h_ref, gain_ref, wg_ref, wu_ref, u_ref, act_ref, s1_ref, q1_ref):
        x = h_ref[...]
        x = (x * _rms_stats(x) * gain_ref[...]).astype(BF16)

        @pl.when(pl.program_id(0) == 0)
        def _():
            u_ref[...] = x

        g = _dot(x, wg_ref[...])
        up = _dot(x, wu_ref[...])
        sg = _sigmoid(g)
        s1 = g * sg
        act_ref[...] = (s1 * up).astype(act_ref.dtype)
        s1_ref[...] = s1.astype(s1_ref.dtype)
        q1_ref[...] = (up * sg * (1.0 + g * (1.0 - sg))).astype(q1_ref.dtype)

    out = pl.BlockSpec((tm, ns), lambda j, m: (m, j))
    tokens = pl.BlockSpec((tm, d), lambda j, m: (m, 0))
    nm = s // tm
    u_once = pl.BlockSpec((tm, d), lambda j, m: (jnp.where(j == 0, m, nm - 1), 0))
    return _hosted(
        body, name=name, grid=(2, nm),
        in_specs=[tokens, pl.BlockSpec((1, d), lambda j, m: (0, 0)), pl.BlockSpec((None, d, ns), lambda j, m: (j, 0, 0)),
                  pl.BlockSpec((None, d, ns), lambda j, m: (j + 2, 0, 0))],
        out_specs=[u_once, out, out, out],
        out_shape=[jax.ShapeDtypeStruct((s, d), BF16)] + [jax.ShapeDtypeStruct((s, 2 * ns), BF16)] * 3,
        compiler_params=_params("arbitrary", "arbitrary"),
    )(h, gain, w, w)


def _ffn_down_bwd(dh, w, s1, q1, name):
    s, d = dh.shape
    f = w.shape[0]
    tm = _tile(s, 512)

    def body(dh_ref, w_ref, s1_ref, q1_ref, o_ref):
        da = _dot_nt(dh_ref[...].astype(BF16), w_ref[...])
        o_ref[:, :f] = (da * q1_ref[...].astype(F32)).astype(o_ref.dtype)
        o_ref[:, f:] = (da * s1_ref[...].astype(F32)).astype(o_ref.dtype)

    return _hosted(
        body, name=name, grid=(s // tm,),
        in_specs=[pl.BlockSpec((tm, d), lambda m: (m, 0)), pl.BlockSpec((f, d), lambda m: (0, 0)),
                  pl.BlockSpec((tm, f), lambda m: (m, 0)), pl.BlockSpec((tm, f), lambda m: (m, 0))],
        out_specs=pl.BlockSpec((tm, 2 * f), lambda m: (m, 0)),
        out_shape=jax.ShapeDtypeStruct((s, 2 * f), BF16),
        compiler_params=_params("parallel"),
    )(dh, w, s1, q1)


def _mm_nt_col_rms_bwd(dy, w, h, gain, dh, name):
    s = dy.shape[0]
    nsh, k, ns = w.shape
    tm = _tile(s, 512)

    def body(dy_ref, w_ref, h_ref, g_ref, dh_ref, o_ref, dg_ref):
        du = _dot_nt(dy_ref[:, :ns], w_ref[0])
        for j in range(1, nsh):
            du = du + _dot_nt(dy_ref[:, j * ns:(j + 1) * ns], w_ref[j])
        dx, dg = _rms_bwd(du, h_ref[...], g_ref[...])
        o_ref[...] = dh_ref[...] + dx
        _accumulate(dg_ref, dg, pl.program_id(0) == 0)

    return _hosted(
        body, name=name, grid=(s // tm,),
        in_specs=[pl.BlockSpec((tm, nsh * ns), lambda m: (m, 0)), pl.BlockSpec((nsh, k, ns), lambda m: (0, 0, 0)),
                  pl.BlockSpec((tm, k), lambda m: (m, 0)), pl.BlockSpec((1, k), lambda m: (0, 0)),
                  pl.BlockSpec((tm, k), lambda m: (m, 0))],
        out_specs=[pl.BlockSpec((tm, k), lambda m: (m, 0)), pl.BlockSpec((1, k), lambda m: (0, 0))],
        out_shape=[jax.ShapeDtypeStruct((s, k), F32), jax.ShapeDtypeStruct((1, k), F32)],
        compiler_params=_params("arbitrary"),
    )(dy, w, h, gain, dh)


def _mm_tn(a, dy, nsh, name):
    s, k = a.shape
    ns = dy.shape[1] // nsh
    tm = _tile(s, 2048)
    tk = _tile(k, 1408, LANES)
    nk, nm = k // tk, s // tm

    def body(a_ref, dy_ref, o_ref, acc_ref):
        m = pl.program_id(2)
        part = _dot_tn(a_ref[...], dy_ref[...].astype(BF16))

        @pl.when(m == 0)
        def _():
            acc_ref[...] = part

        @pl.when(m > 0)
        def _():
            acc_ref[...] += part

        @pl.when(m == nm - 1)
        def _():
            o_ref[...] = acc_ref[...].astype(o_ref.dtype)

    return _hosted(
        body, name=name, grid=(nsh, nk, nm),
        in_specs=[pl.BlockSpec((tm, tk), lambda j, kk, m: (m, kk)), pl.BlockSpec((tm, ns), lambda j, kk, m: (m, j))],
        out_specs=pl.BlockSpec((None, tk, ns), lambda j, kk, m: (j, kk, 0)),
        out_shape=jax.ShapeDtypeStruct((nsh, k, ns), BF16),
        scratch_shapes=[pltpu.VMEM((tk, ns), F32)],
        compiler_params=_params("parallel", "parallel", "arbitrary"),
    )(a, dy)


def _main_spec(tm, w):
    return pl.BlockSpec((tm, w), lambda m: (m, 0))


def _before_spec(tm, hb, w):
    return pl.BlockSpec((hb, w), lambda m: (jnp.maximum(m * (tm // hb) - 1, 0), 0))


def _after_spec(tm, hb, w, s):
    return pl.BlockSpec((hb, w), lambda m: (jnp.minimum((m + 1) * (tm // hb), s // hb - 1), 0))


def _row_spec(w, rows=1):
    return pl.BlockSpec((rows, w), lambda m: (0, 0))


CHUNK_LANES = 4 * LANES
CHUNK_ROWS = 32


def _build_shifts(ext8_ref, residues=range(1, 8)):
    n = ext8_ref.shape[1] - 8
    for r in residues:
        ext8_ref[r, pl.ds(0, n), :] = ext8_ref[0, pl.ds(r, n), :]


def _fold_rows(x):
    return functools.reduce(lambda p, q: p + q, [x[i:i + 8] for i in range(0, x.shape[0], 8)])


def _shifted(ext8_ref, shift, r0, rows, cols):
    return ext8_ref[shift % 8, pl.ds(pl.multiple_of(shift - shift % 8 + r0, 8), rows), cols]


def _lane_chunk(i):
    return pl.ds(pl.multiple_of(i * CHUNK_LANES, CHUNK_LANES), CHUNK_LANES)


def _sum_terms(terms, ways=4):
    accs = []
    for i, t in enumerate(terms):
        if i < ways:
            accs.append(t)
        else:
            accs[i % ways] = accs[i % ways] + t
    while len(accs) > 1:
        accs = [accs[i] + accs[i + 1] if i + 1 < len(accs) else accs[i] for i in range(0, len(accs), 2)]
    return accs[0]


def _accumulate(ref, val, first):
    @pl.when(first)
    def _():
        ref[...] = val

    @pl.when(jnp.logical_not(first))
    def _():
        ref[...] += val


SCONV_Z_SHIFTS = tuple(SCONV_HALO - (SHORT_CONV_W - 1) + k for k in range(SHORT_CONV_W))


def _sconv_z_taps(zext_ref, r0, cols):
    return [_shifted(zext_ref, shift, r0, CHUNK_ROWS, cols) for shift in SCONV_Z_SHIFTS]


def _weighted(cw_ref, cols, terms):
    return _sum_terms((cw_ref[k:k + 1, cols] * t for k, t in enumerate(terms)), ways=len(terms))


def _sconv_fill_z(zext_ref, main_ref, before_ref, d, m):
    hb = SCONV_HALO
    zb = before_ref[:, d:2 * d].astype(F32) * before_ref[:, 2 * d:].astype(F32)
    zext_ref[pl.ds(0, hb), :] = jnp.where(m > 0, zb, 0.0)
    zext_ref[pl.ds(hb, main_ref.shape[0]), :] = main_ref[:, d:2 * d].astype(F32) * main_ref[:, 2 * d:].astype(F32)


def _sconv_fwd(bcv, cw, name):
    s, d3 = bcv.shape
    d = d3 // 3
    tm = _tile(s, 256, CHUNK_ROWS)
    row_chunks = tm // CHUNK_ROWS

    def body(main_ref, before_ref, cw_ref, p_ref, zext_ref):
        m = pl.program_id(0)
        _sconv_fill_z(zext_ref.at[0], main_ref, before_ref, d, m)
        _build_shifts(zext_ref, [shift % 8 for shift in SCONV_Z_SHIFTS if shift % 8])

        def chunk(i, carry):
            cols = _lane_chunk(i // row_chunks)
            r0 = pl.multiple_of((i % row_chunks) * CHUNK_ROWS, CHUNK_ROWS)
            rows = pl.ds(r0, CHUNK_ROWS)
            zc = _weighted(cw_ref, cols, _sconv_z_taps(zext_ref, r0, cols))
            p_ref[rows, cols] = (main_ref[rows, cols].astype(F32) * zc).astype(p_ref.dtype)
            return carry

        lax.fori_loop(0, row_chunks * (d // CHUNK_LANES), chunk, 0)

    return _hosted(
        body, name=name, grid=(s // tm,),
        in_specs=[_main_spec(tm, d3), _before_spec(tm, SCONV_HALO, d3), _row_spec(d, SHORT_CONV_W)],
        out_specs=_main_spec(tm, d),
        out_shape=jax.ShapeDtypeStruct((s, d), BF16),
        scratch_shapes=[pltpu.VMEM((8, tm + SCONV_HALO, d), F32)],
        compiler_params=_params("parallel"),
    )(bcv, bcv, cw)


def _sconv_bwd(dp, bcv, cw, name):
    s, d3 = bcv.shape
    d = d3 // 3
    tm = _tile(s, 256, CHUNK_ROWS)
    nm = s // tm
    ha = 8
    kw = SHORT_CONV_W

    def body(dp_ref, dpa_ref, main_ref, before_ref, after_ref, cw_ref, o_ref, dcw_ref, zext_ref, dext_ref):
        m = pl.program_id(0)
        _sconv_fill_z(zext_ref.at[0], main_ref, before_ref, d, m)
        _build_shifts(zext_ref, [shift % 8 for shift in SCONV_Z_SHIFTS if shift % 8])
        dext_ref[0, pl.ds(0, tm), :] = dp_ref[...] * main_ref[:, :d].astype(F32)
        dza = dpa_ref[...] * after_ref[:, :d].astype(F32)[0:ha]
        dext_ref[0, pl.ds(tm, ha), :] = jnp.where(m < nm - 1, dza, 0.0)
        _build_shifts(dext_ref, range(1, kw))

        @pl.when(m == 0)
        def _():
            dcw_ref[...] = jnp.zeros_like(dcw_ref)

        zero = jnp.zeros((8, CHUNK_LANES), F32)

        def lane_chunk(ci, carry):
            cols = _lane_chunk(ci)
            c_cols, v_cols = (pl.ds(pl.multiple_of(part * d + ci * CHUNK_LANES, CHUNK_LANES), CHUNK_LANES) for part in (1, 2))

            def row_chunk(ri, sums):
                r0 = pl.multiple_of(ri * CHUNK_ROWS, CHUNK_ROWS)
                rows = pl.ds(r0, CHUNK_ROWS)
                z = _sconv_z_taps(zext_ref, r0, cols)
                o_ref[rows, cols] = (dp_ref[rows, cols] * _weighted(cw_ref, cols, z)).astype(o_ref.dtype)
                dzc = [_shifted(dext_ref, kw - 1 - k, r0, CHUNK_ROWS, cols) for k in range(kw)]
                dz = _weighted(cw_ref, cols, dzc)
                o_ref[rows, c_cols] = (dz * main_ref[rows, v_cols].astype(F32)).astype(o_ref.dtype)
                o_ref[rows, v_cols] = (dz * main_ref[rows, c_cols].astype(F32)).astype(o_ref.dtype)
                return tuple(acc + _fold_rows(dzc[kw - 1] * z[k]) for k, acc in enumerate(sums))

            sums = lax.fori_loop(0, tm // CHUNK_ROWS, row_chunk, (zero,) * kw)
            for k in range(kw):
                dcw_ref[k:k + 1, cols] += _colsum(sums[k])
            return carry

        lax.fori_loop(0, d // CHUNK_LANES, lane_chunk, 0)

    return _hosted(
        body, name=name, grid=(nm,),
        in_specs=[_main_spec(tm, d), _after_spec(tm, ha, d, s), _main_spec(tm, d3), _before_spec(tm, SCONV_HALO, d3),
                  _after_spec(tm, SCONV_HALO, d3, s), _row_spec(d, SHORT_CONV_W)],
        out_specs=[_main_spec(tm, d3), _row_spec(d, 8)],
        out_shape=[jax.ShapeDtypeStruct((s, d3), BF16), jax.ShapeDtypeStruct((8, d), F32)],
        scratch_shapes=[pltpu.VMEM((8, tm + SCONV_HALO, d), F32), pltpu.VMEM((8, tm + ha, d), F32)],
        compiler_params=_params("arbitrary"),
    )(dp, dp, bcv, bcv, bcv, cw)


def _pool_counts(t0, tm, w):
    t = t0 + lax.broadcasted_iota(jnp.int32, (tm, 1), 0)
    return jnp.minimum(t + 1, w).astype(F32)


def _pool_fwd(h, gain, wg, scale, name):
    s, d = h.shape
    ng, cg, _ = wg.shape
    tm = _tile(s, 512, POOL_HALO)

    def body(h_ref, hb_ref, g_ref, wg_ref, sc_ref, o_ref, mx_ref, uext_ref):
        m = pl.program_id(0)
        x = h_ref[...]
        gain_row = g_ref[...]
        xb = hb_ref[...]
        uext_ref[pl.ds(0, POOL_HALO), :] = jnp.where(m > 0, xb * _rms_stats(xb) * gain_row, 0.0)
        uext_ref[pl.ds(POOL_HALO, tm), :] = x * _rms_stats(x) * gain_row
        for gi, win in enumerate(POOL_WINDOWS):
            cols = pl.ds(gi * cg, cg)
            u_g = uext_ref[pl.ds(POOL_HALO, tm), cols]
            acc = u_g
            for i in range(1, win):
                acc = acc + uext_ref[pl.ds(POOL_HALO - i, tm), cols]
            mixed = (acc / _pool_counts(m * tm, tm, win) - u_g).astype(BF16)
            mx_ref[:, cols] = mixed
            o_ref[:, cols] = x[:, gi * cg:(gi + 1) * cg] + _dot(mixed, wg_ref[gi]) * sc_ref[:, cols]

    return _hosted(
        body, name=name, grid=(s // tm,),
        in_specs=[_main_spec(tm, d), _before_spec(tm, POOL_HALO, d), _row_spec(d),
                  pl.BlockSpec((ng, cg, cg), lambda m: (0, 0, 0)), _row_spec(d)],
        out_specs=[_main_spec(tm, d), _main_spec(tm, d)],
        out_shape=[jax.ShapeDtypeStruct((s, d), F32), jax.ShapeDtypeStruct((s, d), BF16)],
        scratch_shapes=[pltpu.VMEM((tm + POOL_HALO, d), F32)],
        compiler_params=_params("parallel"),
    )(h, h, gain, wg, scale)


def _pool_bwd_mm(dh, mixed, wg, scale, name):
    s, d = dh.shape
    ng, cg, _ = wg.shape
    tm = _tile(s, 512)

    def body(dh_ref, mx_ref, wg_ref, sc_ref, dmx_ref, dwg_ref, dsc_ref):
        first = pl.program_id(0) == 0
        for gi in range(ng):
            cols = pl.ds(gi * cg, cg)
            dh_g = dh_ref[:, cols]
            mixed = mx_ref[:, cols]
            w_g = wg_ref[gi]
            dy = (dh_g * sc_ref[:, cols]).astype(BF16)
            dmx_ref[:, cols] = _dot_nt(dy, w_g)
            _accumulate(dsc_ref.at[:, cols], _colsum(dh_g * _dot(mixed, w_g)), first)
            _accumulate(dwg_ref.at[gi], _dot_tn(mixed, dy), first)

    return _hosted(
        body, name=name, grid=(s // tm,),
        in_specs=[_main_spec(tm, d), _main_spec(tm, d), pl.BlockSpec((ng, cg, cg), lambda m: (0, 0, 0)), _row_spec(d)],
        out_specs=[_main_spec(tm, d), pl.BlockSpec((ng, cg, cg), lambda m: (0, 0, 0)), _row_spec(d)],
        out_shape=[jax.ShapeDtypeStruct((s, d), F32), jax.ShapeDtypeStruct((ng, cg, cg), F32),
                   jax.ShapeDtypeStruct((1, d), F32)],
        compiler_params=_params("arbitrary"),
    )(dh, mixed, wg, scale)


def _pool_bwd_rms(dmixed, h, gain, dh, name):
    s, d = h.shape
    cg = d // len(POOL_WINDOWS)
    tm = _tile(s, 512, POOL_HALO)
    nm = s // tm

    def body(dmx_ref, dmxa_ref, h_ref, g_ref, dh_ref, o_ref, dg_ref, eext_ref, du_ref):
        m = pl.program_id(0)
        for gi, win in enumerate(POOL_WINDOWS):
            cols = pl.ds(gi * cg, cg)
            dmx = dmx_ref[:, cols]
            eext_ref[pl.ds(0, tm), cols] = dmx / _pool_counts(m * tm, tm, win)
            ea = dmxa_ref[:, cols] / _pool_counts((m + 1) * tm, POOL_HALO, win)
            eext_ref[pl.ds(tm, POOL_HALO), cols] = jnp.where(m < nm - 1, ea, 0.0)
            acc = -dmx
            for i in range(win):
                acc = acc + eext_ref[pl.ds(i, tm), cols]
            du_ref[:, cols] = acc
        dx, dg = _rms_bwd(du_ref[...], h_ref[...], g_ref[...])
        o_ref[...] = dh_ref[...] + dx
        _accumulate(dg_ref, dg, m == 0)

    return _hosted(
        body, name=name, grid=(nm,),
        in_specs=[_main_spec(tm, d), _after_spec(tm, POOL_HALO, d, s), _main_spec(tm, d), _row_spec(d), _main_spec(tm, d)],
        out_specs=[_main_spec(tm, d), _row_spec(d)],
        out_shape=[jax.ShapeDtypeStruct((s, d), F32), jax.ShapeDtypeStruct((1, d), F32)],
        scratch_shapes=[pltpu.VMEM((tm + POOL_HALO, d), F32), pltpu.VMEM((tm, d), F32)],
        compiler_params=_params("arbitrary"),
    )(dmixed, dmixed, h, gain, dh)


def _conf_fill_h(hext_ref, main_ref, before_ref, d, m):
    hb = before_ref[:, :d].astype(F32) * _sigmoid(before_ref[:, d:].astype(F32))
    hext_ref[pl.ds(0, CONF_HALO), :] = jnp.where(m > 0, hb, 0.0)
    hext_ref[pl.ds(CONF_HALO, main_ref.shape[0]), :] = main_ref[:, :d].astype(F32) * _sigmoid(main_ref[:, d:].astype(F32))


def _layernorm_parts(hc, g, b):
    mu = jnp.mean(hc, axis=-1, keepdims=True)
    xc = hc - mu
    rs = lax.rsqrt(jnp.mean(xc * xc, axis=-1, keepdims=True) + LN_EPS)
    xhat = xc * rs
    return xhat, rs, xhat * g + b


def _conf_mid_fwd(ag, dw, b_dw, ln_g, ln_b, name):
    s, d2 = ag.shape
    d = d2 // 2
    tm = _tile(s, 256, CONF_HALO)
    base = CONF_HALO - (CONF_CONV_W - 1)

    def body(main_ref, before_ref, dw_ref, bdw_ref, g_ref, b_ref, s_ref, hc_ref, hext_ref):
        m = pl.program_id(0)
        _conf_fill_h(hext_ref.at[0], main_ref, before_ref, d, m)
        _build_shifts(hext_ref)
        row_chunks = tm // CHUNK_ROWS

        def conv_chunk(i, carry):
            cols = _lane_chunk(i // row_chunks)
            r0 = pl.multiple_of((i % row_chunks) * CHUNK_ROWS, CHUNK_ROWS)
            taps = (dw_ref[kk:kk + 1, cols] * _shifted(hext_ref, base + kk, r0, CHUNK_ROWS, cols) for kk in range(CONF_CONV_W))
            hc_ref[pl.ds(r0, CHUNK_ROWS), cols] = bdw_ref[:, cols] + _sum_terms(taps, ways=1)
            return carry

        lax.fori_loop(0, row_chunks * (d // CHUNK_LANES), conv_chunk, 0)
        _, _, l = _layernorm_parts(hc_ref[...], g_ref[...], b_ref[...])
        s_ref[...] = (l * _sigmoid(l)).astype(s_ref.dtype)

    return _hosted(
        body, name=name, grid=(s // tm,),
        in_specs=[_main_spec(tm, d2), _before_spec(tm, CONF_HALO, d2), _row_spec(d, CONF_CONV_W), _row_spec(d),
                  _row_spec(d), _row_spec(d)],
        out_specs=[_main_spec(tm, d), _main_spec(tm, d)],
        out_shape=[jax.ShapeDtypeStruct((s, d), BF16), jax.ShapeDtypeStruct((s, d), F32)],
        scratch_shapes=[pltpu.VMEM((8, tm + CONF_HALO, d), F32)],
        compiler_params=_params("parallel"),
    )(ag, ag, dw, b_dw, ln_g, ln_b)


def _conf_out_bwd(dh, w, hc, ln_g, ln_b, name):
    s, d = dh.shape
    tm = _tile(s, 256)

    def body(dh_ref, w_ref, hc_ref, g_ref, b_ref, o_ref, dg_ref, db_ref, dbo_ref):
        first = pl.program_id(0) == 0
        dh_t = dh_ref[...]
        ds = _dot_nt(dh_t.astype(BF16), w_ref[...])
        xhat, rs, l = _layernorm_parts(hc_ref[...], g_ref[...], b_ref[...])
        sg = _sigmoid(l)
        dl = ds * sg * (1.0 + l * (1.0 - sg))
        dxh = dl * g_ref[...]
        o_ref[...] = rs * (dxh - jnp.mean(dxh, axis=-1, keepdims=True)
                           - xhat * jnp.mean(dxh * xhat, axis=-1, keepdims=True))
        _accumulate(dg_ref, _colsum(dl * xhat), first)
        _accumulate(db_ref, _colsum(dl), first)
        _accumulate(dbo_ref, _colsum(dh_t), first)

    return _hosted(
        body, name=name, grid=(s // tm,),
        in_specs=[_main_spec(tm, d), pl.BlockSpec((d, d), lambda m: (0, 0)), _main_spec(tm, d), _row_spec(d), _row_spec(d)],
        out_specs=[_main_spec(tm, d), _row_spec(d), _row_spec(d), _row_spec(d)],
        out_shape=[jax.ShapeDtypeStruct((s, d), F32)] + [jax.ShapeDtypeStruct((1, d), F32)] * 3,
        compiler_params=_params("arbitrary"),
    )(dh, w, hc, ln_g, ln_b)


def _conf_mid_bwd(dhc, ag, dw, name):
    s, d2 = ag.shape
    d = d2 // 2
    tm = _tile(s, 256, CONF_HALO)
    nm = s // tm
    kw = CONF_CONV_W
    base = CONF_HALO - (kw - 1)

    def body(dhc_ref, dhca_ref, main_ref, before_ref, dw_ref, o_ref, ddw_ref, dbdw_ref, dbpw_ref, hext_ref, dext_ref):
        m = pl.program_id(0)
        first = m == 0
        _conf_fill_h(hext_ref.at[0], main_ref, before_ref, d, m)
        _build_shifts(hext_ref)
        dext_ref[0, pl.ds(0, tm), :] = dhc_ref[...]
        dext_ref[0, pl.ds(tm, CONF_HALO), :] = jnp.where(m < nm - 1, dhca_ref[...], 0.0)
        _build_shifts(dext_ref)

        @pl.when(first)
        def _():
            ddw_ref[...] = jnp.zeros_like(ddw_ref)
            dbdw_ref[...] = jnp.zeros_like(dbdw_ref)
            dbpw_ref[...] = jnp.zeros_like(dbpw_ref)

        zero = jnp.zeros((8, CHUNK_LANES), F32)
        tap_group = 8

        def fold(x):
            return functools.reduce(lambda p, q: p + q, [x[i:i + 8] for i in range(0, CHUNK_ROWS, 8)])

        def lane_chunk(ci, carry):
            cols = _lane_chunk(ci)
            gate_cols = pl.ds(pl.multiple_of(d + ci * CHUNK_LANES, CHUNK_LANES), CHUNK_LANES)

            def through_conv(ri, sums):
                r0 = pl.multiple_of(ri * CHUNK_ROWS, CHUNK_ROWS)
                rows = pl.ds(r0, CHUNK_ROWS)
                dhh = _sum_terms((dw_ref[kk:kk + 1, cols] * _shifted(dext_ref, kw - 1 - kk, r0, CHUNK_ROWS, cols)
                                  for kk in range(kw)), ways=1)
                a = main_ref[rows, cols].astype(F32)
                sg = _sigmoid(main_ref[rows, gate_cols].astype(F32))
                da = dhh * sg
                dgate = dhh * a * sg * (1.0 - sg)
                o_ref[rows, cols] = da.astype(o_ref.dtype)
                o_ref[rows, gate_cols] = dgate.astype(o_ref.dtype)
                return sums[0] + fold(da), sums[1] + fold(dgate), sums[2] + fold(dext_ref[0, rows, cols])

            sum_da, sum_dgate, sum_dhc = lax.fori_loop(0, tm // CHUNK_ROWS, through_conv, (zero, zero, zero))
            dbdw_ref[:, cols] += _colsum(sum_dhc)
            dbpw_ref[:, cols] += _colsum(sum_da)
            dbpw_ref[:, gate_cols] += _colsum(sum_dgate)

            for k0 in range(0, kw, tap_group):
                group = range(k0, min(k0 + tap_group, kw))

                def tap_gradients(ri, accs, group=group):
                    for sub in range(0, CHUNK_ROWS, 8):
                        r0 = pl.multiple_of(ri * CHUNK_ROWS + sub, 8)
                        dhc_c = dext_ref[0, pl.ds(r0, 8), cols]
                        accs = tuple(acc + dhc_c * _shifted(hext_ref, base + kk, r0, 8, cols) for kk, acc in zip(group, accs))
                    return accs

                accs = lax.fori_loop(0, tm // CHUNK_ROWS, tap_gradients, (zero,) * len(group))
                for kk, acc in zip(group, accs):
                    ddw_ref[kk:kk + 1, cols] += _colsum(acc)
            return carry

        lax.fori_loop(0, d // CHUNK_LANES, lane_chunk, 0)

    return _hosted(
        body, name=name, grid=(nm,),
        in_specs=[_main_spec(tm, d), _after_spec(tm, CONF_HALO, d, s), _main_spec(tm, d2), _before_spec(tm, CONF_HALO, d2),
                  _row_spec(d, kw)],
        out_specs=[_main_spec(tm, d2), _row_spec(d, 32), _row_spec(d), _row_spec(d2)],
        out_shape=[jax.ShapeDtypeStruct((s, d2), BF16), jax.ShapeDtypeStruct((32, d), F32),
                   jax.ShapeDtypeStruct((1, d), F32), jax.ShapeDtypeStruct((1, d2), F32)],
        scratch_shapes=[pltpu.VMEM((8, tm + CONF_HALO, d), F32), pltpu.VMEM((8, tm + CONF_HALO, d), F32)],
        compiler_params=_params("arbitrary"),
    )(dhc, dhc, ag, ag, dw)


def _loss_head(h, gain, target, name):
    s, d = h.shape
    tm = _tile(s, 512)

    def body(h_ref, g_ref, t_ref, loss_ref, dh_ref, dg_ref):
        first = pl.program_id(0) == 0
        x = h_ref[...]
        err = x * _rms_stats(x) * g_ref[...] - t_ref[...]
        part = 0.5 * jnp.sum(jnp.mean(err * err, axis=-1, keepdims=True), axis=0, keepdims=True)
        dx, dg = _rms_bwd(err * (1.0 / d), x, g_ref[...])
        dh_ref[...] = dx
        _accumulate(loss_ref, part, first)
        _accumulate(dg_ref, dg, first)

    return _hosted(
        body, name=name, grid=(s // tm,),
        in_specs=[_main_spec(tm, d), _row_spec(d), _main_spec(tm, d)],
        out_specs=[pl.BlockSpec((1, 1), lambda m: (0, 0)), _main_spec(tm, d), _row_spec(d)],
        out_shape=[jax.ShapeDtypeStruct((1, 1), F32), jax.ShapeDtypeStruct((s, d), F32), jax.ShapeDtypeStruct((1, d), F32)],
        compiler_params=_params("arbitrary"),
    )(h, gain, target)


def _ffn_fwd(h, wts, i):
    u, act, s1, q1 = _ffn_up(h, wts[f"ln2_{i}"], wts[f"ffn{i}_w_gu"], f"ffn{i}_up")
    h_new = _mm_row(act, wts[f"ffn{i}_w_down"], h, None, f"ffn{i}_down")
    return h_new, (h, u, act, s1, q1)


def _ffn_bwd(dh, saved, wts, i, g):
    h, u, act, s1, q1 = saved
    dgu = _ffn_down_bwd(dh, wts[f"ffn{i}_w_down"], s1, q1, f"ffn{i}_down_bwd")
    g[f"ffn{i}_w_down"] = _mm_tn(act, dh, 1, f"ffn{i}_dw_down")
    g[f"ffn{i}_w_gu"] = _mm_tn(u, dgu, N_CHIPS, f"ffn{i}_dw_gu")
    dh_new, g[f"ln2_{i}"] = _mm_nt_col_rms_bwd(dgu, wts[f"ffn{i}_w_gu"], h, wts[f"ln2_{i}"], dh, f"ffn{i}_up_bwd")
    return dh_new


def _device_step(x, target, wts, g=None):
    g = {} if g is None else g
    saved = {}
    h = x

    def short_conv_fwd(h, i):
        u, bcv = _mm_col(h, wts[f"ln1_{i}"], wts[f"a{i}_w_in"], None, f"a{i}_in")
        p = _sconv_fwd(bcv, wts[f"a{i}_conv"], f"a{i}_conv")
        return _mm_row(p, wts[f"a{i}_w_out"], h, None, f"a{i}_out"), (h, u, bcv, p)

    def short_conv_bwd(dh, sv, i):
        h, u, bcv, p = sv
        dp = _mm_nt_row(dh, wts[f"a{i}_w_out"], f"a{i}_out_bwd")
        dbcv, dcw = _sconv_bwd(dp, bcv, wts[f"a{i}_conv"], f"a{i}_conv_bwd")
        g[f"a{i}_conv"] = dcw[:SHORT_CONV_W]
        g[f"a{i}_w_in"] = _mm_tn(u, dbcv, N_CHIPS, f"a{i}_dw_in")
        g[f"a{i}_w_out"] = _mm_tn(p, dh, 1, f"a{i}_dw_out")
        dh, g[f"ln1_{i}"] = _mm_nt_col_rms_bwd(dbcv, wts[f"a{i}_w_in"], h, wts[f"ln1_{i}"], dh, f"a{i}_in_bwd")
        return dh

    h, saved["a0"] = short_conv_fwd(h, 0)
    h, saved["f0"] = _ffn_fwd(h, wts, 0)

    h_in = h
    h, mixed = _pool_fwd(h, wts["ln1_1"], wts["b1_w_grp"], wts["b1_scale"], "b1_fwd")
    saved["b1"] = (h_in, mixed)
    h, saved["f1"] = _ffn_fwd(h, wts, 1)

    h_in = h
    u, ag = _mm_col(h, wts["ln1_2"], wts["c2_w_pw1"], wts["c2_b_pw1"], "c2_pw1")
    sw, hc = _conf_mid_fwd(ag, wts["c2_dw"], wts["c2_b_dw"], wts["c2_ln_g"], wts["c2_ln_b"], "c2_mid")
    h = _mm_row(sw, wts["c2_w_pw2"], h, wts["c2_b_pw2"], "c2_pw2")
    saved["c2"] = (h_in, u, ag, sw, hc)
    h, saved["f2"] = _ffn_fwd(h, wts, 2)

    h, saved["a3"] = short_conv_fwd(h, 3)
    h, saved["f3"] = _ffn_fwd(h, wts, 3)

    loss, dh, g["ln_f"] = _loss_head(h, wts["ln_f"], target, "loss_head")

    def ffn_bwd(dh, i):
        return _ffn_bwd(dh, saved[f"f{i}"], wts, i, g)

    dh = ffn_bwd(dh, 3)
    dh = short_conv_bwd(dh, saved["a3"], 3)

    dh = ffn_bwd(dh, 2)
    h_in, u, ag, sw, hc = saved["c2"]
    dhc, g["c2_ln_g"], g["c2_ln_b"], g["c2_b_pw2"] = _conf_out_bwd(
        dh, wts["c2_w_pw2"], hc, wts["c2_ln_g"], wts["c2_ln_b"], "c2_pw2_bwd")
    g["c2_w_pw2"] = _mm_tn(sw, dh, 1, "c2_dw_pw2")
    dag, ddw, g["c2_b_dw"], g["c2_b_pw1"] = _conf_mid_bwd(dhc, ag, wts["c2_dw"], "c2_mid_bwd")
    g["c2_dw"] = ddw[:CONF_CONV_W]
    g["c2_w_pw1"] = _mm_tn(u, dag, N_CHIPS, "c2_dw_pw1")
    dh, g["ln1_2"] = _mm_nt_col_rms_bwd(dag, wts["c2_w_pw1"], h_in, wts["ln1_2"], dh, "c2_pw1_bwd")

    dh = ffn_bwd(dh, 1)
    h_in, mixed = saved["b1"]
    dmixed, g["b1_w_grp"], g["b1_scale"] = _pool_bwd_mm(dh, mixed, wts["b1_w_grp"], wts["b1_scale"], "b1_bwd_mm")
    dh, g["ln1_1"] = _pool_bwd_rms(dmixed, h_in, wts["ln1_1"], dh, "b1_bwd_rms")

    dh = ffn_bwd(dh, 0)
    dh = short_conv_bwd(dh, saved["a0"], 0)
    return loss, dh, g


MESH = pl.DeviceIdType.MESH
ANY = pl.BlockSpec(memory_space=pl.ANY)


def _position():
    return lax.axis_index("x"), lax.axis_index("y"), lax.axis_index("c")


def _other_chips(x, y):
    return [(1 - x, y), (x, 1 - y), (1 - x, 1 - y)]


def _remote(src, dst, send_sem, recv_sem, to):
    return pltpu.make_async_remote_copy(src_ref=src, dst_ref=dst, send_sem=send_sem, recv_sem=recv_sem,
                                        device_id=to, device_id_type=MESH)


def _half_rows(ref_rows, c):
    hr = ref_rows // 2
    return pl.ds(pl.multiple_of(c * hr, 16), hr)


def _allgather8(v, name):
    m_per, n = v.shape

    def body(v_ref, out_ref, send_sems, recv_sems, local_sem):
        x, y, c = _position()
        me, sibling = (x, y, c), (x, y, 1 - c)
        chips = _other_chips(x, y)

        def rows(px, py, pc):
            return out_ref.at[pl.ds((4 * px + 2 * py + pc) * m_per, m_per), :]

        def copy(k, block, to, src=None):
            return _remote(rows(*block) if src is None else src, rows(*block), send_sems.at[k], recv_sems.at[k], to)

        mine = pltpu.make_async_copy(v_ref, rows(*me), local_sem)
        mine.start()
        first = [copy(0, me, sibling, src=v_ref)]
        first += [copy(1 + j, me, (*chip, c), src=v_ref) for j, chip in enumerate(chips)]
        for cp in first:
            cp.start()
        passed = [copy(4 + j, (*chip, c), sibling) for j, chip in enumerate(chips)]
        for j, chip in enumerate(chips):
            copy(1 + j, (*chip, c), me).wait_recv()
            passed[j].start()
        copy(0, sibling, me).wait_recv()
        for j, chip in enumerate(chips):
            copy(4 + j, (*chip, 1 - c), me).wait_recv()
        for cp in first + passed:
            cp.wait_send()
        mine.wait()

    return _hosted(
        body, name=name,
        out_shape=jax.ShapeDtypeStruct((N_DEV * m_per, n), v.dtype),
        in_specs=[pl.BlockSpec(memory_space=pltpu.VMEM)],
        out_specs=pl.BlockSpec(memory_space=pltpu.VMEM),
        scratch_shapes=[pltpu.SemaphoreType.DMA((7,)), pltpu.SemaphoreType.DMA((7,)), pltpu.SemaphoreType.DMA],
        compiler_params=pltpu.CompilerParams(vmem_limit_bytes=VMEM_LIMIT),
    )(v)


def _cast_to_slot(ws, idx, name):
    r, cols = ws[0].shape
    assert all(w.shape == (r, cols) for w in ws)
    n = len(ws)
    tr = _tile(r, 256, 16)

    def body(idx_ref, *refs):
        for w_ref, o_ref in zip(refs[:n], refs[n:]):
            o_ref[...] = w_ref[...].astype(o_ref.dtype)

    return _hosted(
        body, name=name,
        grid_spec=pltpu.PrefetchScalarGridSpec(
            num_scalar_prefetch=1, grid=(r // tr,),
            in_specs=[pl.BlockSpec((tr, cols), lambda t, idx_ref: (t, 0))] * n,
            out_specs=[pl.BlockSpec((None, tr, cols), lambda t, idx_ref: (idx_ref[0], t, 0))] * n),
        out_shape=[jax.ShapeDtypeStruct((N_CHIPS, r, cols), BF16)] * n,
        compiler_params=_params("parallel"),
    )(idx, *ws)


def _dma_sems(*shape):
    return [pltpu.SemaphoreType.DMA(shape), pltpu.SemaphoreType.DMA(shape)]


def _same_shapes(arrays):
    return [jax.ShapeDtypeStruct(a.shape, a.dtype) for a in arrays]


def _part_rows(ref_rows, c, part):
    hr = ref_rows // 2
    i, n = part
    size = hr // n
    assert size * n == hr and size % 16 == 0, (ref_rows, part)
    return pl.ds(pl.multiple_of(c * hr + i * size, 16), size)


def _task_gather_ici(bufs, done, part=(0, 1)):
    n = len(bufs)

    def copies(outs, sems, landing):
        x, y, c = _position()
        my_chip = 2 * x + y
        res = []
        for i in range(n):
            rows = _part_rows(bufs[i].shape[1], c, part)
            for r, (px, py) in enumerate(_other_chips(x, y)):
                slot = (2 * px + py) if landing else my_chip
                res.append(_remote(outs[i].at[my_chip, rows, :], outs[i].at[slot, rows, :], sems[0].at[i, r], sems[1].at[i, r],
                                   (px, py, c)))
        return res

    def start(ins, outs, sems):
        for cp in copies(outs, sems, False):
            cp.start()

    def wait(ins, outs, sems):
        for cp in copies(outs, sems, True):
            cp.wait_recv()
            cp.wait_send()

    return _Task(bufs, _same_shapes(bufs), {i: i for i in range(n)}, _dma_sems(n, 3), start, wait, done)


def _task_gather_d2d(bufs, done):
    n = len(bufs)

    def copies(outs, sems, landing):
        x, y, c = _position()
        res = []
        for i in range(n):
            rows = _half_rows(bufs[i].shape[1], (1 - c) if landing else c)
            for r, (px, py) in enumerate(_other_chips(x, y)):
                part = outs[i].at[2 * px + py, rows, :]
                res.append(_remote(part, part, sems[0].at[i, r], sems[1].at[i, r], (x, y, 1 - c)))
        return res

    def start(ins, outs, sems):
        for cp in copies(outs, sems, False):
            cp.start()

    def wait(ins, outs, sems):
        for cp in copies(outs, sems, True):
            cp.wait_recv()
        for cp in copies(outs, sems, False):
            cp.wait_send()

    return _Task(bufs, _same_shapes(bufs), {i: i for i in range(n)}, _dma_sems(n, 3), start, wait, done)


def _task_sibling_halves(grads, done):
    n = len(grads)

    def copies(ins, outs, sems):
        x, y, c = _position()
        return [_remote(ins[i].at[:, _half_rows(grads[i].shape[1], 1 - c), :], outs[i], sems[0].at[i], sems[1].at[i],
                        (x, y, 1 - c)) for i in range(n)]

    def start(ins, outs, sems):
        for cp in copies(ins, outs, sems):
            cp.start()

    def wait(ins, outs, sems):
        for cp in copies(ins, outs, sems):
            cp.wait()

    shapes = [jax.ShapeDtypeStruct((g.shape[0], g.shape[1] // 2, g.shape[2]), g.dtype) for g in grads]
    return _Task(grads, shapes, {}, _dma_sems(n), start, wait, done)


def _task_chip_sums(parts, done, landed=None, part=(0, 1)):
    n = len(parts)
    i_part, n_parts = part
    sizes = [p.shape[1] // n_parts for p in parts]
    assert all(p.shape[1] == size * n_parts and size % 16 == 0 for p, size in zip(parts, sizes)), part
    rows = [pl.ds(i_part * size, size) for size in sizes]

    def copies(ins, outs, sems):
        x, y, c = _position()
        return [_remote(ins[i].at[2 * px + py, rows[i], :], outs[i].at[r, rows[i], :], sems[0].at[i, r], sems[1].at[i, r],
                        (px, py, c))
                for i in range(n) for r, (px, py) in enumerate(_other_chips(x, y))]

    def start(ins, outs, sems):
        for cp in copies(ins, outs, sems):
            cp.start()

    def wait(ins, outs, sems):
        for cp in copies(ins, outs, sems):
            cp.wait()

    shapes = [jax.ShapeDtypeStruct((3,) + p.shape[1:], p.dtype) for p in parts]
    if landed is None:
        return _Task(parts, shapes, {}, _dma_sems(n, 3), start, wait, done)
    return _Task(list(parts) + list(landed), shapes, {n + i: i for i in range(n)}, _dma_sems(n, 3), start, wait, done)


def _task_sibling_parts(owns, landeds, done):
    n = len(owns)

    def copies(ins, outs, sems):
        x, y, c = _position()
        sibling = (x, y, 1 - c)
        res = []
        for i in range(n):
            res.append(_remote(ins[i].at[2 * x + y], outs[i].at[0], sems[0].at[i, 0], sems[1].at[i, 0], sibling))
            res.append(_remote(ins[n + i], outs[i].at[pl.ds(1, 3)], sems[0].at[i, 1], sems[1].at[i, 1], sibling))
        return res

    def start(ins, outs, sems):
        for cp in copies(ins, outs, sems):
            cp.start()

    def wait(ins, outs, sems):
        for cp in copies(ins, outs, sems):
            cp.wait()

    return _Task(list(owns) + list(landeds), _same_shapes(owns), {}, _dma_sems(n, 2), start, wait, done)


def _add_halves(grads, sibs, idx, name):
    n = len(grads)
    nsh = grads[0].shape[0]

    def body(idx_ref, *refs):
        for g_ref, s_ref, o_ref in zip(refs[:n], refs[n:2 * n], refs[2 * n:]):
            o_ref[...] = (g_ref[...].astype(F32) + s_ref[...].astype(F32)).astype(o_ref.dtype)

    half_of = [pl.BlockSpec((None, s.shape[1], s.shape[2]), lambda j, idx_ref: (j, idx_ref[1], 0)) for s in sibs]
    whole = [pl.BlockSpec((None, s.shape[1], s.shape[2]), lambda j, idx_ref: (j, 0, 0)) for s in sibs]
    return _hosted(
        body, name=name,
        grid_spec=pltpu.PrefetchScalarGridSpec(num_scalar_prefetch=1, grid=(nsh,), in_specs=half_of + whole, out_specs=whole),
        out_shape=_same_shapes(sibs),
        compiler_params=_params("parallel"),
    )(idx, *grads, *sibs)


def _adamw_reduced(w, own, landed, sib, m, v, idx, name):
    r, cols = w.shape
    hr = r // 2
    tr = _tile(hr, 256, 16)
    nt = hr // tr

    def body(idx_ref, w_ref, p_ref, l_ref, s_ref, m_ref, v_ref, go_ref, d_ref, mo_ref, vo_ref):
        mine = p_ref[...].astype(F32)
        for k in range(3):
            mine = mine + l_ref[k].astype(F32)
        theirs = s_ref[0].astype(F32)
        for k in range(1, 4):
            theirs = theirs + s_ref[k].astype(F32)
        grad = jnp.where(pl.program_id(0) // nt == idx_ref[1], mine, theirs)
        go_ref[...] = grad
        d_ref[...], mo_ref[...], vo_ref[...] = _adamw_update(w_ref[...], grad, m_ref[...], v_ref[...])

    def in_half(t, half):
        return jnp.clip(t - half * nt, 0, nt - 1)

    full = pl.BlockSpec((tr, cols), lambda t, idx_ref: (t, 0))
    return _hosted(
        body, name=name,
        grid_spec=pltpu.PrefetchScalarGridSpec(
            num_scalar_prefetch=1, grid=(2 * nt,),
            in_specs=[full,
                      pl.BlockSpec((None, tr, cols), lambda t, idx_ref: (idx_ref[0], in_half(t, idx_ref[1]), 0)),
                      pl.BlockSpec((3, tr, cols), lambda t, idx_ref: (0, in_half(t, idx_ref[1]), 0)),
                      pl.BlockSpec((4, tr, cols), lambda t, idx_ref: (0, in_half(t, 1 - idx_ref[1]), 0)),
                      full, full],
            out_specs=[full] * 4),
        out_shape=[jax.ShapeDtypeStruct((r, cols), F32)] * 4,
        compiler_params=_params("arbitrary"),
    )(idx, w, own, landed, sib, m, v)


def _sum_devices(blocks, name):
    m8, n = blocks.shape
    m = m8 // N_DEV

    def body(b_ref, o_ref):
        acc = b_ref[pl.ds(0, m), :]
        for k in range(1, N_DEV):
            acc = acc + b_ref[pl.ds(k * m, m), :]
        o_ref[...] = acc

    return _hosted(
        body, name=name, out_shape=jax.ShapeDtypeStruct((m, n), F32),
        in_specs=[pl.BlockSpec(memory_space=pltpu.VMEM)], out_specs=pl.BlockSpec(memory_space=pltpu.VMEM),
        compiler_params=pltpu.CompilerParams(vmem_limit_bytes=VMEM_LIMIT),
    )(blocks)


def _adamw_update(w, grad, m, v):
    new_m = ADAM_B1 * m + (1.0 - ADAM_B1) * grad
    new_v = ADAM_B2 * v + (1.0 - ADAM_B2) * (grad * grad)
    m_hat = new_m * (1.0 / (1.0 - ADAM_B1 ** ADAM_STEP))
    v_hat = new_v * (1.0 / (1.0 - ADAM_B2 ** ADAM_STEP))
    return -ADAM_LR * (m_hat / (jnp.sqrt(v_hat) + ADAM_EPS) + ADAM_WD * w), new_m, new_v


def _adamw(w, g, m, v, name):
    r, cols = w.shape
    tr = _tile(r, 256)

    def body(w_ref, g_ref, m_ref, v_ref, go_ref, d_ref, mo_ref, vo_ref):
        grad = g_ref[...]
        go_ref[...] = grad
        d_ref[...], mo_ref[...], vo_ref[...] = _adamw_update(w_ref[...], grad, m_ref[...], v_ref[...])

    spec = pl.BlockSpec((tr, cols), lambda t: (t, 0))
    return _hosted(
        body, name=name, grid=(r // tr,), in_specs=[spec] * 4, out_specs=[spec] * 4,
        out_shape=[jax.ShapeDtypeStruct((r, cols), F32)] * 4,
        compiler_params=_params("parallel"),
    )(w, g, m, v)


def _adamw_small(grad_blocks, params, name):
    nb, npar = len(grad_blocks), len(params)

    def body(*refs):
        blocks, ins, outs = refs[:nb], refs[nb:nb + 3 * npar], refs[nb + 3 * npar:]
        for p, (w, _, _, blk, row0) in enumerate(params):
            if w.ndim == 1:
                tiled = (w.shape[0] // LANES, LANES)
                grad = blocks[blk][pl.ds(row0, tiled[0]), pl.ds(0, LANES)]
                wmv = [ins[3 * p + k][...].reshape(tiled) for k in range(3)]
            else:
                grad = blocks[blk][pl.ds(row0, w.shape[0]), :]
                wmv = [ins[3 * p + k][...] for k in range(3)]
            for k, res in enumerate((grad,) + _adamw_update(wmv[0], grad, wmv[1], wmv[2])):
                outs[4 * p + k][...] = res.reshape(w.shape)

    args = list(grad_blocks) + [a for w, m, v, _, _ in params for a in (w, m, v)]
    vmem = pl.BlockSpec(memory_space=pltpu.VMEM)
    out = _hosted(
        body, name=name, in_specs=[vmem] * len(args), out_specs=[vmem] * (4 * npar),
        out_shape=[jax.ShapeDtypeStruct(w.shape, F32) for w, _, _, _, _ in params for _ in range(4)],
    )(*args)
    return [tuple(out[4 * p:4 * p + 4]) for p in range(npar)]


WEIGHT_NAMES = (
    "ln1_0", "a0_w_in", "a0_conv", "a0_w_out", "ln2_0", "ffn0_w_gu", "ffn0_w_down",
    "ln1_1", "b1_w_grp", "b1_scale", "ln2_1", "ffn1_w_gu", "ffn1_w_down",
    "ln1_2", "c2_w_pw1", "c2_b_pw1", "c2_dw", "c2_b_dw", "c2_ln_g", "c2_ln_b", "c2_w_pw2", "c2_b_pw2",
    "ln2_2", "ffn2_w_gu", "ffn2_w_down",
    "ln1_3", "a3_w_in", "a3_conv", "a3_w_out", "ln2_3", "ffn3_w_gu", "ffn3_w_down", "ln_f")
BIG = ("a0_w_in", "a0_w_out", "ffn0_w_gu", "ffn0_w_down", "b1_w_grp", "ffn1_w_gu", "ffn1_w_down", "c2_w_pw1", "c2_w_pw2",
       "ffn2_w_gu", "ffn2_w_down", "a3_w_in", "a3_w_out", "ffn3_w_gu", "ffn3_w_down")
GROUPED = "b1_w_grp"
SMALL_SHARDED = ("a0_conv", "a3_conv", "c2_dw")
REPLICATED = tuple(n for n in WEIGHT_NAMES if n not in BIG and n not in SMALL_SHARDED)


def _pad_rows(a, mult=8):
    pad = -a.shape[0] % mult
    return a if pad == 0 else jnp.concatenate([a, jnp.zeros((pad, a.shape[1]), a.dtype)], axis=0)


def _pack_rows(parts, width):
    rows = [p.reshape(-1, width) for p in parts]
    return _pad_rows(jnp.concatenate(rows, axis=0)), [r.shape[0] for r in rows]


def _unpack_rows(packed, counts, shapes):
    out, at = [], 0
    for n, shp in zip(counts, shapes):
        out.append(packed[at:at + n].reshape(shp))
        at += n
    return out


COLUMN_SHARDED = ("w_in", "w_gu", "w_pw1")


class _Weights(dict):
    def __init__(self, bufs):
        super().__init__()
        self.bufs = bufs

    def __missing__(self, name):
        buf = self.bufs[name]
        if name == GROUPED:
            cg = buf.shape[-1]
            rq = cg // N_CHIPS
            return jnp.transpose(buf.reshape(N_CHIPS, -1, rq, cg), (1, 0, 2, 3)).reshape(-1, cg, cg)
        return buf if name.endswith(COLUMN_SHARDED) else buf.reshape(-1, buf.shape[-1])


class _Exchange:
    def __init__(self, w, mom, vel, idx):
        def shards(table):
            return {n: table[n].reshape(-1, table[n].shape[-1]) for n in BIG}

        self.w, self.mom, self.vel, self.idx = shards(w), shards(mom), shards(vel), idx
        self.bufs = {}
        self.weights = _Weights(self.bufs)
        self.grads = {}
        self.sib, self.part, self.landed, self.sib_parts, self.updates = {}, {}, {}, {}, {}

    def cast(self, names):
        by_shape = {}
        for n in names:
            by_shape.setdefault(self.w[n].shape, []).append(n)
        for group in by_shape.values():
            self.bufs.update(zip(group, _cast_to_slot([self.w[n] for n in group], self.idx, f"cast_{group[0]}")))

    @staticmethod
    def _store(table, names):
        def done(arrays):
            table.update(zip(names, arrays))
        return done

    def _grad(self, n):
        g = self.grads[n]
        if n == GROUPED:
            ng, cg, _ = g.shape
            g = jnp.transpose(g.reshape(ng, N_CHIPS, cg // N_CHIPS, cg), (1, 0, 2, 3)).astype(BF16)
        return g.reshape(N_CHIPS, -1, g.shape[-1])

    def gather_ici(self, *names, part=(0, 1)):
        return lambda: _task_gather_ici([self.bufs[n] for n in names], self._store(self.bufs, names), part)

    def gather_d2d(self, *names):
        return lambda: _task_gather_d2d([self.bufs[n] for n in names], self._store(self.bufs, names))

    def sibling_halves(self, *names):
        return lambda: _task_sibling_halves([self._grad(n) for n in names], self._store(self.sib, names))

    def add_halves(self, *names):
        def run():
            parts = _add_halves([self._grad(n) for n in names], [self.sib.pop(n) for n in names], self.idx,
                                f"reduce_add_{names[0]}")
            self.part.update(zip(names, parts))
        return run

    def chip_sums(self, *names, part=(0, 1)):
        def make():
            landed = [self.landed[n] for n in names] if part[0] > 0 else None
            return _task_chip_sums([self.part[n] for n in names], self._store(self.landed, names), landed, part)
        return make

    def sibling_parts(self, *names):
        return lambda: _task_sibling_parts([self.part[n] for n in names], [self.landed[n] for n in names],
                                           self._store(self.sib_parts, names))

    def adamw(self, *names):
        def run():
            for n in names:
                self.updates[n] = _adamw_reduced(self.w[n], self.part.pop(n), self.landed.pop(n), self.sib_parts.pop(n),
                                                 self.mom[n], self.vel[n], self.idx, f"adamw_{n}")
        return run


def _plan(ex):
    s = _Schedule()

    def ffn(i):
        return f"ffn{i}_w_gu", f"ffn{i}_w_down"

    c2, a3 = ("c2_w_pw1", "c2_w_pw2"), ("a3_w_in", "a3_w_out")
    first, second = (0, 2), (1, 2)
    s.host("cast_ffn0_w_gu", ex.gather_ici("a0_w_in", part=first))
    s.host("cast_ffn0_w_down", ex.gather_ici("a0_w_in", part=second))
    s.host("cast_c2_w_pw1", ex.gather_d2d("a0_w_in"))
    gu, down = ffn(0)
    s.host("gather_small", ex.gather_ici("a0_w_out"))
    s.host("a0_in", ex.gather_ici(gu, part=first), ex.gather_d2d("a0_w_out"))
    s.host("a0_conv", ex.gather_ici(gu, part=second))
    s.host("a0_out", ex.gather_ici(down), ex.gather_d2d(gu))
    s.host("ffn0_up", ex.gather_d2d(down))
    gu, down = ffn(1)
    s.host("ffn0_up", ex.gather_ici(gu, GROUPED))
    s.host("ffn0_down", ex.gather_ici(down), ex.gather_d2d(gu, GROUPED))
    s.host("ffn1_up", ex.gather_d2d(down), ex.gather_ici(*c2))
    gu, down = ffn(2)
    s.host("ffn1_up", ex.gather_ici(gu, part=first))
    s.host("ffn1_down", ex.gather_d2d(*c2), ex.gather_ici(down))
    s.host("c2_mid", ex.gather_ici(gu, part=second))
    s.host("c2_pw2", ex.gather_d2d(gu, down), ex.gather_ici(a3[1]))
    s.host("ffn2_up", ex.gather_ici(a3[0]))
    gu, down = ffn(3)
    s.host("ffn2_up", ex.gather_ici(gu, part=first))
    s.host("ffn2_down", ex.gather_d2d(*a3), ex.gather_ici(down))
    s.host("a3_in", ex.gather_ici(gu, part=second))
    s.host("a3_out", ex.gather_d2d(gu, down))

    def reduce_on(names, first_host, ici_hosts, last_host):
        s.host(first_host, ex.sibling_halves(*names))
        s.post(first_host, ex.add_halves(*names))
        for host, hosted, part in ici_hosts:
            s.host(host, ex.chip_sums(*hosted, part=part))
        s.host(last_host, ex.sibling_parts(*names))
        s.post(last_host, ex.adamw(*names))

    whole = (0, 1)
    gu, down = ffn(3)
    reduce_on((gu, down), "a3_out_bwd",
              [("a3_conv_bwd", (down,), whole), ("a3_dw_in", (gu,), first), ("a3_in_bwd", (gu,), second)], "ffn2_down_bwd")
    reduce_on(a3, "ffn2_down_bwd", [("ffn2_dw_gu", a3, whole)], "c2_pw2_bwd")
    gu, down = ffn(0)
    s.host("ffn0_dw_gu", ex.sibling_halves(down))
    s.post("ffn0_dw_gu", ex.add_halves(down))
    s.host("ffn0_up_bwd", ex.chip_sums(down))
    s.host("a0_out_bwd", ex.sibling_halves(gu, GROUPED))
    s.post("a0_out_bwd", ex.add_halves(gu, GROUPED))
    s.host("a0_conv_bwd", ex.chip_sums(gu, part=first), ex.chip_sums(GROUPED))
    s.host("a0_dw_in", ex.chip_sums(gu, part=second))
    s.host("a0_dw_out", ex.sibling_halves("a0_w_in"))
    s.post("a0_dw_out", ex.add_halves("a0_w_in"))
    s.host("a0_in_bwd", ex.chip_sums("a0_w_in"), ex.sibling_parts(gu, down, GROUPED))
    s.host(f"adamw_{gu}", ex.chip_sums("a0_w_out"))
    s.host(f"adamw_{down}", ex.sibling_parts("a0_w_out"))
    reduce_on(ffn(2), "c2_pw2_bwd", [("c2_mid_bwd", ffn(2), whole)], "ffn1_down_bwd")
    reduce_on(c2, "ffn1_down_bwd", [("ffn1_dw_down", c2, whole)], "b1_bwd_mm")
    gu, down = ffn(1)
    reduce_on((gu, down), "b1_bwd_mm", [("ffn0_down_bwd", (down,), whole), ("ffn0_dw_gu", (gu,), whole)], "ffn0_up_bwd")
    return s


def kernel(x, *rest):
    nw = len(WEIGHT_NAMES)
    w = dict(zip(WEIGHT_NAMES, rest[:nw]))
    target = rest[nw]
    mom = dict(zip(WEIGHT_NAMES, rest[nw + 1:2 * nw + 1]))
    vel = dict(zip(WEIGHT_NAMES, rest[2 * nw + 1:3 * nw + 1]))
    cx, cy, cc = _position()
    my_chip = 2 * cx + cy
    ex = _Exchange(w, mom, vel, jnp.stack([my_chip, cc]).astype(jnp.int32))
    _ACTIVE_SCHEDULE[0] = _plan(ex)
    try:
        return _scheduled_step(x, target, w, mom, vel, ex, my_chip)
    finally:
        _ACTIVE_SCHEDULE[0] = None


def _scheduled_step(x, target, w, mom, vel, ex, my_chip):
    d = x.shape[-1]
    cq = d // N_CHIPS
    ex.cast(BIG)

    small_blk, small_counts = _pack_rows([w[n] for n in SMALL_SHARDED], cq)
    small_all = _allgather8(small_blk, "gather_small").reshape(N_CHIPS, 2, small_blk.shape[0], cq)[:, 0]
    small_parts = _unpack_rows(jnp.transpose(small_all, (1, 0, 2)), small_counts,
                               [(w[n].reshape(-1, cq).shape[0], N_CHIPS, cq) for n in SMALL_SHARDED])
    wts = ex.weights
    for n in REPLICATED:
        wts[n] = w[n].reshape(1, -1)
    for n, part in zip(SMALL_SHARDED, small_parts):
        wts[n] = part.reshape(part.shape[0], d)

    loss, dx, g = _device_step(x[0], target[0], wts, ex.grads)

    summed, last = ("ffn0_w_gu", "ffn0_w_down", GROUPED, "a0_w_in"), "a0_w_out"
    _comm_only([ex.sibling_parts("a0_w_in")(), ex.sibling_halves(last)()], "reduce_tail_d2d")
    ex.add_halves(last)()
    ex.adamw(*summed)()
    ex.adamw(last)()
    sched = _ACTIVE_SCHEDULE[0]
    assert not sched.hosts and not sched.posts, (sched.hosts, sched.posts)

    rep_rows = [jnp.pad(g[n].reshape(-1, LANES), ((0, 0), (0, cq - LANES))) for n in REPLICATED]
    by_chip = [jnp.transpose(g[n].reshape(g[n].shape[0], N_CHIPS, cq), (1, 0, 2)) for n in SMALL_SHARDED]
    shard_rows = jnp.concatenate(by_chip, axis=1)
    n_rep, n_shard = sum(r.shape[0] for r in rep_rows), shard_rows.shape[1]
    loss_row = jnp.broadcast_to(loss, (1, cq))
    sm_blk = _pad_rows(jnp.concatenate(rep_rows + [loss_row, shard_rows.reshape(N_CHIPS * n_shard, cq)], axis=0))
    sm_sum = _sum_devices(_allgather8(sm_blk, "gather_small_grads"), "sum_small_grads")
    mine = lax.dynamic_slice_in_dim(sm_sum, n_rep + 1 + my_chip * n_shard, n_shard, axis=0)

    out = ex.updates
    params, at = [], {0: 0, 1: 0}
    for block, names in ((0, REPLICATED), (1, SMALL_SHARDED)):
        for n in names:
            params.append((w[n], mom[n], vel[n], block, at[block]))
            at[block] += w[n].size // LANES if w[n].ndim == 1 else w[n].shape[0]
    out.update(zip(REPLICATED + SMALL_SHARDED, _adamw_small([sm_sum, mine], params, "adamw_small")))

    total = sm_sum[n_rep, 0]
    grads, deltas, new_m, new_v = ([out[n][k].reshape(w[n].shape) for n in WEIGHT_NAMES] for k in range(4))
    return (total, dx.reshape(x.shape), *grads, *deltas, *new_m, *new_v)
```

```python
import functools

import jax
import jax.numpy as jnp
from jax import lax
from jax.experimental import pallas as pl
from jax.experimental.pallas import tpu as pltpu

F32 = jnp.float32
BF16 = jnp.bfloat16

RMS_EPS = 1e-6
LN_EPS = 1e-5
POOL_WINDOWS = (2, 4, 8, 16)
SHORT_CONV_W = 3
CONF_CONV_W = 31
N_CHIPS = 4
N_DEV = 8

ADAM_LR = 0.001
ADAM_B1 = 0.9
ADAM_B2 = 0.999
ADAM_EPS = 1e-08
ADAM_WD = 0.01
ADAM_STEP = 10

V7X_VMEM_BYTES = 64 * 1024 * 1024
VMEM_LIMIT = V7X_VMEM_BYTES - 8 * 1024 * 1024
LANES = 128
POOL_HALO = 16
SCONV_HALO = 16
CONF_HALO = 32


def _params(*sem):
    return pltpu.CompilerParams(dimension_semantics=sem, vmem_limit_bytes=VMEM_LIMIT)


def _tile(n, pref, mult=8):
    t = min(n, pref)
    while t > mult and (n % t or t % mult):
        t -= mult
    assert n % t == 0 and t % mult == 0, (n, pref, mult)
    return t


def _sigmoid(x):
    return jax.nn.sigmoid(x)


def _dot(a, b):
    return jnp.dot(a, b, preferred_element_type=F32)


def _dot_nt(a, b):
    return lax.dot_general(a, b, (((1,), (1,)), ((), ())), preferred_element_type=F32)


def _dot_tn(a, b):
    return lax.dot_general(a, b, (((0,), (0,)), ((), ())), preferred_element_type=F32)


def _colsum(x):
    return jnp.sum(x, axis=0, keepdims=True)


def _rms_stats(x):
    return lax.rsqrt(jnp.mean(x * x, axis=-1, keepdims=True) + RMS_EPS)


def _rms_bwd(du, x, gain):
    r = _rms_stats(x)
    xhat = x * r
    gdy = du * gain
    dx = r * (gdy - xhat * jnp.mean(gdy * xhat, axis=-1, keepdims=True))
    return dx, _colsum(du * xhat)


class _Task:
    def __init__(self, ins, out_shapes, aliases, sems, start, wait, done):
        self.ins, self.out_shapes, self.aliases, self.sems = list(ins), list(out_shapes), dict(aliases), list(sems)
        self.start, self.wait, self.done = start, wait, done


class _Schedule:
    def __init__(self):
        self.hosts, self.posts = {}, {}

    def host(self, kernel_name, *make_tasks):
        self.hosts.setdefault(kernel_name, []).extend(make_tasks)

    def post(self, kernel_name, *thunks):
        self.posts.setdefault(kernel_name, []).extend(thunks)

    def tasks_for(self, kernel_name):
        return [make() for make in self.hosts.pop(kernel_name, ())]

    def finished(self, kernel_name):
        for thunk in self.posts.pop(kernel_name, ()):
            thunk()


_ACTIVE_SCHEDULE = [None]


def _hosted(body, name, **kw):
    def run(*args):
        sched = _ACTIVE_SCHEDULE[0]
        tasks = sched.tasks_for(name) if sched is not None else []
        out = _call_with_tasks(body, name, tasks, kw, args) if tasks else pl.pallas_call(body, name=name, **kw)(*args)
        if sched is not None:
            sched.finished(name)
        return out

    return run


def _call_with_tasks(body, name, tasks, kw, args):
    spec = kw.get("grid_spec")
    n_pre = spec.num_scalar_prefetch if spec is not None else 0
    src = dict(grid=spec.grid, in_specs=spec.in_specs, out_specs=spec.out_specs) if spec is not None else kw
    pre, args = args[:n_pre], args[n_pre:]
    grid = tuple(src.get("grid", ()))
    single = not isinstance(kw["out_shape"], (list, tuple))
    out_shape = [kw["out_shape"]] if single else list(kw["out_shape"])
    out_specs = [src["out_specs"]] if single else list(src["out_specs"])
    scratch = list(kw.get("scratch_shapes", ()))
    n_in, n_out, n_scr = len(args), len(out_shape), len(scratch)
    t_in = [a for t in tasks for a in t.ins]
    t_out = [o for t in tasks for o in t.out_shapes]
    t_sem = [s for t in tasks for s in t.sems]
    aliases, at_in, at_out = {}, n_pre + n_in, n_out
    for t in tasks:
        for i, o in t.aliases.items():
            aliases[at_in + i] = at_out + o
        at_in += len(t.ins)
        at_out += len(t.out_shapes)

    def wrapped(*refs):
        pre_refs, refs = refs[:n_pre], refs[n_pre:]
        a = n_in
        b = a + len(t_in)
        c = b + n_out
        d = c + len(t_out)
        e = d + n_scr
        ins, tins, outs, touts, scr, tsems = refs[:a], refs[a:b], refs[b:c], refs[c:d], refs[d:e], refs[e:]
        views, i0, o0, s0 = [], 0, 0, 0
        for t in tasks:
            views.append((tins[i0:i0 + len(t.ins)], touts[o0:o0 + len(t.out_shapes)], tsems[s0:s0 + len(t.sems)]))
            i0, o0, s0 = i0 + len(t.ins), o0 + len(t.out_shapes), s0 + len(t.sems)

        def start_all():
            for t, v in zip(tasks, views):
                t.start(*v)

        def wait_all():
            for t, v in zip(tasks, views):
                t.wait(*v)

        if grid:
            first = functools.reduce(jnp.logical_and, [pl.program_id(i) == 0 for i in range(len(grid))])
            last = functools.reduce(jnp.logical_and, [pl.program_id(i) == grid[i] - 1 for i in range(len(grid))])
            pl.when(first)(start_all)
            body(*pre_refs, *ins, *outs, *scr)
            pl.when(last)(wait_all)
        else:
            start_all()
            body(*pre_refs, *ins, *outs, *scr)
            wait_all()

    in_specs = list(src["in_specs"]) + [ANY] * len(t_in)
    out_specs = out_specs + [ANY] * len(t_out)
    if spec is not None:
        layout = dict(grid_spec=pltpu.PrefetchScalarGridSpec(
            num_scalar_prefetch=n_pre, grid=grid, in_specs=in_specs, out_specs=out_specs, scratch_shapes=scratch + t_sem))
    else:
        layout = dict(grid=grid, in_specs=in_specs, out_specs=out_specs, scratch_shapes=scratch + t_sem)
    res = pl.pallas_call(
        wrapped, name=name, out_shape=out_shape + t_out, input_output_aliases=aliases,
        compiler_params=pltpu.CompilerParams(dimension_semantics=("arbitrary",) * len(grid), vmem_limit_bytes=VMEM_LIMIT),
        **layout,
    )(*pre, *args, *t_in)
    res = list(res)
    own, rest = res[:n_out], res[n_out:]
    for t in tasks:
        t.done(rest[:len(t.out_shapes)])
        rest = rest[len(t.out_shapes):]
    return own[0] if single else own


def _comm_only(tasks, name):
    _call_with_tasks(lambda: None, name, tasks, dict(grid=(), in_specs=[], out_specs=[], out_shape=[]), ())


def _mm_col(h, gain, w, bias, name):
    s, k = h.shape
    nsh, _, ns = w.shape
    tm = _tile(s, 1024)
    has_bias = bias is not None

    def body(h_ref, gain_ref, w_ref, *rest):
        u_ref, o_ref = rest[-2:]
        x = h_ref[...]
        x = (x * _rms_stats(x) * gain_ref[...]).astype(BF16)
        u_ref[...] = x
        for j in range(nsh):
            cols = pl.ds(j * ns, ns)
            acc = _dot(x, w_ref[j])
            if has_bias:
                acc = acc + rest[0][:, cols]
            o_ref[:, cols] = acc.astype(o_ref.dtype)

    tokens = pl.BlockSpec((tm, k), lambda m: (m, 0))
    in_specs = [tokens, pl.BlockSpec((1, k), lambda m: (0, 0)), pl.BlockSpec((nsh, k, ns), lambda m: (0, 0, 0))]
    args = [h, gain, w]
    if has_bias:
        in_specs.append(pl.BlockSpec((1, nsh * ns), lambda m: (0, 0)))
        args.append(bias)
    return _hosted(
        body, name=name, grid=(s // tm,), in_specs=in_specs,
        out_specs=[tokens, pl.BlockSpec((tm, nsh * ns), lambda m: (m, 0))],
        out_shape=[jax.ShapeDtypeStruct((s, k), BF16), jax.ShapeDtypeStruct((s, nsh * ns), BF16)],
        compiler_params=_params("parallel"),
    )(*args)


def _mm_row(a, w, res, bias, name):
    s = a.shape[0]
    k, n = w.shape
    tm = _tile(s, 1024)
    has_bias = bias is not None

    def body(a_ref, w_ref, res_ref, *rest):
        o_ref = rest[-1]
        y = res_ref[...] + _dot(a_ref[...], w_ref[...])
        if has_bias:
            y = y + rest[0][...]
        o_ref[...] = y

    in_specs = [pl.BlockSpec((tm, k), lambda m: (m, 0)), pl.BlockSpec((k, n), lambda m: (0, 0)),
                pl.BlockSpec((tm, n), lambda m: (m, 0))]
    args = [a, w, res]
    if has_bias:
        in_specs.append(pl.BlockSpec((1, n), lambda m: (0, 0)))
        args.append(bias)
    return _hosted(
        body, name=name, grid=(s // tm,), in_specs=in_specs,
        out_specs=pl.BlockSpec((tm, n), lambda m: (m, 0)),
        out_shape=jax.ShapeDtypeStruct((s, n), F32),
        compiler_params=_params("parallel"),
    )(*args)


def _mm_nt_row(dy, w, name):
    s, n = dy.shape
    k = w.shape[0]
    tm = _tile(s, 512)

    def body(dy_ref, w_ref, o_ref):
        o_ref[...] = _dot_nt(dy_ref[...].astype(BF16), w_ref[...])

    return _hosted(
        body, name=name, grid=(s // tm,),
        in_specs=[pl.BlockSpec((tm, n), lambda m: (m, 0)), pl.BlockSpec((k, n), lambda m: (0, 0))],
        out_specs=pl.BlockSpec((tm, k), lambda m: (m, 0)),
        out_shape=jax.ShapeDtypeStruct((s, k), F32),
        compiler_params=_params("parallel"),
    )(dy, w)


def _ffn_up(h, gain, w, name):
    s, d = h.shape
    _, _, ns = w.shape
    tm = _tile(s, 512)

    def body(h_ref, gain_ref, wg_ref, wu_ref, u_ref, act_ref, s1_ref, q1_ref):
        x = h_ref[...]
        x = (x * _rms_stats(x) * gain_ref[...]).astype(BF16)

        @pl.when(pl.program_id(0) == 0)
        def _():
            u_ref[...] = x

        g = _dot(x, wg_ref[...])
        up = _dot(x, wu_ref[...])
        sg = _sigmoid(g)
        s1 = g * sg
        act_ref[...] = (s1 * up).astype(act_ref.dtype)
        s1_ref[...] = s1.astype(s1_ref.dtype)
        q1_ref[...] = (up * sg * (1.0 + g * (1.0 - sg))).astype(q1_ref.dtype)

    out = pl.BlockSpec((tm, ns), lambda j, m: (m, j))
    tokens = pl.BlockSpec((tm, d), lambda j, m: (m, 0))
    nm = s // tm
    u_once = pl.BlockSpec((tm, d), lambda j, m: (jnp.where(j == 0, m, nm - 1), 0))
    return _hosted(
        body, name=name, grid=(2, nm),
        in_specs=[tokens, pl.BlockSpec((1, d), lambda j, m: (0, 0)), pl.BlockSpec((None, d, ns), lambda j, m: (j, 0, 0)),
                  pl.BlockSpec((None, d, ns), lambda j, m: (j + 2, 0, 0))],
        out_specs=[u_once, out, out, out],
        out_shape=[jax.ShapeDtypeStruct((s, d), BF16)] + [jax.ShapeDtypeStruct((s, 2 * ns), BF16)] * 3,
        compiler_params=_params("arbitrary", "arbitrary"),
    )(h, gain, w, w)


def _ffn_down_bwd(dh, w, s1, q1, name):
    s, d = dh.shape
    f = w.shape[0]
    tm = _tile(s, 512)

    def body(dh_ref, w_ref, s1_ref, q1_ref, o_ref):
        da = _dot_nt(dh_ref[...].astype(BF16), w_ref[...])
        o_ref[:, :f] = (da * q1_ref[...].astype(F32)).astype(o_ref.dtype)
        o_ref[:, f:] = (da * s1_ref[...].astype(F32)).astype(o_ref.dtype)

    return _hosted(
        body, name=name, grid=(s // tm,),
        in_specs=[pl.BlockSpec((tm, d), lambda m: (m, 0)), pl.BlockSpec((f, d), lambda m: (0, 0)),
                  pl.BlockSpec((tm, f), lambda m: (m, 0)), pl.BlockSpec((tm, f), lambda m: (m, 0))],
        out_specs=pl.BlockSpec((tm, 2 * f), lambda m: (m, 0)),
        out_shape=jax.ShapeDtypeStruct((s, 2 * f), BF16),
        compiler_params=_params("parallel"),
    )(dh, w, s1, q1)


def _mm_nt_col_rms_bwd(dy, w, h, gain, dh, name):
    s = dy.shape[0]
    nsh, k, ns = w.shape
    tm = _tile(s, 512)

    def body(dy_ref, w_ref, h_ref, g_ref, dh_ref, o_ref, dg_ref):
        du = _dot_nt(dy_ref[:, :ns], w_ref[0])
        for j in range(1, nsh):
            du = du + _dot_nt(dy_ref[:, j * ns:(j + 1) * ns], w_ref[j])
        dx, dg = _rms_bwd(du, h_ref[...], g_ref[...])
        o_ref[...] = dh_ref[...] + dx
        _accumulate(dg_ref, dg, pl.program_id(0) == 0)

    return _hosted(
        body, name=name, grid=(s // tm,),
        in_specs=[pl.BlockSpec((tm, nsh * ns), lambda m: (m, 0)), pl.BlockSpec((nsh, k, ns), lambda m: (0, 0, 0)),
                  pl.BlockSpec((tm, k), lambda m: (m, 0)), pl.BlockSpec((1, k), lambda m: (0, 0)),
                  pl.BlockSpec((tm, k), lambda m: (m, 0))],
        out_specs=[pl.BlockSpec((tm, k), lambda m: (m, 0)), pl.BlockSpec((1, k), lambda m: (0, 0))],
        out_shape=[jax.ShapeDtypeStruct((s, k), F32), jax.ShapeDtypeStruct((1, k), F32)],
        compiler_params=_params("arbitrary"),
    )(dy, w, h, gain, dh)


def _mm_tn(a, dy, nsh, name):
    s, k = a.shape
    ns = dy.shape[1] // nsh
    tm = _tile(s, 2048)
    tk = _tile(k, 1408, LANES)
    nk, nm = k // tk, s // tm

    def body(a_ref, dy_ref, o_ref, acc_ref):
        m = pl.program_id(2)
        part = _dot_tn(a_ref[...], dy_ref[...].astype(BF16))

        @pl.when(m == 0)
        def _():
            acc_ref[...] = part

        @pl.when(m > 0)
        def _():
            acc_ref[...] += part

        @pl.when(m == nm - 1)
        def _():
            o_ref[...] = acc_ref[...].astype(o_ref.dtype)

    return _hosted(
        body, name=name, grid=(nsh, nk, nm),
        in_specs=[pl.BlockSpec((tm, tk), lambda j, kk, m: (m, kk)), pl.BlockSpec((tm, ns), lambda j, kk, m: (m, j))],
        out_specs=pl.BlockSpec((None, tk, ns), lambda j, kk, m: (j, kk, 0)),
        out_shape=jax.ShapeDtypeStruct((nsh, k, ns), BF16),
        scratch_shapes=[pltpu.VMEM((tk, ns), F32)],
        compiler_params=_params("parallel", "parallel", "arbitrary"),
    )(a, dy)


def _main_spec(tm, w):
    return pl.BlockSpec((tm, w), lambda m: (m, 0))


def _before_spec(tm, hb, w):
    return pl.BlockSpec((hb, w), lambda m: (jnp.maximum(m * (tm // hb) - 1, 0), 0))


def _after_spec(tm, hb, w, s):
    return pl.BlockSpec((hb, w), lambda m: (jnp.minimum((m + 1) * (tm // hb), s // hb - 1), 0))


def _row_spec(w, rows=1):
    return pl.BlockSpec((rows, w), lambda m: (0, 0))


CHUNK_LANES = 4 * LANES
CHUNK_ROWS = 32


def _build_shifts(ext8_ref, residues=range(1, 8)):
    n = ext8_ref.shape[1] - 8
    for r in residues:
        ext8_ref[r, pl.ds(0, n), :] = ext8_ref[0, pl.ds(r, n), :]


def _fold_rows(x):
    return functools.reduce(lambda p, q: p + q, [x[i:i + 8] for i in range(0, x.shape[0], 8)])


def _shifted(ext8_ref, shift, r0, rows, cols):
    return ext8_ref[shift % 8, pl.ds(pl.multiple_of(shift - shift % 8 + r0, 8), rows), cols]


def _lane_chunk(i):
    return pl.ds(pl.multiple_of(i * CHUNK_LANES, CHUNK_LANES), CHUNK_LANES)


def _sum_terms(terms, ways=4):
    accs = []
    for i, t in enumerate(terms):
        if i < ways:
            accs.append(t)
        else:
            accs[i % ways] = accs[i % ways] + t
    while len(accs) > 1:
        accs = [accs[i] + accs[i + 1] if i + 1 < len(accs) else accs[i] for i in range(0, len(accs), 2)]
    return accs[0]


def _accumulate(ref, val, first):
    @pl.when(first)
    def _():
        ref[...] = val

    @pl.when(jnp.logical_not(first))
    def _():
        ref[...] += val


SCONV_Z_SHIFTS = tuple(SCONV_HALO - (SHORT_CONV_W - 1) + k for k in range(SHORT_CONV_W))


def _sconv_z_taps(zext_ref, r0, cols):
    return [_shifted(zext_ref, shift, r0, CHUNK_ROWS, cols) for shift in SCONV_Z_SHIFTS]


def _weighted(cw_ref, cols, terms):
    return _sum_terms((cw_ref[k:k + 1, cols] * t for k, t in enumerate(terms)), ways=len(terms))


def _sconv_fill_z(zext_ref, main_ref, before_ref, d, m):
    hb = SCONV_HALO
    zb = before_ref[:, d:2 * d].astype(F32) * before_ref[:, 2 * d:].astype(F32)
    zext_ref[pl.ds(0, hb), :] = jnp.where(m > 0, zb, 0.0)
    zext_ref[pl.ds(hb, main_ref.shape[0]), :] = main_ref[:, d:2 * d].astype(F32) * main_ref[:, 2 * d:].astype(F32)


def _sconv_fwd(bcv, cw, name):
    s, d3 = bcv.shape
    d = d3 // 3
    tm = _tile(s, 256, CHUNK_ROWS)
    row_chunks = tm // CHUNK_ROWS

    def body(main_ref, before_ref, cw_ref, p_ref, zext_ref):
        m = pl.program_id(0)
        _sconv_fill_z(zext_ref.at[0], main_ref, before_ref, d, m)
        _build_shifts(zext_ref, [shift % 8 for shift in SCONV_Z_SHIFTS if shift % 8])

        def chunk(i, carry):
            cols = _lane_chunk(i // row_chunks)
            r0 = pl.multiple_of((i % row_chunks) * CHUNK_ROWS, CHUNK_ROWS)
            rows = pl.ds(r0, CHUNK_ROWS)
            zc = _weighted(cw_ref, cols, _sconv_z_taps(zext_ref, r0, cols))
            p_ref[rows, cols] = (main_ref[rows, cols].astype(F32) * zc).astype(p_ref.dtype)
            return carry

        lax.fori_loop(0, row_chunks * (d // CHUNK_LANES), chunk, 0)

    return _hosted(
        body, name=name, grid=(s // tm,),
        in_specs=[_main_spec(tm, d3), _before_spec(tm, SCONV_HALO, d3), _row_spec(d, SHORT_CONV_W)],
        out_specs=_main_spec(tm, d),
        out_shape=jax.ShapeDtypeStruct((s, d), BF16),
        scratch_shapes=[pltpu.VMEM((8, tm + SCONV_HALO, d), F32)],
        compiler_params=_params("parallel"),
    )(bcv, bcv, cw)


def _sconv_bwd(dp, bcv, cw, name):
    s, d3 = bcv.shape
    d = d3 // 3
    tm = _tile(s, 256, CHUNK_ROWS)
    nm = s // tm
    ha = 8
    kw = SHORT_CONV_W

    def body(dp_ref, dpa_ref, main_ref, before_ref, after_ref, cw_ref, o_ref, dcw_ref, zext_ref, dext_ref):
        m = pl.program_id(0)
        _sconv_fill_z(zext_ref.at[0], main_ref, before_ref, d, m)
        _build_shifts(zext_ref, [shift % 8 for shift in SCONV_Z_SHIFTS if shift % 8])
        dext_ref[0, pl.ds(0, tm), :] = dp_ref[...] * main_ref[:, :d].astype(F32)
        dza = dpa_ref[...] * after_ref[:, :d].astype(F32)[0:ha]
        dext_ref[0, pl.ds(tm, ha), :] = jnp.where(m < nm - 1, dza, 0.0)
        _build_shifts(dext_ref, range(1, kw))

        @pl.when(m == 0)
        def _():
            dcw_ref[...] = jnp.zeros_like(dcw_ref)

        zero = jnp.zeros((8, CHUNK_LANES), F32)

        def lane_chunk(ci, carry):
            cols = _lane_chunk(ci)
            c_cols, v_cols = (pl.ds(pl.multiple_of(part * d + ci * CHUNK_LANES, CHUNK_LANES), CHUNK_LANES) for part in (1, 2))

            def row_chunk(ri, sums):
                r0 = pl.multiple_of(ri * CHUNK_ROWS, CHUNK_ROWS)
                rows = pl.ds(r0, CHUNK_ROWS)
                z = _sconv_z_taps(zext_ref, r0, cols)
                o_ref[rows, cols] = (dp_ref[rows, cols] * _weighted(cw_ref, cols, z)).astype(o_ref.dtype)
                dzc = [_shifted(dext_ref, kw - 1 - k, r0, CHUNK_ROWS, cols) for k in range(kw)]
                dz = _weighted(cw_ref, cols, dzc)
                o_ref[rows, c_cols] = (dz * main_ref[rows, v_cols].astype(F32)).astype(o_ref.dtype)
                o_ref[rows, v_cols] = (dz * main_ref[rows, c_cols].astype(F32)).astype(o_ref.dtype)
                return tuple(acc + _fold_rows(dzc[kw - 1] * z[k]) for k, acc in enumerate(sums))

            sums = lax.fori_loop(0, tm // CHUNK_ROWS, row_chunk, (zero,) * kw)
            for k in range(kw):
                dcw_ref[k:k + 1, cols] += _colsum(sums[k])
            return carry

        lax.fori_loop(0, d // CHUNK_LANES, lane_chunk, 0)

    return _hosted(
        body, name=name, grid=(nm,),
        in_specs=[_main_spec(tm, d), _after_spec(tm, ha, d, s), _main_spec(tm, d3), _before_spec(tm, SCONV_HALO, d3),
                  _after_spec(tm, SCONV_HALO, d3, s), _row_spec(d, SHORT_CONV_W)],
        out_specs=[_main_spec(tm, d3), _row_spec(d, 8)],
        out_shape=[jax.ShapeDtypeStruct((s, d3), BF16), jax.ShapeDtypeStruct((8, d), F32)],
        scratch_shapes=[pltpu.VMEM((8, tm + SCONV_HALO, d), F32), pltpu.VMEM((8, tm + ha, d), F32)],
        compiler_params=_params("arbitrary"),
    )(dp, dp, bcv, bcv, bcv, cw)


def _pool_counts(t0, tm, w):
    t = t0 + lax.broadcasted_iota(jnp.int32, (tm, 1), 0)
    return jnp.minimum(t + 1, w).astype(F32)


def _pool_fwd(h, gain, wg, scale, name):
    s, d = h.shape
    ng, cg, _ = wg.shape
    tm = _tile(s, 512, POOL_HALO)

    def body(h_ref, hb_ref, g_ref, wg_ref, sc_ref, o_ref, mx_ref, uext_ref):
        m = pl.program_id(0)
        x = h_ref[...]
        gain_row = g_ref[...]
        xb = hb_ref[...]
        uext_ref[pl.ds(0, POOL_HALO), :] = jnp.where(m > 0, xb * _rms_stats(xb) * gain_row, 0.0)
        uext_ref[pl.ds(POOL_HALO, tm), :] = x * _rms_stats(x) * gain_row
        for gi, win in enumerate(POOL_WINDOWS):
            cols = pl.ds(gi * cg, cg)
            u_g = uext_ref[pl.ds(POOL_HALO, tm), cols]
            acc = u_g
            for i in range(1, win):
                acc = acc + uext_ref[pl.ds(POOL_HALO - i, tm), cols]
            mixed = (acc / _pool_counts(m * tm, tm, win) - u_g).astype(BF16)
            mx_ref[:, cols] = mixed
            o_ref[:, cols] = x[:, gi * cg:(gi + 1) * cg] + _dot(mixed, wg_ref[gi]) * sc_ref[:, cols]

    return _hosted(
        body, name=name, grid=(s // tm,),
        in_specs=[_main_spec(tm, d), _before_spec(tm, POOL_HALO, d), _row_spec(d),
                  pl.BlockSpec((ng, cg, cg), lambda m: (0, 0, 0)), _row_spec(d)],
        out_specs=[_main_spec(tm, d), _main_spec(tm, d)],
        out_shape=[jax.ShapeDtypeStruct((s, d), F32), jax.ShapeDtypeStruct((s, d), BF16)],
        scratch_shapes=[pltpu.VMEM((tm + POOL_HALO, d), F32)],
        compiler_params=_params("parallel"),
    )(h, h, gain, wg, scale)


def _pool_bwd_mm(dh, mixed, wg, scale, name):
    s, d = dh.shape
    ng, cg, _ = wg.shape
    tm = _tile(s, 512)

    def body(dh_ref, mx_ref, wg_ref, sc_ref, dmx_ref, dwg_ref, dsc_ref):
        first = pl.program_id(0) == 0
        for gi in range(ng):
            cols = pl.ds(gi * cg, cg)
            dh_g = dh_ref[:, cols]
            mixed = mx_ref[:, cols]
            w_g = wg_ref[gi]
            dy = (dh_g * sc_ref[:, cols]).astype(BF16)
            dmx_ref[:, cols] = _dot_nt(dy, w_g)
            _accumulate(dsc_ref.at[:, cols], _colsum(dh_g * _dot(mixed, w_g)), first)
            _accumulate(dwg_ref.at[gi], _dot_tn(mixed, dy), first)

    return _hosted(
        body, name=name, grid=(s // tm,),
        in_specs=[_main_spec(tm, d), _main_spec(tm, d), pl.BlockSpec((ng, cg, cg), lambda m: (0, 0, 0)), _row_spec(d)],
        out_specs=[_main_spec(tm, d), pl.BlockSpec((ng, cg, cg), lambda m: (0, 0, 0)), _row_spec(d)],
        out_shape=[jax.ShapeDtypeStruct((s, d), F32), jax.ShapeDtypeStruct((ng, cg, cg), F32),
                   jax.ShapeDtypeStruct((1, d), F32)],
        compiler_params=_params("arbitrary"),
    )(dh, mixed, wg, scale)


def _pool_bwd_rms(dmixed, h, gain, dh, name):
    s, d = h.shape
    cg = d // len(POOL_WINDOWS)
    tm = _tile(s, 512, POOL_HALO)
    nm = s // tm

    def body(dmx_ref, dmxa_ref, h_ref, g_ref, dh_ref, o_ref, dg_ref, eext_ref, du_ref):
        m = pl.program_id(0)
        for gi, win in enumerate(POOL_WINDOWS):
            cols = pl.ds(gi * cg, cg)
            dmx = dmx_ref[:, cols]
            eext_ref[pl.ds(0, tm), cols] = dmx / _pool_counts(m * tm, tm, win)
            ea = dmxa_ref[:, cols] / _pool_counts((m + 1) * tm, POOL_HALO, win)
            eext_ref[pl.ds(tm, POOL_HALO), cols] = jnp.where(m < nm - 1, ea, 0.0)
            acc = -dmx
            for i in range(win):
                acc = acc + eext_ref[pl.ds(i, tm), cols]
            du_ref[:, cols] = acc
        dx, dg = _rms_bwd(du_ref[...], h_ref[...], g_ref[...])
        o_ref[...] = dh_ref[...] + dx
        _accumulate(dg_ref, dg, m == 0)

    return _hosted(
        body, name=name, grid=(nm,),
        in_specs=[_main_spec(tm, d), _after_spec(tm, POOL_HALO, d, s), _main_spec(tm, d), _row_spec(d), _main_spec(tm, d)],
        out_specs=[_main_spec(tm, d), _row_spec(d)],
        out_shape=[jax.ShapeDtypeStruct((s, d), F32), jax.ShapeDtypeStruct((1, d), F32)],
        scratch_shapes=[pltpu.VMEM((tm + POOL_HALO, d), F32), pltpu.VMEM((tm, d), F32)],
        compiler_params=_params("arbitrary"),
    )(dmixed, dmixed, h, gain, dh)


def _conf_fill_h(hext_ref, main_ref, before_ref, d, m):
    hb = before_ref[:, :d].astype(F32) * _sigmoid(before_ref[:, d:].astype(F32))
    hext_ref[pl.ds(0, CONF_HALO), :] = jnp.where(m > 0, hb, 0.0)
    hext_ref[pl.ds(CONF_HALO, main_ref.shape[0]), :] = main_ref[:, :d].astype(F32) * _sigmoid(main_ref[:, d:].astype(F32))


def _layernorm_parts(hc, g, b):
    mu = jnp.mean(hc, axis=-1, keepdims=True)
    xc = hc - mu
    rs = lax.rsqrt(jnp.mean(xc * xc, axis=-1, keepdims=True) + LN_EPS)
    xhat = xc * rs
    return xhat, rs, xhat * g + b


def _conf_mid_fwd(ag, dw, b_dw, ln_g, ln_b, name):
    s, d2 = ag.shape
    d = d2 // 2
    tm = _tile(s, 256, CONF_HALO)
    base = CONF_HALO - (CONF_CONV_W - 1)

    def body(main_ref, before_ref, dw_ref, bdw_ref, g_ref, b_ref, s_ref, hc_ref, hext_ref):
        m = pl.program_id(0)
        _conf_fill_h(hext_ref.at[0], main_ref, before_ref, d, m)
        _build_shifts(hext_ref)
        row_chunks = tm // CHUNK_ROWS

        def conv_chunk(i, carry):
            cols = _lane_chunk(i // row_chunks)
            r0 = pl.multiple_of((i % row_chunks) * CHUNK_ROWS, CHUNK_ROWS)
            taps = (dw_ref[kk:kk + 1, cols] * _shifted(hext_ref, base + kk, r0, CHUNK_ROWS, cols) for kk in range(CONF_CONV_W))
            hc_ref[pl.ds(r0, CHUNK_ROWS), cols] = bdw_ref[:, cols] + _sum_terms(taps, ways=1)
            return carry

        lax.fori_loop(0, row_chunks * (d // CHUNK_LANES), conv_chunk, 0)
        _, _, l = _layernorm_parts(hc_ref[...], g_ref[...], b_ref[...])
        s_ref[...] = (l * _sigmoid(l)).astype(s_ref.dtype)

    return _hosted(
        body, name=name, grid=(s // tm,),
        in_specs=[_main_spec(tm, d2), _before_spec(tm, CONF_HALO, d2), _row_spec(d, CONF_CONV_W), _row_spec(d),
                  _row_spec(d), _row_spec(d)],
        out_specs=[_main_spec(tm, d), _main_spec(tm, d)],
        out_shape=[jax.ShapeDtypeStruct((s, d), BF16), jax.ShapeDtypeStruct((s, d), F32)],
        scratch_shapes=[pltpu.VMEM((8, tm + CONF_HALO, d), F32)],
        compiler_params=_params("parallel"),
    )(ag, ag, dw, b_dw, ln_g, ln_b)


def _conf_out_bwd(dh, w, hc, ln_g, ln_b, name):
    s, d = dh.shape
    tm = _tile(s, 256)

    def body(dh_ref, w_ref, hc_ref, g_ref, b_ref, o_ref, dg_ref, db_ref, dbo_ref):
        first = pl.program_id(0) == 0
        dh_t = dh_ref[...]
        ds = _dot_nt(dh_t.astype(BF16), w_ref[...])
        xhat, rs, l = _layernorm_parts(hc_ref[...], g_ref[...], b_ref[...])
        sg = _sigmoid(l)
        dl = ds * sg * (1.0 + l * (1.0 - sg))
        dxh = dl * g_ref[...]
        o_ref[...] = rs * (dxh - jnp.mean(dxh, axis=-1, keepdims=True)
                           - xhat * jnp.mean(dxh * xhat, axis=-1, keepdims=True))
        _accumulate(dg_ref, _colsum(dl * xhat), first)
        _accumulate(db_ref, _colsum(dl), first)
        _accumulate(dbo_ref, _colsum(dh_t), first)

    return _hosted(
        body, name=name, grid=(s // tm,),
        in_specs=[_main_spec(tm, d), pl.BlockSpec((d, d), lambda m: (0, 0)), _main_spec(tm, d), _row_spec(d), _row_spec(d)],
        out_specs=[_main_spec(tm, d), _row_spec(d), _row_spec(d), _row_spec(d)],
        out_shape=[jax.ShapeDtypeStruct((s, d), F32)] + [jax.ShapeDtypeStruct((1, d), F32)] * 3,
        compiler_params=_params("arbitrary"),
    )(dh, w, hc, ln_g, ln_b)


def _conf_mid_bwd(dhc, ag, dw, name):
    s, d2 = ag.shape
    d = d2 // 2
    tm = _tile(s, 256, CONF_HALO)
    nm = s // tm
    kw = CONF_CONV_W
    base = CONF_HALO - (kw - 1)

    def body(dhc_ref, dhca_ref, main_ref, before_ref, dw_ref, o_ref, ddw_ref, dbdw_ref, dbpw_ref, hext_ref, dext_ref):
        m = pl.program_id(0)
        first = m == 0
        _conf_fill_h(hext_ref.at[0], main_ref, before_ref, d, m)
        _build_shifts(hext_ref)
        dext_ref[0, pl.ds(0, tm), :] = dhc_ref[...]
        dext_ref[0, pl.ds(tm, CONF_HALO), :] = jnp.where(m < nm - 1, dhca_ref[...], 0.0)
        _build_shifts(dext_ref)

        @pl.when(first)
        def _():
            ddw_ref[...] = jnp.zeros_like(ddw_ref)
            dbdw_ref[...] = jnp.zeros_like(dbdw_ref)
            dbpw_ref[...] = jnp.zeros_like(dbpw_ref)

        zero = jnp.zeros((8, CHUNK_LANES), F32)
        tap_group = 8

        def fold(x):
            return functools.reduce(lambda p, q: p + q, [x[i:i + 8] for i in range(0, CHUNK_ROWS, 8)])

        def lane_chunk(ci, carry):
            cols = _lane_chunk(ci)
            gate_cols = pl.ds(pl.multiple_of(d + ci * CHUNK_LANES, CHUNK_LANES), CHUNK_LANES)

            def through_conv(ri, sums):
                r0 = pl.multiple_of(ri * CHUNK_ROWS, CHUNK_ROWS)
                rows = pl.ds(r0, CHUNK_ROWS)
                dhh = _sum_terms((dw_ref[kk:kk + 1, cols] * _shifted(dext_ref, kw - 1 - kk, r0, CHUNK_ROWS, cols)
                                  for kk in range(kw)), ways=1)
                a = main_ref[rows, cols].astype(F32)
                sg = _sigmoid(main_ref[rows, gate_cols].astype(F32))
                da = dhh * sg
                dgate = dhh * a * sg * (1.0 - sg)
                o_ref[rows, cols] = da.astype(o_ref.dtype)
                o_ref[rows, gate_cols] = dgate.astype(o_ref.dtype)
                return sums[0] + fold(da), sums[1] + fold(dgate), sums[2] + fold(dext_ref[0, rows, cols])

            sum_da, sum_dgate, sum_dhc = lax.fori_loop(0, tm // CHUNK_ROWS, through_conv, (zero, zero, zero))
            dbdw_ref[:, cols] += _colsum(sum_dhc)
            dbpw_ref[:, cols] += _colsum(sum_da)
            dbpw_ref[:, gate_cols] += _colsum(sum_dgate)

            for k0 in range(0, kw, tap_group):
                group = range(k0, min(k0 + tap_group, kw))

                def tap_gradients(ri, accs, group=group):
                    for sub in range(0, CHUNK_ROWS, 8):
                        r0 = pl.multiple_of(ri * CHUNK_ROWS + sub, 8)
                        dhc_c = dext_ref[0, pl.ds(r0, 8), cols]
                        accs = tuple(acc + dhc_c * _shifted(hext_ref, base + kk, r0, 8, cols) for kk, acc in zip(group, accs))
                    return accs

                accs = lax.fori_loop(0, tm // CHUNK_ROWS, tap_gradients, (zero,) * len(group))
                for kk, acc in zip(group, accs):
                    ddw_ref[kk:kk + 1, cols] += _colsum(acc)
            return carry

        lax.fori_loop(0, d // CHUNK_LANES, lane_chunk, 0)

    return _hosted(
        body, name=name, grid=(nm,),
        in_specs=[_main_spec(tm, d), _after_spec(tm, CONF_HALO, d, s), _main_spec(tm, d2), _before_spec(tm, CONF_HALO, d2),
                  _row_spec(d, kw)],
        out_specs=[_main_spec(tm, d2), _row_spec(d, 32), _row_spec(d), _row_spec(d2)],
        out_shape=[jax.ShapeDtypeStruct((s, d2), BF16), jax.ShapeDtypeStruct((32, d), F32),
                   jax.ShapeDtypeStruct((1, d), F32), jax.ShapeDtypeStruct((1, d2), F32)],
        scratch_shapes=[pltpu.VMEM((8, tm + CONF_HALO, d), F32), pltpu.VMEM((8, tm + CONF_HALO, d), F32)],
        compiler_params=_params("arbitrary"),
    )(dhc, dhc, ag, ag, dw)


def _loss_head(h, gain, target, name):
    s, d = h.shape
    tm = _tile(s, 512)

    def body(h_ref, g_ref, t_ref, loss_ref, dh_ref, dg_ref):
        first = pl.program_id(0) == 0
        x = h_ref[...]
        err = x * _rms_stats(x) * g_ref[...] - t_ref[...]
        part = 0.5 * jnp.sum(jnp.mean(err * err, axis=-1, keepdims=True), axis=0, keepdims=True)
        dx, dg = _rms_bwd(err * (1.0 / d), x, g_ref[...])
        dh_ref[...] = dx
        _accumulate(loss_ref, part, first)
        _accumulate(dg_ref, dg, first)

    return _hosted(
        body, name=name, grid=(s // tm,),
        in_specs=[_main_spec(tm, d), _row_spec(d), _main_spec(tm, d)],
        out_specs=[pl.BlockSpec((1, 1), lambda m: (0, 0)), _main_spec(tm, d), _row_spec(d)],
        out_shape=[jax.ShapeDtypeStruct((1, 1), F32), jax.ShapeDtypeStruct((s, d), F32), jax.ShapeDtypeStruct((1, d), F32)],
        compiler_params=_params("arbitrary"),
    )(h, gain, target)


def _ffn_fwd(h, wts, i):
    u, act, s1, q1 = _ffn_up(h, wts[f"ln2_{i}"], wts[f"ffn{i}_w_gu"], f"ffn{i}_up")
    h_new = _mm_row(act, wts[f"ffn{i}_w_down"], h, None, f"ffn{i}_down")
    return h_new, (h, u, act, s1, q1)


def _ffn_bwd(dh, saved, wts, i, g):
    h, u, act, s1, q1 = saved
    dgu = _ffn_down_bwd(dh, wts[f"ffn{i}_w_down"], s1, q1, f"ffn{i}_down_bwd")
    g[f"ffn{i}_w_down"] = _mm_tn(act, dh, 1, f"ffn{i}_dw_down")
    g[f"ffn{i}_w_gu"] = _mm_tn(u, dgu, N_CHIPS, f"ffn{i}_dw_gu")
    dh_new, g[f"ln2_{i}"] = _mm_nt_col_rms_bwd(dgu, wts[f"ffn{i}_w_gu"], h, wts[f"ln2_{i}"], dh, f"ffn{i}_up_bwd")
    return dh_new


def _device_step(x, target, wts, g=None):
    g = {} if g is None else g
    saved = {}
    h = x

    def short_conv_fwd(h, i):
        u, bcv = _mm_col(h, wts[f"ln1_{i}"], wts[f"a{i}_w_in"], None, f"a{i}_in")
        p = _sconv_fwd(bcv, wts[f"a{i}_conv"], f"a{i}_conv")
        return _mm_row(p, wts[f"a{i}_w_out"], h, None, f"a{i}_out"), (h, u, bcv, p)

    def short_conv_bwd(dh, sv, i):
        h, u, bcv, p = sv
        dp = _mm_nt_row(dh, wts[f"a{i}_w_out"], f"a{i}_out_bwd")
        dbcv, dcw = _sconv_bwd(dp, bcv, wts[f"a{i}_conv"], f"a{i}_conv_bwd")
        g[f"a{i}_conv"] = dcw[:SHORT_CONV_W]
        g[f"a{i}_w_in"] = _mm_tn(u, dbcv, N_CHIPS, f"a{i}_dw_in")
        g[f"a{i}_w_out"] = _mm_tn(p, dh, 1, f"a{i}_dw_out")
        dh, g[f"ln1_{i}"] = _mm_nt_col_rms_bwd(dbcv, wts[f"a{i}_w_in"], h, wts[f"ln1_{i}"], dh, f"a{i}_in_bwd")
        return dh

    h, saved["a0"] = short_conv_fwd(h, 0)
    h, saved["f0"] = _ffn_fwd(h, wts, 0)

    h_in = h
    h, mixed = _pool_fwd(h, wts["ln1_1"], wts["b1_w_grp"], wts["b1_scale"], "b1_fwd")
    saved["b1"] = (h_in, mixed)
    h, saved["f1"] = _ffn_fwd(h, wts, 1)

    h_in = h
    u, ag = _mm_col(h, wts["ln1_2"], wts["c2_w_pw1"], wts["c2_b_pw1"], "c2_pw1")
    sw, hc = _conf_mid_fwd(ag, wts["c2_dw"], wts["c2_b_dw"], wts["c2_ln_g"], wts["c2_ln_b"], "c2_mid")
    h = _mm_row(sw, wts["c2_w_pw2"], h, wts["c2_b_pw2"], "c2_pw2")
    saved["c2"] = (h_in, u, ag, sw, hc)
    h, saved["f2"] = _ffn_fwd(h, wts, 2)

    h, saved["a3"] = short_conv_fwd(h, 3)
    h, saved["f3"] = _ffn_fwd(h, wts, 3)

    loss, dh, g["ln_f"] = _loss_head(h, wts["ln_f"], target, "loss_head")

    def ffn_bwd(dh, i):
        return _ffn_bwd(dh, saved[f"f{i}"], wts, i, g)

    dh = ffn_bwd(dh, 3)
    dh = short_conv_bwd(dh, saved["a3"], 3)

    dh = ffn_bwd(dh, 2)
    h_in, u, ag, sw, hc = saved["c2"]
    dhc, g["c2_ln_g"], g["c2_ln_b"], g["c2_b_pw2"] = _conf_out_bwd(
        dh, wts["c2_w_pw2"], hc, wts["c2_ln_g"], wts["c2_ln_b"], "c2_pw2_bwd")
    g["c2_w_pw2"] = _mm_tn(sw, dh, 1, "c2_dw_pw2")
    dag, ddw, g["c2_b_dw"], g["c2_b_pw1"] = _conf_mid_bwd(dhc, ag, wts["c2_dw"], "c2_mid_bwd")
    g["c2_dw"] = ddw[:CONF_CONV_W]
    g["c2_w_pw1"] = _mm_tn(u, dag, N_CHIPS, "c2_dw_pw1")
    dh, g["ln1_2"] = _mm_nt_col_rms_bwd(dag, wts["c2_w_pw1"], h_in, wts["ln1_2"], dh, "c2_pw1_bwd")

    dh = ffn_bwd(dh, 1)
    h_in, mixed = saved["b1"]
    dmixed, g["b1_w_grp"], g["b1_scale"] = _pool_bwd_mm(dh, mixed, wts["b1_w_grp"], wts["b1_scale"], "b1_bwd_mm")
    dh, g["ln1_1"] = _pool_bwd_rms(dmixed, h_in, wts["ln1_1"], dh, "b1_bwd_rms")

    dh = ffn_bwd(dh, 0)
    dh = short_conv_bwd(dh, saved["a0"], 0)
    return loss, dh, g


MESH = pl.DeviceIdType.MESH
ANY = pl.BlockSpec(memory_space=pl.ANY)


def _position():
    return lax.axis_index("x"), lax.axis_index("y"), lax.axis_index("c")


def _other_chips(x, y):
    return [(1 - x, y), (x, 1 - y), (1 - x, 1 - y)]


def _remote(src, dst, send_sem, recv_sem, to):
    return pltpu.make_async_remote_copy(src_ref=src, dst_ref=dst, send_sem=send_sem, recv_sem=recv_sem,
                                        device_id=to, device_id_type=MESH)


def _half_rows(ref_rows, c):
    hr = ref_rows // 2
    return pl.ds(pl.multiple_of(c * hr, 16), hr)


def _allgather8(v, name):
    m_per, n = v.shape

    def body(v_ref, out_ref, send_sems, recv_sems, local_sem):
        x, y, c = _position()
        me, sibling = (x, y, c), (x, y, 1 - c)
        chips = _other_chips(x, y)

        def rows(px, py, pc):
            return out_ref.at[pl.ds((4 * px + 2 * py + pc) * m_per, m_per), :]

        def copy(k, block, to, src=None):
            return _remote(rows(*block) if src is None else src, rows(*block), send_sems.at[k], recv_sems.at[k], to)

        mine = pltpu.make_async_copy(v_ref, rows(*me), local_sem)
        mine.start()
        first = [copy(0, me, sibling, src=v_ref)]
        first += [copy(1 + j, me, (*chip, c), src=v_ref) for j, chip in enumerate(chips)]
        for cp in first:
            cp.start()
        passed = [copy(4 + j, (*chip, c), sibling) for j, chip in enumerate(chips)]
        for j, chip in enumerate(chips):
            copy(1 + j, (*chip, c), me).wait_recv()
            passed[j].start()
        copy(0, sibling, me).wait_recv()
        for j, chip in enumerate(chips):
            copy(4 + j, (*chip, 1 - c), me).wait_recv()
        for cp in first + passed:
            cp.wait_send()
        mine.wait()

    return _hosted(
        body, name=name,
        out_shape=jax.ShapeDtypeStruct((N_DEV * m_per, n), v.dtype),
        in_specs=[pl.BlockSpec(memory_space=pltpu.VMEM)],
        out_specs=pl.BlockSpec(memory_space=pltpu.VMEM),
        scratch_shapes=[pltpu.SemaphoreType.DMA((7,)), pltpu.SemaphoreType.DMA((7,)), pltpu.SemaphoreType.DMA],
        compiler_params=pltpu.CompilerParams(vmem_limit_bytes=VMEM_LIMIT),
    )(v)


def _cast_to_slot(ws, idx, name):
    r, cols = ws[0].shape
    assert all(w.shape == (r, cols) for w in ws)
    n = len(ws)
    tr = _tile(r, 256, 16)

    def body(idx_ref, *refs):
        for w_ref, o_ref in zip(refs[:n], refs[n:]):
            o_ref[...] = w_ref[...].astype(o_ref.dtype)

    return _hosted(
        body, name=name,
        grid_spec=pltpu.PrefetchScalarGridSpec(
            num_scalar_prefetch=1, grid=(r // tr,),
            in_specs=[pl.BlockSpec((tr, cols), lambda t, idx_ref: (t, 0))] * n,
            out_specs=[pl.BlockSpec((None, tr, cols), lambda t, idx_ref: (idx_ref[0], t, 0))] * n),
        out_shape=[jax.ShapeDtypeStruct((N_CHIPS, r, cols), BF16)] * n,
        compiler_params=_params("parallel"),
    )(idx, *ws)


def _dma_sems(*shape):
    return [pltpu.SemaphoreType.DMA(shape), pltpu.SemaphoreType.DMA(shape)]


def _same_shapes(arrays):
    return [jax.ShapeDtypeStruct(a.shape, a.dtype) for a in arrays]


def _part_rows(ref_rows, c, part):
    hr = ref_rows // 2
    i, n = part
    size = hr // n
    assert size * n == hr and size % 16 == 0, (ref_rows, part)
    return pl.ds(pl.multiple_of(c * hr + i * size, 16), size)


def _task_gather_ici(bufs, done, part=(0, 1)):
    n = len(bufs)

    def copies(outs, sems, landing):
        x, y, c = _position()
        my_chip = 2 * x + y
        res = []
        for i in range(n):
            rows = _part_rows(bufs[i].shape[1], c, part)
            for r, (px, py) in enumerate(_other_chips(x, y)):
                slot = (2 * px + py) if landing else my_chip
                res.append(_remote(outs[i].at[my_chip, rows, :], outs[i].at[slot, rows, :], sems[0].at[i, r], sems[1].at[i, r],
                                   (px, py, c)))
        return res

    def start(ins, outs, sems):
        for cp in copies(outs, sems, False):
            cp.start()

    def wait(ins, outs, sems):
        for cp in copies(outs, sems, True):
            cp.wait_recv()
            cp.wait_send()

    return _Task(bufs, _same_shapes(bufs), {i: i for i in range(n)}, _dma_sems(n, 3), start, wait, done)


def _task_gather_d2d(bufs, done):
    n = len(bufs)

    def copies(outs, sems, landing):
        x, y, c = _position()
        res = []
        for i in range(n):
            rows = _half_rows(bufs[i].shape[1], (1 - c) if landing else c)
            for r, (px, py) in enumerate(_other_chips(x, y)):
                part = outs[i].at[2 * px + py, rows, :]
                res.append(_remote(part, part, sems[0].at[i, r], sems[1].at[i, r], (x, y, 1 - c)))
        return res

    def start(ins, outs, sems):
        for cp in copies(outs, sems, False):
            cp.start()

    def wait(ins, outs, sems):
        for cp in copies(outs, sems, True):
            cp.wait_recv()
        for cp in copies(outs, sems, False):
            cp.wait_send()

    return _Task(bufs, _same_shapes(bufs), {i: i for i in range(n)}, _dma_sems(n, 3), start, wait, done)


def _task_sibling_halves(grads, done):
    n = len(grads)

    def copies(ins, outs, sems):
        x, y, c = _position()
        return [_remote(ins[i].at[:, _half_rows(grads[i].shape[1], 1 - c), :], outs[i], sems[0].at[i], sems[1].at[i],
                        (x, y, 1 - c)) for i in range(n)]

    def start(ins, outs, sems):
        for cp in copies(ins, outs, sems):
            cp.start()

    def wait(ins, outs, sems):
        for cp in copies(ins, outs, sems):
            cp.wait()

    shapes = [jax.ShapeDtypeStruct((g.shape[0], g.shape[1] // 2, g.shape[2]), g.dtype) for g in grads]
    return _Task(grads, shapes, {}, _dma_sems(n), start, wait, done)


def _task_chip_sums(parts, done, landed=None, part=(0, 1)):
    n = len(parts)
    i_part, n_parts = part
    sizes = [p.shape[1] // n_parts for p in parts]
    assert all(p.shape[1] == size * n_parts and size % 16 == 0 for p, size in zip(parts, sizes)), part
    rows = [pl.ds(i_part * size, size) for size in sizes]

    def copies(ins, outs, sems):
        x, y, c = _position()
        return [_remote(ins[i].at[2 * px + py, rows[i], :], outs[i].at[r, rows[i], :], sems[0].at[i, r], sems[1].at[i, r],
                        (px, py, c))
                for i in range(n) for r, (px, py) in enumerate(_other_chips(x, y))]

    def start(ins, outs, sems):
        for cp in copies(ins, outs, sems):
            cp.start()

    def wait(ins, outs, sems):
        for cp in copies(ins, outs, sems):
            cp.wait()

    shapes = [jax.ShapeDtypeStruct((3,) + p.shape[1:], p.dtype) for p in parts]
    if landed is None:
        return _Task(parts, shapes, {}, _dma_sems(n, 3), start, wait, done)
    return _Task(list(parts) + list(landed), shapes, {n + i: i for i in range(n)}, _dma_sems(n, 3), start, wait, done)


def _task_sibling_parts(owns, landeds, done):
    n = len(owns)

    def copies(ins, outs, sems):
        x, y, c = _position()
        sibling = (x, y, 1 - c)
        res = []
        for i in range(n):
            res.append(_remote(ins[i].at[2 * x + y], outs[i].at[0], sems[0].at[i, 0], sems[1].at[i, 0], sibling))
            res.append(_remote(ins[n + i], outs[i].at[pl.ds(1, 3)], sems[0].at[i, 1], sems[1].at[i, 1], sibling))
        return res

    def start(ins, outs, sems):
        for cp in copies(ins, outs, sems):
            cp.start()

    def wait(ins, outs, sems):
        for cp in copies(ins, outs, sems):
            cp.wait()

    return _Task(list(owns) + list(landeds), _same_shapes(owns), {}, _dma_sems(n, 2), start, wait, done)


def _add_halves(grads, sibs, idx, name):
    n = len(grads)
    nsh = grads[0].shape[0]

    def body(idx_ref, *refs):
        for g_ref, s_ref, o_ref in zip(refs[:n], refs[n:2 * n], refs[2 * n:]):
            o_ref[...] = (g_ref[...].astype(F32) + s_ref[...].astype(F32)).astype(o_ref.dtype)

    half_of = [pl.BlockSpec((None, s.shape[1], s.shape[2]), lambda j, idx_ref: (j, idx_ref[1], 0)) for s in sibs]
    whole = [pl.BlockSpec((None, s.shape[1], s.shape[2]), lambda j, idx_ref: (j, 0, 0)) for s in sibs]
    return _hosted(
        body, name=name,
        grid_spec=pltpu.PrefetchScalarGridSpec(num_scalar_prefetch=1, grid=(nsh,), in_specs=half_of + whole, out_specs=whole),
        out_shape=_same_shapes(sibs),
        compiler_params=_params("parallel"),
    )(idx, *grads, *sibs)


def _adamw_reduced(w, own, landed, sib, m, v, idx, name):
    r, cols = w.shape
    hr = r // 2
    tr = _tile(hr, 256, 16)
    nt = hr // tr

    def body(idx_ref, w_ref, p_ref, l_ref, s_ref, m_ref, v_ref, go_ref, d_ref, mo_ref, vo_ref):
        mine = p_ref[...].astype(F32)
        for k in range(3):
            mine = mine + l_ref[k].astype(F32)
        theirs = s_ref[0].astype(F32)
        for k in range(1, 4):
            theirs = theirs + s_ref[k].astype(F32)
        grad = jnp.where(pl.program_id(0) // nt == idx_ref[1], mine, theirs)
        go_ref[...] = grad
        d_ref[...], mo_ref[...], vo_ref[...] = _adamw_update(w_ref[...], grad, m_ref[...], v_ref[...])

    def in_half(t, half):
        return jnp.clip(t - half * nt, 0, nt - 1)

    full = pl.BlockSpec((tr, cols), lambda t, idx_ref: (t, 0))
    return _hosted(
        body, name=name,
        grid_spec=pltpu.PrefetchScalarGridSpec(
            num_scalar_prefetch=1, grid=(2 * nt,),
            in_specs=[full,
                      pl.BlockSpec((None, tr, cols), lambda t, idx_ref: (idx_ref[0], in_half(t, idx_ref[1]), 0)),
                      pl.BlockSpec((3, tr, cols), lambda t, idx_ref: (0, in_half(t, idx_ref[1]), 0)),
                      pl.BlockSpec((4, tr, cols), lambda t, idx_ref: (0, in_half(t, 1 - idx_ref[1]), 0)),
                      full, full],
            out_specs=[full] * 4),
        out_shape=[jax.ShapeDtypeStruct((r, cols), F32)] * 4,
        compiler_params=_params("arbitrary"),
    )(idx, w, own, landed, sib, m, v)


def _sum_devices(blocks, name):
    m8, n = blocks.shape
    m = m8 // N_DEV

    def body(b_ref, o_ref):
        acc = b_ref[pl.ds(0, m), :]
        for k in range(1, N_DEV):
            acc = acc + b_ref[pl.ds(k * m, m), :]
        o_ref[...] = acc

    return _hosted(
        body, name=name, out_shape=jax.ShapeDtypeStruct((m, n), F32),
        in_specs=[pl.BlockSpec(memory_space=pltpu.VMEM)], out_specs=pl.BlockSpec(memory_space=pltpu.VMEM),
        compiler_params=pltpu.CompilerParams(vmem_limit_bytes=VMEM_LIMIT),
    )(blocks)


def _adamw_update(w, grad, m, v):
    new_m = ADAM_B1 * m + (1.0 - ADAM_B1) * grad
    new_v = ADAM_B2 * v + (1.0 - ADAM_B2) * (grad * grad)
    m_hat = new_m * (1.0 / (1.0 - ADAM_B1 ** ADAM_STEP))
    v_hat = new_v * (1.0 / (1.0 - ADAM_B2 ** ADAM_STEP))
    return -ADAM_LR * (m_hat / (jnp.sqrt(v_hat) + ADAM_EPS) + ADAM_WD * w), new_m, new_v


def _adamw_small(grad_blocks, params, name):
    nb, npar = len(grad_blocks), len(params)

    def body(*refs):
        blocks, ins, outs = refs[:nb], refs[nb:nb + 3 * npar], refs[nb + 3 * npar:]
        for p, (w, _, _, blk, row0) in enumerate(params):
            if w.ndim == 1:
                tiled = (w.shape[0] // LANES, LANES)
                grad = blocks[blk][pl.ds(row0, tiled[0]), pl.ds(0, LANES)]
                wmv = [ins[3 * p + k][...].reshape(tiled) for k in range(3)]
            else:
                grad = blocks[blk][pl.ds(row0, w.shape[0]), :]
                wmv = [ins[3 * p + k][...] for k in range(3)]
            for k, res in enumerate((grad,) + _adamw_update(wmv[0], grad, wmv[1], wmv[2])):
                outs[4 * p + k][...] = res.reshape(w.shape)

    args = list(grad_blocks) + [a for w, m, v, _, _ in params for a in (w, m, v)]
    vmem = pl.BlockSpec(memory_space=pltpu.VMEM)
    out = _hosted(
        body, name=name, in_specs=[vmem] * len(args), out_specs=[vmem] * (4 * npar),
        out_shape=[jax.ShapeDtypeStruct(w.shape, F32) for w, _, _, _, _ in params for _ in range(4)],
    )(*args)
    return [tuple(out[4 * p:4 * p + 4]) for p in range(npar)]


WEIGHT_NAMES = (
    "ln1_0", "a0_w_in", "a0_conv", "a0_w_out", "ln2_0", "ffn0_w_gu", "ffn0_w_down",
    "ln1_1", "b1_w_grp", "b1_scale", "ln2_1", "ffn1_w_gu", "ffn1_w_down",
    "ln1_2", "c2_w_pw1", "c2_b_pw1", "c2_dw", "c2_b_dw", "c2_ln_g", "c2_ln_b", "c2_w_pw2", "c2_b_pw2",
    "ln2_2", "ffn2_w_gu", "ffn2_w_down",
    "ln1_3", "a3_w_in", "a3_conv", "a3_w_out", "ln2_3", "ffn3_w_gu", "ffn3_w_down", "ln_f")
BIG = ("a0_w_in", "a0_w_out", "ffn0_w_gu", "ffn0_w_down", "b1_w_grp", "ffn1_w_gu", "ffn1_w_down", "c2_w_pw1", "c2_w_pw2",
       "ffn2_w_gu", "ffn2_w_down", "a3_w_in", "a3_w_out", "ffn3_w_gu", "ffn3_w_down")
GROUPED = "b1_w_grp"
SMALL_SHARDED = ("a0_conv", "a3_conv", "c2_dw")
REPLICATED = tuple(n for n in WEIGHT_NAMES if n not in BIG and n not in SMALL_SHARDED)


def _pad_rows(a, mult=8):
    pad = -a.shape[0] % mult
    return a if pad == 0 else jnp.concatenate([a, jnp.zeros((pad, a.shape[1]), a.dtype)], axis=0)


def _pack_rows(parts, width):
    rows = [p.reshape(-1, width) for p in parts]
    return _pad_rows(jnp.concatenate(rows, axis=0)), [r.shape[0] for r in rows]


def _unpack_rows(packed, counts, shapes):
    out, at = [], 0
    for n, shp in zip(counts, shapes):
        out.append(packed[at:at + n].reshape(shp))
        at += n
    return out


COLUMN_SHARDED = ("w_in", "w_gu", "w_pw1")


class _Weights(dict):
    def __init__(self, bufs):
        super().__init__()
        self.bufs = bufs

    def __missing__(self, name):
        buf = self.bufs[name]
        if name == GROUPED:
            cg = buf.shape[-1]
            rq = cg // N_CHIPS
            return jnp.transpose(buf.reshape(N_CHIPS, -1, rq, cg), (1, 0, 2, 3)).reshape(-1, cg, cg)
        return buf if name.endswith(COLUMN_SHARDED) else buf.reshape(-1, buf.shape[-1])


class _Exchange:
    def __init__(self, w, mom, vel, idx):
        def shards(table):
            return {n: table[n].reshape(-1, table[n].shape[-1]) for n in BIG}

        self.w, self.mom, self.vel, self.idx = shards(w), shards(mom), shards(vel), idx
        self.bufs = {}
        self.weights = _Weights(self.bufs)
        self.grads = {}
        self.sib, self.part, self.landed, self.sib_parts, self.updates = {}, {}, {}, {}, {}

    def cast(self, names):
        by_shape = {}
        for n in names:
            by_shape.setdefault(self.w[n].shape, []).append(n)
        for group in by_shape.values():
            self.bufs.update(zip(group, _cast_to_slot([self.w[n] for n in group], self.idx, f"cast_{group[0]}")))

    @staticmethod
    def _store(table, names):
        def done(arrays):
            table.update(zip(names, arrays))
        return done

    def _grad(self, n):
        g = self.grads[n]
        if n == GROUPED:
            ng, cg, _ = g.shape
            g = jnp.transpose(g.reshape(ng, N_CHIPS, cg // N_CHIPS, cg), (1, 0, 2, 3)).astype(BF16)
        return g.reshape(N_CHIPS, -1, g.shape[-1])

    def gather_ici(self, *names, part=(0, 1)):
        return lambda: _task_gather_ici([self.bufs[n] for n in names], self._store(self.bufs, names), part)

    def gather_d2d(self, *names):
        return lambda: _task_gather_d2d([self.bufs[n] for n in names], self._store(self.bufs, names))

    def sibling_halves(self, *names):
        return lambda: _task_sibling_halves([self._grad(n) for n in names], self._store(self.sib, names))

    def add_halves(self, *names):
        def run():
            parts = _add_halves([self._grad(n) for n in names], [self.sib.pop(n) for n in names], self.idx,
                                f"reduce_add_{names[0]}")
            self.part.update(zip(names, parts))
        return run

    def chip_sums(self, *names, part=(0, 1)):
        def make():
            landed = [self.landed[n] for n in names] if part[0] > 0 else None
            return _task_chip_sums([self.part[n] for n in names], self._store(self.landed, names), landed, part)
        return make

    def sibling_parts(self, *names):
        return lambda: _task_sibling_parts([self.part[n] for n in names], [self.landed[n] for n in names],
                                           self._store(self.sib_parts, names))

    def adamw(self, *names):
        def run():
            for n in names:
                self.updates[n] = _adamw_reduced(self.w[n], self.part.pop(n), self.landed.pop(n), self.sib_parts.pop(n),
                                                 self.mom[n], self.vel[n], self.idx, f"adamw_{n}")
        return run


def _plan(ex):
    s = _Schedule()

    def ffn(i):
        return f"ffn{i}_w_gu", f"ffn{i}_w_down"

    c2, a3 = ("c2_w_pw1", "c2_w_pw2"), ("a3_w_in", "a3_w_out")
    first, second = (0, 2), (1, 2)
    s.host("cast_ffn0_w_gu", ex.gather_ici("a0_w_in", part=first))
    s.host("cast_ffn0_w_down", ex.gather_ici("a0_w_in", part=second))
    s.host("cast_c2_w_pw1", ex.gather_d2d("a0_w_in"))
    gu, down = ffn(0)
    s.host("gather_small", ex.gather_ici("a0_w_out"))
    s.host("a0_in", ex.gather_ici(gu, part=first), ex.gather_d2d("a0_w_out"))
    s.host("a0_conv", ex.gather_ici(gu, part=second))
    s.host("a0_out", ex.gather_ici(down), ex.gather_d2d(gu))
    s.host("ffn0_up", ex.gather_d2d(down))
    gu, down = ffn(1)
    s.host("ffn0_up", ex.gather_ici(gu, GROUPED))
    s.host("ffn0_down", ex.gather_ici(down), ex.gather_d2d(gu, GROUPED))
    s.host("ffn1_up", ex.gather_d2d(down), ex.gather_ici(*c2))
    gu, down = ffn(2)
    s.host("ffn1_up", ex.gather_ici(gu, part=first))
    s.host("ffn1_down", ex.gather_d2d(*c2), ex.gather_ici(down))
    s.host("c2_mid", ex.gather_ici(gu, part=second))
    s.host("c2_pw2", ex.gather_d2d(gu, down), ex.gather_ici(a3[1]))
    s.host("ffn2_up", ex.gather_ici(a3[0]))
    gu, down = ffn(3)
    s.host("ffn2_up", ex.gather_ici(gu, part=first))
    s.host("ffn2_down", ex.gather_d2d(*a3), ex.gather_ici(down))
    s.host("a3_in", ex.gather_ici(gu, part=second))
    s.host("a3_out", ex.gather_d2d(gu, down))

    def reduce_on(names, first_host, ici_hosts, last_host):
        s.host(first_host, ex.sibling_halves(*names))
        s.post(first_host, ex.add_halves(*names))
        for host, hosted, part in ici_hosts:
            s.host(host, ex.chip_sums(*hosted, part=part))
        s.host(last_host, ex.sibling_parts(*names))
        s.post(last_host, ex.adamw(*names))

    whole = (0, 1)
    gu, down = ffn(3)
    reduce_on((gu, down), "a3_out_bwd",
              [("a3_conv_bwd", (down,), whole), ("a3_dw_in", (gu,), first), ("a3_in_bwd", (gu,), second)], "ffn2_down_bwd")
    reduce_on(a3, "ffn2_down_bwd", [("ffn2_dw_gu", a3, whole)], "c2_pw2_bwd")
    gu, down = ffn(0)
    s.host("ffn0_dw_gu", ex.sibling_halves(down))
    s.post("ffn0_dw_gu", ex.add_halves(down))
    s.host("ffn0_up_bwd", ex.chip_sums(down))
    s.host("a0_out_bwd", ex.sibling_halves(gu, GROUPED))
    s.post("a0_out_bwd", ex.add_halves(gu, GROUPED))
    s.host("a0_conv_bwd", ex.chip_sums(gu, part=first), ex.chip_sums(GROUPED))
    s.host("a0_dw_in", ex.chip_sums(gu, part=second))
    s.host("a0_dw_out", ex.sibling_halves("a0_w_in"))
    s.post("a0_dw_out", ex.add_halves("a0_w_in"))
    s.host("a0_in_bwd", ex.chip_sums("a0_w_in"), ex.sibling_parts(gu, down, GROUPED))
    s.host(f"adamw_{gu}", ex.chip_sums("a0_w_out"))
    s.host(f"adamw_{down}", ex.sibling_parts("a0_w_out"))
    reduce_on(ffn(2), "c2_pw2_bwd", [("c2_mid_bwd", ffn(2), whole)], "ffn1_down_bwd")
    reduce_on(c2, "ffn1_down_bwd", [("ffn1_dw_down", c2, whole)], "b1_bwd_mm")
    gu, down = ffn(1)
    reduce_on((gu, down), "b1_bwd_mm", [("ffn0_down_bwd", (down,), whole), ("ffn0_dw_gu", (gu,), whole)], "ffn0_up_bwd")
    return s


def kernel(x, *rest):
    nw = len(WEIGHT_NAMES)
    w = dict(zip(WEIGHT_NAMES, rest[:nw]))
    target = rest[nw]
    mom = dict(zip(WEIGHT_NAMES, rest[nw + 1:2 * nw + 1]))
    vel = dict(zip(WEIGHT_NAMES, rest[2 * nw + 1:3 * nw + 1]))
    cx, cy, cc = _position()
    my_chip = 2 * cx + cy
    ex = _Exchange(w, mom, vel, jnp.stack([my_chip, cc]).astype(jnp.int32))
    _ACTIVE_SCHEDULE[0] = _plan(ex)
    try:
        return _scheduled_step(x, target, w, mom, vel, ex, my_chip)
    finally:
        _ACTIVE_SCHEDULE[0] = None


def _scheduled_step(x, target, w, mom, vel, ex, my_chip):
    d = x.shape[-1]
    cq = d // N_CHIPS
    ex.cast(BIG)

    small_blk, small_counts = _pack_rows([w[n] for n in SMALL_SHARDED], cq)
    small_all = _allgather8(small_blk, "gather_small").reshape(N_CHIPS, 2, small_blk.shape[0], cq)[:, 0]
    small_parts = _unpack_rows(jnp.transpose(small_all, (1, 0, 2)), small_counts,
                               [(w[n].reshape(-1, cq).shape[0], N_CHIPS, cq) for n in SMALL_SHARDED])
    wts = ex.weights
    for n in REPLICATED:
        wts[n] = w[n].reshape(1, -1)
    for n, part in zip(SMALL_SHARDED, small_parts):
        wts[n] = part.reshape(part.shape[0], d)

    loss, dx, g = _device_step(x[0], target[0], wts, ex.grads)

    summed, last = ("ffn0_w_gu", "ffn0_w_down", GROUPED, "a0_w_in"), "a0_w_out"
    _comm_only([ex.sibling_parts("a0_w_in")(), ex.sibling_halves(last)()], "reduce_tail_d2d")
    ex.add_halves(last)()
    ex.adamw(*summed)()
    ex.adamw(last)()
    sched = _ACTIVE_SCHEDULE[0]
    assert not sched.hosts and not sched.posts, (sched.hosts, sched.posts)

    rep_rows = [jnp.pad(g[n].reshape(-1, LANES), ((0, 0), (0, cq - LANES))) for n in REPLICATED]
    by_chip = [jnp.transpose(g[n].reshape(g[n].shape[0], N_CHIPS, cq), (1, 0, 2)) for n in SMALL_SHARDED]
    shard_rows = jnp.concatenate(by_chip, axis=1)
    n_rep, n_shard = sum(r.shape[0] for r in rep_rows), shard_rows.shape[1]
    loss_row = jnp.broadcast_to(loss, (1, cq))
    sm_blk = _pad_rows(jnp.concatenate(rep_rows + [loss_row, shard_rows.reshape(N_CHIPS * n_shard, cq)], axis=0))
    sm_sum = _sum_devices(_allgather8(sm_blk, "gather_small_grads"), "sum_small_grads")
    mine = lax.dynamic_slice_in_dim(sm_sum, n_rep + 1 + my_chip * n_shard, n_shard, axis=0)

    out = ex.updates
    params, at = [], {0: 0, 1: 0}
    for block, names in ((0, REPLICATED), (1, SMALL_SHARDED)):
        for n in names:
            params.append((w[n], mom[n], vel[n], block, at[block]))
            at[block] += w[n].size // LANES if w[n].ndim == 1 else w[n].shape[0]
    out.update(zip(REPLICATED + SMALL_SHARDED, _adamw_small([sm_sum, mine], params, "adamw_small")))

    total = sm_sum[n_rep, 0]
    grads, deltas, new_m, new_v = ([out[n][k].reshape(w[n].shape) for n in WEIGHT_NAMES] for k in range(4))
    return (total, dx.reshape(x.shape), *grads, *deltas, *new_m, *new_v)
```

```python
import functools

import jax
import jax.numpy as jnp
from jax import lax
from jax.experimental import pallas as pl
from jax.experimental.pallas import tpu as pltpu

F32 = jnp.float32
BF16 = jnp.bfloat16

RMS_EPS = 1e-6
LN_EPS = 1e-5
POOL_WINDOWS = (2, 4, 8, 16)
SHORT_CONV_W = 3
CONF_CONV_W = 31
N_CHIPS = 4
N_DEV = 8

ADAM_LR = 0.001
ADAM_B1 = 0.9
ADAM_B2 = 0.999
ADAM_EPS = 1e-08
ADAM_WD = 0.01
ADAM_STEP = 10

V7X_VMEM_BYTES = 64 * 1024 * 1024
VMEM_LIMIT = V7X_VMEM_BYTES - 8 * 1024 * 1024
LANES = 128
POOL_HALO = 16
SCONV_HALO = 16
CONF_HALO = 32


def _params(*sem):
    return pltpu.CompilerParams(dimension_semantics=sem, vmem_limit_bytes=VMEM_LIMIT)


def _tile(n, pref, mult=8):
    t = min(n, pref)
    while t > mult and (n % t or t % mult):
        t -= mult
    assert n % t == 0 and t % mult == 0, (n, pref, mult)
    return t


def _sigmoid(x):
    return jax.nn.sigmoid(x)


def _dot(a, b):
    return jnp.dot(a, b, preferred_element_type=F32)


def _dot_nt(a, b):
    return lax.dot_general(a, b, (((1,), (1,)), ((), ())), preferred_element_type=F32)


def _dot_tn(a, b):
    return lax.dot_general(a, b, (((0,), (0,)), ((), ())), preferred_element_type=F32)


def _colsum(x):
    return jnp.sum(x, axis=0, keepdims=True)


def _rms_stats(x):
    return lax.rsqrt(jnp.mean(x * x, axis=-1, keepdims=True) + RMS_EPS)


def _rms_bwd(du, x, gain):
    r = _rms_stats(x)
    xhat = x * r
    gdy = du * gain
    dx = r * (gdy - xhat * jnp.mean(gdy * xhat, axis=-1, keepdims=True))
    return dx, _colsum(du * xhat)


class _Task:
    def __init__(self, ins, out_shapes, aliases, sems, start, wait, done):
        self.ins, self.out_shapes, self.aliases, self.sems = list(ins), list(out_shapes), dict(aliases), list(sems)
        self.start, self.wait, self.done = start, wait, done


class _Schedule:
    def __init__(self):
        self.hosts, self.posts = {}, {}

    def host(self, kernel_name, *make_tasks):
        self.hosts.setdefault(kernel_name, []).extend(make_tasks)

    def post(self, kernel_name, *thunks):
        self.posts.setdefault(kernel_name, []).extend(thunks)

    def tasks_for(self, kernel_name):
        return [make() for make in self.hosts.pop(kernel_name, ())]

    def finished(self, kernel_name):
        for thunk in self.posts.pop(kernel_name, ()):
            thunk()


_ACTIVE_SCHEDULE = [None]


def _hosted(body, name, **kw):
    def run(*args):
        sched = _ACTIVE_SCHEDULE[0]
        tasks = sched.tasks_for(name) if sched is not None else []
        out = _call_with_tasks(body, name, tasks, kw, args) if tasks else pl.pallas_call(body, name=name, **kw)(*args)
        if sched is not None:
            sched.finished(name)
        return out

    return run


def _call_with_tasks(body, name, tasks, kw, args):
    spec = kw.get("grid_spec")
    n_pre = spec.num_scalar_prefetch if spec is not None else 0
    src = dict(grid=spec.grid, in_specs=spec.in_specs, out_specs=spec.out_specs) if spec is not None else kw
    pre, args = args[:n_pre], args[n_pre:]
    grid = tuple(src.get("grid", ()))
    single = not isinstance(kw["out_shape"], (list, tuple))
    out_shape = [kw["out_shape"]] if single else list(kw["out_shape"])
    out_specs = [src["out_specs"]] if single else list(src["out_specs"])
    scratch = list(kw.get("scratch_shapes", ()))
    n_in, n_out, n_scr = len(args), len(out_shape), len(scratch)
    t_in = [a for t in tasks for a in t.ins]
    t_out = [o for t in tasks for o in t.out_shapes]
    t_sem = [s for t in tasks for s in t.sems]
    aliases, at_in, at_out = {}, n_pre + n_in, n_out
    for t in tasks:
        for i, o in t.aliases.items():
            aliases[at_in + i] = at_out + o
        at_in += len(t.ins)
        at_out += len(t.out_shapes)

    def wrapped(*refs):
        pre_refs, refs = refs[:n_pre], refs[n_pre:]
        a = n_in
        b = a + len(t_in)
        c = b + n_out
        d = c + len(t_out)
        e = d + n_scr
        ins, tins, outs, touts, scr, tsems = refs[:a], refs[a:b], refs[b:c], refs[c:d], refs[d:e], refs[e:]
        views, i0, o0, s0 = [], 0, 0, 0
        for t in tasks:
            views.append((tins[i0:i0 + len(t.ins)], touts[o0:o0 + len(t.out_shapes)], tsems[s0:s0 + len(t.sems)]))
            i0, o0, s0 = i0 + len(t.ins), o0 + len(t.out_shapes), s0 + len(t.sems)

        def start_all():
            for t, v in zip(tasks, views):
                t.start(*v)

        def wait_all():
            for t, v in zip(tasks, views):
                t.wait(*v)

        if grid:
            first = functools.reduce(jnp.logical_and, [pl.program_id(i) == 0 for i in range(len(grid))])
            last = functools.reduce(jnp.logical_and, [pl.program_id(i) == grid[i] - 1 for i in range(len(grid))])
            pl.when(first)(start_all)
            body(*pre_refs, *ins, *outs, *scr)
            pl.when(last)(wait_all)
        else:
            start_all()
            body(*pre_refs, *ins, *outs, *scr)
            wait_all()

    in_specs = list(src["in_specs"]) + [ANY] * len(t_in)
    out_specs = out_specs + [ANY] * len(t_out)
    if spec is not None:
        layout = dict(grid_spec=pltpu.PrefetchScalarGridSpec(
            num_scalar_prefetch=n_pre, grid=grid, in_specs=in_specs, out_specs=out_specs, scratch_shapes=scratch + t_sem))
    else:
        layout = dict(grid=grid, in_specs=in_specs, out_specs=out_specs, scratch_shapes=scratch + t_sem)
    res = pl.pallas_call(
        wrapped, name=name, out_shape=out_shape + t_out, input_output_aliases=aliases,
        compiler_params=pltpu.CompilerParams(dimension_semantics=("arbitrary",) * len(grid), vmem_limit_bytes=VMEM_LIMIT),
        **layout,
    )(*pre, *args, *t_in)
    res = list(res)
    own, rest = res[:n_out], res[n_out:]
    for t in tasks:
        t.done(rest[:len(t.out_shapes)])
        rest = rest[len(t.out_shapes):]
    return own[0] if single else own


def _comm_only(tasks, name):
    _call_with_tasks(lambda: None, name, tasks, dict(grid=(), in_specs=[], out_specs=[], out_shape=[]), ())


def _mm_col(h, gain, w, bias, name):
    s, k = h.shape
    nsh, _, ns = w.shape
    tm = _tile(s, 512)
    has_bias = bias is not None

    def body(h_ref, gain_ref, w_ref, *rest):
        u_ref, o_ref = rest[-2:]
        x = h_ref[...]
        x = (x * _rms_stats(x) * gain_ref[...]).astype(BF16)
        u_ref[...] = x
        for j in range(nsh):
            cols = pl.ds(j * ns, ns)
            acc = _dot(x, w_ref[j])
            if has_bias:
                acc = acc + rest[0][:, cols]
            o_ref[:, cols] = acc.astype(o_ref.dtype)

    tokens = pl.BlockSpec((tm, k), lambda m: (m, 0))
    in_specs = [tokens, pl.BlockSpec((1, k), lambda m: (0, 0)), pl.BlockSpec((nsh, k, ns), lambda m: (0, 0, 0))]
    args = [h, gain, w]
    if has_bias:
        in_specs.append(pl.BlockSpec((1, nsh * ns), lambda m: (0, 0)))
        args.append(bias)
    return _hosted(
        body, name=name, grid=(s // tm,), in_specs=in_specs,
        out_specs=[tokens, pl.BlockSpec((tm, nsh * ns), lambda m: (m, 0))],
        out_shape=[jax.ShapeDtypeStruct((s, k), BF16), jax.ShapeDtypeStruct((s, nsh * ns), BF16)],
        compiler_params=_params("parallel"),
    )(*args)


def _mm_row(a, w, res, bias, name):
    s = a.shape[0]
    k, n = w.shape
    tm = _tile(s, 1024)
    has_bias = bias is not None

    def body(a_ref, w_ref, res_ref, *rest):
        o_ref = rest[-1]
        y = res_ref[...] + _dot(a_ref[...], w_ref[...])
        if has_bias:
            y = y + rest[0][...]
        o_ref[...] = y

    in_specs = [pl.BlockSpec((tm, k), lambda m: (m, 0)), pl.BlockSpec((k, n), lambda m: (0, 0)),
                pl.BlockSpec((tm, n), lambda m: (m, 0))]
    args = [a, w, res]
    if has_bias:
        in_specs.append(pl.BlockSpec((1, n), lambda m: (0, 0)))
        args.append(bias)
    return _hosted(
        body, name=name, grid=(s // tm,), in_specs=in_specs,
        out_specs=pl.BlockSpec((tm, n), lambda m: (m, 0)),
        out_shape=jax.ShapeDtypeStruct((s, n), F32),
        compiler_params=_params("parallel"),
    )(*args)


def _mm_nt_row(dy, w, name):
    s, n = dy.shape
    k = w.shape[0]
    tm = _tile(s, 512)

    def body(dy_ref, w_ref, o_ref):
        o_ref[...] = _dot_nt(dy_ref[...].astype(BF16), w_ref[...])

    return _hosted(
        body, name=name, grid=(s // tm,),
        in_specs=[pl.BlockSpec((tm, n), lambda m: (m, 0)), pl.BlockSpec((k, n), lambda m: (0, 0))],
        out_specs=pl.BlockSpec((tm, k), lambda m: (m, 0)),
        out_shape=jax.ShapeDtypeStruct((s, k), F32),
        compiler_params=_params("parallel"),
    )(dy, w)


def _ffn_up(h, gain, w, name):
    s, d = h.shape
    _, _, ns = w.shape
    tm = _tile(s, 512)

    def body(h_ref, gain_ref, wg_ref, wu_ref, u_ref, act_ref, s1_ref, q1_ref):
        x = h_ref[...]
        x = (x * _rms_stats(x) * gain_ref[...]).astype(BF16)

        @pl.when(pl.program_id(0) == 0)
        def _():
            u_ref[...] = x

        g = _dot(x, wg_ref[...])
        up = _dot(x, wu_ref[...])
        sg = _sigmoid(g)
        s1 = g * sg
        act_ref[...] = (s1 * up).astype(act_ref.dtype)
        s1_ref[...] = s1.astype(s1_ref.dtype)
        q1_ref[...] = (up * sg * (1.0 + g * (1.0 - sg))).astype(q1_ref.dtype)

    out = pl.BlockSpec((tm, ns), lambda j, m: (m, j))
    tokens = pl.BlockSpec((tm, d), lambda j, m: (m, 0))
    nm = s // tm
    u_once = pl.BlockSpec((tm, d), lambda j, m: (jnp.where(j == 0, m, nm - 1), 0))
    return _hosted(
        body, name=name, grid=(2, nm),
        in_specs=[tokens, pl.BlockSpec((1, d), lambda j, m: (0, 0)), pl.BlockSpec((None, d, ns), lambda j, m: (j, 0, 0)),
                  pl.BlockSpec((None, d, ns), lambda j, m: (j + 2, 0, 0))],
        out_specs=[u_once, out, out, out],
        out_shape=[jax.ShapeDtypeStruct((s, d), BF16)] + [jax.ShapeDtypeStruct((s, 2 * ns), BF16)] * 3,
        compiler_params=_params("arbitrary", "arbitrary"),
    )(h, gain, w, w)


def _ffn_down_bwd(dh, w, s1, q1, name):
    s, d = dh.shape
    f = w.shape[0]
    tm = _tile(s, 512)

    def body(dh_ref, w_ref, s1_ref, q1_ref, o_ref):
        da = _dot_nt(dh_ref[...].astype(BF16), w_ref[...])
        o_ref[:, :f] = (da * q1_ref[...].astype(F32)).astype(o_ref.dtype)
        o_ref[:, f:] = (da * s1_ref[...].astype(F32)).astype(o_ref.dtype)

    return _hosted(
        body, name=name, grid=(s // tm,),
        in_specs=[pl.BlockSpec((tm, d), lambda m: (m, 0)), pl.BlockSpec((f, d), lambda m: (0, 0)),
                  pl.BlockSpec((tm, f), lambda m: (m, 0)), pl.BlockSpec((tm, f), lambda m: (m, 0))],
        out_specs=pl.BlockSpec((tm, 2 * f), lambda m: (m, 0)),
        out_shape=jax.ShapeDtypeStruct((s, 2 * f), BF16),
        compiler_params=_params("parallel"),
    )(dh, w, s1, q1)


def _mm_nt_col_rms_bwd(dy, w, h, gain, dh, name):
    s = dy.shape[0]
    nsh, k, ns = w.shape
    tm = _tile(s, 512)

    def body(dy_ref, w_ref, h_ref, g_ref, dh_ref, o_ref, dg_ref):
        du = _dot_nt(dy_ref[:, :ns], w_ref[0])
        for j in range(1, nsh):
            du = du + _dot_nt(dy_ref[:, j * ns:(j + 1) * ns], w_ref[j])
        dx, dg = _rms_bwd(du, h_ref[...], g_ref[...])
        o_ref[...] = dh_ref[...] + dx
        _accumulate(dg_ref, dg, pl.program_id(0) == 0)

    return _hosted(
        body, name=name, grid=(s // tm,),
        in_specs=[pl.BlockSpec((tm, nsh * ns), lambda m: (m, 0)), pl.BlockSpec((nsh, k, ns), lambda m: (0, 0, 0)),
                  pl.BlockSpec((tm, k), lambda m: (m, 0)), pl.BlockSpec((1, k), lambda m: (0, 0)),
                  pl.BlockSpec((tm, k), lambda m: (m, 0))],
        out_specs=[pl.BlockSpec((tm, k), lambda m: (m, 0)), pl.BlockSpec((1, k), lambda m: (0, 0))],
        out_shape=[jax.ShapeDtypeStruct((s, k), F32), jax.ShapeDtypeStruct((1, k), F32)],
        compiler_params=_params("arbitrary"),
    )(dy, w, h, gain, dh)


def _mm_tn(a, dy, nsh, name):
    s, k = a.shape
    ns = dy.shape[1] // nsh
    tm = _tile(s, 2048)
    tk = _tile(k, 1408, LANES)
    nk, nm = k // tk, s // tm

    def body(a_ref, dy_ref, o_ref, acc_ref):
        m = pl.program_id(2)
        part = _dot_tn(a_ref[...], dy_ref[...].astype(BF16))

        @pl.when(m == 0)
        def _():
            acc_ref[...] = part

        @pl.when(m > 0)
        def _():
            acc_ref[...] += part

        @pl.when(m == nm - 1)
        def _():
            o_ref[...] = acc_ref[...].astype(o_ref.dtype)

    return _hosted(
        body, name=name, grid=(nsh, nk, nm),
        in_specs=[pl.BlockSpec((tm, tk), lambda j, kk, m: (m, kk)), pl.BlockSpec((tm, ns), lambda j, kk, m: (m, j))],
        out_specs=pl.BlockSpec((None, tk, ns), lambda j, kk, m: (j, kk, 0)),
        out_shape=jax.ShapeDtypeStruct((nsh, k, ns), BF16),
        scratch_shapes=[pltpu.VMEM((tk, ns), F32)],
        compiler_params=_params("parallel", "parallel", "arbitrary"),
    )(a, dy)


def _main_spec(tm, w):
    return pl.BlockSpec((tm, w), lambda m: (m, 0))


def _before_spec(tm, hb, w):
    return pl.BlockSpec((hb, w), lambda m: (jnp.maximum(m * (tm // hb) - 1, 0), 0))


def _after_spec(tm, hb, w, s):
    return pl.BlockSpec((hb, w), lambda m: (jnp.minimum((m + 1) * (tm // hb), s // hb - 1), 0))


def _row_spec(w, rows=1):
    return pl.BlockSpec((rows, w), lambda m: (0, 0))


CHUNK_LANES = 4 * LANES
CHUNK_ROWS = 32


def _build_shifts(ext8_ref, residues=range(1, 8)):
    n = ext8_ref.shape[1] - 8
    for r in residues:
        ext8_ref[r, pl.ds(0, n), :] = ext8_ref[0, pl.ds(r, n), :]


def _fold_rows(x):
    return functools.reduce(lambda p, q: p + q, [x[i:i + 8] for i in range(0, x.shape[0], 8)])


def _shifted(ext8_ref, shift, r0, rows, cols):
    return ext8_ref[shift % 8, pl.ds(pl.multiple_of(shift - shift % 8 + r0, 8), rows), cols]


def _lane_chunk(i):
    return pl.ds(pl.multiple_of(i * CHUNK_LANES, CHUNK_LANES), CHUNK_LANES)


def _sum_terms(terms, ways=4):
    accs = []
    for i, t in enumerate(terms):
        if i < ways:
            accs.append(t)
        else:
            accs[i % ways] = accs[i % ways] + t
    while len(accs) > 1:
        accs = [accs[i] + accs[i + 1] if i + 1 < len(accs) else accs[i] for i in range(0, len(accs), 2)]
    return accs[0]


def _accumulate(ref, val, first):
    @pl.when(first)
    def _():
        ref[...] = val

    @pl.when(jnp.logical_not(first))
    def _():
        ref[...] += val


SCONV_Z_SHIFTS = tuple(SCONV_HALO - (SHORT_CONV_W - 1) + k for k in range(SHORT_CONV_W))


def _sconv_z_taps(zext_ref, r0, cols):
    return [_shifted(zext_ref, shift, r0, CHUNK_ROWS, cols) for shift in SCONV_Z_SHIFTS]


def _weighted(cw_ref, cols, terms):
    return _sum_terms((cw_ref[k:k + 1, cols] * t for k, t in enumerate(terms)), ways=len(terms))


def _sconv_fill_z(zext_ref, main_ref, before_ref, d, m):
    hb = SCONV_HALO
    zb = before_ref[:, d:2 * d].astype(F32) * before_ref[:, 2 * d:].astype(F32)
    zext_ref[pl.ds(0, hb), :] = jnp.where(m > 0, zb, 0.0)
    zext_ref[pl.ds(hb, main_ref.shape[0]), :] = main_ref[:, d:2 * d].astype(F32) * main_ref[:, 2 * d:].astype(F32)


def _sconv_fwd(bcv, cw, name):
    s, d3 = bcv.shape
    d = d3 // 3
    tm = _tile(s, 256, CHUNK_ROWS)
    row_chunks = tm // CHUNK_ROWS

    def body(main_ref, before_ref, cw_ref, p_ref, zext_ref):
        m = pl.program_id(0)
        _sconv_fill_z(zext_ref.at[0], main_ref, before_ref, d, m)
        _build_shifts(zext_ref, [shift % 8 for shift in SCONV_Z_SHIFTS if shift % 8])

        def chunk(i, carry):
            cols = _lane_chunk(i // row_chunks)
            r0 = pl.multiple_of((i % row_chunks) * CHUNK_ROWS, CHUNK_ROWS)
            rows = pl.ds(r0, CHUNK_ROWS)
            zc = _weighted(cw_ref, cols, _sconv_z_taps(zext_ref, r0, cols))
            p_ref[rows, cols] = (main_ref[rows, cols].astype(F32) * zc).astype(p_ref.dtype)
            return carry

        lax.fori_loop(0, row_chunks * (d // CHUNK_LANES), chunk, 0)

    return _hosted(
        body, name=name, grid=(s // tm,),
        in_specs=[_main_spec(tm, d3), _before_spec(tm, SCONV_HALO, d3), _row_spec(d, SHORT_CONV_W)],
        out_specs=_main_spec(tm, d),
        out_shape=jax.ShapeDtypeStruct((s, d), BF16),
        scratch_shapes=[pltpu.VMEM((8, tm + SCONV_HALO, d), F32)],
        compiler_params=_params("parallel"),
    )(bcv, bcv, cw)


def _sconv_bwd(dp, bcv, cw, name):
    s, d3 = bcv.shape
    d = d3 // 3
    tm = _tile(s, 256, CHUNK_ROWS)
    nm = s // tm
    ha = 8
    kw = SHORT_CONV_W

    def body(dp_ref, dpa_ref, main_ref, before_ref, after_ref, cw_ref, o_ref, dcw_ref, zext_ref, dext_ref):
        m = pl.program_id(0)
        _sconv_fill_z(zext_ref.at[0], main_ref, before_ref, d, m)
        _build_shifts(zext_ref, [shift % 8 for shift in SCONV_Z_SHIFTS if shift % 8])
        dext_ref[0, pl.ds(0, tm), :] = dp_ref[...] * main_ref[:, :d].astype(F32)
        dza = dpa_ref[...] * after_ref[:, :d].astype(F32)[0:ha]
        dext_ref[0, pl.ds(tm, ha), :] = jnp.where(m < nm - 1, dza, 0.0)
        _build_shifts(dext_ref, range(1, kw))

        @pl.when(m == 0)
        def _():
            dcw_ref[...] = jnp.zeros_like(dcw_ref)

        zero = jnp.zeros((8, CHUNK_LANES), F32)

        def lane_chunk(ci, carry):
            cols = _lane_chunk(ci)
            c_cols, v_cols = (pl.ds(pl.multiple_of(part * d + ci * CHUNK_LANES, CHUNK_LANES), CHUNK_LANES) for part in (1, 2))

            def row_chunk(ri, sums):
                r0 = pl.multiple_of(ri * CHUNK_ROWS, CHUNK_ROWS)
                rows = pl.ds(r0, CHUNK_ROWS)
                z = _sconv_z_taps(zext_ref, r0, cols)
                o_ref[rows, cols] = (dp_ref[rows, cols] * _weighted(cw_ref, cols, z)).astype(o_ref.dtype)
                dzc = [_shifted(dext_ref, kw - 1 - k, r0, CHUNK_ROWS, cols) for k in range(kw)]
                dz = _weighted(cw_ref, cols, dzc)
                o_ref[rows, c_cols] = (dz * main_ref[rows, v_cols].astype(F32)).astype(o_ref.dtype)
                o_ref[rows, v_cols] = (dz * main_ref[rows, c_cols].astype(F32)).astype(o_ref.dtype)
                return tuple(acc + _fold_rows(dzc[kw - 1] * z[k]) for k, acc in enumerate(sums))

            sums = lax.fori_loop(0, tm // CHUNK_ROWS, row_chunk, (zero,) * kw)
            for k in range(kw):
                dcw_ref[k:k + 1, cols] += _colsum(sums[k])
            return carry

        lax.fori_loop(0, d // CHUNK_LANES, lane_chunk, 0)

    return _hosted(
        body, name=name, grid=(nm,),
        in_specs=[_main_spec(tm, d), _after_spec(tm, ha, d, s), _main_spec(tm, d3), _before_spec(tm, SCONV_HALO, d3),
                  _after_spec(tm, SCONV_HALO, d3, s), _row_spec(d, SHORT_CONV_W)],
        out_specs=[_main_spec(tm, d3), _row_spec(d, 8)],
        out_shape=[jax.ShapeDtypeStruct((s, d3), BF16), jax.ShapeDtypeStruct((8, d), F32)],
        scratch_shapes=[pltpu.VMEM((8, tm + SCONV_HALO, d), F32), pltpu.VMEM((8, tm + ha, d), F32)],
        compiler_params=_params("arbitrary"),
    )(dp, dp, bcv, bcv, bcv, cw)


def _pool_counts(t0, tm, w):
    t = t0 + lax.broadcasted_iota(jnp.int32, (tm, 1), 0)
    return jnp.minimum(t + 1, w).astype(F32)


def _pool_fwd(h, gain, wg, scale, name):
    s, d = h.shape
    ng, cg, _ = wg.shape
    tm = _tile(s, 512, POOL_HALO)

    def body(h_ref, hb_ref, g_ref, wg_ref, sc_ref, o_ref, mx_ref, uext_ref):
        m = pl.program_id(0)
        x = h_ref[...]
        gain_row = g_ref[...]
        xb = hb_ref[...]
        uext_ref[pl.ds(0, POOL_HALO), :] = jnp.where(m > 0, xb * _rms_stats(xb) * gain_row, 0.0)
        uext_ref[pl.ds(POOL_HALO, tm), :] = x * _rms_stats(x) * gain_row
        for gi, win in enumerate(POOL_WINDOWS):
            cols = pl.ds(gi * cg, cg)
            u_g = uext_ref[pl.ds(POOL_HALO, tm), cols]
            acc = u_g
            for i in range(1, win):
                acc = acc + uext_ref[pl.ds(POOL_HALO - i, tm), cols]
            mixed = (acc / _pool_counts(m * tm, tm, win) - u_g).astype(BF16)
            mx_ref[:, cols] = mixed
            o_ref[:, cols] = x[:, gi * cg:(gi + 1) * cg] + _dot(mixed, wg_ref[gi]) * sc_ref[:, cols]

    return _hosted(
        body, name=name, grid=(s // tm,),
        in_specs=[_main_spec(tm, d), _before_spec(tm, POOL_HALO, d), _row_spec(d),
                  pl.BlockSpec((ng, cg, cg), lambda m: (0, 0, 0)), _row_spec(d)],
        out_specs=[_main_spec(tm, d), _main_spec(tm, d)],
        out_shape=[jax.ShapeDtypeStruct((s, d), F32), jax.ShapeDtypeStruct((s, d), BF16)],
        scratch_shapes=[pltpu.VMEM((tm + POOL_HALO, d), F32)],
        compiler_params=_params("parallel"),
    )(h, h, gain, wg, scale)


def _pool_bwd_mm(dh, mixed, wg, scale, name):
    s, d = dh.shape
    ng, cg, _ = wg.shape
    tm = _tile(s, 512)

    def body(dh_ref, mx_ref, wg_ref, sc_ref, dmx_ref, dwg_ref, dsc_ref):
        first = pl.program_id(0) == 0
        for gi in range(ng):
            cols = pl.ds(gi * cg, cg)
            dh_g = dh_ref[:, cols]
            mixed = mx_ref[:, cols]
            w_g = wg_ref[gi]
            dy = (dh_g * sc_ref[:, cols]).astype(BF16)
            dmx_ref[:, cols] = _dot_nt(dy, w_g)
            _accumulate(dsc_ref.at[:, cols], _colsum(dh_g * _dot(mixed, w_g)), first)
            _accumulate(dwg_ref.at[gi], _dot_tn(mixed, dy), first)

    return _hosted(
        body, name=name, grid=(s // tm,),
        in_specs=[_main_spec(tm, d), _main_spec(tm, d), pl.BlockSpec((ng, cg, cg), lambda m: (0, 0, 0)), _row_spec(d)],
        out_specs=[_main_spec(tm, d), pl.BlockSpec((ng, cg, cg), lambda m: (0, 0, 0)), _row_spec(d)],
        out_shape=[jax.ShapeDtypeStruct((s, d), F32), jax.ShapeDtypeStruct((ng, cg, cg), F32),
                   jax.ShapeDtypeStruct((1, d), F32)],
        compiler_params=_params("arbitrary"),
    )(dh, mixed, wg, scale)


def _pool_bwd_rms(dmixed, h, gain, dh, name):
    s, d = h.shape
    cg = d // len(POOL_WINDOWS)
    tm = _tile(s, 512, POOL_HALO)
    nm = s // tm

    def body(dmx_ref, dmxa_ref, h_ref, g_ref, dh_ref, o_ref, dg_ref, eext_ref, du_ref):
        m = pl.program_id(0)
        for gi, win in enumerate(POOL_WINDOWS):
            cols = pl.ds(gi * cg, cg)
            dmx = dmx_ref[:, cols]
            eext_ref[pl.ds(0, tm), cols] = dmx / _pool_counts(m * tm, tm, win)
            ea = dmxa_ref[:, cols] / _pool_counts((m + 1) * tm, POOL_HALO, win)
            eext_ref[pl.ds(tm, POOL_HALO), cols] = jnp.where(m < nm - 1, ea, 0.0)
            acc = -dmx
            for i in range(win):
                acc = acc + eext_ref[pl.ds(i, tm), cols]
            du_ref[:, cols] = acc
        dx, dg = _rms_bwd(du_ref[...], h_ref[...], g_ref[...])
        o_ref[...] = dh_ref[...] + dx
        _accumulate(dg_ref, dg, m == 0)

    return _hosted(
        body, name=name, grid=(nm,),
        in_specs=[_main_spec(tm, d), _after_spec(tm, POOL_HALO, d, s), _main_spec(tm, d), _row_spec(d), _main_spec(tm, d)],
        out_specs=[_main_spec(tm, d), _row_spec(d)],
        out_shape=[jax.ShapeDtypeStruct((s, d), F32), jax.ShapeDtypeStruct((1, d), F32)],
        scratch_shapes=[pltpu.VMEM((tm + POOL_HALO, d), F32), pltpu.VMEM((tm, d), F32)],
        compiler_params=_params("arbitrary"),
    )(dmixed, dmixed, h, gain, dh)


def _conf_fill_h(hext_ref, main_ref, before_ref, d, m):
    hb = before_ref[:, :d].astype(F32) * _sigmoid(before_ref[:, d:].astype(F32))
    hext_ref[pl.ds(0, CONF_HALO), :] = jnp.where(m > 0, hb, 0.0)
    hext_ref[pl.ds(CONF_HALO, main_ref.shape[0]), :] = main_ref[:, :d].astype(F32) * _sigmoid(main_ref[:, d:].astype(F32))


def _layernorm_parts(hc, g, b):
    mu = jnp.mean(hc, axis=-1, keepdims=True)
    xc = hc - mu
    rs = lax.rsqrt(jnp.mean(xc * xc, axis=-1, keepdims=True) + LN_EPS)
    xhat = xc * rs
    return xhat, rs, xhat * g + b


def _conf_mid_fwd(ag, dw, b_dw, ln_g, ln_b, name):
    s, d2 = ag.shape
    d = d2 // 2
    tm = _tile(s, 256, CONF_HALO)
    base = CONF_HALO - (CONF_CONV_W - 1)

    def body(main_ref, before_ref, dw_ref, bdw_ref, g_ref, b_ref, s_ref, hc_ref, hext_ref):
        m = pl.program_id(0)
        _conf_fill_h(hext_ref.at[0], main_ref, before_ref, d, m)
        _build_shifts(hext_ref)
        row_chunks = tm // CHUNK_ROWS

        def conv_chunk(i, carry):
            cols = _lane_chunk(i // row_chunks)
            r0 = pl.multiple_of((i % row_chunks) * CHUNK_ROWS, CHUNK_ROWS)
            taps = (dw_ref[kk:kk + 1, cols] * _shifted(hext_ref, base + kk, r0, CHUNK_ROWS, cols) for kk in range(CONF_CONV_W))
            hc_ref[pl.ds(r0, CHUNK_ROWS), cols] = bdw_ref[:, cols] + _sum_terms(taps, ways=1)
            return carry

        lax.fori_loop(0, row_chunks * (d // CHUNK_LANES), conv_chunk, 0)
        _, _, l = _layernorm_parts(hc_ref[...], g_ref[...], b_ref[...])
        s_ref[...] = (l * _sigmoid(l)).astype(s_ref.dtype)

    return _hosted(
        body, name=name, grid=(s // tm,),
        in_specs=[_main_spec(tm, d2), _before_spec(tm, CONF_HALO, d2), _row_spec(d, CONF_CONV_W), _row_spec(d),
                  _row_spec(d), _row_spec(d)],
        out_specs=[_main_spec(tm, d), _main_spec(tm, d)],
        out_shape=[jax.ShapeDtypeStruct((s, d), BF16), jax.ShapeDtypeStruct((s, d), F32)],
        scratch_shapes=[pltpu.VMEM((8, tm + CONF_HALO, d), F32)],
        compiler_params=_params("parallel"),
    )(ag, ag, dw, b_dw, ln_g, ln_b)


def _conf_out_bwd(dh, w, hc, ln_g, ln_b, name):
    s, d = dh.shape
    tm = _tile(s, 256)

    def body(dh_ref, w_ref, hc_ref, g_ref, b_ref, o_ref, dg_ref, db_ref, dbo_ref):
        first = pl.program_id(0) == 0
        dh_t = dh_ref[...]
        ds = _dot_nt(dh_t.astype(BF16), w_ref[...])
        xhat, rs, l = _layernorm_parts(hc_ref[...], g_ref[...], b_ref[...])
        sg = _sigmoid(l)
        dl = ds * sg * (1.0 + l * (1.0 - sg))
        dxh = dl * g_ref[...]
        o_ref[...] = rs * (dxh - jnp.mean(dxh, axis=-1, keepdims=True)
                           - xhat * jnp.mean(dxh * xhat, axis=-1, keepdims=True))
        _accumulate(dg_ref, _colsum(dl * xhat), first)
        _accumulate(db_ref, _colsum(dl), first)
        _accumulate(dbo_ref, _colsum(dh_t), first)

    return _hosted(
        body, name=name, grid=(s // tm,),
        in_specs=[_main_spec(tm, d), pl.BlockSpec((d, d), lambda m: (0, 0)), _main_spec(tm, d), _row_spec(d), _row_spec(d)],
        out_specs=[_main_spec(tm, d), _row_spec(d), _row_spec(d), _row_spec(d)],
        out_shape=[jax.ShapeDtypeStruct((s, d), F32)] + [jax.ShapeDtypeStruct((1, d), F32)] * 3,
        compiler_params=_params("arbitrary"),
    )(dh, w, hc, ln_g, ln_b)


def _conf_mid_bwd(dhc, ag, dw, name):
    s, d2 = ag.shape
    d = d2 // 2
    tm = _tile(s, 256, CONF_HALO)
    nm = s // tm
    kw = CONF_CONV_W
    base = CONF_HALO - (kw - 1)

    def body(dhc_ref, dhca_ref, main_ref, before_ref, dw_ref, o_ref, ddw_ref, dbdw_ref, dbpw_ref, hext_ref, dext_ref):
        m = pl.program_id(0)
        first = m == 0
        _conf_fill_h(hext_ref.at[0], main_ref, before_ref, d, m)
        _build_shifts(hext_ref)
        dext_ref[0, pl.ds(0, tm), :] = dhc_ref[...]
        dext_ref[0, pl.ds(tm, CONF_HALO), :] = jnp.where(m < nm - 1, dhca_ref[...], 0.0)
        _build_shifts(dext_ref)

        @pl.when(first)
        def _():
            ddw_ref[...] = jnp.zeros_like(ddw_ref)
            dbdw_ref[...] = jnp.zeros_like(dbdw_ref)
            dbpw_ref[...] = jnp.zeros_like(dbpw_ref)

        zero = jnp.zeros((8, CHUNK_LANES), F32)
        tap_group = 8

        def fold(x):
            return functools.reduce(lambda p, q: p + q, [x[i:i + 8] for i in range(0, CHUNK_ROWS, 8)])

        def lane_chunk(ci, carry):
            cols = _lane_chunk(ci)
            gate_cols = pl.ds(pl.multiple_of(d + ci * CHUNK_LANES, CHUNK_LANES), CHUNK_LANES)

            def through_conv(ri, sums):
                r0 = pl.multiple_of(ri * CHUNK_ROWS, CHUNK_ROWS)
                rows = pl.ds(r0, CHUNK_ROWS)
                dhh = _sum_terms((dw_ref[kk:kk + 1, cols] * _shifted(dext_ref, kw - 1 - kk, r0, CHUNK_ROWS, cols)
                                  for kk in range(kw)), ways=1)
                a = main_ref[rows, cols].astype(F32)
                sg = _sigmoid(main_ref[rows, gate_cols].astype(F32))
                da = dhh * sg
                dgate = dhh * a * sg * (1.0 - sg)
                o_ref[rows, cols] = da.astype(o_ref.dtype)
                o_ref[rows, gate_cols] = dgate.astype(o_ref.dtype)
                return sums[0] + fold(da), sums[1] + fold(dgate), sums[2] + fold(dext_ref[0, rows, cols])

            sum_da, sum_dgate, sum_dhc = lax.fori_loop(0, tm // CHUNK_ROWS, through_conv, (zero, zero, zero))
            dbdw_ref[:, cols] += _colsum(sum_dhc)
            dbpw_ref[:, cols] += _colsum(sum_da)
            dbpw_ref[:, gate_cols] += _colsum(sum_dgate)

            for k0 in range(0, kw, tap_group):
                group = range(k0, min(k0 + tap_group, kw))

                def tap_gradients(ri, accs, group=group):
                    for sub in range(0, CHUNK_ROWS, 8):
                        r0 = pl.multiple_of(ri * CHUNK_ROWS + sub, 8)
                        dhc_c = dext_ref[0, pl.ds(r0, 8), cols]
                        accs = tuple(acc + dhc_c * _shifted(hext_ref, base + kk, r0, 8, cols) for kk, acc in zip(group, accs))
                    return accs

                accs = lax.fori_loop(0, tm // CHUNK_ROWS, tap_gradients, (zero,) * len(group))
                for kk, acc in zip(group, accs):
                    ddw_ref[kk:kk + 1, cols] += _colsum(acc)
            return carry

        lax.fori_loop(0, d // CHUNK_LANES, lane_chunk, 0)

    return _hosted(
        body, name=name, grid=(nm,),
        in_specs=[_main_spec(tm, d), _after_spec(tm, CONF_HALO, d, s), _main_spec(tm, d2), _before_spec(tm, CONF_HALO, d2),
                  _row_spec(d, kw)],
        out_specs=[_main_spec(tm, d2), _row_spec(d, 32), _row_spec(d), _row_spec(d2)],
        out_shape=[jax.ShapeDtypeStruct((s, d2), BF16), jax.ShapeDtypeStruct((32, d), F32),
                   jax.ShapeDtypeStruct((1, d), F32), jax.ShapeDtypeStruct((1, d2), F32)],
        scratch_shapes=[pltpu.VMEM((8, tm + CONF_HALO, d), F32), pltpu.VMEM((8, tm + CONF_HALO, d), F32)],
        compiler_params=_params("arbitrary"),
    )(dhc, dhc, ag, ag, dw)


def _loss_head(h, gain, target, name):
    s, d = h.shape
    tm = _tile(s, 512)

    def body(h_ref, g_ref, t_ref, loss_ref, dh_ref, dg_ref):
        first = pl.program_id(0) == 0
        x = h_ref[...]
        err = x * _rms_stats(x) * g_ref[...] - t_ref[...]
        part = 0.5 * jnp.sum(jnp.mean(err * err, axis=-1, keepdims=True), axis=0, keepdims=True)
        dx, dg = _rms_bwd(err * (1.0 / d), x, g_ref[...])
        dh_ref[...] = dx
        _accumulate(loss_ref, part, first)
        _accumulate(dg_ref, dg, first)

    return _hosted(
        body, name=name, grid=(s // tm,),
        in_specs=[_main_spec(tm, d), _row_spec(d), _main_spec(tm, d)],
        out_specs=[pl.BlockSpec((1, 1), lambda m: (0, 0)), _main_spec(tm, d), _row_spec(d)],
        out_shape=[jax.ShapeDtypeStruct((1, 1), F32), jax.ShapeDtypeStruct((s, d), F32), jax.ShapeDtypeStruct((1, d), F32)],
        compiler_params=_params("arbitrary"),
    )(h, gain, target)


def _ffn_fwd(h, wts, i):
    u, act, s1, q1 = _ffn_up(h, wts[f"ln2_{i}"], wts[f"ffn{i}_w_gu"], f"ffn{i}_up")
    h_new = _mm_row(act, wts[f"ffn{i}_w_down"], h, None, f"ffn{i}_down")
    return h_new, (h, u, act, s1, q1)


def _ffn_bwd(dh, saved, wts, i, g):
    h, u, act, s1, q1 = saved
    dgu = _ffn_down_bwd(dh, wts[f"ffn{i}_w_down"], s1, q1, f"ffn{i}_down_bwd")
    g[f"ffn{i}_w_down"] = _mm_tn(act, dh, 1, f"ffn{i}_dw_down")
    g[f"ffn{i}_w_gu"] = _mm_tn(u, dgu, N_CHIPS, f"ffn{i}_dw_gu")
    dh_new, g[f"ln2_{i}"] = _mm_nt_col_rms_bwd(dgu, wts[f"ffn{i}_w_gu"], h, wts[f"ln2_{i}"], dh, f"ffn{i}_up_bwd")
    return dh_new


def _device_step(x, target, wts, g=None):
    g = {} if g is None else g
    saved = {}
    h = x

    def short_conv_fwd(h, i):
        u, bcv = _mm_col(h, wts[f"ln1_{i}"], wts[f"a{i}_w_in"], None, f"a{i}_in")
        p = _sconv_fwd(bcv, wts[f"a{i}_conv"], f"a{i}_conv")
        return _mm_row(p, wts[f"a{i}_w_out"], h, None, f"a{i}_out"), (h, u, bcv, p)

    def short_conv_bwd(dh, sv, i):
        h, u, bcv, p = sv
        dp = _mm_nt_row(dh, wts[f"a{i}_w_out"], f"a{i}_out_bwd")
        dbcv, dcw = _sconv_bwd(dp, bcv, wts[f"a{i}_conv"], f"a{i}_conv_bwd")
        g[f"a{i}_conv"] = dcw[:SHORT_CONV_W]
        g[f"a{i}_w_in"] = _mm_tn(u, dbcv, N_CHIPS, f"a{i}_dw_in")
        g[f"a{i}_w_out"] = _mm_tn(p, dh, 1, f"a{i}_dw_out")
        dh, g[f"ln1_{i}"] = _mm_nt_col_rms_bwd(dbcv, wts[f"a{i}_w_in"], h, wts[f"ln1_{i}"], dh, f"a{i}_in_bwd")
        return dh

    h, saved["a0"] = short_conv_fwd(h, 0)
    h, saved["f0"] = _ffn_fwd(h, wts, 0)

    h_in = h
    h, mixed = _pool_fwd(h, wts["ln1_1"], wts["b1_w_grp"], wts["b1_scale"], "b1_fwd")
    saved["b1"] = (h_in, mixed)
    h, saved["f1"] = _ffn_fwd(h, wts, 1)

    h_in = h
    u, ag = _mm_col(h, wts["ln1_2"], wts["c2_w_pw1"], wts["c2_b_pw1"], "c2_pw1")
    sw, hc = _conf_mid_fwd(ag, wts["c2_dw"], wts["c2_b_dw"], wts["c2_ln_g"], wts["c2_ln_b"], "c2_mid")
    h = _mm_row(sw, wts["c2_w_pw2"], h, wts["c2_b_pw2"], "c2_pw2")
    saved["c2"] = (h_in, u, ag, sw, hc)
    h, saved["f2"] = _ffn_fwd(h, wts, 2)

    h, saved["a3"] = short_conv_fwd(h, 3)
    h, saved["f3"] = _ffn_fwd(h, wts, 3)

    loss, dh, g["ln_f"] = _loss_head(h, wts["ln_f"], target, "loss_head")

    def ffn_bwd(dh, i):
        return _ffn_bwd(dh, saved[f"f{i}"], wts, i, g)

    dh = ffn_bwd(dh, 3)
    dh = short_conv_bwd(dh, saved["a3"], 3)

    dh = ffn_bwd(dh, 2)
    h_in, u, ag, sw, hc = saved["c2"]
    dhc, g["c2_ln_g"], g["c2_ln_b"], g["c2_b_pw2"] = _conf_out_bwd(
        dh, wts["c2_w_pw2"], hc, wts["c2_ln_g"], wts["c2_ln_b"], "c2_pw2_bwd")
    g["c2_w_pw2"] = _mm_tn(sw, dh, 1, "c2_dw_pw2")
    dag, ddw, g["c2_b_dw"], g["c2_b_pw1"] = _conf_mid_bwd(dhc, ag, wts["c2_dw"], "c2_mid_bwd")
    g["c2_dw"] = ddw[:CONF_CONV_W]
    g["c2_w_pw1"] = _mm_tn(u, dag, N_CHIPS, "c2_dw_pw1")
    dh, g["ln1_2"] = _mm_nt_col_rms_bwd(dag, wts["c2_w_pw1"], h_in, wts["ln1_2"], dh, "c2_pw1_bwd")

    dh = ffn_bwd(dh, 1)
    h_in, mixed = saved["b1"]
    dmixed, g["b1_w_grp"], g["b1_scale"] = _pool_bwd_mm(dh, mixed, wts["b1_w_grp"], wts["b1_scale"], "b1_bwd_mm")
    dh, g["ln1_1"] = _pool_bwd_rms(dmixed, h_in, wts["ln1_1"], dh, "b1_bwd_rms")

    dh = ffn_bwd(dh, 0)
    dh = short_conv_bwd(dh, saved["a0"], 0)
    return loss, dh, g


MESH = pl.DeviceIdType.MESH
ANY = pl.BlockSpec(memory_space=pl.ANY)


def _position():
    return lax.axis_index("x"), lax.axis_index("y"), lax.axis_index("c")


def _other_chips(x, y):
    return [(1 - x, y), (x, 1 - y), (1 - x, 1 - y)]


def _remote(src, dst, send_sem, recv_sem, to):
    return pltpu.make_async_remote_copy(src_ref=src, dst_ref=dst, send_sem=send_sem, recv_sem=recv_sem,
                                        device_id=to, device_id_type=MESH)


def _half_rows(ref_rows, c):
    hr = ref_rows // 2
    return pl.ds(pl.multiple_of(c * hr, 16), hr)


def _allgather8(v, name):
    m_per, n = v.shape

    def body(v_ref, out_ref, send_sems, recv_sems, local_sem):
        x, y, c = _position()
        me, sibling = (x, y, c), (x, y, 1 - c)
        chips = _other_chips(x, y)

        def rows(px, py, pc):
            return out_ref.at[pl.ds((4 * px + 2 * py + pc) * m_per, m_per), :]

        def copy(k, block, to, src=None):
            return _remote(rows(*block) if src is None else src, rows(*block), send_sems.at[k], recv_sems.at[k], to)

        mine = pltpu.make_async_copy(v_ref, rows(*me), local_sem)
        mine.start()
        first = [copy(0, me, sibling, src=v_ref)]
        first += [copy(1 + j, me, (*chip, c), src=v_ref) for j, chip in enumerate(chips)]
        for cp in first:
            cp.start()
        passed = [copy(4 + j, (*chip, c), sibling) for j, chip in enumerate(chips)]
        for j, chip in enumerate(chips):
            copy(1 + j, (*chip, c), me).wait_recv()
            passed[j].start()
        copy(0, sibling, me).wait_recv()
        for j, chip in enumerate(chips):
            copy(4 + j, (*chip, 1 - c), me).wait_recv()
        for cp in first + passed:
            cp.wait_send()
        mine.wait()

    return _hosted(
        body, name=name,
        out_shape=jax.ShapeDtypeStruct((N_DEV * m_per, n), v.dtype),
        in_specs=[pl.BlockSpec(memory_space=pltpu.VMEM)],
        out_specs=pl.BlockSpec(memory_space=pltpu.VMEM),
        scratch_shapes=[pltpu.SemaphoreType.DMA((7,)), pltpu.SemaphoreType.DMA((7,)), pltpu.SemaphoreType.DMA],
        compiler_params=pltpu.CompilerParams(vmem_limit_bytes=VMEM_LIMIT),
    )(v)


def _cast_to_slot(ws, idx, name):
    r, cols = ws[0].shape
    assert all(w.shape == (r, cols) for w in ws)
    n = len(ws)
    tr = _tile(r, 256, 16)

    def body(idx_ref, *refs):
        for w_ref, o_ref in zip(refs[:n], refs[n:]):
            o_ref[...] = w_ref[...].astype(o_ref.dtype)

    return _hosted(
        body, name=name,
        grid_spec=pltpu.PrefetchScalarGridSpec(
            num_scalar_prefetch=1, grid=(r // tr,),
            in_specs=[pl.BlockSpec((tr, cols), lambda t, idx_ref: (t, 0))] * n,
            out_specs=[pl.BlockSpec((None, tr, cols), lambda t, idx_ref: (idx_ref[0], t, 0))] * n),
        out_shape=[jax.ShapeDtypeStruct((N_CHIPS, r, cols), BF16)] * n,
        compiler_params=_params("parallel"),
    )(idx, *ws)


def _dma_sems(*shape):
    return [pltpu.SemaphoreType.DMA(shape), pltpu.SemaphoreType.DMA(shape)]


def _same_shapes(arrays):
    return [jax.ShapeDtypeStruct(a.shape, a.dtype) for a in arrays]


def _part_rows(ref_rows, c, part):
    hr = ref_rows // 2
    i, n = part
    size = hr // n
    assert size * n == hr and size % 16 == 0, (ref_rows, part)
    return pl.ds(pl.multiple_of(c * hr + i * size, 16), size)


def _task_gather_ici(bufs, done, part=(0, 1)):
    n = len(bufs)

    def copies(outs, sems, landing):
        x, y, c = _position()
        my_chip = 2 * x + y
        res = []
        for i in range(n):
            rows = _part_rows(bufs[i].shape[1], c, part)
            for r, (px, py) in enumerate(_other_chips(x, y)):
                slot = (2 * px + py) if landing else my_chip
                res.append(_remote(outs[i].at[my_chip, rows, :], outs[i].at[slot, rows, :], sems[0].at[i, r], sems[1].at[i, r],
                                   (px, py, c)))
        return res

    def start(ins, outs, sems):
        for cp in copies(outs, sems, False):
            cp.start()

    def wait(ins, outs, sems):
        for cp in copies(outs, sems, True):
            cp.wait_recv()
            cp.wait_send()

    return _Task(bufs, _same_shapes(bufs), {i: i for i in range(n)}, _dma_sems(n, 3), start, wait, done)


def _task_gather_d2d(bufs, done):
    n = len(bufs)

    def copies(outs, sems, landing):
        x, y, c = _position()
        res = []
        for i in range(n):
            rows = _half_rows(bufs[i].shape[1], (1 - c) if landing else c)
            for r, (px, py) in enumerate(_other_chips(x, y)):
                part = outs[i].at[2 * px + py, rows, :]
                res.append(_remote(part, part, sems[0].at[i, r], sems[1].at[i, r], (x, y, 1 - c)))
        return res

    def start(ins, outs, sems):
        for cp in copies(outs, sems, False):
            cp.start()

    def wait(ins, outs, sems):
        for cp in copies(outs, sems, True):
            cp.wait_recv()
        for cp in copies(outs, sems, False):
            cp.wait_send()

    return _Task(bufs, _same_shapes(bufs), {i: i for i in range(n)}, _dma_sems(n, 3), start, wait, done)


def _task_sibling_halves(grads, done):
    n = len(grads)

    def copies(ins, outs, sems):
        x, y, c = _position()
        return [_remote(ins[i].at[:, _half_rows(grads[i].shape[1], 1 - c), :], outs[i], sems[0].at[i], sems[1].at[i],
                        (x, y, 1 - c)) for i in range(n)]

    def start(ins, outs, sems):
        for cp in copies(ins, outs, sems):
            cp.start()

    def wait(ins, outs, sems):
        for cp in copies(ins, outs, sems):
            cp.wait()

    shapes = [jax.ShapeDtypeStruct((g.shape[0], g.shape[1] // 2, g.shape[2]), g.dtype) for g in grads]
    return _Task(grads, shapes, {}, _dma_sems(n), start, wait, done)


def _task_chip_sums(parts, done, landed=None, part=(0, 1)):
    n = len(parts)
    i_part, n_parts = part
    sizes = [p.shape[1] // n_parts for p in parts]
    assert all(p.shape[1] == size * n_parts and size % 16 == 0 for p, size in zip(parts, sizes)), part
    rows = [pl.ds(i_part * size, size) for size in sizes]

    def copies(ins, outs, sems):
        x, y, c = _position()
        return [_remote(ins[i].at[2 * px + py, rows[i], :], outs[i].at[r, rows[i], :], sems[0].at[i, r], sems[1].at[i, r],
                        (px, py, c))
                for i in range(n) for r, (px, py) in enumerate(_other_chips(x, y))]

    def start(ins, outs, sems):
        for cp in copies(ins, outs, sems):
            cp.start()

    def wait(ins, outs, sems):
        for cp in copies(ins, outs, sems):
            cp.wait()

    shapes = [jax.ShapeDtypeStruct((3,) + p.shape[1:], p.dtype) for p in parts]
    if landed is None:
        return _Task(parts, shapes, {}, _dma_sems(n, 3), start, wait, done)
    return _Task(list(parts) + list(landed), shapes, {n + i: i for i in range(n)}, _dma_sems(n, 3), start, wait, done)


def _task_sibling_parts(owns, landeds, done):
    n = len(owns)

    def copies(ins, outs, sems):
        x, y, c = _position()
        sibling = (x, y, 1 - c)
        res = []
        for i in range(n):
            res.append(_remote(ins[i].at[2 * x + y], outs[i].at[0], sems[0].at[i, 0], sems[1].at[i, 0], sibling))
            res.append(_remote(ins[n + i], outs[i].at[pl.ds(1, 3)], sems[0].at[i, 1], sems[1].at[i, 1], sibling))
        return res

    def start(ins, outs, sems):
        for cp in copies(ins, outs, sems):
            cp.start()

    def wait(ins, outs, sems):
        for cp in copies(ins, outs, sems):
            cp.wait()

    return _Task(list(owns) + list(landeds), _same_shapes(owns), {}, _dma_sems(n, 2), start, wait, done)


def _add_halves(grads, sibs, idx, name):
    n = len(grads)
    nsh = grads[0].shape[0]

    def body(idx_ref, *refs):
        for g_ref, s_ref, o_ref in zip(refs[:n], refs[n:2 * n], refs[2 * n:]):
            o_ref[...] = (g_ref[...].astype(F32) + s_ref[...].astype(F32)).astype(o_ref.dtype)

    half_of = [pl.BlockSpec((None, s.shape[1], s.shape[2]), lambda j, idx_ref: (j, idx_ref[1], 0)) for s in sibs]
    whole = [pl.BlockSpec((None, s.shape[1], s.shape[2]), lambda j, idx_ref: (j, 0, 0)) for s in sibs]
    return _hosted(
        body, name=name,
        grid_spec=pltpu.PrefetchScalarGridSpec(num_scalar_prefetch=1, grid=(nsh,), in_specs=half_of + whole, out_specs=whole),
        out_shape=_same_shapes(sibs),
        compiler_params=_params("parallel"),
    )(idx, *grads, *sibs)


def _adamw_reduced(w, own, landed, sib, m, v, idx, name):
    r, cols = w.shape
    hr = r // 2
    tr = _tile(hr, 256, 16)
    nt = hr // tr

    def body(idx_ref, w_ref, p_ref, l_ref, s_ref, m_ref, v_ref, go_ref, d_ref, mo_ref, vo_ref):
        mine = p_ref[...].astype(F32)
        for k in range(3):
            mine = mine + l_ref[k].astype(F32)
        theirs = s_ref[0].astype(F32)
        for k in range(1, 4):
            theirs = theirs + s_ref[k].astype(F32)
        grad = jnp.where(pl.program_id(0) // nt == idx_ref[1], mine, theirs)
        go_ref[...] = grad
        d_ref[...], mo_ref[...], vo_ref[...] = _adamw_update(w_ref[...], grad, m_ref[...], v_ref[...])

    def in_half(t, half):
        return jnp.clip(t - half * nt, 0, nt - 1)

    full = pl.BlockSpec((tr, cols), lambda t, idx_ref: (t, 0))
    return _hosted(
        body, name=name,
        grid_spec=pltpu.PrefetchScalarGridSpec(
            num_scalar_prefetch=1, grid=(2 * nt,),
            in_specs=[full,
                      pl.BlockSpec((None, tr, cols), lambda t, idx_ref: (idx_ref[0], in_half(t, idx_ref[1]), 0)),
                      pl.BlockSpec((3, tr, cols), lambda t, idx_ref: (0, in_half(t, idx_ref[1]), 0)),
                      pl.BlockSpec((4, tr, cols), lambda t, idx_ref: (0, in_half(t, 1 - idx_ref[1]), 0)),
                      full, full],
            out_specs=[full] * 4),
        out_shape=[jax.ShapeDtypeStruct((r, cols), F32)] * 4,
        compiler_params=_params("arbitrary"),
    )(idx, w, own, landed, sib, m, v)


def _sum_devices(blocks, name):
    m8, n = blocks.shape
    m = m8 // N_DEV

    def body(b_ref, o_ref):
        acc = b_ref[pl.ds(0, m), :]
        for k in range(1, N_DEV):
            acc = acc + b_ref[pl.ds(k * m, m), :]
        o_ref[...] = acc

    return _hosted(
        body, name=name, out_shape=jax.ShapeDtypeStruct((m, n), F32),
        in_specs=[pl.BlockSpec(memory_space=pltpu.VMEM)], out_specs=pl.BlockSpec(memory_space=pltpu.VMEM),
        compiler_params=pltpu.CompilerParams(vmem_limit_bytes=VMEM_LIMIT),
    )(blocks)


def _adamw_update(w, grad, m, v):
    new_m = ADAM_B1 * m + (1.0 - ADAM_B1) * grad
    new_v = ADAM_B2 * v + (1.0 - ADAM_B2) * (grad * grad)
    m_hat = new_m * (1.0 / (1.0 - ADAM_B1 ** ADAM_STEP))
    v_hat = new_v * (1.0 / (1.0 - ADAM_B2 ** ADAM_STEP))
    return -ADAM_LR * (m_hat / (jnp.sqrt(v_hat) + ADAM_EPS) + ADAM_WD * w), new_m, new_v


def _adamw_small(grad_blocks, params, name):
    nb, npar = len(grad_blocks), len(params)

    def body(*refs):
        blocks, ins, outs = refs[:nb], refs[nb:nb + 3 * npar], refs[nb + 3 * npar:]
        for p, (w, _, _, blk, row0) in enumerate(params):
            if w.ndim == 1:
                tiled = (w.shape[0] // LANES, LANES)
                grad = blocks[blk][pl.ds(row0, tiled[0]), pl.ds(0, LANES)]
                wmv = [ins[3 * p + k][...].reshape(tiled) for k in range(3)]
            else:
                grad = blocks[blk][pl.ds(row0, w.shape[0]), :]
                wmv = [ins[3 * p + k][...] for k in range(3)]
            for k, res in enumerate((grad,) + _adamw_update(wmv[0], grad, wmv[1], wmv[2])):
                outs[4 * p + k][...] = res.reshape(w.shape)

    args = list(grad_blocks) + [a for w, m, v, _, _ in params for a in (w, m, v)]
    vmem = pl.BlockSpec(memory_space=pltpu.VMEM)
    out = _hosted(
        body, name=name, in_specs=[vmem] * len(args), out_specs=[vmem] * (4 * npar),
        out_shape=[jax.ShapeDtypeStruct(w.shape, F32) for w, _, _, _, _ in params for _ in range(4)],
    )(*args)
    return [tuple(out[4 * p:4 * p + 4]) for p in range(npar)]


WEIGHT_NAMES = (
    "ln1_0", "a0_w_in", "a0_conv", "a0_w_out", "ln2_0", "ffn0_w_gu", "ffn0_w_down",
    "ln1_1", "b1_w_grp", "b1_scale", "ln2_1", "ffn1_w_gu", "ffn1_w_down",
    "ln1_2", "c2_w_pw1", "c2_b_pw1", "c2_dw", "c2_b_dw", "c2_ln_g", "c2_ln_b", "c2_w_pw2", "c2_b_pw2",
    "ln2_2", "ffn2_w_gu", "ffn2_w_down",
    "ln1_3", "a3_w_in", "a3_conv", "a3_w_out", "ln2_3", "ffn3_w_gu", "ffn3_w_down", "ln_f")
BIG = ("a0_w_in", "a0_w_out", "ffn0_w_gu", "ffn0_w_down", "b1_w_grp", "ffn1_w_gu", "ffn1_w_down", "c2_w_pw1", "c2_w_pw2",
       "ffn2_w_gu", "ffn2_w_down", "a3_w_in", "a3_w_out", "ffn3_w_gu", "ffn3_w_down")
GROUPED = "b1_w_grp"
SMALL_SHARDED = ("a0_conv", "a3_conv", "c2_dw")
REPLICATED = tuple(n for n in WEIGHT_NAMES if n not in BIG and n not in SMALL_SHARDED)


def _pad_rows(a, mult=8):
    pad = -a.shape[0] % mult
    return a if pad == 0 else jnp.concatenate([a, jnp.zeros((pad, a.shape[1]), a.dtype)], axis=0)


def _pack_rows(parts, width):
    rows = [p.reshape(-1, width) for p in parts]
    return _pad_rows(jnp.concatenate(rows, axis=0)), [r.shape[0] for r in rows]


def _unpack_rows(packed, counts, shapes):
    out, at = [], 0
    for n, shp in zip(counts, shapes):
        out.append(packed[at:at + n].reshape(shp))
        at += n
    return out


COLUMN_SHARDED = ("w_in", "w_gu", "w_pw1")


class _Weights(dict):
    def __init__(self, bufs):
        super().__init__()
        self.bufs = bufs

    def __missing__(self, name):
        buf = self.bufs[name]
        if name == GROUPED:
            cg = buf.shape[-1]
            rq = cg // N_CHIPS
            return jnp.transpose(buf.reshape(N_CHIPS, -1, rq, cg), (1, 0, 2, 3)).reshape(-1, cg, cg)
        return buf if name.endswith(COLUMN_SHARDED) else buf.reshape(-1, buf.shape[-1])


class _Exchange:
    def __init__(self, w, mom, vel, idx):
        def shards(table):
            return {n: table[n].reshape(-1, table[n].shape[-1]) for n in BIG}

        self.w, self.mom, self.vel, self.idx = shards(w), shards(mom), shards(vel), idx
        self.bufs = {}
        self.weights = _Weights(self.bufs)
        self.grads = {}
        self.sib, self.part, self.landed, self.sib_parts, self.updates = {}, {}, {}, {}, {}

    def cast(self, names):
        by_shape = {}
        for n in names:
            by_shape.setdefault(self.w[n].shape, []).append(n)
        for group in by_shape.values():
            self.bufs.update(zip(group, _cast_to_slot([self.w[n] for n in group], self.idx, f"cast_{group[0]}")))

    @staticmethod
    def _store(table, names):
        def done(arrays):
            table.update(zip(names, arrays))
        return done

    def _grad(self, n):
        g = self.grads[n]
        if n == GROUPED:
            ng, cg, _ = g.shape
            g = jnp.transpose(g.reshape(ng, N_CHIPS, cg // N_CHIPS, cg), (1, 0, 2, 3)).astype(BF16)
        return g.reshape(N_CHIPS, -1, g.shape[-1])

    def gather_ici(self, *names, part=(0, 1)):
        return lambda: _task_gather_ici([self.bufs[n] for n in names], self._store(self.bufs, names), part)

    def gather_d2d(self, *names):
        return lambda: _task_gather_d2d([self.bufs[n] for n in names], self._store(self.bufs, names))

    def sibling_halves(self, *names):
        return lambda: _task_sibling_halves([self._grad(n) for n in names], self._store(self.sib, names))

    def add_halves(self, *names):
        def run():
            parts = _add_halves([self._grad(n) for n in names], [self.sib.pop(n) for n in names], self.idx,
                                f"reduce_add_{names[0]}")
            self.part.update(zip(names, parts))
        return run

    def chip_sums(self, *names, part=(0, 1)):
        def make():
            landed = [self.landed[n] for n in names] if part[0] > 0 else None
            return _task_chip_sums([self.part[n] for n in names], self._store(self.landed, names), landed, part)
        return make

    def sibling_parts(self, *names):
        return lambda: _task_sibling_parts([self.part[n] for n in names], [self.landed[n] for n in names],
                                           self._store(self.sib_parts, names))

    def adamw(self, *names):
        def run():
            for n in names:
                self.updates[n] = _adamw_reduced(self.w[n], self.part.pop(n), self.landed.pop(n), self.sib_parts.pop(n),
                                                 self.mom[n], self.vel[n], self.idx, f"adamw_{n}")
        return run


def _plan(ex):
    s = _Schedule()

    def ffn(i):
        return f"ffn{i}_w_gu", f"ffn{i}_w_down"

    c2, a3 = ("c2_w_pw1", "c2_w_pw2"), ("a3_w_in", "a3_w_out")
    first, second = (0, 2), (1, 2)
    s.host("cast_ffn0_w_gu", ex.gather_ici("a0_w_in", part=first))
    s.host("cast_ffn0_w_down", ex.gather_ici("a0_w_in", part=second))
    s.host("cast_c2_w_pw1", ex.gather_d2d("a0_w_in"))
    gu, down = ffn(0)
    s.host("gather_small", ex.gather_ici("a0_w_out"))
    s.host("a0_in", ex.gather_ici(gu, part=first), ex.gather_d2d("a0_w_out"))
    s.host("a0_conv", ex.gather_ici(gu, part=second))
    s.host("a0_out", ex.gather_ici(down), ex.gather_d2d(gu))
    s.host("ffn0_up", ex.gather_d2d(down))
    gu, down = ffn(1)
    s.host("ffn0_up", ex.gather_ici(gu, GROUPED))
    s.host("ffn0_down", ex.gather_ici(down), ex.gather_d2d(gu, GROUPED))
    s.host("ffn1_up", ex.gather_d2d(down), ex.gather_ici(*c2))
    gu, down = ffn(2)
    s.host("ffn1_up", ex.gather_ici(gu, part=first))
    s.host("ffn1_down", ex.gather_d2d(*c2), ex.gather_ici(down))
    s.host("c2_mid", ex.gather_ici(gu, part=second))
    s.host("c2_pw2", ex.gather_d2d(gu, down), ex.gather_ici(a3[1]))
    s.host("ffn2_up", ex.gather_ici(a3[0]))
    gu, down = ffn(3)
    s.host("ffn2_up", ex.gather_ici(gu, part=first))
    s.host("ffn2_down", ex.gather_d2d(*a3), ex.gather_ici(down))
    s.host("a3_in", ex.gather_ici(gu, part=second))
    s.host("a3_out", ex.gather_d2d(gu, down))

    def reduce_on(names, first_host, ici_hosts, last_host):
        s.host(first_host, ex.sibling_halves(*names))
        s.post(first_host, ex.add_halves(*names))
        for host, hosted, part in ici_hosts:
            s.host(host, ex.chip_sums(*hosted, part=part))
        s.host(last_host, ex.sibling_parts(*names))
        s.post(last_host, ex.adamw(*names))

    whole = (0, 1)
    gu, down = ffn(3)
    reduce_on((gu, down), "a3_out_bwd",
              [("a3_conv_bwd", (down,), whole), ("a3_dw_in", (gu,), first), ("a3_in_bwd", (gu,), second)], "ffn2_down_bwd")
    reduce_on(a3, "ffn2_down_bwd", [("ffn2_dw_gu", a3, whole)], "c2_pw2_bwd")
    gu, down = ffn(0)
    s.host("ffn0_dw_gu", ex.sibling_halves(down))
    s.post("ffn0_dw_gu", ex.add_halves(down))
    s.host("ffn0_up_bwd", ex.chip_sums(down))
    s.host("a0_out_bwd", ex.sibling_halves(gu, GROUPED))
    s.post("a0_out_bwd", ex.add_halves(gu, GROUPED))
    s.host("a0_conv_bwd", ex.chip_sums(gu, part=first), ex.chip_sums(GROUPED))
    s.host("a0_dw_in", ex.chip_sums(gu, part=second))
    s.host("a0_dw_out", ex.sibling_halves("a0_w_in"))
    s.post("a0_dw_out", ex.add_halves("a0_w_in"))
    s.host("a0_in_bwd", ex.chip_sums("a0_w_in"), ex.sibling_parts(gu, down, GROUPED))
    s.host("gather_small_grads", ex.chip_sums("a0_w_out"))
    s.host("sum_small_grads", ex.sibling_parts("a0_w_out"))
    reduce_on(ffn(2), "c2_pw2_bwd", [("c2_mid_bwd", ffn(2), whole)], "ffn1_down_bwd")
    reduce_on(c2, "ffn1_down_bwd", [("ffn1_dw_down", c2, whole)], "b1_bwd_mm")
    gu, down = ffn(1)
    reduce_on((gu, down), "b1_bwd_mm", [("ffn0_down_bwd", (down,), whole), ("ffn0_dw_gu", (gu,), whole)], "ffn0_up_bwd")
    return s


def kernel(x, *rest):
    nw = len(WEIGHT_NAMES)
    w = dict(zip(WEIGHT_NAMES, rest[:nw]))
    target = rest[nw]
    mom = dict(zip(WEIGHT_NAMES, rest[nw + 1:2 * nw + 1]))
    vel = dict(zip(WEIGHT_NAMES, rest[2 * nw + 1:3 * nw + 1]))
    cx, cy, cc = _position()
    my_chip = 2 * cx + cy
    ex = _Exchange(w, mom, vel, jnp.stack([my_chip, cc]).astype(jnp.int32))
    _ACTIVE_SCHEDULE[0] = _plan(ex)
    try:
        return _scheduled_step(x, target, w, mom, vel, ex, my_chip)
    finally:
        _ACTIVE_SCHEDULE[0] = None


def _scheduled_step(x, target, w, mom, vel, ex, my_chip):
    d = x.shape[-1]
    cq = d // N_CHIPS
    ex.cast(BIG)

    small_blk, small_counts = _pack_rows([w[n] for n in SMALL_SHARDED], cq)
    small_all = _allgather8(small_blk, "gather_small").reshape(N_CHIPS, 2, small_blk.shape[0], cq)[:, 0]
    small_parts = _unpack_rows(jnp.transpose(small_all, (1, 0, 2)), small_counts,
                               [(w[n].reshape(-1, cq).shape[0], N_CHIPS, cq) for n in SMALL_SHARDED])
    wts = ex.weights
    for n in REPLICATED:
        wts[n] = w[n].reshape(1, -1)
    for n, part in zip(SMALL_SHARDED, small_parts):
        wts[n] = part.reshape(part.shape[0], d)

    loss, dx, g = _device_step(x[0], target[0], wts, ex.grads)

    summed, last = ("ffn0_w_gu", "ffn0_w_down", GROUPED, "a0_w_in"), "a0_w_out"
    _comm_only([ex.sibling_parts("a0_w_in")(), ex.sibling_halves(last)()], "reduce_tail_d2d")
    ex.add_halves(last)()

    rep_rows = [jnp.pad(g[n].reshape(-1, LANES), ((0, 0), (0, cq - LANES))) for n in REPLICATED]
    by_chip = [jnp.transpose(g[n].reshape(g[n].shape[0], N_CHIPS, cq), (1, 0, 2)) for n in SMALL_SHARDED]
    shard_rows = jnp.concatenate(by_chip, axis=1)
    n_rep, n_shard = sum(r.shape[0] for r in rep_rows), shard_rows.shape[1]
    loss_row = jnp.broadcast_to(loss, (1, cq))
    sm_blk = _pad_rows(jnp.concatenate(rep_rows + [loss_row, shard_rows.reshape(N_CHIPS * n_shard, cq)], axis=0))
    sm_sum = _sum_devices(_allgather8(sm_blk, "gather_small_grads"), "sum_small_grads")
    mine = lax.dynamic_slice_in_dim(sm_sum, n_rep + 1 + my_chip * n_shard, n_shard, axis=0)
    ex.adamw(*summed)()
    ex.adamw(last)()
    sched = _ACTIVE_SCHEDULE[0]
    assert not sched.hosts and not sched.posts, (sched.hosts, sched.posts)

    out = ex.updates
    params, at = [], {0: 0, 1: 0}
    for block, names in ((0, REPLICATED), (1, SMALL_SHARDED)):
        for n in names:
            params.append((w[n], mom[n], vel[n], block, at[block]))
            at[block] += w[n].size // LANES if w[n].ndim == 1 else w[n].shape[0]
    out.update(zip(REPLICATED + SMALL_SHARDED, _adamw_small([sm_sum, mine], params, "adamw_small")))

    total = sm_sum[n_rep, 0]
    grads, deltas, new_m, new_v = ([out[n][k].reshape(w[n].shape) for n in WEIGHT_NAMES] for k in range(4))
    return (total, dx.reshape(x.shape), *grads, *deltas, *new_m, *new_v)
```

```python
import functools

import jax
import jax.numpy as jnp
from jax import lax
from jax.experimental import pallas as pl
from jax.experimental.pallas import tpu as pltpu

F32 = jnp.float32
BF16 = jnp.bfloat16

RMS_EPS = 1e-6
LN_EPS = 1e-5
POOL_WINDOWS = (2, 4, 8, 16)
SHORT_CONV_W = 3
CONF_CONV_W = 31
N_CHIPS = 4
N_DEV = 8

ADAM_LR = 0.001
ADAM_B1 = 0.9
ADAM_B2 = 0.999
ADAM_EPS = 1e-08
ADAM_WD = 0.01
ADAM_STEP = 10

V7X_VMEM_BYTES = 64 * 1024 * 1024
VMEM_LIMIT = V7X_VMEM_BYTES - 8 * 1024 * 1024
LANES = 128
POOL_HALO = 16
SCONV_HALO = 16
CONF_HALO = 32


def _params(*sem):
    return pltpu.CompilerParams(dimension_semantics=sem, vmem_limit_bytes=VMEM_LIMIT)


def _tile(n, pref, mult=8):
    t = min(n, pref)
    while t > mult and (n % t or t % mult):
        t -= mult
    assert n % t == 0 and t % mult == 0, (n, pref, mult)
    return t


def _sigmoid(x):
    return jax.nn.sigmoid(x)


def _dot(a, b):
    return jnp.dot(a, b, preferred_element_type=F32)


def _dot_nt(a, b):
    return lax.dot_general(a, b, (((1,), (1,)), ((), ())), preferred_element_type=F32)


def _dot_tn(a, b):
    return lax.dot_general(a, b, (((0,), (0,)), ((), ())), preferred_element_type=F32)


def _colsum(x):
    return jnp.sum(x, axis=0, keepdims=True)


def _rms_stats(x):
    return lax.rsqrt(jnp.mean(x * x, axis=-1, keepdims=True) + RMS_EPS)


def _rms_bwd(du, x, gain):
    r = _rms_stats(x)
    xhat = x * r
    gdy = du * gain
    dx = r * (gdy - xhat * jnp.mean(gdy * xhat, axis=-1, keepdims=True))
    return dx, _colsum(du * xhat)


class _Task:
    def __init__(self, ins, out_shapes, aliases, sems, start, wait, done):
        self.ins, self.out_shapes, self.aliases, self.sems = list(ins), list(out_shapes), dict(aliases), list(sems)
        self.start, self.wait, self.done = start, wait, done


class _Schedule:
    def __init__(self):
        self.hosts, self.posts = {}, {}

    def host(self, kernel_name, *make_tasks):
        self.hosts.setdefault(kernel_name, []).extend(make_tasks)

    def post(self, kernel_name, *thunks):
        self.posts.setdefault(kernel_name, []).extend(thunks)

    def tasks_for(self, kernel_name):
        return [make() for make in self.hosts.pop(kernel_name, ())]

    def finished(self, kernel_name):
        for thunk in self.posts.pop(kernel_name, ()):
            thunk()


_ACTIVE_SCHEDULE = [None]


def _hosted(body, name, **kw):
    def run(*args):
        sched = _ACTIVE_SCHEDULE[0]
        tasks = sched.tasks_for(name) if sched is not None else []
        out = _call_with_tasks(body, name, tasks, kw, args) if tasks else pl.pallas_call(body, name=name, **kw)(*args)
        if sched is not None:
            sched.finished(name)
        return out

    return run


def _call_with_tasks(body, name, tasks, kw, args):
    spec = kw.get("grid_spec")
    n_pre = spec.num_scalar_prefetch if spec is not None else 0
    src = dict(grid=spec.grid, in_specs=spec.in_specs, out_specs=spec.out_specs) if spec is not None else kw
    pre, args = args[:n_pre], args[n_pre:]
    grid = tuple(src.get("grid", ()))
    single = not isinstance(kw["out_shape"], (list, tuple))
    out_shape = [kw["out_shape"]] if single else list(kw["out_shape"])
    out_specs = [src["out_specs"]] if single else list(src["out_specs"])
    scratch = list(kw.get("scratch_shapes", ()))
    n_in, n_out, n_scr = len(args), len(out_shape), len(scratch)
    t_in = [a for t in tasks for a in t.ins]
    t_out = [o for t in tasks for o in t.out_shapes]
    t_sem = [s for t in tasks for s in t.sems]
    aliases, at_in, at_out = {}, n_pre + n_in, n_out
    for t in tasks:
        for i, o in t.aliases.items():
            aliases[at_in + i] = at_out + o
        at_in += len(t.ins)
        at_out += len(t.out_shapes)

    def wrapped(*refs):
        pre_refs, refs = refs[:n_pre], refs[n_pre:]
        a = n_in
        b = a + len(t_in)
        c = b + n_out
        d = c + len(t_out)
        e = d + n_scr
        ins, tins, outs, touts, scr, tsems = refs[:a], refs[a:b], refs[b:c], refs[c:d], refs[d:e], refs[e:]
        views, i0, o0, s0 = [], 0, 0, 0
        for t in tasks:
            views.append((tins[i0:i0 + len(t.ins)], touts[o0:o0 + len(t.out_shapes)], tsems[s0:s0 + len(t.sems)]))
            i0, o0, s0 = i0 + len(t.ins), o0 + len(t.out_shapes), s0 + len(t.sems)

        def start_all():
            for t, v in zip(tasks, views):
                t.start(*v)

        def wait_all():
            for t, v in zip(tasks, views):
                t.wait(*v)

        if grid:
            first = functools.reduce(jnp.logical_and, [pl.program_id(i) == 0 for i in range(len(grid))])
            last = functools.reduce(jnp.logical_and, [pl.program_id(i) == grid[i] - 1 for i in range(len(grid))])
            pl.when(first)(start_all)
            body(*pre_refs, *ins, *outs, *scr)
            pl.when(last)(wait_all)
        else:
            start_all()
            body(*pre_refs, *ins, *outs, *scr)
            wait_all()

    in_specs = list(src["in_specs"]) + [ANY] * len(t_in)
    out_specs = out_specs + [ANY] * len(t_out)
    if spec is not None:
        layout = dict(grid_spec=pltpu.PrefetchScalarGridSpec(
            num_scalar_prefetch=n_pre, grid=grid, in_specs=in_specs, out_specs=out_specs, scratch_shapes=scratch + t_sem))
    else:
        layout = dict(grid=grid, in_specs=in_specs, out_specs=out_specs, scratch_shapes=scratch + t_sem)
    res = pl.pallas_call(
        wrapped, name=name, out_shape=out_shape + t_out, input_output_aliases=aliases,
        compiler_params=pltpu.CompilerParams(dimension_semantics=("arbitrary",) * len(grid), vmem_limit_bytes=VMEM_LIMIT),
        **layout,
    )(*pre, *args, *t_in)
    res = list(res)
    own, rest = res[:n_out], res[n_out:]
    for t in tasks:
        t.done(rest[:len(t.out_shapes)])
        rest = rest[len(t.out_shapes):]
    return own[0] if single else own


def _comm_only(tasks, name):
    _call_with_tasks(lambda: None, name, tasks, dict(grid=(), in_specs=[], out_specs=[], out_shape=[]), ())


def _mm_col(h, gain, w, bias, name):
    s, k = h.shape
    nsh, _, ns = w.shape
    tm = _tile(s, 512)
    has_bias = bias is not None

    def body(h_ref, gain_ref, w_ref, *rest):
        u_ref, o_ref = rest[-2:]
        x = h_ref[...]
        x = (x * _rms_stats(x) * gain_ref[...]).astype(BF16)
        u_ref[...] = x
        for j in range(nsh):
            cols = pl.ds(j * ns, ns)
            acc = _dot(x, w_ref[j])
            if has_bias:
                acc = acc + rest[0][:, cols]
            o_ref[:, cols] = acc.astype(o_ref.dtype)

    tokens = pl.BlockSpec((tm, k), lambda m: (m, 0))
    in_specs = [tokens, pl.BlockSpec((1, k), lambda m: (0, 0)), pl.BlockSpec((nsh, k, ns), lambda m: (0, 0, 0))]
    args = [h, gain, w]
    if has_bias:
        in_specs.append(pl.BlockSpec((1, nsh * ns), lambda m: (0, 0)))
        args.append(bias)
    return _hosted(
        body, name=name, grid=(s // tm,), in_specs=in_specs,
        out_specs=[tokens, pl.BlockSpec((tm, nsh * ns), lambda m: (m, 0))],
        out_shape=[jax.ShapeDtypeStruct((s, k), BF16), jax.ShapeDtypeStruct((s, nsh * ns), BF16)],
        compiler_params=_params("parallel"),
    )(*args)


def _mm_row(a, w, res, bias, name):
    s = a.shape[0]
    k, n = w.shape
    tm = _tile(s, 1024)
    has_bias = bias is not None

    def body(a_ref, w_ref, res_ref, *rest):
        o_ref = rest[-1]
        y = res_ref[...] + _dot(a_ref[...], w_ref[...])
        if has_bias:
            y = y + rest[0][...]
        o_ref[...] = y

    in_specs = [pl.BlockSpec((tm, k), lambda m: (m, 0)), pl.BlockSpec((k, n), lambda m: (0, 0)),
                pl.BlockSpec((tm, n), lambda m: (m, 0))]
    args = [a, w, res]
    if has_bias:
        in_specs.append(pl.BlockSpec((1, n), lambda m: (0, 0)))
        args.append(bias)
    return _hosted(
        body, name=name, grid=(s // tm,), in_specs=in_specs,
        out_specs=pl.BlockSpec((tm, n), lambda m: (m, 0)),
        out_shape=jax.ShapeDtypeStruct((s, n), F32),
        compiler_params=_params("parallel"),
    )(*args)


def _mm_nt_row(dy, w, name):
    s, n = dy.shape
    k = w.shape[0]
    tm = _tile(s, 512)

    def body(dy_ref, w_ref, o_ref):
        o_ref[...] = _dot_nt(dy_ref[...].astype(BF16), w_ref[...])

    return _hosted(
        body, name=name, grid=(s // tm,),
        in_specs=[pl.BlockSpec((tm, n), lambda m: (m, 0)), pl.BlockSpec((k, n), lambda m: (0, 0))],
        out_specs=pl.BlockSpec((tm, k), lambda m: (m, 0)),
        out_shape=jax.ShapeDtypeStruct((s, k), F32),
        compiler_params=_params("parallel"),
    )(dy, w)


def _ffn_up(h, gain, w, name):
    s, d = h.shape
    _, _, ns = w.shape
    tm = _tile(s, 512)

    def body(h_ref, gain_ref, wg_ref, wu_ref, u_ref, act_ref, s1_ref, q1_ref):
        x = h_ref[...]
        x = (x * _rms_stats(x) * gain_ref[...]).astype(BF16)

        @pl.when(pl.program_id(0) == 0)
        def _():
            u_ref[...] = x

        g = _dot(x, wg_ref[...])
        up = _dot(x, wu_ref[...])
        sg = _sigmoid(g)
        s1 = g * sg
        act_ref[...] = (s1 * up).astype(act_ref.dtype)
        s1_ref[...] = s1.astype(s1_ref.dtype)
        q1_ref[...] = (up * sg * (1.0 + g * (1.0 - sg))).astype(q1_ref.dtype)

    out = pl.BlockSpec((tm, ns), lambda j, m: (m, j))
    tokens = pl.BlockSpec((tm, d), lambda j, m: (m, 0))
    nm = s // tm
    u_once = pl.BlockSpec((tm, d), lambda j, m: (jnp.where(j == 0, m, nm - 1), 0))
    return _hosted(
        body, name=name, grid=(2, nm),
        in_specs=[tokens, pl.BlockSpec((1, d), lambda j, m: (0, 0)), pl.BlockSpec((None, d, ns), lambda j, m: (j, 0, 0)),
                  pl.BlockSpec((None, d, ns), lambda j, m: (j + 2, 0, 0))],
        out_specs=[u_once, out, out, out],
        out_shape=[jax.ShapeDtypeStruct((s, d), BF16)] + [jax.ShapeDtypeStruct((s, 2 * ns), BF16)] * 3,
        compiler_params=_params("arbitrary", "arbitrary"),
    )(h, gain, w, w)


def _ffn_down_bwd(dh, w, s1, q1, name):
    s, d = dh.shape
    f = w.shape[0]
    tm = _tile(s, 512)

    def body(dh_ref, w_ref, s1_ref, q1_ref, o_ref):
        da = _dot_nt(dh_ref[...].astype(BF16), w_ref[...])
        o_ref[:, :f] = (da * q1_ref[...].astype(F32)).astype(o_ref.dtype)
        o_ref[:, f:] = (da * s1_ref[...].astype(F32)).astype(o_ref.dtype)

    return _hosted(
        body, name=name, grid=(s // tm,),
        in_specs=[pl.BlockSpec((tm, d), lambda m: (m, 0)), pl.BlockSpec((f, d), lambda m: (0, 0)),
                  pl.BlockSpec((tm, f), lambda m: (m, 0)), pl.BlockSpec((tm, f), lambda m: (m, 0))],
        out_specs=pl.BlockSpec((tm, 2 * f), lambda m: (m, 0)),
        out_shape=jax.ShapeDtypeStruct((s, 2 * f), BF16),
        compiler_params=_params("parallel"),
    )(dh, w, s1, q1)


def _mm_nt_col_rms_bwd(dy, w, h, gain, dh, name):
    s = dy.shape[0]
    nsh, k, ns = w.shape
    tm = _tile(s, 512)

    def body(dy_ref, w_ref, h_ref, g_ref, dh_ref, o_ref, dg_ref):
        du = _dot_nt(dy_ref[:, :ns], w_ref[0])
        for j in range(1, nsh):
            du = du + _dot_nt(dy_ref[:, j * ns:(j + 1) * ns], w_ref[j])
        dx, dg = _rms_bwd(du, h_ref[...], g_ref[...])
        o_ref[...] = dh_ref[...] + dx
        _accumulate(dg_ref, dg, pl.program_id(0) == 0)

    return _hosted(
        body, name=name, grid=(s // tm,),
        in_specs=[pl.BlockSpec((tm, nsh * ns), lambda m: (m, 0)), pl.BlockSpec((nsh, k, ns), lambda m: (0, 0, 0)),
                  pl.BlockSpec((tm, k), lambda m: (m, 0)), pl.BlockSpec((1, k), lambda m: (0, 0)),
                  pl.BlockSpec((tm, k), lambda m: (m, 0))],
        out_specs=[pl.BlockSpec((tm, k), lambda m: (m, 0)), pl.BlockSpec((1, k), lambda m: (0, 0))],
        out_shape=[jax.ShapeDtypeStruct((s, k), F32), jax.ShapeDtypeStruct((1, k), F32)],
        compiler_params=_params("arbitrary"),
    )(dy, w, h, gain, dh)


def _mm_tn(a, dy, nsh, name):
    s, k = a.shape
    ns = dy.shape[1] // nsh
    tm = _tile(s, 2048)
    tk = _tile(k, 1408, LANES)
    nk, nm = k // tk, s // tm

    def body(a_ref, dy_ref, o_ref, acc_ref):
        m = pl.program_id(2)
        part = _dot_tn(a_ref[...], dy_ref[...].astype(BF16))

        @pl.when(m == 0)
        def _():
            acc_ref[...] = part

        @pl.when(m > 0)
        def _():
            acc_ref[...] += part

        @pl.when(m == nm - 1)
        def _():
            o_ref[...] = acc_ref[...].astype(o_ref.dtype)

    return _hosted(
        body, name=name, grid=(nsh, nk, nm),
        in_specs=[pl.BlockSpec((tm, tk), lambda j, kk, m: (m, kk)), pl.BlockSpec((tm, ns), lambda j, kk, m: (m, j))],
        out_specs=pl.BlockSpec((None, tk, ns), lambda j, kk, m: (j, kk, 0)),
        out_shape=jax.ShapeDtypeStruct((nsh, k, ns), BF16),
        scratch_shapes=[pltpu.VMEM((tk, ns), F32)],
        compiler_params=_params("parallel", "parallel", "arbitrary"),
    )(a, dy)


def _main_spec(tm, w):
    return pl.BlockSpec((tm, w), lambda m: (m, 0))


def _before_spec(tm, hb, w):
    return pl.BlockSpec((hb, w), lambda m: (jnp.maximum(m * (tm // hb) - 1, 0), 0))


def _after_spec(tm, hb, w, s):
    return pl.BlockSpec((hb, w), lambda m: (jnp.minimum((m + 1) * (tm // hb), s // hb - 1), 0))


def _row_spec(w, rows=1):
    return pl.BlockSpec((rows, w), lambda m: (0, 0))


CHUNK_LANES = 4 * LANES
CHUNK_ROWS = 32


def _build_shifts(ext8_ref, residues=range(1, 8)):
    n = ext8_ref.shape[1] - 8
    for r in residues:
        ext8_ref[r, pl.ds(0, n), :] = ext8_ref[0, pl.ds(r, n), :]


def _fold_rows(x):
    return functools.reduce(lambda p, q: p + q, [x[i:i + 8] for i in range(0, x.shape[0], 8)])


def _shifted(ext8_ref, shift, r0, rows, cols):
    return ext8_ref[shift % 8, pl.ds(pl.multiple_of(shift - shift % 8 + r0, 8), rows), cols]


def _lane_chunk(i):
    return pl.ds(pl.multiple_of(i * CHUNK_LANES, CHUNK_LANES), CHUNK_LANES)


def _sum_terms(terms, ways=4):
    accs = []
    for i, t in enumerate(terms):
        if i < ways:
            accs.append(t)
        else:
            accs[i % ways] = accs[i % ways] + t
    while len(accs) > 1:
        accs = [accs[i] + accs[i + 1] if i + 1 < len(accs) else accs[i] for i in range(0, len(accs), 2)]
    return accs[0]


def _accumulate(ref, val, first):
    @pl.when(first)
    def _():
        ref[...] = val

    @pl.when(jnp.logical_not(first))
    def _():
        ref[...] += val


SCONV_Z_SHIFTS = tuple(SCONV_HALO - (SHORT_CONV_W - 1) + k for k in range(SHORT_CONV_W))


def _sconv_z_taps(zext_ref, r0, cols):
    return [_shifted(zext_ref, shift, r0, CHUNK_ROWS, cols) for shift in SCONV_Z_SHIFTS]


def _weighted(cw_ref, cols, terms):
    return _sum_terms((cw_ref[k:k + 1, cols] * t for k, t in enumerate(terms)), ways=len(terms))


def _sconv_fill_z(zext_ref, main_ref, before_ref, d, m):
    hb = SCONV_HALO
    zb = before_ref[:, d:2 * d].astype(F32) * before_ref[:, 2 * d:].astype(F32)
    zext_ref[pl.ds(0, hb), :] = jnp.where(m > 0, zb, 0.0)
    zext_ref[pl.ds(hb, main_ref.shape[0]), :] = main_ref[:, d:2 * d].astype(F32) * main_ref[:, 2 * d:].astype(F32)


def _sconv_fwd(bcv, cw, name):
    s, d3 = bcv.shape
    d = d3 // 3
    tm = _tile(s, 256, CHUNK_ROWS)
    row_chunks = tm // CHUNK_ROWS

    def body(main_ref, before_ref, cw_ref, p_ref, zext_ref):
        m = pl.program_id(0)
        _sconv_fill_z(zext_ref.at[0], main_ref, before_ref, d, m)
        _build_shifts(zext_ref, [shift % 8 for shift in SCONV_Z_SHIFTS if shift % 8])

        def chunk(i, carry):
            cols = _lane_chunk(i // row_chunks)
            r0 = pl.multiple_of((i % row_chunks) * CHUNK_ROWS, CHUNK_ROWS)
            rows = pl.ds(r0, CHUNK_ROWS)
            zc = _weighted(cw_ref, cols, _sconv_z_taps(zext_ref, r0, cols))
            p_ref[rows, cols] = (main_ref[rows, cols].astype(F32) * zc).astype(p_ref.dtype)
            return carry

        lax.fori_loop(0, row_chunks * (d // CHUNK_LANES), chunk, 0)

    return _hosted(
        body, name=name, grid=(s // tm,),
        in_specs=[_main_spec(tm, d3), _before_spec(tm, SCONV_HALO, d3), _row_spec(d, SHORT_CONV_W)],
        out_specs=_main_spec(tm, d),
        out_shape=jax.ShapeDtypeStruct((s, d), BF16),
        scratch_shapes=[pltpu.VMEM((8, tm + SCONV_HALO, d), F32)],
        compiler_params=_params("parallel"),
    )(bcv, bcv, cw)


def _sconv_bwd(dp, bcv, cw, name):
    s, d3 = bcv.shape
    d = d3 // 3
    tm = _tile(s, 256, CHUNK_ROWS)
    nm = s // tm
    ha = 8
    kw = SHORT_CONV_W

    def body(dp_ref, dpa_ref, main_ref, before_ref, after_ref, cw_ref, o_ref, dcw_ref, zext_ref, dext_ref):
        m = pl.program_id(0)
        _sconv_fill_z(zext_ref.at[0], main_ref, before_ref, d, m)
        _build_shifts(zext_ref, [shift % 8 for shift in SCONV_Z_SHIFTS if shift % 8])
        dext_ref[0, pl.ds(0, tm), :] = dp_ref[...] * main_ref[:, :d].astype(F32)
        dza = dpa_ref[...] * after_ref[:, :d].astype(F32)[0:ha]
        dext_ref[0, pl.ds(tm, ha), :] = jnp.where(m < nm - 1, dza, 0.0)
        _build_shifts(dext_ref, range(1, kw))

        @pl.when(m == 0)
        def _():
            dcw_ref[...] = jnp.zeros_like(dcw_ref)

        zero = jnp.zeros((8, CHUNK_LANES), F32)

        def lane_chunk(ci, carry):
            cols = _lane_chunk(ci)
            c_cols, v_cols = (pl.ds(pl.multiple_of(part * d + ci * CHUNK_LANES, CHUNK_LANES), CHUNK_LANES) for part in (1, 2))

            def row_chunk(ri, sums):
                r0 = pl.multiple_of(ri * CHUNK_ROWS, CHUNK_ROWS)
                rows = pl.ds(r0, CHUNK_ROWS)
                z = _sconv_z_taps(zext_ref, r0, cols)
                o_ref[rows, cols] = (dp_ref[rows, cols] * _weighted(cw_ref, cols, z)).astype(o_ref.dtype)
                dzc = [_shifted(dext_ref, kw - 1 - k, r0, CHUNK_ROWS, cols) for k in range(kw)]
                dz = _weighted(cw_ref, cols, dzc)
                o_ref[rows, c_cols] = (dz * main_ref[rows, v_cols].astype(F32)).astype(o_ref.dtype)
                o_ref[rows, v_cols] = (dz * main_ref[rows, c_cols].astype(F32)).astype(o_ref.dtype)
                return tuple(acc + _fold_rows(dzc[kw - 1] * z[k]) for k, acc in enumerate(sums))

            sums = lax.fori_loop(0, tm // CHUNK_ROWS, row_chunk, (zero,) * kw)
            for k in range(kw):
                dcw_ref[k:k + 1, cols] += _colsum(sums[k])
            return carry

        lax.fori_loop(0, d // CHUNK_LANES, lane_chunk, 0)

    return _hosted(
        body, name=name, grid=(nm,),
        in_specs=[_main_spec(tm, d), _after_spec(tm, ha, d, s), _main_spec(tm, d3), _before_spec(tm, SCONV_HALO, d3),
                  _after_spec(tm, SCONV_HALO, d3, s), _row_spec(d, SHORT_CONV_W)],
        out_specs=[_main_spec(tm, d3), _row_spec(d, 8)],
        out_shape=[jax.ShapeDtypeStruct((s, d3), BF16), jax.ShapeDtypeStruct((8, d), F32)],
        scratch_shapes=[pltpu.VMEM((8, tm + SCONV_HALO, d), F32), pltpu.VMEM((8, tm + ha, d), F32)],
        compiler_params=_params("arbitrary"),
    )(dp, dp, bcv, bcv, bcv, cw)


def _pool_counts(t0, tm, w):
    t = t0 + lax.broadcasted_iota(jnp.int32, (tm, 1), 0)
    return jnp.minimum(t + 1, w).astype(F32)


def _pool_fwd(h, gain, wg, scale, name):
    s, d = h.shape
    ng, cg, _ = wg.shape
    tm = _tile(s, 512, POOL_HALO)

    def body(h_ref, hb_ref, g_ref, wg_ref, sc_ref, o_ref, mx_ref, uext_ref):
        m = pl.program_id(0)
        x = h_ref[...]
        gain_row = g_ref[...]
        xb = hb_ref[...]
        uext_ref[pl.ds(0, POOL_HALO), :] = jnp.where(m > 0, xb * _rms_stats(xb) * gain_row, 0.0)
        uext_ref[pl.ds(POOL_HALO, tm), :] = x * _rms_stats(x) * gain_row
        for gi, win in enumerate(POOL_WINDOWS):
            cols = pl.ds(gi * cg, cg)
            u_g = uext_ref[pl.ds(POOL_HALO, tm), cols]
            acc = u_g
            for i in range(1, win):
                acc = acc + uext_ref[pl.ds(POOL_HALO - i, tm), cols]
            mixed = (acc / _pool_counts(m * tm, tm, win) - u_g).astype(BF16)
            mx_ref[:, cols] = mixed
            o_ref[:, cols] = x[:, gi * cg:(gi + 1) * cg] + _dot(mixed, wg_ref[gi]) * sc_ref[:, cols]

    return _hosted(
        body, name=name, grid=(s // tm,),
        in_specs=[_main_spec(tm, d), _before_spec(tm, POOL_HALO, d), _row_spec(d),
                  pl.BlockSpec((ng, cg, cg), lambda m: (0, 0, 0)), _row_spec(d)],
        out_specs=[_main_spec(tm, d), _main_spec(tm, d)],
        out_shape=[jax.ShapeDtypeStruct((s, d), F32), jax.ShapeDtypeStruct((s, d), BF16)],
        scratch_shapes=[pltpu.VMEM((tm + POOL_HALO, d), F32)],
        compiler_params=_params("parallel"),
    )(h, h, gain, wg, scale)


def _pool_bwd_mm(dh, mixed, wg, scale, name):
    s, d = dh.shape
    ng, cg, _ = wg.shape
    tm = _tile(s, 512)

    def body(dh_ref, mx_ref, wg_ref, sc_ref, dmx_ref, dwg_ref, dsc_ref):
        first = pl.program_id(0) == 0
        for gi in range(ng):
            cols = pl.ds(gi * cg, cg)
            dh_g = dh_ref[:, cols]
            mixed = mx_ref[:, cols]
            w_g = wg_ref[gi]
            dy = (dh_g * sc_ref[:, cols]).astype(BF16)
            dmx_ref[:, cols] = _dot_nt(dy, w_g)
            _accumulate(dsc_ref.at[:, cols], _colsum(dh_g * _dot(mixed, w_g)), first)
            _accumulate(dwg_ref.at[gi], _dot_tn(mixed, dy), first)

    return _hosted(
        body, name=name, grid=(s // tm,),
        in_specs=[_main_spec(tm, d), _main_spec(tm, d), pl.BlockSpec((ng, cg, cg), lambda m: (0, 0, 0)), _row_spec(d)],
        out_specs=[_main_spec(tm, d), pl.BlockSpec((ng, cg, cg), lambda m: (0, 0, 0)), _row_spec(d)],
        out_shape=[jax.ShapeDtypeStruct((s, d), F32), jax.ShapeDtypeStruct((ng, cg, cg), F32),
                   jax.ShapeDtypeStruct((1, d), F32)],
        compiler_params=_params("arbitrary"),
    )(dh, mixed, wg, scale)


def _pool_bwd_rms(dmixed, h, gain, dh, name):
    s, d = h.shape
    cg = d // len(POOL_WINDOWS)
    tm = _tile(s, 512, POOL_HALO)
    nm = s // tm

    def body(dmx_ref, dmxa_ref, h_ref, g_ref, dh_ref, o_ref, dg_ref, eext_ref, du_ref):
        m = pl.program_id(0)
        for gi, win in enumerate(POOL_WINDOWS):
            cols = pl.ds(gi * cg, cg)
            dmx = dmx_ref[:, cols]
            eext_ref[pl.ds(0, tm), cols] = dmx / _pool_counts(m * tm, tm, win)
            ea = dmxa_ref[:, cols] / _pool_counts((m + 1) * tm, POOL_HALO, win)
            eext_ref[pl.ds(tm, POOL_HALO), cols] = jnp.where(m < nm - 1, ea, 0.0)
            acc = -dmx
            for i in range(win):
                acc = acc + eext_ref[pl.ds(i, tm), cols]
            du_ref[:, cols] = acc
        dx, dg = _rms_bwd(du_ref[...], h_ref[...], g_ref[...])
        o_ref[...] = dh_ref[...] + dx
        _accumulate(dg_ref, dg, m == 0)

    return _hosted(
        body, name=name, grid=(nm,),
        in_specs=[_main_spec(tm, d), _after_spec(tm, POOL_HALO, d, s), _main_spec(tm, d), _row_spec(d), _main_spec(tm, d)],
        out_specs=[_main_spec(tm, d), _row_spec(d)],
        out_shape=[jax.ShapeDtypeStruct((s, d), F32), jax.ShapeDtypeStruct((1, d), F32)],
        scratch_shapes=[pltpu.VMEM((tm + POOL_HALO, d), F32), pltpu.VMEM((tm, d), F32)],
        compiler_params=_params("arbitrary"),
    )(dmixed, dmixed, h, gain, dh)


def _conf_fill_h(hext_ref, main_ref, before_ref, d, m):
    hb = before_ref[:, :d].astype(F32) * _sigmoid(before_ref[:, d:].astype(F32))
    hext_ref[pl.ds(0, CONF_HALO), :] = jnp.where(m > 0, hb, 0.0)
    hext_ref[pl.ds(CONF_HALO, main_ref.shape[0]), :] = main_ref[:, :d].astype(F32) * _sigmoid(main_ref[:, d:].astype(F32))


def _layernorm_parts(hc, g, b):
    mu = jnp.mean(hc, axis=-1, keepdims=True)
    xc = hc - mu
    rs = lax.rsqrt(jnp.mean(xc * xc, axis=-1, keepdims=True) + LN_EPS)
    xhat = xc * rs
    return xhat, rs, xhat * g + b


def _conf_mid_fwd(ag, dw, b_dw, ln_g, ln_b, name):
    s, d2 = ag.shape
    d = d2 // 2
    tm = _tile(s, 256, CONF_HALO)
    base = CONF_HALO - (CONF_CONV_W - 1)

    def body(main_ref, before_ref, dw_ref, bdw_ref, g_ref, b_ref, s_ref, hc_ref, hext_ref):
        m = pl.program_id(0)
        _conf_fill_h(hext_ref.at[0], main_ref, before_ref, d, m)
        _build_shifts(hext_ref)
        row_chunks = tm // CHUNK_ROWS

        def conv_chunk(i, carry):
            cols = _lane_chunk(i // row_chunks)
            r0 = pl.multiple_of((i % row_chunks) * CHUNK_ROWS, CHUNK_ROWS)
            taps = (dw_ref[kk:kk + 1, cols] * _shifted(hext_ref, base + kk, r0, CHUNK_ROWS, cols) for kk in range(CONF_CONV_W))
            hc_ref[pl.ds(r0, CHUNK_ROWS), cols] = bdw_ref[:, cols] + _sum_terms(taps, ways=1)
            return carry

        lax.fori_loop(0, row_chunks * (d // CHUNK_LANES), conv_chunk, 0)
        _, _, l = _layernorm_parts(hc_ref[...], g_ref[...], b_ref[...])
        s_ref[...] = (l * _sigmoid(l)).astype(s_ref.dtype)

    return _hosted(
        body, name=name, grid=(s // tm,),
        in_specs=[_main_spec(tm, d2), _before_spec(tm, CONF_HALO, d2), _row_spec(d, CONF_CONV_W), _row_spec(d),
                  _row_spec(d), _row_spec(d)],
        out_specs=[_main_spec(tm, d), _main_spec(tm, d)],
        out_shape=[jax.ShapeDtypeStruct((s, d), BF16), jax.ShapeDtypeStruct((s, d), F32)],
        scratch_shapes=[pltpu.VMEM((8, tm + CONF_HALO, d), F32)],
        compiler_params=_params("parallel"),
    )(ag, ag, dw, b_dw, ln_g, ln_b)


def _conf_out_bwd(dh, w, hc, ln_g, ln_b, name):
    s, d = dh.shape
    tm = _tile(s, 512)

    def body(dh_ref, w_ref, hc_ref, g_ref, b_ref, o_ref, dg_ref, db_ref, dbo_ref):
        first = pl.program_id(0) == 0
        dh_t = dh_ref[...]
        ds = _dot_nt(dh_t.astype(BF16), w_ref[...])
        xhat, rs, l = _layernorm_parts(hc_ref[...], g_ref[...], b_ref[...])
        sg = _sigmoid(l)
        dl = ds * sg * (1.0 + l * (1.0 - sg))
        dxh = dl * g_ref[...]
        o_ref[...] = rs * (dxh - jnp.mean(dxh, axis=-1, keepdims=True)
                           - xhat * jnp.mean(dxh * xhat, axis=-1, keepdims=True))
        _accumulate(dg_ref, _colsum(dl * xhat), first)
        _accumulate(db_ref, _colsum(dl), first)
        _accumulate(dbo_ref, _colsum(dh_t), first)

    return _hosted(
        body, name=name, grid=(s // tm,),
        in_specs=[_main_spec(tm, d), pl.BlockSpec((d, d), lambda m: (0, 0)), _main_spec(tm, d), _row_spec(d), _row_spec(d)],
        out_specs=[_main_spec(tm, d), _row_spec(d), _row_spec(d), _row_spec(d)],
        out_shape=[jax.ShapeDtypeStruct((s, d), F32)] + [jax.ShapeDtypeStruct((1, d), F32)] * 3,
        compiler_params=_params("arbitrary"),
    )(dh, w, hc, ln_g, ln_b)


def _conf_mid_bwd(dhc, ag, dw, name):
    s, d2 = ag.shape
    d = d2 // 2
    tm = _tile(s, 256, CONF_HALO)
    nm = s // tm
    kw = CONF_CONV_W
    base = CONF_HALO - (kw - 1)

    def body(dhc_ref, dhca_ref, main_ref, before_ref, dw_ref, o_ref, ddw_ref, dbdw_ref, dbpw_ref, hext_ref, dext_ref):
        m = pl.program_id(0)
        first = m == 0
        _conf_fill_h(hext_ref.at[0], main_ref, before_ref, d, m)
        _build_shifts(hext_ref)
        dext_ref[0, pl.ds(0, tm), :] = dhc_ref[...]
        dext_ref[0, pl.ds(tm, CONF_HALO), :] = jnp.where(m < nm - 1, dhca_ref[...], 0.0)
        _build_shifts(dext_ref)

        @pl.when(first)
        def _():
            ddw_ref[...] = jnp.zeros_like(ddw_ref)
            dbdw_ref[...] = jnp.zeros_like(dbdw_ref)
            dbpw_ref[...] = jnp.zeros_like(dbpw_ref)

        zero = jnp.zeros((8, CHUNK_LANES), F32)
        tap_group = 8

        def fold(x):
            return functools.reduce(lambda p, q: p + q, [x[i:i + 8] for i in range(0, CHUNK_ROWS, 8)])

        def lane_chunk(ci, carry):
            cols = _lane_chunk(ci)
            gate_cols = pl.ds(pl.multiple_of(d + ci * CHUNK_LANES, CHUNK_LANES), CHUNK_LANES)

            def through_conv(ri, sums):
                r0 = pl.multiple_of(ri * CHUNK_ROWS, CHUNK_ROWS)
                rows = pl.ds(r0, CHUNK_ROWS)
                dhh = _sum_terms((dw_ref[kk:kk + 1, cols] * _shifted(dext_ref, kw - 1 - kk, r0, CHUNK_ROWS, cols)
                                  for kk in range(kw)), ways=1)
                a = main_ref[rows, cols].astype(F32)
                sg = _sigmoid(main_ref[rows, gate_cols].astype(F32))
                da = dhh * sg
                dgate = dhh * a * sg * (1.0 - sg)
                o_ref[rows, cols] = da.astype(o_ref.dtype)
                o_ref[rows, gate_cols] = dgate.astype(o_ref.dtype)
                return sums[0] + fold(da), sums[1] + fold(dgate), sums[2] + fold(dext_ref[0, rows, cols])

            sum_da, sum_dgate, sum_dhc = lax.fori_loop(0, tm // CHUNK_ROWS, through_conv, (zero, zero, zero))
            dbdw_ref[:, cols] += _colsum(sum_dhc)
            dbpw_ref[:, cols] += _colsum(sum_da)
            dbpw_ref[:, gate_cols] += _colsum(sum_dgate)

            for k0 in range(0, kw, tap_group):
                group = range(k0, min(k0 + tap_group, kw))

                def tap_gradients(ri, accs, group=group):
                    for sub in range(0, CHUNK_ROWS, 8):
                        r0 = pl.multiple_of(ri * CHUNK_ROWS + sub, 8)
                        dhc_c = dext_ref[0, pl.ds(r0, 8), cols]
                        accs = tuple(acc + dhc_c * _shifted(hext_ref, base + kk, r0, 8, cols) for kk, acc in zip(group, accs))
                    return accs

                accs = lax.fori_loop(0, tm // CHUNK_ROWS, tap_gradients, (zero,) * len(group))
                for kk, acc in zip(group, accs):
                    ddw_ref[kk:kk + 1, cols] += _colsum(acc)
            return carry

        lax.fori_loop(0, d // CHUNK_LANES, lane_chunk, 0)

    return _hosted(
        body, name=name, grid=(nm,),
        in_specs=[_main_spec(tm, d), _after_spec(tm, CONF_HALO, d, s), _main_spec(tm, d2), _before_spec(tm, CONF_HALO, d2),
                  _row_spec(d, kw)],
        out_specs=[_main_spec(tm, d2), _row_spec(d, 32), _row_spec(d), _row_spec(d2)],
        out_shape=[jax.ShapeDtypeStruct((s, d2), BF16), jax.ShapeDtypeStruct((32, d), F32),
                   jax.ShapeDtypeStruct((1, d), F32), jax.ShapeDtypeStruct((1, d2), F32)],
        scratch_shapes=[pltpu.VMEM((8, tm + CONF_HALO, d), F32), pltpu.VMEM((8, tm + CONF_HALO, d), F32)],
        compiler_params=_params("arbitrary"),
    )(dhc, dhc, ag, ag, dw)


def _loss_head(h, gain, target, name):
    s, d = h.shape
    tm = _tile(s, 512)

    def body(h_ref, g_ref, t_ref, loss_ref, dh_ref, dg_ref):
        first = pl.program_id(0) == 0
        x = h_ref[...]
        err = x * _rms_stats(x) * g_ref[...] - t_ref[...]
        part = 0.5 * jnp.sum(jnp.mean(err * err, axis=-1, keepdims=True), axis=0, keepdims=True)
        dx, dg = _rms_bwd(err * (1.0 / d), x, g_ref[...])
        dh_ref[...] = dx
        _accumulate(loss_ref, part, first)
        _accumulate(dg_ref, dg, first)

    return _hosted(
        body, name=name, grid=(s // tm,),
        in_specs=[_main_spec(tm, d), _row_spec(d), _main_spec(tm, d)],
        out_specs=[pl.BlockSpec((1, 1), lambda m: (0, 0)), _main_spec(tm, d), _row_spec(d)],
        out_shape=[jax.ShapeDtypeStruct((1, 1), F32), jax.ShapeDtypeStruct((s, d), F32), jax.ShapeDtypeStruct((1, d), F32)],
        compiler_params=_params("arbitrary"),
    )(h, gain, target)


def _ffn_fwd(h, wts, i):
    u, act, s1, q1 = _ffn_up(h, wts[f"ln2_{i}"], wts[f"ffn{i}_w_gu"], f"ffn{i}_up")
    h_new = _mm_row(act, wts[f"ffn{i}_w_down"], h, None, f"ffn{i}_down")
    return h_new, (h, u, act, s1, q1)


def _ffn_bwd(dh, saved, wts, i, g):
    h, u, act, s1, q1 = saved
    dgu = _ffn_down_bwd(dh, wts[f"ffn{i}_w_down"], s1, q1, f"ffn{i}_down_bwd")
    g[f"ffn{i}_w_down"] = _mm_tn(act, dh, 1, f"ffn{i}_dw_down")
    g[f"ffn{i}_w_gu"] = _mm_tn(u, dgu, N_CHIPS, f"ffn{i}_dw_gu")
    dh_new, g[f"ln2_{i}"] = _mm_nt_col_rms_bwd(dgu, wts[f"ffn{i}_w_gu"], h, wts[f"ln2_{i}"], dh, f"ffn{i}_up_bwd")
    return dh_new


def _device_step(x, target, wts, g=None):
    g = {} if g is None else g
    saved = {}
    h = x

    def short_conv_fwd(h, i):
        u, bcv = _mm_col(h, wts[f"ln1_{i}"], wts[f"a{i}_w_in"], None, f"a{i}_in")
        p = _sconv_fwd(bcv, wts[f"a{i}_conv"], f"a{i}_conv")
        return _mm_row(p, wts[f"a{i}_w_out"], h, None, f"a{i}_out"), (h, u, bcv, p)

    def short_conv_bwd(dh, sv, i):
        h, u, bcv, p = sv
        dp = _mm_nt_row(dh, wts[f"a{i}_w_out"], f"a{i}_out_bwd")
        dbcv, dcw = _sconv_bwd(dp, bcv, wts[f"a{i}_conv"], f"a{i}_conv_bwd")
        g[f"a{i}_conv"] = dcw[:SHORT_CONV_W]
        g[f"a{i}_w_in"] = _mm_tn(u, dbcv, N_CHIPS, f"a{i}_dw_in")
        g[f"a{i}_w_out"] = _mm_tn(p, dh, 1, f"a{i}_dw_out")
        dh, g[f"ln1_{i}"] = _mm_nt_col_rms_bwd(dbcv, wts[f"a{i}_w_in"], h, wts[f"ln1_{i}"], dh, f"a{i}_in_bwd")
        return dh

    h, saved["a0"] = short_conv_fwd(h, 0)
    h, saved["f0"] = _ffn_fwd(h, wts, 0)

    h_in = h
    h, mixed = _pool_fwd(h, wts["ln1_1"], wts["b1_w_grp"], wts["b1_scale"], "b1_fwd")
    saved["b1"] = (h_in, mixed)
    h, saved["f1"] = _ffn_fwd(h, wts, 1)

    h_in = h
    u, ag = _mm_col(h, wts["ln1_2"], wts["c2_w_pw1"], wts["c2_b_pw1"], "c2_pw1")
    sw, hc = _conf_mid_fwd(ag, wts["c2_dw"], wts["c2_b_dw"], wts["c2_ln_g"], wts["c2_ln_b"], "c2_mid")
    h = _mm_row(sw, wts["c2_w_pw2"], h, wts["c2_b_pw2"], "c2_pw2")
    saved["c2"] = (h_in, u, ag, sw, hc)
    h, saved["f2"] = _ffn_fwd(h, wts, 2)

    h, saved["a3"] = short_conv_fwd(h, 3)
    h, saved["f3"] = _ffn_fwd(h, wts, 3)

    loss, dh, g["ln_f"] = _loss_head(h, wts["ln_f"], target, "loss_head")

    def ffn_bwd(dh, i):
        return _ffn_bwd(dh, saved[f"f{i}"], wts, i, g)

    dh = ffn_bwd(dh, 3)
    dh = short_conv_bwd(dh, saved["a3"], 3)

    dh = ffn_bwd(dh, 2)
    h_in, u, ag, sw, hc = saved["c2"]
    dhc, g["c2_ln_g"], g["c2_ln_b"], g["c2_b_pw2"] = _conf_out_bwd(
        dh, wts["c2_w_pw2"], hc, wts["c2_ln_g"], wts["c2_ln_b"], "c2_pw2_bwd")
    g["c2_w_pw2"] = _mm_tn(sw, dh, 1, "c2_dw_pw2")
    dag, ddw, g["c2_b_dw"], g["c2_b_pw1"] = _conf_mid_bwd(dhc, ag, wts["c2_dw"], "c2_mid_bwd")
    g["c2_dw"] = ddw[:CONF_CONV_W]
    g["c2_w_pw1"] = _mm_tn(u, dag, N_CHIPS, "c2_dw_pw1")
    dh, g["ln1_2"] = _mm_nt_col_rms_bwd(dag, wts["c2_w_pw1"], h_in, wts["ln1_2"], dh, "c2_pw1_bwd")

    dh = ffn_bwd(dh, 1)
    h_in, mixed = saved["b1"]
    dmixed, g["b1_w_grp"], g["b1_scale"] = _pool_bwd_mm(dh, mixed, wts["b1_w_grp"], wts["b1_scale"], "b1_bwd_mm")
    dh, g["ln1_1"] = _pool_bwd_rms(dmixed, h_in, wts["ln1_1"], dh, "b1_bwd_rms")

    dh = ffn_bwd(dh, 0)
    dh = short_conv_bwd(dh, saved["a0"], 0)
    return loss, dh, g


MESH = pl.DeviceIdType.MESH
ANY = pl.BlockSpec(memory_space=pl.ANY)


def _position():
    return lax.axis_index("x"), lax.axis_index("y"), lax.axis_index("c")


def _other_chips(x, y):
    return [(1 - x, y), (x, 1 - y), (1 - x, 1 - y)]


def _remote(src, dst, send_sem, recv_sem, to):
    return pltpu.make_async_remote_copy(src_ref=src, dst_ref=dst, send_sem=send_sem, recv_sem=recv_sem,
                                        device_id=to, device_id_type=MESH)


def _half_rows(ref_rows, c):
    hr = ref_rows // 2
    return pl.ds(pl.multiple_of(c * hr, 16), hr)


def _allgather8(v, name):
    m_per, n = v.shape

    def body(v_ref, out_ref, send_sems, recv_sems, local_sem):
        x, y, c = _position()
        me, sibling = (x, y, c), (x, y, 1 - c)
        chips = _other_chips(x, y)

        def rows(px, py, pc):
            return out_ref.at[pl.ds((4 * px + 2 * py + pc) * m_per, m_per), :]

        def copy(k, block, to, src=None):
            return _remote(rows(*block) if src is None else src, rows(*block), send_sems.at[k], recv_sems.at[k], to)

        mine = pltpu.make_async_copy(v_ref, rows(*me), local_sem)
        mine.start()
        first = [copy(0, me, sibling, src=v_ref)]
        first += [copy(1 + j, me, (*chip, c), src=v_ref) for j, chip in enumerate(chips)]
        for cp in first:
            cp.start()
        passed = [copy(4 + j, (*chip, c), sibling) for j, chip in enumerate(chips)]
        for j, chip in enumerate(chips):
            copy(1 + j, (*chip, c), me).wait_recv()
            passed[j].start()
        copy(0, sibling, me).wait_recv()
        for j, chip in enumerate(chips):
            copy(4 + j, (*chip, 1 - c), me).wait_recv()
        for cp in first + passed:
            cp.wait_send()
        mine.wait()

    return _hosted(
        body, name=name,
        out_shape=jax.ShapeDtypeStruct((N_DEV * m_per, n), v.dtype),
        in_specs=[pl.BlockSpec(memory_space=pltpu.VMEM)],
        out_specs=pl.BlockSpec(memory_space=pltpu.VMEM),
        scratch_shapes=[pltpu.SemaphoreType.DMA((7,)), pltpu.SemaphoreType.DMA((7,)), pltpu.SemaphoreType.DMA],
        compiler_params=pltpu.CompilerParams(vmem_limit_bytes=VMEM_LIMIT),
    )(v)


def _cast_to_slot(ws, idx, name):
    r, cols = ws[0].shape
    assert all(w.shape == (r, cols) for w in ws)
    n = len(ws)
    tr = _tile(r, 256, 16)

    def body(idx_ref, *refs):
        for w_ref, o_ref in zip(refs[:n], refs[n:]):
            o_ref[...] = w_ref[...].astype(o_ref.dtype)

    return _hosted(
        body, name=name,
        grid_spec=pltpu.PrefetchScalarGridSpec(
            num_scalar_prefetch=1, grid=(r // tr,),
            in_specs=[pl.BlockSpec((tr, cols), lambda t, idx_ref: (t, 0))] * n,
            out_specs=[pl.BlockSpec((None, tr, cols), lambda t, idx_ref: (idx_ref[0], t, 0))] * n),
        out_shape=[jax.ShapeDtypeStruct((N_CHIPS, r, cols), BF16)] * n,
        compiler_params=_params("parallel"),
    )(idx, *ws)


def _dma_sems(*shape):
    return [pltpu.SemaphoreType.DMA(shape), pltpu.SemaphoreType.DMA(shape)]


def _same_shapes(arrays):
    return [jax.ShapeDtypeStruct(a.shape, a.dtype) for a in arrays]


def _part_rows(ref_rows, c, part):
    hr = ref_rows // 2
    i, n = part
    size = hr // n
    assert size * n == hr and size % 16 == 0, (ref_rows, part)
    return pl.ds(pl.multiple_of(c * hr + i * size, 16), size)


def _task_gather_ici(bufs, done, part=(0, 1)):
    n = len(bufs)

    def copies(outs, sems, landing):
        x, y, c = _position()
        my_chip = 2 * x + y
        res = []
        for i in range(n):
            rows = _part_rows(bufs[i].shape[1], c, part)
            for r, (px, py) in enumerate(_other_chips(x, y)):
                slot = (2 * px + py) if landing else my_chip
                res.append(_remote(outs[i].at[my_chip, rows, :], outs[i].at[slot, rows, :], sems[0].at[i, r], sems[1].at[i, r],
                                   (px, py, c)))
        return res

    def start(ins, outs, sems):
        for cp in copies(outs, sems, False):
            cp.start()

    def wait(ins, outs, sems):
        for cp in copies(outs, sems, True):
            cp.wait_recv()
            cp.wait_send()

    return _Task(bufs, _same_shapes(bufs), {i: i for i in range(n)}, _dma_sems(n, 3), start, wait, done)


def _task_gather_d2d(bufs, done):
    n = len(bufs)

    def copies(outs, sems, landing):
        x, y, c = _position()
        res = []
        for i in range(n):
            rows = _half_rows(bufs[i].shape[1], (1 - c) if landing else c)
            for r, (px, py) in enumerate(_other_chips(x, y)):
                part = outs[i].at[2 * px + py, rows, :]
                res.append(_remote(part, part, sems[0].at[i, r], sems[1].at[i, r], (x, y, 1 - c)))
        return res

    def start(ins, outs, sems):
        for cp in copies(outs, sems, False):
            cp.start()

    def wait(ins, outs, sems):
        for cp in copies(outs, sems, True):
            cp.wait_recv()
        for cp in copies(outs, sems, False):
            cp.wait_send()

    return _Task(bufs, _same_shapes(bufs), {i: i for i in range(n)}, _dma_sems(n, 3), start, wait, done)


def _task_sibling_halves(grads, done):
    n = len(grads)

    def copies(ins, outs, sems):
        x, y, c = _position()
        return [_remote(ins[i].at[:, _half_rows(grads[i].shape[1], 1 - c), :], outs[i], sems[0].at[i], sems[1].at[i],
                        (x, y, 1 - c)) for i in range(n)]

    def start(ins, outs, sems):
        for cp in copies(ins, outs, sems):
            cp.start()

    def wait(ins, outs, sems):
        for cp in copies(ins, outs, sems):
            cp.wait()

    shapes = [jax.ShapeDtypeStruct((g.shape[0], g.shape[1] // 2, g.shape[2]), g.dtype) for g in grads]
    return _Task(grads, shapes, {}, _dma_sems(n), start, wait, done)


def _task_chip_sums(parts, done, landed=None, part=(0, 1)):
    n = len(parts)
    i_part, n_parts = part
    sizes = [p.shape[1] // n_parts for p in parts]
    assert all(p.shape[1] == size * n_parts and size % 16 == 0 for p, size in zip(parts, sizes)), part
    rows = [pl.ds(i_part * size, size) for size in sizes]

    def copies(ins, outs, sems):
        x, y, c = _position()
        return [_remote(ins[i].at[2 * px + py, rows[i], :], outs[i].at[r, rows[i], :], sems[0].at[i, r], sems[1].at[i, r],
                        (px, py, c))
                for i in range(n) for r, (px, py) in enumerate(_other_chips(x, y))]

    def start(ins, outs, sems):
        for cp in copies(ins, outs, sems):
            cp.start()

    def wait(ins, outs, sems):
        for cp in copies(ins, outs, sems):
            cp.wait()

    shapes = [jax.ShapeDtypeStruct((3,) + p.shape[1:], p.dtype) for p in parts]
    if landed is None:
        return _Task(parts, shapes, {}, _dma_sems(n, 3), start, wait, done)
    return _Task(list(parts) + list(landed), shapes, {n + i: i for i in range(n)}, _dma_sems(n, 3), start, wait, done)


def _task_sibling_parts(owns, landeds, done):
    n = len(owns)

    def copies(ins, outs, sems):
        x, y, c = _position()
        sibling = (x, y, 1 - c)
        res = []
        for i in range(n):
            res.append(_remote(ins[i].at[2 * x + y], outs[i].at[0], sems[0].at[i, 0], sems[1].at[i, 0], sibling))
            res.append(_remote(ins[n + i], outs[i].at[pl.ds(1, 3)], sems[0].at[i, 1], sems[1].at[i, 1], sibling))
        return res

    def start(ins, outs, sems):
        for cp in copies(ins, outs, sems):
            cp.start()

    def wait(ins, outs, sems):
        for cp in copies(ins, outs, sems):
            cp.wait()

    return _Task(list(owns) + list(landeds), _same_shapes(owns), {}, _dma_sems(n, 2), start, wait, done)


def _add_halves(grads, sibs, idx, name):
    n = len(grads)
    nsh = grads[0].shape[0]

    def body(idx_ref, *refs):
        for g_ref, s_ref, o_ref in zip(refs[:n], refs[n:2 * n], refs[2 * n:]):
            o_ref[...] = (g_ref[...].astype(F32) + s_ref[...].astype(F32)).astype(o_ref.dtype)

    half_of = [pl.BlockSpec((None, s.shape[1], s.shape[2]), lambda j, idx_ref: (j, idx_ref[1], 0)) for s in sibs]
    whole = [pl.BlockSpec((None, s.shape[1], s.shape[2]), lambda j, idx_ref: (j, 0, 0)) for s in sibs]
    return _hosted(
        body, name=name,
        grid_spec=pltpu.PrefetchScalarGridSpec(num_scalar_prefetch=1, grid=(nsh,), in_specs=half_of + whole, out_specs=whole),
        out_shape=_same_shapes(sibs),
        compiler_params=_params("parallel"),
    )(idx, *grads, *sibs)


def _adamw_reduced(w, own, landed, sib, m, v, idx, name):
    r, cols = w.shape
    hr = r // 2
    tr = _tile(hr, 256, 16)
    nt = hr // tr

    def body(idx_ref, w_ref, p_ref, l_ref, s_ref, m_ref, v_ref, go_ref, d_ref, mo_ref, vo_ref):
        mine = p_ref[...].astype(F32)
        for k in range(3):
            mine = mine + l_ref[k].astype(F32)
        theirs = s_ref[0].astype(F32)
        for k in range(1, 4):
            theirs = theirs + s_ref[k].astype(F32)
        grad = jnp.where(pl.program_id(0) // nt == idx_ref[1], mine, theirs)
        go_ref[...] = grad
        d_ref[...], mo_ref[...], vo_ref[...] = _adamw_update(w_ref[...], grad, m_ref[...], v_ref[...])

    def in_half(t, half):
        return jnp.clip(t - half * nt, 0, nt - 1)

    full = pl.BlockSpec((tr, cols), lambda t, idx_ref: (t, 0))
    return _hosted(
        body, name=name,
        grid_spec=pltpu.PrefetchScalarGridSpec(
            num_scalar_prefetch=1, grid=(2 * nt,),
            in_specs=[full,
                      pl.BlockSpec((None, tr, cols), lambda t, idx_ref: (idx_ref[0], in_half(t, idx_ref[1]), 0)),
                      pl.BlockSpec((3, tr, cols), lambda t, idx_ref: (0, in_half(t, idx_ref[1]), 0)),
                      pl.BlockSpec((4, tr, cols), lambda t, idx_ref: (0, in_half(t, 1 - idx_ref[1]), 0)),
                      full, full],
            out_specs=[full] * 4),
        out_shape=[jax.ShapeDtypeStruct((r, cols), F32)] * 4,
        compiler_params=_params("arbitrary"),
    )(idx, w, own, landed, sib, m, v)


def _sum_devices(blocks, name):
    m8, n = blocks.shape
    m = m8 // N_DEV

    def body(b_ref, o_ref):
        acc = b_ref[pl.ds(0, m), :]
        for k in range(1, N_DEV):
            acc = acc + b_ref[pl.ds(k * m, m), :]
        o_ref[...] = acc

    return _hosted(
        body, name=name, out_shape=jax.ShapeDtypeStruct((m, n), F32),
        in_specs=[pl.BlockSpec(memory_space=pltpu.VMEM)], out_specs=pl.BlockSpec(memory_space=pltpu.VMEM),
        compiler_params=pltpu.CompilerParams(vmem_limit_bytes=VMEM_LIMIT),
    )(blocks)


def _adamw_update(w, grad, m, v):
    new_m = ADAM_B1 * m + (1.0 - ADAM_B1) * grad
    new_v = ADAM_B2 * v + (1.0 - ADAM_B2) * (grad * grad)
    m_hat = new_m * (1.0 / (1.0 - ADAM_B1 ** ADAM_STEP))
    v_hat = new_v * (1.0 / (1.0 - ADAM_B2 ** ADAM_STEP))
    return -ADAM_LR * (m_hat / (jnp.sqrt(v_hat) + ADAM_EPS) + ADAM_WD * w), new_m, new_v


def _adamw_small(grad_blocks, params, name):
    nb, npar = len(grad_blocks), len(params)

    def body(*refs):
        blocks, ins, outs = refs[:nb], refs[nb:nb + 3 * npar], refs[nb + 3 * npar:]
        for p, (w, _, _, blk, row0) in enumerate(params):
            if w.ndim == 1:
                tiled = (w.shape[0] // LANES, LANES)
                grad = blocks[blk][pl.ds(row0, tiled[0]), pl.ds(0, LANES)]
                wmv = [ins[3 * p + k][...].reshape(tiled) for k in range(3)]
            else:
                grad = blocks[blk][pl.ds(row0, w.shape[0]), :]
                wmv = [ins[3 * p + k][...] for k in range(3)]
            for k, res in enumerate((grad,) + _adamw_update(wmv[0], grad, wmv[1], wmv[2])):
                outs[4 * p + k][...] = res.reshape(w.shape)

    args = list(grad_blocks) + [a for w, m, v, _, _ in params for a in (w, m, v)]
    vmem = pl.BlockSpec(memory_space=pltpu.VMEM)
    out = _hosted(
        body, name=name, in_specs=[vmem] * len(args), out_specs=[vmem] * (4 * npar),
        out_shape=[jax.ShapeDtypeStruct(w.shape, F32) for w, _, _, _, _ in params for _ in range(4)],
    )(*args)
    return [tuple(out[4 * p:4 * p + 4]) for p in range(npar)]


WEIGHT_NAMES = (
    "ln1_0", "a0_w_in", "a0_conv", "a0_w_out", "ln2_0", "ffn0_w_gu", "ffn0_w_down",
    "ln1_1", "b1_w_grp", "b1_scale", "ln2_1", "ffn1_w_gu", "ffn1_w_down",
    "ln1_2", "c2_w_pw1", "c2_b_pw1", "c2_dw", "c2_b_dw", "c2_ln_g", "c2_ln_b", "c2_w_pw2", "c2_b_pw2",
    "ln2_2", "ffn2_w_gu", "ffn2_w_down",
    "ln1_3", "a3_w_in", "a3_conv", "a3_w_out", "ln2_3", "ffn3_w_gu", "ffn3_w_down", "ln_f")
BIG = ("a0_w_in", "a0_w_out", "ffn0_w_gu", "ffn0_w_down", "b1_w_grp", "ffn1_w_gu", "ffn1_w_down", "c2_w_pw1", "c2_w_pw2",
       "ffn2_w_gu", "ffn2_w_down", "a3_w_in", "a3_w_out", "ffn3_w_gu", "ffn3_w_down")
GROUPED = "b1_w_grp"
SMALL_SHARDED = ("a0_conv", "a3_conv", "c2_dw")
REPLICATED = tuple(n for n in WEIGHT_NAMES if n not in BIG and n not in SMALL_SHARDED)


def _pad_rows(a, mult=8):
    pad = -a.shape[0] % mult
    return a if pad == 0 else jnp.concatenate([a, jnp.zeros((pad, a.shape[1]), a.dtype)], axis=0)


def _pack_rows(parts, width):
    rows = [p.reshape(-1, width) for p in parts]
    return _pad_rows(jnp.concatenate(rows, axis=0)), [r.shape[0] for r in rows]


def _unpack_rows(packed, counts, shapes):
    out, at = [], 0
    for n, shp in zip(counts, shapes):
        out.append(packed[at:at + n].reshape(shp))
        at += n
    return out


COLUMN_SHARDED = ("w_in", "w_gu", "w_pw1")


class _Weights(dict):
    def __init__(self, bufs):
        super().__init__()
        self.bufs = bufs

    def __missing__(self, name):
        buf = self.bufs[name]
        if name == GROUPED:
            cg = buf.shape[-1]
            rq = cg // N_CHIPS
            return jnp.transpose(buf.reshape(N_CHIPS, -1, rq, cg), (1, 0, 2, 3)).reshape(-1, cg, cg)
        return buf if name.endswith(COLUMN_SHARDED) else buf.reshape(-1, buf.shape[-1])


class _Exchange:
    def __init__(self, w, mom, vel, idx):
        def shards(table):
            return {n: table[n].reshape(-1, table[n].shape[-1]) for n in BIG}

        self.w, self.mom, self.vel, self.idx = shards(w), shards(mom), shards(vel), idx
        self.bufs = {}
        self.weights = _Weights(self.bufs)
        self.grads = {}
        self.sib, self.part, self.landed, self.sib_parts, self.updates = {}, {}, {}, {}, {}

    def cast(self, names):
        by_shape = {}
        for n in names:
            by_shape.setdefault(self.w[n].shape, []).append(n)
        for group in by_shape.values():
            self.bufs.update(zip(group, _cast_to_slot([self.w[n] for n in group], self.idx, f"cast_{group[0]}")))

    @staticmethod
    def _store(table, names):
        def done(arrays):
            table.update(zip(names, arrays))
        return done

    def _grad(self, n):
        g = self.grads[n]
        if n == GROUPED:
            ng, cg, _ = g.shape
            g = jnp.transpose(g.reshape(ng, N_CHIPS, cg // N_CHIPS, cg), (1, 0, 2, 3)).astype(BF16)
        return g.reshape(N_CHIPS, -1, g.shape[-1])

    def gather_ici(self, *names, part=(0, 1)):
        return lambda: _task_gather_ici([self.bufs[n] for n in names], self._store(self.bufs, names), part)

    def gather_d2d(self, *names):
        return lambda: _task_gather_d2d([self.bufs[n] for n in names], self._store(self.bufs, names))

    def sibling_halves(self, *names):
        return lambda: _task_sibling_halves([self._grad(n) for n in names], self._store(self.sib, names))

    def add_halves(self, *names):
        def run():
            parts = _add_halves([self._grad(n) for n in names], [self.sib.pop(n) for n in names], self.idx,
                                f"reduce_add_{names[0]}")
            self.part.update(zip(names, parts))
        return run

    def chip_sums(self, *names, part=(0, 1)):
        def make():
            landed = [self.landed[n] for n in names] if part[0] > 0 else None
            return _task_chip_sums([self.part[n] for n in names], self._store(self.landed, names), landed, part)
        return make

    def sibling_parts(self, *names):
        return lambda: _task_sibling_parts([self.part[n] for n in names], [self.landed[n] for n in names],
                                           self._store(self.sib_parts, names))

    def adamw(self, *names):
        def run():
            for n in names:
                self.updates[n] = _adamw_reduced(self.w[n], self.part.pop(n), self.landed.pop(n), self.sib_parts.pop(n),
                                                 self.mom[n], self.vel[n], self.idx, f"adamw_{n}")
        return run


def _plan(ex):
    s = _Schedule()

    def ffn(i):
        return f"ffn{i}_w_gu", f"ffn{i}_w_down"

    c2, a3 = ("c2_w_pw1", "c2_w_pw2"), ("a3_w_in", "a3_w_out")
    first, second = (0, 2), (1, 2)
    s.host("cast_ffn0_w_gu", ex.gather_ici("a0_w_in", part=first))
    s.host("cast_ffn0_w_down", ex.gather_ici("a0_w_in", part=second))
    s.host("cast_c2_w_pw1", ex.gather_d2d("a0_w_in"))
    gu, down = ffn(0)
    s.host("gather_small", ex.gather_ici("a0_w_out"))
    s.host("a0_in", ex.gather_ici(gu, part=first), ex.gather_d2d("a0_w_out"))
    s.host("a0_conv", ex.gather_ici(gu, part=second))
    s.host("a0_out", ex.gather_ici(down), ex.gather_d2d(gu))
    s.host("ffn0_up", ex.gather_d2d(down))
    gu, down = ffn(1)
    s.host("ffn0_up", ex.gather_ici(gu, GROUPED))
    s.host("ffn0_down", ex.gather_ici(down), ex.gather_d2d(gu, GROUPED))
    s.host("ffn1_up", ex.gather_d2d(down), ex.gather_ici(*c2))
    gu, down = ffn(2)
    s.host("ffn1_up", ex.gather_ici(gu, part=first))
    s.host("ffn1_down", ex.gather_d2d(*c2), ex.gather_ici(down))
    s.host("c2_mid", ex.gather_ici(gu, part=second))
    s.host("c2_pw2", ex.gather_d2d(gu, down), ex.gather_ici(a3[1]))
    s.host("ffn2_up", ex.gather_ici(a3[0]))
    gu, down = ffn(3)
    s.host("ffn2_up", ex.gather_ici(gu, part=first))
    s.host("ffn2_down", ex.gather_d2d(*a3), ex.gather_ici(down))
    s.host("a3_in", ex.gather_ici(gu, part=second))
    s.host("a3_out", ex.gather_d2d(gu, down))

    def reduce_on(names, first_host, ici_hosts, last_host):
        s.host(first_host, ex.sibling_halves(*names))
        s.post(first_host, ex.add_halves(*names))
        for host, hosted, part in ici_hosts:
            s.host(host, ex.chip_sums(*hosted, part=part))
        s.host(last_host, ex.sibling_parts(*names))
        s.post(last_host, ex.adamw(*names))

    whole = (0, 1)
    gu, down = ffn(3)
    reduce_on((gu, down), "a3_out_bwd",
              [("a3_conv_bwd", (down,), whole), ("a3_dw_in", (gu,), first), ("a3_in_bwd", (gu,), second)], "ffn2_down_bwd")
    reduce_on(a3, "ffn2_down_bwd", [("ffn2_dw_gu", a3, whole)], "c2_pw2_bwd")
    gu, down = ffn(0)
    s.host("ffn0_dw_gu", ex.sibling_halves(down))
    s.post("ffn0_dw_gu", ex.add_halves(down))
    s.host("ffn0_up_bwd", ex.chip_sums(down))
    s.host("a0_out_bwd", ex.sibling_halves(gu, GROUPED))
    s.post("a0_out_bwd", ex.add_halves(gu, GROUPED))
    s.host("a0_conv_bwd", ex.chip_sums(gu, part=first), ex.chip_sums(GROUPED))
    s.host("a0_dw_in", ex.chip_sums(gu, part=second))
    s.host("a0_dw_out", ex.sibling_halves("a0_w_in"))
    s.post("a0_dw_out", ex.add_halves("a0_w_in"))
    s.host("a0_in_bwd", ex.chip_sums("a0_w_in"), ex.sibling_parts(gu, down, GROUPED))
    s.host("gather_small_grads", ex.chip_sums("a0_w_out"))
    s.host("sum_small_grads", ex.sibling_parts("a0_w_out"))
    reduce_on(ffn(2), "c2_pw2_bwd", [("c2_mid_bwd", ffn(2), whole)], "ffn1_down_bwd")
    reduce_on(c2, "ffn1_down_bwd", [("ffn1_dw_down", c2, whole)], "b1_bwd_mm")
    gu, down = ffn(1)
    reduce_on((gu, down), "b1_bwd_mm", [("ffn0_down_bwd", (down,), whole), ("ffn0_dw_gu", (gu,), whole)], "ffn0_up_bwd")
    return s


def kernel(x, *rest):
    nw = len(WEIGHT_NAMES)
    w = dict(zip(WEIGHT_NAMES, rest[:nw]))
    target = rest[nw]
    mom = dict(zip(WEIGHT_NAMES, rest[nw + 1:2 * nw + 1]))
    vel = dict(zip(WEIGHT_NAMES, rest[2 * nw + 1:3 * nw + 1]))
    cx, cy, cc = _position()
    my_chip = 2 * cx + cy
    ex = _Exchange(w, mom, vel, jnp.stack([my_chip, cc]).astype(jnp.int32))
    _ACTIVE_SCHEDULE[0] = _plan(ex)
    try:
        return _scheduled_step(x, target, w, mom, vel, ex, my_chip)
    finally:
        _ACTIVE_SCHEDULE[0] = None


def _scheduled_step(x, target, w, mom, vel, ex, my_chip):
    d = x.shape[-1]
    cq = d // N_CHIPS
    ex.cast(BIG)

    small_blk, small_counts = _pack_rows([w[n] for n in SMALL_SHARDED], cq)
    small_all = _allgather8(small_blk, "gather_small").reshape(N_CHIPS, 2, small_blk.shape[0], cq)[:, 0]
    small_parts = _unpack_rows(jnp.transpose(small_all, (1, 0, 2)), small_counts,
                               [(w[n].reshape(-1, cq).shape[0], N_CHIPS, cq) for n in SMALL_SHARDED])
    wts = ex.weights
    for n in REPLICATED:
        wts[n] = w[n].reshape(1, -1)
    for n, part in zip(SMALL_SHARDED, small_parts):
        wts[n] = part.reshape(part.shape[0], d)

    loss, dx, g = _device_step(x[0], target[0], wts, ex.grads)

    summed, last = ("ffn0_w_gu", "ffn0_w_down", GROUPED, "a0_w_in"), "a0_w_out"
    _comm_only([ex.sibling_parts("a0_w_in")(), ex.sibling_halves(last)()], "reduce_tail_d2d")
    ex.add_halves(last)()

    rep_rows = [jnp.pad(g[n].reshape(-1, LANES), ((0, 0), (0, cq - LANES))) for n in REPLICATED]
    by_chip = [jnp.transpose(g[n].reshape(g[n].shape[0], N_CHIPS, cq), (1, 0, 2)) for n in SMALL_SHARDED]
    shard_rows = jnp.concatenate(by_chip, axis=1)
    n_rep, n_shard = sum(r.shape[0] for r in rep_rows), shard_rows.shape[1]
    loss_row = jnp.broadcast_to(loss, (1, cq))
    sm_blk = _pad_rows(jnp.concatenate(rep_rows + [loss_row, shard_rows.reshape(N_CHIPS * n_shard, cq)], axis=0))
    sm_sum = _sum_devices(_allgather8(sm_blk, "gather_small_grads"), "sum_small_grads")
    mine = lax.dynamic_slice_in_dim(sm_sum, n_rep + 1 + my_chip * n_shard, n_shard, axis=0)
    ex.adamw(*summed)()
    ex.adamw(last)()
    sched = _ACTIVE_SCHEDULE[0]
    assert not sched.hosts and not sched.posts, (sched.hosts, sched.posts)

    out = ex.updates
    params, at = [], {0: 0, 1: 0}
    for block, names in ((0, REPLICATED), (1, SMALL_SHARDED)):
        for n in names:
            params.append((w[n], mom[n], vel[n], block, at[block]))
            at[block] += w[n].size // LANES if w[n].ndim == 1 else w[n].shape[0]
    out.update(zip(REPLICATED + SMALL_SHARDED, _adamw_small([sm_sum, mine], params, "adamw_small")))

    total = sm_sum[n_rep, 0]
    grads, deltas, new_m, new_v = ([out[n][k].reshape(w[n].shape) for n in WEIGHT_NAMES] for k in range(4))
    return (total, dx.reshape(x.shape), *grads, *deltas, *new_m, *new_v)
```
